```python
import math
import jax, jax.numpy as jnp
from jax import lax
import numpy as np

D_MODEL = 1024
BATCH = 32
SEQ = 2048
DEPTH = 1

RNN_WIDTH = D_MODEL * 5 // 4
RNN_BLOCKS = 10
RNN_BLOCK = RNN_WIDTH // RNN_BLOCKS
RNN_CONV = 4
LRU_C = 8.0
HEAD_DIM = 128
KV_HEADS = 4
DILATED_CONFIGS = ((128, 1), (512, 4), (2048, 16))
N_GROUPS = len(DILATED_CONFIGS)
Q_HEADS = N_GROUPS * KV_HEADS
ATTN_BLOCK = 128
REL_BUCKETS = 32
REL_MAX_DIST = 2048
FFN_WIDTH = 3 * D_MODEL
FFN_CONV = 3
EPS = 1e-6

IN_SPLITS = (RNN_WIDTH, Q_HEADS * HEAD_DIM, KV_HEADS * HEAD_DIM, KV_HEADS * HEAD_DIM, D_MODEL, D_MODEL)
IN_WIDTH = sum(IN_SPLITS)

kernel_name = "hybrid_rglru_dilated_attn_convffn"


def rms_norm(x, g):
    xf = x.astype(jnp.float32)
    y = xf * lax.rsqrt(jnp.mean(xf * xf, axis=-1, keepdims=True) + EPS)
    return (y * g.astype(jnp.float32)).astype(x.dtype)


def causal_dwconv(x, w, b):
    k = w.shape[0]
    y = lax.conv_general_dilated(x, w[:, None, :].astype(x.dtype), window_strides=(1,),
                                 padding=[(k - 1, 0)], dimension_numbers=("NWC", "WIO", "NWC"),
                                 feature_group_count=x.shape[-1])
    return y + b.astype(x.dtype)


def _t5_bucket(dist):
    max_exact = REL_BUCKETS // 2
    d = np.maximum(dist, 1).astype(np.float32)
    large = max_exact + np.log(d / max_exact) / math.log(REL_MAX_DIST / max_exact) * (REL_BUCKETS - max_exact)
    large = np.minimum(large.astype(np.int32), REL_BUCKETS - 1)
    return np.where(dist < max_exact, dist, large).astype(np.int32)


def _band_structure(n_blocks, dilation, n_back):
    qi = np.arange(ATTN_BLOCK)[None, :, None]
    kj = np.arange(2 * ATTN_BLOCK)[None, None, :]
    nb = np.arange(n_blocks)[:, None, None]
    delta = ATTN_BLOCK + qi - kj
    mask = (delta >= 0) & (delta <= n_back) & ((nb - 1) * ATTN_BLOCK + kj >= 0)
    bucket = _t5_bucket(np.maximum(delta[0], 0) * dilation)
    return mask, bucket


def dilated_group(q, k, v, bias_g, window, dilation):
    B, H, S, hd = q.shape
    r = dilation
    n_back = window // dilation
    M = S // r
    nb = -(-M // ATTN_BLOCK)
    Mp = nb * ATTN_BLOCK

    def to_sub(t):
        return t.reshape(B, H, M, r, hd).transpose(0, 1, 3, 2, 4)

    qs = jnp.pad(to_sub(q), ((0, 0), (0, 0), (0, 0), (0, Mp - M), (0, 0)))
    qs = qs.reshape(B, H, r, nb, ATTN_BLOCK, hd)

    def key_blocks(t):
        ts = jnp.pad(to_sub(t), ((0, 0), (0, 0), (0, 0), (ATTN_BLOCK, Mp - M), (0, 0)))
        ts = ts.reshape(B, H, r, nb + 1, ATTN_BLOCK, hd)
        return jnp.concatenate([ts[:, :, :, :-1], ts[:, :, :, 1:]], axis=4)

    kb = key_blocks(k)
    vb = key_blocks(v)
    mask, bucket = _band_structure(nb, r, n_back)
    bias = bias_g[jnp.asarray(bucket)].astype(jnp.float32).transpose(2, 0, 1)

    logits = jnp.einsum("bhrnqd,bhrnkd->bhrnqk", qs, kb).astype(jnp.float32) * (HEAD_DIM ** -0.5)
    logits = logits + bias[None, :, None, None]
    logits = jnp.where(jnp.asarray(mask)[None, None, None], logits, -jnp.inf)
    mx = jnp.max(logits, axis=-1, keepdims=True)
    p = jnp.exp(logits - mx)
    den = jnp.sum(p, axis=-1, keepdims=True)
    o = jnp.einsum("bhrnqk,bhrnkd->bhrnqd", p, vb.astype(jnp.float32)) / den
    lse = (mx + jnp.log(den))[..., 0]

    o = o.reshape(B, H, r, Mp, hd)[:, :, :, :M].transpose(0, 1, 3, 2, 4).reshape(B, H, S, hd)
    lse = lse.reshape(B, H, r, Mp)[:, :, :, :M].transpose(0, 1, 3, 2).reshape(B, H, S)
    return o, lse


def rg_lru(xc, w_a, b_a, w_x, b_x, lam):
    B, S, C = xc.shape
    xb = xc.reshape(B, S, RNN_BLOCKS, RNN_BLOCK)
    r = jax.nn.sigmoid(jnp.einsum("bsnc,ncd->bsnd", xb, w_a).reshape(B, S, C).astype(jnp.float32)
                       + b_a.astype(jnp.float32))
    i = jax.nn.sigmoid(jnp.einsum("bsnc,ncd->bsnd", xb, w_x).reshape(B, S, C).astype(jnp.float32)
                       + b_x.astype(jnp.float32))
    log_a = -LRU_C * r * jax.nn.softplus(-lam.astype(jnp.float32))
    a = jnp.exp(log_a)
    u = jnp.sqrt(-jnp.expm1(2.0 * log_a)) * (i * xc.astype(jnp.float32))

    def step(h, inp):
        a_t, u_t = inp
        h = a_t * h + u_t
        return h, h

    _, hs = lax.scan(step, jnp.zeros((B, C), jnp.float32), (a.transpose(1, 0, 2), u.transpose(1, 0, 2)))
    return hs.transpose(1, 0, 2)


def _fwd_setup_inputs(seed: int = 0) -> dict:
    key = jax.random.key(seed)
    ks = jax.random.split(key, 24)
    f32 = jnp.float32
    L, D = DEPTH, D_MODEL

    def nrm(k, shape, scale):
        return jax.random.normal(k, shape, f32) * scale

    u = jax.random.uniform(ks[10], (L, RNN_WIDTH), f32, minval=0.9, maxval=0.999)
    p = u ** (1.0 / LRU_C)
    lru_lambda = jnp.log(p) - jnp.log1p(-p)
    return {
        "x": nrm(ks[0], (BATCH, SEQ, D), 1.0),
        "rel_bias": nrm(ks[1], (REL_BUCKETS, Q_HEADS), 0.1),
        "norm_mix_pre": 1.0 + nrm(ks[2], (L, D), 0.05),
        "norm_mix_post": 1.0 + nrm(ks[3], (L, D), 0.05),
        "w_in": nrm(ks[4], (L, D, IN_WIDTH), D ** -0.5),
        "conv_rnn_w": nrm(ks[5], (L, RNN_CONV, RNN_WIDTH), RNN_CONV ** -0.5),
        "conv_rnn_b": nrm(ks[6], (L, RNN_WIDTH), 0.01),
        "w_rg_a": nrm(ks[7], (L, RNN_BLOCKS, RNN_BLOCK, RNN_BLOCK), RNN_BLOCK ** -0.5),
        "b_rg_a": nrm(ks[8], (L, RNN_WIDTH), 0.01),
        "w_rg_x": nrm(ks[9], (L, RNN_BLOCKS, RNN_BLOCK, RNN_BLOCK), RNN_BLOCK ** -0.5),
        "b_rg_x": nrm(ks[11], (L, RNN_WIDTH), 0.01),
        "lru_lambda": lru_lambda,
        "w_branch_rnn": nrm(ks[12], (L, RNN_WIDTH, D), RNN_WIDTH ** -0.5),
        "w_branch_att": nrm(ks[13], (L, KV_HEADS * HEAD_DIM, D), (KV_HEADS * HEAD_DIM) ** -0.5),
        "w_out": nrm(ks[14], (L, D, D), D ** -0.5),
        "norm_ffn_pre": 1.0 + nrm(ks[15], (L, D), 0.05),
        "norm_ffn_post": 1.0 + nrm(ks[16], (L, D), 0.05),
        "w_ffn_gate": nrm(ks[17], (L, D, FFN_WIDTH), D ** -0.5),
        "w_ffn_up": nrm(ks[18], (L, D, FFN_WIDTH), D ** -0.5),
        "conv_ffn_w": nrm(ks[19], (L, FFN_CONV, FFN_WIDTH), FFN_CONV ** -0.5),
        "conv_ffn_b": nrm(ks[20], (L, FFN_WIDTH), 0.01),
        "w_ffn_down": nrm(ks[21], (L, FFN_WIDTH, D), FFN_WIDTH ** -0.5),
    }


def _fwd_reference(x, rel_bias, norm_mix_pre, norm_mix_post, w_in, conv_rnn_w, conv_rnn_b, w_rg_a, b_rg_a,
              w_rg_x, b_rg_x, lru_lambda, w_branch_rnn, w_branch_att, w_out, norm_ffn_pre, norm_ffn_post,
              w_ffn_gate, w_ffn_up, conv_ffn_w, conv_ffn_b, w_ffn_down):
    B, S, D = x.shape
    split_idx = [int(s) for s in np.cumsum(IN_SPLITS)[:-1]]
    h = x
    for l in range(DEPTH):
        hn = rms_norm(h, norm_mix_pre[l])
        proj = hn @ w_in[l]
        xr, q, k, v, g_rnn, g_att = jnp.split(proj, split_idx, axis=-1)

        xc = causal_dwconv(xr, conv_rnn_w[l], conv_rnn_b[l])
        y_rnn = rg_lru(xc, w_rg_a[l], b_rg_a[l], w_rg_x[l], b_rg_x[l], lru_lambda[l]).astype(x.dtype)

        qg = q.reshape(B, S, N_GROUPS, KV_HEADS, HEAD_DIM).transpose(2, 0, 3, 1, 4)
        kh = k.reshape(B, S, KV_HEADS, HEAD_DIM).transpose(0, 2, 1, 3)
        vh = v.reshape(B, S, KV_HEADS, HEAD_DIM).transpose(0, 2, 1, 3)
        outs, lses = [], []
        for g, (window, dilation) in enumerate(DILATED_CONFIGS):
            o_g, lse_g = dilated_group(qg[g], kh, vh, rel_bias[:, g * KV_HEADS:(g + 1) * KV_HEADS],
                                       window, dilation)
            outs.append(o_g)
            lses.append(lse_g)
        alpha = jax.nn.softmax(jnp.stack(lses, axis=0), axis=0)
        o_att = jnp.sum(alpha[..., None] * jnp.stack(outs, axis=0), axis=0)
        o_att = o_att.transpose(0, 2, 1, 3).reshape(B, S, KV_HEADS * HEAD_DIM).astype(x.dtype)

        merged = (jax.nn.sigmoid(g_rnn) * (y_rnn @ w_branch_rnn[l])
                  + jax.nn.sigmoid(g_att) * (o_att @ w_branch_att[l]))
        mix = merged @ w_out[l]
        h = h + rms_norm(mix, norm_mix_post[l])

        hn = rms_norm(h, norm_ffn_pre[l])
        gate = causal_dwconv(hn @ w_ffn_gate[l], conv_ffn_w[l], conv_ffn_b[l])
        ff = (jax.nn.gelu(gate, approximate=True) * (hn @ w_ffn_up[l])) @ w_ffn_down[l]
        h = h + rms_norm(ff, norm_ffn_post[l])
    return h


import jax as _jax
import jax.numpy as _jnp

TWIN_FORMAT = 'train_step'
FWD_PARAMS = ['x', 'rel_bias', 'norm_mix_pre', 'norm_mix_post', 'w_in', 'conv_rnn_w', 'conv_rnn_b', 'w_rg_a', 'b_rg_a', 'w_rg_x', 'b_rg_x', 'lru_lambda', 'w_branch_rnn', 'w_branch_att', 'w_out', 'norm_ffn_pre', 'norm_ffn_post', 'w_ffn_gate', 'w_ffn_up', 'conv_ffn_w', 'conv_ffn_b', 'w_ffn_down']
TWIN_WEIGHTS = ['rel_bias', 'norm_mix_pre', 'norm_mix_post', 'w_in', 'conv_rnn_w', 'conv_rnn_b', 'w_rg_a', 'b_rg_a', 'w_rg_x', 'b_rg_x', 'lru_lambda', 'w_branch_rnn', 'w_branch_att', 'w_out', 'norm_ffn_pre', 'norm_ffn_post', 'w_ffn_gate', 'w_ffn_up', 'conv_ffn_w', 'conv_ffn_b', 'w_ffn_down']
TWIN_DIFF_INPUT = 'x'
TWIN_INPUTS = ['x', 'rel_bias', 'norm_mix_pre', 'norm_mix_post', 'w_in', 'conv_rnn_w', 'conv_rnn_b', 'w_rg_a', 'b_rg_a', 'w_rg_x', 'b_rg_x', 'lru_lambda', 'w_branch_rnn', 'w_branch_att', 'w_out', 'norm_ffn_pre', 'norm_ffn_post', 'w_ffn_gate', 'w_ffn_up', 'conv_ffn_w', 'conv_ffn_b', 'w_ffn_down', 'loss_target', 'm_rel_bias', 'm_norm_mix_pre', 'm_norm_mix_post', 'm_w_in', 'm_conv_rnn_w', 'm_conv_rnn_b', 'm_w_rg_a', 'm_b_rg_a', 'm_w_rg_x', 'm_b_rg_x', 'm_lru_lambda', 'm_w_branch_rnn', 'm_w_branch_att', 'm_w_out', 'm_norm_ffn_pre', 'm_norm_ffn_post', 'm_w_ffn_gate', 'm_w_ffn_up', 'm_conv_ffn_w', 'm_conv_ffn_b', 'm_w_ffn_down', 'v_rel_bias', 'v_norm_mix_pre', 'v_norm_mix_post', 'v_w_in', 'v_conv_rnn_w', 'v_conv_rnn_b', 'v_w_rg_a', 'v_b_rg_a', 'v_w_rg_x', 'v_b_rg_x', 'v_lru_lambda', 'v_w_branch_rnn', 'v_w_branch_att', 'v_w_out', 'v_norm_ffn_pre', 'v_norm_ffn_post', 'v_w_ffn_gate', 'v_w_ffn_up', 'v_conv_ffn_w', 'v_conv_ffn_b', 'v_w_ffn_down']
TWIN_OUTPUTS = ['loss', 'grad_x', 'grad_rel_bias', 'grad_norm_mix_pre', 'grad_norm_mix_post', 'grad_w_in', 'grad_conv_rnn_w', 'grad_conv_rnn_b', 'grad_w_rg_a', 'grad_b_rg_a', 'grad_w_rg_x', 'grad_b_rg_x', 'grad_lru_lambda', 'grad_w_branch_rnn', 'grad_w_branch_att', 'grad_w_out', 'grad_norm_ffn_pre', 'grad_norm_ffn_post', 'grad_w_ffn_gate', 'grad_w_ffn_up', 'grad_conv_ffn_w', 'grad_conv_ffn_b', 'grad_w_ffn_down', 'delta_rel_bias', 'delta_norm_mix_pre', 'delta_norm_mix_post', 'delta_w_in', 'delta_conv_rnn_w', 'delta_conv_rnn_b', 'delta_w_rg_a', 'delta_b_rg_a', 'delta_w_rg_x', 'delta_b_rg_x', 'delta_lru_lambda', 'delta_w_branch_rnn', 'delta_w_branch_att', 'delta_w_out', 'delta_norm_ffn_pre', 'delta_norm_ffn_post', 'delta_w_ffn_gate', 'delta_w_ffn_up', 'delta_conv_ffn_w', 'delta_conv_ffn_b', 'delta_w_ffn_down', 'new_m_rel_bias', 'new_m_norm_mix_pre', 'new_m_norm_mix_post', 'new_m_w_in', 'new_m_conv_rnn_w', 'new_m_conv_rnn_b', 'new_m_w_rg_a', 'new_m_b_rg_a', 'new_m_w_rg_x', 'new_m_b_rg_x', 'new_m_lru_lambda', 'new_m_w_branch_rnn', 'new_m_w_branch_att', 'new_m_w_out', 'new_m_norm_ffn_pre', 'new_m_norm_ffn_post', 'new_m_w_ffn_gate', 'new_m_w_ffn_up', 'new_m_conv_ffn_w', 'new_m_conv_ffn_b', 'new_m_w_ffn_down', 'new_v_rel_bias', 'new_v_norm_mix_pre', 'new_v_norm_mix_post', 'new_v_w_in', 'new_v_conv_rnn_w', 'new_v_conv_rnn_b', 'new_v_w_rg_a', 'new_v_b_rg_a', 'new_v_w_rg_x', 'new_v_b_rg_x', 'new_v_lru_lambda', 'new_v_w_branch_rnn', 'new_v_w_branch_att', 'new_v_w_out', 'new_v_norm_ffn_pre', 'new_v_norm_ffn_post', 'new_v_w_ffn_gate', 'new_v_w_ffn_up', 'new_v_conv_ffn_w', 'new_v_conv_ffn_b', 'new_v_w_ffn_down']
TWIN_LEAF_KINDS = {'loss': 'loss', 'grad_x': 'grad_x', 'grad_rel_bias': 'grad_w', 'grad_norm_mix_pre': 'grad_w', 'grad_norm_mix_post': 'grad_w', 'grad_w_in': 'grad_w', 'grad_conv_rnn_w': 'grad_w', 'grad_conv_rnn_b': 'grad_w', 'grad_w_rg_a': 'grad_w', 'grad_b_rg_a': 'grad_w', 'grad_w_rg_x': 'grad_w', 'grad_b_rg_x': 'grad_w', 'grad_lru_lambda': 'grad_w', 'grad_w_branch_rnn': 'grad_w', 'grad_w_branch_att': 'grad_w', 'grad_w_out': 'grad_w', 'grad_norm_ffn_pre': 'grad_w', 'grad_norm_ffn_post': 'grad_w', 'grad_w_ffn_gate': 'grad_w', 'grad_w_ffn_up': 'grad_w', 'grad_conv_ffn_w': 'grad_w', 'grad_conv_ffn_b': 'grad_w', 'grad_w_ffn_down': 'grad_w', 'delta_rel_bias': 'delta_w', 'delta_norm_mix_pre': 'delta_w', 'delta_norm_mix_post': 'delta_w', 'delta_w_in': 'delta_w', 'delta_conv_rnn_w': 'delta_w', 'delta_conv_rnn_b': 'delta_w', 'delta_w_rg_a': 'delta_w', 'delta_b_rg_a': 'delta_w', 'delta_w_rg_x': 'delta_w', 'delta_b_rg_x': 'delta_w', 'delta_lru_lambda': 'delta_w', 'delta_w_branch_rnn': 'delta_w', 'delta_w_branch_att': 'delta_w', 'delta_w_out': 'delta_w', 'delta_norm_ffn_pre': 'delta_w', 'delta_norm_ffn_post': 'delta_w', 'delta_w_ffn_gate': 'delta_w', 'delta_w_ffn_up': 'delta_w', 'delta_conv_ffn_w': 'delta_w', 'delta_conv_ffn_b': 'delta_w', 'delta_w_ffn_down': 'delta_w', 'new_m_rel_bias': 'new_m', 'new_m_norm_mix_pre': 'new_m', 'new_m_norm_mix_post': 'new_m', 'new_m_w_in': 'new_m', 'new_m_conv_rnn_w': 'new_m', 'new_m_conv_rnn_b': 'new_m', 'new_m_w_rg_a': 'new_m', 'new_m_b_rg_a': 'new_m', 'new_m_w_rg_x': 'new_m', 'new_m_b_rg_x': 'new_m', 'new_m_lru_lambda': 'new_m', 'new_m_w_branch_rnn': 'new_m', 'new_m_w_branch_att': 'new_m', 'new_m_w_out': 'new_m', 'new_m_norm_ffn_pre': 'new_m', 'new_m_norm_ffn_post': 'new_m', 'new_m_w_ffn_gate': 'new_m', 'new_m_w_ffn_up': 'new_m', 'new_m_conv_ffn_w': 'new_m', 'new_m_conv_ffn_b': 'new_m', 'new_m_w_ffn_down': 'new_m', 'new_v_rel_bias': 'new_v', 'new_v_norm_mix_pre': 'new_v', 'new_v_norm_mix_post': 'new_v', 'new_v_w_in': 'new_v', 'new_v_conv_rnn_w': 'new_v', 'new_v_conv_rnn_b': 'new_v', 'new_v_w_rg_a': 'new_v', 'new_v_b_rg_a': 'new_v', 'new_v_w_rg_x': 'new_v', 'new_v_b_rg_x': 'new_v', 'new_v_lru_lambda': 'new_v', 'new_v_w_branch_rnn': 'new_v', 'new_v_w_branch_att': 'new_v', 'new_v_w_out': 'new_v', 'new_v_norm_ffn_pre': 'new_v', 'new_v_norm_ffn_post': 'new_v', 'new_v_w_ffn_gate': 'new_v', 'new_v_w_ffn_up': 'new_v', 'new_v_conv_ffn_w': 'new_v', 'new_v_conv_ffn_b': 'new_v', 'new_v_w_ffn_down': 'new_v'}


def _forward(args):
    return _fwd_reference(*[args[k] for k in FWD_PARAMS])


def _output_shape():
    out = _jax.eval_shape(lambda: _forward(_fwd_setup_inputs(0)))
    return out.shape, out.dtype

N_MICROBATCH = 1
ADAM_LR = 0.001
ADAM_B1 = 0.9
ADAM_B2 = 0.999
ADAM_EPS = 1e-08
ADAM_WD = 0.01
ADAM_STEP = 10
PER_EXAMPLE_BATCH_AXIS = {'x': 0, 'loss_target': 0}
SHARED_INPUTS = []
_WEIGHT_DTYPES = {'rel_bias': _jnp.float32, 'norm_mix_pre': _jnp.float32, 'norm_mix_post': _jnp.float32, 'w_in': _jnp.float32, 'conv_rnn_w': _jnp.float32, 'conv_rnn_b': _jnp.float32, 'w_rg_a': _jnp.float32, 'b_rg_a': _jnp.float32, 'w_rg_x': _jnp.float32, 'b_rg_x': _jnp.float32, 'lru_lambda': _jnp.float32, 'w_branch_rnn': _jnp.float32, 'w_branch_att': _jnp.float32, 'w_out': _jnp.float32, 'norm_ffn_pre': _jnp.float32, 'norm_ffn_post': _jnp.float32, 'w_ffn_gate': _jnp.float32, 'w_ffn_up': _jnp.float32, 'conv_ffn_w': _jnp.float32, 'conv_ffn_b': _jnp.float32, 'w_ffn_down': _jnp.float32}
MOMENT_SCALE = {'rel_bias': 3.101233e-01, 'norm_mix_pre': 1.748940e+00, 'norm_mix_post': 6.675698e+01, 'w_in': 6.842802e-01, 'conv_rnn_w': 2.768603e+00, 'conv_rnn_b': 8.613823e+01, 'w_rg_a': 2.679146e+00, 'b_rg_a': 1.383686e+00, 'w_rg_x': 4.774530e+00, 'b_rg_x': 7.905959e-01, 'lru_lambda': 2.108050e+00, 'w_branch_rnn': 4.593310e+00, 'w_branch_att': 3.200269e-01, 'w_out': 4.314244e+00, 'norm_ffn_pre': 3.197295e+00, 'norm_ffn_post': 6.438177e+01, 'w_ffn_gate': 7.370587e-01, 'w_ffn_up': 1.769639e+00, 'conv_ffn_w': 1.485387e+00, 'conv_ffn_b': 2.842220e+00, 'w_ffn_down': 3.185856e+00}


def _to_microbatches(a, axis):
    t = _jnp.moveaxis(a, axis, 0)
    t = t.reshape((N_MICROBATCH, t.shape[0] // N_MICROBATCH) + t.shape[1:])
    return _jnp.moveaxis(t, 1, axis + 1)


def setup_inputs(seed: int = 0) -> dict:
    inp = _fwd_setup_inputs(seed)
    key = _jax.random.fold_in(_jax.random.key(seed), 7919)
    shape, _ = _output_shape()
    out = dict(inp)
    out["loss_target"] = _jax.random.normal(_jax.random.fold_in(key, 0), shape, _jnp.float32)
    for i, name in enumerate(TWIN_WEIGHTS):
        w = inp[name].astype(_jnp.float32)
        if MOMENT_SCALE is None:
            s = _jnp.sqrt(_jnp.mean(_jnp.square(w)) + 1e-30)
        else:
            s = MOMENT_SCALE[name]
        km, kv = _jax.random.split(_jax.random.fold_in(key, i + 1))
        out[name] = w
        out["m_" + name] = s * _jax.random.normal(km, w.shape, _jnp.float32)
        out["v_" + name] = (s * s) * _jax.random.uniform(kv, w.shape, _jnp.float32, 0.5, 1.5)
    if N_MICROBATCH > 1:
        for name, axis in PER_EXAMPLE_BATCH_AXIS.items():
            out[name] = _to_microbatches(out[name], axis)
    return {'x': out['x'], 'rel_bias': out['rel_bias'], 'norm_mix_pre': out['norm_mix_pre'], 'norm_mix_post': out['norm_mix_post'], 'w_in': out['w_in'], 'conv_rnn_w': out['conv_rnn_w'], 'conv_rnn_b': out['conv_rnn_b'], 'w_rg_a': out['w_rg_a'], 'b_rg_a': out['b_rg_a'], 'w_rg_x': out['w_rg_x'], 'b_rg_x': out['b_rg_x'], 'lru_lambda': out['lru_lambda'], 'w_branch_rnn': out['w_branch_rnn'], 'w_branch_att': out['w_branch_att'], 'w_out': out['w_out'], 'norm_ffn_pre': out['norm_ffn_pre'], 'norm_ffn_post': out['norm_ffn_post'], 'w_ffn_gate': out['w_ffn_gate'], 'w_ffn_up': out['w_ffn_up'], 'conv_ffn_w': out['conv_ffn_w'], 'conv_ffn_b': out['conv_ffn_b'], 'w_ffn_down': out['w_ffn_down'], 'loss_target': out['loss_target'], 'm_rel_bias': out['m_rel_bias'], 'm_norm_mix_pre': out['m_norm_mix_pre'], 'm_norm_mix_post': out['m_norm_mix_post'], 'm_w_in': out['m_w_in'], 'm_conv_rnn_w': out['m_conv_rnn_w'], 'm_conv_rnn_b': out['m_conv_rnn_b'], 'm_w_rg_a': out['m_w_rg_a'], 'm_b_rg_a': out['m_b_rg_a'], 'm_w_rg_x': out['m_w_rg_x'], 'm_b_rg_x': out['m_b_rg_x'], 'm_lru_lambda': out['m_lru_lambda'], 'm_w_branch_rnn': out['m_w_branch_rnn'], 'm_w_branch_att': out['m_w_branch_att'], 'm_w_out': out['m_w_out'], 'm_norm_ffn_pre': out['m_norm_ffn_pre'], 'm_norm_ffn_post': out['m_norm_ffn_post'], 'm_w_ffn_gate': out['m_w_ffn_gate'], 'm_w_ffn_up': out['m_w_ffn_up'], 'm_conv_ffn_w': out['m_conv_ffn_w'], 'm_conv_ffn_b': out['m_conv_ffn_b'], 'm_w_ffn_down': out['m_w_ffn_down'], 'v_rel_bias': out['v_rel_bias'], 'v_norm_mix_pre': out['v_norm_mix_pre'], 'v_norm_mix_post': out['v_norm_mix_post'], 'v_w_in': out['v_w_in'], 'v_conv_rnn_w': out['v_conv_rnn_w'], 'v_conv_rnn_b': out['v_conv_rnn_b'], 'v_w_rg_a': out['v_w_rg_a'], 'v_b_rg_a': out['v_b_rg_a'], 'v_w_rg_x': out['v_w_rg_x'], 'v_b_rg_x': out['v_b_rg_x'], 'v_lru_lambda': out['v_lru_lambda'], 'v_w_branch_rnn': out['v_w_branch_rnn'], 'v_w_branch_att': out['v_w_branch_att'], 'v_w_out': out['v_w_out'], 'v_norm_ffn_pre': out['v_norm_ffn_pre'], 'v_norm_ffn_post': out['v_norm_ffn_post'], 'v_w_ffn_gate': out['v_w_ffn_gate'], 'v_w_ffn_up': out['v_w_ffn_up'], 'v_conv_ffn_w': out['v_conv_ffn_w'], 'v_conv_ffn_b': out['v_conv_ffn_b'], 'v_w_ffn_down': out['v_w_ffn_down']}


def _loss(weights, diff, rest, loss_target):
    with _jax.named_scope("forward"):
        args = {**rest, TWIN_DIFF_INPUT: diff, **{k: w.astype(_WEIGHT_DTYPES[k]) for k, w in weights.items()}}
        y = _forward(args)
    with _jax.named_scope("loss_head"):
        err = _jnp.square(y.astype(_jnp.float32) - loss_target)
        return 0.5 * _jnp.sum(_jnp.mean(err, axis=-1)) if err.ndim else 0.5 * err


def _adamw(w, g, m, v):
    m = ADAM_B1 * m + (1.0 - ADAM_B1) * g
    v = ADAM_B2 * v + (1.0 - ADAM_B2) * _jnp.square(g)
    m_hat = m / (1.0 - ADAM_B1 ** ADAM_STEP)
    v_hat = v / (1.0 - ADAM_B2 ** ADAM_STEP)
    delta = -ADAM_LR * (m_hat / (_jnp.sqrt(v_hat) + ADAM_EPS) + ADAM_WD * w)
    return delta, m, v


def reference(x, rel_bias, norm_mix_pre, norm_mix_post, w_in, conv_rnn_w, conv_rnn_b, w_rg_a, b_rg_a, w_rg_x, b_rg_x, lru_lambda, w_branch_rnn, w_branch_att, w_out, norm_ffn_pre, norm_ffn_post, w_ffn_gate, w_ffn_up, conv_ffn_w, conv_ffn_b, w_ffn_down, loss_target, m_rel_bias, m_norm_mix_pre, m_norm_mix_post, m_w_in, m_conv_rnn_w, m_conv_rnn_b, m_w_rg_a, m_b_rg_a, m_w_rg_x, m_b_rg_x, m_lru_lambda, m_w_branch_rnn, m_w_branch_att, m_w_out, m_norm_ffn_pre, m_norm_ffn_post, m_w_ffn_gate, m_w_ffn_up, m_conv_ffn_w, m_conv_ffn_b, m_w_ffn_down, v_rel_bias, v_norm_mix_pre, v_norm_mix_post, v_w_in, v_conv_rnn_w, v_conv_rnn_b, v_w_rg_a, v_b_rg_a, v_w_rg_x, v_b_rg_x, v_lru_lambda, v_w_branch_rnn, v_w_branch_att, v_w_out, v_norm_ffn_pre, v_norm_ffn_post, v_w_ffn_gate, v_w_ffn_up, v_conv_ffn_w, v_conv_ffn_b, v_w_ffn_down):
    given = dict(x=x, rel_bias=rel_bias, norm_mix_pre=norm_mix_pre, norm_mix_post=norm_mix_post, w_in=w_in, conv_rnn_w=conv_rnn_w, conv_rnn_b=conv_rnn_b, w_rg_a=w_rg_a, b_rg_a=b_rg_a, w_rg_x=w_rg_x, b_rg_x=b_rg_x, lru_lambda=lru_lambda, w_branch_rnn=w_branch_rnn, w_branch_att=w_branch_att, w_out=w_out, norm_ffn_pre=norm_ffn_pre, norm_ffn_post=norm_ffn_post, w_ffn_gate=w_ffn_gate, w_ffn_up=w_ffn_up, conv_ffn_w=conv_ffn_w, conv_ffn_b=conv_ffn_b, w_ffn_down=w_ffn_down, loss_target=loss_target, m_rel_bias=m_rel_bias, m_norm_mix_pre=m_norm_mix_pre, m_norm_mix_post=m_norm_mix_post, m_w_in=m_w_in, m_conv_rnn_w=m_conv_rnn_w, m_conv_rnn_b=m_conv_rnn_b, m_w_rg_a=m_w_rg_a, m_b_rg_a=m_b_rg_a, m_w_rg_x=m_w_rg_x, m_b_rg_x=m_b_rg_x, m_lru_lambda=m_lru_lambda, m_w_branch_rnn=m_w_branch_rnn, m_w_branch_att=m_w_branch_att, m_w_out=m_w_out, m_norm_ffn_pre=m_norm_ffn_pre, m_norm_ffn_post=m_norm_ffn_post, m_w_ffn_gate=m_w_ffn_gate, m_w_ffn_up=m_w_ffn_up, m_conv_ffn_w=m_conv_ffn_w, m_conv_ffn_b=m_conv_ffn_b, m_w_ffn_down=m_w_ffn_down, v_rel_bias=v_rel_bias, v_norm_mix_pre=v_norm_mix_pre, v_norm_mix_post=v_norm_mix_post, v_w_in=v_w_in, v_conv_rnn_w=v_conv_rnn_w, v_conv_rnn_b=v_conv_rnn_b, v_w_rg_a=v_w_rg_a, v_b_rg_a=v_b_rg_a, v_w_rg_x=v_w_rg_x, v_b_rg_x=v_b_rg_x, v_lru_lambda=v_lru_lambda, v_w_branch_rnn=v_w_branch_rnn, v_w_branch_att=v_w_branch_att, v_w_out=v_w_out, v_norm_ffn_pre=v_norm_ffn_pre, v_norm_ffn_post=v_norm_ffn_post, v_w_ffn_gate=v_w_ffn_gate, v_w_ffn_up=v_w_ffn_up, v_conv_ffn_w=v_conv_ffn_w, v_conv_ffn_b=v_conv_ffn_b, v_w_ffn_down=v_w_ffn_down)
    weights = {n: given[n] for n in TWIN_WEIGHTS}
    shared = {n: given[n] for n in SHARED_INPUTS}
    per_example = {n: given[n] for n in ['x']}
    grad_fn = _jax.value_and_grad(_loss, argnums=(0, 1))

    def one_microbatch(ex, loss_target):
        ex = dict(ex)
        diff = ex.pop(TWIN_DIFF_INPUT)
        return grad_fn(weights, diff, {**shared, **ex}, loss_target)

    if N_MICROBATCH == 1:
        loss, (grad_w, grad_x) = one_microbatch(per_example, given["loss_target"])
    else:
        def body(carry, xs):
            loss_sum, grad_sum = carry
            l_k, (gw_k, gx_k) = one_microbatch(xs[0], xs[1])
            with _jax.named_scope("update"):
                return (loss_sum + l_k, _jax.tree.map(_jnp.add, grad_sum, gw_k)), gx_k

        init = (_jnp.zeros((), _jnp.float32), _jax.tree.map(_jnp.zeros_like, weights))
        (loss, grad_w), grad_x = _jax.lax.scan(body, init, (per_example, given["loss_target"]))
    with _jax.named_scope("update"):
        delta_w, new_m, new_v = {}, {}, {}
        for n in TWIN_WEIGHTS:
            delta_w[n], new_m[n], new_v[n] = _adamw(weights[n], grad_w[n], given["m_" + n], given["v_" + n])
    return (loss, grad_x, *[grad_w[n] for n in TWIN_WEIGHTS], *[delta_w[n] for n in TWIN_WEIGHTS],
            *[new_m[n] for n in TWIN_WEIGHTS], *[new_v[n] for n in TWIN_WEIGHTS])
```

```python
import functools
import math

import numpy as np
import jax
import jax.numpy as jnp
from jax import lax
from jax.experimental import pallas as pl
from jax.experimental.pallas import tpu as pltpu

f32, bf16 = jnp.float32, jnp.bfloat16
SDS = jax.ShapeDtypeStruct
MESH = pl.DeviceIdType.MESH
ANY = pl.BlockSpec(memory_space=pl.ANY)

D = 1024
SEQ = 2048
RNN_W = 1280
RNN_BLOCKS = 10
LANES = 128
RNN_CONV = 4
LRU_C = 8.0
HD = 128
KVH = 4
DILATIONS = (1, 4, 16)
NG = 3
ATT_BLK = 128
NBLK_SEQ = SEQ // ATT_BLK
REL_BUCKETS = 32
REL_MAX_DIST = 2048
FFN_W = 3072
FFN_CONV = 3
EPS = 1e-6
IN_W = 5888
ATT_COLS = 5 * HD
C_ATT = RNN_W
C_GATE = RNN_W + KVH * ATT_COLS
NEG = -1e30

ADAM_LR, ADAM_B1, ADAM_B2, ADAM_EPS, ADAM_WD, ADAM_STEP = 0.001, 0.9, 0.999, 1e-08, 0.01, 10

VMEM_LIMIT_BYTES = 56 * 1024 * 1024
N_DEV = 8


def _params(sem=None):
    return pltpu.CompilerParams(dimension_semantics=sem, vmem_limit_bytes=VMEM_LIMIT_BYTES)


def _sigmoid(x):
    return 1.0 / (1.0 + jnp.exp(-x))


_DIMS = {"nn": (((1,), (0,)), ((), ())), "nt": (((1,), (1,)), ((), ())), "tn": (((0,), (0,)), ((), ()))}


def _mm(a, b, mode, out_dtype, name, tm, tn, tk, add=None):
    if mode == "nn":
        (M, K), (K2, N) = a.shape, b.shape
    elif mode == "nt":
        (M, K), (N, K2) = a.shape, b.shape
    else:
        (K, M), (K2, N) = a.shape, b.shape
    assert K == K2 and M % tm == 0 and N % tn == 0 and K % tk == 0, (name, a.shape, b.shape)
    nk = K // tk
    has_add = add is not None

    def body(*refs):
        if has_add:
            a_ref, b_ref, c_ref, o_ref = refs[:4]
        else:
            a_ref, b_ref, o_ref = refs[:3]
        part = lax.dot_general(a_ref[...].astype(bf16), b_ref[...].astype(bf16), _DIMS[mode],
                               preferred_element_type=f32)

        def finish(acc):
            if has_add:
                acc = acc + c_ref[...]
            o_ref[...] = acc.astype(o_ref.dtype)

        if nk == 1:
            finish(part)
        else:
            acc_ref = refs[-1]
            k = pl.program_id(2)

            @pl.when(k == 0)
            def _():
                acc_ref[...] = part

            @pl.when(k > 0)
            def _():
                acc_ref[...] += part

            @pl.when(k == nk - 1)
            def _():
                finish(acc_ref[...])

    if mode == "tn":
        a_spec = pl.BlockSpec((tk, tm), lambda i, j, k: (k, i))
    else:
        a_spec = pl.BlockSpec((tm, tk), lambda i, j, k: (i, k))
    if mode == "nt":
        b_spec = pl.BlockSpec((tn, tk), lambda i, j, k: (j, k))
    else:
        b_spec = pl.BlockSpec((tk, tn), lambda i, j, k: (k, j))
    o_spec = pl.BlockSpec((tm, tn), lambda i, j, k: (i, j))
    in_specs = [a_spec, b_spec] + ([o_spec] if has_add else [])
    args = (a, b) + ((add,) if has_add else ())
    return pl.pallas_call(
        body, name=name, out_shape=SDS((M, N), out_dtype), grid=(M // tm, N // tn, nk),
        in_specs=in_specs, out_specs=o_spec,
        scratch_shapes=[pltpu.VMEM((tm, tn), f32)] if nk > 1 else [],
        compiler_params=_params(("parallel", "parallel", "arbitrary")),
    )(*args)


ROW_TILE = 512


def _rms_fwd(x, g):
    r = lax.rsqrt(jnp.mean(x * x, axis=-1, keepdims=True) + EPS)
    return x * r * g


def _rms_bwd(x, g, dy):
    r = lax.rsqrt(jnp.mean(x * x, axis=-1, keepdims=True) + EPS)
    xh = x * r
    dxh = dy * g
    dx = r * (dxh - xh * jnp.mean(dxh * xh, axis=-1, keepdims=True))
    return dx, jnp.sum(dy * xh, axis=0, keepdims=True)


def _acc_out(ref, val):
    @pl.when(pl.program_id(0) == 0)
    def _():
        ref[...] = val

    @pl.when(pl.program_id(0) > 0)
    def _():
        ref[...] += val


def _row_spec(width=D):
    return pl.BlockSpec((ROW_TILE, width), lambda i: (i, 0))


def _vec_spec(width=D):
    return pl.BlockSpec((1, width), lambda i: (0, 0))


def _norm_in(x, g):
    def body(x_ref, g_ref, o_ref):
        o_ref[...] = _rms_fwd(x_ref[...], g_ref[...]).astype(bf16)

    T = x.shape[0]
    return pl.pallas_call(body, name="norm_in", out_shape=SDS((T, D), bf16), grid=(T // ROW_TILE,),
                          in_specs=[_row_spec(), _vec_spec()], out_specs=_row_spec(),
                          compiler_params=_params(("parallel",)))(x, g)


def _mid_fwd(x, mix, g_post, g_fpre):
    def body(x_ref, mix_ref, gp_ref, gf_ref, h1_ref, hn2_ref):
        h1 = x_ref[...] + _rms_fwd(mix_ref[...], gp_ref[...])
        h1_ref[...] = h1
        hn2_ref[...] = _rms_fwd(h1, gf_ref[...]).astype(bf16)

    T = x.shape[0]
    return pl.pallas_call(body, name="mid_fwd", out_shape=(SDS((T, D), f32), SDS((T, D), bf16)),
                          grid=(T // ROW_TILE,),
                          in_specs=[_row_spec(), _row_spec(), _vec_spec(), _vec_spec()],
                          out_specs=(_row_spec(), _row_spec()),
                          compiler_params=_params(("parallel",)))(x, mix, g_post, g_fpre)


def _final(h1, ff, g_fpost, target):
    def body(h1_ref, ff_ref, g_ref, t_ref, loss_ref, dy_ref, dff_ref, dg_ref):
        ff = ff_ref[...]
        g = g_ref[...]
        e = h1_ref[...] + _rms_fwd(ff, g) - t_ref[...]
        part = jnp.sum(jnp.sum(e * e, axis=1, keepdims=True), axis=0, keepdims=True) * (0.5 / D)
        dy = e * (1.0 / D)
        dy_ref[...] = dy
        dff, dg = _rms_bwd(ff, g, dy)
        dff_ref[...] = dff.astype(bf16)
        _acc_out(loss_ref, part)
        _acc_out(dg_ref, dg)

    T = h1.shape[0]
    return pl.pallas_call(
        body, name="final", out_shape=(SDS((1, 1), f32), SDS((T, D), f32), SDS((T, D), bf16), SDS((1, D), f32)),
        grid=(T // ROW_TILE,),
        in_specs=[_row_spec(), _row_spec(), _vec_spec(), _row_spec()],
        out_specs=(pl.BlockSpec((1, 1), lambda i: (0, 0)), _row_spec(), _row_spec(), _vec_spec()),
        compiler_params=_params(("arbitrary",)))(h1, ff, g_fpost, target)


def _mid_bwd(dy, dhn2, h1, g_fpre, mix, g_post):
    def body(dy_ref, dhn2_ref, h1_ref, gf_ref, mix_ref, gp_ref, dh1_ref, dmix_ref, dgf_ref, dgp_ref):
        d1, dgf = _rms_bwd(h1_ref[...], gf_ref[...], dhn2_ref[...])
        dh1 = dy_ref[...] + d1
        dh1_ref[...] = dh1
        dmix, dgp = _rms_bwd(mix_ref[...], gp_ref[...], dh1)
        dmix_ref[...] = dmix.astype(bf16)
        _acc_out(dgf_ref, dgf)
        _acc_out(dgp_ref, dgp)

    T = dy.shape[0]
    return pl.pallas_call(
        body, name="mid_bwd", out_shape=(SDS((T, D), f32), SDS((T, D), bf16), SDS((1, D), f32), SDS((1, D), f32)),
        grid=(T // ROW_TILE,),
        in_specs=[_row_spec(), _row_spec(), _row_spec(), _vec_spec(), _row_spec(), _vec_spec()],
        out_specs=(_row_spec(), _row_spec(), _vec_spec(), _vec_spec()),
        compiler_params=_params(("arbitrary",)))(dy, dhn2, h1, g_fpre, mix, g_post)


def _in_bwd(dh1, dhn, x, g_pre):
    def body(dh1_ref, dhn_ref, x_ref, g_ref, dx_ref, dg_ref):
        d, dg = _rms_bwd(x_ref[...], g_ref[...], dhn_ref[...])
        dx_ref[...] = dh1_ref[...] + d
        _acc_out(dg_ref, dg)

    T = x.shape[0]
    return pl.pallas_call(
        body, name="in_bwd", out_shape=(SDS((T, D), f32), SDS((1, D), f32)), grid=(T // ROW_TILE,),
        in_specs=[_row_spec(), _row_spec(), _row_spec(), _vec_spec()],
        out_specs=(_row_spec(), _vec_spec()),
        compiler_params=_params(("arbitrary",)))(dh1, dhn, x, g_pre)


def _shift_dn(x, d, row):
    if d == 0:
        return x
    return jnp.where(row >= d, pltpu.roll(x, d, 0), 0.0)


def _shift_up(x, d, row):
    if d == 0:
        return x
    n = x.shape[0]
    return jnp.where(row < n - d, pltpu.roll(x, n - d, 0), 0.0)


def _conv_fwd(x, w_ref, b, row):
    K = w_ref.shape[0]
    y = b
    for k in range(K):
        y = y + w_ref[k:k + 1, :] * _shift_dn(x, K - 1 - k, row)
    return y


def _conv_bwd(x, w_ref, dy, row):
    K = w_ref.shape[0]
    dx = jnp.zeros_like(dy)
    dws = []
    for k in range(K):
        dx = dx + w_ref[k:k + 1, :] * _shift_up(dy, K - 1 - k, row)
        dws.append(jnp.sum(dy * _shift_dn(x, K - 1 - k, row), axis=0, keepdims=True))
    return dx, dws, jnp.sum(dy, axis=0, keepdims=True)


def _scan_fwd(a, u, row):
    n = a.shape[0]
    d = 1
    while d < n:
        u = u + a * jnp.where(row >= d, pltpu.roll(u, d, 0), 0.0)
        if 2 * d < n:
            a = a * jnp.where(row >= d, pltpu.roll(a, d, 0), 1.0)
        d *= 2
    return u


def _scan_bwd(b, u, row):
    n = b.shape[0]
    d = 1
    while d < n:
        u = u + b * jnp.where(row < n - d, pltpu.roll(u, n - d, 0), 0.0)
        if 2 * d < n:
            b = b * jnp.where(row < n - d, pltpu.roll(b, n - d, 0), 1.0)
        d *= 2
    return u


def _neg_expm1(z):
    series = -z * (1.0 + z * (0.5 + z * (1.0 / 6.0 + z * (1.0 / 24.0 + z * (1.0 / 120.0)))))
    return jnp.where(z > -0.1, series, 1.0 - jnp.exp(z))


def _rnn_gates(xr, cw_ref, cb, wa, ba, wx, bx, lam, row):
    xc = _conv_fwd(xr, cw_ref, cb, row)
    xcb = xc.astype(bf16)
    r = _sigmoid(jnp.dot(xcb, wa.astype(bf16), preferred_element_type=f32) + ba)
    i = _sigmoid(jnp.dot(xcb, wx.astype(bf16), preferred_element_type=f32) + bx)
    z = -lam
    sp = jnp.maximum(z, 0.0) + jnp.log(1.0 + jnp.exp(-jnp.abs(z)))
    log_a = (-LRU_C * sp) * r
    a = jnp.exp(log_a)
    s = jnp.sqrt(_neg_expm1(2.0 * log_a))
    return xc, xcb, r, i, sp, a, s


def _rnn_specs(B):
    blk = lambda: pl.BlockSpec((SEQ, LANES), lambda b, n: (b, n))
    return dict(
        act=blk,
        convw=pl.BlockSpec((RNN_CONV, LANES), lambda b, n: (0, n)),
        vec=lambda: pl.BlockSpec((1, LANES), lambda b, n: (0, n)),
        gate=lambda: pl.BlockSpec((None, LANES, LANES), lambda b, n: (n, 0, 0)),
    )


def _rnn_fwd(proj, cw, cb, wa, ba, wx, bx, lam):
    T = proj.shape[0]
    B = T // SEQ

    def body(x_ref, cw_ref, cb_ref, wa_ref, ba_ref, wx_ref, bx_ref, lam_ref, h_ref):
        row = lax.broadcasted_iota(jnp.int32, (SEQ, LANES), 0)
        xc, _, r, i, sp, a, s = _rnn_gates(x_ref[...], cw_ref, cb_ref[...], wa_ref[...], ba_ref[...],
                                          wx_ref[...], bx_ref[...], lam_ref[...], row)
        h_ref[...] = _scan_fwd(a, s * (i * xc), row)

    sp_ = _rnn_specs(B)
    return pl.pallas_call(
        body, name="rnn_fwd", out_shape=SDS((T, RNN_W), f32), grid=(B, RNN_BLOCKS),
        in_specs=[sp_["act"](), sp_["convw"], sp_["vec"](), sp_["gate"](), sp_["vec"](), sp_["gate"](),
                  sp_["vec"](), sp_["vec"]()],
        out_specs=sp_["act"](),
        compiler_params=_params(("parallel", "parallel")))(proj, cw, cb, wa, ba, wx, bx, lam)


def _rnn_bwd(proj, h, dh, cw, cb, wa, ba, wx, bx, lam):
    T = proj.shape[0]
    B = T // SEQ

    def body(x_ref, h_ref, dh_ref, cw_ref, cb_ref, wa_ref, ba_ref, wx_ref, bx_ref, lam_ref,
             dx_ref, dcw_ref, dcb_ref, dwa_ref, dba_ref, dwx_ref, dbx_ref, dlam_ref):
        row = lax.broadcasted_iota(jnp.int32, (SEQ, LANES), 0)
        xr = x_ref[...]
        wa, wx, lam = wa_ref[...], wx_ref[...], lam_ref[...]
        xc, xcb, r, i, sp, a, s = _rnn_gates(xr, cw_ref, cb_ref[...], wa, ba_ref[...], wx, bx_ref[...], lam, row)
        hprev = _shift_dn(h_ref[...], 1, row)
        g = _scan_bwd(_shift_up(a, 1, row), dh_ref[...], row)
        da = g * hprev
        ds = g * (i * xc)
        di = g * (s * xc)
        dxc = g * (s * i)
        dla = da * a - ds * (a * a) / s
        dr = dla * (-LRU_C * sp)
        dsp = jnp.sum(dla * (-LRU_C * r), axis=0, keepdims=True)
        dlam = -dsp * _sigmoid(-lam)
        dga = dr * r * (1.0 - r)
        dgx = di * i * (1.0 - i)
        dgab, dgxb = dga.astype(bf16), dgx.astype(bf16)
        dwa = lax.dot_general(xcb, dgab, _DIMS["tn"], preferred_element_type=f32)
        dwx = lax.dot_general(xcb, dgxb, _DIMS["tn"], preferred_element_type=f32)
        dxc = dxc + lax.dot_general(dgab, wa.astype(bf16), _DIMS["nt"], preferred_element_type=f32)
        dxc = dxc + lax.dot_general(dgxb, wx.astype(bf16), _DIMS["nt"], preferred_element_type=f32)
        dx, dws, db = _conv_bwd(xr, cw_ref, dxc, row)
        dx_ref[...] = dx.astype(bf16)
        first = pl.program_id(1) == 0

        def acc(ref, val):
            @pl.when(first)
            def _():
                ref[...] = val

            @pl.when(jnp.logical_not(first))
            def _():
                ref[...] += val

        for k in range(RNN_CONV):
            acc(dcw_ref.at[k:k + 1, :], dws[k])
        acc(dcb_ref, db)
        acc(dwa_ref, dwa)
        acc(dba_ref, jnp.sum(dga, axis=0, keepdims=True))
        acc(dwx_ref, dwx)
        acc(dbx_ref, jnp.sum(dgx, axis=0, keepdims=True))
        acc(dlam_ref, dlam)

    blk = lambda: pl.BlockSpec((SEQ, LANES), lambda n, b: (b, n))
    convw = lambda: pl.BlockSpec((RNN_CONV, LANES), lambda n, b: (0, n))
    vec = lambda: pl.BlockSpec((1, LANES), lambda n, b: (0, n))
    gate = lambda: pl.BlockSpec((None, LANES, LANES), lambda n, b: (n, 0, 0))
    vshape = SDS((1, RNN_W), f32)
    gshape = SDS((RNN_BLOCKS, LANES, LANES), f32)
    return pl.pallas_call(
        body, name="rnn_bwd",
        out_shape=(SDS((T, RNN_W), bf16), SDS((RNN_CONV, RNN_W), f32), vshape, gshape, vshape, gshape, vshape, vshape),
        grid=(RNN_BLOCKS, B),
        in_specs=[blk(), blk(), blk(), convw(), vec(), gate(), vec(), gate(), vec(), vec()],
        out_specs=(blk(), convw(), vec(), gate(), vec(), gate(), vec(), vec()),
        compiler_params=_params(("parallel", "arbitrary")))(proj, h, dh, cw, cb, wa, ba, wx, bx, lam)


def _t5_bucket(dist):
    max_exact = REL_BUCKETS // 2
    d = np.maximum(dist, 1).astype(np.float32)
    large = max_exact + np.log(d / max_exact) / math.log(REL_MAX_DIST / max_exact) * (REL_BUCKETS - max_exact)
    large = np.minimum(large.astype(np.int32), REL_BUCKETS - 1)
    return np.where(dist < max_exact, dist, large).astype(np.int32)


def _bucket_maps():
    qi = np.arange(ATT_BLK)[:, None]
    kj = np.arange(2 * ATT_BLK)[None, :]
    delta = ATT_BLK + qi - kj
    valid = (delta >= 0) & (delta <= ATT_BLK)
    maps = [np.where(valid, _t5_bucket(np.maximum(delta, 0) * r), -1) for r in DILATIONS]
    return np.stack(maps).astype(np.int32)


def _bias_tables(rel_bias, buckets):
    def body(rb_ref, bk_ref, o_ref):
        for g in range(NG):
            bk = bk_ref[g]
            for h in range(KVH):
                acc = jnp.full(bk.shape, NEG, f32)
                for b in range(REL_BUCKETS):
                    acc = jnp.where(bk == b, rb_ref[b, g * KVH + h], acc)
                o_ref[h, g] = acc

    return pl.pallas_call(
        body, name="bias_tables", out_shape=SDS((KVH, NG, ATT_BLK, 2 * ATT_BLK), f32),
        in_specs=[pl.BlockSpec(memory_space=pltpu.SMEM), pl.BlockSpec(memory_space=pltpu.VMEM)],
        out_specs=pl.BlockSpec(memory_space=pltpu.VMEM), compiler_params=_params())(rel_bias, buckets)


def _bias_grad(dbias, buckets):
    def body(db_ref, bk_ref, o_ref):
        rr = lax.broadcasted_iota(jnp.int32, (REL_BUCKETS, LANES), 0)
        cc = lax.broadcasted_iota(jnp.int32, (REL_BUCKETS, LANES), 1)
        out = jnp.zeros((REL_BUCKETS, LANES), f32)
        for g in range(NG):
            bk = bk_ref[g]
            for h in range(KVH):
                d = db_ref[h, g]
                for b in range(REL_BUCKETS):
                    m = jnp.where(bk == b, d, 0.0)
                    s = jnp.sum(jnp.sum(m, axis=1, keepdims=True), axis=0, keepdims=True)
                    out = jnp.where((rr == b) & (cc == g * KVH + h), s, out)
        o_ref[...] = out

    return pl.pallas_call(body, name="bias_grad", out_shape=SDS((REL_BUCKETS, LANES), f32),
                          compiler_params=_params())(dbias, buckets)


def _to_sub(dst_ref, src_ref, r, dtype):
    M = SEQ // r
    for c in range(r):
        if r == 1:
            v = src_ref[...]
        else:
            v = src_ref[pl.ds(c, M, stride=r), :]
        dst_ref[pl.ds(c * M, M), :] = v.astype(dtype)


def _from_sub(dst_ref, src_ref, r, accumulate=False):
    M = SEQ // r
    for c in range(r):
        v = src_ref[pl.ds(c * M, M), :]
        idx = slice(None) if r == 1 else pl.ds(c, M, stride=r)
        if accumulate:
            dst_ref[idx, :] = dst_ref[idx, :] + v
        else:
            dst_ref[idx, :] = v


_COL = lambda k: slice(k * HD, (k + 1) * HD)
SCALE = HD ** -0.5


def _qkv_spec(k, bh):
    def index(*ids):
        b, h = bh(*ids)
        return (b, C_ATT // HD + 5 * h + k)

    return pl.BlockSpec((SEQ, HD), index)


def _att_fwd(proj, bias):
    T = proj.shape[0]
    B = T // SEQ

    def body(q0_ref, q1_ref, q2_ref, k_ref, v_ref, bias_ref, o_ref, lse_ref, qp, kp, vp, op, lp, og, lg):
        q_refs = (q0_ref, q1_ref, q2_ref)
        for g, r in enumerate(DILATIONS):
            nb = NBLK_SEQ // r
            _to_sub(qp, q_refs[g], r, bf16)
            _to_sub(kp, k_ref, r, bf16)
            _to_sub(vp, v_ref, r, bf16)
            bias_p = bias_ref[g, :, 0:ATT_BLK]
            bias_c = bias_ref[g, :, ATT_BLK:2 * ATT_BLK]

            def step(j, carry):
                cur = pl.ds(pl.multiple_of(j * ATT_BLK, ATT_BLK), ATT_BLK)
                prv = pl.ds(pl.multiple_of(jnp.maximum(j - 1, 0) * ATT_BLK, ATT_BLK), ATT_BLK)
                has_prev = (j % nb) != 0
                q = qp[cur, :]
                s_c = lax.dot_general(q, kp[cur, :], _DIMS["nt"], preferred_element_type=f32) * SCALE + bias_c
                s_p = lax.dot_general(q, kp[prv, :], _DIMS["nt"], preferred_element_type=f32) * SCALE + bias_p
                s_p = jnp.where(has_prev, s_p, NEG)
                m = jnp.maximum(jnp.max(s_c, axis=-1, keepdims=True), jnp.max(s_p, axis=-1, keepdims=True))
                p_c = jnp.exp(s_c - m)
                p_p = jnp.exp(s_p - m)
                den = jnp.sum(p_c, axis=-1, keepdims=True) + jnp.sum(p_p, axis=-1, keepdims=True)
                o = jnp.dot(p_c.astype(bf16), vp[cur, :], preferred_element_type=f32)
                o = o + jnp.dot(p_p.astype(bf16), vp[prv, :], preferred_element_type=f32)
                op[cur, :] = o / den
                lp[cur, :] = jnp.broadcast_to(m + jnp.log(den), (ATT_BLK, HD))
                return carry

            lax.fori_loop(0, NBLK_SEQ, step, 0)
            _from_sub(og.at[g], op, r)
            _from_sub(lg.at[g], lp, r)
        l0, l1, l2 = lg[0], lg[1], lg[2]
        mx = jnp.maximum(jnp.maximum(l0, l1), l2)
        e0, e1, e2 = jnp.exp(l0 - mx), jnp.exp(l1 - mx), jnp.exp(l2 - mx)
        den = e0 + e1 + e2
        o_ref[...] = (e0 * og[0] + e1 * og[1] + e2 * og[2]) / den
        lse_ref[...] = mx + jnp.log(den)

    return pl.pallas_call(
        body, name="att_fwd", out_shape=(SDS((T, KVH * HD), f32), SDS((KVH, T, HD), f32)), grid=(B, KVH),
        in_specs=[_qkv_spec(k, lambda b, h: (b, h)) for k in range(5)]
                 + [pl.BlockSpec((None, NG, ATT_BLK, 2 * ATT_BLK), lambda b, h: (h, 0, 0, 0))],
        out_specs=(pl.BlockSpec((SEQ, HD), lambda b, h: (b, h)),
                   pl.BlockSpec((None, SEQ, HD), lambda b, h: (h, b, 0))),
        scratch_shapes=[pltpu.VMEM((SEQ, HD), bf16)] * 3 + [pltpu.VMEM((SEQ, HD), f32)] * 2
                       + [pltpu.VMEM((NG, SEQ, HD), f32)] * 2,
        compiler_params=_params(("parallel", "parallel")))(proj, proj, proj, proj, proj, bias)


def _att_bwd(proj, bias, o, lse, do):
    T = proj.shape[0]
    B = T // SEQ

    def body(q0_ref, q1_ref, q2_ref, k_ref, v_ref, bias_ref, o_ref, lse_ref, do_ref, dx_ref, db_ref,
             qp, kp, vp, dop, lp, dlp, dqp, dkp, dvp, dln, nat, dkn, dvn):
        q_refs = (q0_ref, q1_ref, q2_ref)
        first = pl.program_id(1) == 0

        @pl.when(first)
        def _():
            db_ref[...] = jnp.zeros_like(db_ref)

        dln[...] = jnp.broadcast_to(jnp.sum(do_ref[...] * o_ref[...], axis=-1, keepdims=True), (SEQ, HD))
        dkn[...] = jnp.zeros_like(dkn)
        dvn[...] = jnp.zeros_like(dvn)
        for g, r in enumerate(DILATIONS):
            nb = NBLK_SEQ // r
            _to_sub(qp, q_refs[g], r, bf16)
            _to_sub(kp, k_ref, r, bf16)
            _to_sub(vp, v_ref, r, bf16)
            _to_sub(dop, do_ref, r, bf16)
            _to_sub(lp, lse_ref, r, f32)
            _to_sub(dlp, dln, r, f32)
            dkp[...] = jnp.zeros_like(dkp)
            dvp[...] = jnp.zeros_like(dvp)
            bias_p = bias_ref[g, :, 0:ATT_BLK]
            bias_c = bias_ref[g, :, ATT_BLK:2 * ATT_BLK]

            def step(j, carry):
                cur = pl.ds(pl.multiple_of(j * ATT_BLK, ATT_BLK), ATT_BLK)
                prv = pl.ds(pl.multiple_of(jnp.maximum(j - 1, 0) * ATT_BLK, ATT_BLK), ATT_BLK)
                has_prev = (j % nb) != 0
                q, kc, kv, vc, vv, dob = qp[cur, :], kp[cur, :], kp[prv, :], vp[cur, :], vp[prv, :], dop[cur, :]
                lse_j = lp[cur, 0:1]
                dl_j = dlp[cur, 0:1]
                s_c = lax.dot_general(q, kc, _DIMS["nt"], preferred_element_type=f32) * SCALE + bias_c
                s_p = lax.dot_general(q, kv, _DIMS["nt"], preferred_element_type=f32) * SCALE + bias_p
                s_p = jnp.where(has_prev, s_p, NEG)
                p_c = jnp.exp(s_c - lse_j)
                p_p = jnp.exp(s_p - lse_j)
                dp_c = lax.dot_general(dob, vc, _DIMS["nt"], preferred_element_type=f32)
                dp_p = lax.dot_general(dob, vv, _DIMS["nt"], preferred_element_type=f32)
                ds_c = p_c * (dp_c - dl_j)
                ds_p = p_p * (dp_p - dl_j)
                db_ref[g, :, ATT_BLK:2 * ATT_BLK] += ds_c
                db_ref[g, :, 0:ATT_BLK] += ds_p
                ds_cb, ds_pb = ds_c.astype(bf16), ds_p.astype(bf16)
                p_cb, p_pb = p_c.astype(bf16), p_p.astype(bf16)
                dq = jnp.dot(ds_cb, kc, preferred_element_type=f32) + jnp.dot(ds_pb, kv, preferred_element_type=f32)
                dqp[cur, :] = dq * SCALE
                dkp[cur, :] += lax.dot_general(ds_cb, q, _DIMS["tn"], preferred_element_type=f32) * SCALE
                dkp[prv, :] += lax.dot_general(ds_pb, q, _DIMS["tn"], preferred_element_type=f32) * SCALE
                dvp[cur, :] += lax.dot_general(p_cb, dob, _DIMS["tn"], preferred_element_type=f32)
                dvp[prv, :] += lax.dot_general(p_pb, dob, _DIMS["tn"], preferred_element_type=f32)
                return carry

            lax.fori_loop(0, NBLK_SEQ, step, 0)
            _from_sub(nat, dqp, r)
            dx_ref[:, _COL(g)] = nat[...].astype(bf16)
            _from_sub(dkn, dkp, r, accumulate=True)
            _from_sub(dvn, dvp, r, accumulate=True)
        dx_ref[:, _COL(3)] = dkn[...].astype(bf16)
        dx_ref[:, _COL(4)] = dvn[...].astype(bf16)

    blk = lambda: pl.BlockSpec((SEQ, HD), lambda h, b: (b, h))
    bias_spec = lambda: pl.BlockSpec((None, NG, ATT_BLK, 2 * ATT_BLK), lambda h, b: (h, 0, 0, 0))
    return pl.pallas_call(
        body, name="att_bwd",
        out_shape=(SDS((T, KVH * ATT_COLS), bf16), SDS((KVH, NG, ATT_BLK, 2 * ATT_BLK), f32)), grid=(KVH, B),
        in_specs=[_qkv_spec(k, lambda h, b: (b, h)) for k in range(5)]
                 + [bias_spec(), blk(), pl.BlockSpec((None, SEQ, HD), lambda h, b: (h, b, 0)), blk()],
        out_specs=(pl.BlockSpec((SEQ, ATT_COLS), lambda h, b: (b, h)), bias_spec()),
        scratch_shapes=[pltpu.VMEM((SEQ, HD), bf16)] * 4 + [pltpu.VMEM((SEQ, HD), f32)] * 9,
        compiler_params=_params(("parallel", "arbitrary")))(proj, proj, proj, proj, proj, bias, o, lse, do)


MERGE_ROWS, MERGE_COLS = 1024, 256
_G_RNN_BLK = C_GATE // MERGE_COLS
_G_ATT_BLK = (C_GATE + D) // MERGE_COLS


def _merge_specs():
    cols = lambda off: pl.BlockSpec((MERGE_ROWS, MERGE_COLS), lambda i, j: (i, off + j))
    return cols(_G_RNN_BLK), cols(_G_ATT_BLK), cols(0)


def _merge_fwd(proj, pr, pa):
    def body(gr_ref, ga_ref, pr_ref, pa_ref, o_ref):
        o_ref[...] = (_sigmoid(gr_ref[...]) * pr_ref[...] + _sigmoid(ga_ref[...]) * pa_ref[...]).astype(bf16)

    T = proj.shape[0]
    s_gr, s_ga, s0 = _merge_specs()
    return pl.pallas_call(body, name="merge_fwd", out_shape=SDS((T, D), bf16),
                          grid=(T // MERGE_ROWS, D // MERGE_COLS),
                          in_specs=[s_gr, s_ga, s0, s0], out_specs=s0,
                          compiler_params=_params(("parallel", "parallel")))(proj, proj, pr, pa)


def _merge_bwd(proj, pr, pa, dm):
    nj = D // MERGE_COLS

    def body(g_ref, pr_ref, pa_ref, dm_ref, dp_ref, dg_ref):
        dm_ = dm_ref[...]
        s = _sigmoid(g_ref[...])
        p = jnp.where(pl.program_id(1) < nj, pr_ref[...], pa_ref[...])
        dp_ref[...] = (dm_ * s).astype(bf16)
        dg_ref[...] = (dm_ * p * s * (1.0 - s)).astype(bf16)

    T = proj.shape[0]
    blk = (MERGE_ROWS, MERGE_COLS)
    wrap = pl.BlockSpec(blk, lambda i, j: (i, j % nj))
    out = pl.BlockSpec(blk, lambda i, j: (i, j))
    return pl.pallas_call(
        body, name="merge_bwd", out_shape=(SDS((T, 2 * D), bf16), SDS((T, 2 * D), bf16)),
        grid=(T // MERGE_ROWS, 2 * nj),
        in_specs=[pl.BlockSpec(blk, lambda i, j: (i, _G_RNN_BLK + j)), wrap, wrap, wrap], out_specs=(out, out),
        compiler_params=_params(("parallel", "parallel")))(proj, pr, pa, dm)


FFN_COLS = 256
GELU_C = math.sqrt(2.0 / math.pi)
GELU_A = 0.044715


def _gelu_parts(x):
    t = jnp.tanh(GELU_C * (x + GELU_A * x * x * x))
    return 0.5 * x * (1.0 + t), t


def _ffn_act_fwd(gpre, up, cw, cb):
    def body(g_ref, u_ref, cw_ref, cb_ref, o_ref):
        row = lax.broadcasted_iota(jnp.int32, (SEQ, FFN_COLS), 0)
        gate = _conv_fwd(g_ref[...], cw_ref, cb_ref[...], row)
        o_ref[...] = (_gelu_parts(gate)[0] * u_ref[...]).astype(bf16)

    T = gpre.shape[0]
    blk = lambda: pl.BlockSpec((SEQ, FFN_COLS), lambda b, j: (b, j))
    return pl.pallas_call(
        body, name="ffn_act_fwd", out_shape=SDS((T, FFN_W), bf16), grid=(T // SEQ, FFN_W // FFN_COLS),
        in_specs=[blk(), blk(), pl.BlockSpec((FFN_CONV, FFN_COLS), lambda b, j: (0, j)),
                  pl.BlockSpec((1, FFN_COLS), lambda b, j: (0, j))],
        out_specs=blk(), compiler_params=_params(("parallel", "parallel")))(gpre, up, cw, cb)


def _ffn_act_bwd(gpre, up, cw, cb, dact):
    def body(g_ref, u_ref, cw_ref, cb_ref, da_ref, dg_ref, du_ref, dcw_ref, dcb_ref):
        row = lax.broadcasted_iota(jnp.int32, (SEQ, FFN_COLS), 0)
        gp = g_ref[...]
        gate = _conv_fwd(gp, cw_ref, cb_ref[...], row)
        gel, t = _gelu_parts(gate)
        da = da_ref[...]
        du_ref[...] = (da * gel).astype(bf16)
        dgel = 0.5 * (1.0 + t) + 0.5 * gate * (1.0 - t * t) * (GELU_C * (1.0 + 3.0 * GELU_A * gate * gate))
        dgate = da * u_ref[...] * dgel
        dx, dws, db = _conv_bwd(gp, cw_ref, dgate, row)
        dg_ref[...] = dx.astype(bf16)
        first = pl.program_id(1) == 0

        def acc(ref, val):
            @pl.when(first)
            def _():
                ref[...] = val

            @pl.when(jnp.logical_not(first))
            def _():
                ref[...] += val

        for k in range(FFN_CONV):
            acc(dcw_ref.at[k:k + 1, :], dws[k])
        acc(dcb_ref, db)

    T = gpre.shape[0]
    blk = lambda: pl.BlockSpec((SEQ, FFN_COLS), lambda j, b: (b, j))
    cws = lambda: pl.BlockSpec((FFN_CONV, FFN_COLS), lambda j, b: (0, j))
    cbs = lambda: pl.BlockSpec((1, FFN_COLS), lambda j, b: (0, j))
    return pl.pallas_call(
        body, name="ffn_act_bwd",
        out_shape=(SDS((T, FFN_W), bf16), SDS((T, FFN_W), bf16), SDS((FFN_CONV, FFN_W), f32), SDS((1, FFN_W), f32)),
        grid=(FFN_W // FFN_COLS, T // SEQ),
        in_specs=[blk(), blk(), cws(), cbs(), blk()], out_specs=(blk(), blk(), cws(), cbs()),
        compiler_params=_params(("parallel", "arbitrary")))(gpre, up, cw, cb, dact)


def _coords():
    return lax.axis_index("x"), lax.axis_index("y"), lax.axis_index("c")


def _all_gather(xs, name):
    R, C = xs.shape

    def body(x_ref, out_ref, send_sems, recv_sems, local_sem):
        x, y, c = _coords()
        me, sibling = (x, y, c), (x, y, 1 - c)
        chips = [(1 - x, y), (x, 1 - y), (1 - x, 1 - y)]

        def rows(px, py, pc):
            return out_ref.at[4 * px + 2 * py + pc]

        def copy(k, block, to, src=None):
            return pltpu.make_async_remote_copy(
                src_ref=rows(*block) if src is None else src, dst_ref=rows(*block),
                send_sem=send_sems.at[k], recv_sem=recv_sems.at[k], device_id=to, device_id_type=MESH)

        mine = pltpu.make_async_copy(x_ref, rows(*me), local_sem)
        mine.start()
        first = [copy(0, me, sibling, src=x_ref)]
        first += [copy(1 + j, me, (*chip, c), src=x_ref) for j, chip in enumerate(chips)]
        for cp in first:
            cp.start()
        passed = [copy(4 + j, (*chip, c), sibling) for j, chip in enumerate(chips)]
        for j, chip in enumerate(chips):
            copy(1 + j, (*chip, c), me).wait_recv()
            passed[j].start()
        copy(0, sibling, me).wait_recv()
        for j, chip in enumerate(chips):
            copy(4 + j, (*chip, 1 - c), me).wait_recv()
        for cp in first + passed:
            cp.wait_send()
        mine.wait()

    return pl.pallas_call(
        body, name=name, out_shape=SDS((N_DEV, R, C), xs.dtype), in_specs=[ANY], out_specs=ANY,
        scratch_shapes=[pltpu.SemaphoreType.DMA((7,)), pltpu.SemaphoreType.DMA((7,)), pltpu.SemaphoreType.DMA(())],
    )(xs)


def _sibling_exchange(g8):
    R, C = g8.shape[-2:]

    def body(g_ref, recv_ref, send_sems, recv_sems):
        x, y, c = _coords()
        cps = []
        for k in range(4):
            cps.append(pltpu.make_async_remote_copy(
                src_ref=g_ref.at[2 * k + 1 - c], dst_ref=recv_ref.at[k], send_sem=send_sems.at[k],
                recv_sem=recv_sems.at[k], device_id=(x, y, 1 - c), device_id_type=MESH))
        for cp in cps:
            cp.start()
        for cp in cps:
            cp.wait()

    return pl.pallas_call(
        body, name="rs_sibling", out_shape=SDS((4, R, C), g8.dtype), in_specs=[ANY], out_specs=ANY,
        scratch_shapes=[pltpu.SemaphoreType.DMA((4,)), pltpu.SemaphoreType.DMA((4,))])(g8)


def _chip_exchange(pa):
    R, C = pa.shape[-2:]

    def body(p_ref, recv_ref, send_sems, recv_sems):
        x, y, c = _coords()
        chips = [(1 - x, y), (x, 1 - y), (1 - x, 1 - y)]
        cps = []
        for j, (cx, cy) in enumerate(chips):
            cps.append(pltpu.make_async_remote_copy(
                src_ref=p_ref.at[2 * cx + cy], dst_ref=recv_ref.at[j], send_sem=send_sems.at[j],
                recv_sem=recv_sems.at[j], device_id=(cx, cy, c), device_id_type=MESH))
        for cp in cps:
            cp.start()
        for cp in cps:
            cp.wait()

    return pl.pallas_call(
        body, name="rs_chips", out_shape=SDS((3, R, C), pa.dtype), in_specs=[ANY], out_specs=ANY,
        scratch_shapes=[pltpu.SemaphoreType.DMA((3,)), pltpu.SemaphoreType.DMA((3,))])(pa)


PACK_TILE = 1632
PACK_ROWS = 11 * PACK_TILE


def _pair_sum(g8, recv, c_idx):
    R = g8.shape[-2]

    def body(c_ref, g_ref, r_ref, o_ref):
        o_ref[...] = (g_ref[...] + r_ref[...]).astype(bf16)

    return pl.pallas_call(
        body, name="rs_pair_sum", out_shape=SDS((4, R, LANES), bf16),
        grid_spec=pltpu.PrefetchScalarGridSpec(
            num_scalar_prefetch=1, grid=(4, R // PACK_TILE),
            in_specs=[pl.BlockSpec((None, PACK_TILE, LANES), lambda k, i, c: (2 * k + c[0], i, 0)),
                      pl.BlockSpec((None, PACK_TILE, LANES), lambda k, i, c: (k, i, 0))],
            out_specs=pl.BlockSpec((None, PACK_TILE, LANES), lambda k, i, c: (k, i, 0))),
        compiler_params=_params(("parallel", "parallel")))(c_idx, g8, recv)


def _adamw_math(w, g, m, v):
    m = ADAM_B1 * m + (1.0 - ADAM_B1) * g
    v = ADAM_B2 * v + (1.0 - ADAM_B2) * (g * g)
    m_hat = m / (1.0 - ADAM_B1 ** ADAM_STEP)
    v_hat = v / (1.0 - ADAM_B2 ** ADAM_STEP)
    delta = -ADAM_LR * (m_hat / (jnp.sqrt(v_hat) + ADAM_EPS) + ADAM_WD * w)
    return delta, m, v


def _adamw_sharded(pa, recv, k_idx, w, m, v):
    R = w.shape[0]

    def body(k_ref, p_ref, r_ref, w_ref, m_ref, v_ref, g_ref, d_ref, nm_ref, nv_ref):
        g = p_ref[...].astype(f32)
        for j in range(3):
            g = g + r_ref[j].astype(f32)
        d, nm, nv = _adamw_math(w_ref[...], g, m_ref[...], v_ref[...])
        g_ref[...], d_ref[...], nm_ref[...], nv_ref[...] = g, d, nm, nv

    t = lambda: pl.BlockSpec((PACK_TILE, LANES), lambda i, k: (i, 0))
    return pl.pallas_call(
        body, name="adamw_sharded", out_shape=(SDS((R, LANES), f32),) * 4,
        grid_spec=pltpu.PrefetchScalarGridSpec(
            num_scalar_prefetch=1, grid=(R // PACK_TILE,),
            in_specs=[pl.BlockSpec((None, PACK_TILE, LANES), lambda i, k: (k[0], i, 0)),
                      pl.BlockSpec((3, PACK_TILE, LANES), lambda i, k: (0, i, 0)), t(), t(), t()],
            out_specs=(t(), t(), t(), t())),
        compiler_params=_params(("parallel",)))(k_idx, pa, recv, w, m, v)


SMALL_TILE = 896
SMALL_ROWS = 3 * SMALL_TILE


def _adamw_replicated(parts, w, m, v):
    R = w.shape[0]

    def body(p_ref, w_ref, m_ref, v_ref, g_ref, d_ref, nm_ref, nv_ref):
        g = p_ref[0]
        for j in range(1, N_DEV):
            g = g + p_ref[j]
        d, nm, nv = _adamw_math(w_ref[...], g, m_ref[...], v_ref[...])
        g_ref[...], d_ref[...], nm_ref[...], nv_ref[...] = g, d, nm, nv

    t = lambda: pl.BlockSpec((SMALL_TILE, LANES), lambda i: (i, 0))
    return pl.pallas_call(
        body, name="adamw_replicated", out_shape=(SDS((R, LANES), f32),) * 4, grid=(R // SMALL_TILE,),
        in_specs=[pl.BlockSpec((N_DEV, SMALL_TILE, LANES), lambda i: (0, i, 0)), t(), t(), t()],
        out_specs=(t(), t(), t(), t()), compiler_params=_params(("parallel",)))(parts, w, m, v)


MATS = (("w_in", D, IN_W, "col"), ("w_branch_rnn", RNN_W, D, "row"), ("w_branch_att", KVH * HD, D, "col"),
        ("w_out", D, D, "row"), ("w_ffn_gate", D, FFN_W, "col"), ("w_ffn_up", D, FFN_W, "col"),
        ("w_ffn_down", FFN_W, D, "row"))
MAT_ROWS = tuple(r * c // N_DEV // LANES for _, r, c, _ in MATS)
MAT_ROWS_TOTAL = sum(MAT_ROWS)
CONV_RNN_ROWS, CONV_FFN_ROWS = 8, 16
assert MAT_ROWS_TOTAL + CONV_RNN_ROWS + CONV_FFN_ROWS + 8 == PACK_ROWS

SMALLS = (("rel_bias", (REL_BUCKETS, 12)), ("norm_mix_pre", (1, D)), ("norm_mix_post", (1, D)),
          ("conv_rnn_b", (1, RNN_W)), ("w_rg_a", (1, RNN_BLOCKS, LANES, LANES)), ("b_rg_a", (1, RNN_W)),
          ("w_rg_x", (1, RNN_BLOCKS, LANES, LANES)), ("b_rg_x", (1, RNN_W)), ("lru_lambda", (1, RNN_W)),
          ("norm_ffn_pre", (1, D)), ("norm_ffn_post", (1, D)), ("conv_ffn_b", (1, FFN_W)))


def _rows8(n_elems):
    return -(-n_elems // (8 * LANES)) * 8


def _pad_rows(a2d, rows):
    return jnp.pad(a2d, ((0, rows - a2d.shape[0]), (0, 0)))


def _pack_small(named):
    parts = []
    for name, shape in SMALLS:
        n = int(np.prod(shape))
        flat = named[name].reshape(-1)
        flat = jnp.pad(flat, (0, _rows8(n) * LANES - n))
        parts.append(flat.reshape(-1, LANES))
    out = jnp.concatenate(parts, axis=0)
    return _pad_rows(out, SMALL_ROWS)


def _unpack_small(pack):
    out, r0 = {}, 0
    for name, shape in SMALLS:
        n = int(np.prod(shape))
        rows = _rows8(n)
        out[name] = pack[r0:r0 + rows].reshape(-1)[:n].reshape(shape)
        r0 += rows
    return out


def _pack_shards(named):
    parts = [named[name].reshape(-1, LANES) for name, _, _, _ in MATS]
    parts.append(_pad_rows(named["conv_rnn_w"].reshape(-1, LANES), CONV_RNN_ROWS))
    parts.append(_pad_rows(named["conv_ffn_w"].reshape(-1, LANES), CONV_FFN_ROWS + 8))
    return jnp.concatenate(parts, axis=0)


def _unpack_shards(pack):
    out, r0 = {}, 0
    for (name, r, c, kind), rows in zip(MATS, MAT_ROWS):
        shape = (1, r, c // N_DEV) if kind == "col" else (1, r // N_DEV, c)
        out[name] = pack[r0:r0 + rows].reshape(shape)
        r0 += rows
    out["conv_rnn_w"] = pack[r0:r0 + 5].reshape(1, RNN_CONV, RNN_W // N_DEV)
    r0 += CONV_RNN_ROWS
    out["conv_ffn_w"] = pack[r0:r0 + 9].reshape(1, FFN_CONV, FFN_W // N_DEV)
    return out


def _gathered_to_full(g, r, c, kind):
    if kind == "row":
        return g.reshape(r, c)
    return g.reshape(N_DEV, r, c // N_DEV).transpose(1, 0, 2).reshape(r, c)


def _full_to_blocks(a, kind):
    r, c = a.shape
    if kind == "row":
        return a.reshape(N_DEV, -1, LANES)
    return a.reshape(r, N_DEV, c // N_DEV).transpose(1, 0, 2).reshape(N_DEV, -1, LANES)


def _w_in_to_internal(w):
    K = w.shape[0]
    q = w[:, 1280:2816].reshape(K, NG, KVH, 1, HD).transpose(0, 2, 1, 3, 4).reshape(K, KVH, NG, HD)
    k = w[:, 2816:3328].reshape(K, KVH, 1, HD)
    v = w[:, 3328:3840].reshape(K, KVH, 1, HD)
    att = jnp.concatenate([q, k, v], axis=2).reshape(K, KVH * ATT_COLS)
    return jnp.concatenate([w[:, :1280], att, w[:, 3840:]], axis=1)


def _w_in_from_internal(w):
    K = w.shape[0]
    att = w[:, C_ATT:C_GATE].reshape(K, KVH, 5, HD)
    q = att[:, :, 0:3].transpose(0, 2, 1, 3).reshape(K, NG * KVH * HD)
    k = att[:, :, 3].reshape(K, KVH * HD)
    v = att[:, :, 4].reshape(K, KVH * HD)
    return jnp.concatenate([w[:, :C_ATT], q, k, v, w[:, C_GATE:]], axis=1)


_IN_NAMES = ('x', 'rel_bias', 'norm_mix_pre', 'norm_mix_post', 'w_in', 'conv_rnn_w', 'conv_rnn_b', 'w_rg_a', 'b_rg_a',
             'w_rg_x', 'b_rg_x', 'lru_lambda', 'w_branch_rnn', 'w_branch_att', 'w_out', 'norm_ffn_pre',
             'norm_ffn_post', 'w_ffn_gate', 'w_ffn_up', 'conv_ffn_w', 'conv_ffn_b', 'w_ffn_down')
_WEIGHTS = _IN_NAMES[1:]
_SHARDED = tuple(n for n, _, _, _ in MATS) + ("conv_rnn_w", "conv_ffn_w")


def _step(inp):
    x_idx, y_idx, c_idx = _coords()
    W = {n: inp[n] for n in _WEIGHTS}
    x = inp["x"].reshape(-1, D)
    target = inp["loss_target"].reshape(-1, D)
    T = x.shape[0]

    local = {n: W[n][0] for n in _SHARDED}
    mats16 = jnp.concatenate([local[n].astype(bf16).reshape(-1, LANES) for n, _, _, _ in MATS], axis=0)
    conv32 = jnp.concatenate([_pad_rows(local["conv_rnn_w"].reshape(-1, LANES), CONV_RNN_ROWS),
                              _pad_rows(local["conv_ffn_w"].reshape(-1, LANES), CONV_FFN_ROWS)], axis=0)
    conv_bits = lax.bitcast_convert_type(conv32, bf16).reshape(48, LANES)
    wpack = jnp.concatenate([mats16, conv_bits], axis=0)
    gathered = _all_gather(wpack, "ag_weights")
    full, r0 = {}, 0
    for (name, r, c, kind), rows in zip(MATS, MAT_ROWS):
        full[name] = _gathered_to_full(gathered[:, r0:r0 + rows], r, c, kind)
        r0 += rows
    conv_all = lax.bitcast_convert_type(gathered[:, r0:r0 + 48].reshape(N_DEV, 24, LANES, 2), f32)
    cw_rnn = conv_all[:, 0:5].reshape(N_DEV, RNN_CONV, RNN_W // N_DEV).transpose(1, 0, 2).reshape(RNN_CONV, RNN_W)
    cw_ffn = conv_all[:, CONV_RNN_ROWS:CONV_RNN_ROWS + 9]
    cw_ffn = cw_ffn.reshape(N_DEV, FFN_CONV, FFN_W // N_DEV).transpose(1, 0, 2).reshape(FFN_CONV, FFN_W)
    w_in = _w_in_to_internal(full["w_in"])
    w_brnn, w_batt, w_out = full["w_branch_rnn"], full["w_branch_att"], full["w_out"]
    w_gate, w_up, w_down = full["w_ffn_gate"], full["w_ffn_up"], full["w_ffn_down"]

    wa, wx = W["w_rg_a"][0], W["w_rg_x"][0]
    buckets = jnp.asarray(_bucket_maps())

    hn = _norm_in(x, W["norm_mix_pre"])
    proj = _mm(hn, w_in, "nn", f32, "mm_proj", 1024, 256, 1024)
    h_rnn = _rnn_fwd(proj, cw_rnn, W["conv_rnn_b"], wa, W["b_rg_a"], wx, W["b_rg_x"], W["lru_lambda"])
    bias = _bias_tables(W["rel_bias"], buckets)
    o_att, lse = _att_fwd(proj, bias)
    pr = _mm(h_rnn, w_brnn, "nn", f32, "mm_pr", 1024, 512, 1280)
    pa = _mm(o_att, w_batt, "nn", f32, "mm_pa", 1024, 512, 512)
    merged = _merge_fwd(proj, pr, pa)
    mix = _mm(merged, w_out, "nn", f32, "mm_mix", 1024, 512, 1024)
    h1, hn2 = _mid_fwd(x, mix, W["norm_mix_post"], W["norm_ffn_pre"])
    gpre = _mm(hn2, w_gate, "nn", f32, "mm_gate", 1024, 512, 1024)
    up = _mm(hn2, w_up, "nn", f32, "mm_up", 1024, 512, 1024)
    act = _ffn_act_fwd(gpre, up, cw_ffn, W["conv_ffn_b"])
    ff = _mm(act, w_down, "nn", f32, "mm_down", 1024, 512, 1024)
    loss_part, dy, dff, dg_fpost = _final(h1, ff, W["norm_ffn_post"], target)

    dact = _mm(dff, w_down, "nt", f32, "mm_dact", 1024, 512, 1024)
    dw_down = _mm(act, dff, "tn", f32, "mm_dw_down", 512, 512, 1024)
    dgpre, dup, dcw_ffn, dcb_ffn = _ffn_act_bwd(gpre, up, cw_ffn, W["conv_ffn_b"], dact)
    dw_gate = _mm(hn2, dgpre, "tn", f32, "mm_dw_gate", 512, 512, 1024)
    dw_up = _mm(hn2, dup, "tn", f32, "mm_dw_up", 512, 512, 1024)
    dhn2 = _mm(dgpre, w_gate, "nt", f32, "mm_dhn2_a", 1024, 512, 1024)
    dhn2 = _mm(dup, w_up, "nt", f32, "mm_dhn2_b", 1024, 512, 1024, add=dhn2)
    dh1, dmix, dg_fpre, dg_post = _mid_bwd(dy, dhn2, h1, W["norm_ffn_pre"], mix, W["norm_mix_post"])
    dmerged = _mm(dmix, w_out, "nt", f32, "mm_dmerged", 1024, 512, 1024)
    dw_out = _mm(merged, dmix, "tn", f32, "mm_dw_out", 512, 512, 1024)
    dprpa, dgates = _merge_bwd(proj, pr, pa, dmerged)
    dpr, dpa = dprpa[:, :D], dprpa[:, D:]
    dh_rnn = _mm(dpr, w_brnn, "nt", f32, "mm_dh_rnn", 1024, 640, 1024)
    dw_brnn = _mm(h_rnn, dpr, "tn", f32, "mm_dw_brnn", 640, 512, 1024)
    do_att = _mm(dpa, w_batt, "nt", f32, "mm_do_att", 1024, 512, 1024)
    dw_batt = _mm(o_att, dpa, "tn", f32, "mm_dw_batt", 512, 512, 1024)
    dqkv, dbias = _att_bwd(proj, bias, o_att, lse, do_att)
    drel = _bias_grad(dbias, buckets)
    dxr, dcw_rnn, dcb_rnn, dwa, dba, dwx, dbx, dlam = _rnn_bwd(
        proj, h_rnn, dh_rnn, cw_rnn, W["conv_rnn_b"], wa, W["b_rg_a"], wx, W["b_rg_x"], W["lru_lambda"])
    dw_in = jnp.concatenate([
        _mm(hn, dxr, "tn", f32, "mm_dw_in_r", 512, 640, 1024),
        _mm(hn, dqkv, "tn", f32, "mm_dw_in_a", 512, 640, 1024),
        _mm(hn, dgates, "tn", f32, "mm_dw_in_g", 512, 512, 1024)], axis=1)
    dhn = _mm(dxr, w_in[:, :C_ATT], "nt", f32, "mm_dhn_r", 1024, 512, 1280)
    dhn = _mm(dqkv, w_in[:, C_ATT:C_GATE], "nt", f32, "mm_dhn_a", 1024, 512, 1280, add=dhn)
    dhn = _mm(dgates, w_in[:, C_GATE:], "nt", f32, "mm_dhn_g", 1024, 512, 1024, add=dhn)
    dx, dg_pre = _in_bwd(dh1, dhn, x, W["norm_mix_pre"])

    dfull = {"w_in": _w_in_from_internal(dw_in), "w_branch_rnn": dw_brnn, "w_branch_att": dw_batt, "w_out": dw_out,
             "w_ffn_gate": dw_gate, "w_ffn_up": dw_up, "w_ffn_down": dw_down}
    blocks = [_full_to_blocks(dfull[n], kind) for n, _, _, kind in MATS]
    conv_r = _full_to_blocks(dcw_rnn, "col")
    conv_f = _full_to_blocks(dcw_ffn, "col")
    blocks.append(jnp.pad(conv_r, ((0, 0), (0, CONV_RNN_ROWS - 5), (0, 0))))
    blocks.append(jnp.pad(conv_f, ((0, 0), (0, CONV_FFN_ROWS + 8 - 9), (0, 0))))
    gpack = jnp.concatenate(blocks, axis=1)
    from_sibling = _sibling_exchange(gpack)
    c_arr = jnp.reshape(c_idx, (1,)).astype(jnp.int32)
    pair = _pair_sum(gpack, from_sibling, c_arr)
    from_chips = _chip_exchange(pair)
    k_arr = jnp.reshape(2 * x_idx + y_idx, (1,)).astype(jnp.int32)
    wl = {n: inp[n][0] for n in _SHARDED}
    ml = {n: inp["m_" + n][0] for n in _SHARDED}
    vl = {n: inp["v_" + n][0] for n in _SHARDED}
    sh = [_unpack_shards(p) for p in _adamw_sharded(pair, from_chips, k_arr, _pack_shards(wl), _pack_shards(ml),
                                                    _pack_shards(vl))]

    gsmall = {"rel_bias": drel[:, :12], "norm_mix_pre": dg_pre, "norm_mix_post": dg_post, "conv_rnn_b": dcb_rnn,
              "w_rg_a": dwa[None], "b_rg_a": dba, "w_rg_x": dwx[None], "b_rg_x": dbx, "lru_lambda": dlam,
              "norm_ffn_pre": dg_fpre, "norm_ffn_post": dg_fpost, "conv_ffn_b": dcb_ffn}
    parts = _all_gather(_pack_small(gsmall), "ag_small_grads")
    sm = [_unpack_small(p) for p in _adamw_replicated(
        parts, _pack_small({n: inp[n] for n, _ in SMALLS}), _pack_small({n: inp["m_" + n] for n, _ in SMALLS}),
        _pack_small({n: inp["v_" + n] for n, _ in SMALLS}))]

    loss = lax.psum(loss_part[0, 0], ("x", "y", "c"))
    outs = [loss, dx.reshape(inp["x"].shape)]
    for k in range(4):
        for n in _WEIGHTS:
            outs.append(sh[k][n] if n in _SHARDED else sm[k][n])
    return tuple(outs)


def kernel(x, rel_bias, norm_mix_pre, norm_mix_post, w_in, conv_rnn_w, conv_rnn_b, w_rg_a, b_rg_a, w_rg_x, b_rg_x, lru_lambda, w_branch_rnn, w_branch_att, w_out, norm_ffn_pre, norm_ffn_post, w_ffn_gate, w_ffn_up, conv_ffn_w, conv_ffn_b, w_ffn_down, loss_target, m_rel_bias, m_norm_mix_pre, m_norm_mix_post, m_w_in, m_conv_rnn_w, m_conv_rnn_b, m_w_rg_a, m_b_rg_a, m_w_rg_x, m_b_rg_x, m_lru_lambda, m_w_branch_rnn, m_w_branch_att, m_w_out, m_norm_ffn_pre, m_norm_ffn_post, m_w_ffn_gate, m_w_ffn_up, m_conv_ffn_w, m_conv_ffn_b, m_w_ffn_down, v_rel_bias, v_norm_mix_pre, v_norm_mix_post, v_w_in, v_conv_rnn_w, v_conv_rnn_b, v_w_rg_a, v_b_rg_a, v_w_rg_x, v_b_rg_x, v_lru_lambda, v_w_branch_rnn, v_w_branch_att, v_w_out, v_norm_ffn_pre, v_norm_ffn_post, v_w_ffn_gate, v_w_ffn_up, v_conv_ffn_w, v_conv_ffn_b, v_w_ffn_down):
    vals = locals()
    names = list(_IN_NAMES) + ["loss_target"] + ["m_" + n for n in _WEIGHTS] + ["v_" + n for n in _WEIGHTS]
    return _step({n: vals[n] for n in names})
```

```python
import functools
import math

import numpy as np
import jax
import jax.numpy as jnp
from jax import lax
from jax.experimental import pallas as pl
from jax.experimental.pallas import tpu as pltpu

f32, bf16 = jnp.float32, jnp.bfloat16
SDS = jax.ShapeDtypeStruct
MESH = pl.DeviceIdType.MESH
ANY = pl.BlockSpec(memory_space=pl.ANY)

D = 1024
SEQ = 2048
RNN_W = 1280
RNN_BLOCKS = 10
LANES = 128
RNN_CONV = 4
LRU_C = 8.0
HD = 128
KVH = 4
DILATIONS = (1, 4, 16)
NG = 3
ATT_BLK = 128
NBLK_SEQ = SEQ // ATT_BLK
REL_BUCKETS = 32
REL_MAX_DIST = 2048
FFN_W = 3072
FFN_CONV = 3
EPS = 1e-6
IN_W = 5888
ATT_COLS = 5 * HD
C_ATT = RNN_W
C_GATE = RNN_W + KVH * ATT_COLS
NEG = -1e30

ADAM_LR, ADAM_B1, ADAM_B2, ADAM_EPS, ADAM_WD, ADAM_STEP = 0.001, 0.9, 0.999, 1e-08, 0.01, 10

VMEM_LIMIT_BYTES = 56 * 1024 * 1024
N_DEV = 8


def _params(sem=None):
    return pltpu.CompilerParams(dimension_semantics=sem, vmem_limit_bytes=VMEM_LIMIT_BYTES)


def _sigmoid(x):
    return 1.0 / (1.0 + jnp.exp(-x))


_DIMS = {"nn": (((1,), (0,)), ((), ())), "nt": (((1,), (1,)), ((), ())), "tn": (((0,), (0,)), ((), ()))}


def _mm(a, b, mode, out_dtype, name, tm, tn, tk, add=None, cols_outer=False):
    if mode == "nn":
        (M, K), (K2, N) = a.shape, b.shape
    elif mode == "nt":
        (M, K), (N, K2) = a.shape, b.shape
    else:
        (K, M), (K2, N) = a.shape, b.shape
    assert K == K2 and M % tm == 0 and N % tn == 0 and K % tk == 0, (name, a.shape, b.shape)
    nk = K // tk
    has_add = add is not None

    def body(*refs):
        if has_add:
            a_ref, b_ref, c_ref, o_ref = refs[:4]
        else:
            a_ref, b_ref, o_ref = refs[:3]
        part = lax.dot_general(a_ref[...].astype(bf16), b_ref[...].astype(bf16), _DIMS[mode],
                               preferred_element_type=f32)

        def finish(acc):
            if has_add:
                acc = acc + c_ref[...]
            o_ref[...] = acc.astype(o_ref.dtype)

        if nk == 1:
            finish(part)
        else:
            acc_ref = refs[-1]
            k = pl.program_id(2)

            @pl.when(k == 0)
            def _():
                acc_ref[...] = part

            @pl.when(k > 0)
            def _():
                acc_ref[...] += part

            @pl.when(k == nk - 1)
            def _():
                finish(acc_ref[...])

    def ij(f):
        return (lambda j, i, k: f(i, j, k)) if cols_outer else f

    if mode == "tn":
        a_spec = pl.BlockSpec((tk, tm), ij(lambda i, j, k: (k, i)))
    else:
        a_spec = pl.BlockSpec((tm, tk), ij(lambda i, j, k: (i, k)))
    if mode == "nt":
        b_spec = pl.BlockSpec((tn, tk), ij(lambda i, j, k: (j, k)))
    else:
        b_spec = pl.BlockSpec((tk, tn), ij(lambda i, j, k: (k, j)))
    o_spec = pl.BlockSpec((tm, tn), ij(lambda i, j, k: (i, j)))
    in_specs = [a_spec, b_spec] + ([o_spec] if has_add else [])
    args = (a, b) + ((add,) if has_add else ())
    grid = (N // tn, M // tm, nk) if cols_outer else (M // tm, N // tn, nk)
    return pl.pallas_call(
        body, name=name, out_shape=SDS((M, N), out_dtype), grid=grid,
        in_specs=in_specs, out_specs=o_spec,
        scratch_shapes=[pltpu.VMEM((tm, tn), f32)] if nk > 1 else [],
        compiler_params=_params(("parallel", "parallel", "arbitrary")),
    )(*args)


ROW_TILE = 512


def _rms_fwd(x, g):
    r = lax.rsqrt(jnp.mean(x * x, axis=-1, keepdims=True) + EPS)
    return x * r * g


def _rms_bwd(x, g, dy):
    r = lax.rsqrt(jnp.mean(x * x, axis=-1, keepdims=True) + EPS)
    xh = x * r
    dxh = dy * g
    dx = r * (dxh - xh * jnp.mean(dxh * xh, axis=-1, keepdims=True))
    return dx, jnp.sum(dy * xh, axis=0, keepdims=True)


def _acc_out(ref, val):
    @pl.when(pl.program_id(0) == 0)
    def _():
        ref[...] = val

    @pl.when(pl.program_id(0) > 0)
    def _():
        ref[...] += val


def _row_spec(width=D):
    return pl.BlockSpec((ROW_TILE, width), lambda i: (i, 0))


def _vec_spec(width=D):
    return pl.BlockSpec((1, width), lambda i: (0, 0))


def _norm_in(x, g):
    def body(x_ref, g_ref, o_ref):
        o_ref[...] = _rms_fwd(x_ref[...], g_ref[...]).astype(bf16)

    T = x.shape[0]
    return pl.pallas_call(body, name="norm_in", out_shape=SDS((T, D), bf16), grid=(T // ROW_TILE,),
                          in_specs=[_row_spec(), _vec_spec()], out_specs=_row_spec(),
                          compiler_params=_params(("parallel",)))(x, g)


def _mid_fwd(x, mix, g_post, g_fpre):
    def body(x_ref, mix_ref, gp_ref, gf_ref, h1_ref, hn2_ref):
        h1 = x_ref[...] + _rms_fwd(mix_ref[...], gp_ref[...])
        h1_ref[...] = h1
        hn2_ref[...] = _rms_fwd(h1, gf_ref[...]).astype(bf16)

    T = x.shape[0]
    return pl.pallas_call(body, name="mid_fwd", out_shape=(SDS((T, D), f32), SDS((T, D), bf16)),
                          grid=(T // ROW_TILE,),
                          in_specs=[_row_spec(), _row_spec(), _vec_spec(), _vec_spec()],
                          out_specs=(_row_spec(), _row_spec()),
                          compiler_params=_params(("parallel",)))(x, mix, g_post, g_fpre)


def _final(h1, ff, g_fpost, target):
    def body(h1_ref, ff_ref, g_ref, t_ref, loss_ref, dy_ref, dff_ref, dg_ref):
        ff = ff_ref[...]
        g = g_ref[...]
        e = h1_ref[...] + _rms_fwd(ff, g) - t_ref[...]
        part = jnp.sum(jnp.sum(e * e, axis=1, keepdims=True), axis=0, keepdims=True) * (0.5 / D)
        dy = e * (1.0 / D)
        dy_ref[...] = dy
        dff, dg = _rms_bwd(ff, g, dy)
        dff_ref[...] = dff.astype(bf16)
        _acc_out(loss_ref, part)
        _acc_out(dg_ref, dg)

    T = h1.shape[0]
    return pl.pallas_call(
        body, name="final", out_shape=(SDS((1, 1), f32), SDS((T, D), f32), SDS((T, D), bf16), SDS((1, D), f32)),
        grid=(T // ROW_TILE,),
        in_specs=[_row_spec(), _row_spec(), _vec_spec(), _row_spec()],
        out_specs=(pl.BlockSpec((1, 1), lambda i: (0, 0)), _row_spec(), _row_spec(), _vec_spec()),
        compiler_params=_params(("arbitrary",)))(h1, ff, g_fpost, target)


def _mid_bwd(dy, dhn2, h1, g_fpre, mix, g_post):
    def body(dy_ref, dhn2_ref, h1_ref, gf_ref, mix_ref, gp_ref, dh1_ref, dmix_ref, dgf_ref, dgp_ref):
        d1, dgf = _rms_bwd(h1_ref[...], gf_ref[...], dhn2_ref[...])
        dh1 = dy_ref[...] + d1
        dh1_ref[...] = dh1
        dmix, dgp = _rms_bwd(mix_ref[...], gp_ref[...], dh1)
        dmix_ref[...] = dmix.astype(bf16)
        _acc_out(dgf_ref, dgf)
        _acc_out(dgp_ref, dgp)

    T = dy.shape[0]
    return pl.pallas_call(
        body, name="mid_bwd", out_shape=(SDS((T, D), f32), SDS((T, D), bf16), SDS((1, D), f32), SDS((1, D), f32)),
        grid=(T // ROW_TILE,),
        in_specs=[_row_spec(), _row_spec(), _row_spec(), _vec_spec(), _row_spec(), _vec_spec()],
        out_specs=(_row_spec(), _row_spec(), _vec_spec(), _vec_spec()),
        compiler_params=_params(("arbitrary",)))(dy, dhn2, h1, g_fpre, mix, g_post)


def _in_bwd(dh1, dhn, x, g_pre):
    def body(dh1_ref, dhn_ref, x_ref, g_ref, dx_ref, dg_ref):
        d, dg = _rms_bwd(x_ref[...], g_ref[...], dhn_ref[...])
        dx_ref[...] = dh1_ref[...] + d
        _acc_out(dg_ref, dg)

    T = x.shape[0]
    return pl.pallas_call(
        body, name="in_bwd", out_shape=(SDS((T, D), f32), SDS((1, D), f32)), grid=(T // ROW_TILE,),
        in_specs=[_row_spec(), _row_spec(), _row_spec(), _vec_spec()],
        out_specs=(_row_spec(), _vec_spec()),
        compiler_params=_params(("arbitrary",)))(dh1, dhn, x, g_pre)


def _shift_dn(x, d, row):
    if d == 0:
        return x
    return jnp.where(row >= d, pltpu.roll(x, d, 0), 0.0)


def _shift_up(x, d, row):
    if d == 0:
        return x
    n = x.shape[0]
    return jnp.where(row < n - d, pltpu.roll(x, n - d, 0), 0.0)


def _conv_fwd(x, w_ref, b, row):
    K = w_ref.shape[0]
    y = b
    for k in range(K):
        y = y + w_ref[k:k + 1, :] * _shift_dn(x, K - 1 - k, row)
    return y


def _conv_bwd(x, w_ref, dy, row):
    K = w_ref.shape[0]
    dx = jnp.zeros_like(dy)
    dws = []
    for k in range(K):
        dx = dx + w_ref[k:k + 1, :] * _shift_up(dy, K - 1 - k, row)
        dws.append(jnp.sum(dy * _shift_dn(x, K - 1 - k, row), axis=0, keepdims=True))
    return dx, dws, jnp.sum(dy, axis=0, keepdims=True)


def _scan_fwd(a, u, row):
    n = a.shape[0]
    d = 1
    while d < n:
        u = u + a * jnp.where(row >= d, pltpu.roll(u, d, 0), 0.0)
        if 2 * d < n:
            a = a * jnp.where(row >= d, pltpu.roll(a, d, 0), 1.0)
        d *= 2
    return u


def _scan_bwd(b, u, row):
    n = b.shape[0]
    d = 1
    while d < n:
        u = u + b * jnp.where(row < n - d, pltpu.roll(u, n - d, 0), 0.0)
        if 2 * d < n:
            b = b * jnp.where(row < n - d, pltpu.roll(b, n - d, 0), 1.0)
        d *= 2
    return u


def _neg_expm1(z):
    series = -z * (1.0 + z * (0.5 + z * (1.0 / 6.0 + z * (1.0 / 24.0 + z * (1.0 / 120.0)))))
    return jnp.where(z > -0.1, series, 1.0 - jnp.exp(z))


def _rnn_gates(xr, cw_ref, cb, wa, ba, wx, bx, lam, row):
    xc = _conv_fwd(xr, cw_ref, cb, row)
    xcb = xc.astype(bf16)
    r = _sigmoid(jnp.dot(xcb, wa.astype(bf16), preferred_element_type=f32) + ba)
    i = _sigmoid(jnp.dot(xcb, wx.astype(bf16), preferred_element_type=f32) + bx)
    z = -lam
    sp = jnp.maximum(z, 0.0) + jnp.log(1.0 + jnp.exp(-jnp.abs(z)))
    log_a = (-LRU_C * sp) * r
    a = jnp.exp(log_a)
    s = jnp.sqrt(_neg_expm1(2.0 * log_a))
    return xc, xcb, r, i, sp, a, s


def _rnn_specs(B):
    blk = lambda: pl.BlockSpec((SEQ, LANES), lambda b, n: (b, n))
    return dict(
        act=blk,
        convw=pl.BlockSpec((RNN_CONV, LANES), lambda b, n: (0, n)),
        vec=lambda: pl.BlockSpec((1, LANES), lambda b, n: (0, n)),
        gate=lambda: pl.BlockSpec((None, LANES, LANES), lambda b, n: (n, 0, 0)),
    )


def _rnn_fwd(proj, cw, cb, wa, ba, wx, bx, lam):
    T = proj.shape[0]
    B = T // SEQ

    def body(x_ref, cw_ref, cb_ref, wa_ref, ba_ref, wx_ref, bx_ref, lam_ref, h_ref):
        row = lax.broadcasted_iota(jnp.int32, (SEQ, LANES), 0)
        xc, _, r, i, sp, a, s = _rnn_gates(x_ref[...], cw_ref, cb_ref[...], wa_ref[...], ba_ref[...],
                                          wx_ref[...], bx_ref[...], lam_ref[...], row)
        h_ref[...] = _scan_fwd(a, s * (i * xc), row)

    sp_ = _rnn_specs(B)
    return pl.pallas_call(
        body, name="rnn_fwd", out_shape=SDS((T, RNN_W), f32), grid=(B, RNN_BLOCKS),
        in_specs=[sp_["act"](), sp_["convw"], sp_["vec"](), sp_["gate"](), sp_["vec"](), sp_["gate"](),
                  sp_["vec"](), sp_["vec"]()],
        out_specs=sp_["act"](),
        compiler_params=_params(("parallel", "parallel")))(proj, cw, cb, wa, ba, wx, bx, lam)


def _rnn_bwd(proj, h, dh, cw, cb, wa, ba, wx, bx, lam):
    T = proj.shape[0]
    B = T // SEQ

    def body(x_ref, h_ref, dh_ref, cw_ref, cb_ref, wa_ref, ba_ref, wx_ref, bx_ref, lam_ref,
             dx_ref, dcw_ref, dcb_ref, dwa_ref, dba_ref, dwx_ref, dbx_ref, dlam_ref):
        row = lax.broadcasted_iota(jnp.int32, (SEQ, LANES), 0)
        xr = x_ref[...]
        wa, wx, lam = wa_ref[...], wx_ref[...], lam_ref[...]
        xc, xcb, r, i, sp, a, s = _rnn_gates(xr, cw_ref, cb_ref[...], wa, ba_ref[...], wx, bx_ref[...], lam, row)
        hprev = _shift_dn(h_ref[...], 1, row)
        g = _scan_bwd(_shift_up(a, 1, row), dh_ref[...].astype(f32), row)
        da = g * hprev
        ds = g * (i * xc)
        di = g * (s * xc)
        dxc = g * (s * i)
        dla = da * a - ds * (a * a) / s
        dr = dla * (-LRU_C * sp)
        dsp = jnp.sum(dla * (-LRU_C * r), axis=0, keepdims=True)
        dlam = -dsp * _sigmoid(-lam)
        dga = dr * r * (1.0 - r)
        dgx = di * i * (1.0 - i)
        dgab, dgxb = dga.astype(bf16), dgx.astype(bf16)
        dwa = lax.dot_general(xcb, dgab, _DIMS["tn"], preferred_element_type=f32)
        dwx = lax.dot_general(xcb, dgxb, _DIMS["tn"], preferred_element_type=f32)
        dxc = dxc + lax.dot_general(dgab, wa.astype(bf16), _DIMS["nt"], preferred_element_type=f32)
        dxc = dxc + lax.dot_general(dgxb, wx.astype(bf16), _DIMS["nt"], preferred_element_type=f32)
        dx, dws, db = _conv_bwd(xr, cw_ref, dxc, row)
        dx_ref[...] = dx.astype(bf16)
        first = pl.program_id(1) == 0

        def acc(ref, val):
            @pl.when(first)
            def _():
                ref[...] = val

            @pl.when(jnp.logical_not(first))
            def _():
                ref[...] += val

        for k in range(RNN_CONV):
            acc(dcw_ref.at[k:k + 1, :], dws[k])
        acc(dcb_ref, db)
        acc(dwa_ref, dwa)
        acc(dba_ref, jnp.sum(dga, axis=0, keepdims=True))
        acc(dwx_ref, dwx)
        acc(dbx_ref, jnp.sum(dgx, axis=0, keepdims=True))
        acc(dlam_ref, dlam)

    blk = lambda: pl.BlockSpec((SEQ, LANES), lambda n, b: (b, n))
    convw = lambda: pl.BlockSpec((RNN_CONV, LANES), lambda n, b: (0, n))
    vec = lambda: pl.BlockSpec((1, LANES), lambda n, b: (0, n))
    gate = lambda: pl.BlockSpec((None, LANES, LANES), lambda n, b: (n, 0, 0))
    vshape = SDS((1, RNN_W), f32)
    gshape = SDS((RNN_BLOCKS, LANES, LANES), f32)
    return pl.pallas_call(
        body, name="rnn_bwd",
        out_shape=(SDS((T, RNN_W), bf16), SDS((RNN_CONV, RNN_W), f32), vshape, gshape, vshape, gshape, vshape, vshape),
        grid=(RNN_BLOCKS, B),
        in_specs=[blk(), blk(), blk(), convw(), vec(), gate(), vec(), gate(), vec(), vec()],
        out_specs=(blk(), convw(), vec(), gate(), vec(), gate(), vec(), vec()),
        compiler_params=_params(("parallel", "arbitrary")))(proj, h, dh, cw, cb, wa, ba, wx, bx, lam)


def _t5_bucket(dist):
    max_exact = REL_BUCKETS // 2
    d = np.maximum(dist, 1).astype(np.float32)
    large = max_exact + np.log(d / max_exact) / math.log(REL_MAX_DIST / max_exact) * (REL_BUCKETS - max_exact)
    large = np.minimum(large.astype(np.int32), REL_BUCKETS - 1)
    return np.where(dist < max_exact, dist, large).astype(np.int32)


def _bucket_maps():
    qi = np.arange(ATT_BLK)[:, None]
    kj = np.arange(2 * ATT_BLK)[None, :]
    delta = ATT_BLK + qi - kj
    valid = (delta >= 0) & (delta <= ATT_BLK)
    maps = [np.where(valid, _t5_bucket(np.maximum(delta, 0) * r), -1) for r in DILATIONS]
    return np.stack(maps).astype(np.int32)


def _bias_tables(rel_bias, buckets):
    def body(rb_ref, bk_ref, o_ref):
        for g in range(NG):
            bk = bk_ref[g]
            for h in range(KVH):
                acc = jnp.full(bk.shape, NEG, f32)
                for b in range(REL_BUCKETS):
                    acc = jnp.where(bk == b, rb_ref[b, g * KVH + h], acc)
                o_ref[h, g] = acc

    return pl.pallas_call(
        body, name="bias_tables", out_shape=SDS((KVH, NG, ATT_BLK, 2 * ATT_BLK), f32),
        in_specs=[pl.BlockSpec(memory_space=pltpu.SMEM), pl.BlockSpec(memory_space=pltpu.VMEM)],
        out_specs=pl.BlockSpec(memory_space=pltpu.VMEM), compiler_params=_params())(rel_bias, buckets)


def _bias_grad(dbias, buckets):
    def body(db_ref, bk_ref, o_ref):
        rr = lax.broadcasted_iota(jnp.int32, (REL_BUCKETS, LANES), 0)
        cc = lax.broadcasted_iota(jnp.int32, (REL_BUCKETS, LANES), 1)
        out = jnp.zeros((REL_BUCKETS, LANES), f32)
        for g in range(NG):
            bk = bk_ref[g]
            for h in range(KVH):
                d = db_ref[h, g]
                for b in range(REL_BUCKETS):
                    m = jnp.where(bk == b, d, 0.0)
                    s = jnp.sum(jnp.sum(m, axis=1, keepdims=True), axis=0, keepdims=True)
                    out = jnp.where((rr == b) & (cc == g * KVH + h), s, out)
        o_ref[...] = out

    return pl.pallas_call(body, name="bias_grad", out_shape=SDS((REL_BUCKETS, LANES), f32),
                          compiler_params=_params())(dbias, buckets)


def _to_sub(dst_ref, src_ref, r, dtype):
    M = SEQ // r
    for c in range(r):
        if r == 1:
            v = src_ref[...]
        else:
            v = src_ref[pl.ds(c, M, stride=r), :]
        dst_ref[pl.ds(c * M, M), :] = v.astype(dtype)


def _from_sub(dst_ref, src_ref, r, accumulate=False):
    M = SEQ // r
    for c in range(r):
        v = src_ref[pl.ds(c * M, M), :]
        idx = slice(None) if r == 1 else pl.ds(c, M, stride=r)
        if accumulate:
            dst_ref[idx, :] = dst_ref[idx, :] + v
        else:
            dst_ref[idx, :] = v


_COL = lambda k: slice(k * HD, (k + 1) * HD)
SCALE = HD ** -0.5


def _qkv_spec(k, bh):
    def index(*ids):
        b, h = bh(*ids)
        return (b, C_ATT // HD + 5 * h + k)

    return pl.BlockSpec((SEQ, HD), index)


def _att_fwd(proj, bias):
    T = proj.shape[0]
    B = T // SEQ

    def body(q0_ref, q1_ref, q2_ref, k_ref, v_ref, bias_ref, o_ref, lse_ref, qp, kp, vp, op, lp, og, lg):
        q_refs = (q0_ref, q1_ref, q2_ref)
        for g, r in enumerate(DILATIONS):
            nb = NBLK_SEQ // r
            _to_sub(qp, q_refs[g], r, bf16)
            _to_sub(kp, k_ref, r, bf16)
            _to_sub(vp, v_ref, r, bf16)
            bias_p = bias_ref[g, :, 0:ATT_BLK]
            bias_c = bias_ref[g, :, ATT_BLK:2 * ATT_BLK]

            def step(j, carry):
                cur = pl.ds(pl.multiple_of(j * ATT_BLK, ATT_BLK), ATT_BLK)
                prv = pl.ds(pl.multiple_of(jnp.maximum(j - 1, 0) * ATT_BLK, ATT_BLK), ATT_BLK)
                has_prev = (j % nb) != 0
                q = qp[cur, :]
                s_c = lax.dot_general(q, kp[cur, :], _DIMS["nt"], preferred_element_type=f32) * SCALE + bias_c
                s_p = lax.dot_general(q, kp[prv, :], _DIMS["nt"], preferred_element_type=f32) * SCALE + bias_p
                s_p = jnp.where(has_prev, s_p, NEG)
                m = jnp.maximum(jnp.max(s_c, axis=-1, keepdims=True), jnp.max(s_p, axis=-1, keepdims=True))
                p_c = jnp.exp(s_c - m)
                p_p = jnp.exp(s_p - m)
                den = jnp.sum(p_c, axis=-1, keepdims=True) + jnp.sum(p_p, axis=-1, keepdims=True)
                o = jnp.dot(p_c.astype(bf16), vp[cur, :], preferred_element_type=f32)
                o = o + jnp.dot(p_p.astype(bf16), vp[prv, :], preferred_element_type=f32)
                op[cur, :] = o / den
                lp[cur, :] = jnp.broadcast_to(m + jnp.log(den), (ATT_BLK, HD))
                return carry

            lax.fori_loop(0, NBLK_SEQ, step, 0, unroll=2)
            _from_sub(og.at[g], op, r)
            _from_sub(lg.at[g], lp, r)
        l0, l1, l2 = lg[0], lg[1], lg[2]
        mx = jnp.maximum(jnp.maximum(l0, l1), l2)
        e0, e1, e2 = jnp.exp(l0 - mx), jnp.exp(l1 - mx), jnp.exp(l2 - mx)
        den = e0 + e1 + e2
        o_ref[...] = (e0 * og[0] + e1 * og[1] + e2 * og[2]) / den
        lse_ref[...] = mx + jnp.log(den)

    return pl.pallas_call(
        body, name="att_fwd", out_shape=(SDS((T, KVH * HD), f32), SDS((KVH, T, HD), f32)), grid=(B, KVH),
        in_specs=[_qkv_spec(k, lambda b, h: (b, h)) for k in range(5)]
                 + [pl.BlockSpec((None, NG, ATT_BLK, 2 * ATT_BLK), lambda b, h: (h, 0, 0, 0))],
        out_specs=(pl.BlockSpec((SEQ, HD), lambda b, h: (b, h)),
                   pl.BlockSpec((None, SEQ, HD), lambda b, h: (h, b, 0))),
        scratch_shapes=[pltpu.VMEM((SEQ, HD), bf16)] * 3 + [pltpu.VMEM((SEQ, HD), f32)] * 2
                       + [pltpu.VMEM((NG, SEQ, HD), f32)] * 2,
        compiler_params=_params(("parallel", "parallel")))(proj, proj, proj, proj, proj, bias)


def _att_bwd(proj, bias, o, lse, do):
    T = proj.shape[0]
    B = T // SEQ

    def body(q0_ref, q1_ref, q2_ref, k_ref, v_ref, bias_ref, o_ref, lse_ref, do_ref, dx_ref, db_ref,
             qp, kp, vp, dop, lp, dlp, dqp, dkp, dvp, dln, nat, dkn, dvn):
        q_refs = (q0_ref, q1_ref, q2_ref)
        first = pl.program_id(1) == 0

        @pl.when(first)
        def _():
            db_ref[...] = jnp.zeros_like(db_ref)

        dln[...] = jnp.broadcast_to(jnp.sum(do_ref[...] * o_ref[...], axis=-1, keepdims=True), (SEQ, HD))
        dkn[...] = jnp.zeros_like(dkn)
        dvn[...] = jnp.zeros_like(dvn)
        for g, r in enumerate(DILATIONS):
            nb = NBLK_SEQ // r
            _to_sub(qp, q_refs[g], r, bf16)
            _to_sub(kp, k_ref, r, bf16)
            _to_sub(vp, v_ref, r, bf16)
            _to_sub(dop, do_ref, r, bf16)
            _to_sub(lp, lse_ref, r, f32)
            _to_sub(dlp, dln, r, f32)
            dkp[...] = jnp.zeros_like(dkp)
            dvp[...] = jnp.zeros_like(dvp)
            bias_p = bias_ref[g, :, 0:ATT_BLK]
            bias_c = bias_ref[g, :, ATT_BLK:2 * ATT_BLK]

            def step(j, carry):
                cur = pl.ds(pl.multiple_of(j * ATT_BLK, ATT_BLK), ATT_BLK)
                prv = pl.ds(pl.multiple_of(jnp.maximum(j - 1, 0) * ATT_BLK, ATT_BLK), ATT_BLK)
                has_prev = (j % nb) != 0
                q, kc, kv, vc, vv, dob = qp[cur, :], kp[cur, :], kp[prv, :], vp[cur, :], vp[prv, :], dop[cur, :]
                lse_j = lp[cur, 0:1]
                dl_j = dlp[cur, 0:1]
                s_c = lax.dot_general(q, kc, _DIMS["nt"], preferred_element_type=f32) * SCALE + bias_c
                s_p = lax.dot_general(q, kv, _DIMS["nt"], preferred_element_type=f32) * SCALE + bias_p
                s_p = jnp.where(has_prev, s_p, NEG)
                p_c = jnp.exp(s_c - lse_j)
                p_p = jnp.exp(s_p - lse_j)
                dp_c = lax.dot_general(dob, vc, _DIMS["nt"], preferred_element_type=f32)
                dp_p = lax.dot_general(dob, vv, _DIMS["nt"], preferred_element_type=f32)
                ds_c = p_c * (dp_c - dl_j)
                ds_p = p_p * (dp_p - dl_j)
                db_ref[g, :, ATT_BLK:2 * ATT_BLK] += ds_c
                db_ref[g, :, 0:ATT_BLK] += ds_p
                ds_cb, ds_pb = ds_c.astype(bf16), ds_p.astype(bf16)
                p_cb, p_pb = p_c.astype(bf16), p_p.astype(bf16)
                dq = jnp.dot(ds_cb, kc, preferred_element_type=f32) + jnp.dot(ds_pb, kv, preferred_element_type=f32)
                dqp[cur, :] = dq * SCALE
                dkp[cur, :] += lax.dot_general(ds_cb, q, _DIMS["tn"], preferred_element_type=f32) * SCALE
                dkp[prv, :] += lax.dot_general(ds_pb, q, _DIMS["tn"], preferred_element_type=f32) * SCALE
                dvp[cur, :] += lax.dot_general(p_cb, dob, _DIMS["tn"], preferred_element_type=f32)
                dvp[prv, :] += lax.dot_general(p_pb, dob, _DIMS["tn"], preferred_element_type=f32)
                return carry

            lax.fori_loop(0, NBLK_SEQ, step, 0, unroll=2)
            _from_sub(nat, dqp, r)
            dx_ref[:, _COL(g)] = nat[...].astype(bf16)
            _from_sub(dkn, dkp, r, accumulate=True)
            _from_sub(dvn, dvp, r, accumulate=True)
        dx_ref[:, _COL(3)] = dkn[...].astype(bf16)
        dx_ref[:, _COL(4)] = dvn[...].astype(bf16)

    blk = lambda: pl.BlockSpec((SEQ, HD), lambda h, b: (b, h))
    bias_spec = lambda: pl.BlockSpec((None, NG, ATT_BLK, 2 * ATT_BLK), lambda h, b: (h, 0, 0, 0))
    return pl.pallas_call(
        body, name="att_bwd",
        out_shape=(SDS((T, KVH * ATT_COLS), bf16), SDS((KVH, NG, ATT_BLK, 2 * ATT_BLK), f32)), grid=(KVH, B),
        in_specs=[_qkv_spec(k, lambda h, b: (b, h)) for k in range(5)]
                 + [bias_spec(), blk(), pl.BlockSpec((None, SEQ, HD), lambda h, b: (h, b, 0)), blk()],
        out_specs=(pl.BlockSpec((SEQ, ATT_COLS), lambda h, b: (b, h)), bias_spec()),
        scratch_shapes=[pltpu.VMEM((SEQ, HD), bf16)] * 4 + [pltpu.VMEM((SEQ, HD), f32)] * 9,
        compiler_params=_params(("parallel", "arbitrary")))(proj, proj, proj, proj, proj, bias, o, lse, do)


MERGE_ROWS, MERGE_COLS = 1024, 256
_G_RNN_BLK = C_GATE // MERGE_COLS
_G_ATT_BLK = (C_GATE + D) // MERGE_COLS


def _merge_specs():
    cols = lambda off: pl.BlockSpec((MERGE_ROWS, MERGE_COLS), lambda i, j: (i, off + j))
    return cols(_G_RNN_BLK), cols(_G_ATT_BLK), cols(0)


def _merge_fwd(proj, pr, pa):
    def body(gr_ref, ga_ref, pr_ref, pa_ref, o_ref):
        o_ref[...] = (_sigmoid(gr_ref[...]) * pr_ref[...].astype(f32)
                      + _sigmoid(ga_ref[...]) * pa_ref[...].astype(f32)).astype(bf16)

    T = proj.shape[0]
    s_gr, s_ga, s0 = _merge_specs()
    return pl.pallas_call(body, name="merge_fwd", out_shape=SDS((T, D), bf16),
                          grid=(T // MERGE_ROWS, D // MERGE_COLS),
                          in_specs=[s_gr, s_ga, s0, s0], out_specs=s0,
                          compiler_params=_params(("parallel", "parallel")))(proj, proj, pr, pa)


def _merge_bwd(proj, pr, pa, dm):
    nj = D // MERGE_COLS

    def body(g_ref, pr_ref, pa_ref, dm_ref, dp_ref, dg_ref):
        dm_ = dm_ref[...].astype(f32)
        s = _sigmoid(g_ref[...])
        p = jnp.where(pl.program_id(1) < nj, pr_ref[...], pa_ref[...]).astype(f32)
        dp_ref[...] = (dm_ * s).astype(bf16)
        dg_ref[...] = (dm_ * p * s * (1.0 - s)).astype(bf16)

    T = proj.shape[0]
    blk = (MERGE_ROWS, MERGE_COLS)
    wrap = pl.BlockSpec(blk, lambda i, j: (i, j % nj))
    out = pl.BlockSpec(blk, lambda i, j: (i, j))
    return pl.pallas_call(
        body, name="merge_bwd", out_shape=(SDS((T, 2 * D), bf16), SDS((T, 2 * D), bf16)),
        grid=(T // MERGE_ROWS, 2 * nj),
        in_specs=[pl.BlockSpec(blk, lambda i, j: (i, _G_RNN_BLK + j)), wrap, wrap, wrap], out_specs=(out, out),
        compiler_params=_params(("parallel", "parallel")))(proj, pr, pa, dm)


FFN_COLS = 256
GELU_C = math.sqrt(2.0 / math.pi)
GELU_A = 0.044715


def _gelu_parts(x):
    t = jnp.tanh(GELU_C * (x + GELU_A * x * x * x))
    return 0.5 * x * (1.0 + t), t


def _ffn_act_fwd(gpre, up, cw, cb):
    def body(g_ref, u_ref, cw_ref, cb_ref, o_ref):
        row = lax.broadcasted_iota(jnp.int32, (SEQ, FFN_COLS), 0)
        gate = _conv_fwd(g_ref[...].astype(f32), cw_ref, cb_ref[...], row)
        o_ref[...] = (_gelu_parts(gate)[0] * u_ref[...].astype(f32)).astype(bf16)

    T = gpre.shape[0]
    blk = lambda: pl.BlockSpec((SEQ, FFN_COLS), lambda b, j: (b, j))
    return pl.pallas_call(
        body, name="ffn_act_fwd", out_shape=SDS((T, FFN_W), bf16), grid=(T // SEQ, FFN_W // FFN_COLS),
        in_specs=[blk(), blk(), pl.BlockSpec((FFN_CONV, FFN_COLS), lambda b, j: (0, j)),
                  pl.BlockSpec((1, FFN_COLS), lambda b, j: (0, j))],
        out_specs=blk(), compiler_params=_params(("parallel", "parallel")))(gpre, up, cw, cb)


def _ffn_act_bwd(gpre, up, cw, cb, dact):
    def body(g_ref, u_ref, cw_ref, cb_ref, da_ref, dg_ref, du_ref, dcw_ref, dcb_ref):
        row = lax.broadcasted_iota(jnp.int32, (SEQ, FFN_COLS), 0)
        gp = g_ref[...].astype(f32)
        gate = _conv_fwd(gp, cw_ref, cb_ref[...], row)
        gel, t = _gelu_parts(gate)
        da = da_ref[...].astype(f32)
        du_ref[...] = (da * gel).astype(bf16)
        dgel = 0.5 * (1.0 + t) + 0.5 * gate * (1.0 - t * t) * (GELU_C * (1.0 + 3.0 * GELU_A * gate * gate))
        dgate = da * u_ref[...].astype(f32) * dgel
        dx, dws, db = _conv_bwd(gp, cw_ref, dgate, row)
        dg_ref[...] = dx.astype(bf16)
        first = pl.program_id(1) == 0

        def acc(ref, val):
            @pl.when(first)
            def _():
                ref[...] = val

            @pl.when(jnp.logical_not(first))
            def _():
                ref[...] += val

        for k in range(FFN_CONV):
            acc(dcw_ref.at[k:k + 1, :], dws[k])
        acc(dcb_ref, db)

    T = gpre.shape[0]
    blk = lambda: pl.BlockSpec((SEQ, FFN_COLS), lambda j, b: (b, j))
    cws = lambda: pl.BlockSpec((FFN_CONV, FFN_COLS), lambda j, b: (0, j))
    cbs = lambda: pl.BlockSpec((1, FFN_COLS), lambda j, b: (0, j))
    return pl.pallas_call(
        body, name="ffn_act_bwd",
        out_shape=(SDS((T, FFN_W), bf16), SDS((T, FFN_W), bf16), SDS((FFN_CONV, FFN_W), f32), SDS((1, FFN_W), f32)),
        grid=(FFN_W // FFN_COLS, T // SEQ),
        in_specs=[blk(), blk(), cws(), cbs(), blk()], out_specs=(blk(), blk(), cws(), cbs()),
        compiler_params=_params(("parallel", "arbitrary")))(gpre, up, cw, cb, dact)


def _coords():
    return lax.axis_index("x"), lax.axis_index("y"), lax.axis_index("c")


def _all_gather(xs, name):
    R, C = xs.shape

    def body(x_ref, out_ref, send_sems, recv_sems, local_sem):
        x, y, c = _coords()
        me, sibling = (x, y, c), (x, y, 1 - c)
        chips = [(1 - x, y), (x, 1 - y), (1 - x, 1 - y)]

        def rows(px, py, pc):
            return out_ref.at[4 * px + 2 * py + pc]

        def copy(k, block, to, src=None):
            return pltpu.make_async_remote_copy(
                src_ref=rows(*block) if src is None else src, dst_ref=rows(*block),
                send_sem=send_sems.at[k], recv_sem=recv_sems.at[k], device_id=to, device_id_type=MESH)

        mine = pltpu.make_async_copy(x_ref, rows(*me), local_sem)
        mine.start()
        first = [copy(0, me, sibling, src=x_ref)]
        first += [copy(1 + j, me, (*chip, c), src=x_ref) for j, chip in enumerate(chips)]
        for cp in first:
            cp.start()
        passed = [copy(4 + j, (*chip, c), sibling) for j, chip in enumerate(chips)]
        for j, chip in enumerate(chips):
            copy(1 + j, (*chip, c), me).wait_recv()
            passed[j].start()
        copy(0, sibling, me).wait_recv()
        for j, chip in enumerate(chips):
            copy(4 + j, (*chip, 1 - c), me).wait_recv()
        for cp in first + passed:
            cp.wait_send()
        mine.wait()

    return pl.pallas_call(
        body, name=name, out_shape=SDS((N_DEV, R, C), xs.dtype), in_specs=[ANY], out_specs=ANY,
        scratch_shapes=[pltpu.SemaphoreType.DMA((7,)), pltpu.SemaphoreType.DMA((7,)), pltpu.SemaphoreType.DMA(())],
    )(xs)


def _sibling_exchange(g8):
    R, C = g8.shape[-2:]

    def body(g_ref, recv_ref, send_sems, recv_sems):
        x, y, c = _coords()
        cps = []
        for k in range(4):
            cps.append(pltpu.make_async_remote_copy(
                src_ref=g_ref.at[2 * k + 1 - c], dst_ref=recv_ref.at[k], send_sem=send_sems.at[k],
                recv_sem=recv_sems.at[k], device_id=(x, y, 1 - c), device_id_type=MESH))
        for cp in cps:
            cp.start()
        for cp in cps:
            cp.wait()

    return pl.pallas_call(
        body, name="rs_sibling", out_shape=SDS((4, R, C), g8.dtype), in_specs=[ANY], out_specs=ANY,
        scratch_shapes=[pltpu.SemaphoreType.DMA((4,)), pltpu.SemaphoreType.DMA((4,))])(g8)


def _chip_exchange(pa):
    R, C = pa.shape[-2:]

    def body(p_ref, recv_ref, send_sems, recv_sems):
        x, y, c = _coords()
        chips = [(1 - x, y), (x, 1 - y), (1 - x, 1 - y)]
        cps = []
        for j, (cx, cy) in enumerate(chips):
            cps.append(pltpu.make_async_remote_copy(
                src_ref=p_ref.at[2 * cx + cy], dst_ref=recv_ref.at[j], send_sem=send_sems.at[j],
                recv_sem=recv_sems.at[j], device_id=(cx, cy, c), device_id_type=MESH))
        for cp in cps:
            cp.start()
        for cp in cps:
            cp.wait()

    return pl.pallas_call(
        body, name="rs_chips", out_shape=SDS((3, R, C), pa.dtype), in_specs=[ANY], out_specs=ANY,
        scratch_shapes=[pltpu.SemaphoreType.DMA((3,)), pltpu.SemaphoreType.DMA((3,))])(pa)


PACK_TILE = 1632
PACK_ROWS = 11 * PACK_TILE


def _pair_sum(g8, recv, c_idx):
    R = g8.shape[-2]

    def body(c_ref, g_ref, r_ref, o_ref):
        o_ref[...] = (g_ref[...] + r_ref[...]).astype(bf16)

    return pl.pallas_call(
        body, name="rs_pair_sum", out_shape=SDS((4, R, LANES), bf16),
        grid_spec=pltpu.PrefetchScalarGridSpec(
            num_scalar_prefetch=1, grid=(4, R // PACK_TILE),
            in_specs=[pl.BlockSpec((None, PACK_TILE, LANES), lambda k, i, c: (2 * k + c[0], i, 0)),
                      pl.BlockSpec((None, PACK_TILE, LANES), lambda k, i, c: (k, i, 0))],
            out_specs=pl.BlockSpec((None, PACK_TILE, LANES), lambda k, i, c: (k, i, 0))),
        compiler_params=_params(("parallel", "parallel")))(c_idx, g8, recv)


def _adamw_math(w, g, m, v):
    m = ADAM_B1 * m + (1.0 - ADAM_B1) * g
    v = ADAM_B2 * v + (1.0 - ADAM_B2) * (g * g)
    m_hat = m / (1.0 - ADAM_B1 ** ADAM_STEP)
    v_hat = v / (1.0 - ADAM_B2 ** ADAM_STEP)
    delta = -ADAM_LR * (m_hat / (jnp.sqrt(v_hat) + ADAM_EPS) + ADAM_WD * w)
    return delta, m, v


def _adamw_sharded(pa, recv, k_idx, w, m, v):
    R = w.shape[0]

    def body(k_ref, p_ref, r_ref, w_ref, m_ref, v_ref, g_ref, d_ref, nm_ref, nv_ref):
        g = p_ref[...].astype(f32)
        for j in range(3):
            g = g + r_ref[j].astype(f32)
        d, nm, nv = _adamw_math(w_ref[...], g, m_ref[...], v_ref[...])
        g_ref[...], d_ref[...], nm_ref[...], nv_ref[...] = g, d, nm, nv

    t = lambda: pl.BlockSpec((PACK_TILE, LANES), lambda i, k: (i, 0))
    return pl.pallas_call(
        body, name="adamw_sharded", out_shape=(SDS((R, LANES), f32),) * 4,
        grid_spec=pltpu.PrefetchScalarGridSpec(
            num_scalar_prefetch=1, grid=(R // PACK_TILE,),
            in_specs=[pl.BlockSpec((None, PACK_TILE, LANES), lambda i, k: (k[0], i, 0)),
                      pl.BlockSpec((3, PACK_TILE, LANES), lambda i, k: (0, i, 0)), t(), t(), t()],
            out_specs=(t(), t(), t(), t())),
        compiler_params=_params(("parallel",)))(k_idx, pa, recv, w, m, v)


SMALL_TILE = 896
SMALL_ROWS = 3 * SMALL_TILE


def _adamw_replicated(parts, w, m, v):
    R = w.shape[0]

    def body(p_ref, w_ref, m_ref, v_ref, g_ref, d_ref, nm_ref, nv_ref):
        g = p_ref[0]
        for j in range(1, N_DEV):
            g = g + p_ref[j]
        d, nm, nv = _adamw_math(w_ref[...], g, m_ref[...], v_ref[...])
        g_ref[...], d_ref[...], nm_ref[...], nv_ref[...] = g, d, nm, nv

    t = lambda: pl.BlockSpec((SMALL_TILE, LANES), lambda i: (i, 0))
    return pl.pallas_call(
        body, name="adamw_replicated", out_shape=(SDS((R, LANES), f32),) * 4, grid=(R // SMALL_TILE,),
        in_specs=[pl.BlockSpec((N_DEV, SMALL_TILE, LANES), lambda i: (0, i, 0)), t(), t(), t()],
        out_specs=(t(), t(), t(), t()), compiler_params=_params(("parallel",)))(parts, w, m, v)


MATS = (("w_in", D, IN_W, "col"), ("w_branch_rnn", RNN_W, D, "row"), ("w_branch_att", KVH * HD, D, "col"),
        ("w_out", D, D, "row"), ("w_ffn_gate", D, FFN_W, "col"), ("w_ffn_up", D, FFN_W, "col"),
        ("w_ffn_down", FFN_W, D, "row"))
MAT_ROWS = tuple(r * c // N_DEV // LANES for _, r, c, _ in MATS)
MAT_ROWS_TOTAL = sum(MAT_ROWS)
CONV_RNN_ROWS, CONV_FFN_ROWS = 8, 16
assert MAT_ROWS_TOTAL + CONV_RNN_ROWS + CONV_FFN_ROWS + 8 == PACK_ROWS

SMALLS = (("rel_bias", (REL_BUCKETS, 12)), ("norm_mix_pre", (1, D)), ("norm_mix_post", (1, D)),
          ("conv_rnn_b", (1, RNN_W)), ("w_rg_a", (1, RNN_BLOCKS, LANES, LANES)), ("b_rg_a", (1, RNN_W)),
          ("w_rg_x", (1, RNN_BLOCKS, LANES, LANES)), ("b_rg_x", (1, RNN_W)), ("lru_lambda", (1, RNN_W)),
          ("norm_ffn_pre", (1, D)), ("norm_ffn_post", (1, D)), ("conv_ffn_b", (1, FFN_W)))


def _rows8(n_elems):
    return -(-n_elems // (8 * LANES)) * 8


def _pad_rows(a2d, rows):
    return jnp.pad(a2d, ((0, rows - a2d.shape[0]), (0, 0)))


def _pack_small(named):
    parts = []
    for name, shape in SMALLS:
        n = int(np.prod(shape))
        flat = named[name].reshape(-1)
        flat = jnp.pad(flat, (0, _rows8(n) * LANES - n))
        parts.append(flat.reshape(-1, LANES))
    out = jnp.concatenate(parts, axis=0)
    return _pad_rows(out, SMALL_ROWS)


def _unpack_small(pack):
    out, r0 = {}, 0
    for name, shape in SMALLS:
        n = int(np.prod(shape))
        rows = _rows8(n)
        out[name] = pack[r0:r0 + rows].reshape(-1)[:n].reshape(shape)
        r0 += rows
    return out


def _pack_shards(named):
    parts = [named[name].reshape(-1, LANES) for name, _, _, _ in MATS]
    parts.append(_pad_rows(named["conv_rnn_w"].reshape(-1, LANES), CONV_RNN_ROWS))
    parts.append(_pad_rows(named["conv_ffn_w"].reshape(-1, LANES), CONV_FFN_ROWS + 8))
    return jnp.concatenate(parts, axis=0)


def _unpack_shards(pack):
    out, r0 = {}, 0
    for (name, r, c, kind), rows in zip(MATS, MAT_ROWS):
        shape = (1, r, c // N_DEV) if kind == "col" else (1, r // N_DEV, c)
        out[name] = pack[r0:r0 + rows].reshape(shape)
        r0 += rows
    out["conv_rnn_w"] = pack[r0:r0 + 5].reshape(1, RNN_CONV, RNN_W // N_DEV)
    r0 += CONV_RNN_ROWS
    out["conv_ffn_w"] = pack[r0:r0 + 9].reshape(1, FFN_CONV, FFN_W // N_DEV)
    return out


def _gathered_to_full(g, r, c, kind):
    if kind == "row":
        return g.reshape(r, c)
    return g.reshape(N_DEV, r, c // N_DEV).transpose(1, 0, 2).reshape(r, c)


def _full_to_blocks(a, kind):
    r, c = a.shape
    if kind == "row":
        return a.reshape(N_DEV, -1, LANES)
    return a.reshape(r, N_DEV, c // N_DEV).transpose(1, 0, 2).reshape(N_DEV, -1, LANES)


def _w_in_to_internal(w):
    K = w.shape[0]
    q = w[:, 1280:2816].reshape(K, NG, KVH, 1, HD).transpose(0, 2, 1, 3, 4).reshape(K, KVH, NG, HD)
    k = w[:, 2816:3328].reshape(K, KVH, 1, HD)
    v = w[:, 3328:3840].reshape(K, KVH, 1, HD)
    att = jnp.concatenate([q, k, v], axis=2).reshape(K, KVH * ATT_COLS)
    return jnp.concatenate([w[:, :1280], att, w[:, 3840:]], axis=1)


def _w_in_from_internal(w):
    K = w.shape[0]
    att = w[:, C_ATT:C_GATE].reshape(K, KVH, 5, HD)
    q = att[:, :, 0:3].transpose(0, 2, 1, 3).reshape(K, NG * KVH * HD)
    k = att[:, :, 3].reshape(K, KVH * HD)
    v = att[:, :, 4].reshape(K, KVH * HD)
    return jnp.concatenate([w[:, :C_ATT], q, k, v, w[:, C_GATE:]], axis=1)


_IN_NAMES = ('x', 'rel_bias', 'norm_mix_pre', 'norm_mix_post', 'w_in', 'conv_rnn_w', 'conv_rnn_b', 'w_rg_a', 'b_rg_a',
             'w_rg_x', 'b_rg_x', 'lru_lambda', 'w_branch_rnn', 'w_branch_att', 'w_out', 'norm_ffn_pre',
             'norm_ffn_post', 'w_ffn_gate', 'w_ffn_up', 'conv_ffn_w', 'conv_ffn_b', 'w_ffn_down')
_WEIGHTS = _IN_NAMES[1:]
_SHARDED = tuple(n for n, _, _, _ in MATS) + ("conv_rnn_w", "conv_ffn_w")


def _step(inp):
    x_idx, y_idx, c_idx = _coords()
    W = {n: inp[n] for n in _WEIGHTS}
    x = inp["x"].reshape(-1, D)
    target = inp["loss_target"].reshape(-1, D)
    T = x.shape[0]

    local = {n: W[n][0] for n in _SHARDED}
    mats16 = jnp.concatenate([local[n].astype(bf16).reshape(-1, LANES) for n, _, _, _ in MATS], axis=0)
    conv32 = jnp.concatenate([_pad_rows(local["conv_rnn_w"].reshape(-1, LANES), CONV_RNN_ROWS),
                              _pad_rows(local["conv_ffn_w"].reshape(-1, LANES), CONV_FFN_ROWS)], axis=0)
    conv_bits = lax.bitcast_convert_type(conv32, bf16).reshape(48, LANES)
    wpack = jnp.concatenate([mats16, conv_bits], axis=0)
    gathered = _all_gather(wpack, "ag_weights")
    full, r0 = {}, 0
    for (name, r, c, kind), rows in zip(MATS, MAT_ROWS):
        full[name] = _gathered_to_full(gathered[:, r0:r0 + rows], r, c, kind)
        r0 += rows
    conv_all = lax.bitcast_convert_type(gathered[:, r0:r0 + 48].reshape(N_DEV, 24, LANES, 2), f32)
    cw_rnn = conv_all[:, 0:5].reshape(N_DEV, RNN_CONV, RNN_W // N_DEV).transpose(1, 0, 2).reshape(RNN_CONV, RNN_W)
    cw_ffn = conv_all[:, CONV_RNN_ROWS:CONV_RNN_ROWS + 9]
    cw_ffn = cw_ffn.reshape(N_DEV, FFN_CONV, FFN_W // N_DEV).transpose(1, 0, 2).reshape(FFN_CONV, FFN_W)
    w_in = _w_in_to_internal(full["w_in"])
    w_brnn, w_batt, w_out = full["w_branch_rnn"], full["w_branch_att"], full["w_out"]
    w_gate, w_up, w_down = full["w_ffn_gate"], full["w_ffn_up"], full["w_ffn_down"]

    wa, wx = W["w_rg_a"][0], W["w_rg_x"][0]
    buckets = jnp.asarray(_bucket_maps())

    hn = _norm_in(x, W["norm_mix_pre"])
    proj = _mm(hn, w_in, "nn", f32, "mm_proj", 512, IN_W // 2, 1024, cols_outer=True)
    h_rnn = _rnn_fwd(proj, cw_rnn, W["conv_rnn_b"], wa, W["b_rg_a"], wx, W["b_rg_x"], W["lru_lambda"])
    bias = _bias_tables(W["rel_bias"], buckets)
    o_att, lse = _att_fwd(proj, bias)
    pr = _mm(h_rnn, w_brnn, "nn", bf16, "mm_pr", 1024, 1024, 1280)
    pa = _mm(o_att, w_batt, "nn", bf16, "mm_pa", 1024, 1024, 512)
    merged = _merge_fwd(proj, pr, pa)
    mix = _mm(merged, w_out, "nn", f32, "mm_mix", 1024, 1024, 1024)
    h1, hn2 = _mid_fwd(x, mix, W["norm_mix_post"], W["norm_ffn_pre"])
    gpre = _mm(hn2, w_gate, "nn", bf16, "mm_gate", 1024, 1024, 1024, cols_outer=True)
    up = _mm(hn2, w_up, "nn", bf16, "mm_up", 1024, 1024, 1024, cols_outer=True)
    act = _ffn_act_fwd(gpre, up, cw_ffn, W["conv_ffn_b"])
    ff = _mm(act, w_down, "nn", f32, "mm_down", 1024, 1024, 1024)
    loss_part, dy, dff, dg_fpost = _final(h1, ff, W["norm_ffn_post"], target)

    dact = _mm(dff, w_down, "nt", bf16, "mm_dact", 1024, 1024, 1024, cols_outer=True)
    dw_down = _mm(act, dff, "tn", f32, "mm_dw_down", 1024, 1024, 2048)
    dgpre, dup, dcw_ffn, dcb_ffn = _ffn_act_bwd(gpre, up, cw_ffn, W["conv_ffn_b"], dact)
    dw_gate = _mm(hn2, dgpre, "tn", f32, "mm_dw_gate", 1024, 1024, 2048)
    dw_up = _mm(hn2, dup, "tn", f32, "mm_dw_up", 1024, 1024, 2048)
    dhn2 = _mm(dgpre, w_gate, "nt", f32, "mm_dhn2_a", 1024, 1024, 1024)
    dhn2 = _mm(dup, w_up, "nt", f32, "mm_dhn2_b", 1024, 1024, 1024, add=dhn2)
    dh1, dmix, dg_fpre, dg_post = _mid_bwd(dy, dhn2, h1, W["norm_ffn_pre"], mix, W["norm_mix_post"])
    dmerged = _mm(dmix, w_out, "nt", bf16, "mm_dmerged", 1024, 1024, 1024)
    dw_out = _mm(merged, dmix, "tn", f32, "mm_dw_out", 1024, 1024, 2048)
    dprpa, dgates = _merge_bwd(proj, pr, pa, dmerged)
    dpr, dpa = dprpa[:, :D], dprpa[:, D:]
    dh_rnn = _mm(dpr, w_brnn, "nt", bf16, "mm_dh_rnn", 1024, 1280, 1024)
    dw_brnn = _mm(h_rnn, dpr, "tn", f32, "mm_dw_brnn", 1280, 1024, 1024)
    do_att = _mm(dpa, w_batt, "nt", f32, "mm_do_att", 1024, 512, 1024)
    dw_batt = _mm(o_att, dpa, "tn", f32, "mm_dw_batt", 512, 1024, 2048)
    dqkv, dbias = _att_bwd(proj, bias, o_att, lse, do_att)
    drel = _bias_grad(dbias, buckets)
    dxr, dcw_rnn, dcb_rnn, dwa, dba, dwx, dbx, dlam = _rnn_bwd(
        proj, h_rnn, dh_rnn, cw_rnn, W["conv_rnn_b"], wa, W["b_rg_a"], wx, W["b_rg_x"], W["lru_lambda"])
    dw_in = jnp.concatenate([
        _mm(hn, dxr, "tn", f32, "mm_dw_in_r", 1024, 1280, 1024),
        _mm(hn, dqkv, "tn", f32, "mm_dw_in_a", 1024, 1280, 1024),
        _mm(hn, dgates, "tn", f32, "mm_dw_in_g", 1024, 1024, 2048)], axis=1)
    dhn = _mm(dxr, w_in[:, :C_ATT], "nt", f32, "mm_dhn_r", 1024, 1024, 1280)
    dhn = _mm(dqkv, w_in[:, C_ATT:C_GATE], "nt", f32, "mm_dhn_a", 1024, 1024, 1280, add=dhn)
    dhn = _mm(dgates, w_in[:, C_GATE:], "nt", f32, "mm_dhn_g", 1024, 1024, 1024, add=dhn)
    dx, dg_pre = _in_bwd(dh1, dhn, x, W["norm_mix_pre"])

    dfull = {"w_in": _w_in_from_internal(dw_in), "w_branch_rnn": dw_brnn, "w_branch_att": dw_batt, "w_out": dw_out,
             "w_ffn_gate": dw_gate, "w_ffn_up": dw_up, "w_ffn_down": dw_down}
    blocks = [_full_to_blocks(dfull[n], kind) for n, _, _, kind in MATS]
    conv_r = _full_to_blocks(dcw_rnn, "col")
    conv_f = _full_to_blocks(dcw_ffn, "col")
    blocks.append(jnp.pad(conv_r, ((0, 0), (0, CONV_RNN_ROWS - 5), (0, 0))))
    blocks.append(jnp.pad(conv_f, ((0, 0), (0, CONV_FFN_ROWS + 8 - 9), (0, 0))))
    gpack = jnp.concatenate(blocks, axis=1)
    from_sibling = _sibling_exchange(gpack)
    c_arr = jnp.reshape(c_idx, (1,)).astype(jnp.int32)
    pair = _pair_sum(gpack, from_sibling, c_arr)
    from_chips = _chip_exchange(pair)
    k_arr = jnp.reshape(2 * x_idx + y_idx, (1,)).astype(jnp.int32)
    wl = {n: inp[n][0] for n in _SHARDED}
    ml = {n: inp["m_" + n][0] for n in _SHARDED}
    vl = {n: inp["v_" + n][0] for n in _SHARDED}
    sh = [_unpack_shards(p) for p in _adamw_sharded(pair, from_chips, k_arr, _pack_shards(wl), _pack_shards(ml),
                                                    _pack_shards(vl))]

    gsmall = {"rel_bias": drel[:, :12], "norm_mix_pre": dg_pre, "norm_mix_post": dg_post, "conv_rnn_b": dcb_rnn,
              "w_rg_a": dwa[None], "b_rg_a": dba, "w_rg_x": dwx[None], "b_rg_x": dbx, "lru_lambda": dlam,
              "norm_ffn_pre": dg_fpre, "norm_ffn_post": dg_fpost, "conv_ffn_b": dcb_ffn}
    parts = _all_gather(_pack_small(gsmall), "ag_small_grads")
    sm = [_unpack_small(p) for p in _adamw_replicated(
        parts, _pack_small({n: inp[n] for n, _ in SMALLS}), _pack_small({n: inp["m_" + n] for n, _ in SMALLS}),
        _pack_small({n: inp["v_" + n] for n, _ in SMALLS}))]

    loss = lax.psum(loss_part[0, 0], ("x", "y", "c"))
    outs = [loss, dx.reshape(inp["x"].shape)]
    for k in range(4):
        for n in _WEIGHTS:
            outs.append(sh[k][n] if n in _SHARDED else sm[k][n])
    return tuple(outs)


def kernel(x, rel_bias, norm_mix_pre, norm_mix_post, w_in, conv_rnn_w, conv_rnn_b, w_rg_a, b_rg_a, w_rg_x, b_rg_x, lru_lambda, w_branch_rnn, w_branch_att, w_out, norm_ffn_pre, norm_ffn_post, w_ffn_gate, w_ffn_up, conv_ffn_w, conv_ffn_b, w_ffn_down, loss_target, m_rel_bias, m_norm_mix_pre, m_norm_mix_post, m_w_in, m_conv_rnn_w, m_conv_rnn_b, m_w_rg_a, m_b_rg_a, m_w_rg_x, m_b_rg_x, m_lru_lambda, m_w_branch_rnn, m_w_branch_att, m_w_out, m_norm_ffn_pre, m_norm_ffn_post, m_w_ffn_gate, m_w_ffn_up, m_conv_ffn_w, m_conv_ffn_b, m_w_ffn_down, v_rel_bias, v_norm_mix_pre, v_norm_mix_post, v_w_in, v_conv_rnn_w, v_conv_rnn_b, v_w_rg_a, v_b_rg_a, v_w_rg_x, v_b_rg_x, v_lru_lambda, v_w_branch_rnn, v_w_branch_att, v_w_out, v_norm_ffn_pre, v_norm_ffn_post, v_w_ffn_gate, v_w_ffn_up, v_conv_ffn_w, v_conv_ffn_b, v_w_ffn_down):
    vals = locals()
    names = list(_IN_NAMES) + ["loss_target"] + ["m_" + n for n in _WEIGHTS] + ["v_" + n for n in _WEIGHTS]
    return _step({n: vals[n] for n in names})
```

```python
import functools
import math

import numpy as np
import jax
import jax.numpy as jnp
from jax import lax
from jax.experimental import pallas as pl
from jax.experimental.pallas import tpu as pltpu

f32, bf16 = jnp.float32, jnp.bfloat16
SDS = jax.ShapeDtypeStruct
MESH = pl.DeviceIdType.MESH
ANY = pl.BlockSpec(memory_space=pl.ANY)

D = 1024
SEQ = 2048
RNN_W = 1280
RNN_BLOCKS = 10
LANES = 128
SUBLANES = 8
RNN_CONV = 4
LRU_C = 8.0
HD = 128
KVH = 4
DILATIONS = (1, 4, 16)
NG = 3
ATT_BLK = 128
NBLK_SEQ = SEQ // ATT_BLK
REL_BUCKETS = 32
REL_MAX_DIST = 2048
FFN_W = 3072
FFN_CONV = 3
EPS = 1e-6
IN_W = 5888
ATT_COLS = 5 * HD
C_ATT = RNN_W
C_GATE = RNN_W + KVH * ATT_COLS
NEG = -1e30

ADAM_LR, ADAM_B1, ADAM_B2, ADAM_EPS, ADAM_WD, ADAM_STEP = 0.001, 0.9, 0.999, 1e-08, 0.01, 10

VMEM_LIMIT_BYTES = 56 * 1024 * 1024
N_DEV = 8


def _params(sem=None):
    return pltpu.CompilerParams(dimension_semantics=sem, vmem_limit_bytes=VMEM_LIMIT_BYTES)


def _sigmoid(x):
    return 1.0 / (1.0 + jnp.exp(-x))


class _Comm:
    def __init__(self, inputs, out_shapes, sem_shapes, start, finish):
        self.inputs, self.out_shapes, self.sem_shapes = tuple(inputs), tuple(out_shapes), list(sem_shapes)
        self.start, self.finish = start, finish


def _call(body, args, *, name, grid, in_specs, out_specs, out_shape, scratch_shapes=(), semantics, comm=None):
    if comm is None:
        return pl.pallas_call(body, name=name, grid=grid, in_specs=list(in_specs), out_specs=tuple(out_specs),
                              out_shape=tuple(out_shape), scratch_shapes=list(scratch_shapes),
                              compiler_params=_params(semantics))(*args), ()
    n_in, n_out, n_scr = len(in_specs), len(out_shape), len(scratch_shapes)
    c_in, c_out = len(comm.inputs), len(comm.out_shapes)

    def fused(*refs):
        ins, refs = refs[:n_in], refs[n_in:]
        cin, refs = refs[:c_in], refs[c_in:]
        outs, refs = refs[:n_out], refs[n_out:]
        cout, refs = refs[:c_out], refs[c_out:]
        scr, csem = refs[:n_scr], refs[n_scr:]
        first = functools.reduce(jnp.logical_and, [pl.program_id(d) == 0 for d in range(len(grid))])
        last = functools.reduce(jnp.logical_and, [pl.program_id(d) == grid[d] - 1 for d in range(len(grid))])

        @pl.when(first)
        def _():
            comm.start(cin, cout, csem)

        body(*ins, *outs, *scr)

        @pl.when(last)
        def _():
            comm.finish(cin, cout, csem)

    res = pl.pallas_call(
        fused, name=name, grid=grid, in_specs=list(in_specs) + [ANY] * c_in,
        out_specs=tuple(out_specs) + (ANY,) * c_out, out_shape=tuple(out_shape) + comm.out_shapes,
        scratch_shapes=list(scratch_shapes) + comm.sem_shapes,
        compiler_params=_params(("arbitrary",) * len(grid)))(*args, *comm.inputs)
    return res[:n_out], res[n_out:]


_DIMS = {"nn": (((1,), (0,)), ((), ())), "nt": (((1,), (1,)), ((), ())), "tn": (((0,), (0,)), ((), ()))}


def _mm(a, b, mode, out_dtype, name, tm, tn, tk, add=None, cols_outer=False, b_blocked=False, out_blocked=False):
    if b_blocked:
        nb_, r_, c_ = b.shape
        b_shape = (r_, nb_ * c_)
        assert (tn if mode == "nn" else tk) == c_ and mode in ("nn", "nt"), name
    else:
        b_shape = b.shape
    if mode == "nn":
        (M, K), (K2, N) = a.shape, b_shape
    elif mode == "nt":
        (M, K), (N, K2) = a.shape, b_shape
    else:
        (K, M), (K2, N) = a.shape, b_shape
    assert K == K2 and M % tm == 0 and N % tn == 0 and K % tk == 0, (name, a.shape, b.shape)
    assert not out_blocked or (mode == "tn" and add is None), name
    nk = K // tk
    has_add = add is not None

    def body(*refs):
        if has_add:
            a_ref, b_ref, c_ref, o_ref = refs[:4]
        else:
            a_ref, b_ref, o_ref = refs[:3]
        part = lax.dot_general(a_ref[...].astype(bf16), b_ref[...].astype(bf16), _DIMS[mode],
                               preferred_element_type=f32)

        def finish(acc):
            if has_add:
                acc = acc + c_ref[...]
            o_ref[...] = acc.astype(o_ref.dtype)

        if nk == 1:
            finish(part)
        else:
            acc_ref = refs[-1]
            k = pl.program_id(2)

            @pl.when(k == 0)
            def _():
                acc_ref[...] = part

            @pl.when(k > 0)
            def _():
                acc_ref[...] += part

            @pl.when(k == nk - 1)
            def _():
                finish(acc_ref[...])

    def ij(f):
        return (lambda j, i, k: f(i, j, k)) if cols_outer else f

    if mode == "tn":
        a_spec = pl.BlockSpec((tk, tm), ij(lambda i, j, k: (k, i)))
    else:
        a_spec = pl.BlockSpec((tm, tk), ij(lambda i, j, k: (i, k)))
    if mode == "nt" and b_blocked:
        b_spec = pl.BlockSpec((None, tn, tk), ij(lambda i, j, k: (k, j, 0)))
    elif mode == "nt":
        b_spec = pl.BlockSpec((tn, tk), ij(lambda i, j, k: (j, k)))
    elif b_blocked:
        b_spec = pl.BlockSpec((None, tk, tn), ij(lambda i, j, k: (j, k, 0)))
    else:
        b_spec = pl.BlockSpec((tk, tn), ij(lambda i, j, k: (k, j)))
    if out_blocked:
        o_spec = pl.BlockSpec((None, tm, tn), ij(lambda i, j, k: (j, i, 0)))
        out_shape = SDS((N // tn, M, tn), out_dtype)
    else:
        o_spec = pl.BlockSpec((tm, tn), ij(lambda i, j, k: (i, j)))
        out_shape = SDS((M, N), out_dtype)
    in_specs = [a_spec, b_spec] + ([o_spec] if has_add else [])
    args = (a, b) + ((add,) if has_add else ())
    grid = (N // tn, M // tm, nk) if cols_outer else (M // tm, N // tn, nk)
    return pl.pallas_call(
        body, name=name, out_shape=out_shape, grid=grid,
        in_specs=in_specs, out_specs=o_spec,
        scratch_shapes=[pltpu.VMEM((tm, tn), f32)] if nk > 1 else [],
        compiler_params=_params(("parallel", "parallel", "arbitrary")),
    )(*args)


ROW_TILE = 512


def _rms_fwd(x, g):
    r = lax.rsqrt(jnp.mean(x * x, axis=-1, keepdims=True) + EPS)
    return x * r * g


def _rms_bwd(x, g, dy):
    r = lax.rsqrt(jnp.mean(x * x, axis=-1, keepdims=True) + EPS)
    xh = x * r
    dxh = dy * g
    dx = r * (dxh - xh * jnp.mean(dxh * xh, axis=-1, keepdims=True))
    return dx, jnp.sum(dy * xh, axis=0, keepdims=True)


def _acc_out(ref, val):
    @pl.when(pl.program_id(0) == 0)
    def _():
        ref[...] = val

    @pl.when(pl.program_id(0) > 0)
    def _():
        ref[...] += val


def _row_spec(width=D):
    return pl.BlockSpec((ROW_TILE, width), lambda i: (i, 0))


def _vec_spec(width=D):
    return pl.BlockSpec((1, width), lambda i: (0, 0))


def _norm_in(x, g):
    def body(x_ref, g_ref, o_ref):
        o_ref[...] = _rms_fwd(x_ref[...], g_ref[...]).astype(bf16)

    T = x.shape[0]
    return pl.pallas_call(body, name="norm_in", out_shape=SDS((T, D), bf16), grid=(T // ROW_TILE,),
                          in_specs=[_row_spec(), _vec_spec()], out_specs=_row_spec(),
                          compiler_params=_params(("parallel",)))(x, g)


def _mid_fwd(x, mix, g_post, g_fpre):
    def body(x_ref, mix_ref, gp_ref, gf_ref, h1_ref, hn2_ref):
        h1 = x_ref[...] + _rms_fwd(mix_ref[...], gp_ref[...])
        h1_ref[...] = h1
        hn2_ref[...] = _rms_fwd(h1, gf_ref[...]).astype(bf16)

    T = x.shape[0]
    return pl.pallas_call(body, name="mid_fwd", out_shape=(SDS((T, D), f32), SDS((T, D), bf16)),
                          grid=(T // ROW_TILE,),
                          in_specs=[_row_spec(), _row_spec(), _vec_spec(), _vec_spec()],
                          out_specs=(_row_spec(), _row_spec()),
                          compiler_params=_params(("parallel",)))(x, mix, g_post, g_fpre)


def _final(h1, ff, g_fpost, target):
    def body(h1_ref, ff_ref, g_ref, t_ref, loss_ref, dy_ref, dff_ref, dg_ref):
        ff = ff_ref[...]
        g = g_ref[...]
        e = h1_ref[...] + _rms_fwd(ff, g) - t_ref[...]
        part = jnp.sum(jnp.sum(e * e, axis=1, keepdims=True), axis=0, keepdims=True) * (0.5 / D)
        dy = e * (1.0 / D)
        dy_ref[...] = dy
        dff, dg = _rms_bwd(ff, g, dy)
        dff_ref[...] = dff.astype(bf16)
        _acc_out(loss_ref, part)
        _acc_out(dg_ref, dg)

    T = h1.shape[0]
    return pl.pallas_call(
        body, name="final", out_shape=(SDS((1, 1), f32), SDS((T, D), f32), SDS((T, D), bf16), SDS((1, D), f32)),
        grid=(T // ROW_TILE,),
        in_specs=[_row_spec(), _row_spec(), _vec_spec(), _row_spec()],
        out_specs=(pl.BlockSpec((1, 1), lambda i: (0, 0)), _row_spec(), _row_spec(), _vec_spec()),
        compiler_params=_params(("arbitrary",)))(h1, ff, g_fpost, target)


def _mid_bwd(dy, dhn2, h1, g_fpre, mix, g_post):
    def body(dy_ref, dhn2_ref, h1_ref, gf_ref, mix_ref, gp_ref, dh1_ref, dmix_ref, dgf_ref, dgp_ref):
        d1, dgf = _rms_bwd(h1_ref[...], gf_ref[...], dhn2_ref[...])
        dh1 = dy_ref[...] + d1
        dh1_ref[...] = dh1
        dmix, dgp = _rms_bwd(mix_ref[...], gp_ref[...], dh1)
        dmix_ref[...] = dmix.astype(bf16)
        _acc_out(dgf_ref, dgf)
        _acc_out(dgp_ref, dgp)

    T = dy.shape[0]
    return pl.pallas_call(
        body, name="mid_bwd", out_shape=(SDS((T, D), f32), SDS((T, D), bf16), SDS((1, D), f32), SDS((1, D), f32)),
        grid=(T // ROW_TILE,),
        in_specs=[_row_spec(), _row_spec(), _row_spec(), _vec_spec(), _row_spec(), _vec_spec()],
        out_specs=(_row_spec(), _row_spec(), _vec_spec(), _vec_spec()),
        compiler_params=_params(("arbitrary",)))(dy, dhn2, h1, g_fpre, mix, g_post)


def _in_bwd(dh1, dhn, x, g_pre, comm=None):
    def body(dh1_ref, dhn_ref, x_ref, g_ref, dx_ref, dg_ref):
        d, dg = _rms_bwd(x_ref[...], g_ref[...], dhn_ref[...])
        dx_ref[...] = dh1_ref[...] + d
        _acc_out(dg_ref, dg)

    T = x.shape[0]
    return _call(
        body, (dh1, dhn, x, g_pre), name="in_bwd", out_shape=(SDS((T, D), f32), SDS((1, D), f32)),
        grid=(T // ROW_TILE,), in_specs=[_row_spec(), _row_spec(), _row_spec(), _vec_spec()],
        out_specs=(_row_spec(), _vec_spec()), semantics=("arbitrary",), comm=comm)


def _shift_dn(x, d, row):
    if d == 0:
        return x
    return jnp.where(row >= d, pltpu.roll(x, d, 0), 0.0)


def _shift_up(x, d, row):
    if d == 0:
        return x
    n = x.shape[0]
    return jnp.where(row < n - d, pltpu.roll(x, n - d, 0), 0.0)


def _conv_fwd(x, w_ref, b, row):
    K = w_ref.shape[0]
    y = b
    for k in range(K):
        y = y + w_ref[k:k + 1, :] * _shift_dn(x, K - 1 - k, row)
    return y


def _conv_bwd(x, w_ref, dy, row):
    K = w_ref.shape[0]
    dx = jnp.zeros_like(dy)
    dws = []
    for k in range(K):
        dx = dx + w_ref[k:k + 1, :] * _shift_up(dy, K - 1 - k, row)
        dws.append(jnp.sum(dy * _shift_dn(x, K - 1 - k, row), axis=0, keepdims=True))
    return dx, dws, jnp.sum(dy, axis=0, keepdims=True)


def _scan_fwd(a, u, row):
    n = a.shape[0]
    d = 1
    while d < n:
        last = 2 * d >= n
        if d < SUBLANES:
            u = u + a * jnp.where(row >= d, pltpu.roll(u, d, 0), 0.0)
            if not last:
                a = a * jnp.where(row >= d, pltpu.roll(a, d, 0), 1.0)
        else:
            u = jnp.concatenate([u[:d], u[d:] + a[d:] * u[:n - d]], axis=0)
            if not last:
                a = jnp.concatenate([a[:d], a[d:] * a[:n - d]], axis=0)
        d *= 2
    return u


def _scan_bwd(b, u, row):
    n = b.shape[0]
    d = 1
    while d < n:
        last = 2 * d >= n
        if d < SUBLANES:
            u = u + b * jnp.where(row < n - d, pltpu.roll(u, n - d, 0), 0.0)
            if not last:
                b = b * jnp.where(row < n - d, pltpu.roll(b, n - d, 0), 1.0)
        else:
            u = jnp.concatenate([u[:n - d] + b[:n - d] * u[d:], u[n - d:]], axis=0)
            if not last:
                b = jnp.concatenate([b[:n - d] * b[d:], b[n - d:]], axis=0)
        d *= 2
    return u


def _neg_expm1(z):
    series = -z * (1.0 + z * (0.5 + z * (1.0 / 6.0 + z * (1.0 / 24.0 + z * (1.0 / 120.0)))))
    return jnp.where(z > -0.1, series, 1.0 - jnp.exp(z))


def _rnn_gates(xr, cw_ref, cb, wa, ba, wx, bx, lam, row):
    xc = _conv_fwd(xr, cw_ref, cb, row)
    xcb = xc.astype(bf16)
    r = _sigmoid(jnp.dot(xcb, wa.astype(bf16), preferred_element_type=f32) + ba)
    i = _sigmoid(jnp.dot(xcb, wx.astype(bf16), preferred_element_type=f32) + bx)
    z = -lam
    sp = jnp.maximum(z, 0.0) + jnp.log(1.0 + jnp.exp(-jnp.abs(z)))
    log_a = (-LRU_C * sp) * r
    a = jnp.exp(log_a)
    s = jnp.sqrt(_neg_expm1(2.0 * log_a))
    return xc, xcb, r, i, sp, a, s


def _rnn_specs(B):
    blk = lambda: pl.BlockSpec((SEQ, LANES), lambda b, n: (b, n))
    return dict(
        act=blk,
        convw=pl.BlockSpec((RNN_CONV, LANES), lambda b, n: (0, n)),
        vec=lambda: pl.BlockSpec((1, LANES), lambda b, n: (0, n)),
        gate=lambda: pl.BlockSpec((None, LANES, LANES), lambda b, n: (n, 0, 0)),
    )


def _rnn_fwd(proj, cw, cb, wa, ba, wx, bx, lam):
    T = proj.shape[0]
    B = T // SEQ

    def body(x_ref, cw_ref, cb_ref, wa_ref, ba_ref, wx_ref, bx_ref, lam_ref, h_ref):
        row = lax.broadcasted_iota(jnp.int32, (SEQ, LANES), 0)
        xc, _, r, i, sp, a, s = _rnn_gates(x_ref[...], cw_ref, cb_ref[...], wa_ref[...], ba_ref[...],
                                          wx_ref[...], bx_ref[...], lam_ref[...], row)
        h_ref[...] = _scan_fwd(a, s * (i * xc), row)

    sp_ = _rnn_specs(B)
    return pl.pallas_call(
        body, name="rnn_fwd", out_shape=SDS((T, RNN_W), f32), grid=(B, RNN_BLOCKS),
        in_specs=[sp_["act"](), sp_["convw"], sp_["vec"](), sp_["gate"](), sp_["vec"](), sp_["gate"](),
                  sp_["vec"](), sp_["vec"]()],
        out_specs=sp_["act"](),
        compiler_params=_params(("parallel", "parallel")))(proj, cw, cb, wa, ba, wx, bx, lam)


def _rnn_bwd(proj, h, dh, cw, cb, wa, ba, wx, bx, lam, comm=None):
    T = proj.shape[0]
    B = T // SEQ

    def body(x_ref, h_ref, dh_ref, cw_ref, cb_ref, wa_ref, ba_ref, wx_ref, bx_ref, lam_ref,
             dx_ref, dcw_ref, dcb_ref, dwa_ref, dba_ref, dwx_ref, dbx_ref, dlam_ref):
        row = lax.broadcasted_iota(jnp.int32, (SEQ, LANES), 0)
        xr = x_ref[...]
        wa, wx, lam = wa_ref[...], wx_ref[...], lam_ref[...]
        xc, xcb, r, i, sp, a, s = _rnn_gates(xr, cw_ref, cb_ref[...], wa, ba_ref[...], wx, bx_ref[...], lam, row)
        hprev = _shift_dn(h_ref[...], 1, row)
        g = _scan_bwd(_shift_up(a, 1, row), dh_ref[...].astype(f32), row)
        da = g * hprev
        ds = g * (i * xc)
        di = g * (s * xc)
        dxc = g * (s * i)
        dla = da * a - ds * (a * a) / s
        dr = dla * (-LRU_C * sp)
        dsp = jnp.sum(dla * (-LRU_C * r), axis=0, keepdims=True)
        dlam = -dsp * _sigmoid(-lam)
        dga = dr * r * (1.0 - r)
        dgx = di * i * (1.0 - i)
        dgab, dgxb = dga.astype(bf16), dgx.astype(bf16)
        dwa = lax.dot_general(xcb, dgab, _DIMS["tn"], preferred_element_type=f32)
        dwx = lax.dot_general(xcb, dgxb, _DIMS["tn"], preferred_element_type=f32)
        dxc = dxc + lax.dot_general(dgab, wa.astype(bf16), _DIMS["nt"], preferred_element_type=f32)
        dxc = dxc + lax.dot_general(dgxb, wx.astype(bf16), _DIMS["nt"], preferred_element_type=f32)
        dx, dws, db = _conv_bwd(xr, cw_ref, dxc, row)
        dx_ref[...] = dx.astype(bf16)
        first = pl.program_id(1) == 0

        def acc(ref, val):
            @pl.when(first)
            def _():
                ref[...] = val

            @pl.when(jnp.logical_not(first))
            def _():
                ref[...] += val

        for k in range(RNN_CONV):
            acc(dcw_ref.at[k:k + 1, :], dws[k])
        acc(dcb_ref, db)
        acc(dwa_ref, dwa)
        acc(dba_ref, jnp.sum(dga, axis=0, keepdims=True))
        acc(dwx_ref, dwx)
        acc(dbx_ref, jnp.sum(dgx, axis=0, keepdims=True))
        acc(dlam_ref, dlam)

    blk = lambda: pl.BlockSpec((SEQ, LANES), lambda n, b: (b, n))
    convw = lambda: pl.BlockSpec((RNN_CONV, LANES), lambda n, b: (0, n))
    vec = lambda: pl.BlockSpec((1, LANES), lambda n, b: (0, n))
    gate = lambda: pl.BlockSpec((None, LANES, LANES), lambda n, b: (n, 0, 0))
    vshape = SDS((1, RNN_W), f32)
    gshape = SDS((RNN_BLOCKS, LANES, LANES), f32)
    return _call(
        body, (proj, h, dh, cw, cb, wa, ba, wx, bx, lam), name="rnn_bwd",
        out_shape=(SDS((T, RNN_W), bf16), SDS((RNN_CONV, RNN_W), f32), vshape, gshape, vshape, gshape, vshape, vshape),
        grid=(RNN_BLOCKS, B),
        in_specs=[blk(), blk(), blk(), convw(), vec(), gate(), vec(), gate(), vec(), vec()],
        out_specs=(blk(), convw(), vec(), gate(), vec(), gate(), vec(), vec()),
        semantics=("parallel", "arbitrary"), comm=comm)


def _t5_bucket(dist):
    max_exact = REL_BUCKETS // 2
    d = np.maximum(dist, 1).astype(np.float32)
    large = max_exact + np.log(d / max_exact) / math.log(REL_MAX_DIST / max_exact) * (REL_BUCKETS - max_exact)
    large = np.minimum(large.astype(np.int32), REL_BUCKETS - 1)
    return np.where(dist < max_exact, dist, large).astype(np.int32)


def _bucket_maps():
    qi = np.arange(ATT_BLK)[:, None]
    kj = np.arange(2 * ATT_BLK)[None, :]
    delta = ATT_BLK + qi - kj
    valid = (delta >= 0) & (delta <= ATT_BLK)
    maps = [np.where(valid, _t5_bucket(np.maximum(delta, 0) * r), -1) for r in DILATIONS]
    return np.stack(maps).astype(np.int32)


def _bias_tables(rel_bias, buckets):
    def body(rb_ref, bk_ref, o_ref):
        for g in range(NG):
            bk = bk_ref[g]
            for h in range(KVH):
                acc = jnp.full(bk.shape, NEG, f32)
                for b in range(REL_BUCKETS):
                    acc = jnp.where(bk == b, rb_ref[b, g * KVH + h], acc)
                o_ref[h, g] = acc

    return pl.pallas_call(
        body, name="bias_tables", out_shape=SDS((KVH, NG, ATT_BLK, 2 * ATT_BLK), f32),
        in_specs=[pl.BlockSpec(memory_space=pltpu.SMEM), pl.BlockSpec(memory_space=pltpu.VMEM)],
        out_specs=pl.BlockSpec(memory_space=pltpu.VMEM), compiler_params=_params())(rel_bias, buckets)


def _bias_grad(dbias, buckets):
    def body(db_ref, bk_ref, o_ref):
        rr = lax.broadcasted_iota(jnp.int32, (REL_BUCKETS, NG * KVH), 0)
        cc = lax.broadcasted_iota(jnp.int32, (REL_BUCKETS, NG * KVH), 1)
        out = jnp.zeros((REL_BUCKETS, NG * KVH), f32)
        for g in range(NG):
            bk = bk_ref[g]
            for h in range(KVH):
                d = db_ref[h, g]
                for b in range(REL_BUCKETS):
                    m = jnp.where(bk == b, d, 0.0)
                    s = jnp.sum(jnp.sum(m, axis=1, keepdims=True), axis=0, keepdims=True)
                    out = jnp.where((rr == b) & (cc == g * KVH + h), s, out)
        o_ref[...] = out

    return pl.pallas_call(body, name="bias_grad", out_shape=SDS((REL_BUCKETS, NG * KVH), f32),
                          compiler_params=_params())(dbias, buckets)


def _to_sub(dst_ref, src_ref, r, dtype):
    M = SEQ // r
    for c in range(r):
        if r == 1:
            v = src_ref[...]
        else:
            v = src_ref[pl.ds(c, M, stride=r), :]
        dst_ref[pl.ds(c * M, M), :] = v.astype(dtype)


def _from_sub(dst_ref, src_ref, r, accumulate=False):
    M = SEQ // r
    for c in range(r):
        v = src_ref[pl.ds(c * M, M), :]
        idx = slice(None) if r == 1 else pl.ds(c, M, stride=r)
        if accumulate:
            dst_ref[idx, :] = dst_ref[idx, :] + v
        else:
            dst_ref[idx, :] = v


_COL = lambda k: slice(k * HD, (k + 1) * HD)
SCALE = HD ** -0.5


def _qkv_spec(k, bh):
    def index(*ids):
        b, h = bh(*ids)
        return (b, C_ATT // HD + 5 * h + k)

    return pl.BlockSpec((SEQ, HD), index)


def _att_fwd(proj, bias, comm=None):
    T = proj.shape[0]
    B = T // SEQ

    def body(q0_ref, q1_ref, q2_ref, k_ref, v_ref, bias_ref, o_ref, lse_ref, qp, kp, vp, op, lp, og, lg):
        q_refs = (q0_ref, q1_ref, q2_ref)
        for g, r in enumerate(DILATIONS):
            nb = NBLK_SEQ // r
            _to_sub(qp, q_refs[g], r, bf16)
            _to_sub(kp, k_ref, r, bf16)
            _to_sub(vp, v_ref, r, bf16)
            bias_p = bias_ref[g, :, 0:ATT_BLK]
            bias_c = bias_ref[g, :, ATT_BLK:2 * ATT_BLK]

            def step(j, carry):
                cur = pl.ds(pl.multiple_of(j * ATT_BLK, ATT_BLK), ATT_BLK)
                prv = pl.ds(pl.multiple_of(jnp.maximum(j - 1, 0) * ATT_BLK, ATT_BLK), ATT_BLK)
                has_prev = (j % nb) != 0
                q = qp[cur, :]
                s_c = lax.dot_general(q, kp[cur, :], _DIMS["nt"], preferred_element_type=f32) * SCALE + bias_c
                s_p = lax.dot_general(q, kp[prv, :], _DIMS["nt"], preferred_element_type=f32) * SCALE + bias_p
                s_p = jnp.where(has_prev, s_p, NEG)
                m = jnp.maximum(jnp.max(s_c, axis=-1, keepdims=True), jnp.max(s_p, axis=-1, keepdims=True))
                p_c = jnp.exp(s_c - m)
                p_p = jnp.exp(s_p - m)
                den = jnp.sum(p_c, axis=-1, keepdims=True) + jnp.sum(p_p, axis=-1, keepdims=True)
                o = jnp.dot(p_c.astype(bf16), vp[cur, :], preferred_element_type=f32)
                o = o + jnp.dot(p_p.astype(bf16), vp[prv, :], preferred_element_type=f32)
                op[cur, :] = o / den
                lp[cur, :] = jnp.broadcast_to(m + jnp.log(den), (ATT_BLK, HD))
                return carry

            lax.fori_loop(0, NBLK_SEQ, step, 0, unroll=2)
            _from_sub(og.at[g], op, r)
            _from_sub(lg.at[g], lp, r)
        l0, l1, l2 = lg[0], lg[1], lg[2]
        mx = jnp.maximum(jnp.maximum(l0, l1), l2)
        e0, e1, e2 = jnp.exp(l0 - mx), jnp.exp(l1 - mx), jnp.exp(l2 - mx)
        den = e0 + e1 + e2
        o_ref[...] = (e0 * og[0] + e1 * og[1] + e2 * og[2]) / den
        lse_ref[...] = mx + jnp.log(den)

    return _call(
        body, (proj, proj, proj, proj, proj, bias), name="att_fwd",
        out_shape=(SDS((T, KVH * HD), f32), SDS((KVH, T, HD), f32)), grid=(B, KVH),
        in_specs=[_qkv_spec(k, lambda b, h: (b, h)) for k in range(5)]
                 + [pl.BlockSpec((None, NG, ATT_BLK, 2 * ATT_BLK), lambda b, h: (h, 0, 0, 0))],
        out_specs=(pl.BlockSpec((SEQ, HD), lambda b, h: (b, h)),
                   pl.BlockSpec((None, SEQ, HD), lambda b, h: (h, b, 0))),
        scratch_shapes=[pltpu.VMEM((SEQ, HD), bf16)] * 3 + [pltpu.VMEM((SEQ, HD), f32)] * 2
                       + [pltpu.VMEM((NG, SEQ, HD), f32)] * 2,
        semantics=("parallel", "parallel"), comm=comm)


def _att_bwd(proj, bias, o, lse, do, comm=None):
    T = proj.shape[0]
    B = T // SEQ

    def body(q0_ref, q1_ref, q2_ref, k_ref, v_ref, bias_ref, o_ref, lse_ref, do_ref, dx_ref, db_ref,
             qp, kp, vp, dop, lp, dlp, dqp, dkp, dvp, dln, nat, dkn, dvn):
        q_refs = (q0_ref, q1_ref, q2_ref)
        first = pl.program_id(1) == 0

        @pl.when(first)
        def _():
            db_ref[...] = jnp.zeros_like(db_ref)

        dln[...] = jnp.broadcast_to(jnp.sum(do_ref[...] * o_ref[...], axis=-1, keepdims=True), (SEQ, HD))
        dkn[...] = jnp.zeros_like(dkn)
        dvn[...] = jnp.zeros_like(dvn)
        for g, r in enumerate(DILATIONS):
            nb = NBLK_SEQ // r
            _to_sub(qp, q_refs[g], r, bf16)
            _to_sub(kp, k_ref, r, bf16)
            _to_sub(vp, v_ref, r, bf16)
            _to_sub(dop, do_ref, r, bf16)
            _to_sub(lp, lse_ref, r, f32)
            _to_sub(dlp, dln, r, f32)
            dkp[...] = jnp.zeros_like(dkp)
            dvp[...] = jnp.zeros_like(dvp)
            bias_p = bias_ref[g, :, 0:ATT_BLK]
            bias_c = bias_ref[g, :, ATT_BLK:2 * ATT_BLK]

            def step(j, carry):
                cur = pl.ds(pl.multiple_of(j * ATT_BLK, ATT_BLK), ATT_BLK)
                prv = pl.ds(pl.multiple_of(jnp.maximum(j - 1, 0) * ATT_BLK, ATT_BLK), ATT_BLK)
                has_prev = (j % nb) != 0
                q, kc, kv, vc, vv, dob = qp[cur, :], kp[cur, :], kp[prv, :], vp[cur, :], vp[prv, :], dop[cur, :]
                lse_j = lp[cur, 0:1]
                dl_j = dlp[cur, 0:1]
                s_c = lax.dot_general(q, kc, _DIMS["nt"], preferred_element_type=f32) * SCALE + bias_c
                s_p = lax.dot_general(q, kv, _DIMS["nt"], preferred_element_type=f32) * SCALE + bias_p
                s_p = jnp.where(has_prev, s_p, NEG)
                p_c = jnp.exp(s_c - lse_j)
                p_p = jnp.exp(s_p - lse_j)
                dp_c = lax.dot_general(dob, vc, _DIMS["nt"], preferred_element_type=f32)
                dp_p = lax.dot_general(dob, vv, _DIMS["nt"], preferred_element_type=f32)
                ds_c = p_c * (dp_c - dl_j)
                ds_p = p_p * (dp_p - dl_j)
                db_ref[g, :, ATT_BLK:2 * ATT_BLK] += ds_c
                db_ref[g, :, 0:ATT_BLK] += ds_p
                ds_cb, ds_pb = ds_c.astype(bf16), ds_p.astype(bf16)
                p_cb, p_pb = p_c.astype(bf16), p_p.astype(bf16)
                dq = jnp.dot(ds_cb, kc, preferred_element_type=f32) + jnp.dot(ds_pb, kv, preferred_element_type=f32)
                dqp[cur, :] = dq * SCALE
                dkp[cur, :] += lax.dot_general(ds_cb, q, _DIMS["tn"], preferred_element_type=f32) * SCALE
                dkp[prv, :] += lax.dot_general(ds_pb, q, _DIMS["tn"], preferred_element_type=f32) * SCALE
                dvp[cur, :] += lax.dot_general(p_cb, dob, _DIMS["tn"], preferred_element_type=f32)
                dvp[prv, :] += lax.dot_general(p_pb, dob, _DIMS["tn"], preferred_element_type=f32)
                return carry

            lax.fori_loop(0, NBLK_SEQ, step, 0, unroll=2)
            _from_sub(nat, dqp, r)
            dx_ref[:, _COL(g)] = nat[...].astype(bf16)
            _from_sub(dkn, dkp, r, accumulate=True)
            _from_sub(dvn, dvp, r, accumulate=True)
        dx_ref[:, _COL(3)] = dkn[...].astype(bf16)
        dx_ref[:, _COL(4)] = dvn[...].astype(bf16)

    blk = lambda: pl.BlockSpec((SEQ, HD), lambda h, b: (b, h))
    bias_spec = lambda: pl.BlockSpec((None, NG, ATT_BLK, 2 * ATT_BLK), lambda h, b: (h, 0, 0, 0))
    return _call(
        body, (proj, proj, proj, proj, proj, bias, o, lse, do), name="att_bwd",
        out_shape=(SDS((T, KVH * ATT_COLS), bf16), SDS((KVH, NG, ATT_BLK, 2 * ATT_BLK), f32)), grid=(KVH, B),
        in_specs=[_qkv_spec(k, lambda h, b: (b, h)) for k in range(5)]
                 + [bias_spec(), blk(), pl.BlockSpec((None, SEQ, HD), lambda h, b: (h, b, 0)), blk()],
        out_specs=(pl.BlockSpec((SEQ, ATT_COLS), lambda h, b: (b, h)), bias_spec()),
        scratch_shapes=[pltpu.VMEM((SEQ, HD), bf16)] * 4 + [pltpu.VMEM((SEQ, HD), f32)] * 9,
        semantics=("parallel", "arbitrary"), comm=comm)


MERGE_ROWS, MERGE_COLS = 1024, 256
_G_RNN_BLK = C_GATE // MERGE_COLS
_G_ATT_BLK = (C_GATE + D) // MERGE_COLS


def _merge_specs():
    cols = lambda off: pl.BlockSpec((MERGE_ROWS, MERGE_COLS), lambda i, j: (i, off + j))
    return cols(_G_RNN_BLK), cols(_G_ATT_BLK), cols(0)


def _merge_fwd(proj, pr, pa):
    def body(gr_ref, ga_ref, pr_ref, pa_ref, o_ref):
        o_ref[...] = (_sigmoid(gr_ref[...]) * pr_ref[...].astype(f32)
                      + _sigmoid(ga_ref[...]) * pa_ref[...].astype(f32)).astype(bf16)

    T = proj.shape[0]
    s_gr, s_ga, s0 = _merge_specs()
    return pl.pallas_call(body, name="merge_fwd", out_shape=SDS((T, D), bf16),
                          grid=(T // MERGE_ROWS, D // MERGE_COLS),
                          in_specs=[s_gr, s_ga, s0, s0], out_specs=s0,
                          compiler_params=_params(("parallel", "parallel")))(proj, proj, pr, pa)


def _merge_bwd(proj, pr, pa, dm):
    nj = D // MERGE_COLS

    def body(g_ref, pr_ref, pa_ref, dm_ref, dp_ref, dg_ref):
        dm_ = dm_ref[...].astype(f32)
        s = _sigmoid(g_ref[...])
        p = jnp.where(pl.program_id(1) < nj, pr_ref[...], pa_ref[...]).astype(f32)
        dp_ref[...] = (dm_ * s).astype(bf16)
        dg_ref[...] = (dm_ * p * s * (1.0 - s)).astype(bf16)

    T = proj.shape[0]
    blk = (MERGE_ROWS, MERGE_COLS)
    wrap = pl.BlockSpec(blk, lambda i, j: (i, j % nj))
    out = pl.BlockSpec(blk, lambda i, j: (i, j))
    return pl.pallas_call(
        body, name="merge_bwd", out_shape=(SDS((T, 2 * D), bf16), SDS((T, 2 * D), bf16)),
        grid=(T // MERGE_ROWS, 2 * nj),
        in_specs=[pl.BlockSpec(blk, lambda i, j: (i, _G_RNN_BLK + j)), wrap, wrap, wrap], out_specs=(out, out),
        compiler_params=_params(("parallel", "parallel")))(proj, pr, pa, dm)


FFN_COLS = 256
GELU_C = math.sqrt(2.0 / math.pi)
GELU_A = 0.044715


def _gelu_parts(x):
    t = jnp.tanh(GELU_C * (x + GELU_A * x * x * x))
    return 0.5 * x * (1.0 + t), t


def _ffn_act_fwd(gpre, up, cw, cb):
    def body(g_ref, u_ref, cw_ref, cb_ref, o_ref):
        row = lax.broadcasted_iota(jnp.int32, (SEQ, FFN_COLS), 0)
        gate = _conv_fwd(g_ref[...].astype(f32), cw_ref, cb_ref[...], row)
        o_ref[...] = (_gelu_parts(gate)[0] * u_ref[...].astype(f32)).astype(bf16)

    T = gpre.shape[0]
    blk = lambda: pl.BlockSpec((SEQ, FFN_COLS), lambda b, j: (b, j))
    return pl.pallas_call(
        body, name="ffn_act_fwd", out_shape=SDS((T, FFN_W), bf16), grid=(T // SEQ, FFN_W // FFN_COLS),
        in_specs=[blk(), blk(), pl.BlockSpec((FFN_CONV, FFN_COLS), lambda b, j: (0, j)),
                  pl.BlockSpec((1, FFN_COLS), lambda b, j: (0, j))],
        out_specs=blk(), compiler_params=_params(("parallel", "parallel")))(gpre, up, cw, cb)


def _ffn_act_bwd(gpre, up, cw, cb, dact):
    def body(g_ref, u_ref, cw_ref, cb_ref, da_ref, dg_ref, du_ref, dcw_ref, dcb_ref):
        row = lax.broadcasted_iota(jnp.int32, (SEQ, FFN_COLS), 0)
        gp = g_ref[...].astype(f32)
        gate = _conv_fwd(gp, cw_ref, cb_ref[...], row)
        gel, t = _gelu_parts(gate)
        da = da_ref[...].astype(f32)
        du_ref[...] = (da * gel).astype(bf16)
        dgel = 0.5 * (1.0 + t) + 0.5 * gate * (1.0 - t * t) * (GELU_C * (1.0 + 3.0 * GELU_A * gate * gate))
        dgate = da * u_ref[...].astype(f32) * dgel
        dx, dws, db = _conv_bwd(gp, cw_ref, dgate, row)
        dg_ref[...] = dx.astype(bf16)
        first = pl.program_id(1) == 0

        def acc(ref, val):
            @pl.when(first)
            def _():
                ref[...] = val

            @pl.when(jnp.logical_not(first))
            def _():
                ref[...] += val

        for k in range(FFN_CONV):
            acc(dcw_ref.at[k:k + 1, :], dws[k])
        acc(dcb_ref, db)

    T = gpre.shape[0]
    blk = lambda: pl.BlockSpec((SEQ, FFN_COLS), lambda j, b: (b, j))
    cws = lambda: pl.BlockSpec((FFN_CONV, FFN_COLS), lambda j, b: (0, j))
    cbs = lambda: pl.BlockSpec((1, FFN_COLS), lambda j, b: (0, j))
    return pl.pallas_call(
        body, name="ffn_act_bwd",
        out_shape=(SDS((T, FFN_W), bf16), SDS((T, FFN_W), bf16), SDS((FFN_CONV, FFN_W), f32), SDS((1, FFN_W), f32)),
        grid=(FFN_W // FFN_COLS, T // SEQ),
        in_specs=[blk(), blk(), cws(), cbs(), blk()], out_specs=(blk(), blk(), cws(), cbs()),
        compiler_params=_params(("parallel", "arbitrary")))(gpre, up, cw, cb, dact)


def _coords():
    return lax.axis_index("x"), lax.axis_index("y"), lax.axis_index("c")


def _dev_index(dev):
    return 4 * dev[0] + 2 * dev[1] + dev[2]


def _dma_sems(n):
    return [pltpu.SemaphoreType.DMA((n,)), pltpu.SemaphoreType.DMA((n,))]


def _gather_two_level(arrays):
    n = len(arrays)

    def plan(ins, outs, sems):
        send_sems, recv_sems, local_sems = sems
        x, y, c = _coords()
        me, sibling = (x, y, c), (x, y, 1 - c)
        chips = [(1 - x, y), (x, 1 - y), (1 - x, 1 - y)]

        def copy(a, k, block, to, own=False):
            dst = outs[a].at[_dev_index(block)]
            return pltpu.make_async_remote_copy(
                src_ref=ins[a] if own else dst, dst_ref=dst, send_sem=send_sems.at[7 * a + k],
                recv_sem=recv_sems.at[7 * a + k], device_id=to, device_id_type=MESH)

        mine = [pltpu.make_async_copy(ins[a], outs[a].at[_dev_index(me)], local_sems.at[a]) for a in range(n)]
        first = [copy(a, 0, me, sibling, own=True) for a in range(n)]
        first += [copy(a, 1 + j, me, (*chip, c), own=True) for a in range(n) for j, chip in enumerate(chips)]
        passed = [[copy(a, 4 + j, (*chip, c), sibling) for a in range(n)] for j, chip in enumerate(chips)]
        arrive_ici = [[copy(a, 1 + j, (*chip, c), me) for a in range(n)] for j, chip in enumerate(chips)]
        arrive_d2d = [copy(a, 0, sibling, me) for a in range(n)]
        arrive_d2d += [copy(a, 4 + j, (*chip, 1 - c), me) for a in range(n) for j, chip in enumerate(chips)]
        return mine, first, passed, arrive_ici, arrive_d2d

    def start(ins, outs, sems):
        mine, first, _, _, _ = plan(ins, outs, sems)
        for cp in mine + first:
            cp.start()

    def finish(ins, outs, sems):
        mine, first, passed, arrive_ici, arrive_d2d = plan(ins, outs, sems)
        for j in range(3):
            for cp in arrive_ici[j]:
                cp.wait_recv()
            for cp in passed[j]:
                cp.start()
        for cp in arrive_d2d:
            cp.wait_recv()
        for cp in first + [cp for group in passed for cp in group]:
            cp.wait_send()
        for cp in mine:
            cp.wait()

    return _Comm(arrays, [SDS((N_DEV,) + a.shape, a.dtype) for a in arrays],
                 _dma_sems(7 * n) + [pltpu.SemaphoreType.DMA((n,))], start, finish)


def _gather_direct(arrays):
    n = len(arrays)

    def plan(ins, outs, sems):
        send_sems, recv_sems, local_sems = sems
        x, y, c = _coords()
        me = (x, y, c)
        mine = [pltpu.make_async_copy(ins[a], outs[a].at[_dev_index(me)], local_sems.at[a]) for a in range(n)]
        sends, arrivals = [], []
        for a in range(n):
            for k in range(1, N_DEV):
                peer = (1 - x if k & 4 else x, 1 - y if k & 2 else y, 1 - c if k & 1 else c)
                s = 7 * a + k - 1
                for slot, out in ((me, sends), (peer, arrivals)):
                    out.append(pltpu.make_async_remote_copy(
                        src_ref=ins[a], dst_ref=outs[a].at[_dev_index(slot)], send_sem=send_sems.at[s],
                        recv_sem=recv_sems.at[s], device_id=peer, device_id_type=MESH))
        return mine, sends, arrivals

    def start(ins, outs, sems):
        mine, sends, _ = plan(ins, outs, sems)
        for cp in mine + sends:
            cp.start()

    def finish(ins, outs, sems):
        mine, sends, arrivals = plan(ins, outs, sems)
        for cp in arrivals:
            cp.wait_recv()
        for cp in sends:
            cp.wait_send()
        for cp in mine:
            cp.wait()

    return _Comm(arrays, [SDS((N_DEV,) + a.shape, a.dtype) for a in arrays],
                 _dma_sems(7 * n) + [pltpu.SemaphoreType.DMA((n,))], start, finish)


def _exchange(arrays, n_blocks, route):
    n = len(arrays)

    def plan(ins, outs, sems):
        send_sems, recv_sems = sems
        cps = []
        for a in range(n):
            for j, (src, peer) in enumerate(route(*_coords())):
                cps.append(pltpu.make_async_remote_copy(
                    src_ref=ins[a].at[src], dst_ref=outs[a].at[j], send_sem=send_sems.at[n_blocks * a + j],
                    recv_sem=recv_sems.at[n_blocks * a + j], device_id=peer, device_id_type=MESH))
        return cps

    def start(ins, outs, sems):
        for cp in plan(ins, outs, sems):
            cp.start()

    def finish(ins, outs, sems):
        for cp in plan(ins, outs, sems):
            cp.wait()

    return _Comm(arrays, [SDS((n_blocks,) + a.shape[1:], a.dtype) for a in arrays], _dma_sems(n_blocks * n),
                 start, finish)


def _sibling_exchange(arrays):
    return _exchange(arrays, 4, lambda x, y, c: [(2 * k + 1 - c, (x, y, 1 - c)) for k in range(4)])


def _chip_exchange(arrays):
    return _exchange(arrays, 3, lambda x, y, c: [(2 * cx + cy, (cx, cy, c))
                                                 for cx, cy in ((1 - x, y), (x, 1 - y), (1 - x, 1 - y))])


def _both(first, second):
    k_in, k_out, k_sem = len(first.inputs), len(first.out_shapes), len(first.sem_shapes)

    def start(ins, outs, sems):
        first.start(ins[:k_in], outs[:k_out], sems[:k_sem])
        second.start(ins[k_in:], outs[k_out:], sems[k_sem:])

    def finish(ins, outs, sems):
        second.finish(ins[k_in:], outs[k_out:], sems[k_sem:])
        first.finish(ins[:k_in], outs[:k_out], sems[:k_sem])

    return _Comm(first.inputs + second.inputs, first.out_shapes + second.out_shapes,
                 first.sem_shapes + second.sem_shapes, start, finish)


def _run(comm, name):
    def body(*refs):
        k_in, k_out = len(comm.inputs), len(comm.out_shapes)
        ins, outs, sems = refs[:k_in], refs[k_in:k_in + k_out], refs[k_in + k_out:]
        comm.start(ins, outs, sems)
        comm.finish(ins, outs, sems)

    return pl.pallas_call(body, name=name, out_shape=comm.out_shapes, in_specs=[ANY] * len(comm.inputs),
                          out_specs=(ANY,) * len(comm.out_shapes), scratch_shapes=comm.sem_shapes)(*comm.inputs)


TILE_ELEMS = 192 * 1024


def _row_tile(R, C):
    if R * C <= TILE_ELEMS:
        return R
    return max(t for t in range(SUBLANES, R, SUBLANES) if R % t == 0 and t * C <= TILE_ELEMS)


def _pair_sum(g8, recv, c_idx, name):
    R, C = g8.shape[-2:]
    t = _row_tile(R, C)

    def body(c_ref, g_ref, r_ref, o_ref):
        o_ref[...] = (g_ref[...] + r_ref[...]).astype(bf16)

    return pl.pallas_call(
        body, name="pair_sum_" + name, out_shape=SDS((4, R, C), bf16),
        grid_spec=pltpu.PrefetchScalarGridSpec(
            num_scalar_prefetch=1, grid=(4, R // t),
            in_specs=[pl.BlockSpec((None, t, C), lambda k, i, c: (2 * k + c[0], i, 0)),
                      pl.BlockSpec((None, t, C), lambda k, i, c: (k, i, 0))],
            out_specs=pl.BlockSpec((None, t, C), lambda k, i, c: (k, i, 0))),
        compiler_params=_params(("parallel", "parallel")))(c_idx, g8, recv)


def _adamw_math(w, g, m, v):
    m = ADAM_B1 * m + (1.0 - ADAM_B1) * g
    v = ADAM_B2 * v + (1.0 - ADAM_B2) * (g * g)
    m_hat = m / (1.0 - ADAM_B1 ** ADAM_STEP)
    v_hat = v / (1.0 - ADAM_B2 ** ADAM_STEP)
    delta = -ADAM_LR * (m_hat / (jnp.sqrt(v_hat) + ADAM_EPS) + ADAM_WD * w)
    return delta, m, v


def _adamw_sharded(pa, recv, k_idx, w, m, v, name):
    R, C = w.shape
    t = _row_tile(R, C)

    def body(k_ref, p_ref, r_ref, w_ref, m_ref, v_ref, g_ref, d_ref, nm_ref, nv_ref):
        g = p_ref[...].astype(f32)
        for j in range(3):
            g = g + r_ref[j].astype(f32)
        d, nm, nv = _adamw_math(w_ref[...], g, m_ref[...], v_ref[...])
        g_ref[...], d_ref[...], nm_ref[...], nv_ref[...] = g, d, nm, nv

    tile = lambda: pl.BlockSpec((t, C), lambda i, k: (i, 0))
    return pl.pallas_call(
        body, name="adamw_" + name, out_shape=(SDS((R, C), f32),) * 4,
        grid_spec=pltpu.PrefetchScalarGridSpec(
            num_scalar_prefetch=1, grid=(R // t,),
            in_specs=[pl.BlockSpec((None, t, C), lambda i, k: (k[0], i, 0)),
                      pl.BlockSpec((3, t, C), lambda i, k: (0, i, 0)), tile(), tile(), tile()],
            out_specs=(tile(), tile(), tile(), tile())),
        compiler_params=_params(("parallel",)))(k_idx, pa, recv, w, m, v)


def _adamw_replicated(parts, ws, ms, vs):
    n = len(ws)

    def body(*refs):
        p, w, m, v = (refs[i * n:(i + 1) * n] for i in range(4))
        outs = refs[4 * n:]
        for a in range(n):
            g = p[a][0]
            for j in range(1, N_DEV):
                g = g + p[a][j]
            d, nm, nv = _adamw_math(w[a][...], g, m[a][...], v[a][...])
            for i, val in enumerate((g, d, nm, nv)):
                outs[i * n + a][...] = val

    shapes = tuple(SDS(w.shape, f32) for w in ws)
    res = pl.pallas_call(body, name="adamw_replicated", out_shape=shapes * 4,
                         compiler_params=_params())(*parts, *ws, *ms, *vs)
    return [res[i * n:(i + 1) * n] for i in range(4)]


def _cols_to_full(g):
    n, r, c = g.shape
    return g.transpose(1, 0, 2).reshape(r, n * c)


def _full_to_cols(a):
    r, c = a.shape
    return a.reshape(r, N_DEV, c // N_DEV).transpose(1, 0, 2)


def _rows_blocked(a):
    r, c = a.shape
    return a.reshape(N_DEV, r // N_DEV, c)


def _w_in_to_internal(w):
    K = w.shape[0]
    q = w[:, 1280:2816].reshape(K, NG, KVH, 1, HD).transpose(0, 2, 1, 3, 4).reshape(K, KVH, NG, HD)
    k = w[:, 2816:3328].reshape(K, KVH, 1, HD)
    v = w[:, 3328:3840].reshape(K, KVH, 1, HD)
    att = jnp.concatenate([q, k, v], axis=2).reshape(K, KVH * ATT_COLS)
    return jnp.concatenate([w[:, :1280], att, w[:, 3840:]], axis=1)


def _w_in_from_internal(w):
    K = w.shape[0]
    att = w[:, C_ATT:C_GATE].reshape(K, KVH, 5, HD)
    q = att[:, :, 0:3].transpose(0, 2, 1, 3).reshape(K, NG * KVH * HD)
    k = att[:, :, 3].reshape(K, KVH * HD)
    v = att[:, :, 4].reshape(K, KVH * HD)
    return jnp.concatenate([w[:, :C_ATT], q, k, v, w[:, C_GATE:]], axis=1)


_IN_NAMES = ('x', 'rel_bias', 'norm_mix_pre', 'norm_mix_post', 'w_in', 'conv_rnn_w', 'conv_rnn_b', 'w_rg_a', 'b_rg_a',
             'w_rg_x', 'b_rg_x', 'lru_lambda', 'w_branch_rnn', 'w_branch_att', 'w_out', 'norm_ffn_pre',
             'norm_ffn_post', 'w_ffn_gate', 'w_ffn_up', 'conv_ffn_w', 'conv_ffn_b', 'w_ffn_down')
_WEIGHTS = _IN_NAMES[1:]
_SHARDED = {"w_in": "col", "conv_rnn_w": "col", "w_branch_rnn": "row", "w_branch_att": "col", "w_out": "row",
            "w_ffn_gate": "col", "w_ffn_up": "col", "conv_ffn_w": "col", "w_ffn_down": "row"}
_REPLICATED = tuple(n for n in _WEIGHTS if n not in _SHARDED)
_EARLY = ("w_branch_rnn", "w_branch_att", "w_out", "w_ffn_gate", "w_ffn_up", "conv_ffn_w", "w_ffn_down")
_LATE = ("w_in", "conv_rnn_w")


def _flat2(a):
    return a.reshape(-1, a.shape[-1])


def _train_step(inp):
    x_idx, y_idx, c_idx = _coords()
    W = {n: inp[n] for n in _WEIGHTS}
    x = inp["x"].reshape(-1, D)
    target = inp["loss_target"].reshape(-1, D)
    shard = {n: inp[n][0] for n in _SHARDED}

    g_in, g_cr, g_cf = _run(_gather_two_level([shard["w_in"].astype(bf16), shard["conv_rnn_w"],
                                               shard["conv_ffn_w"]]), "ag_w_in")
    w_in = _w_in_to_internal(_cols_to_full(g_in))
    cw_rnn, cw_ffn = _cols_to_full(g_cr), _cols_to_full(g_cf)
    later = ("w_branch_rnn", "w_branch_att", "w_out", "w_ffn_gate", "w_ffn_up", "w_ffn_down")
    ag_rest = _gather_direct([shard[n].astype(bf16) for n in later])

    wa, wx = W["w_rg_a"][0], W["w_rg_x"][0]
    buckets = jnp.asarray(_bucket_maps())

    hn = _norm_in(x, W["norm_mix_pre"])
    proj = _mm(hn, w_in, "nn", f32, "mm_proj", 512, IN_W // 2, 1024, cols_outer=True)
    h_rnn = _rnn_fwd(proj, cw_rnn, W["conv_rnn_b"], wa, W["b_rg_a"], wx, W["b_rg_x"], W["lru_lambda"])
    bias = _bias_tables(W["rel_bias"], buckets)
    (o_att, lse), gathered = _att_fwd(proj, bias, comm=ag_rest)
    gathered = dict(zip(later, gathered))
    w_brnn = gathered["w_branch_rnn"].reshape(RNN_W, D)
    w_batt = _cols_to_full(gathered["w_branch_att"])
    w_out = gathered["w_out"].reshape(D, D)
    w_gate, w_up = gathered["w_ffn_gate"], gathered["w_ffn_up"]
    w_down = gathered["w_ffn_down"].reshape(FFN_W, D)
    fcol = FFN_W // N_DEV
    pr = _mm(h_rnn, w_brnn, "nn", bf16, "mm_pr", 1024, 1024, 1280)
    pa = _mm(o_att, w_batt, "nn", bf16, "mm_pa", 1024, 1024, 512)
    merged = _merge_fwd(proj, pr, pa)
    mix = _mm(merged, w_out, "nn", f32, "mm_mix", 1024, 1024, 1024)
    h1, hn2 = _mid_fwd(x, mix, W["norm_mix_post"], W["norm_ffn_pre"])
    gpre = _mm(hn2, w_gate, "nn", bf16, "mm_gate", 1024, fcol, 1024, cols_outer=True, b_blocked=True)
    up = _mm(hn2, w_up, "nn", bf16, "mm_up", 1024, fcol, 1024, cols_outer=True, b_blocked=True)
    act = _ffn_act_fwd(gpre, up, cw_ffn, W["conv_ffn_b"])
    ff = _mm(act, w_down, "nn", f32, "mm_down", 1024, 1024, 1024)
    loss_part, dy, dff, dg_fpost = _final(h1, ff, W["norm_ffn_post"], target)

    grads = {}
    dact = _mm(dff, w_down, "nt", bf16, "mm_dact", 1024, 1024, 1024, cols_outer=True)
    grads["w_ffn_down"] = _rows_blocked(_mm(act, dff, "tn", f32, "mm_dw_down", 1024, 1024, 2048))
    dgpre, dup, dcw_ffn, dcb_ffn = _ffn_act_bwd(gpre, up, cw_ffn, W["conv_ffn_b"], dact)
    grads["conv_ffn_w"] = _full_to_cols(dcw_ffn)
    grads["w_ffn_gate"] = _mm(hn2, dgpre, "tn", f32, "mm_dw_gate", 1024, fcol, 2048, out_blocked=True)
    grads["w_ffn_up"] = _mm(hn2, dup, "tn", f32, "mm_dw_up", 1024, fcol, 2048, out_blocked=True)
    dhn2 = _mm(dgpre, w_gate, "nt", f32, "mm_dhn2_a", 1024, 1024, fcol, b_blocked=True)
    dhn2 = _mm(dup, w_up, "nt", f32, "mm_dhn2_b", 1024, 1024, fcol, add=dhn2, b_blocked=True)
    dh1, dmix, dg_fpre, dg_post = _mid_bwd(dy, dhn2, h1, W["norm_ffn_pre"], mix, W["norm_mix_post"])
    dmerged = _mm(dmix, w_out, "nt", bf16, "mm_dmerged", 1024, 1024, 1024)
    grads["w_out"] = _rows_blocked(_mm(merged, dmix, "tn", f32, "mm_dw_out", 1024, 1024, 2048))
    dprpa, dgates = _merge_bwd(proj, pr, pa, dmerged)
    dpr, dpa = dprpa[:, :D], dprpa[:, D:]
    dh_rnn = _mm(dpr, w_brnn, "nt", bf16, "mm_dh_rnn", 1024, 1280, 1024)
    grads["w_branch_rnn"] = _rows_blocked(_mm(h_rnn, dpr, "tn", f32, "mm_dw_brnn", 1280, 1024, 1024))
    do_att = _mm(dpa, w_batt, "nt", f32, "mm_do_att", 1024, 512, 1024)
    grads["w_branch_att"] = _full_to_cols(_mm(o_att, dpa, "tn", f32, "mm_dw_batt", 512, 1024, 2048))

    c_arr = jnp.reshape(c_idx, (1,)).astype(jnp.int32)
    k_arr = jnp.reshape(2 * x_idx + y_idx, (1,)).astype(jnp.int32)
    (dqkv, dbias), from_sibling = _att_bwd(proj, bias, o_att, lse, do_att,
                                           comm=_sibling_exchange([grads[n] for n in _EARLY]))
    pair = {n: _pair_sum(grads[n], r, c_arr, n) for n, r in zip(_EARLY, from_sibling)}
    drel = _bias_grad(dbias, buckets)
    (dxr, dcw_rnn, dcb_rnn, dwa, dba, dwx, dbx, dlam), from_chips = _rnn_bwd(
        proj, h_rnn, dh_rnn, cw_rnn, W["conv_rnn_b"], wa, W["b_rg_a"], wx, W["b_rg_x"], W["lru_lambda"],
        comm=_chip_exchange([pair[n] for n in _EARLY]))
    from_chips = dict(zip(_EARLY, from_chips))
    dw_in = jnp.concatenate([
        _mm(hn, dxr, "tn", f32, "mm_dw_in_r", 1024, 1280, 1024),
        _mm(hn, dqkv, "tn", f32, "mm_dw_in_a", 1024, 1280, 1024),
        _mm(hn, dgates, "tn", f32, "mm_dw_in_g", 1024, 1024, 2048)], axis=1)
    grads["w_in"] = _full_to_cols(_w_in_from_internal(dw_in))
    grads["conv_rnn_w"] = _full_to_cols(dcw_rnn)
    dhn = _mm(dxr, w_in[:, :C_ATT], "nt", f32, "mm_dhn_r", 1024, 1024, 1280)
    dhn = _mm(dqkv, w_in[:, C_ATT:C_GATE], "nt", f32, "mm_dhn_a", 1024, 1024, 1280, add=dhn)
    dhn = _mm(dgates, w_in[:, C_GATE:], "nt", f32, "mm_dhn_g", 1024, 1024, 1024, add=dhn)
    (dx, dg_pre), from_sibling = _in_bwd(dh1, dhn, x, W["norm_mix_pre"],
                                         comm=_sibling_exchange([grads[n] for n in _LATE]))
    pair.update({n: _pair_sum(grads[n], r, c_arr, n) for n, r in zip(_LATE, from_sibling)})

    gsmall = {"rel_bias": drel, "norm_mix_pre": dg_pre, "norm_mix_post": dg_post, "conv_rnn_b": dcb_rnn,
              "w_rg_a": dwa, "b_rg_a": dba, "w_rg_x": dwx, "b_rg_x": dbx, "lru_lambda": dlam,
              "norm_ffn_pre": dg_fpre, "norm_ffn_post": dg_fpost, "conv_ffn_b": dcb_ffn}
    last = _run(_both(_chip_exchange([pair[n] for n in _LATE]),
                      _gather_two_level([_flat2(gsmall[n]) for n in _REPLICATED])), "rs_late_ag_small")
    from_chips.update(zip(_LATE, last[:len(_LATE)]))
    parts = last[len(_LATE):]

    out = {}
    for n in _SHARDED:
        res = _adamw_sharded(pair[n], from_chips[n], k_arr, shard[n], inp["m_" + n][0], inp["v_" + n][0], n)
        out[n] = [r[None] for r in res]
    small = _adamw_replicated(parts, *[[_flat2(inp[p + n]) for n in _REPLICATED] for p in ("", "m_", "v_")])
    for a, n in enumerate(_REPLICATED):
        out[n] = [small[i][a].reshape(inp[n].shape) for i in range(4)]

    loss = lax.psum(loss_part[0, 0], ("x", "y", "c"))
    outs = [loss, dx.reshape(inp["x"].shape)]
    for i in range(4):
        outs.extend(out[n][i] for n in _WEIGHTS)
    return tuple(outs)


def kernel(x, rel_bias, norm_mix_pre, norm_mix_post, w_in, conv_rnn_w, conv_rnn_b, w_rg_a, b_rg_a, w_rg_x, b_rg_x, lru_lambda, w_branch_rnn, w_branch_att, w_out, norm_ffn_pre, norm_ffn_post, w_ffn_gate, w_ffn_up, conv_ffn_w, conv_ffn_b, w_ffn_down, loss_target, m_rel_bias, m_norm_mix_pre, m_norm_mix_post, m_w_in, m_conv_rnn_w, m_conv_rnn_b, m_w_rg_a, m_b_rg_a, m_w_rg_x, m_b_rg_x, m_lru_lambda, m_w_branch_rnn, m_w_branch_att, m_w_out, m_norm_ffn_pre, m_norm_ffn_post, m_w_ffn_gate, m_w_ffn_up, m_conv_ffn_w, m_conv_ffn_b, m_w_ffn_down, v_rel_bias, v_norm_mix_pre, v_norm_mix_post, v_w_in, v_conv_rnn_w, v_conv_rnn_b, v_w_rg_a, v_b_rg_a, v_w_rg_x, v_b_rg_x, v_lru_lambda, v_w_branch_rnn, v_w_branch_att, v_w_out, v_norm_ffn_pre, v_norm_ffn_post, v_w_ffn_gate, v_w_ffn_up, v_conv_ffn_w, v_conv_ffn_b, v_w_ffn_down):
    vals = locals()
    names = list(_IN_NAMES) + ["loss_target"] + ["m_" + n for n in _WEIGHTS] + ["v_" + n for n in _WEIGHTS]
    return _train_step({n: vals[n] for n in names})
```

```python
import functools
import math

import numpy as np
import jax
import jax.numpy as jnp
from jax import lax
from jax.experimental import pallas as pl
from jax.experimental.pallas import tpu as pltpu

f32, bf16 = jnp.float32, jnp.bfloat16
SDS = jax.ShapeDtypeStruct
MESH = pl.DeviceIdType.MESH
ANY = pl.BlockSpec(memory_space=pl.ANY)

D = 1024
SEQ = 2048
RNN_W = 1280
RNN_BLOCKS = 10
LANES = 128
SUBLANES = 8
RNN_CONV = 4
LRU_C = 8.0
HD = 128
KVH = 4
DILATIONS = (1, 4, 16)
NG = 3
ATT_BLK = 128
NBLK_SEQ = SEQ // ATT_BLK
REL_BUCKETS = 32
REL_MAX_DIST = 2048
FFN_W = 3072
FFN_CONV = 3
EPS = 1e-6
IN_W = 5888
ATT_COLS = 5 * HD
C_ATT = RNN_W
C_GATE = RNN_W + KVH * ATT_COLS
NEG = -1e30

ADAM_LR, ADAM_B1, ADAM_B2, ADAM_EPS, ADAM_WD, ADAM_STEP = 0.001, 0.9, 0.999, 1e-08, 0.01, 10

VMEM_LIMIT_BYTES = 56 * 1024 * 1024
N_DEV = 8


def _params(sem=None):
    return pltpu.CompilerParams(dimension_semantics=sem, vmem_limit_bytes=VMEM_LIMIT_BYTES)


def _sigmoid(x):
    return 1.0 / (1.0 + jnp.exp(-x))


class _Comm:
    def __init__(self, inputs, out_shapes, sem_shapes, start, finish):
        self.inputs, self.out_shapes, self.sem_shapes = tuple(inputs), tuple(out_shapes), list(sem_shapes)
        self.start, self.finish = start, finish


def _call(body, args, *, name, grid, in_specs, out_specs, out_shape, scratch_shapes=(), semantics, comm=None):
    if comm is None:
        return pl.pallas_call(body, name=name, grid=grid, in_specs=list(in_specs), out_specs=tuple(out_specs),
                              out_shape=tuple(out_shape), scratch_shapes=list(scratch_shapes),
                              compiler_params=_params(semantics))(*args), ()
    n_in, n_out, n_scr = len(in_specs), len(out_shape), len(scratch_shapes)
    c_in, c_out = len(comm.inputs), len(comm.out_shapes)

    def fused(*refs):
        ins, refs = refs[:n_in], refs[n_in:]
        cin, refs = refs[:c_in], refs[c_in:]
        outs, refs = refs[:n_out], refs[n_out:]
        cout, refs = refs[:c_out], refs[c_out:]
        scr, csem = refs[:n_scr], refs[n_scr:]
        first = functools.reduce(jnp.logical_and, [pl.program_id(d) == 0 for d in range(len(grid))])
        last = functools.reduce(jnp.logical_and, [pl.program_id(d) == grid[d] - 1 for d in range(len(grid))])

        @pl.when(first)
        def _():
            comm.start(cin, cout, csem)

        body(*ins, *outs, *scr)

        @pl.when(last)
        def _():
            comm.finish(cin, cout, csem)

    res = pl.pallas_call(
        fused, name=name, grid=grid, in_specs=list(in_specs) + [ANY] * c_in,
        out_specs=tuple(out_specs) + (ANY,) * c_out, out_shape=tuple(out_shape) + comm.out_shapes,
        scratch_shapes=list(scratch_shapes) + comm.sem_shapes,
        compiler_params=_params(("arbitrary",) * len(grid)))(*args, *comm.inputs)
    return res[:n_out], res[n_out:]


_DIMS = {"nn": (((1,), (0,)), ((), ())), "nt": (((1,), (1,)), ((), ())), "tn": (((0,), (0,)), ((), ()))}


def _mm(a, b, mode, out_dtype, name, tm, tn, tk, add=None, cols_outer=False, comm=None):
    if mode == "nn":
        (M, K), (K2, N) = a.shape, b.shape
    elif mode == "nt":
        (M, K), (N, K2) = a.shape, b.shape
    else:
        (K, M), (K2, N) = a.shape, b.shape
    assert K == K2 and M % tm == 0 and N % tn == 0 and K % tk == 0, (name, a.shape, b.shape)
    nk = K // tk
    has_add = add is not None

    def body(*refs):
        if has_add:
            a_ref, b_ref, c_ref, o_ref = refs[:4]
        else:
            a_ref, b_ref, o_ref = refs[:3]
        part = lax.dot_general(a_ref[...].astype(bf16), b_ref[...].astype(bf16), _DIMS[mode],
                               preferred_element_type=f32)

        def finish(acc):
            if has_add:
                acc = acc + c_ref[...]
            o_ref[...] = acc.astype(o_ref.dtype)

        if nk == 1:
            finish(part)
        else:
            acc_ref = refs[-1]
            k = pl.program_id(2)

            @pl.when(k == 0)
            def _():
                acc_ref[...] = part

            @pl.when(k > 0)
            def _():
                acc_ref[...] += part

            @pl.when(k == nk - 1)
            def _():
                finish(acc_ref[...])

    def ij(f):
        return (lambda j, i, k: f(i, j, k)) if cols_outer else f

    if mode == "tn":
        a_spec = pl.BlockSpec((tk, tm), ij(lambda i, j, k: (k, i)))
    else:
        a_spec = pl.BlockSpec((tm, tk), ij(lambda i, j, k: (i, k)))
    if mode == "nt":
        b_spec = pl.BlockSpec((tn, tk), ij(lambda i, j, k: (j, k)))
    else:
        b_spec = pl.BlockSpec((tk, tn), ij(lambda i, j, k: (k, j)))
    o_spec = pl.BlockSpec((tm, tn), ij(lambda i, j, k: (i, j)))
    in_specs = [a_spec, b_spec] + ([o_spec] if has_add else [])
    args = (a, b) + ((add,) if has_add else ())
    grid = (N // tn, M // tm, nk) if cols_outer else (M // tm, N // tn, nk)
    (out,), extra = _call(
        body, args, name=name, out_shape=(SDS((M, N), out_dtype),), grid=grid, in_specs=in_specs,
        out_specs=(o_spec,), scratch_shapes=[pltpu.VMEM((tm, tn), f32)] if nk > 1 else [],
        semantics=("parallel", "parallel", "arbitrary"), comm=comm)
    return out if comm is None else (out, extra)


ROW_TILE = 512


def _rms_fwd(x, g):
    r = lax.rsqrt(jnp.mean(x * x, axis=-1, keepdims=True) + EPS)
    return x * r * g


def _rms_bwd(x, g, dy):
    r = lax.rsqrt(jnp.mean(x * x, axis=-1, keepdims=True) + EPS)
    xh = x * r
    dxh = dy * g
    dx = r * (dxh - xh * jnp.mean(dxh * xh, axis=-1, keepdims=True))
    return dx, jnp.sum(dy * xh, axis=0, keepdims=True)


def _acc_out(ref, val):
    @pl.when(pl.program_id(0) == 0)
    def _():
        ref[...] = val

    @pl.when(pl.program_id(0) > 0)
    def _():
        ref[...] += val


def _row_spec(width=D):
    return pl.BlockSpec((ROW_TILE, width), lambda i: (i, 0))


def _vec_spec(width=D):
    return pl.BlockSpec((1, width), lambda i: (0, 0))


def _norm_in(x, g):
    def body(x_ref, g_ref, o_ref):
        o_ref[...] = _rms_fwd(x_ref[...], g_ref[...]).astype(bf16)

    T = x.shape[0]
    return pl.pallas_call(body, name="norm_in", out_shape=SDS((T, D), bf16), grid=(T // ROW_TILE,),
                          in_specs=[_row_spec(), _vec_spec()], out_specs=_row_spec(),
                          compiler_params=_params(("parallel",)))(x, g)


def _mid_fwd(x, mix, g_post, g_fpre):
    def body(x_ref, mix_ref, gp_ref, gf_ref, h1_ref, hn2_ref):
        h1 = x_ref[...] + _rms_fwd(mix_ref[...], gp_ref[...])
        h1_ref[...] = h1
        hn2_ref[...] = _rms_fwd(h1, gf_ref[...]).astype(bf16)

    T = x.shape[0]
    return pl.pallas_call(body, name="mid_fwd", out_shape=(SDS((T, D), f32), SDS((T, D), bf16)),
                          grid=(T // ROW_TILE,),
                          in_specs=[_row_spec(), _row_spec(), _vec_spec(), _vec_spec()],
                          out_specs=(_row_spec(), _row_spec()),
                          compiler_params=_params(("parallel",)))(x, mix, g_post, g_fpre)


def _final(h1, ff, g_fpost, target):
    def body(h1_ref, ff_ref, g_ref, t_ref, loss_ref, dy_ref, dff_ref, dg_ref):
        ff = ff_ref[...]
        g = g_ref[...]
        e = h1_ref[...] + _rms_fwd(ff, g) - t_ref[...]
        part = jnp.sum(jnp.sum(e * e, axis=1, keepdims=True), axis=0, keepdims=True) * (0.5 / D)
        dy = e * (1.0 / D)
        dy_ref[...] = dy
        dff, dg = _rms_bwd(ff, g, dy)
        dff_ref[...] = dff.astype(bf16)
        _acc_out(loss_ref, part)
        _acc_out(dg_ref, dg)

    T = h1.shape[0]
    return pl.pallas_call(
        body, name="final", out_shape=(SDS((1, 1), f32), SDS((T, D), f32), SDS((T, D), bf16), SDS((1, D), f32)),
        grid=(T // ROW_TILE,),
        in_specs=[_row_spec(), _row_spec(), _vec_spec(), _row_spec()],
        out_specs=(pl.BlockSpec((1, 1), lambda i: (0, 0)), _row_spec(), _row_spec(), _vec_spec()),
        compiler_params=_params(("arbitrary",)))(h1, ff, g_fpost, target)


def _mid_bwd(dy, dhn2, h1, g_fpre, mix, g_post):
    def body(dy_ref, dhn2_ref, h1_ref, gf_ref, mix_ref, gp_ref, dh1_ref, dmix_ref, dgf_ref, dgp_ref):
        d1, dgf = _rms_bwd(h1_ref[...], gf_ref[...], dhn2_ref[...])
        dh1 = dy_ref[...] + d1
        dh1_ref[...] = dh1
        dmix, dgp = _rms_bwd(mix_ref[...], gp_ref[...], dh1)
        dmix_ref[...] = dmix.astype(bf16)
        _acc_out(dgf_ref, dgf)
        _acc_out(dgp_ref, dgp)

    T = dy.shape[0]
    return pl.pallas_call(
        body, name="mid_bwd", out_shape=(SDS((T, D), f32), SDS((T, D), bf16), SDS((1, D), f32), SDS((1, D), f32)),
        grid=(T // ROW_TILE,),
        in_specs=[_row_spec(), _row_spec(), _row_spec(), _vec_spec(), _row_spec(), _vec_spec()],
        out_specs=(_row_spec(), _row_spec(), _vec_spec(), _vec_spec()),
        compiler_params=_params(("arbitrary",)))(dy, dhn2, h1, g_fpre, mix, g_post)


def _in_bwd(dh1, dhn, x, g_pre):
    def body(dh1_ref, dhn_ref, x_ref, g_ref, dx_ref, dg_ref):
        d, dg = _rms_bwd(x_ref[...], g_ref[...], dhn_ref[...])
        dx_ref[...] = dh1_ref[...] + d
        _acc_out(dg_ref, dg)

    T = x.shape[0]
    return pl.pallas_call(
        body, name="in_bwd", out_shape=(SDS((T, D), f32), SDS((1, D), f32)), grid=(T // ROW_TILE,),
        in_specs=[_row_spec(), _row_spec(), _row_spec(), _vec_spec()], out_specs=(_row_spec(), _vec_spec()),
        compiler_params=_params(("arbitrary",)))(dh1, dhn, x, g_pre)


def _shift_dn(x, d, row):
    if d == 0:
        return x
    return jnp.where(row >= d, pltpu.roll(x, d, 0), 0.0)


def _shift_up(x, d, row):
    if d == 0:
        return x
    n = x.shape[0]
    return jnp.where(row < n - d, pltpu.roll(x, n - d, 0), 0.0)


def _conv_fwd(x, w_ref, b, row):
    K = w_ref.shape[0]
    y = b
    for k in range(K):
        y = y + w_ref[k:k + 1, :] * _shift_dn(x, K - 1 - k, row)
    return y


def _conv_bwd(x, w_ref, dy, row):
    K = w_ref.shape[0]
    dx = jnp.zeros_like(dy)
    dws = []
    for k in range(K):
        dx = dx + w_ref[k:k + 1, :] * _shift_up(dy, K - 1 - k, row)
        dws.append(jnp.sum(dy * _shift_dn(x, K - 1 - k, row), axis=0, keepdims=True))
    return dx, dws, jnp.sum(dy, axis=0, keepdims=True)


def _scan_fwd(a, u, row):
    n = a.shape[0]
    d = 1
    while d < n:
        last = 2 * d >= n
        if d < SUBLANES:
            u = u + a * jnp.where(row >= d, pltpu.roll(u, d, 0), 0.0)
            if not last:
                a = a * jnp.where(row >= d, pltpu.roll(a, d, 0), 1.0)
        else:
            u = jnp.concatenate([u[:d], u[d:] + a[d:] * u[:n - d]], axis=0)
            if not last:
                a = jnp.concatenate([a[:d], a[d:] * a[:n - d]], axis=0)
        d *= 2
    return u


def _scan_bwd(b, u, row):
    n = b.shape[0]
    d = 1
    while d < n:
        last = 2 * d >= n
        if d < SUBLANES:
            u = u + b * jnp.where(row < n - d, pltpu.roll(u, n - d, 0), 0.0)
            if not last:
                b = b * jnp.where(row < n - d, pltpu.roll(b, n - d, 0), 1.0)
        else:
            u = jnp.concatenate([u[:n - d] + b[:n - d] * u[d:], u[n - d:]], axis=0)
            if not last:
                b = jnp.concatenate([b[:n - d] * b[d:], b[n - d:]], axis=0)
        d *= 2
    return u


def _neg_expm1(z):
    series = -z * (1.0 + z * (0.5 + z * (1.0 / 6.0 + z * (1.0 / 24.0 + z * (1.0 / 120.0)))))
    return jnp.where(z > -0.1, series, 1.0 - jnp.exp(z))


def _rnn_gates(xr, cw_ref, cb, wa, ba, wx, bx, lam, row):
    xc = _conv_fwd(xr, cw_ref, cb, row)
    xcb = xc.astype(bf16)
    r = _sigmoid(jnp.dot(xcb, wa.astype(bf16), preferred_element_type=f32) + ba)
    i = _sigmoid(jnp.dot(xcb, wx.astype(bf16), preferred_element_type=f32) + bx)
    z = -lam
    sp = jnp.maximum(z, 0.0) + jnp.log(1.0 + jnp.exp(-jnp.abs(z)))
    log_a = (-LRU_C * sp) * r
    a = jnp.exp(log_a)
    s = jnp.sqrt(_neg_expm1(2.0 * log_a))
    return xc, xcb, r, i, sp, a, s


def _rnn_specs(B):
    blk = lambda: pl.BlockSpec((SEQ, LANES), lambda b, n: (b, n))
    return dict(
        act=blk,
        convw=pl.BlockSpec((RNN_CONV, LANES), lambda b, n: (0, n)),
        vec=lambda: pl.BlockSpec((1, LANES), lambda b, n: (0, n)),
        gate=lambda: pl.BlockSpec((None, LANES, LANES), lambda b, n: (n, 0, 0)),
    )


def _rnn_fwd(proj, cw, cb, wa, ba, wx, bx, lam):
    T = proj.shape[0]
    B = T // SEQ

    def body(x_ref, cw_ref, cb_ref, wa_ref, ba_ref, wx_ref, bx_ref, lam_ref, h_ref):
        row = lax.broadcasted_iota(jnp.int32, (SEQ, LANES), 0)
        xc, _, r, i, sp, a, s = _rnn_gates(x_ref[...], cw_ref, cb_ref[...], wa_ref[...], ba_ref[...],
                                          wx_ref[...], bx_ref[...], lam_ref[...], row)
        h_ref[...] = _scan_fwd(a, s * (i * xc), row)

    sp_ = _rnn_specs(B)
    return pl.pallas_call(
        body, name="rnn_fwd", out_shape=SDS((T, RNN_W), f32), grid=(B, RNN_BLOCKS),
        in_specs=[sp_["act"](), sp_["convw"], sp_["vec"](), sp_["gate"](), sp_["vec"](), sp_["gate"](),
                  sp_["vec"](), sp_["vec"]()],
        out_specs=sp_["act"](),
        compiler_params=_params(("parallel", "parallel")))(proj, cw, cb, wa, ba, wx, bx, lam)


def _rnn_bwd(proj, h, dh, cw, cb, wa, ba, wx, bx, lam, comm=None):
    T = proj.shape[0]
    B = T // SEQ

    def body(x_ref, h_ref, dh_ref, cw_ref, cb_ref, wa_ref, ba_ref, wx_ref, bx_ref, lam_ref,
             dx_ref, dcw_ref, dcb_ref, dwa_ref, dba_ref, dwx_ref, dbx_ref, dlam_ref):
        row = lax.broadcasted_iota(jnp.int32, (SEQ, LANES), 0)
        xr = x_ref[...]
        wa, wx, lam = wa_ref[...], wx_ref[...], lam_ref[...]
        xc, xcb, r, i, sp, a, s = _rnn_gates(xr, cw_ref, cb_ref[...], wa, ba_ref[...], wx, bx_ref[...], lam, row)
        hprev = _shift_dn(h_ref[...], 1, row)
        g = _scan_bwd(_shift_up(a, 1, row), dh_ref[...].astype(f32), row)
        da = g * hprev
        ds = g * (i * xc)
        di = g * (s * xc)
        dxc = g * (s * i)
        dla = da * a - ds * (a * a) / s
        dr = dla * (-LRU_C * sp)
        dsp = jnp.sum(dla * (-LRU_C * r), axis=0, keepdims=True)
        dlam = -dsp * _sigmoid(-lam)
        dga = dr * r * (1.0 - r)
        dgx = di * i * (1.0 - i)
        dgab, dgxb = dga.astype(bf16), dgx.astype(bf16)
        dwa = lax.dot_general(xcb, dgab, _DIMS["tn"], preferred_element_type=f32)
        dwx = lax.dot_general(xcb, dgxb, _DIMS["tn"], preferred_element_type=f32)
        dxc = dxc + lax.dot_general(dgab, wa.astype(bf16), _DIMS["nt"], preferred_element_type=f32)
        dxc = dxc + lax.dot_general(dgxb, wx.astype(bf16), _DIMS["nt"], preferred_element_type=f32)
        dx, dws, db = _conv_bwd(xr, cw_ref, dxc, row)
        dx_ref[...] = dx.astype(bf16)
        first = pl.program_id(1) == 0

        def acc(ref, val):
            @pl.when(first)
            def _():
                ref[...] = val

            @pl.when(jnp.logical_not(first))
            def _():
                ref[...] += val

        for k in range(RNN_CONV):
            acc(dcw_ref.at[k:k + 1, :], dws[k])
        acc(dcb_ref, db)
        acc(dwa_ref, dwa)
        acc(dba_ref, jnp.sum(dga, axis=0, keepdims=True))
        acc(dwx_ref, dwx)
        acc(dbx_ref, jnp.sum(dgx, axis=0, keepdims=True))
        acc(dlam_ref, dlam)

    blk = lambda: pl.BlockSpec((SEQ, LANES), lambda n, b: (b, n))
    convw = lambda: pl.BlockSpec((RNN_CONV, LANES), lambda n, b: (0, n))
    vec = lambda: pl.BlockSpec((1, LANES), lambda n, b: (0, n))
    gate = lambda: pl.BlockSpec((None, LANES, LANES), lambda n, b: (n, 0, 0))
    vshape = SDS((1, RNN_W), f32)
    gshape = SDS((RNN_BLOCKS, LANES, LANES), f32)
    return _call(
        body, (proj, h, dh, cw, cb, wa, ba, wx, bx, lam), name="rnn_bwd",
        out_shape=(SDS((T, RNN_W), bf16), SDS((RNN_CONV, RNN_W), f32), vshape, gshape, vshape, gshape, vshape, vshape),
        grid=(RNN_BLOCKS, B),
        in_specs=[blk(), blk(), blk(), convw(), vec(), gate(), vec(), gate(), vec(), vec()],
        out_specs=(blk(), convw(), vec(), gate(), vec(), gate(), vec(), vec()),
        semantics=("parallel", "arbitrary"), comm=comm)


def _t5_bucket(dist):
    max_exact = REL_BUCKETS // 2
    d = np.maximum(dist, 1).astype(np.float32)
    large = max_exact + np.log(d / max_exact) / math.log(REL_MAX_DIST / max_exact) * (REL_BUCKETS - max_exact)
    large = np.minimum(large.astype(np.int32), REL_BUCKETS - 1)
    return np.where(dist < max_exact, dist, large).astype(np.int32)


def _bucket_maps():
    qi = np.arange(ATT_BLK)[:, None]
    kj = np.arange(2 * ATT_BLK)[None, :]
    delta = ATT_BLK + qi - kj
    valid = (delta >= 0) & (delta <= ATT_BLK)
    maps = [np.where(valid, _t5_bucket(np.maximum(delta, 0) * r), -1) for r in DILATIONS]
    return np.stack(maps).astype(np.int32)


def _bias_tables(rel_bias, buckets):
    def body(rb_ref, bk_ref, o_ref):
        for g in range(NG):
            bk = bk_ref[g]
            for h in range(KVH):
                acc = jnp.full(bk.shape, NEG, f32)
                for b in range(REL_BUCKETS):
                    acc = jnp.where(bk == b, rb_ref[b, g * KVH + h], acc)
                o_ref[h, g] = acc

    return pl.pallas_call(
        body, name="bias_tables", out_shape=SDS((KVH, NG, ATT_BLK, 2 * ATT_BLK), f32),
        in_specs=[pl.BlockSpec(memory_space=pltpu.SMEM), pl.BlockSpec(memory_space=pltpu.VMEM)],
        out_specs=pl.BlockSpec(memory_space=pltpu.VMEM), compiler_params=_params())(rel_bias, buckets)


def _bias_grad(dbias, buckets):
    def body(db_ref, bk_ref, o_ref):
        rr = lax.broadcasted_iota(jnp.int32, (REL_BUCKETS, NG * KVH), 0)
        cc = lax.broadcasted_iota(jnp.int32, (REL_BUCKETS, NG * KVH), 1)
        out = jnp.zeros((REL_BUCKETS, NG * KVH), f32)
        for g in range(NG):
            bk = bk_ref[g]
            for h in range(KVH):
                d = db_ref[h, g]
                for b in range(REL_BUCKETS):
                    m = jnp.where(bk == b, d, 0.0)
                    s = jnp.sum(jnp.sum(m, axis=1, keepdims=True), axis=0, keepdims=True)
                    out = jnp.where((rr == b) & (cc == g * KVH + h), s, out)
        o_ref[...] = out

    return pl.pallas_call(body, name="bias_grad", out_shape=SDS((REL_BUCKETS, NG * KVH), f32),
                          compiler_params=_params())(dbias, buckets)


def _to_sub(dst_ref, src_ref, r, dtype, offset=0):
    M = SEQ // r
    for c in range(r):
        if r == 1:
            v = src_ref[...]
        else:
            v = src_ref[pl.ds(c, M, stride=r), :]
        dst_ref[pl.ds(offset + c * M, M), :] = v.astype(dtype)


def _from_sub(dst_ref, src_ref, r, accumulate=False, offset=0):
    M = SEQ // r
    for c in range(r):
        v = src_ref[pl.ds(offset + c * M, M), :]
        idx = slice(None) if r == 1 else pl.ds(c, M, stride=r)
        if accumulate:
            dst_ref[idx, :] = dst_ref[idx, :] + v
        else:
            dst_ref[idx, :] = v


_COL = lambda k: slice(k * HD, (k + 1) * HD)
SCALE = HD ** -0.5


def _qkv_spec(k, bh):
    def index(*ids):
        b, h = bh(*ids)
        return (b, C_ATT // HD + 5 * h + k)

    return pl.BlockSpec((SEQ, HD), index)


def _key_window(bias_ref, g, nb):
    if nb == 1:
        bias_own = bias_ref[g, :, ATT_BLK:2 * ATT_BLK]
        return lambda j: (pl.ds(pl.multiple_of((j + 1) * ATT_BLK, ATT_BLK), ATT_BLK), bias_own)
    bias_g = bias_ref[g]
    col = lax.broadcasted_iota(jnp.int32, bias_g.shape, 1)
    bias_first = jnp.where(col >= ATT_BLK, bias_g, NEG)
    return lambda j: (pl.ds(pl.multiple_of(j * ATT_BLK, ATT_BLK), 2 * ATT_BLK),
                      jnp.where(j % nb != 0, bias_g, bias_first))


def _att_fwd(proj, bias, comm=None):
    T = proj.shape[0]
    B = T // SEQ

    def body(q0_ref, q1_ref, q2_ref, k_ref, v_ref, bias_ref, o_ref, lse_ref, qp, kp, vp, op, lp, og, lg):
        q_refs = (q0_ref, q1_ref, q2_ref)
        kp[0:ATT_BLK, :] = jnp.zeros((ATT_BLK, HD), bf16)
        vp[0:ATT_BLK, :] = jnp.zeros((ATT_BLK, HD), bf16)
        for g, r in enumerate(DILATIONS):
            nb = NBLK_SEQ // r
            _to_sub(qp, q_refs[g], r, bf16)
            _to_sub(kp, k_ref, r, bf16, offset=ATT_BLK)
            _to_sub(vp, v_ref, r, bf16, offset=ATT_BLK)
            keys = _key_window(bias_ref, g, nb)

            def step(j, carry):
                cur = pl.ds(pl.multiple_of(j * ATT_BLK, ATT_BLK), ATT_BLK)
                win, bias_j = keys(j)
                s = lax.dot_general(qp[cur, :], kp[win, :], _DIMS["nt"], preferred_element_type=f32) * SCALE + bias_j
                m = jnp.max(s, axis=-1, keepdims=True)
                p = jnp.exp(s - m)
                den = jnp.sum(p, axis=-1, keepdims=True)
                o = jnp.dot(p.astype(bf16), vp[win, :], preferred_element_type=f32)
                op[cur, :] = o / den
                lp[cur, :] = jnp.broadcast_to(m + jnp.log(den), (ATT_BLK, HD))
                return carry

            lax.fori_loop(0, NBLK_SEQ, step, 0, unroll=2)
            _from_sub(og.at[g], op, r)
            _from_sub(lg.at[g], lp, r)
        l0, l1, l2 = lg[0], lg[1], lg[2]
        mx = jnp.maximum(jnp.maximum(l0, l1), l2)
        e0, e1, e2 = jnp.exp(l0 - mx), jnp.exp(l1 - mx), jnp.exp(l2 - mx)
        den = e0 + e1 + e2
        o_ref[...] = (e0 * og[0] + e1 * og[1] + e2 * og[2]) / den
        lse_ref[...] = mx + jnp.log(den)

    return _call(
        body, (proj, proj, proj, proj, proj, bias), name="att_fwd",
        out_shape=(SDS((T, KVH * HD), f32), SDS((KVH, T, HD), f32)), grid=(B, KVH),
        in_specs=[_qkv_spec(k, lambda b, h: (b, h)) for k in range(5)]
                 + [pl.BlockSpec((None, NG, ATT_BLK, 2 * ATT_BLK), lambda b, h: (h, 0, 0, 0))],
        out_specs=(pl.BlockSpec((SEQ, HD), lambda b, h: (b, h)),
                   pl.BlockSpec((None, SEQ, HD), lambda b, h: (h, b, 0))),
        scratch_shapes=[pltpu.VMEM((SEQ, HD), bf16)] + [pltpu.VMEM((SEQ + ATT_BLK, HD), bf16)] * 2
                       + [pltpu.VMEM((SEQ, HD), f32)] * 2 + [pltpu.VMEM((NG, SEQ, HD), f32)] * 2,
        semantics=("parallel", "parallel"), comm=comm)


def _att_bwd(proj, bias, o, lse, do, comm=None):
    T = proj.shape[0]
    B = T // SEQ

    def body(q0_ref, q1_ref, q2_ref, k_ref, v_ref, bias_ref, o_ref, lse_ref, do_ref, dx_ref, db_ref,
             qp, kp, vp, dop, lp, dlp, dqp, dkp, dvp, dln, nat, dkn, dvn):
        q_refs = (q0_ref, q1_ref, q2_ref)
        first = pl.program_id(1) == 0

        @pl.when(first)
        def _():
            db_ref[...] = jnp.zeros_like(db_ref)

        dln[...] = jnp.broadcast_to(jnp.sum(do_ref[...] * o_ref[...], axis=-1, keepdims=True), (SEQ, HD))
        dkn[...] = jnp.zeros_like(dkn)
        dvn[...] = jnp.zeros_like(dvn)
        kp[0:ATT_BLK, :] = jnp.zeros((ATT_BLK, HD), bf16)
        vp[0:ATT_BLK, :] = jnp.zeros((ATT_BLK, HD), bf16)
        for g, r in enumerate(DILATIONS):
            nb = NBLK_SEQ // r
            _to_sub(qp, q_refs[g], r, bf16)
            _to_sub(kp, k_ref, r, bf16, offset=ATT_BLK)
            _to_sub(vp, v_ref, r, bf16, offset=ATT_BLK)
            _to_sub(dop, do_ref, r, bf16)
            _to_sub(lp, lse_ref, r, f32)
            _to_sub(dlp, dln, r, f32)
            dkp[...] = jnp.zeros_like(dkp)
            dvp[...] = jnp.zeros_like(dvp)
            keys = _key_window(bias_ref, g, nb)
            db_cols = slice(ATT_BLK, 2 * ATT_BLK) if nb == 1 else slice(None)

            def step(j, carry):
                cur = pl.ds(pl.multiple_of(j * ATT_BLK, ATT_BLK), ATT_BLK)
                win, bias_j = keys(j)
                q, kw, vw, dob = qp[cur, :], kp[win, :], vp[win, :], dop[cur, :]
                s = lax.dot_general(q, kw, _DIMS["nt"], preferred_element_type=f32) * SCALE + bias_j
                p = jnp.exp(s - lp[cur, 0:1])
                dp = lax.dot_general(dob, vw, _DIMS["nt"], preferred_element_type=f32)
                ds = p * (dp - dlp[cur, 0:1])
                db_ref[g, :, db_cols] += ds
                dsb, pb = ds.astype(bf16), p.astype(bf16)
                dqp[cur, :] = jnp.dot(dsb, kw, preferred_element_type=f32) * SCALE
                dkp[win, :] += lax.dot_general(dsb, q, _DIMS["tn"], preferred_element_type=f32) * SCALE
                dvp[win, :] += lax.dot_general(pb, dob, _DIMS["tn"], preferred_element_type=f32)
                return carry

            lax.fori_loop(0, NBLK_SEQ, step, 0, unroll=2)
            _from_sub(nat, dqp, r)
            dx_ref[:, _COL(g)] = nat[...].astype(bf16)
            _from_sub(dkn, dkp, r, accumulate=True, offset=ATT_BLK)
            _from_sub(dvn, dvp, r, accumulate=True, offset=ATT_BLK)
        dx_ref[:, _COL(3)] = dkn[...].astype(bf16)
        dx_ref[:, _COL(4)] = dvn[...].astype(bf16)

    blk = lambda: pl.BlockSpec((SEQ, HD), lambda h, b: (b, h))
    bias_spec = lambda: pl.BlockSpec((None, NG, ATT_BLK, 2 * ATT_BLK), lambda h, b: (h, 0, 0, 0))
    pad = lambda dtype: pltpu.VMEM((SEQ + ATT_BLK, HD), dtype)
    return _call(
        body, (proj, proj, proj, proj, proj, bias, o, lse, do), name="att_bwd",
        out_shape=(SDS((T, KVH * ATT_COLS), bf16), SDS((KVH, NG, ATT_BLK, 2 * ATT_BLK), f32)), grid=(KVH, B),
        in_specs=[_qkv_spec(k, lambda h, b: (b, h)) for k in range(5)]
                 + [bias_spec(), blk(), pl.BlockSpec((None, SEQ, HD), lambda h, b: (h, b, 0)), blk()],
        out_specs=(pl.BlockSpec((SEQ, ATT_COLS), lambda h, b: (b, h)), bias_spec()),
        scratch_shapes=[pltpu.VMEM((SEQ, HD), bf16), pad(bf16), pad(bf16), pltpu.VMEM((SEQ, HD), bf16)]
                       + [pltpu.VMEM((SEQ, HD), f32)] * 3 + [pad(f32)] * 2 + [pltpu.VMEM((SEQ, HD), f32)] * 4,
        semantics=("parallel", "arbitrary"), comm=comm)


MERGE_ROWS, MERGE_COLS = 1024, 256
_G_RNN_BLK = C_GATE // MERGE_COLS
_G_ATT_BLK = (C_GATE + D) // MERGE_COLS


def _merge_specs():
    cols = lambda off: pl.BlockSpec((MERGE_ROWS, MERGE_COLS), lambda i, j: (i, off + j))
    return cols(_G_RNN_BLK), cols(_G_ATT_BLK), cols(0)


def _merge_fwd(proj, pr, pa):
    def body(gr_ref, ga_ref, pr_ref, pa_ref, o_ref):
        o_ref[...] = (_sigmoid(gr_ref[...]) * pr_ref[...].astype(f32)
                      + _sigmoid(ga_ref[...]) * pa_ref[...].astype(f32)).astype(bf16)

    T = proj.shape[0]
    s_gr, s_ga, s0 = _merge_specs()
    return pl.pallas_call(body, name="merge_fwd", out_shape=SDS((T, D), bf16),
                          grid=(T // MERGE_ROWS, D // MERGE_COLS),
                          in_specs=[s_gr, s_ga, s0, s0], out_specs=s0,
                          compiler_params=_params(("parallel", "parallel")))(proj, proj, pr, pa)


def _merge_bwd(proj, pr, pa, dm):
    nj = D // MERGE_COLS

    def body(g_ref, pr_ref, pa_ref, dm_ref, dp_ref, dg_ref):
        dm_ = dm_ref[...].astype(f32)
        s = _sigmoid(g_ref[...])
        p = jnp.where(pl.program_id(1) < nj, pr_ref[...], pa_ref[...]).astype(f32)
        dp_ref[...] = (dm_ * s).astype(bf16)
        dg_ref[...] = (dm_ * p * s * (1.0 - s)).astype(bf16)

    T = proj.shape[0]
    blk = (MERGE_ROWS, MERGE_COLS)
    wrap = pl.BlockSpec(blk, lambda i, j: (i, j % nj))
    pr_spec = pl.BlockSpec(blk, lambda i, j: (i, jnp.minimum(j, nj - 1)))
    pa_spec = pl.BlockSpec(blk, lambda i, j: (i, jnp.maximum(j - nj, 0)))
    out = pl.BlockSpec(blk, lambda i, j: (i, j))
    return pl.pallas_call(
        body, name="merge_bwd", out_shape=(SDS((T, 2 * D), bf16), SDS((T, 2 * D), bf16)),
        grid=(T // MERGE_ROWS, 2 * nj),
        in_specs=[pl.BlockSpec(blk, lambda i, j: (i, _G_RNN_BLK + j)), pr_spec, pa_spec, wrap], out_specs=(out, out),
        compiler_params=_params(("parallel", "parallel")))(proj, pr, pa, dm)


FFN_COLS = 256
GELU_C = math.sqrt(2.0 / math.pi)
GELU_A = 0.044715


def _gelu_parts(x):
    t = jnp.tanh(GELU_C * (x + GELU_A * x * x * x))
    return 0.5 * x * (1.0 + t), t


def _ffn_act_fwd(gpre, up, cw, cb):
    def body(g_ref, u_ref, cw_ref, cb_ref, o_ref):
        row = lax.broadcasted_iota(jnp.int32, (SEQ, FFN_COLS), 0)
        gate = _conv_fwd(g_ref[...].astype(f32), cw_ref, cb_ref[...], row)
        o_ref[...] = (_gelu_parts(gate)[0] * u_ref[...].astype(f32)).astype(bf16)

    T = gpre.shape[0]
    blk = lambda: pl.BlockSpec((SEQ, FFN_COLS), lambda b, j: (b, j))
    return pl.pallas_call(
        body, name="ffn_act_fwd", out_shape=SDS((T, FFN_W), bf16), grid=(T // SEQ, FFN_W // FFN_COLS),
        in_specs=[blk(), blk(), pl.BlockSpec((FFN_CONV, FFN_COLS), lambda b, j: (0, j)),
                  pl.BlockSpec((1, FFN_COLS), lambda b, j: (0, j))],
        out_specs=blk(), compiler_params=_params(("parallel", "parallel")))(gpre, up, cw, cb)


def _ffn_act_bwd(gpre, up, cw, cb, dact):
    def body(g_ref, u_ref, cw_ref, cb_ref, da_ref, dg_ref, du_ref, dcw_ref, dcb_ref):
        row = lax.broadcasted_iota(jnp.int32, (SEQ, FFN_COLS), 0)
        gp = g_ref[...].astype(f32)
        gate = _conv_fwd(gp, cw_ref, cb_ref[...], row)
        gel, t = _gelu_parts(gate)
        da = da_ref[...].astype(f32)
        du_ref[...] = (da * gel).astype(bf16)
        dgel = 0.5 * (1.0 + t) + 0.5 * gate * (1.0 - t * t) * (GELU_C * (1.0 + 3.0 * GELU_A * gate * gate))
        dgate = da * u_ref[...].astype(f32) * dgel
        dx, dws, db = _conv_bwd(gp, cw_ref, dgate, row)
        dg_ref[...] = dx.astype(bf16)
        first = pl.program_id(1) == 0

        def acc(ref, val):
            @pl.when(first)
            def _():
                ref[...] = val

            @pl.when(jnp.logical_not(first))
            def _():
                ref[...] += val

        for k in range(FFN_CONV):
            acc(dcw_ref.at[k:k + 1, :], dws[k])
        acc(dcb_ref, db)

    T = gpre.shape[0]
    blk = lambda: pl.BlockSpec((SEQ, FFN_COLS), lambda j, b: (b, j))
    cws = lambda: pl.BlockSpec((FFN_CONV, FFN_COLS), lambda j, b: (0, j))
    cbs = lambda: pl.BlockSpec((1, FFN_COLS), lambda j, b: (0, j))
    return pl.pallas_call(
        body, name="ffn_act_bwd",
        out_shape=(SDS((T, FFN_W), bf16), SDS((T, FFN_W), bf16), SDS((FFN_CONV, FFN_W), f32), SDS((1, FFN_W), f32)),
        grid=(FFN_W // FFN_COLS, T // SEQ),
        in_specs=[blk(), blk(), cws(), cbs(), blk()], out_specs=(blk(), blk(), cws(), cbs()),
        compiler_params=_params(("parallel", "arbitrary")))(gpre, up, cw, cb, dact)


def _coords():
    return lax.axis_index("x"), lax.axis_index("y"), lax.axis_index("c")


def _dev_index(dev):
    return 4 * dev[0] + 2 * dev[1] + dev[2]


def _dma_sems(n):
    return [pltpu.SemaphoreType.DMA((n,)), pltpu.SemaphoreType.DMA((n,))]


def _gather_two_level(arrays):
    n = len(arrays)

    def plan(ins, outs, sems):
        send_sems, recv_sems, local_sems = sems
        x, y, c = _coords()
        me, sibling = (x, y, c), (x, y, 1 - c)
        chips = [(1 - x, y), (x, 1 - y), (1 - x, 1 - y)]

        def copy(a, k, block, to, own=False):
            dst = outs[a].at[_dev_index(block)]
            return pltpu.make_async_remote_copy(
                src_ref=ins[a] if own else dst, dst_ref=dst, send_sem=send_sems.at[7 * a + k],
                recv_sem=recv_sems.at[7 * a + k], device_id=to, device_id_type=MESH)

        mine = [pltpu.make_async_copy(ins[a], outs[a].at[_dev_index(me)], local_sems.at[a]) for a in range(n)]
        first = [copy(a, 0, me, sibling, own=True) for a in range(n)]
        first += [copy(a, 1 + j, me, (*chip, c), own=True) for a in range(n) for j, chip in enumerate(chips)]
        passed = [[copy(a, 4 + j, (*chip, c), sibling) for a in range(n)] for j, chip in enumerate(chips)]
        arrive_ici = [[copy(a, 1 + j, (*chip, c), me) for a in range(n)] for j, chip in enumerate(chips)]
        arrive_d2d = [copy(a, 0, sibling, me) for a in range(n)]
        arrive_d2d += [copy(a, 4 + j, (*chip, 1 - c), me) for a in range(n) for j, chip in enumerate(chips)]
        return mine, first, passed, arrive_ici, arrive_d2d

    def start(ins, outs, sems):
        mine, first, _, _, _ = plan(ins, outs, sems)
        for cp in mine + first:
            cp.start()

    def finish(ins, outs, sems):
        mine, first, passed, arrive_ici, arrive_d2d = plan(ins, outs, sems)
        for j in range(3):
            for cp in arrive_ici[j]:
                cp.wait_recv()
            for cp in passed[j]:
                cp.start()
        for cp in arrive_d2d:
            cp.wait_recv()
        for cp in first + [cp for group in passed for cp in group]:
            cp.wait_send()
        for cp in mine:
            cp.wait()

    return _Comm(arrays, [SDS((N_DEV,) + a.shape, a.dtype) for a in arrays],
                 _dma_sems(7 * n) + [pltpu.SemaphoreType.DMA((n,))], start, finish)


def _gather_direct(arrays):
    n = len(arrays)

    def plan(ins, outs, sems):
        send_sems, recv_sems, local_sems = sems
        x, y, c = _coords()
        me = (x, y, c)
        mine = [pltpu.make_async_copy(ins[a], outs[a].at[_dev_index(me)], local_sems.at[a]) for a in range(n)]
        sends, arrivals = [], []
        for a in range(n):
            for k in range(1, N_DEV):
                peer = (1 - x if k & 4 else x, 1 - y if k & 2 else y, 1 - c if k & 1 else c)
                s = 7 * a + k - 1
                for slot, out in ((me, sends), (peer, arrivals)):
                    out.append(pltpu.make_async_remote_copy(
                        src_ref=ins[a], dst_ref=outs[a].at[_dev_index(slot)], send_sem=send_sems.at[s],
                        recv_sem=recv_sems.at[s], device_id=peer, device_id_type=MESH))
        return mine, sends, arrivals

    def start(ins, outs, sems):
        mine, sends, _ = plan(ins, outs, sems)
        for cp in mine + sends:
            cp.start()

    def finish(ins, outs, sems):
        mine, sends, arrivals = plan(ins, outs, sems)
        for cp in arrivals:
            cp.wait_recv()
        for cp in sends:
            cp.wait_send()
        for cp in mine:
            cp.wait()

    return _Comm(arrays, [SDS((N_DEV,) + a.shape, a.dtype) for a in arrays],
                 _dma_sems(7 * n) + [pltpu.SemaphoreType.DMA((n,))], start, finish)


def _exchange(arrays, n_blocks, route):
    n = len(arrays)

    def plan(ins, outs, sems):
        send_sems, recv_sems = sems
        cps = []
        for a in range(n):
            for j, (src, peer) in enumerate(route(*_coords())):
                cps.append(pltpu.make_async_remote_copy(
                    src_ref=ins[a].at[src], dst_ref=outs[a].at[j], send_sem=send_sems.at[n_blocks * a + j],
                    recv_sem=recv_sems.at[n_blocks * a + j], device_id=peer, device_id_type=MESH))
        return cps

    def start(ins, outs, sems):
        for cp in plan(ins, outs, sems):
            cp.start()

    def finish(ins, outs, sems):
        for cp in plan(ins, outs, sems):
            cp.wait()

    return _Comm(arrays, [SDS((n_blocks,) + a.shape[1:], a.dtype) for a in arrays], _dma_sems(n_blocks * n),
                 start, finish)


def _sibling_exchange(arrays):
    return _exchange(arrays, 4, lambda x, y, c: [(2 * k + 1 - c, (x, y, 1 - c)) for k in range(4)])


def _chip_exchange(arrays):
    return _exchange(arrays, 3, lambda x, y, c: [(2 * cx + cy, (cx, cy, c))
                                                 for cx, cy in ((1 - x, y), (x, 1 - y), (1 - x, 1 - y))])


def _run(comm, name):
    def body(*refs):
        k_in, k_out = len(comm.inputs), len(comm.out_shapes)
        ins, outs, sems = refs[:k_in], refs[k_in:k_in + k_out], refs[k_in + k_out:]
        comm.start(ins, outs, sems)
        comm.finish(ins, outs, sems)

    return pl.pallas_call(body, name=name, out_shape=comm.out_shapes, in_specs=[ANY] * len(comm.inputs),
                          out_specs=(ANY,) * len(comm.out_shapes), scratch_shapes=comm.sem_shapes)(*comm.inputs)


TILE_ELEMS = 192 * 1024


def _row_tile(R, C):
    if R * C <= TILE_ELEMS:
        return R
    return max(t for t in range(SUBLANES, R, SUBLANES) if R % t == 0 and t * C <= TILE_ELEMS)


def _pair_sum(g8, recv, c_idx, name):
    R, C = g8.shape[-2:]
    t = _row_tile(R, C)

    def body(c_ref, g_ref, r_ref, o_ref):
        o_ref[...] = (g_ref[...] + r_ref[...]).astype(bf16)

    return pl.pallas_call(
        body, name="pair_sum_" + name, out_shape=SDS((4, R, C), bf16),
        grid_spec=pltpu.PrefetchScalarGridSpec(
            num_scalar_prefetch=1, grid=(4, R // t),
            in_specs=[pl.BlockSpec((None, t, C), lambda k, i, c: (2 * k + c[0], i, 0)),
                      pl.BlockSpec((None, t, C), lambda k, i, c: (k, i, 0))],
            out_specs=pl.BlockSpec((None, t, C), lambda k, i, c: (k, i, 0))),
        compiler_params=_params(("parallel", "parallel")))(c_idx, g8, recv)


def _adamw_math(w, g, m, v):
    m = ADAM_B1 * m + (1.0 - ADAM_B1) * g
    v = ADAM_B2 * v + (1.0 - ADAM_B2) * (g * g)
    m_hat = m / (1.0 - ADAM_B1 ** ADAM_STEP)
    v_hat = v / (1.0 - ADAM_B2 ** ADAM_STEP)
    delta = -ADAM_LR * (m_hat / (jnp.sqrt(v_hat) + ADAM_EPS) + ADAM_WD * w)
    return delta, m, v


def _adamw_sharded(pa, recv, k_idx, w, m, v, name):
    R, C = w.shape
    t = _row_tile(R, C)

    def body(k_ref, p_ref, r_ref, w_ref, m_ref, v_ref, g_ref, d_ref, nm_ref, nv_ref):
        g = p_ref[...].astype(f32)
        for j in range(3):
            g = g + r_ref[j].astype(f32)
        d, nm, nv = _adamw_math(w_ref[...], g, m_ref[...], v_ref[...])
        g_ref[...], d_ref[...], nm_ref[...], nv_ref[...] = g, d, nm, nv

    tile = lambda: pl.BlockSpec((t, C), lambda i, k: (i, 0))
    return pl.pallas_call(
        body, name="adamw_" + name, out_shape=(SDS((R, C), f32),) * 4,
        grid_spec=pltpu.PrefetchScalarGridSpec(
            num_scalar_prefetch=1, grid=(R // t,),
            in_specs=[pl.BlockSpec((None, t, C), lambda i, k: (k[0], i, 0)),
                      pl.BlockSpec((3, t, C), lambda i, k: (0, i, 0)), tile(), tile(), tile()],
            out_specs=(tile(), tile(), tile(), tile())),
        compiler_params=_params(("parallel",)))(k_idx, pa, recv, w, m, v)


def _adamw_replicated(parts, ws, ms, vs):
    n = len(ws)

    def body(*refs):
        p, w, m, v = (refs[i * n:(i + 1) * n] for i in range(4))
        outs = refs[4 * n:]
        for a in range(n):
            g = p[a][0].astype(f32)
            for j in range(1, N_DEV):
                g = g + p[a][j].astype(f32)
            d, nm, nv = _adamw_math(w[a][...], g, m[a][...], v[a][...])
            for i, val in enumerate((g, d, nm, nv)):
                outs[i * n + a][...] = val

    shapes = tuple(SDS(w.shape, f32) for w in ws)
    res = pl.pallas_call(body, name="adamw_replicated", out_shape=shapes * 4,
                         compiler_params=_params())(*parts, *ws, *ms, *vs)
    return [res[i * n:(i + 1) * n] for i in range(4)]


def _cols_to_full(g):
    n, r, c = g.shape
    return g.transpose(1, 0, 2).reshape(r, n * c)


def _full_to_cols(a):
    r, c = a.shape
    return a.reshape(r, N_DEV, c // N_DEV).transpose(1, 0, 2)


def _rows_blocked(a):
    r, c = a.shape
    return a.reshape(N_DEV, r // N_DEV, c)


def _w_in_to_internal(w):
    K = w.shape[0]
    q = w[:, 1280:2816].reshape(K, NG, KVH, 1, HD).transpose(0, 2, 1, 3, 4).reshape(K, KVH, NG, HD)
    k = w[:, 2816:3328].reshape(K, KVH, 1, HD)
    v = w[:, 3328:3840].reshape(K, KVH, 1, HD)
    att = jnp.concatenate([q, k, v], axis=2).reshape(K, KVH * ATT_COLS)
    return jnp.concatenate([w[:, :1280], att, w[:, 3840:]], axis=1)


def _w_in_from_internal(w):
    K = w.shape[0]
    att = w[:, C_ATT:C_GATE].reshape(K, KVH, 5, HD)
    q = att[:, :, 0:3].transpose(0, 2, 1, 3).reshape(K, NG * KVH * HD)
    k = att[:, :, 3].reshape(K, KVH * HD)
    v = att[:, :, 4].reshape(K, KVH * HD)
    return jnp.concatenate([w[:, :C_ATT], q, k, v, w[:, C_GATE:]], axis=1)


_IN_NAMES = ('x', 'rel_bias', 'norm_mix_pre', 'norm_mix_post', 'w_in', 'conv_rnn_w', 'conv_rnn_b', 'w_rg_a', 'b_rg_a',
             'w_rg_x', 'b_rg_x', 'lru_lambda', 'w_branch_rnn', 'w_branch_att', 'w_out', 'norm_ffn_pre',
             'norm_ffn_post', 'w_ffn_gate', 'w_ffn_up', 'conv_ffn_w', 'conv_ffn_b', 'w_ffn_down')
_WEIGHTS = _IN_NAMES[1:]
_SHARDED = {"w_in": "col", "conv_rnn_w": "col", "w_branch_rnn": "row", "w_branch_att": "col", "w_out": "row",
            "w_ffn_gate": "col", "w_ffn_up": "col", "conv_ffn_w": "col", "w_ffn_down": "row"}
_REPLICATED = tuple(n for n in _WEIGHTS if n not in _SHARDED)
_EARLY = ("w_branch_rnn", "w_branch_att", "w_out", "w_ffn_gate", "w_ffn_up", "conv_ffn_w", "w_ffn_down")
_LATE = ("w_in", "conv_rnn_w")


def _flat2(a):
    return a.reshape(-1, a.shape[-1])


def _train_step(inp):
    x_idx, y_idx, c_idx = _coords()
    W = {n: inp[n] for n in _WEIGHTS}
    x = inp["x"].reshape(-1, D)
    target = inp["loss_target"].reshape(-1, D)
    shard = {n: inp[n][0] for n in _SHARDED}

    g_in, g_cr, g_cf = _run(_gather_two_level([shard["w_in"].astype(bf16), shard["conv_rnn_w"],
                                               shard["conv_ffn_w"]]), "ag_w_in")
    w_in = _w_in_to_internal(_cols_to_full(g_in))
    cw_rnn, cw_ffn = _cols_to_full(g_cr), _cols_to_full(g_cf)
    later = ("w_branch_rnn", "w_branch_att", "w_out", "w_ffn_gate", "w_ffn_up", "w_ffn_down")
    ag_rest = _gather_direct([shard[n].astype(bf16) for n in later])

    wa, wx = W["w_rg_a"][0], W["w_rg_x"][0]
    buckets = jnp.asarray(_bucket_maps())

    hn = _norm_in(x, W["norm_mix_pre"])
    proj = _mm(hn, w_in, "nn", f32, "mm_proj", 512, IN_W // 2, 1024, cols_outer=True)
    h_rnn = _rnn_fwd(proj, cw_rnn, W["conv_rnn_b"], wa, W["b_rg_a"], wx, W["b_rg_x"], W["lru_lambda"])
    bias = _bias_tables(W["rel_bias"], buckets)
    (o_att, lse), gathered = _att_fwd(proj, bias, comm=ag_rest)
    gathered = dict(zip(later, gathered))
    w_brnn = gathered["w_branch_rnn"].reshape(RNN_W, D)
    w_batt = _cols_to_full(gathered["w_branch_att"])
    w_out = gathered["w_out"].reshape(D, D)
    w_gate, w_up = _cols_to_full(gathered["w_ffn_gate"]), _cols_to_full(gathered["w_ffn_up"])
    w_down = gathered["w_ffn_down"].reshape(FFN_W, D)
    pr = _mm(h_rnn, w_brnn, "nn", bf16, "mm_pr", 1024, 1024, 1280)
    pa = _mm(o_att, w_batt, "nn", bf16, "mm_pa", 1024, 1024, 512)
    merged = _merge_fwd(proj, pr, pa)
    mix = _mm(merged, w_out, "nn", f32, "mm_mix", 1024, 1024, 1024)
    h1, hn2 = _mid_fwd(x, mix, W["norm_mix_post"], W["norm_ffn_pre"])
    gpre = _mm(hn2, w_gate, "nn", bf16, "mm_gate", 1024, 1024, 1024, cols_outer=True)
    up = _mm(hn2, w_up, "nn", bf16, "mm_up", 1024, 1024, 1024, cols_outer=True)
    act = _ffn_act_fwd(gpre, up, cw_ffn, W["conv_ffn_b"])
    ff = _mm(act, w_down, "nn", f32, "mm_down", 1024, 1024, 1024)
    loss_part, dy, dff, dg_fpost = _final(h1, ff, W["norm_ffn_post"], target)

    grads = {}
    dact = _mm(dff, w_down, "nt", bf16, "mm_dact", 1024, 1024, 1024, cols_outer=True)
    grads["w_ffn_down"] = _rows_blocked(_mm(act, dff, "tn", f32, "mm_dw_down", 1024, 1024, 2048))
    dgpre, dup, dcw_ffn, dcb_ffn = _ffn_act_bwd(gpre, up, cw_ffn, W["conv_ffn_b"], dact)
    grads["conv_ffn_w"] = _full_to_cols(dcw_ffn)
    grads["w_ffn_gate"] = _full_to_cols(_mm(hn2, dgpre, "tn", f32, "mm_dw_gate", 1024, 1024, 2048))
    grads["w_ffn_up"] = _full_to_cols(_mm(hn2, dup, "tn", f32, "mm_dw_up", 1024, 1024, 2048))
    dhn2 = _mm(dgpre, w_gate, "nt", f32, "mm_dhn2_a", 1024, 1024, 1024)
    dhn2 = _mm(dup, w_up, "nt", f32, "mm_dhn2_b", 1024, 1024, 1024, add=dhn2)
    dh1, dmix, dg_fpre, dg_post = _mid_bwd(dy, dhn2, h1, W["norm_ffn_pre"], mix, W["norm_mix_post"])
    dmerged = _mm(dmix, w_out, "nt", bf16, "mm_dmerged", 1024, 1024, 1024)
    grads["w_out"] = _rows_blocked(_mm(merged, dmix, "tn", f32, "mm_dw_out", 1024, 1024, 2048))
    dprpa, dgates = _merge_bwd(proj, pr, pa, dmerged)
    dpr, dpa = dprpa[:, :D], dprpa[:, D:]
    dh_rnn = _mm(dpr, w_brnn, "nt", bf16, "mm_dh_rnn", 1024, 1280, 1024)
    grads["w_branch_rnn"] = _rows_blocked(_mm(h_rnn, dpr, "tn", f32, "mm_dw_brnn", 1280, 1024, 1024))
    do_att = _mm(dpa, w_batt, "nt", f32, "mm_do_att", 1024, 512, 1024)
    grads["w_branch_att"] = _full_to_cols(_mm(o_att, dpa, "tn", f32, "mm_dw_batt", 512, 1024, 2048))

    c_arr = jnp.reshape(c_idx, (1,)).astype(jnp.int32)
    k_arr = jnp.reshape(2 * x_idx + y_idx, (1,)).astype(jnp.int32)
    (dqkv, dbias), from_sibling = _att_bwd(proj, bias, o_att, lse, do_att,
                                           comm=_sibling_exchange([grads[n] for n in _EARLY]))
    pair = {n: _pair_sum(grads[n], r, c_arr, n) for n, r in zip(_EARLY, from_sibling)}
    drel = _bias_grad(dbias, buckets)
    (dxr, dcw_rnn, dcb_rnn, dwa, dba, dwx, dbx, dlam), from_chips = _rnn_bwd(
        proj, h_rnn, dh_rnn, cw_rnn, W["conv_rnn_b"], wa, W["b_rg_a"], wx, W["b_rg_x"], W["lru_lambda"],
        comm=_chip_exchange([pair[n] for n in _EARLY]))
    from_chips = dict(zip(_EARLY, from_chips))
    gsmall = {"rel_bias": drel, "norm_mix_post": dg_post, "conv_rnn_b": dcb_rnn, "w_rg_a": dwa.astype(bf16),
              "b_rg_a": dba, "w_rg_x": dwx.astype(bf16), "b_rg_x": dbx, "lru_lambda": dlam,
              "norm_ffn_pre": dg_fpre, "norm_ffn_post": dg_fpost, "conv_ffn_b": dcb_ffn}
    dw_in_a, parts = _mm(hn, dqkv, "tn", f32, "mm_dw_in_a", 1024, 1280, 1024,
                         comm=_gather_direct([_flat2(gsmall[n]) for n in gsmall]))
    parts = dict(zip(gsmall, parts))
    dw_in = jnp.concatenate([_mm(hn, dxr, "tn", f32, "mm_dw_in_r", 1024, 1280, 1024), dw_in_a,
                             _mm(hn, dgates, "tn", f32, "mm_dw_in_g", 1024, 1024, 2048)], axis=1)
    grads["w_in"] = _full_to_cols(_w_in_from_internal(dw_in))
    grads["conv_rnn_w"] = _full_to_cols(dcw_rnn)
    dhn, from_sibling = _mm(dxr, w_in[:, :C_ATT], "nt", f32, "mm_dhn_r", 1024, 1024, 1280,
                            comm=_sibling_exchange([grads[n] for n in _LATE]))
    pair.update({n: _pair_sum(grads[n], r, c_arr, n) for n, r in zip(_LATE, from_sibling)})
    dhn, late = _mm(dqkv, w_in[:, C_ATT:C_GATE], "nt", f32, "mm_dhn_a", 1024, 1024, 1280, add=dhn,
                    comm=_chip_exchange([pair[n] for n in _LATE]))
    from_chips.update(zip(_LATE, late))
    dhn = _mm(dgates, w_in[:, C_GATE:], "nt", f32, "mm_dhn_g", 1024, 1024, 1024, add=dhn)
    dx, dg_pre = _in_bwd(dh1, dhn, x, W["norm_mix_pre"])
    parts["norm_mix_pre"], = _run(_gather_two_level([dg_pre]), "ag_norm_mix_pre")
    parts = [parts[n] for n in _REPLICATED]

    out = {}
    for n in _SHARDED:
        res = _adamw_sharded(pair[n], from_chips[n], k_arr, shard[n], inp["m_" + n][0], inp["v_" + n][0], n)
        out[n] = [r[None] for r in res]
    small = _adamw_replicated(parts, *[[_flat2(inp[p + n]) for n in _REPLICATED] for p in ("", "m_", "v_")])
    for a, n in enumerate(_REPLICATED):
        out[n] = [small[i][a].reshape(inp[n].shape) for i in range(4)]

    loss = lax.psum(loss_part[0, 0], ("x", "y", "c"))
    outs = [loss, dx.reshape(inp["x"].shape)]
    for i in range(4):
        outs.extend(out[n][i] for n in _WEIGHTS)
    return tuple(outs)


def kernel(x, rel_bias, norm_mix_pre, norm_mix_post, w_in, conv_rnn_w, conv_rnn_b, w_rg_a, b_rg_a, w_rg_x, b_rg_x, lru_lambda, w_branch_rnn, w_branch_att, w_out, norm_ffn_pre, norm_ffn_post, w_ffn_gate, w_ffn_up, conv_ffn_w, conv_ffn_b, w_ffn_down, loss_target, m_rel_bias, m_norm_mix_pre, m_norm_mix_post, m_w_in, m_conv_rnn_w, m_conv_rnn_b, m_w_rg_a, m_b_rg_a, m_w_rg_x, m_b_rg_x, m_lru_lambda, m_w_branch_rnn, m_w_branch_att, m_w_out, m_norm_ffn_pre, m_norm_ffn_post, m_w_ffn_gate, m_w_ffn_up, m_conv_ffn_w, m_conv_ffn_b, m_w_ffn_down, v_rel_bias, v_norm_mix_pre, v_norm_mix_post, v_w_in, v_conv_rnn_w, v_conv_rnn_b, v_w_rg_a, v_b_rg_a, v_w_rg_x, v_b_rg_x, v_lru_lambda, v_w_branch_rnn, v_w_branch_att, v_w_out, v_norm_ffn_pre, v_norm_ffn_post, v_w_ffn_gate, v_w_ffn_up, v_conv_ffn_w, v_conv_ffn_b, v_w_ffn_down):
    vals = locals()
    names = list(_IN_NAMES) + ["loss_target"] + ["m_" + n for n in _WEIGHTS] + ["v_" + n for n in _WEIGHTS]
    return _train_step({n: vals[n] for n in names})
```

```python
import functools
import math

import numpy as np
import jax
import jax.numpy as jnp
from jax import lax
from jax.experimental import pallas as pl
from jax.experimental.pallas import tpu as pltpu

f32, bf16 = jnp.float32, jnp.bfloat16
SDS = jax.ShapeDtypeStruct
MESH = pl.DeviceIdType.MESH
ANY = pl.BlockSpec(memory_space=pl.ANY)

D = 1024
SEQ = 2048
RNN_W = 1280
RNN_BLOCKS = 10
LANES = 128
SUBLANES = 8
RNN_CONV = 4
LRU_C = 8.0
HD = 128
KVH = 4
DILATIONS = (1, 4, 16)
NG = 3
ATT_BLK = 128
NBLK_SEQ = SEQ // ATT_BLK
ATT_UNROLL = 8
REL_BUCKETS = 32
REL_MAX_DIST = 2048
FFN_W = 3072
FFN_CONV = 3
EPS = 1e-6
IN_W = 5888
ATT_COLS = 5 * HD
C_ATT = RNN_W
C_GATE = RNN_W + KVH * ATT_COLS
NEG = -1e30

ADAM_LR, ADAM_B1, ADAM_B2, ADAM_EPS, ADAM_WD, ADAM_STEP = 0.001, 0.9, 0.999, 1e-08, 0.01, 10

VMEM_LIMIT_BYTES = 56 * 1024 * 1024
N_DEV = 8


def _params(sem=None):
    return pltpu.CompilerParams(dimension_semantics=sem, vmem_limit_bytes=VMEM_LIMIT_BYTES)


def _sigmoid(x):
    return 1.0 / (1.0 + jnp.exp(-x))


class _Comm:
    def __init__(self, inputs, out_shapes, sem_shapes, start, finish):
        self.inputs, self.out_shapes, self.sem_shapes = tuple(inputs), tuple(out_shapes), list(sem_shapes)
        self.start, self.finish = start, finish


def _call(body, args, *, name, grid, in_specs, out_specs, out_shape, scratch_shapes=(), semantics, comm=None):
    if comm is None:
        return pl.pallas_call(body, name=name, grid=grid, in_specs=list(in_specs), out_specs=tuple(out_specs),
                              out_shape=tuple(out_shape), scratch_shapes=list(scratch_shapes),
                              compiler_params=_params(semantics))(*args), ()
    n_in, n_out, n_scr = len(in_specs), len(out_shape), len(scratch_shapes)
    c_in, c_out = len(comm.inputs), len(comm.out_shapes)

    def fused(*refs):
        ins, refs = refs[:n_in], refs[n_in:]
        cin, refs = refs[:c_in], refs[c_in:]
        outs, refs = refs[:n_out], refs[n_out:]
        cout, refs = refs[:c_out], refs[c_out:]
        scr, csem = refs[:n_scr], refs[n_scr:]
        first = functools.reduce(jnp.logical_and, [pl.program_id(d) == 0 for d in range(len(grid))])
        last = functools.reduce(jnp.logical_and, [pl.program_id(d) == grid[d] - 1 for d in range(len(grid))])

        @pl.when(first)
        def _():
            comm.start(cin, cout, csem)

        body(*ins, *outs, *scr)

        @pl.when(last)
        def _():
            comm.finish(cin, cout, csem)

    res = pl.pallas_call(
        fused, name=name, grid=grid, in_specs=list(in_specs) + [ANY] * c_in,
        out_specs=tuple(out_specs) + (ANY,) * c_out, out_shape=tuple(out_shape) + comm.out_shapes,
        scratch_shapes=list(scratch_shapes) + comm.sem_shapes,
        compiler_params=_params(("arbitrary",) * len(grid)))(*args, *comm.inputs)
    return res[:n_out], res[n_out:]


_DIMS = {"nn": (((1,), (0,)), ((), ())), "nt": (((1,), (1,)), ((), ())), "tn": (((0,), (0,)), ((), ()))}


def _mm(a, b, mode, out_dtype, name, tm, tn, tk, add=None, cols_outer=False, comm=None):
    if mode == "nn":
        (M, K), (K2, N) = a.shape, b.shape
    elif mode == "nt":
        (M, K), (N, K2) = a.shape, b.shape
    else:
        (K, M), (K2, N) = a.shape, b.shape
    assert K == K2 and M % tm == 0 and N % tn == 0 and K % tk == 0, (name, a.shape, b.shape)
    nk = K // tk
    has_add = add is not None

    def body(*refs):
        if has_add:
            a_ref, b_ref, c_ref, o_ref = refs[:4]
        else:
            a_ref, b_ref, o_ref = refs[:3]
        part = lax.dot_general(a_ref[...].astype(bf16), b_ref[...].astype(bf16), _DIMS[mode],
                               preferred_element_type=f32)

        def finish(acc):
            if has_add:
                acc = acc + c_ref[...]
            o_ref[...] = acc.astype(o_ref.dtype)

        if nk == 1:
            finish(part)
        else:
            acc_ref = refs[-1]
            k = pl.program_id(2)

            @pl.when(k == 0)
            def _():
                acc_ref[...] = part

            @pl.when(k > 0)
            def _():
                acc_ref[...] += part

            @pl.when(k == nk - 1)
            def _():
                finish(acc_ref[...])

    def ij(f):
        return (lambda j, i, k: f(i, j, k)) if cols_outer else f

    if mode == "tn":
        a_spec = pl.BlockSpec((tk, tm), ij(lambda i, j, k: (k, i)))
    else:
        a_spec = pl.BlockSpec((tm, tk), ij(lambda i, j, k: (i, k)))
    if mode == "nt":
        b_spec = pl.BlockSpec((tn, tk), ij(lambda i, j, k: (j, k)))
    else:
        b_spec = pl.BlockSpec((tk, tn), ij(lambda i, j, k: (k, j)))
    o_spec = pl.BlockSpec((tm, tn), ij(lambda i, j, k: (i, j)))
    in_specs = [a_spec, b_spec] + ([o_spec] if has_add else [])
    args = (a, b) + ((add,) if has_add else ())
    grid = (N // tn, M // tm, nk) if cols_outer else (M // tm, N // tn, nk)
    (out,), extra = _call(
        body, args, name=name, out_shape=(SDS((M, N), out_dtype),), grid=grid, in_specs=in_specs,
        out_specs=(o_spec,), scratch_shapes=[pltpu.VMEM((tm, tn), f32)] if nk > 1 else [],
        semantics=("parallel", "parallel", "arbitrary"), comm=comm)
    return out if comm is None else (out, extra)


ROW_TILE = 512


def _rms_fwd(x, g):
    r = lax.rsqrt(jnp.mean(x * x, axis=-1, keepdims=True) + EPS)
    return x * r * g


def _rms_bwd(x, g, dy):
    r = lax.rsqrt(jnp.mean(x * x, axis=-1, keepdims=True) + EPS)
    xh = x * r
    dxh = dy * g
    dx = r * (dxh - xh * jnp.mean(dxh * xh, axis=-1, keepdims=True))
    return dx, jnp.sum(dy * xh, axis=0, keepdims=True)


def _acc_out(ref, val):
    @pl.when(pl.program_id(0) == 0)
    def _():
        ref[...] = val

    @pl.when(pl.program_id(0) > 0)
    def _():
        ref[...] += val


def _row_spec(width=D):
    return pl.BlockSpec((ROW_TILE, width), lambda i: (i, 0))


def _vec_spec(width=D):
    return pl.BlockSpec((1, width), lambda i: (0, 0))


def _norm_in(x, g):
    def body(x_ref, g_ref, o_ref):
        o_ref[...] = _rms_fwd(x_ref[...], g_ref[...]).astype(bf16)

    T = x.shape[0]
    return pl.pallas_call(body, name="norm_in", out_shape=SDS((T, D), bf16), grid=(T // ROW_TILE,),
                          in_specs=[_row_spec(), _vec_spec()], out_specs=_row_spec(),
                          compiler_params=_params(("parallel",)))(x, g)


def _mid_fwd(x, mix, g_post, g_fpre):
    def body(x_ref, mix_ref, gp_ref, gf_ref, h1_ref, hn2_ref):
        h1 = x_ref[...] + _rms_fwd(mix_ref[...], gp_ref[...])
        h1_ref[...] = h1
        hn2_ref[...] = _rms_fwd(h1, gf_ref[...]).astype(bf16)

    T = x.shape[0]
    return pl.pallas_call(body, name="mid_fwd", out_shape=(SDS((T, D), f32), SDS((T, D), bf16)),
                          grid=(T // ROW_TILE,),
                          in_specs=[_row_spec(), _row_spec(), _vec_spec(), _vec_spec()],
                          out_specs=(_row_spec(), _row_spec()),
                          compiler_params=_params(("parallel",)))(x, mix, g_post, g_fpre)


def _final(h1, ff, g_fpost, target):
    def body(h1_ref, ff_ref, g_ref, t_ref, loss_ref, dy_ref, dff_ref, dg_ref):
        ff = ff_ref[...]
        g = g_ref[...]
        e = h1_ref[...] + _rms_fwd(ff, g) - t_ref[...]
        part = jnp.sum(jnp.sum(e * e, axis=1, keepdims=True), axis=0, keepdims=True) * (0.5 / D)
        dy = e * (1.0 / D)
        dy_ref[...] = dy
        dff, dg = _rms_bwd(ff, g, dy)
        dff_ref[...] = dff.astype(bf16)
        _acc_out(loss_ref, part)
        _acc_out(dg_ref, dg)

    T = h1.shape[0]
    return pl.pallas_call(
        body, name="final", out_shape=(SDS((1, 1), f32), SDS((T, D), f32), SDS((T, D), bf16), SDS((1, D), f32)),
        grid=(T // ROW_TILE,),
        in_specs=[_row_spec(), _row_spec(), _vec_spec(), _row_spec()],
        out_specs=(pl.BlockSpec((1, 1), lambda i: (0, 0)), _row_spec(), _row_spec(), _vec_spec()),
        compiler_params=_params(("arbitrary",)))(h1, ff, g_fpost, target)


def _mid_bwd(dy, dhn2, h1, g_fpre, mix, g_post):
    def body(dy_ref, dhn2_ref, h1_ref, gf_ref, mix_ref, gp_ref, dh1_ref, dmix_ref, dgf_ref, dgp_ref):
        d1, dgf = _rms_bwd(h1_ref[...], gf_ref[...], dhn2_ref[...])
        dh1 = dy_ref[...] + d1
        dh1_ref[...] = dh1
        dmix, dgp = _rms_bwd(mix_ref[...], gp_ref[...], dh1)
        dmix_ref[...] = dmix.astype(bf16)
        _acc_out(dgf_ref, dgf)
        _acc_out(dgp_ref, dgp)

    T = dy.shape[0]
    return pl.pallas_call(
        body, name="mid_bwd", out_shape=(SDS((T, D), f32), SDS((T, D), bf16), SDS((1, D), f32), SDS((1, D), f32)),
        grid=(T // ROW_TILE,),
        in_specs=[_row_spec(), _row_spec(), _row_spec(), _vec_spec(), _row_spec(), _vec_spec()],
        out_specs=(_row_spec(), _row_spec(), _vec_spec(), _vec_spec()),
        compiler_params=_params(("arbitrary",)))(dy, dhn2, h1, g_fpre, mix, g_post)


def _in_bwd(dh1, dhn, x, g_pre):
    def body(dh1_ref, dhn_ref, x_ref, g_ref, dx_ref, dg_ref):
        d, dg = _rms_bwd(x_ref[...], g_ref[...], dhn_ref[...])
        dx_ref[...] = dh1_ref[...] + d
        _acc_out(dg_ref, dg)

    T = x.shape[0]
    return pl.pallas_call(
        body, name="in_bwd", out_shape=(SDS((T, D), f32), SDS((1, D), f32)), grid=(T // ROW_TILE,),
        in_specs=[_row_spec(), _row_spec(), _row_spec(), _vec_spec()], out_specs=(_row_spec(), _vec_spec()),
        compiler_params=_params(("arbitrary",)))(dh1, dhn, x, g_pre)


def _shift_dn(x, d, row, fill=0.0):
    if d == 0:
        return x
    y = pltpu.roll(x, d, 0)
    head = jnp.where(row[:SUBLANES] >= d, y[:SUBLANES], fill)
    return jnp.concatenate([head, y[SUBLANES:]], axis=0)


def _shift_up(x, d, row, fill=0.0):
    if d == 0:
        return x
    n = x.shape[0]
    y = pltpu.roll(x, n - d, 0)
    tail = jnp.where(row[:SUBLANES] < SUBLANES - d, y[n - SUBLANES:], fill)
    return jnp.concatenate([y[:n - SUBLANES], tail], axis=0)


def _conv_fwd(x, w_ref, b, row):
    K = w_ref.shape[0]
    y = b
    for k in range(K):
        y = y + w_ref[k:k + 1, :] * _shift_dn(x, K - 1 - k, row)
    return y


def _conv_bwd(x, w_ref, dy, row):
    K = w_ref.shape[0]
    dx = jnp.zeros_like(dy)
    dws = []
    for k in range(K):
        dx = dx + w_ref[k:k + 1, :] * _shift_up(dy, K - 1 - k, row)
        dws.append(jnp.sum(dy * _shift_dn(x, K - 1 - k, row), axis=0, keepdims=True))
    return dx, dws, jnp.sum(dy, axis=0, keepdims=True)


def _scan_fwd(a, u, row):
    n = a.shape[0]
    d = 1
    while d < n:
        last = 2 * d >= n
        if d < SUBLANES:
            u = u + a * _shift_dn(u, d, row)
            if not last:
                a = a * _shift_dn(a, d, row, fill=1.0)
        else:
            u = jnp.concatenate([u[:d], u[d:] + a[d:] * u[:n - d]], axis=0)
            if not last:
                a = jnp.concatenate([a[:d], a[d:] * a[:n - d]], axis=0)
        d *= 2
    return u


def _scan_bwd(b, u, row):
    n = b.shape[0]
    d = 1
    while d < n:
        last = 2 * d >= n
        if d < SUBLANES:
            u = u + b * _shift_up(u, d, row)
            if not last:
                b = b * _shift_up(b, d, row, fill=1.0)
        else:
            u = jnp.concatenate([u[:n - d] + b[:n - d] * u[d:], u[n - d:]], axis=0)
            if not last:
                b = jnp.concatenate([b[:n - d] * b[d:], b[n - d:]], axis=0)
        d *= 2
    return u


def _neg_expm1(z):
    series = -z * (1.0 + z * (0.5 + z * (1.0 / 6.0 + z * (1.0 / 24.0 + z * (1.0 / 120.0)))))
    return jnp.where(z > -0.1, series, 1.0 - jnp.exp(z))


def _rnn_gates(xr, cw_ref, cb, wa, ba, wx, bx, lam, row):
    xc = _conv_fwd(xr, cw_ref, cb, row)
    xcb = xc.astype(bf16)
    r = _sigmoid(jnp.dot(xcb, wa.astype(bf16), preferred_element_type=f32) + ba)
    i = _sigmoid(jnp.dot(xcb, wx.astype(bf16), preferred_element_type=f32) + bx)
    z = -lam
    sp = jnp.maximum(z, 0.0) + jnp.log(1.0 + jnp.exp(-jnp.abs(z)))
    log_a = (-LRU_C * sp) * r
    a = jnp.exp(log_a)
    s = jnp.sqrt(_neg_expm1(2.0 * log_a))
    return xc, xcb, r, i, sp, a, s


def _rnn_specs(B):
    blk = lambda: pl.BlockSpec((SEQ, LANES), lambda b, n: (b, n))
    return dict(
        act=blk,
        convw=pl.BlockSpec((RNN_CONV, LANES), lambda b, n: (0, n)),
        vec=lambda: pl.BlockSpec((1, LANES), lambda b, n: (0, n)),
        gate=lambda: pl.BlockSpec((None, LANES, LANES), lambda b, n: (n, 0, 0)),
    )


def _rnn_fwd(proj, cw, cb, wa, ba, wx, bx, lam):
    T = proj.shape[0]
    B = T // SEQ

    def body(x_ref, cw_ref, cb_ref, wa_ref, ba_ref, wx_ref, bx_ref, lam_ref, h_ref):
        row = lax.broadcasted_iota(jnp.int32, (SEQ, LANES), 0)
        xc, _, r, i, sp, a, s = _rnn_gates(x_ref[...], cw_ref, cb_ref[...], wa_ref[...], ba_ref[...],
                                          wx_ref[...], bx_ref[...], lam_ref[...], row)
        h_ref[...] = _scan_fwd(a, s * (i * xc), row)

    sp_ = _rnn_specs(B)
    return pl.pallas_call(
        body, name="rnn_fwd", out_shape=SDS((T, RNN_W), f32), grid=(B, RNN_BLOCKS),
        in_specs=[sp_["act"](), sp_["convw"], sp_["vec"](), sp_["gate"](), sp_["vec"](), sp_["gate"](),
                  sp_["vec"](), sp_["vec"]()],
        out_specs=sp_["act"](),
        compiler_params=_params(("parallel", "parallel")))(proj, cw, cb, wa, ba, wx, bx, lam)


def _rnn_bwd(proj, h, dh, cw, cb, wa, ba, wx, bx, lam, comm=None):
    T = proj.shape[0]
    B = T // SEQ

    def body(x_ref, h_ref, dh_ref, cw_ref, cb_ref, wa_ref, ba_ref, wx_ref, bx_ref, lam_ref,
             dx_ref, dcw_ref, dcb_ref, dwa_ref, dba_ref, dwx_ref, dbx_ref, dlam_ref):
        row = lax.broadcasted_iota(jnp.int32, (SEQ, LANES), 0)
        xr = x_ref[...]
        wa, wx, lam = wa_ref[...], wx_ref[...], lam_ref[...]
        xc, xcb, r, i, sp, a, s = _rnn_gates(xr, cw_ref, cb_ref[...], wa, ba_ref[...], wx, bx_ref[...], lam, row)
        hprev = _shift_dn(h_ref[...], 1, row)
        g = _scan_bwd(_shift_up(a, 1, row), dh_ref[...].astype(f32), row)
        da = g * hprev
        ds = g * (i * xc)
        di = g * (s * xc)
        dxc = g * (s * i)
        dla = da * a - ds * (a * a) / s
        dr = dla * (-LRU_C * sp)
        dsp = jnp.sum(dla * (-LRU_C * r), axis=0, keepdims=True)
        dlam = -dsp * _sigmoid(-lam)
        dga = dr * r * (1.0 - r)
        dgx = di * i * (1.0 - i)
        dgab, dgxb = dga.astype(bf16), dgx.astype(bf16)
        dwa = lax.dot_general(xcb, dgab, _DIMS["tn"], preferred_element_type=f32)
        dwx = lax.dot_general(xcb, dgxb, _DIMS["tn"], preferred_element_type=f32)
        dxc = dxc + lax.dot_general(dgab, wa.astype(bf16), _DIMS["nt"], preferred_element_type=f32)
        dxc = dxc + lax.dot_general(dgxb, wx.astype(bf16), _DIMS["nt"], preferred_element_type=f32)
        dx, dws, db = _conv_bwd(xr, cw_ref, dxc, row)
        dx_ref[...] = dx.astype(bf16)
        first = pl.program_id(1) == 0

        def acc(ref, val):
            @pl.when(first)
            def _():
                ref[...] = val

            @pl.when(jnp.logical_not(first))
            def _():
                ref[...] += val

        for k in range(RNN_CONV):
            acc(dcw_ref.at[k:k + 1, :], dws[k])
        acc(dcb_ref, db)
        acc(dwa_ref, dwa)
        acc(dba_ref, jnp.sum(dga, axis=0, keepdims=True))
        acc(dwx_ref, dwx)
        acc(dbx_ref, jnp.sum(dgx, axis=0, keepdims=True))
        acc(dlam_ref, dlam)

    blk = lambda: pl.BlockSpec((SEQ, LANES), lambda n, b: (b, n))
    convw = lambda: pl.BlockSpec((RNN_CONV, LANES), lambda n, b: (0, n))
    vec = lambda: pl.BlockSpec((1, LANES), lambda n, b: (0, n))
    gate = lambda: pl.BlockSpec((None, LANES, LANES), lambda n, b: (n, 0, 0))
    vshape = SDS((1, RNN_W), f32)
    gshape = SDS((RNN_BLOCKS, LANES, LANES), f32)
    return _call(
        body, (proj, h, dh, cw, cb, wa, ba, wx, bx, lam), name="rnn_bwd",
        out_shape=(SDS((T, RNN_W), bf16), SDS((RNN_CONV, RNN_W), f32), vshape, gshape, vshape, gshape, vshape, vshape),
        grid=(RNN_BLOCKS, B),
        in_specs=[blk(), blk(), blk(), convw(), vec(), gate(), vec(), gate(), vec(), vec()],
        out_specs=(blk(), convw(), vec(), gate(), vec(), gate(), vec(), vec()),
        semantics=("parallel", "arbitrary"), comm=comm)


def _t5_bucket(dist):
    max_exact = REL_BUCKETS // 2
    d = np.maximum(dist, 1).astype(np.float32)
    large = max_exact + np.log(d / max_exact) / math.log(REL_MAX_DIST / max_exact) * (REL_BUCKETS - max_exact)
    large = np.minimum(large.astype(np.int32), REL_BUCKETS - 1)
    return np.where(dist < max_exact, dist, large).astype(np.int32)


def _bucket_maps():
    qi = np.arange(ATT_BLK)[:, None]
    kj = np.arange(2 * ATT_BLK)[None, :]
    delta = ATT_BLK + qi - kj
    valid = (delta >= 0) & (delta <= ATT_BLK)
    maps = [np.where(valid, _t5_bucket(np.maximum(delta, 0) * r), -1) for r in DILATIONS]
    return np.stack(maps).astype(np.int32)


def _bias_tables(rel_bias, buckets):
    def body(rb_ref, bk_ref, o_ref):
        for g in range(NG):
            bk = bk_ref[g]
            for h in range(KVH):
                acc = jnp.full(bk.shape, NEG, f32)
                for b in range(REL_BUCKETS):
                    acc = jnp.where(bk == b, rb_ref[b, g * KVH + h], acc)
                o_ref[h, g] = acc

    return pl.pallas_call(
        body, name="bias_tables", out_shape=SDS((KVH, NG, ATT_BLK, 2 * ATT_BLK), f32),
        in_specs=[pl.BlockSpec(memory_space=pltpu.SMEM), pl.BlockSpec(memory_space=pltpu.VMEM)],
        out_specs=pl.BlockSpec(memory_space=pltpu.VMEM), compiler_params=_params())(rel_bias, buckets)


def _bias_grad(dbias, buckets):
    def body(db_ref, bk_ref, o_ref):
        rr = lax.broadcasted_iota(jnp.int32, (REL_BUCKETS, NG * KVH), 0)
        cc = lax.broadcasted_iota(jnp.int32, (REL_BUCKETS, NG * KVH), 1)
        out = jnp.zeros((REL_BUCKETS, NG * KVH), f32)
        for g in range(NG):
            bk = bk_ref[g]
            for h in range(KVH):
                d = db_ref[h, g]
                for b in range(REL_BUCKETS):
                    m = jnp.where(bk == b, d, 0.0)
                    s = jnp.sum(jnp.sum(m, axis=1, keepdims=True), axis=0, keepdims=True)
                    out = jnp.where((rr == b) & (cc == g * KVH + h), s, out)
        o_ref[...] = out

    return pl.pallas_call(body, name="bias_grad", out_shape=SDS((REL_BUCKETS, NG * KVH), f32),
                          compiler_params=_params())(dbias, buckets)


def _to_sub(dst_ref, src_ref, r, dtype, offset=0):
    M = SEQ // r
    for c in range(r):
        if r == 1:
            v = src_ref[...]
        else:
            v = src_ref[pl.ds(c, M, stride=r), :]
        dst_ref[pl.ds(offset + c * M, M), :] = v.astype(dtype)


def _from_sub(dst_ref, src_ref, r, accumulate=False, offset=0):
    M = SEQ // r
    for c in range(r):
        v = src_ref[pl.ds(offset + c * M, M), :]
        idx = slice(None) if r == 1 else pl.ds(c, M, stride=r)
        if accumulate:
            dst_ref[idx, :] = dst_ref[idx, :] + v
        else:
            dst_ref[idx, :] = v


_COL = lambda k: slice(k * HD, (k + 1) * HD)
SCALE = HD ** -0.5


def _qkv_spec(k, bh):
    def index(*ids):
        b, h = bh(*ids)
        return (b, C_ATT // HD + 5 * h + k)

    return pl.BlockSpec((SEQ, HD), index)


def _key_window(bias_ref, g, nb):
    if nb == 1:
        bias_own = bias_ref[g, :, ATT_BLK:2 * ATT_BLK]
        return lambda j: (pl.ds(pl.multiple_of((j + 1) * ATT_BLK, ATT_BLK), ATT_BLK), bias_own)
    bias_g = bias_ref[g]
    col = lax.broadcasted_iota(jnp.int32, bias_g.shape, 1)
    bias_first = jnp.where(col >= ATT_BLK, bias_g, NEG)
    return lambda j: (pl.ds(pl.multiple_of(j * ATT_BLK, ATT_BLK), 2 * ATT_BLK),
                      jnp.where(j % nb != 0, bias_g, bias_first))


def _att_fwd(proj, bias, comm=None):
    T = proj.shape[0]
    B = T // SEQ

    def body(q0_ref, q1_ref, q2_ref, k_ref, v_ref, bias_ref, o_ref, lse_ref, qp, kp, vp, kt, op, lp, og, lg):
        q_refs = (q0_ref, q1_ref, q2_ref)
        kp[0:ATT_BLK, :] = jnp.zeros((ATT_BLK, HD), bf16)
        vp[0:ATT_BLK, :] = jnp.zeros((ATT_BLK, HD), bf16)
        for g, r in enumerate(DILATIONS):
            nb = NBLK_SEQ // r
            _to_sub(qp, q_refs[g], r, bf16)
            _to_sub(kp, k_ref, r, bf16, offset=ATT_BLK)
            _to_sub(vp, v_ref, r, bf16, offset=ATT_BLK)
            kt[...] = kp[...].T
            keys = _key_window(bias_ref, g, nb)

            def step(j, carry):
                cur = pl.ds(pl.multiple_of(j * ATT_BLK, ATT_BLK), ATT_BLK)
                win, bias_j = keys(j)
                s = jnp.dot(qp[cur, :], kt[:, win], preferred_element_type=f32) * SCALE + bias_j
                m = jnp.max(s, axis=-1, keepdims=True)
                p = jnp.exp(s - m)
                den = jnp.sum(p, axis=-1, keepdims=True)
                o = jnp.dot(p.astype(bf16), vp[win, :], preferred_element_type=f32)
                op[cur, :] = o / den
                lp[cur, :] = jnp.broadcast_to(m + jnp.log(den), (ATT_BLK, HD))
                return carry

            lax.fori_loop(0, NBLK_SEQ, step, 0, unroll=ATT_UNROLL)
            _from_sub(og.at[g], op, r)
            _from_sub(lg.at[g], lp, r)
        l0, l1, l2 = lg[0], lg[1], lg[2]
        mx = jnp.maximum(jnp.maximum(l0, l1), l2)
        e0, e1, e2 = jnp.exp(l0 - mx), jnp.exp(l1 - mx), jnp.exp(l2 - mx)
        den = e0 + e1 + e2
        o_ref[...] = (e0 * og[0] + e1 * og[1] + e2 * og[2]) / den
        lse_ref[...] = mx + jnp.log(den)

    return _call(
        body, (proj, proj, proj, proj, proj, bias), name="att_fwd",
        out_shape=(SDS((T, KVH * HD), f32), SDS((KVH, T, HD), f32)), grid=(B, KVH),
        in_specs=[_qkv_spec(k, lambda b, h: (b, h)) for k in range(5)]
                 + [pl.BlockSpec((None, NG, ATT_BLK, 2 * ATT_BLK), lambda b, h: (h, 0, 0, 0))],
        out_specs=(pl.BlockSpec((SEQ, HD), lambda b, h: (b, h)),
                   pl.BlockSpec((None, SEQ, HD), lambda b, h: (h, b, 0))),
        scratch_shapes=[pltpu.VMEM((SEQ, HD), bf16)] + [pltpu.VMEM((SEQ + ATT_BLK, HD), bf16)] * 2
                       + [pltpu.VMEM((HD, SEQ + ATT_BLK), bf16)]
                       + [pltpu.VMEM((SEQ, HD), f32)] * 2 + [pltpu.VMEM((NG, SEQ, HD), f32)] * 2,
        semantics=("parallel", "parallel"), comm=comm)


def _att_bwd(proj, bias, o, lse, do, comm=None):
    T = proj.shape[0]
    B = T // SEQ

    def body(q0_ref, q1_ref, q2_ref, k_ref, v_ref, bias_ref, o_ref, lse_ref, do_ref, dx_ref, db_ref,
             qp, kp, vp, dop, qt, kt, vt, dot, lp, dlp, dqp, dkt, dvt, dln, nat, dkn, dvn):
        q_refs = (q0_ref, q1_ref, q2_ref)
        first = pl.program_id(1) == 0

        @pl.when(first)
        def _():
            db_ref[...] = jnp.zeros_like(db_ref)

        dln[...] = jnp.broadcast_to(jnp.sum(do_ref[...] * o_ref[...], axis=-1, keepdims=True), (SEQ, HD))
        dkn[...] = jnp.zeros_like(dkn)
        dvn[...] = jnp.zeros_like(dvn)
        kp[0:ATT_BLK, :] = jnp.zeros((ATT_BLK, HD), bf16)
        vp[0:ATT_BLK, :] = jnp.zeros((ATT_BLK, HD), bf16)
        for g, r in enumerate(DILATIONS):
            nb = NBLK_SEQ // r
            _to_sub(qp, q_refs[g], r, bf16)
            _to_sub(kp, k_ref, r, bf16, offset=ATT_BLK)
            _to_sub(vp, v_ref, r, bf16, offset=ATT_BLK)
            _to_sub(dop, do_ref, r, bf16)
            _to_sub(lp, lse_ref, r, f32)
            _to_sub(dlp, dln, r, f32)
            qt[...], kt[...], vt[...], dot[...] = qp[...].T, kp[...].T, vp[...].T, dop[...].T
            dkt[...] = jnp.zeros_like(dkt)
            dvt[...] = jnp.zeros_like(dvt)
            keys = _key_window(bias_ref, g, nb)
            db_cols = slice(ATT_BLK, 2 * ATT_BLK) if nb == 1 else slice(None)

            def step(j, carry):
                cur = pl.ds(pl.multiple_of(j * ATT_BLK, ATT_BLK), ATT_BLK)
                win, bias_j = keys(j)
                s = jnp.dot(qp[cur, :], kt[:, win], preferred_element_type=f32) * SCALE + bias_j
                p = jnp.exp(s - lp[cur, 0:1])
                dp = jnp.dot(dop[cur, :], vt[:, win], preferred_element_type=f32)
                ds = p * (dp - dlp[cur, 0:1])
                db_ref[g, :, db_cols] += ds
                dsb, pb = ds.astype(bf16), p.astype(bf16)
                dqp[cur, :] = jnp.dot(dsb, kp[win, :], preferred_element_type=f32) * SCALE
                dkt[:, win] += jnp.dot(qt[:, cur], dsb, preferred_element_type=f32) * SCALE
                dvt[:, win] += jnp.dot(dot[:, cur], pb, preferred_element_type=f32)
                return carry

            lax.fori_loop(0, NBLK_SEQ, step, 0, unroll=ATT_UNROLL)
            _from_sub(nat, dqp, r)
            dx_ref[:, _COL(g)] = nat[...].astype(bf16)
            dqp[...] = dkt[:, ATT_BLK:].T
            _from_sub(dkn, dqp, r, accumulate=True)
            dqp[...] = dvt[:, ATT_BLK:].T
            _from_sub(dvn, dqp, r, accumulate=True)
        dx_ref[:, _COL(3)] = dkn[...].astype(bf16)
        dx_ref[:, _COL(4)] = dvn[...].astype(bf16)

    blk = lambda: pl.BlockSpec((SEQ, HD), lambda h, b: (b, h))
    bias_spec = lambda: pl.BlockSpec((None, NG, ATT_BLK, 2 * ATT_BLK), lambda h, b: (h, 0, 0, 0))
    pad = lambda dtype: pltpu.VMEM((SEQ + ATT_BLK, HD), dtype)
    pad_t = lambda dtype: pltpu.VMEM((HD, SEQ + ATT_BLK), dtype)
    seq_t = pltpu.VMEM((HD, SEQ), bf16)
    return _call(
        body, (proj, proj, proj, proj, proj, bias, o, lse, do), name="att_bwd",
        out_shape=(SDS((T, KVH * ATT_COLS), bf16), SDS((KVH, NG, ATT_BLK, 2 * ATT_BLK), f32)), grid=(KVH, B),
        in_specs=[_qkv_spec(k, lambda h, b: (b, h)) for k in range(5)]
                 + [bias_spec(), blk(), pl.BlockSpec((None, SEQ, HD), lambda h, b: (h, b, 0)), blk()],
        out_specs=(pl.BlockSpec((SEQ, ATT_COLS), lambda h, b: (b, h)), bias_spec()),
        scratch_shapes=[pltpu.VMEM((SEQ, HD), bf16), pad(bf16), pad(bf16), pltpu.VMEM((SEQ, HD), bf16),
                        seq_t, pad_t(bf16), pad_t(bf16), seq_t]
                       + [pltpu.VMEM((SEQ, HD), f32)] * 3 + [pad_t(f32)] * 2 + [pltpu.VMEM((SEQ, HD), f32)] * 4,
        semantics=("parallel", "arbitrary"), comm=comm)


MERGE_ROWS, MERGE_COLS = 1024, 256
_G_RNN_BLK = C_GATE // MERGE_COLS
_G_ATT_BLK = (C_GATE + D) // MERGE_COLS


def _merge_specs():
    cols = lambda off: pl.BlockSpec((MERGE_ROWS, MERGE_COLS), lambda i, j: (i, off + j))
    return cols(_G_RNN_BLK), cols(_G_ATT_BLK), cols(0)


def _merge_fwd(proj, pr, pa):
    def body(gr_ref, ga_ref, pr_ref, pa_ref, o_ref):
        o_ref[...] = (_sigmoid(gr_ref[...]) * pr_ref[...].astype(f32)
                      + _sigmoid(ga_ref[...]) * pa_ref[...].astype(f32)).astype(bf16)

    T = proj.shape[0]
    s_gr, s_ga, s0 = _merge_specs()
    return pl.pallas_call(body, name="merge_fwd", out_shape=SDS((T, D), bf16),
                          grid=(T // MERGE_ROWS, D // MERGE_COLS),
                          in_specs=[s_gr, s_ga, s0, s0], out_specs=s0,
                          compiler_params=_params(("parallel", "parallel")))(proj, proj, pr, pa)


def _merge_bwd(proj, pr, pa, dm):
    nj = D // MERGE_COLS

    def body(g_ref, pr_ref, pa_ref, dm_ref, dp_ref, dg_ref):
        dm_ = dm_ref[...].astype(f32)
        s = _sigmoid(g_ref[...])
        p = jnp.where(pl.program_id(1) < nj, pr_ref[...], pa_ref[...]).astype(f32)
        dp_ref[...] = (dm_ * s).astype(bf16)
        dg_ref[...] = (dm_ * p * s * (1.0 - s)).astype(bf16)

    T = proj.shape[0]
    blk = (MERGE_ROWS, MERGE_COLS)
    wrap = pl.BlockSpec(blk, lambda i, j: (i, j % nj))
    pr_spec = pl.BlockSpec(blk, lambda i, j: (i, jnp.minimum(j, nj - 1)))
    pa_spec = pl.BlockSpec(blk, lambda i, j: (i, jnp.maximum(j - nj, 0)))
    out = pl.BlockSpec(blk, lambda i, j: (i, j))
    return pl.pallas_call(
        body, name="merge_bwd", out_shape=(SDS((T, 2 * D), bf16), SDS((T, 2 * D), bf16)),
        grid=(T // MERGE_ROWS, 2 * nj),
        in_specs=[pl.BlockSpec(blk, lambda i, j: (i, _G_RNN_BLK + j)), pr_spec, pa_spec, wrap], out_specs=(out, out),
        compiler_params=_params(("parallel", "parallel")))(proj, pr, pa, dm)


FFN_COLS = 256
GELU_C = math.sqrt(2.0 / math.pi)
GELU_A = 0.044715


def _gelu_parts(x):
    t = jnp.tanh(GELU_C * (x + GELU_A * x * x * x))
    return 0.5 * x * (1.0 + t), t


def _ffn_act_fwd(gpre, up, cw, cb):
    def body(g_ref, u_ref, cw_ref, cb_ref, o_ref):
        row = lax.broadcasted_iota(jnp.int32, (SEQ, FFN_COLS), 0)
        gate = _conv_fwd(g_ref[...].astype(f32), cw_ref, cb_ref[...], row)
        o_ref[...] = (_gelu_parts(gate)[0] * u_ref[...].astype(f32)).astype(bf16)

    T = gpre.shape[0]
    blk = lambda: pl.BlockSpec((SEQ, FFN_COLS), lambda b, j: (b, j))
    return pl.pallas_call(
        body, name="ffn_act_fwd", out_shape=SDS((T, FFN_W), bf16), grid=(T // SEQ, FFN_W // FFN_COLS),
        in_specs=[blk(), blk(), pl.BlockSpec((FFN_CONV, FFN_COLS), lambda b, j: (0, j)),
                  pl.BlockSpec((1, FFN_COLS), lambda b, j: (0, j))],
        out_specs=blk(), compiler_params=_params(("parallel", "parallel")))(gpre, up, cw, cb)


def _ffn_act_bwd(gpre, up, cw, cb, dact):
    def body(g_ref, u_ref, cw_ref, cb_ref, da_ref, dg_ref, du_ref, dcw_ref, dcb_ref):
        row = lax.broadcasted_iota(jnp.int32, (SEQ, FFN_COLS), 0)
        gp = g_ref[...].astype(f32)
        gate = _conv_fwd(gp, cw_ref, cb_ref[...], row)
        gel, t = _gelu_parts(gate)
        da = da_ref[...].astype(f32)
        du_ref[...] = (da * gel).astype(bf16)
        dgel = 0.5 * (1.0 + t) + 0.5 * gate * (1.0 - t * t) * (GELU_C * (1.0 + 3.0 * GELU_A * gate * gate))
        dgate = da * u_ref[...].astype(f32) * dgel
        dx, dws, db = _conv_bwd(gp, cw_ref, dgate, row)
        dg_ref[...] = dx.astype(bf16)
        first = pl.program_id(1) == 0

        def acc(ref, val):
            @pl.when(first)
            def _():
                ref[...] = val

            @pl.when(jnp.logical_not(first))
            def _():
                ref[...] += val

        for k in range(FFN_CONV):
            acc(dcw_ref.at[k:k + 1, :], dws[k])
        acc(dcb_ref, db)

    T = gpre.shape[0]
    blk = lambda: pl.BlockSpec((SEQ, FFN_COLS), lambda j, b: (b, j))
    cws = lambda: pl.BlockSpec((FFN_CONV, FFN_COLS), lambda j, b: (0, j))
    cbs = lambda: pl.BlockSpec((1, FFN_COLS), lambda j, b: (0, j))
    return pl.pallas_call(
        body, name="ffn_act_bwd",
        out_shape=(SDS((T, FFN_W), bf16), SDS((T, FFN_W), bf16), SDS((FFN_CONV, FFN_W), f32), SDS((1, FFN_W), f32)),
        grid=(FFN_W // FFN_COLS, T // SEQ),
        in_specs=[blk(), blk(), cws(), cbs(), blk()], out_specs=(blk(), blk(), cws(), cbs()),
        compiler_params=_params(("parallel", "arbitrary")))(gpre, up, cw, cb, dact)


def _coords():
    return lax.axis_index("x"), lax.axis_index("y"), lax.axis_index("c")


def _dev_index(dev):
    return 4 * dev[0] + 2 * dev[1] + dev[2]


def _dma_sems(n):
    return [pltpu.SemaphoreType.DMA((n,)), pltpu.SemaphoreType.DMA((n,))]


def _gather_two_level(arrays):
    n = len(arrays)

    def plan(ins, outs, sems):
        send_sems, recv_sems, local_sems = sems
        x, y, c = _coords()
        me, sibling = (x, y, c), (x, y, 1 - c)
        chips = [(1 - x, y), (x, 1 - y), (1 - x, 1 - y)]

        def copy(a, k, block, to, own=False):
            dst = outs[a].at[_dev_index(block)]
            return pltpu.make_async_remote_copy(
                src_ref=ins[a] if own else dst, dst_ref=dst, send_sem=send_sems.at[7 * a + k],
                recv_sem=recv_sems.at[7 * a + k], device_id=to, device_id_type=MESH)

        mine = [pltpu.make_async_copy(ins[a], outs[a].at[_dev_index(me)], local_sems.at[a]) for a in range(n)]
        first = [copy(a, 0, me, sibling, own=True) for a in range(n)]
        first += [copy(a, 1 + j, me, (*chip, c), own=True) for a in range(n) for j, chip in enumerate(chips)]
        passed = [[copy(a, 4 + j, (*chip, c), sibling) for a in range(n)] for j, chip in enumerate(chips)]
        arrive_ici = [[copy(a, 1 + j, (*chip, c), me) for a in range(n)] for j, chip in enumerate(chips)]
        arrive_d2d = [copy(a, 0, sibling, me) for a in range(n)]
        arrive_d2d += [copy(a, 4 + j, (*chip, 1 - c), me) for a in range(n) for j, chip in enumerate(chips)]
        return mine, first, passed, arrive_ici, arrive_d2d

    def start(ins, outs, sems):
        mine, first, _, _, _ = plan(ins, outs, sems)
        for cp in mine + first:
            cp.start()

    def finish(ins, outs, sems):
        mine, first, passed, arrive_ici, arrive_d2d = plan(ins, outs, sems)
        for j in range(3):
            for cp in arrive_ici[j]:
                cp.wait_recv()
            for cp in passed[j]:
                cp.start()
        for cp in arrive_d2d:
            cp.wait_recv()
        for cp in first + [cp for group in passed for cp in group]:
            cp.wait_send()
        for cp in mine:
            cp.wait()

    return _Comm(arrays, [SDS((N_DEV,) + a.shape, a.dtype) for a in arrays],
                 _dma_sems(7 * n) + [pltpu.SemaphoreType.DMA((n,))], start, finish)


def _gather_direct(arrays):
    n = len(arrays)

    def plan(ins, outs, sems):
        send_sems, recv_sems, local_sems = sems
        x, y, c = _coords()
        me = (x, y, c)
        mine = [pltpu.make_async_copy(ins[a], outs[a].at[_dev_index(me)], local_sems.at[a]) for a in range(n)]
        sends, arrivals = [], []
        for a in range(n):
            for k in range(1, N_DEV):
                peer = (1 - x if k & 4 else x, 1 - y if k & 2 else y, 1 - c if k & 1 else c)
                s = 7 * a + k - 1
                for slot, out in ((me, sends), (peer, arrivals)):
                    out.append(pltpu.make_async_remote_copy(
                        src_ref=ins[a], dst_ref=outs[a].at[_dev_index(slot)], send_sem=send_sems.at[s],
                        recv_sem=recv_sems.at[s], device_id=peer, device_id_type=MESH))
        return mine, sends, arrivals

    def start(ins, outs, sems):
        mine, sends, _ = plan(ins, outs, sems)
        for cp in mine + sends:
            cp.start()

    def finish(ins, outs, sems):
        mine, sends, arrivals = plan(ins, outs, sems)
        for cp in arrivals:
            cp.wait_recv()
        for cp in sends:
            cp.wait_send()
        for cp in mine:
            cp.wait()

    return _Comm(arrays, [SDS((N_DEV,) + a.shape, a.dtype) for a in arrays],
                 _dma_sems(7 * n) + [pltpu.SemaphoreType.DMA((n,))], start, finish)


def _exchange(arrays, n_blocks, route):
    n = len(arrays)

    def plan(ins, outs, sems):
        send_sems, recv_sems = sems
        cps = []
        for a in range(n):
            for j, (src, peer) in enumerate(route(*_coords())):
                cps.append(pltpu.make_async_remote_copy(
                    src_ref=ins[a].at[src], dst_ref=outs[a].at[j], send_sem=send_sems.at[n_blocks * a + j],
                    recv_sem=recv_sems.at[n_blocks * a + j], device_id=peer, device_id_type=MESH))
        return cps

    def start(ins, outs, sems):
        for cp in plan(ins, outs, sems):
            cp.start()

    def finish(ins, outs, sems):
        for cp in plan(ins, outs, sems):
            cp.wait()

    return _Comm(arrays, [SDS((n_blocks,) + a.shape[1:], a.dtype) for a in arrays], _dma_sems(n_blocks * n),
                 start, finish)


def _sibling_exchange(arrays):
    return _exchange(arrays, 4, lambda x, y, c: [(2 * k + 1 - c, (x, y, 1 - c)) for k in range(4)])


def _chip_exchange(arrays):
    return _exchange(arrays, 3, lambda x, y, c: [(2 * cx + cy, (cx, cy, c))
                                                 for cx, cy in ((1 - x, y), (x, 1 - y), (1 - x, 1 - y))])


def _run(comm, name):
    def body(*refs):
        k_in, k_out = len(comm.inputs), len(comm.out_shapes)
        ins, outs, sems = refs[:k_in], refs[k_in:k_in + k_out], refs[k_in + k_out:]
        comm.start(ins, outs, sems)
        comm.finish(ins, outs, sems)

    return pl.pallas_call(body, name=name, out_shape=comm.out_shapes, in_specs=[ANY] * len(comm.inputs),
                          out_specs=(ANY,) * len(comm.out_shapes), scratch_shapes=comm.sem_shapes)(*comm.inputs)


TILE_ELEMS = 192 * 1024


def _row_tile(R, C):
    if R * C <= TILE_ELEMS:
        return R
    return max(t for t in range(SUBLANES, R, SUBLANES) if R % t == 0 and t * C <= TILE_ELEMS)


def _pair_sum(g8, recv, c_idx, name):
    R, C = g8.shape[-2:]
    t = _row_tile(R, C)

    def body(c_ref, g_ref, r_ref, o_ref):
        o_ref[...] = (g_ref[...] + r_ref[...]).astype(bf16)

    return pl.pallas_call(
        body, name="pair_sum_" + name, out_shape=SDS((4, R, C), bf16),
        grid_spec=pltpu.PrefetchScalarGridSpec(
            num_scalar_prefetch=1, grid=(4, R // t),
            in_specs=[pl.BlockSpec((None, t, C), lambda k, i, c: (2 * k + c[0], i, 0)),
                      pl.BlockSpec((None, t, C), lambda k, i, c: (k, i, 0))],
            out_specs=pl.BlockSpec((None, t, C), lambda k, i, c: (k, i, 0))),
        compiler_params=_params(("parallel", "parallel")))(c_idx, g8, recv)


def _adamw_math(w, g, m, v):
    m = ADAM_B1 * m + (1.0 - ADAM_B1) * g
    v = ADAM_B2 * v + (1.0 - ADAM_B2) * (g * g)
    m_hat = m / (1.0 - ADAM_B1 ** ADAM_STEP)
    v_hat = v / (1.0 - ADAM_B2 ** ADAM_STEP)
    delta = -ADAM_LR * (m_hat / (jnp.sqrt(v_hat) + ADAM_EPS) + ADAM_WD * w)
    return delta, m, v


def _adamw_sharded(pa, recv, k_idx, w, m, v, name):
    R, C = w.shape
    t = _row_tile(R, C)

    def body(k_ref, p_ref, r_ref, w_ref, m_ref, v_ref, g_ref, d_ref, nm_ref, nv_ref):
        g = p_ref[...].astype(f32)
        for j in range(3):
            g = g + r_ref[j].astype(f32)
        d, nm, nv = _adamw_math(w_ref[...], g, m_ref[...], v_ref[...])
        g_ref[...], d_ref[...], nm_ref[...], nv_ref[...] = g, d, nm, nv

    tile = lambda: pl.BlockSpec((t, C), lambda i, k: (i, 0))
    return pl.pallas_call(
        body, name="adamw_" + name, out_shape=(SDS((R, C), f32),) * 4,
        grid_spec=pltpu.PrefetchScalarGridSpec(
            num_scalar_prefetch=1, grid=(R // t,),
            in_specs=[pl.BlockSpec((None, t, C), lambda i, k: (k[0], i, 0)),
                      pl.BlockSpec((3, t, C), lambda i, k: (0, i, 0)), tile(), tile(), tile()],
            out_specs=(tile(), tile(), tile(), tile())),
        compiler_params=_params(("parallel",)))(k_idx, pa, recv, w, m, v)


def _adamw_replicated(parts, ws, ms, vs):
    n = len(ws)

    def body(*refs):
        p, w, m, v = (refs[i * n:(i + 1) * n] for i in range(4))
        outs = refs[4 * n:]
        for a in range(n):
            g = p[a][0].astype(f32)
            for j in range(1, N_DEV):
                g = g + p[a][j].astype(f32)
            d, nm, nv = _adamw_math(w[a][...], g, m[a][...], v[a][...])
            for i, val in enumerate((g, d, nm, nv)):
                outs[i * n + a][...] = val

    shapes = tuple(SDS(w.shape, f32) for w in ws)
    res = pl.pallas_call(body, name="adamw_replicated", out_shape=shapes * 4,
                         compiler_params=_params())(*parts, *ws, *ms, *vs)
    return [res[i * n:(i + 1) * n] for i in range(4)]


def _cols_to_full(g):
    n, r, c = g.shape
    return g.transpose(1, 0, 2).reshape(r, n * c)


def _full_to_cols(a):
    r, c = a.shape
    return a.reshape(r, N_DEV, c // N_DEV).transpose(1, 0, 2)


def _rows_blocked(a):
    r, c = a.shape
    return a.reshape(N_DEV, r // N_DEV, c)


def _w_in_to_internal(w):
    K = w.shape[0]
    q = w[:, 1280:2816].reshape(K, NG, KVH, 1, HD).transpose(0, 2, 1, 3, 4).reshape(K, KVH, NG, HD)
    k = w[:, 2816:3328].reshape(K, KVH, 1, HD)
    v = w[:, 3328:3840].reshape(K, KVH, 1, HD)
    att = jnp.concatenate([q, k, v], axis=2).reshape(K, KVH * ATT_COLS)
    return jnp.concatenate([w[:, :1280], att, w[:, 3840:]], axis=1)


def _w_in_from_internal(w):
    K = w.shape[0]
    att = w[:, C_ATT:C_GATE].reshape(K, KVH, 5, HD)
    q = att[:, :, 0:3].transpose(0, 2, 1, 3).reshape(K, NG * KVH * HD)
    k = att[:, :, 3].reshape(K, KVH * HD)
    v = att[:, :, 4].reshape(K, KVH * HD)
    return jnp.concatenate([w[:, :C_ATT], q, k, v, w[:, C_GATE:]], axis=1)


_IN_NAMES = ('x', 'rel_bias', 'norm_mix_pre', 'norm_mix_post', 'w_in', 'conv_rnn_w', 'conv_rnn_b', 'w_rg_a', 'b_rg_a',
             'w_rg_x', 'b_rg_x', 'lru_lambda', 'w_branch_rnn', 'w_branch_att', 'w_out', 'norm_ffn_pre',
             'norm_ffn_post', 'w_ffn_gate', 'w_ffn_up', 'conv_ffn_w', 'conv_ffn_b', 'w_ffn_down')
_WEIGHTS = _IN_NAMES[1:]
_SHARDED = {"w_in": "col", "conv_rnn_w": "col", "w_branch_rnn": "row", "w_branch_att": "col", "w_out": "row",
            "w_ffn_gate": "col", "w_ffn_up": "col", "conv_ffn_w": "col", "w_ffn_down": "row"}
_REPLICATED = tuple(n for n in _WEIGHTS if n not in _SHARDED)
_EARLY = ("w_branch_rnn", "w_branch_att", "w_out", "w_ffn_gate", "w_ffn_up", "conv_ffn_w", "w_ffn_down")
_LATE = ("w_in", "conv_rnn_w")


def _flat2(a):
    return a.reshape(-1, a.shape[-1])


def _train_step(inp):
    x_idx, y_idx, c_idx = _coords()
    W = {n: inp[n] for n in _WEIGHTS}
    x = inp["x"].reshape(-1, D)
    target = inp["loss_target"].reshape(-1, D)
    shard = {n: inp[n][0] for n in _SHARDED}

    g_in, g_cr, g_cf = _run(_gather_two_level([shard["w_in"].astype(bf16), shard["conv_rnn_w"],
                                               shard["conv_ffn_w"]]), "ag_w_in")
    w_in = _w_in_to_internal(_cols_to_full(g_in))
    cw_rnn, cw_ffn = _cols_to_full(g_cr), _cols_to_full(g_cf)
    later = ("w_branch_rnn", "w_branch_att", "w_out", "w_ffn_gate", "w_ffn_up", "w_ffn_down")
    ag_rest = _gather_direct([shard[n].astype(bf16) for n in later])

    wa, wx = W["w_rg_a"][0], W["w_rg_x"][0]
    buckets = jnp.asarray(_bucket_maps())

    hn = _norm_in(x, W["norm_mix_pre"])
    proj = _mm(hn, w_in, "nn", f32, "mm_proj", 512, IN_W // 2, 1024, cols_outer=True)
    h_rnn = _rnn_fwd(proj, cw_rnn, W["conv_rnn_b"], wa, W["b_rg_a"], wx, W["b_rg_x"], W["lru_lambda"])
    bias = _bias_tables(W["rel_bias"], buckets)
    (o_att, lse), gathered = _att_fwd(proj, bias, comm=ag_rest)
    gathered = dict(zip(later, gathered))
    w_brnn = gathered["w_branch_rnn"].reshape(RNN_W, D)
    w_batt = _cols_to_full(gathered["w_branch_att"])
    w_out = gathered["w_out"].reshape(D, D)
    w_gate, w_up = _cols_to_full(gathered["w_ffn_gate"]), _cols_to_full(gathered["w_ffn_up"])
    w_down = gathered["w_ffn_down"].reshape(FFN_W, D)
    pr = _mm(h_rnn, w_brnn, "nn", bf16, "mm_pr", 1024, 1024, 1280)
    pa = _mm(o_att, w_batt, "nn", bf16, "mm_pa", 1024, 1024, 512)
    merged = _merge_fwd(proj, pr, pa)
    mix = _mm(merged, w_out, "nn", f32, "mm_mix", 1024, 1024, 1024)
    h1, hn2 = _mid_fwd(x, mix, W["norm_mix_post"], W["norm_ffn_pre"])
    gpre = _mm(hn2, w_gate, "nn", bf16, "mm_gate", 1024, 1024, 1024, cols_outer=True)
    up = _mm(hn2, w_up, "nn", bf16, "mm_up", 1024, 1024, 1024, cols_outer=True)
    act = _ffn_act_fwd(gpre, up, cw_ffn, W["conv_ffn_b"])
    ff = _mm(act, w_down, "nn", f32, "mm_down", 1024, 1024, 1024)
    loss_part, dy, dff, dg_fpost = _final(h1, ff, W["norm_ffn_post"], target)

    grads = {}
    dact = _mm(dff, w_down, "nt", bf16, "mm_dact", 1024, 1024, 1024, cols_outer=True)
    grads["w_ffn_down"] = _rows_blocked(_mm(act, dff, "tn", f32, "mm_dw_down", 1024, 1024, 2048))
    dgpre, dup, dcw_ffn, dcb_ffn = _ffn_act_bwd(gpre, up, cw_ffn, W["conv_ffn_b"], dact)
    grads["conv_ffn_w"] = _full_to_cols(dcw_ffn)
    grads["w_ffn_gate"] = _full_to_cols(_mm(hn2, dgpre, "tn", f32, "mm_dw_gate", 1024, 1024, 2048))
    grads["w_ffn_up"] = _full_to_cols(_mm(hn2, dup, "tn", f32, "mm_dw_up", 1024, 1024, 2048))
    dhn2 = _mm(dgpre, w_gate, "nt", f32, "mm_dhn2_a", 1024, 1024, 1024)
    dhn2 = _mm(dup, w_up, "nt", f32, "mm_dhn2_b", 1024, 1024, 1024, add=dhn2)
    dh1, dmix, dg_fpre, dg_post = _mid_bwd(dy, dhn2, h1, W["norm_ffn_pre"], mix, W["norm_mix_post"])
    dmerged = _mm(dmix, w_out, "nt", bf16, "mm_dmerged", 1024, 1024, 1024)
    grads["w_out"] = _rows_blocked(_mm(merged, dmix, "tn", f32, "mm_dw_out", 1024, 1024, 2048))
    dprpa, dgates = _merge_bwd(proj, pr, pa, dmerged)
    dpr, dpa = dprpa[:, :D], dprpa[:, D:]
    dh_rnn = _mm(dpr, w_brnn, "nt", bf16, "mm_dh_rnn", 1024, 1280, 1024)
    grads["w_branch_rnn"] = _rows_blocked(_mm(h_rnn, dpr, "tn", f32, "mm_dw_brnn", 1280, 1024, 1024))
    do_att = _mm(dpa, w_batt, "nt", f32, "mm_do_att", 1024, 512, 1024)
    grads["w_branch_att"] = _full_to_cols(_mm(o_att, dpa, "tn", f32, "mm_dw_batt", 512, 1024, 2048))

    c_arr = jnp.reshape(c_idx, (1,)).astype(jnp.int32)
    k_arr = jnp.reshape(2 * x_idx + y_idx, (1,)).astype(jnp.int32)
    (dqkv, dbias), from_sibling = _att_bwd(proj, bias, o_att, lse, do_att,
                                           comm=_sibling_exchange([grads[n] for n in _EARLY]))
    pair = {n: _pair_sum(grads[n], r, c_arr, n) for n, r in zip(_EARLY, from_sibling)}
    drel = _bias_grad(dbias, buckets)
    (dxr, dcw_rnn, dcb_rnn, dwa, dba, dwx, dbx, dlam), from_chips = _rnn_bwd(
        proj, h_rnn, dh_rnn, cw_rnn, W["conv_rnn_b"], wa, W["b_rg_a"], wx, W["b_rg_x"], W["lru_lambda"],
        comm=_chip_exchange([pair[n] for n in _EARLY]))
    from_chips = dict(zip(_EARLY, from_chips))
    gsmall = {"rel_bias": drel, "norm_mix_post": dg_post, "conv_rnn_b": dcb_rnn, "w_rg_a": dwa.astype(bf16),
              "b_rg_a": dba, "w_rg_x": dwx.astype(bf16), "b_rg_x": dbx, "lru_lambda": dlam,
              "norm_ffn_pre": dg_fpre, "norm_ffn_post": dg_fpost, "conv_ffn_b": dcb_ffn}
    dw_in_a, parts = _mm(hn, dqkv, "tn", f32, "mm_dw_in_a", 1024, 1280, 1024,
                         comm=_gather_direct([_flat2(gsmall[n]) for n in gsmall]))
    parts = dict(zip(gsmall, parts))
    dw_in = jnp.concatenate([_mm(hn, dxr, "tn", f32, "mm_dw_in_r", 1024, 1280, 1024), dw_in_a,
                             _mm(hn, dgates, "tn", f32, "mm_dw_in_g", 1024, 1024, 2048)], axis=1)
    grads["w_in"] = _full_to_cols(_w_in_from_internal(dw_in))
    grads["conv_rnn_w"] = _full_to_cols(dcw_rnn)
    dhn, from_sibling = _mm(dxr, w_in[:, :C_ATT], "nt", f32, "mm_dhn_r", 1024, 1024, 1280,
                            comm=_sibling_exchange([grads[n] for n in _LATE]))
    pair.update({n: _pair_sum(grads[n], r, c_arr, n) for n, r in zip(_LATE, from_sibling)})
    dhn, late = _mm(dqkv, w_in[:, C_ATT:C_GATE], "nt", f32, "mm_dhn_a", 1024, 1024, 1280, add=dhn,
                    comm=_chip_exchange([pair[n] for n in _LATE]))
    from_chips.update(zip(_LATE, late))
    dhn = _mm(dgates, w_in[:, C_GATE:], "nt", f32, "mm_dhn_g", 1024, 1024, 1024, add=dhn)
    dx, dg_pre = _in_bwd(dh1, dhn, x, W["norm_mix_pre"])
    parts["norm_mix_pre"], = _run(_gather_two_level([dg_pre]), "ag_norm_mix_pre")
    parts = [parts[n] for n in _REPLICATED]

    out = {}
    for n in _SHARDED:
        res = _adamw_sharded(pair[n], from_chips[n], k_arr, shard[n], inp["m_" + n][0], inp["v_" + n][0], n)
        out[n] = [r[None] for r in res]
    small = _adamw_replicated(parts, *[[_flat2(inp[p + n]) for n in _REPLICATED] for p in ("", "m_", "v_")])
    for a, n in enumerate(_REPLICATED):
        out[n] = [small[i][a].reshape(inp[n].shape) for i in range(4)]

    loss = lax.psum(loss_part[0, 0], ("x", "y", "c"))
    outs = [loss, dx.reshape(inp["x"].shape)]
    for i in range(4):
        outs.extend(out[n][i] for n in _WEIGHTS)
    return tuple(outs)


def kernel(x, rel_bias, norm_mix_pre, norm_mix_post, w_in, conv_rnn_w, conv_rnn_b, w_rg_a, b_rg_a, w_rg_x, b_rg_x, lru_lambda, w_branch_rnn, w_branch_att, w_out, norm_ffn_pre, norm_ffn_post, w_ffn_gate, w_ffn_up, conv_ffn_w, conv_ffn_b, w_ffn_down, loss_target, m_rel_bias, m_norm_mix_pre, m_norm_mix_post, m_w_in, m_conv_rnn_w, m_conv_rnn_b, m_w_rg_a, m_b_rg_a, m_w_rg_x, m_b_rg_x, m_lru_lambda, m_w_branch_rnn, m_w_branch_att, m_w_out, m_norm_ffn_pre, m_norm_ffn_post, m_w_ffn_gate, m_w_ffn_up, m_conv_ffn_w, m_conv_ffn_b, m_w_ffn_down, v_rel_bias, v_norm_mix_pre, v_norm_mix_post, v_w_in, v_conv_rnn_w, v_conv_rnn_b, v_w_rg_a, v_b_rg_a, v_w_rg_x, v_b_rg_x, v_lru_lambda, v_w_branch_rnn, v_w_branch_att, v_w_out, v_norm_ffn_pre, v_norm_ffn_post, v_w_ffn_gate, v_w_ffn_up, v_conv_ffn_w, v_conv_ffn_b, v_w_ffn_down):
    vals = locals()
    names = list(_IN_NAMES) + ["loss_target"] + ["m_" + n for n in _WEIGHTS] + ["v_" + n for n in _WEIGHTS]
    return _train_step({n: vals[n] for n in names})
```

```python
import functools
import math

import numpy as np
import jax
import jax.numpy as jnp
from jax import lax
from jax.experimental import pallas as pl
from jax.experimental.pallas import tpu as pltpu

f32, bf16 = jnp.float32, jnp.bfloat16
SDS = jax.ShapeDtypeStruct
MESH = pl.DeviceIdType.MESH
ANY = pl.BlockSpec(memory_space=pl.ANY)

D = 1024
SEQ = 2048
RNN_W = 1280
RNN_BLOCKS = 10
LANES = 128
SUBLANES = 8
RNN_CONV = 4
LRU_C = 8.0
HD = 128
KVH = 4
DILATIONS = (1, 4, 16)
NG = 3
ATT_BLK = 128
NBLK_SEQ = SEQ // ATT_BLK
ATT_UNROLL = 8
REL_BUCKETS = 32
REL_MAX_DIST = 2048
FFN_W = 3072
FFN_CONV = 3
EPS = 1e-6
IN_W = 5888
ATT_COLS = 5 * HD
C_ATT = RNN_W
C_GATE = RNN_W + KVH * ATT_COLS
NEG = -1e30

ADAM_LR, ADAM_B1, ADAM_B2, ADAM_EPS, ADAM_WD, ADAM_STEP = 0.001, 0.9, 0.999, 1e-08, 0.01, 10

VMEM_LIMIT_BYTES = 56 * 1024 * 1024
N_DEV = 8


def _params(sem=None):
    return pltpu.CompilerParams(dimension_semantics=sem, vmem_limit_bytes=VMEM_LIMIT_BYTES)


def _sigmoid(x):
    return 1.0 / (1.0 + jnp.exp(-x))


class _Comm:
    def __init__(self, inputs, out_shapes, sem_shapes, start, finish):
        self.inputs, self.out_shapes, self.sem_shapes = tuple(inputs), tuple(out_shapes), list(sem_shapes)
        self.start, self.finish = start, finish


def _call(body, args, *, name, grid, in_specs, out_specs, out_shape, scratch_shapes=(), semantics, comm=None):
    if comm is None:
        return pl.pallas_call(body, name=name, grid=grid, in_specs=list(in_specs), out_specs=tuple(out_specs),
                              out_shape=tuple(out_shape), scratch_shapes=list(scratch_shapes),
                              compiler_params=_params(semantics))(*args), ()
    n_in, n_out, n_scr = len(in_specs), len(out_shape), len(scratch_shapes)
    c_in, c_out = len(comm.inputs), len(comm.out_shapes)

    def fused(*refs):
        ins, refs = refs[:n_in], refs[n_in:]
        cin, refs = refs[:c_in], refs[c_in:]
        outs, refs = refs[:n_out], refs[n_out:]
        cout, refs = refs[:c_out], refs[c_out:]
        scr, csem = refs[:n_scr], refs[n_scr:]
        first = functools.reduce(jnp.logical_and, [pl.program_id(d) == 0 for d in range(len(grid))])
        last = functools.reduce(jnp.logical_and, [pl.program_id(d) == grid[d] - 1 for d in range(len(grid))])

        @pl.when(first)
        def _():
            comm.start(cin, cout, csem)

        body(*ins, *outs, *scr)

        @pl.when(last)
        def _():
            comm.finish(cin, cout, csem)

    res = pl.pallas_call(
        fused, name=name, grid=grid, in_specs=list(in_specs) + [ANY] * c_in,
        out_specs=tuple(out_specs) + (ANY,) * c_out, out_shape=tuple(out_shape) + comm.out_shapes,
        scratch_shapes=list(scratch_shapes) + comm.sem_shapes,
        compiler_params=_params(("arbitrary",) * len(grid)))(*args, *comm.inputs)
    return res[:n_out], res[n_out:]


_DIMS = {"nn": (((1,), (0,)), ((), ())), "nt": (((1,), (1,)), ((), ())), "tn": (((0,), (0,)), ((), ()))}


def _mm(a, b, mode, out_dtype, name, tm, tn, tk, cols_outer=False, comm=None):
    if mode == "nn":
        (M, K), (K2, N) = a.shape, b.shape
    elif mode == "nt":
        (M, K), (N, K2) = a.shape, b.shape
    else:
        (K, M), (K2, N) = a.shape, b.shape
    assert K == K2 and M % tm == 0 and N % tn == 0 and K % tk == 0, (name, a.shape, b.shape)
    nk = K // tk

    def body(a_ref, b_ref, o_ref, *scratch):
        part = lax.dot_general(a_ref[...].astype(bf16), b_ref[...].astype(bf16), _DIMS[mode],
                               preferred_element_type=f32)
        if nk == 1:
            o_ref[...] = part.astype(o_ref.dtype)
        else:
            acc_ref, = scratch
            k = pl.program_id(2)

            @pl.when(k == 0)
            def _():
                acc_ref[...] = part

            @pl.when(k > 0)
            def _():
                acc_ref[...] += part

            @pl.when(k == nk - 1)
            def _():
                o_ref[...] = acc_ref[...].astype(o_ref.dtype)

    def ij(f):
        return (lambda j, i, k: f(i, j, k)) if cols_outer else f

    if mode == "tn":
        a_spec = pl.BlockSpec((tk, tm), ij(lambda i, j, k: (k, i)))
    else:
        a_spec = pl.BlockSpec((tm, tk), ij(lambda i, j, k: (i, k)))
    if mode == "nt":
        b_spec = pl.BlockSpec((tn, tk), ij(lambda i, j, k: (j, k)))
    else:
        b_spec = pl.BlockSpec((tk, tn), ij(lambda i, j, k: (k, j)))
    o_spec = pl.BlockSpec((tm, tn), ij(lambda i, j, k: (i, j)))
    grid = (N // tn, M // tm, nk) if cols_outer else (M // tm, N // tn, nk)
    (out,), extra = _call(
        body, (a, b), name=name, out_shape=(SDS((M, N), out_dtype),), grid=grid, in_specs=[a_spec, b_spec],
        out_specs=(o_spec,), scratch_shapes=[pltpu.VMEM((tm, tn), f32)] if nk > 1 else [],
        semantics=("parallel", "parallel", "arbitrary"), comm=comm)
    return out if comm is None else (out, extra)


def _mm_nt_sum(pairs, name, tm, tn, add=None, comm=None):
    M, N = pairs[0][0].shape[0], pairs[0][1].shape[0]
    nks = [a.shape[1] // tk for a, _, tk in pairs]
    starts = [sum(nks[:p]) for p in range(len(pairs))]
    nk = sum(nks)
    has_add = add is not None

    def body(*refs):
        o_ref, acc_ref = refs[-2], refs[-1]
        k = pl.program_id(2)
        for p in range(len(pairs)):
            def product(p=p):
                return lax.dot_general(refs[2 * p][...], refs[2 * p + 1][...], _DIMS["nt"], preferred_element_type=f32)

            if p == 0:
                @pl.when(k == 0)
                def _():
                    acc_ref[...] = product()

            @pl.when((k >= max(starts[p], 1)) & (k < starts[p] + nks[p]))
            def _():
                acc_ref[...] += product()

        @pl.when(k == nk - 1)
        def _():
            o_ref[...] = acc_ref[...] + refs[-3][...] if has_add else acc_ref[...]

    in_specs, args = [], []
    for (a, b, tk), k0, n in zip(pairs, starts, nks):
        assert a.shape[1] == b.shape[1] and a.shape[1] % tk == 0 and a.dtype == b.dtype == bf16, name
        chunk = lambda k, k0=k0, n=n: jnp.clip(k - k0, 0, n - 1)
        in_specs += [pl.BlockSpec((tm, tk), lambda i, j, k, c=chunk: (i, c(k))),
                     pl.BlockSpec((tn, tk), lambda i, j, k, c=chunk: (j, c(k)))]
        args += [a, b]
    o_spec = pl.BlockSpec((tm, tn), lambda i, j, k: (i, j))
    if has_add:
        in_specs.append(o_spec)
        args.append(add)
    (out,), extra = _call(
        body, args, name=name, out_shape=(SDS((M, N), f32),), grid=(M // tm, N // tn, nk), in_specs=in_specs,
        out_specs=(o_spec,), scratch_shapes=[pltpu.VMEM((tm, tn), f32)],
        semantics=("parallel", "parallel", "arbitrary"), comm=comm)
    return out if comm is None else (out, extra)


ROW_TILE = 512


def _rms_fwd(x, g):
    r = lax.rsqrt(jnp.mean(x * x, axis=-1, keepdims=True) + EPS)
    return x * r * g


def _rms_bwd(x, g, dy):
    r = lax.rsqrt(jnp.mean(x * x, axis=-1, keepdims=True) + EPS)
    xh = x * r
    dxh = dy * g
    dx = r * (dxh - xh * jnp.mean(dxh * xh, axis=-1, keepdims=True))
    return dx, jnp.sum(dy * xh, axis=0, keepdims=True)


def _acc_out(ref, val):
    @pl.when(pl.program_id(0) == 0)
    def _():
        ref[...] = val

    @pl.when(pl.program_id(0) > 0)
    def _():
        ref[...] += val


def _row_spec(width=D):
    return pl.BlockSpec((ROW_TILE, width), lambda i: (i, 0))


def _vec_spec(width=D):
    return pl.BlockSpec((1, width), lambda i: (0, 0))


def _norm_in(x, g):
    def body(x_ref, g_ref, o_ref):
        o_ref[...] = _rms_fwd(x_ref[...], g_ref[...]).astype(bf16)

    T = x.shape[0]
    return pl.pallas_call(body, name="norm_in", out_shape=SDS((T, D), bf16), grid=(T // ROW_TILE,),
                          in_specs=[_row_spec(), _vec_spec()], out_specs=_row_spec(),
                          compiler_params=_params(("parallel",)))(x, g)


def _mid_fwd(x, mix, g_post, g_fpre):
    def body(x_ref, mix_ref, gp_ref, gf_ref, h1_ref, hn2_ref):
        h1 = x_ref[...] + _rms_fwd(mix_ref[...], gp_ref[...])
        h1_ref[...] = h1
        hn2_ref[...] = _rms_fwd(h1, gf_ref[...]).astype(bf16)

    T = x.shape[0]
    return pl.pallas_call(body, name="mid_fwd", out_shape=(SDS((T, D), f32), SDS((T, D), bf16)),
                          grid=(T // ROW_TILE,),
                          in_specs=[_row_spec(), _row_spec(), _vec_spec(), _vec_spec()],
                          out_specs=(_row_spec(), _row_spec()),
                          compiler_params=_params(("parallel",)))(x, mix, g_post, g_fpre)


def _final(h1, ff, g_fpost, target):
    def body(h1_ref, ff_ref, g_ref, t_ref, loss_ref, dy_ref, dff_ref, dg_ref):
        ff = ff_ref[...]
        g = g_ref[...]
        e = h1_ref[...] + _rms_fwd(ff, g) - t_ref[...]
        part = jnp.sum(jnp.sum(e * e, axis=1, keepdims=True), axis=0, keepdims=True) * (0.5 / D)
        dy = e * (1.0 / D)
        dy_ref[...] = dy
        dff, dg = _rms_bwd(ff, g, dy)
        dff_ref[...] = dff.astype(bf16)
        _acc_out(loss_ref, part)
        _acc_out(dg_ref, dg)

    T = h1.shape[0]
    return pl.pallas_call(
        body, name="final", out_shape=(SDS((1, 1), f32), SDS((T, D), f32), SDS((T, D), bf16), SDS((1, D), f32)),
        grid=(T // ROW_TILE,),
        in_specs=[_row_spec(), _row_spec(), _vec_spec(), _row_spec()],
        out_specs=(pl.BlockSpec((1, 1), lambda i: (0, 0)), _row_spec(), _row_spec(), _vec_spec()),
        compiler_params=_params(("arbitrary",)))(h1, ff, g_fpost, target)


def _mid_bwd(dy, dhn2, h1, g_fpre, mix, g_post):
    def body(dy_ref, dhn2_ref, h1_ref, gf_ref, mix_ref, gp_ref, dh1_ref, dmix_ref, dgf_ref, dgp_ref):
        d1, dgf = _rms_bwd(h1_ref[...], gf_ref[...], dhn2_ref[...])
        dh1 = dy_ref[...] + d1
        dh1_ref[...] = dh1
        dmix, dgp = _rms_bwd(mix_ref[...], gp_ref[...], dh1)
        dmix_ref[...] = dmix.astype(bf16)
        _acc_out(dgf_ref, dgf)
        _acc_out(dgp_ref, dgp)

    T = dy.shape[0]
    return pl.pallas_call(
        body, name="mid_bwd", out_shape=(SDS((T, D), f32), SDS((T, D), bf16), SDS((1, D), f32), SDS((1, D), f32)),
        grid=(T // ROW_TILE,),
        in_specs=[_row_spec(), _row_spec(), _row_spec(), _vec_spec(), _row_spec(), _vec_spec()],
        out_specs=(_row_spec(), _row_spec(), _vec_spec(), _vec_spec()),
        compiler_params=_params(("arbitrary",)))(dy, dhn2, h1, g_fpre, mix, g_post)


def _in_bwd(dh1, dhn, x, g_pre):
    def body(dh1_ref, dhn_ref, x_ref, g_ref, dx_ref, dg_ref):
        d, dg = _rms_bwd(x_ref[...], g_ref[...], dhn_ref[...])
        dx_ref[...] = dh1_ref[...] + d
        _acc_out(dg_ref, dg)

    T = x.shape[0]
    return pl.pallas_call(
        body, name="in_bwd", out_shape=(SDS((T, D), f32), SDS((1, D), f32)), grid=(T // ROW_TILE,),
        in_specs=[_row_spec(), _row_spec(), _row_spec(), _vec_spec()], out_specs=(_row_spec(), _vec_spec()),
        compiler_params=_params(("arbitrary",)))(dh1, dhn, x, g_pre)


def _shift_dn(x, d, row, fill=0.0):
    if d == 0:
        return x
    y = pltpu.roll(x, d, 0)
    head = jnp.where(row[:SUBLANES] >= d, y[:SUBLANES], fill)
    return jnp.concatenate([head, y[SUBLANES:]], axis=0)


def _shift_up(x, d, row, fill=0.0):
    if d == 0:
        return x
    n = x.shape[0]
    y = pltpu.roll(x, n - d, 0)
    tail = jnp.where(row[:SUBLANES] < SUBLANES - d, y[n - SUBLANES:], fill)
    return jnp.concatenate([y[:n - SUBLANES], tail], axis=0)


def _conv_fwd(x, w_ref, b, row):
    K = w_ref.shape[0]
    y = b
    for k in range(K):
        y = y + w_ref[k:k + 1, :] * _shift_dn(x, K - 1 - k, row)
    return y


def _conv_bwd(x, w_ref, dy, row):
    K = w_ref.shape[0]
    dx = jnp.zeros_like(dy)
    dws = []
    for k in range(K):
        dx = dx + w_ref[k:k + 1, :] * _shift_up(dy, K - 1 - k, row)
        dws.append(jnp.sum(dy * _shift_dn(x, K - 1 - k, row), axis=0, keepdims=True))
    return dx, dws, jnp.sum(dy, axis=0, keepdims=True)


def _scan_fwd(a, u, row):
    n = a.shape[0]
    d = 1
    while d < n:
        last = 2 * d >= n
        if d < SUBLANES:
            u = u + a * _shift_dn(u, d, row)
            if not last:
                a = a * _shift_dn(a, d, row, fill=1.0)
        else:
            u = jnp.concatenate([u[:d], u[d:] + a[d:] * u[:n - d]], axis=0)
            if not last:
                a = jnp.concatenate([a[:d], a[d:] * a[:n - d]], axis=0)
        d *= 2
    return u


def _scan_bwd(b, u, row):
    n = b.shape[0]
    d = 1
    while d < n:
        last = 2 * d >= n
        if d < SUBLANES:
            u = u + b * _shift_up(u, d, row)
            if not last:
                b = b * _shift_up(b, d, row, fill=1.0)
        else:
            u = jnp.concatenate([u[:n - d] + b[:n - d] * u[d:], u[n - d:]], axis=0)
            if not last:
                b = jnp.concatenate([b[:n - d] * b[d:], b[n - d:]], axis=0)
        d *= 2
    return u


def _neg_expm1(z):
    series = -z * (1.0 + z * (0.5 + z * (1.0 / 6.0 + z * (1.0 / 24.0 + z * (1.0 / 120.0)))))
    return jnp.where(z > -0.1, series, 1.0 - jnp.exp(z))


def _rnn_gates(xr, cw_ref, cb, wa, ba, wx, bx, lam, row):
    xc = _conv_fwd(xr, cw_ref, cb, row)
    xcb = xc.astype(bf16)
    r = _sigmoid(jnp.dot(xcb, wa.astype(bf16), preferred_element_type=f32) + ba)
    i = _sigmoid(jnp.dot(xcb, wx.astype(bf16), preferred_element_type=f32) + bx)
    z = -lam
    sp = jnp.maximum(z, 0.0) + jnp.log(1.0 + jnp.exp(-jnp.abs(z)))
    log_a = (-LRU_C * sp) * r
    a = jnp.exp(log_a)
    s = jnp.sqrt(_neg_expm1(2.0 * log_a))
    return xc, xcb, r, i, sp, a, s


def _rnn_specs(B):
    blk = lambda: pl.BlockSpec((SEQ, LANES), lambda b, n: (b, n))
    return dict(
        act=blk,
        convw=pl.BlockSpec((RNN_CONV, LANES), lambda b, n: (0, n)),
        vec=lambda: pl.BlockSpec((1, LANES), lambda b, n: (0, n)),
        gate=lambda: pl.BlockSpec((None, LANES, LANES), lambda b, n: (n, 0, 0)),
    )


def _rnn_fwd(proj, cw, cb, wa, ba, wx, bx, lam, comm=None):
    T = proj.shape[0]
    B = T // SEQ

    def body(x_ref, cw_ref, cb_ref, wa_ref, ba_ref, wx_ref, bx_ref, lam_ref, h_ref):
        row = lax.broadcasted_iota(jnp.int32, (SEQ, LANES), 0)
        xc, _, r, i, sp, a, s = _rnn_gates(x_ref[...], cw_ref, cb_ref[...], wa_ref[...], ba_ref[...],
                                          wx_ref[...], bx_ref[...], lam_ref[...], row)
        h_ref[...] = _scan_fwd(a, s * (i * xc), row)

    sp_ = _rnn_specs(B)
    (h,), extra = _call(
        body, (proj, cw, cb, wa, ba, wx, bx, lam), name="rnn_fwd", out_shape=(SDS((T, RNN_W), f32),),
        grid=(B, RNN_BLOCKS),
        in_specs=[sp_["act"](), sp_["convw"], sp_["vec"](), sp_["gate"](), sp_["vec"](), sp_["gate"](),
                  sp_["vec"](), sp_["vec"]()],
        out_specs=(sp_["act"](),), semantics=("parallel", "parallel"), comm=comm)
    return h, extra


def _rnn_bwd(proj, h, dh, cw, cb, wa, ba, wx, bx, lam, comm=None):
    T = proj.shape[0]
    B = T // SEQ

    def body(x_ref, h_ref, dh_ref, cw_ref, cb_ref, wa_ref, ba_ref, wx_ref, bx_ref, lam_ref,
             dx_ref, dcw_ref, dcb_ref, dwa_ref, dba_ref, dwx_ref, dbx_ref, dlam_ref):
        row = lax.broadcasted_iota(jnp.int32, (SEQ, LANES), 0)
        xr = x_ref[...]
        wa, wx, lam = wa_ref[...], wx_ref[...], lam_ref[...]
        xc, xcb, r, i, sp, a, s = _rnn_gates(xr, cw_ref, cb_ref[...], wa, ba_ref[...], wx, bx_ref[...], lam, row)
        hprev = _shift_dn(h_ref[...], 1, row)
        g = _scan_bwd(_shift_up(a, 1, row), dh_ref[...].astype(f32), row)
        da = g * hprev
        ds = g * (i * xc)
        di = g * (s * xc)
        dxc = g * (s * i)
        dla = da * a - ds * (a * a) / s
        dr = dla * (-LRU_C * sp)
        dsp = jnp.sum(dla * (-LRU_C * r), axis=0, keepdims=True)
        dlam = -dsp * _sigmoid(-lam)
        dga = dr * r * (1.0 - r)
        dgx = di * i * (1.0 - i)
        dgab, dgxb = dga.astype(bf16), dgx.astype(bf16)
        dwa = lax.dot_general(xcb, dgab, _DIMS["tn"], preferred_element_type=f32)
        dwx = lax.dot_general(xcb, dgxb, _DIMS["tn"], preferred_element_type=f32)
        dxc = dxc + lax.dot_general(dgab, wa.astype(bf16), _DIMS["nt"], preferred_element_type=f32)
        dxc = dxc + lax.dot_general(dgxb, wx.astype(bf16), _DIMS["nt"], preferred_element_type=f32)
        dx, dws, db = _conv_bwd(xr, cw_ref, dxc, row)
        dx_ref[...] = dx.astype(bf16)
        first = pl.program_id(1) == 0

        def acc(ref, val):
            @pl.when(first)
            def _():
                ref[...] = val

            @pl.when(jnp.logical_not(first))
            def _():
                ref[...] += val

        for k in range(RNN_CONV):
            acc(dcw_ref.at[k:k + 1, :], dws[k])
        acc(dcb_ref, db)
        acc(dwa_ref, dwa)
        acc(dba_ref, jnp.sum(dga, axis=0, keepdims=True))
        acc(dwx_ref, dwx)
        acc(dbx_ref, jnp.sum(dgx, axis=0, keepdims=True))
        acc(dlam_ref, dlam)

    blk = lambda: pl.BlockSpec((SEQ, LANES), lambda n, b: (b, n))
    convw = lambda: pl.BlockSpec((RNN_CONV, LANES), lambda n, b: (0, n))
    vec = lambda: pl.BlockSpec((1, LANES), lambda n, b: (0, n))
    gate = lambda: pl.BlockSpec((None, LANES, LANES), lambda n, b: (n, 0, 0))
    vshape = SDS((1, RNN_W), f32)
    gshape = SDS((RNN_BLOCKS, LANES, LANES), f32)
    return _call(
        body, (proj, h, dh, cw, cb, wa, ba, wx, bx, lam), name="rnn_bwd",
        out_shape=(SDS((T, RNN_W), bf16), SDS((RNN_CONV, RNN_W), f32), vshape, gshape, vshape, gshape, vshape, vshape),
        grid=(RNN_BLOCKS, B),
        in_specs=[blk(), blk(), blk(), convw(), vec(), gate(), vec(), gate(), vec(), vec()],
        out_specs=(blk(), convw(), vec(), gate(), vec(), gate(), vec(), vec()),
        semantics=("parallel", "arbitrary"), comm=comm)


def _t5_bucket(dist):
    max_exact = REL_BUCKETS // 2
    d = np.maximum(dist, 1).astype(np.float32)
    large = max_exact + np.log(d / max_exact) / math.log(REL_MAX_DIST / max_exact) * (REL_BUCKETS - max_exact)
    large = np.minimum(large.astype(np.int32), REL_BUCKETS - 1)
    return np.where(dist < max_exact, dist, large).astype(np.int32)


def _bucket_maps():
    qi = np.arange(ATT_BLK)[:, None]
    kj = np.arange(2 * ATT_BLK)[None, :]
    delta = ATT_BLK + qi - kj
    valid = (delta >= 0) & (delta <= ATT_BLK)
    maps = [np.where(valid, _t5_bucket(np.maximum(delta, 0) * r), -1) for r in DILATIONS]
    return np.stack(maps).astype(np.int32)


def _bias_tables(rel_bias, buckets):
    def body(rb_ref, bk_ref, o_ref):
        for g in range(NG):
            bk = bk_ref[g]
            for h in range(KVH):
                acc = jnp.full(bk.shape, NEG, f32)
                for b in range(REL_BUCKETS):
                    acc = jnp.where(bk == b, rb_ref[b, g * KVH + h], acc)
                o_ref[h, g] = acc

    return pl.pallas_call(
        body, name="bias_tables", out_shape=SDS((KVH, NG, ATT_BLK, 2 * ATT_BLK), f32),
        in_specs=[pl.BlockSpec(memory_space=pltpu.SMEM), pl.BlockSpec(memory_space=pltpu.VMEM)],
        out_specs=pl.BlockSpec(memory_space=pltpu.VMEM), compiler_params=_params())(rel_bias, buckets)


def _bias_grad(dbias, buckets):
    def body(db_ref, bk_ref, o_ref):
        rr = lax.broadcasted_iota(jnp.int32, (REL_BUCKETS, NG * KVH), 0)
        cc = lax.broadcasted_iota(jnp.int32, (REL_BUCKETS, NG * KVH), 1)
        out = jnp.zeros((REL_BUCKETS, NG * KVH), f32)
        for g in range(NG):
            bk = bk_ref[g]
            for h in range(KVH):
                d = db_ref[h, g]
                for b in range(REL_BUCKETS):
                    m = jnp.where(bk == b, d, 0.0)
                    s = jnp.sum(jnp.sum(m, axis=1, keepdims=True), axis=0, keepdims=True)
                    out = jnp.where((rr == b) & (cc == g * KVH + h), s, out)
        o_ref[...] = out

    return pl.pallas_call(body, name="bias_grad", out_shape=SDS((REL_BUCKETS, NG * KVH), f32),
                          compiler_params=_params())(dbias, buckets)


def _to_sub(dst_ref, src_ref, r, dtype, offset=0):
    M = SEQ // r
    for c in range(r):
        if r == 1:
            v = src_ref[...]
        else:
            v = src_ref[pl.ds(c, M, stride=r), :]
        dst_ref[pl.ds(offset + c * M, M), :] = v.astype(dtype)


def _from_sub(dst_ref, src_ref, r, accumulate=False, offset=0):
    M = SEQ // r
    for c in range(r):
        v = src_ref[pl.ds(offset + c * M, M), :]
        idx = slice(None) if r == 1 else pl.ds(c, M, stride=r)
        if accumulate:
            dst_ref[idx, :] = dst_ref[idx, :] + v
        else:
            dst_ref[idx, :] = v


_COL = lambda k: slice(k * HD, (k + 1) * HD)
SCALE = HD ** -0.5


def _qkv_spec(k, bh):
    def index(*ids):
        b, h = bh(*ids)
        return (b, C_ATT // HD + 5 * h + k)

    return pl.BlockSpec((SEQ, HD), index)


def _key_window(bias_ref, g, nb):
    if nb == 1:
        bias_own = bias_ref[g, :, ATT_BLK:2 * ATT_BLK]
        return lambda j: (pl.ds(pl.multiple_of((j + 1) * ATT_BLK, ATT_BLK), ATT_BLK), bias_own)
    bias_g = bias_ref[g]
    col = lax.broadcasted_iota(jnp.int32, bias_g.shape, 1)
    bias_first = jnp.where(col >= ATT_BLK, bias_g, NEG)
    return lambda j: (pl.ds(pl.multiple_of(j * ATT_BLK, ATT_BLK), 2 * ATT_BLK),
                      jnp.where(j % nb != 0, bias_g, bias_first))


def _att_fwd(proj, bias, comm=None):
    T = proj.shape[0]
    B = T // SEQ

    def body(q0_ref, q1_ref, q2_ref, k_ref, v_ref, bias_ref, o_ref, lse_ref, qp, kp, vp, kt, op, lp, og, lg):
        q_refs = (q0_ref, q1_ref, q2_ref)
        kp[0:ATT_BLK, :] = jnp.zeros((ATT_BLK, HD), bf16)
        vp[0:ATT_BLK, :] = jnp.zeros((ATT_BLK, HD), bf16)
        for g, r in enumerate(DILATIONS):
            nb = NBLK_SEQ // r
            _to_sub(qp, q_refs[g], r, bf16)
            _to_sub(kp, k_ref, r, bf16, offset=ATT_BLK)
            _to_sub(vp, v_ref, r, bf16, offset=ATT_BLK)
            kt[...] = kp[...].T
            keys = _key_window(bias_ref, g, nb)

            def step(j, carry):
                cur = pl.ds(pl.multiple_of(j * ATT_BLK, ATT_BLK), ATT_BLK)
                win, bias_j = keys(j)
                s = jnp.dot(qp[cur, :], kt[:, win], preferred_element_type=f32) * SCALE + bias_j
                m = jnp.max(s, axis=-1, keepdims=True)
                p = jnp.exp(s - m)
                den = jnp.sum(p, axis=-1, keepdims=True)
                o = jnp.dot(p.astype(bf16), vp[win, :], preferred_element_type=f32)
                op[cur, :] = o / den
                lp[cur, :] = jnp.broadcast_to(m + jnp.log(den), (ATT_BLK, HD))
                return carry

            lax.fori_loop(0, NBLK_SEQ, step, 0, unroll=ATT_UNROLL)
            _from_sub(og.at[g], op, r)
            _from_sub(lg.at[g], lp, r)
        l0, l1, l2 = lg[0], lg[1], lg[2]
        mx = jnp.maximum(jnp.maximum(l0, l1), l2)
        e0, e1, e2 = jnp.exp(l0 - mx), jnp.exp(l1 - mx), jnp.exp(l2 - mx)
        den = e0 + e1 + e2
        o_ref[...] = (e0 * og[0] + e1 * og[1] + e2 * og[2]) / den
        lse_ref[...] = mx + jnp.log(den)

    return _call(
        body, (proj, proj, proj, proj, proj, bias), name="att_fwd",
        out_shape=(SDS((T, KVH * HD), f32), SDS((KVH, T, HD), f32)), grid=(B, KVH),
        in_specs=[_qkv_spec(k, lambda b, h: (b, h)) for k in range(5)]
                 + [pl.BlockSpec((None, NG, ATT_BLK, 2 * ATT_BLK), lambda b, h: (h, 0, 0, 0))],
        out_specs=(pl.BlockSpec((SEQ, HD), lambda b, h: (b, h)),
                   pl.BlockSpec((None, SEQ, HD), lambda b, h: (h, b, 0))),
        scratch_shapes=[pltpu.VMEM((SEQ, HD), bf16)] + [pltpu.VMEM((SEQ + ATT_BLK, HD), bf16)] * 2
                       + [pltpu.VMEM((HD, SEQ + ATT_BLK), bf16)]
                       + [pltpu.VMEM((SEQ, HD), f32)] * 2 + [pltpu.VMEM((NG, SEQ, HD), f32)] * 2,
        semantics=("parallel", "parallel"), comm=comm)


def _att_bwd(proj, bias, o, lse, do, comm=None):
    T = proj.shape[0]
    B = T // SEQ

    def body(q0_ref, q1_ref, q2_ref, k_ref, v_ref, bias_ref, o_ref, lse_ref, do_ref, dx_ref, db_ref,
             qp, kp, vp, dop, qt, kt, vt, dot, lp, dlp, dqp, dkt, dvt, dln, nat, dkn, dvn):
        q_refs = (q0_ref, q1_ref, q2_ref)
        first = pl.program_id(1) == 0

        @pl.when(first)
        def _():
            db_ref[...] = jnp.zeros_like(db_ref)

        dln[...] = jnp.broadcast_to(jnp.sum(do_ref[...] * o_ref[...], axis=-1, keepdims=True), (SEQ, HD))
        dkn[...] = jnp.zeros_like(dkn)
        dvn[...] = jnp.zeros_like(dvn)
        kp[0:ATT_BLK, :] = jnp.zeros((ATT_BLK, HD), bf16)
        vp[0:ATT_BLK, :] = jnp.zeros((ATT_BLK, HD), bf16)
        for g, r in enumerate(DILATIONS):
            nb = NBLK_SEQ // r
            _to_sub(qp, q_refs[g], r, bf16)
            _to_sub(kp, k_ref, r, bf16, offset=ATT_BLK)
            _to_sub(vp, v_ref, r, bf16, offset=ATT_BLK)
            _to_sub(dop, do_ref, r, bf16)
            _to_sub(lp, lse_ref, r, f32)
            _to_sub(dlp, dln, r, f32)
            qt[...], kt[...], vt[...], dot[...] = qp[...].T, kp[...].T, vp[...].T, dop[...].T
            dkt[...] = jnp.zeros_like(dkt)
            dvt[...] = jnp.zeros_like(dvt)
            keys = _key_window(bias_ref, g, nb)
            db_cols = slice(ATT_BLK, 2 * ATT_BLK) if nb == 1 else slice(None)

            def step(j, carry):
                cur = pl.ds(pl.multiple_of(j * ATT_BLK, ATT_BLK), ATT_BLK)
                win, bias_j = keys(j)
                s = jnp.dot(qp[cur, :], kt[:, win], preferred_element_type=f32) * SCALE + bias_j
                p = jnp.exp(s - lp[cur, 0:1])
                dp = jnp.dot(dop[cur, :], vt[:, win], preferred_element_type=f32)
                ds = p * (dp - dlp[cur, 0:1])
                db_ref[g, :, db_cols] += ds
                dsb, pb = ds.astype(bf16), p.astype(bf16)
                dqp[cur, :] = jnp.dot(dsb, kp[win, :], preferred_element_type=f32) * SCALE
                dkt[:, win] += jnp.dot(qt[:, cur], dsb, preferred_element_type=f32) * SCALE
                dvt[:, win] += jnp.dot(dot[:, cur], pb, preferred_element_type=f32)
                return carry

            lax.fori_loop(0, NBLK_SEQ, step, 0, unroll=ATT_UNROLL)
            _from_sub(nat, dqp, r)
            dx_ref[:, _COL(g)] = nat[...].astype(bf16)
            dqp[...] = dkt[:, ATT_BLK:].T
            _from_sub(dkn, dqp, r, accumulate=True)
            dqp[...] = dvt[:, ATT_BLK:].T
            _from_sub(dvn, dqp, r, accumulate=True)
        dx_ref[:, _COL(3)] = dkn[...].astype(bf16)
        dx_ref[:, _COL(4)] = dvn[...].astype(bf16)

    blk = lambda: pl.BlockSpec((SEQ, HD), lambda h, b: (b, h))
    bias_spec = lambda: pl.BlockSpec((None, NG, ATT_BLK, 2 * ATT_BLK), lambda h, b: (h, 0, 0, 0))
    pad = lambda dtype: pltpu.VMEM((SEQ + ATT_BLK, HD), dtype)
    pad_t = lambda dtype: pltpu.VMEM((HD, SEQ + ATT_BLK), dtype)
    seq_t = pltpu.VMEM((HD, SEQ), bf16)
    return _call(
        body, (proj, proj, proj, proj, proj, bias, o, lse, do), name="att_bwd",
        out_shape=(SDS((T, KVH * ATT_COLS), bf16), SDS((KVH, NG, ATT_BLK, 2 * ATT_BLK), f32)), grid=(KVH, B),
        in_specs=[_qkv_spec(k, lambda h, b: (b, h)) for k in range(5)]
                 + [bias_spec(), blk(), pl.BlockSpec((None, SEQ, HD), lambda h, b: (h, b, 0)), blk()],
        out_specs=(pl.BlockSpec((SEQ, ATT_COLS), lambda h, b: (b, h)), bias_spec()),
        scratch_shapes=[pltpu.VMEM((SEQ, HD), bf16), pad(bf16), pad(bf16), pltpu.VMEM((SEQ, HD), bf16),
                        seq_t, pad_t(bf16), pad_t(bf16), seq_t]
                       + [pltpu.VMEM((SEQ, HD), f32)] * 3 + [pad_t(f32)] * 2 + [pltpu.VMEM((SEQ, HD), f32)] * 4,
        semantics=("parallel", "arbitrary"), comm=comm)


MERGE_ROWS, MERGE_COLS = 1024, 256
_G_RNN_BLK = C_GATE // MERGE_COLS
_G_ATT_BLK = (C_GATE + D) // MERGE_COLS


def _merge_specs():
    cols = lambda off: pl.BlockSpec((MERGE_ROWS, MERGE_COLS), lambda i, j: (i, off + j))
    return cols(_G_RNN_BLK), cols(_G_ATT_BLK), cols(0)


def _merge_fwd(proj, pr, pa):
    def body(gr_ref, ga_ref, pr_ref, pa_ref, o_ref):
        o_ref[...] = (_sigmoid(gr_ref[...]) * pr_ref[...].astype(f32)
                      + _sigmoid(ga_ref[...]) * pa_ref[...].astype(f32)).astype(bf16)

    T = proj.shape[0]
    s_gr, s_ga, s0 = _merge_specs()
    return pl.pallas_call(body, name="merge_fwd", out_shape=SDS((T, D), bf16),
                          grid=(T // MERGE_ROWS, D // MERGE_COLS),
                          in_specs=[s_gr, s_ga, s0, s0], out_specs=s0,
                          compiler_params=_params(("parallel", "parallel")))(proj, proj, pr, pa)


def _merge_bwd(proj, pr, pa, dm):
    nj = D // MERGE_COLS

    def body(g_ref, pr_ref, pa_ref, dm_ref, dp_ref, dg_ref):
        dm_ = dm_ref[...].astype(f32)
        s = _sigmoid(g_ref[...])
        p = jnp.where(pl.program_id(1) < nj, pr_ref[...], pa_ref[...]).astype(f32)
        dp_ref[...] = (dm_ * s).astype(bf16)
        dg_ref[...] = (dm_ * p * s * (1.0 - s)).astype(bf16)

    T = proj.shape[0]
    blk = (MERGE_ROWS, MERGE_COLS)
    wrap = pl.BlockSpec(blk, lambda i, j: (i, j % nj))
    pr_spec = pl.BlockSpec(blk, lambda i, j: (i, jnp.minimum(j, nj - 1)))
    pa_spec = pl.BlockSpec(blk, lambda i, j: (i, jnp.maximum(j - nj, 0)))
    out = pl.BlockSpec(blk, lambda i, j: (i, j))
    return pl.pallas_call(
        body, name="merge_bwd", out_shape=(SDS((T, 2 * D), bf16), SDS((T, 2 * D), bf16)),
        grid=(T // MERGE_ROWS, 2 * nj),
        in_specs=[pl.BlockSpec(blk, lambda i, j: (i, _G_RNN_BLK + j)), pr_spec, pa_spec, wrap], out_specs=(out, out),
        compiler_params=_params(("parallel", "parallel")))(proj, pr, pa, dm)


FFN_COLS = 256
GELU_C = math.sqrt(2.0 / math.pi)
GELU_A = 0.044715


def _gelu_parts(x):
    t = jnp.tanh(GELU_C * (x + GELU_A * x * x * x))
    return 0.5 * x * (1.0 + t), t


def _ffn_act_fwd(gpre, up, cw, cb):
    def body(g_ref, u_ref, cw_ref, cb_ref, o_ref):
        row = lax.broadcasted_iota(jnp.int32, (SEQ, FFN_COLS), 0)
        gate = _conv_fwd(g_ref[...].astype(f32), cw_ref, cb_ref[...], row)
        o_ref[...] = (_gelu_parts(gate)[0] * u_ref[...].astype(f32)).astype(bf16)

    T = gpre.shape[0]
    blk = lambda: pl.BlockSpec((SEQ, FFN_COLS), lambda b, j: (b, j))
    return pl.pallas_call(
        body, name="ffn_act_fwd", out_shape=SDS((T, FFN_W), bf16), grid=(T // SEQ, FFN_W // FFN_COLS),
        in_specs=[blk(), blk(), pl.BlockSpec((FFN_CONV, FFN_COLS), lambda b, j: (0, j)),
                  pl.BlockSpec((1, FFN_COLS), lambda b, j: (0, j))],
        out_specs=blk(), compiler_params=_params(("parallel", "parallel")))(gpre, up, cw, cb)


def _ffn_act_bwd(gpre, up, cw, cb, dact):
    def body(g_ref, u_ref, cw_ref, cb_ref, da_ref, dg_ref, du_ref, dcw_ref, dcb_ref):
        row = lax.broadcasted_iota(jnp.int32, (SEQ, FFN_COLS), 0)
        gp = g_ref[...].astype(f32)
        gate = _conv_fwd(gp, cw_ref, cb_ref[...], row)
        gel, t = _gelu_parts(gate)
        da = da_ref[...].astype(f32)
        du_ref[...] = (da * gel).astype(bf16)
        dgel = 0.5 * (1.0 + t) + 0.5 * gate * (1.0 - t * t) * (GELU_C * (1.0 + 3.0 * GELU_A * gate * gate))
        dgate = da * u_ref[...].astype(f32) * dgel
        dx, dws, db = _conv_bwd(gp, cw_ref, dgate, row)
        dg_ref[...] = dx.astype(bf16)
        first = pl.program_id(1) == 0

        def acc(ref, val):
            @pl.when(first)
            def _():
                ref[...] = val

            @pl.when(jnp.logical_not(first))
            def _():
                ref[...] += val

        for k in range(FFN_CONV):
            acc(dcw_ref.at[k:k + 1, :], dws[k])
        acc(dcb_ref, db)

    T = gpre.shape[0]
    blk = lambda: pl.BlockSpec((SEQ, FFN_COLS), lambda j, b: (b, j))
    cws = lambda: pl.BlockSpec((FFN_CONV, FFN_COLS), lambda j, b: (0, j))
    cbs = lambda: pl.BlockSpec((1, FFN_COLS), lambda j, b: (0, j))
    return pl.pallas_call(
        body, name="ffn_act_bwd",
        out_shape=(SDS((T, FFN_W), bf16), SDS((T, FFN_W), bf16), SDS((FFN_CONV, FFN_W), f32), SDS((1, FFN_W), f32)),
        grid=(FFN_W // FFN_COLS, T // SEQ),
        in_specs=[blk(), blk(), cws(), cbs(), blk()], out_specs=(blk(), blk(), cws(), cbs()),
        compiler_params=_params(("parallel", "arbitrary")))(gpre, up, cw, cb, dact)


def _coords():
    return lax.axis_index("x"), lax.axis_index("y"), lax.axis_index("c")


def _dev_index(dev):
    return 4 * dev[0] + 2 * dev[1] + dev[2]


def _dma_sems(n):
    return [pltpu.SemaphoreType.DMA((n,)), pltpu.SemaphoreType.DMA((n,))]


def _gather_two_level(arrays):
    n = len(arrays)

    def plan(ins, outs, sems):
        send_sems, recv_sems, local_sems = sems
        x, y, c = _coords()
        me, sibling = (x, y, c), (x, y, 1 - c)
        chips = [(1 - x, y), (x, 1 - y), (1 - x, 1 - y)]

        def copy(a, k, block, to, own=False):
            dst = outs[a].at[_dev_index(block)]
            return pltpu.make_async_remote_copy(
                src_ref=ins[a] if own else dst, dst_ref=dst, send_sem=send_sems.at[7 * a + k],
                recv_sem=recv_sems.at[7 * a + k], device_id=to, device_id_type=MESH)

        mine = [pltpu.make_async_copy(ins[a], outs[a].at[_dev_index(me)], local_sems.at[a]) for a in range(n)]
        first = [copy(a, 0, me, sibling, own=True) for a in range(n)]
        first += [copy(a, 1 + j, me, (*chip, c), own=True) for a in range(n) for j, chip in enumerate(chips)]
        passed = [[copy(a, 4 + j, (*chip, c), sibling) for a in range(n)] for j, chip in enumerate(chips)]
        arrive_ici = [[copy(a, 1 + j, (*chip, c), me) for a in range(n)] for j, chip in enumerate(chips)]
        arrive_d2d = [copy(a, 0, sibling, me) for a in range(n)]
        arrive_d2d += [copy(a, 4 + j, (*chip, 1 - c), me) for a in range(n) for j, chip in enumerate(chips)]
        return mine, first, passed, arrive_ici, arrive_d2d

    def start(ins, outs, sems):
        mine, first, _, _, _ = plan(ins, outs, sems)
        for cp in mine + first:
            cp.start()

    def finish(ins, outs, sems):
        mine, first, passed, arrive_ici, arrive_d2d = plan(ins, outs, sems)
        for j in range(3):
            for cp in arrive_ici[j]:
                cp.wait_recv()
            for cp in passed[j]:
                cp.start()
        for cp in arrive_d2d:
            cp.wait_recv()
        for cp in first + [cp for group in passed for cp in group]:
            cp.wait_send()
        for cp in mine:
            cp.wait()

    return _Comm(arrays, [SDS((N_DEV,) + a.shape, a.dtype) for a in arrays],
                 _dma_sems(7 * n) + [pltpu.SemaphoreType.DMA((n,))], start, finish)


def _gather_direct(arrays):
    n = len(arrays)

    def plan(ins, outs, sems):
        send_sems, recv_sems, local_sems = sems
        x, y, c = _coords()
        me = (x, y, c)
        mine = [pltpu.make_async_copy(ins[a], outs[a].at[_dev_index(me)], local_sems.at[a]) for a in range(n)]
        sends, arrivals = [], []
        for a in range(n):
            for k in range(1, N_DEV):
                peer = (1 - x if k & 4 else x, 1 - y if k & 2 else y, 1 - c if k & 1 else c)
                s = 7 * a + k - 1
                for slot, out in ((me, sends), (peer, arrivals)):
                    out.append(pltpu.make_async_remote_copy(
                        src_ref=ins[a], dst_ref=outs[a].at[_dev_index(slot)], send_sem=send_sems.at[s],
                        recv_sem=recv_sems.at[s], device_id=peer, device_id_type=MESH))
        return mine, sends, arrivals

    def start(ins, outs, sems):
        mine, sends, _ = plan(ins, outs, sems)
        for cp in mine + sends:
            cp.start()

    def finish(ins, outs, sems):
        mine, sends, arrivals = plan(ins, outs, sems)
        for cp in arrivals:
            cp.wait_recv()
        for cp in sends:
            cp.wait_send()
        for cp in mine:
            cp.wait()

    return _Comm(arrays, [SDS((N_DEV,) + a.shape, a.dtype) for a in arrays],
                 _dma_sems(7 * n) + [pltpu.SemaphoreType.DMA((n,))], start, finish)


def _exchange(arrays, n_blocks, route):
    n = len(arrays)

    def plan(ins, outs, sems):
        send_sems, recv_sems = sems
        cps = []
        for a in range(n):
            for j, (src, peer) in enumerate(route(*_coords())):
                cps.append(pltpu.make_async_remote_copy(
                    src_ref=ins[a].at[src], dst_ref=outs[a].at[j], send_sem=send_sems.at[n_blocks * a + j],
                    recv_sem=recv_sems.at[n_blocks * a + j], device_id=peer, device_id_type=MESH))
        return cps

    def start(ins, outs, sems):
        for cp in plan(ins, outs, sems):
            cp.start()

    def finish(ins, outs, sems):
        for cp in plan(ins, outs, sems):
            cp.wait()

    return _Comm(arrays, [SDS((n_blocks,) + a.shape[1:], a.dtype) for a in arrays], _dma_sems(n_blocks * n),
                 start, finish)


def _sibling_exchange(arrays):
    return _exchange(arrays, 4, lambda x, y, c: [(2 * k + 1 - c, (x, y, 1 - c)) for k in range(4)])


def _chip_exchange(arrays):
    return _exchange(arrays, 3, lambda x, y, c: [(2 * cx + cy, (cx, cy, c))
                                                 for cx, cy in ((1 - x, y), (x, 1 - y), (1 - x, 1 - y))])


def _run(comm, name):
    def body(*refs):
        k_in, k_out = len(comm.inputs), len(comm.out_shapes)
        ins, outs, sems = refs[:k_in], refs[k_in:k_in + k_out], refs[k_in + k_out:]
        comm.start(ins, outs, sems)
        comm.finish(ins, outs, sems)

    return pl.pallas_call(body, name=name, out_shape=comm.out_shapes, in_specs=[ANY] * len(comm.inputs),
                          out_specs=(ANY,) * len(comm.out_shapes), scratch_shapes=comm.sem_shapes)(*comm.inputs)


TILE_ELEMS = 192 * 1024


def _row_tile(R, C):
    if R * C <= TILE_ELEMS:
        return R
    return max(t for t in range(SUBLANES, R, SUBLANES) if R % t == 0 and t * C <= TILE_ELEMS)


def _pair_sum(g8, recv, c_idx, name):
    R, C = g8.shape[-2:]
    t = _row_tile(R, C)

    def body(c_ref, g_ref, r_ref, o_ref):
        o_ref[...] = (g_ref[...] + r_ref[...]).astype(bf16)

    return pl.pallas_call(
        body, name="pair_sum_" + name, out_shape=SDS((4, R, C), bf16),
        grid_spec=pltpu.PrefetchScalarGridSpec(
            num_scalar_prefetch=1, grid=(4, R // t),
            in_specs=[pl.BlockSpec((None, t, C), lambda k, i, c: (2 * k + c[0], i, 0)),
                      pl.BlockSpec((None, t, C), lambda k, i, c: (k, i, 0))],
            out_specs=pl.BlockSpec((None, t, C), lambda k, i, c: (k, i, 0))),
        compiler_params=_params(("parallel", "parallel")))(c_idx, g8, recv)


def _adamw_math(w, g, m, v):
    m = ADAM_B1 * m + (1.0 - ADAM_B1) * g
    v = ADAM_B2 * v + (1.0 - ADAM_B2) * (g * g)
    m_hat = m / (1.0 - ADAM_B1 ** ADAM_STEP)
    v_hat = v / (1.0 - ADAM_B2 ** ADAM_STEP)
    delta = -ADAM_LR * (m_hat / (jnp.sqrt(v_hat) + ADAM_EPS) + ADAM_WD * w)
    return delta, m, v


def _adamw_sharded(pa, recv, k_idx, w, m, v, name):
    R, C = w.shape
    t = _row_tile(R, C)

    def body(k_ref, p_ref, r_ref, w_ref, m_ref, v_ref, g_ref, d_ref, nm_ref, nv_ref):
        g = p_ref[...].astype(f32)
        for j in range(3):
            g = g + r_ref[j].astype(f32)
        d, nm, nv = _adamw_math(w_ref[...], g, m_ref[...], v_ref[...])
        g_ref[...], d_ref[...], nm_ref[...], nv_ref[...] = g, d, nm, nv

    tile = lambda: pl.BlockSpec((t, C), lambda i, k: (i, 0))
    return pl.pallas_call(
        body, name="adamw_" + name, out_shape=(SDS((R, C), f32),) * 4,
        grid_spec=pltpu.PrefetchScalarGridSpec(
            num_scalar_prefetch=1, grid=(R // t,),
            in_specs=[pl.BlockSpec((None, t, C), lambda i, k: (k[0], i, 0)),
                      pl.BlockSpec((3, t, C), lambda i, k: (0, i, 0)), tile(), tile(), tile()],
            out_specs=(tile(), tile(), tile(), tile())),
        compiler_params=_params(("parallel",)))(k_idx, pa, recv, w, m, v)


def _adamw_replicated(parts, ws, ms, vs):
    n = len(ws)

    def body(*refs):
        p, w, m, v = (refs[i * n:(i + 1) * n] for i in range(4))
        outs = refs[4 * n:]
        for a in range(n):
            g = p[a][0].astype(f32)
            for j in range(1, N_DEV):
                g = g + p[a][j].astype(f32)
            d, nm, nv = _adamw_math(w[a][...], g, m[a][...], v[a][...])
            for i, val in enumerate((g, d, nm, nv)):
                outs[i * n + a][...] = val

    shapes = tuple(SDS(w.shape, f32) for w in ws)
    res = pl.pallas_call(body, name="adamw_replicated", out_shape=shapes * 4,
                         compiler_params=_params())(*parts, *ws, *ms, *vs)
    return [res[i * n:(i + 1) * n] for i in range(4)]


def _cols_to_full(g):
    n, r, c = g.shape
    return g.transpose(1, 0, 2).reshape(r, n * c)


def _full_to_cols(a):
    r, c = a.shape
    return a.reshape(r, N_DEV, c // N_DEV).transpose(1, 0, 2)


def _rows_blocked(a):
    r, c = a.shape
    return a.reshape(N_DEV, r // N_DEV, c)


def _w_in_to_internal(w):
    K = w.shape[0]
    q = w[:, 1280:2816].reshape(K, NG, KVH, 1, HD).transpose(0, 2, 1, 3, 4).reshape(K, KVH, NG, HD)
    k = w[:, 2816:3328].reshape(K, KVH, 1, HD)
    v = w[:, 3328:3840].reshape(K, KVH, 1, HD)
    att = jnp.concatenate([q, k, v], axis=2).reshape(K, KVH * ATT_COLS)
    return jnp.concatenate([w[:, :1280], att, w[:, 3840:]], axis=1)


def _w_in_from_internal(w):
    K = w.shape[0]
    att = w[:, C_ATT:C_GATE].reshape(K, KVH, 5, HD)
    q = att[:, :, 0:3].transpose(0, 2, 1, 3).reshape(K, NG * KVH * HD)
    k = att[:, :, 3].reshape(K, KVH * HD)
    v = att[:, :, 4].reshape(K, KVH * HD)
    return jnp.concatenate([w[:, :C_ATT], q, k, v, w[:, C_GATE:]], axis=1)


_IN_NAMES = ('x', 'rel_bias', 'norm_mix_pre', 'norm_mix_post', 'w_in', 'conv_rnn_w', 'conv_rnn_b', 'w_rg_a', 'b_rg_a',
             'w_rg_x', 'b_rg_x', 'lru_lambda', 'w_branch_rnn', 'w_branch_att', 'w_out', 'norm_ffn_pre',
             'norm_ffn_post', 'w_ffn_gate', 'w_ffn_up', 'conv_ffn_w', 'conv_ffn_b', 'w_ffn_down')
_WEIGHTS = _IN_NAMES[1:]
_SHARDED = {"w_in": "col", "conv_rnn_w": "col", "w_branch_rnn": "row", "w_branch_att": "col", "w_out": "row",
            "w_ffn_gate": "col", "w_ffn_up": "col", "conv_ffn_w": "col", "w_ffn_down": "row"}
_REPLICATED = tuple(n for n in _WEIGHTS if n not in _SHARDED)
_EARLY = ("w_branch_rnn", "w_branch_att", "w_out", "w_ffn_gate", "w_ffn_up", "conv_ffn_w", "w_ffn_down")
_LATE = ("w_in", "conv_rnn_w")


def _flat2(a):
    return a.reshape(-1, a.shape[-1])


def _train_step(inp):
    x_idx, y_idx, c_idx = _coords()
    W = {n: inp[n] for n in _WEIGHTS}
    x = inp["x"].reshape(-1, D)
    target = inp["loss_target"].reshape(-1, D)
    shard = {n: inp[n][0] for n in _SHARDED}

    g_in, g_cr, g_cf = _run(_gather_two_level([shard["w_in"].astype(bf16), shard["conv_rnn_w"],
                                               shard["conv_ffn_w"]]), "ag_w_in")
    w_in = _w_in_to_internal(_cols_to_full(g_in))
    cw_rnn, cw_ffn = _cols_to_full(g_cr), _cols_to_full(g_cf)
    behind_rnn = ("w_branch_rnn", "w_branch_att", "w_out", "w_ffn_down")
    behind_att = ("w_ffn_gate", "w_ffn_up")

    wa, wx = W["w_rg_a"][0], W["w_rg_x"][0]
    buckets = jnp.asarray(_bucket_maps())

    hn = _norm_in(x, W["norm_mix_pre"])
    proj = _mm(hn, w_in, "nn", f32, "mm_proj", 512, IN_W // 2, 1024, cols_outer=True)
    h_rnn, got = _rnn_fwd(proj, cw_rnn, W["conv_rnn_b"], wa, W["b_rg_a"], wx, W["b_rg_x"], W["lru_lambda"],
                          comm=_gather_direct([shard[n].astype(bf16) for n in behind_rnn]))
    gathered = dict(zip(behind_rnn, got))
    bias = _bias_tables(W["rel_bias"], buckets)
    (o_att, lse), got = _att_fwd(proj, bias, comm=_gather_direct([shard[n].astype(bf16) for n in behind_att]))
    gathered.update(zip(behind_att, got))
    w_brnn = gathered["w_branch_rnn"].reshape(RNN_W, D)
    w_batt = _cols_to_full(gathered["w_branch_att"])
    w_out = gathered["w_out"].reshape(D, D)
    w_gate, w_up = _cols_to_full(gathered["w_ffn_gate"]), _cols_to_full(gathered["w_ffn_up"])
    w_down = gathered["w_ffn_down"].reshape(FFN_W, D)
    pr = _mm(h_rnn, w_brnn, "nn", bf16, "mm_pr", 1024, 1024, 1280)
    pa = _mm(o_att, w_batt, "nn", bf16, "mm_pa", 1024, 1024, 512)
    merged = _merge_fwd(proj, pr, pa)
    mix = _mm(merged, w_out, "nn", f32, "mm_mix", 1024, 1024, 1024)
    h1, hn2 = _mid_fwd(x, mix, W["norm_mix_post"], W["norm_ffn_pre"])
    gpre = _mm(hn2, w_gate, "nn", bf16, "mm_gate", 1024, 1024, 1024, cols_outer=True)
    up = _mm(hn2, w_up, "nn", bf16, "mm_up", 1024, 1024, 1024, cols_outer=True)
    act = _ffn_act_fwd(gpre, up, cw_ffn, W["conv_ffn_b"])
    ff = _mm(act, w_down, "nn", f32, "mm_down", 1024, 1024, 1024)
    loss_part, dy, dff, dg_fpost = _final(h1, ff, W["norm_ffn_post"], target)

    grads = {}
    dact = _mm(dff, w_down, "nt", bf16, "mm_dact", 1024, 1024, 1024, cols_outer=True)
    grads["w_ffn_down"] = _rows_blocked(_mm(act, dff, "tn", f32, "mm_dw_down", 1024, 1024, 2048))
    dgpre, dup, dcw_ffn, dcb_ffn = _ffn_act_bwd(gpre, up, cw_ffn, W["conv_ffn_b"], dact)
    grads["conv_ffn_w"] = _full_to_cols(dcw_ffn)
    grads["w_ffn_gate"] = _full_to_cols(_mm(hn2, dgpre, "tn", f32, "mm_dw_gate", 1024, 1024, 2048))
    grads["w_ffn_up"] = _full_to_cols(_mm(hn2, dup, "tn", f32, "mm_dw_up", 1024, 1024, 2048))
    dhn2 = _mm_nt_sum([(dgpre, w_gate, 1024), (dup, w_up, 1024)], "mm_dhn2", 1024, 1024)
    dh1, dmix, dg_fpre, dg_post = _mid_bwd(dy, dhn2, h1, W["norm_ffn_pre"], mix, W["norm_mix_post"])
    dmerged = _mm(dmix, w_out, "nt", bf16, "mm_dmerged", 1024, 1024, 1024)
    grads["w_out"] = _rows_blocked(_mm(merged, dmix, "tn", f32, "mm_dw_out", 1024, 1024, 2048))
    dprpa, dgates = _merge_bwd(proj, pr, pa, dmerged)
    dpr, dpa = dprpa[:, :D], dprpa[:, D:]
    dh_rnn = _mm(dpr, w_brnn, "nt", bf16, "mm_dh_rnn", 1024, 1280, 1024)
    grads["w_branch_rnn"] = _rows_blocked(_mm(h_rnn, dpr, "tn", f32, "mm_dw_brnn", 1280, 1024, 1024))
    do_att = _mm(dpa, w_batt, "nt", f32, "mm_do_att", 1024, 512, 1024)
    grads["w_branch_att"] = _full_to_cols(_mm(o_att, dpa, "tn", f32, "mm_dw_batt", 512, 1024, 2048))

    c_arr = jnp.reshape(c_idx, (1,)).astype(jnp.int32)
    k_arr = jnp.reshape(2 * x_idx + y_idx, (1,)).astype(jnp.int32)
    (dqkv, dbias), from_sibling = _att_bwd(proj, bias, o_att, lse, do_att,
                                           comm=_sibling_exchange([grads[n] for n in _EARLY]))
    pair = {n: _pair_sum(grads[n], r, c_arr, n) for n, r in zip(_EARLY, from_sibling)}
    drel = _bias_grad(dbias, buckets)
    (dxr, dcw_rnn, dcb_rnn, dwa, dba, dwx, dbx, dlam), from_chips = _rnn_bwd(
        proj, h_rnn, dh_rnn, cw_rnn, W["conv_rnn_b"], wa, W["b_rg_a"], wx, W["b_rg_x"], W["lru_lambda"],
        comm=_chip_exchange([pair[n] for n in _EARLY]))
    from_chips = dict(zip(_EARLY, from_chips))
    gsmall = {"rel_bias": drel, "norm_mix_post": dg_post, "conv_rnn_b": dcb_rnn, "w_rg_a": dwa.astype(bf16),
              "b_rg_a": dba, "w_rg_x": dwx.astype(bf16), "b_rg_x": dbx, "lru_lambda": dlam,
              "norm_ffn_pre": dg_fpre, "norm_ffn_post": dg_fpost, "conv_ffn_b": dcb_ffn}
    dw_in_a, parts = _mm(hn, dqkv, "tn", f32, "mm_dw_in_a", 1024, 1280, 1024,
                         comm=_gather_direct([_flat2(gsmall[n]) for n in gsmall]))
    parts = dict(zip(gsmall, parts))
    dw_in = jnp.concatenate([_mm(hn, dxr, "tn", f32, "mm_dw_in_r", 1024, 1280, 1024), dw_in_a,
                             _mm(hn, dgates, "tn", f32, "mm_dw_in_g", 1024, 1024, 2048)], axis=1)
    grads["w_in"] = _full_to_cols(_w_in_from_internal(dw_in))
    grads["conv_rnn_w"] = _full_to_cols(dcw_rnn)
    dhn, from_sibling = _mm(dxr, w_in[:, :C_ATT], "nt", f32, "mm_dhn_r", 1024, 1024, 1280,
                            comm=_sibling_exchange([grads[n] for n in _LATE]))
    pair.update({n: _pair_sum(grads[n], r, c_arr, n) for n, r in zip(_LATE, from_sibling)})
    dhn, late = _mm_nt_sum([(dqkv, w_in[:, C_ATT:C_GATE], 1280), (dgates, w_in[:, C_GATE:], 1024)], "mm_dhn_ag",
                           1024, 1024, add=dhn, comm=_chip_exchange([pair[n] for n in _LATE]))
    from_chips.update(zip(_LATE, late))
    dx, dg_pre = _in_bwd(dh1, dhn, x, W["norm_mix_pre"])
    parts["norm_mix_pre"], = _run(_gather_two_level([dg_pre]), "ag_norm_mix_pre")
    parts = [parts[n] for n in _REPLICATED]

    out = {}
    for n in _SHARDED:
        res = _adamw_sharded(pair[n], from_chips[n], k_arr, shard[n], inp["m_" + n][0], inp["v_" + n][0], n)
        out[n] = [r[None] for r in res]
    small = _adamw_replicated(parts, *[[_flat2(inp[p + n]) for n in _REPLICATED] for p in ("", "m_", "v_")])
    for a, n in enumerate(_REPLICATED):
        out[n] = [small[i][a].reshape(inp[n].shape) for i in range(4)]

    loss = lax.psum(loss_part[0, 0], ("x", "y", "c"))
    outs = [loss, dx.reshape(inp["x"].shape)]
    for i in range(4):
        outs.extend(out[n][i] for n in _WEIGHTS)
    return tuple(outs)


def kernel(x, rel_bias, norm_mix_pre, norm_mix_post, w_in, conv_rnn_w, conv_rnn_b, w_rg_a, b_rg_a, w_rg_x, b_rg_x, lru_lambda, w_branch_rnn, w_branch_att, w_out, norm_ffn_pre, norm_ffn_post, w_ffn_gate, w_ffn_up, conv_ffn_w, conv_ffn_b, w_ffn_down, loss_target, m_rel_bias, m_norm_mix_pre, m_norm_mix_post, m_w_in, m_conv_rnn_w, m_conv_rnn_b, m_w_rg_a, m_b_rg_a, m_w_rg_x, m_b_rg_x, m_lru_lambda, m_w_branch_rnn, m_w_branch_att, m_w_out, m_norm_ffn_pre, m_norm_ffn_post, m_w_ffn_gate, m_w_ffn_up, m_conv_ffn_w, m_conv_ffn_b, m_w_ffn_down, v_rel_bias, v_norm_mix_pre, v_norm_mix_post, v_w_in, v_conv_rnn_w, v_conv_rnn_b, v_w_rg_a, v_b_rg_a, v_w_rg_x, v_b_rg_x, v_lru_lambda, v_w_branch_rnn, v_w_branch_att, v_w_out, v_norm_ffn_pre, v_norm_ffn_post, v_w_ffn_gate, v_w_ffn_up, v_conv_ffn_w, v_conv_ffn_b, v_w_ffn_down):
    vals = locals()
    names = list(_IN_NAMES) + ["loss_target"] + ["m_" + n for n in _WEIGHTS] + ["v_" + n for n in _WEIGHTS]
    return _train_step({n: vals[n] for n in names})
```

```python
import functools
import math

import numpy as np
import jax
import jax.numpy as jnp
from jax import lax
from jax.experimental import pallas as pl
from jax.experimental.pallas import tpu as pltpu

f32, bf16 = jnp.float32, jnp.bfloat16
SDS = jax.ShapeDtypeStruct
MESH = pl.DeviceIdType.MESH
ANY = pl.BlockSpec(memory_space=pl.ANY)

D = 1024
SEQ = 2048
RNN_W = 1280
RNN_BLOCKS = 10
LANES = 128
SUBLANES = 8
RNN_CONV = 4
LRU_C = 8.0
HD = 128
KVH = 4
DILATIONS = (1, 4, 16)
NG = 3
ATT_BLK = 128
NBLK_SEQ = SEQ // ATT_BLK
ATT_UNROLL = 8
REL_BUCKETS = 32
REL_MAX_DIST = 2048
FFN_W = 3072
FFN_CONV = 3
EPS = 1e-6
IN_W = 5888
ATT_COLS = 5 * HD
C_ATT = RNN_W
C_GATE = RNN_W + KVH * ATT_COLS
NEG = -1e30

ADAM_LR, ADAM_B1, ADAM_B2, ADAM_EPS, ADAM_WD, ADAM_STEP = 0.001, 0.9, 0.999, 1e-08, 0.01, 10

VMEM_LIMIT_BYTES = 56 * 1024 * 1024
N_DEV = 8


def _params(sem=None):
    return pltpu.CompilerParams(dimension_semantics=sem, vmem_limit_bytes=VMEM_LIMIT_BYTES)


def _sigmoid(x):
    return 1.0 / (1.0 + jnp.exp(-x))


class _Comm:
    def __init__(self, inputs, out_shapes, sem_shapes, start, finish):
        self.inputs, self.out_shapes, self.sem_shapes = tuple(inputs), tuple(out_shapes), list(sem_shapes)
        self.start, self.finish = start, finish


def _call(body, args, *, name, grid, in_specs, out_specs, out_shape, scratch_shapes=(), semantics, comm=None):
    if comm is None:
        return pl.pallas_call(body, name=name, grid=grid, in_specs=list(in_specs), out_specs=tuple(out_specs),
                              out_shape=tuple(out_shape), scratch_shapes=list(scratch_shapes),
                              compiler_params=_params(semantics))(*args), ()
    n_in, n_out, n_scr = len(in_specs), len(out_shape), len(scratch_shapes)
    c_in, c_out = len(comm.inputs), len(comm.out_shapes)

    def fused(*refs):
        ins, refs = refs[:n_in], refs[n_in:]
        cin, refs = refs[:c_in], refs[c_in:]
        outs, refs = refs[:n_out], refs[n_out:]
        cout, refs = refs[:c_out], refs[c_out:]
        scr, csem = refs[:n_scr], refs[n_scr:]
        first = functools.reduce(jnp.logical_and, [pl.program_id(d) == 0 for d in range(len(grid))])
        last = functools.reduce(jnp.logical_and, [pl.program_id(d) == grid[d] - 1 for d in range(len(grid))])

        @pl.when(first)
        def _():
            comm.start(cin, cout, csem)

        body(*ins, *outs, *scr)

        @pl.when(last)
        def _():
            comm.finish(cin, cout, csem)

    res = pl.pallas_call(
        fused, name=name, grid=grid, in_specs=list(in_specs) + [ANY] * c_in,
        out_specs=tuple(out_specs) + (ANY,) * c_out, out_shape=tuple(out_shape) + comm.out_shapes,
        scratch_shapes=list(scratch_shapes) + comm.sem_shapes,
        compiler_params=_params(("arbitrary",) * len(grid)))(*args, *comm.inputs)
    return res[:n_out], res[n_out:]


_DIMS = {"nn": (((1,), (0,)), ((), ())), "nt": (((1,), (1,)), ((), ())), "tn": (((0,), (0,)), ((), ()))}


def _mm(a, b, mode, out_dtype, name, tm, tn, tk, cols_outer=False, comm=None):
    if mode == "nn":
        (M, K), (K2, N) = a.shape, b.shape
    elif mode == "nt":
        (M, K), (N, K2) = a.shape, b.shape
    else:
        (K, M), (K2, N) = a.shape, b.shape
    assert K == K2 and M % tm == 0 and N % tn == 0 and K % tk == 0, (name, a.shape, b.shape)
    nk = K // tk

    def body(a_ref, b_ref, o_ref, *scratch):
        part = lax.dot_general(a_ref[...].astype(bf16), b_ref[...].astype(bf16), _DIMS[mode],
                               preferred_element_type=f32)
        if nk == 1:
            o_ref[...] = part.astype(o_ref.dtype)
        else:
            acc_ref, = scratch
            k = pl.program_id(2)

            @pl.when(k == 0)
            def _():
                acc_ref[...] = part

            @pl.when(k > 0)
            def _():
                acc_ref[...] += part

            @pl.when(k == nk - 1)
            def _():
                o_ref[...] = acc_ref[...].astype(o_ref.dtype)

    def ij(f):
        return (lambda j, i, k: f(i, j, k)) if cols_outer else f

    if mode == "tn":
        a_spec = pl.BlockSpec((tk, tm), ij(lambda i, j, k: (k, i)))
    else:
        a_spec = pl.BlockSpec((tm, tk), ij(lambda i, j, k: (i, k)))
    if mode == "nt":
        b_spec = pl.BlockSpec((tn, tk), ij(lambda i, j, k: (j, k)))
    else:
        b_spec = pl.BlockSpec((tk, tn), ij(lambda i, j, k: (k, j)))
    o_spec = pl.BlockSpec((tm, tn), ij(lambda i, j, k: (i, j)))
    grid = (N // tn, M // tm, nk) if cols_outer else (M // tm, N // tn, nk)
    (out,), extra = _call(
        body, (a, b), name=name, out_shape=(SDS((M, N), out_dtype),), grid=grid, in_specs=[a_spec, b_spec],
        out_specs=(o_spec,), scratch_shapes=[pltpu.VMEM((tm, tn), f32)] if nk > 1 else [],
        semantics=("parallel", "parallel", "arbitrary"), comm=comm)
    return out if comm is None else (out, extra)


def _mm_nt_sum(pairs, name, tm, tn, comm=None):
    M, N = pairs[0][0].shape[0], pairs[0][1].shape[0]
    nks = [a.shape[1] // tk for a, _, tk in pairs]
    starts = [sum(nks[:p]) for p in range(len(pairs))]
    nk = sum(nks)

    def body(*refs):
        o_ref, acc_ref = refs[-2], refs[-1]
        k = pl.program_id(2)
        for p in range(len(pairs)):
            def product(p=p):
                return lax.dot_general(refs[2 * p][...], refs[2 * p + 1][...], _DIMS["nt"], preferred_element_type=f32)

            if p == 0:
                @pl.when(k == 0)
                def _():
                    acc_ref[...] = product()

            @pl.when((k >= max(starts[p], 1)) & (k < starts[p] + nks[p]))
            def _():
                acc_ref[...] += product()

        @pl.when(k == nk - 1)
        def _():
            o_ref[...] = acc_ref[...]

    in_specs, args = [], []
    for (a, b, tk), k0, n in zip(pairs, starts, nks):
        assert a.shape[1] == b.shape[1] and a.shape[1] % tk == 0 and a.dtype == b.dtype == bf16, name
        chunk = lambda k, k0=k0, n=n: jnp.clip(k - k0, 0, n - 1)
        in_specs += [pl.BlockSpec((tm, tk), lambda i, j, k, c=chunk: (i, c(k))),
                     pl.BlockSpec((tn, tk), lambda i, j, k, c=chunk: (j, c(k)))]
        args += [a, b]
    o_spec = pl.BlockSpec((tm, tn), lambda i, j, k: (i, j))
    (out,), extra = _call(
        body, args, name=name, out_shape=(SDS((M, N), f32),), grid=(M // tm, N // tn, nk), in_specs=in_specs,
        out_specs=(o_spec,), scratch_shapes=[pltpu.VMEM((tm, tn), f32)],
        semantics=("parallel", "parallel", "arbitrary"), comm=comm)
    return out if comm is None else (out, extra)


ROW_TILE = 512


def _rms_fwd(x, g):
    r = lax.rsqrt(jnp.mean(x * x, axis=-1, keepdims=True) + EPS)
    return x * r * g


def _rms_bwd(x, g, dy):
    r = lax.rsqrt(jnp.mean(x * x, axis=-1, keepdims=True) + EPS)
    xh = x * r
    dxh = dy * g
    dx = r * (dxh - xh * jnp.mean(dxh * xh, axis=-1, keepdims=True))
    return dx, jnp.sum(dy * xh, axis=0, keepdims=True)


def _acc_out(ref, val):
    @pl.when(pl.program_id(0) == 0)
    def _():
        ref[...] = val

    @pl.when(pl.program_id(0) > 0)
    def _():
        ref[...] += val


def _row_spec(width=D):
    return pl.BlockSpec((ROW_TILE, width), lambda i: (i, 0))


def _vec_spec(width=D):
    return pl.BlockSpec((1, width), lambda i: (0, 0))


def _norm_in(x, g, comm=None):
    def body(x_ref, g_ref, o_ref):
        o_ref[...] = _rms_fwd(x_ref[...], g_ref[...]).astype(bf16)

    T = x.shape[0]
    (hn,), extra = _call(body, (x, g), name="norm_in", out_shape=(SDS((T, D), bf16),), grid=(T // ROW_TILE,),
                         in_specs=[_row_spec(), _vec_spec()], out_specs=(_row_spec(),), semantics=("parallel",),
                         comm=comm)
    return hn, extra


def _mid_fwd(x, mix, g_post, g_fpre):
    def body(x_ref, mix_ref, gp_ref, gf_ref, h1_ref, hn2_ref):
        h1 = x_ref[...] + _rms_fwd(mix_ref[...], gp_ref[...])
        h1_ref[...] = h1
        hn2_ref[...] = _rms_fwd(h1, gf_ref[...]).astype(bf16)

    T = x.shape[0]
    return pl.pallas_call(body, name="mid_fwd", out_shape=(SDS((T, D), f32), SDS((T, D), bf16)),
                          grid=(T // ROW_TILE,),
                          in_specs=[_row_spec(), _row_spec(), _vec_spec(), _vec_spec()],
                          out_specs=(_row_spec(), _row_spec()),
                          compiler_params=_params(("parallel",)))(x, mix, g_post, g_fpre)


def _final(h1, ff, g_fpost, target):
    def body(h1_ref, ff_ref, g_ref, t_ref, loss_ref, dy_ref, dff_ref, dg_ref):
        ff = ff_ref[...]
        g = g_ref[...]
        e = h1_ref[...] + _rms_fwd(ff, g) - t_ref[...]
        part = jnp.sum(jnp.sum(e * e, axis=1, keepdims=True), axis=0, keepdims=True) * (0.5 / D)
        dy = e * (1.0 / D)
        dy_ref[...] = dy
        dff, dg = _rms_bwd(ff, g, dy)
        dff_ref[...] = dff.astype(bf16)
        _acc_out(loss_ref, part)
        _acc_out(dg_ref, dg)

    T = h1.shape[0]
    return pl.pallas_call(
        body, name="final", out_shape=(SDS((1, 1), f32), SDS((T, D), f32), SDS((T, D), bf16), SDS((1, D), f32)),
        grid=(T // ROW_TILE,),
        in_specs=[_row_spec(), _row_spec(), _vec_spec(), _row_spec()],
        out_specs=(pl.BlockSpec((1, 1), lambda i: (0, 0)), _row_spec(), _row_spec(), _vec_spec()),
        compiler_params=_params(("arbitrary",)))(h1, ff, g_fpost, target)


def _mid_bwd(dy, dhn2, h1, g_fpre, mix, g_post):
    def body(dy_ref, dhn2_ref, h1_ref, gf_ref, mix_ref, gp_ref, dh1_ref, dmix_ref, dgf_ref, dgp_ref):
        d1, dgf = _rms_bwd(h1_ref[...], gf_ref[...], dhn2_ref[...])
        dh1 = dy_ref[...] + d1
        dh1_ref[...] = dh1
        dmix, dgp = _rms_bwd(mix_ref[...], gp_ref[...], dh1)
        dmix_ref[...] = dmix.astype(bf16)
        _acc_out(dgf_ref, dgf)
        _acc_out(dgp_ref, dgp)

    T = dy.shape[0]
    return pl.pallas_call(
        body, name="mid_bwd", out_shape=(SDS((T, D), f32), SDS((T, D), bf16), SDS((1, D), f32), SDS((1, D), f32)),
        grid=(T // ROW_TILE,),
        in_specs=[_row_spec(), _row_spec(), _row_spec(), _vec_spec(), _row_spec(), _vec_spec()],
        out_specs=(_row_spec(), _row_spec(), _vec_spec(), _vec_spec()),
        compiler_params=_params(("arbitrary",)))(dy, dhn2, h1, g_fpre, mix, g_post)


def _in_bwd(dh1, dhn, x, g_pre):
    def body(dh1_ref, dhn_ref, x_ref, g_ref, dx_ref, dg_ref):
        d, dg = _rms_bwd(x_ref[...], g_ref[...], dhn_ref[...])
        dx_ref[...] = dh1_ref[...] + d
        _acc_out(dg_ref, dg)

    T = x.shape[0]
    return pl.pallas_call(
        body, name="in_bwd", out_shape=(SDS((T, D), f32), SDS((1, D), f32)), grid=(T // ROW_TILE,),
        in_specs=[_row_spec(), _row_spec(), _row_spec(), _vec_spec()], out_specs=(_row_spec(), _vec_spec()),
        compiler_params=_params(("arbitrary",)))(dh1, dhn, x, g_pre)


def _shift_dn(x, d, row, fill=0.0):
    if d == 0:
        return x
    y = pltpu.roll(x, d, 0)
    head = jnp.where(row[:SUBLANES] >= d, y[:SUBLANES], fill)
    return jnp.concatenate([head, y[SUBLANES:]], axis=0)


def _shift_up(x, d, row, fill=0.0):
    if d == 0:
        return x
    n = x.shape[0]
    y = pltpu.roll(x, n - d, 0)
    tail = jnp.where(row[:SUBLANES] < SUBLANES - d, y[n - SUBLANES:], fill)
    return jnp.concatenate([y[:n - SUBLANES], tail], axis=0)


def _conv_fwd(x, w_ref, b, row):
    K = w_ref.shape[0]
    y = b
    for k in range(K):
        y = y + w_ref[k:k + 1, :] * _shift_dn(x, K - 1 - k, row)
    return y


def _conv_bwd(x, w_ref, dy, row):
    K = w_ref.shape[0]
    dx = jnp.zeros_like(dy)
    dws = []
    for k in range(K):
        dx = dx + w_ref[k:k + 1, :] * _shift_up(dy, K - 1 - k, row)
        dws.append(jnp.sum(dy * _shift_dn(x, K - 1 - k, row), axis=0, keepdims=True))
    return dx, dws, jnp.sum(dy, axis=0, keepdims=True)


def _scan_fwd(a, u, row):
    n = a.shape[0]
    d = 1
    while d < n:
        last = 2 * d >= n
        if d < SUBLANES:
            u = u + a * _shift_dn(u, d, row)
            if not last:
                a = a * _shift_dn(a, d, row, fill=1.0)
        else:
            u = jnp.concatenate([u[:d], u[d:] + a[d:] * u[:n - d]], axis=0)
            if not last:
                a = jnp.concatenate([a[:d], a[d:] * a[:n - d]], axis=0)
        d *= 2
    return u


def _scan_bwd(b, u, row):
    n = b.shape[0]
    d = 1
    while d < n:
        last = 2 * d >= n
        if d < SUBLANES:
            u = u + b * _shift_up(u, d, row)
            if not last:
                b = b * _shift_up(b, d, row, fill=1.0)
        else:
            u = jnp.concatenate([u[:n - d] + b[:n - d] * u[d:], u[n - d:]], axis=0)
            if not last:
                b = jnp.concatenate([b[:n - d] * b[d:], b[n - d:]], axis=0)
        d *= 2
    return u


def _neg_expm1(z):
    series = -z * (1.0 + z * (0.5 + z * (1.0 / 6.0 + z * (1.0 / 24.0 + z * (1.0 / 120.0)))))
    return jnp.where(z > -0.1, series, 1.0 - jnp.exp(z))


def _softplus_neg(lam):
    z = -lam
    return jnp.maximum(z, 0.0) + jnp.log(1.0 + jnp.exp(-jnp.abs(z)))


def _rnn_specs(B):
    blk = lambda: pl.BlockSpec((SEQ, LANES), lambda b, n: (b, n))
    return dict(
        act=blk,
        convw=pl.BlockSpec((RNN_CONV, LANES), lambda b, n: (0, n)),
        vec=lambda: pl.BlockSpec((1, LANES), lambda b, n: (0, n)),
        gate=lambda: pl.BlockSpec((None, LANES, LANES), lambda b, n: (n, 0, 0)),
    )


def _rnn_fwd(proj, cw, cb, wa, ba, wx, bx, lam, comm=None):
    T = proj.shape[0]
    B = T // SEQ

    def body(x_ref, cw_ref, cb_ref, wa_ref, ba_ref, wx_ref, bx_ref, lam_ref, h_ref, xc_ref, r_ref, i_ref, a_ref, s_ref):
        row = lax.broadcasted_iota(jnp.int32, (SEQ, LANES), 0)
        xc = _conv_fwd(x_ref[...], cw_ref, cb_ref[...], row)
        xcb = xc.astype(bf16)
        r = _sigmoid(jnp.dot(xcb, wa_ref[...].astype(bf16), preferred_element_type=f32) + ba_ref[...])
        i = _sigmoid(jnp.dot(xcb, wx_ref[...].astype(bf16), preferred_element_type=f32) + bx_ref[...])
        log_a = (-LRU_C * _softplus_neg(lam_ref[...])) * r
        a = jnp.exp(log_a)
        s = jnp.sqrt(_neg_expm1(2.0 * log_a))
        xc_ref[...], r_ref[...], i_ref[...], a_ref[...], s_ref[...] = xc, r, i, a, s
        h_ref[...] = _scan_fwd(a, s * (i * xc), row)

    sp_ = _rnn_specs(B)
    return _call(
        body, (proj, cw, cb, wa, ba, wx, bx, lam), name="rnn_fwd", out_shape=(SDS((T, RNN_W), f32),) * 6,
        grid=(B, RNN_BLOCKS),
        in_specs=[sp_["act"](), sp_["convw"], sp_["vec"](), sp_["gate"](), sp_["vec"](), sp_["gate"](),
                  sp_["vec"](), sp_["vec"]()],
        out_specs=tuple(sp_["act"]() for _ in range(6)), semantics=("parallel", "parallel"), comm=comm)


def _rnn_bwd(proj, saved, dh, cw, wa, wx, lam, comm=None):
    T = proj.shape[0]
    B = T // SEQ

    def body(x_ref, h_ref, xc_ref, r_ref, i_ref, a_ref, s_ref, dh_ref, cw_ref, wa_ref, wx_ref, lam_ref,
             dx_ref, dcw_ref, dcb_ref, dwa_ref, dba_ref, dwx_ref, dbx_ref, dlam_ref):
        row = lax.broadcasted_iota(jnp.int32, (SEQ, LANES), 0)
        xr = x_ref[...]
        wa, wx, lam = wa_ref[...], wx_ref[...], lam_ref[...]
        xc, r, i, a, s = xc_ref[...], r_ref[...], i_ref[...], a_ref[...], s_ref[...]
        xcb = xc.astype(bf16)
        sp = _softplus_neg(lam)
        hprev = _shift_dn(h_ref[...], 1, row)
        g = _scan_bwd(_shift_up(a, 1, row), dh_ref[...].astype(f32), row)
        da = g * hprev
        ds = g * (i * xc)
        di = g * (s * xc)
        dxc = g * (s * i)
        dla = da * a - ds * (a * a) / s
        dr = dla * (-LRU_C * sp)
        dsp = jnp.sum(dla * (-LRU_C * r), axis=0, keepdims=True)
        dlam = -dsp * _sigmoid(-lam)
        dga = dr * r * (1.0 - r)
        dgx = di * i * (1.0 - i)
        dgab, dgxb = dga.astype(bf16), dgx.astype(bf16)
        dwa = lax.dot_general(xcb, dgab, _DIMS["tn"], preferred_element_type=f32)
        dwx = lax.dot_general(xcb, dgxb, _DIMS["tn"], preferred_element_type=f32)
        dxc = dxc + lax.dot_general(dgab, wa.astype(bf16), _DIMS["nt"], preferred_element_type=f32)
        dxc = dxc + lax.dot_general(dgxb, wx.astype(bf16), _DIMS["nt"], preferred_element_type=f32)
        dx, dws, db = _conv_bwd(xr, cw_ref, dxc, row)
        dx_ref[...] = dx.astype(bf16)
        first = pl.program_id(1) == 0

        def acc(ref, val):
            @pl.when(first)
            def _():
                ref[...] = val

            @pl.when(jnp.logical_not(first))
            def _():
                ref[...] += val

        for k in range(RNN_CONV):
            acc(dcw_ref.at[k:k + 1, :], dws[k])
        acc(dcb_ref, db)
        acc(dwa_ref, dwa)
        acc(dba_ref, jnp.sum(dga, axis=0, keepdims=True))
        acc(dwx_ref, dwx)
        acc(dbx_ref, jnp.sum(dgx, axis=0, keepdims=True))
        acc(dlam_ref, dlam)

    blk = lambda: pl.BlockSpec((SEQ, LANES), lambda n, b: (b, n))
    convw = lambda: pl.BlockSpec((RNN_CONV, LANES), lambda n, b: (0, n))
    vec = lambda: pl.BlockSpec((1, LANES), lambda n, b: (0, n))
    gate = lambda: pl.BlockSpec((None, LANES, LANES), lambda n, b: (n, 0, 0))
    vshape = SDS((1, RNN_W), f32)
    gshape = SDS((RNN_BLOCKS, LANES, LANES), f32)
    return _call(
        body, (proj, *saved, dh, cw, wa, wx, lam), name="rnn_bwd",
        out_shape=(SDS((T, RNN_W), bf16), SDS((RNN_CONV, RNN_W), f32), vshape, gshape, vshape, gshape, vshape, vshape),
        grid=(RNN_BLOCKS, B),
        in_specs=[blk() for _ in range(8)] + [convw(), gate(), gate(), vec()],
        out_specs=(blk(), convw(), vec(), gate(), vec(), gate(), vec(), vec()),
        semantics=("parallel", "arbitrary"), comm=comm)


def _t5_bucket(dist):
    max_exact = REL_BUCKETS // 2
    d = np.maximum(dist, 1).astype(np.float32)
    large = max_exact + np.log(d / max_exact) / math.log(REL_MAX_DIST / max_exact) * (REL_BUCKETS - max_exact)
    large = np.minimum(large.astype(np.int32), REL_BUCKETS - 1)
    return np.where(dist < max_exact, dist, large).astype(np.int32)


def _bucket_maps():
    qi = np.arange(ATT_BLK)[:, None]
    kj = np.arange(2 * ATT_BLK)[None, :]
    delta = ATT_BLK + qi - kj
    valid = (delta >= 0) & (delta <= ATT_BLK)
    maps = [np.where(valid, _t5_bucket(np.maximum(delta, 0) * r), -1) for r in DILATIONS]
    return np.stack(maps).astype(np.int32)


def _bias_tables(rel_bias, buckets):
    def body(rb_ref, bk_ref, o_ref):
        for g in range(NG):
            bk = bk_ref[g]
            for h in range(KVH):
                acc = jnp.full(bk.shape, NEG, f32)
                for b in range(REL_BUCKETS):
                    acc = jnp.where(bk == b, rb_ref[b, g * KVH + h], acc)
                o_ref[h, g] = acc

    return pl.pallas_call(
        body, name="bias_tables", out_shape=SDS((KVH, NG, ATT_BLK, 2 * ATT_BLK), f32),
        in_specs=[pl.BlockSpec(memory_space=pltpu.SMEM), pl.BlockSpec(memory_space=pltpu.VMEM)],
        out_specs=pl.BlockSpec(memory_space=pltpu.VMEM), compiler_params=_params())(rel_bias, buckets)


def _bias_grad(dbias, buckets):
    def body(db_ref, bk_ref, o_ref):
        rr = lax.broadcasted_iota(jnp.int32, (REL_BUCKETS, NG * KVH), 0)
        cc = lax.broadcasted_iota(jnp.int32, (REL_BUCKETS, NG * KVH), 1)
        out = jnp.zeros((REL_BUCKETS, NG * KVH), f32)
        for g in range(NG):
            bk = bk_ref[g]
            for h in range(KVH):
                d = db_ref[h, g]
                for b in range(REL_BUCKETS):
                    m = jnp.where(bk == b, d, 0.0)
                    s = jnp.sum(jnp.sum(m, axis=1, keepdims=True), axis=0, keepdims=True)
                    out = jnp.where((rr == b) & (cc == g * KVH + h), s, out)
        o_ref[...] = out

    return pl.pallas_call(body, name="bias_grad", out_shape=SDS((REL_BUCKETS, NG * KVH), f32),
                          compiler_params=_params())(dbias, buckets)


def _to_sub(dst_ref, src_ref, r, dtype, offset=0):
    M = SEQ // r
    for c in range(r):
        if r == 1:
            v = src_ref[...]
        else:
            v = src_ref[pl.ds(c, M, stride=r), :]
        dst_ref[pl.ds(offset + c * M, M), :] = v.astype(dtype)


def _from_sub(dst_ref, src_ref, r, accumulate=False, offset=0):
    M = SEQ // r
    for c in range(r):
        v = src_ref[pl.ds(offset + c * M, M), :]
        idx = slice(None) if r == 1 else pl.ds(c, M, stride=r)
        if accumulate:
            dst_ref[idx, :] = dst_ref[idx, :] + v
        else:
            dst_ref[idx, :] = v


_COL = lambda k: slice(k * HD, (k + 1) * HD)
SCALE = HD ** -0.5


def _qkv_spec(k, bh):
    def index(*ids):
        b, h = bh(*ids)
        return (b, C_ATT // HD + 5 * h + k)

    return pl.BlockSpec((SEQ, HD), index)


def _key_window(bias_ref, g, nb):
    if nb == 1:
        bias_own = bias_ref[g, :, ATT_BLK:2 * ATT_BLK]
        return lambda j: (pl.ds(pl.multiple_of((j + 1) * ATT_BLK, ATT_BLK), ATT_BLK), bias_own)
    bias_g = bias_ref[g]
    col = lax.broadcasted_iota(jnp.int32, bias_g.shape, 1)
    bias_first = jnp.where(col >= ATT_BLK, bias_g, NEG)
    return lambda j: (pl.ds(pl.multiple_of(j * ATT_BLK, ATT_BLK), 2 * ATT_BLK),
                      jnp.where(j % nb != 0, bias_g, bias_first))


def _att_fwd(proj, bias, comm=None):
    T = proj.shape[0]
    B = T // SEQ

    def body(q0_ref, q1_ref, q2_ref, k_ref, v_ref, bias_ref, o_ref, lse_ref, qp, kp, vp, kt, op, lp, og, lg):
        q_refs = (q0_ref, q1_ref, q2_ref)
        kp[0:ATT_BLK, :] = jnp.zeros((ATT_BLK, HD), bf16)
        vp[0:ATT_BLK, :] = jnp.zeros((ATT_BLK, HD), bf16)
        for g, r in enumerate(DILATIONS):
            nb = NBLK_SEQ // r
            _to_sub(qp, q_refs[g], r, bf16)
            _to_sub(kp, k_ref, r, bf16, offset=ATT_BLK)
            _to_sub(vp, v_ref, r, bf16, offset=ATT_BLK)
            kt[...] = kp[...].T
            keys = _key_window(bias_ref, g, nb)

            def step(j, carry):
                cur = pl.ds(pl.multiple_of(j * ATT_BLK, ATT_BLK), ATT_BLK)
                win, bias_j = keys(j)
                s = jnp.dot(qp[cur, :], kt[:, win], preferred_element_type=f32) * SCALE + bias_j
                m = jnp.max(s, axis=-1, keepdims=True)
                p = jnp.exp(s - m)
                den = jnp.sum(p, axis=-1, keepdims=True)
                o = jnp.dot(p.astype(bf16), vp[win, :], preferred_element_type=f32)
                op[cur, :] = o / den
                lp[cur, :] = jnp.broadcast_to(m + jnp.log(den), (ATT_BLK, HD))
                return carry

            lax.fori_loop(0, NBLK_SEQ, step, 0, unroll=ATT_UNROLL)
            _from_sub(og.at[g], op, r)
            _from_sub(lg.at[g], lp, r)
        l0, l1, l2 = lg[0], lg[1], lg[2]
        mx = jnp.maximum(jnp.maximum(l0, l1), l2)
        e0, e1, e2 = jnp.exp(l0 - mx), jnp.exp(l1 - mx), jnp.exp(l2 - mx)
        den = e0 + e1 + e2
        o_ref[...] = (e0 * og[0] + e1 * og[1] + e2 * og[2]) / den
        lse_ref[...] = mx + jnp.log(den)

    return _call(
        body, (proj, proj, proj, proj, proj, bias), name="att_fwd",
        out_shape=(SDS((T, KVH * HD), f32), SDS((KVH, T, HD), f32)), grid=(B, KVH),
        in_specs=[_qkv_spec(k, lambda b, h: (b, h)) for k in range(5)]
                 + [pl.BlockSpec((None, NG, ATT_BLK, 2 * ATT_BLK), lambda b, h: (h, 0, 0, 0))],
        out_specs=(pl.BlockSpec((SEQ, HD), lambda b, h: (b, h)),
                   pl.BlockSpec((None, SEQ, HD), lambda b, h: (h, b, 0))),
        scratch_shapes=[pltpu.VMEM((SEQ, HD), bf16)] + [pltpu.VMEM((SEQ + ATT_BLK, HD), bf16)] * 2
                       + [pltpu.VMEM((HD, SEQ + ATT_BLK), bf16)]
                       + [pltpu.VMEM((SEQ, HD), f32)] * 2 + [pltpu.VMEM((NG, SEQ, HD), f32)] * 2,
        semantics=("parallel", "parallel"), comm=comm)


def _att_bwd(proj, bias, o, lse, do, comm=None):
    T = proj.shape[0]
    B = T // SEQ

    def body(q0_ref, q1_ref, q2_ref, k_ref, v_ref, bias_ref, o_ref, lse_ref, do_ref, dx_ref, db_ref,
             qp, kp, vp, dop, qt, kt, vt, dot, lp, dlp, dqp, dkt, dvt, dln, nat, dkn, dvn):
        q_refs = (q0_ref, q1_ref, q2_ref)
        first = pl.program_id(1) == 0

        @pl.when(first)
        def _():
            db_ref[...] = jnp.zeros_like(db_ref)

        dln[...] = jnp.broadcast_to(jnp.sum(do_ref[...] * o_ref[...], axis=-1, keepdims=True), (SEQ, HD))
        dkn[...] = jnp.zeros_like(dkn)
        dvn[...] = jnp.zeros_like(dvn)
        kp[0:ATT_BLK, :] = jnp.zeros((ATT_BLK, HD), bf16)
        vp[0:ATT_BLK, :] = jnp.zeros((ATT_BLK, HD), bf16)
        for g, r in enumerate(DILATIONS):
            nb = NBLK_SEQ // r
            _to_sub(qp, q_refs[g], r, bf16)
            _to_sub(kp, k_ref, r, bf16, offset=ATT_BLK)
            _to_sub(vp, v_ref, r, bf16, offset=ATT_BLK)
            _to_sub(dop, do_ref, r, bf16)
            _to_sub(lp, lse_ref, r, f32)
            _to_sub(dlp, dln, r, f32)
            qt[...], kt[...], vt[...], dot[...] = qp[...].T, kp[...].T, vp[...].T, dop[...].T
            dkt[...] = jnp.zeros_like(dkt)
            dvt[...] = jnp.zeros_like(dvt)
            keys = _key_window(bias_ref, g, nb)
            db_cols = slice(ATT_BLK, 2 * ATT_BLK) if nb == 1 else slice(None)

            def step(j, carry):
                cur = pl.ds(pl.multiple_of(j * ATT_BLK, ATT_BLK), ATT_BLK)
                win, bias_j = keys(j)
                s = jnp.dot(qp[cur, :], kt[:, win], preferred_element_type=f32) * SCALE + bias_j
                p = jnp.exp(s - lp[cur, 0:1])
                dp = jnp.dot(dop[cur, :], vt[:, win], preferred_element_type=f32)
                ds = p * (dp - dlp[cur, 0:1])
                db_ref[g, :, db_cols] += ds
                dsb, pb = ds.astype(bf16), p.astype(bf16)
                dqp[cur, :] = jnp.dot(dsb, kp[win, :], preferred_element_type=f32) * SCALE
                dkt[:, win] += jnp.dot(qt[:, cur], dsb, preferred_element_type=f32) * SCALE
                dvt[:, win] += jnp.dot(dot[:, cur], pb, preferred_element_type=f32)
                return carry

            lax.fori_loop(0, NBLK_SEQ, step, 0, unroll=ATT_UNROLL)
            _from_sub(nat, dqp, r)
            dx_ref[:, _COL(g)] = nat[...].astype(bf16)
            dqp[...] = dkt[:, ATT_BLK:].T
            _from_sub(dkn, dqp, r, accumulate=True)
            dqp[...] = dvt[:, ATT_BLK:].T
            _from_sub(dvn, dqp, r, accumulate=True)
        dx_ref[:, _COL(3)] = dkn[...].astype(bf16)
        dx_ref[:, _COL(4)] = dvn[...].astype(bf16)

    blk = lambda: pl.BlockSpec((SEQ, HD), lambda h, b: (b, h))
    bias_spec = lambda: pl.BlockSpec((None, NG, ATT_BLK, 2 * ATT_BLK), lambda h, b: (h, 0, 0, 0))
    pad = lambda dtype: pltpu.VMEM((SEQ + ATT_BLK, HD), dtype)
    pad_t = lambda dtype: pltpu.VMEM((HD, SEQ + ATT_BLK), dtype)
    seq_t = pltpu.VMEM((HD, SEQ), bf16)
    return _call(
        body, (proj, proj, proj, proj, proj, bias, o, lse, do), name="att_bwd",
        out_shape=(SDS((T, KVH * ATT_COLS), bf16), SDS((KVH, NG, ATT_BLK, 2 * ATT_BLK), f32)), grid=(KVH, B),
        in_specs=[_qkv_spec(k, lambda h, b: (b, h)) for k in range(5)]
                 + [bias_spec(), blk(), pl.BlockSpec((None, SEQ, HD), lambda h, b: (h, b, 0)), blk()],
        out_specs=(pl.BlockSpec((SEQ, ATT_COLS), lambda h, b: (b, h)), bias_spec()),
        scratch_shapes=[pltpu.VMEM((SEQ, HD), bf16), pad(bf16), pad(bf16), pltpu.VMEM((SEQ, HD), bf16),
                        seq_t, pad_t(bf16), pad_t(bf16), seq_t]
                       + [pltpu.VMEM((SEQ, HD), f32)] * 3 + [pad_t(f32)] * 2 + [pltpu.VMEM((SEQ, HD), f32)] * 4,
        semantics=("parallel", "arbitrary"), comm=comm)


MERGE_ROWS, MERGE_COLS = 1024, 256
_G_RNN_BLK = C_GATE // MERGE_COLS
_G_ATT_BLK = (C_GATE + D) // MERGE_COLS


def _merge_specs():
    cols = lambda off: pl.BlockSpec((MERGE_ROWS, MERGE_COLS), lambda i, j: (i, off + j))
    return cols(_G_RNN_BLK), cols(_G_ATT_BLK), cols(0)


def _merge_fwd(proj, pr, pa):
    def body(gr_ref, ga_ref, pr_ref, pa_ref, o_ref):
        o_ref[...] = (_sigmoid(gr_ref[...]) * pr_ref[...].astype(f32)
                      + _sigmoid(ga_ref[...]) * pa_ref[...].astype(f32)).astype(bf16)

    T = proj.shape[0]
    s_gr, s_ga, s0 = _merge_specs()
    return pl.pallas_call(body, name="merge_fwd", out_shape=SDS((T, D), bf16),
                          grid=(T // MERGE_ROWS, D // MERGE_COLS),
                          in_specs=[s_gr, s_ga, s0, s0], out_specs=s0,
                          compiler_params=_params(("parallel", "parallel")))(proj, proj, pr, pa)


def _merge_bwd(proj, pr, pa, dm):
    nj = D // MERGE_COLS

    def body(g_ref, pr_ref, pa_ref, dm_ref, dp_ref, dg_ref):
        dm_ = dm_ref[...].astype(f32)
        s = _sigmoid(g_ref[...])
        p = jnp.where(pl.program_id(1) < nj, pr_ref[...], pa_ref[...]).astype(f32)
        dp_ref[...] = (dm_ * s).astype(bf16)
        dg_ref[...] = (dm_ * p * s * (1.0 - s)).astype(bf16)

    T = proj.shape[0]
    blk = (MERGE_ROWS, MERGE_COLS)
    wrap = pl.BlockSpec(blk, lambda i, j: (i, j % nj))
    pr_spec = pl.BlockSpec(blk, lambda i, j: (i, jnp.minimum(j, nj - 1)))
    pa_spec = pl.BlockSpec(blk, lambda i, j: (i, jnp.maximum(j - nj, 0)))
    out = pl.BlockSpec(blk, lambda i, j: (i, j))
    return pl.pallas_call(
        body, name="merge_bwd", out_shape=(SDS((T, 2 * D), bf16), SDS((T, 2 * D), bf16)),
        grid=(T // MERGE_ROWS, 2 * nj),
        in_specs=[pl.BlockSpec(blk, lambda i, j: (i, _G_RNN_BLK + j)), pr_spec, pa_spec, wrap], out_specs=(out, out),
        compiler_params=_params(("parallel", "parallel")))(proj, pr, pa, dm)


FFN_COLS = 256
GELU_C = math.sqrt(2.0 / math.pi)
GELU_A = 0.044715


def _gelu_parts(x):
    t = jnp.tanh(GELU_C * (x + GELU_A * x * x * x))
    return 0.5 * x * (1.0 + t), t


def _ffn_act_fwd(gpre, up, cw, cb):
    def body(g_ref, u_ref, cw_ref, cb_ref, o_ref):
        row = lax.broadcasted_iota(jnp.int32, (SEQ, FFN_COLS), 0)
        gate = _conv_fwd(g_ref[...].astype(f32), cw_ref, cb_ref[...], row)
        o_ref[...] = (_gelu_parts(gate)[0] * u_ref[...].astype(f32)).astype(bf16)

    T = gpre.shape[0]
    blk = lambda: pl.BlockSpec((SEQ, FFN_COLS), lambda b, j: (b, j))
    return pl.pallas_call(
        body, name="ffn_act_fwd", out_shape=SDS((T, FFN_W), bf16), grid=(T // SEQ, FFN_W // FFN_COLS),
        in_specs=[blk(), blk(), pl.BlockSpec((FFN_CONV, FFN_COLS), lambda b, j: (0, j)),
                  pl.BlockSpec((1, FFN_COLS), lambda b, j: (0, j))],
        out_specs=blk(), compiler_params=_params(("parallel", "parallel")))(gpre, up, cw, cb)


def _ffn_act_bwd(gpre, up, cw, cb, dact):
    def body(g_ref, u_ref, cw_ref, cb_ref, da_ref, dg_ref, du_ref, dcw_ref, dcb_ref):
        row = lax.broadcasted_iota(jnp.int32, (SEQ, FFN_COLS), 0)
        gp = g_ref[...].astype(f32)
        gate = _conv_fwd(gp, cw_ref, cb_ref[...], row)
        gel, t = _gelu_parts(gate)
        da = da_ref[...].astype(f32)
        du_ref[...] = (da * gel).astype(bf16)
        dgel = 0.5 * (1.0 + t) + 0.5 * gate * (1.0 - t * t) * (GELU_C * (1.0 + 3.0 * GELU_A * gate * gate))
        dgate = da * u_ref[...].astype(f32) * dgel
        dx, dws, db = _conv_bwd(gp, cw_ref, dgate, row)
        dg_ref[...] = dx.astype(bf16)
        first = pl.program_id(1) == 0

        def acc(ref, val):
            @pl.when(first)
            def _():
                ref[...] = val

            @pl.when(jnp.logical_not(first))
            def _():
                ref[...] += val

        for k in range(FFN_CONV):
            acc(dcw_ref.at[k:k + 1, :], dws[k])
        acc(dcb_ref, db)

    T = gpre.shape[0]
    blk = lambda: pl.BlockSpec((SEQ, FFN_COLS), lambda j, b: (b, j))
    cws = lambda: pl.BlockSpec((FFN_CONV, FFN_COLS), lambda j, b: (0, j))
    cbs = lambda: pl.BlockSpec((1, FFN_COLS), lambda j, b: (0, j))
    return pl.pallas_call(
        body, name="ffn_act_bwd",
        out_shape=(SDS((T, FFN_W), bf16), SDS((T, FFN_W), bf16), SDS((FFN_CONV, FFN_W), f32), SDS((1, FFN_W), f32)),
        grid=(FFN_W // FFN_COLS, T // SEQ),
        in_specs=[blk(), blk(), cws(), cbs(), blk()], out_specs=(blk(), blk(), cws(), cbs()),
        compiler_params=_params(("parallel", "arbitrary")))(gpre, up, cw, cb, dact)


def _coords():
    return lax.axis_index("x"), lax.axis_index("y"), lax.axis_index("c")


def _dev_index(dev):
    return 4 * dev[0] + 2 * dev[1] + dev[2]


def _dma_sems(n):
    return [pltpu.SemaphoreType.DMA((n,)), pltpu.SemaphoreType.DMA((n,))]


def _gather_two_level(arrays):
    n = len(arrays)

    def plan(ins, outs, sems):
        send_sems, recv_sems, local_sems = sems
        x, y, c = _coords()
        me, sibling = (x, y, c), (x, y, 1 - c)
        chips = [(1 - x, y), (x, 1 - y), (1 - x, 1 - y)]

        def copy(a, k, block, to, own=False):
            dst = outs[a].at[_dev_index(block)]
            return pltpu.make_async_remote_copy(
                src_ref=ins[a] if own else dst, dst_ref=dst, send_sem=send_sems.at[7 * a + k],
                recv_sem=recv_sems.at[7 * a + k], device_id=to, device_id_type=MESH)

        mine = [pltpu.make_async_copy(ins[a], outs[a].at[_dev_index(me)], local_sems.at[a]) for a in range(n)]
        first = [copy(a, 0, me, sibling, own=True) for a in range(n)]
        first += [copy(a, 1 + j, me, (*chip, c), own=True) for a in range(n) for j, chip in enumerate(chips)]
        passed = [[copy(a, 4 + j, (*chip, c), sibling) for a in range(n)] for j, chip in enumerate(chips)]
        arrive_ici = [[copy(a, 1 + j, (*chip, c), me) for a in range(n)] for j, chip in enumerate(chips)]
        arrive_d2d = [copy(a, 0, sibling, me) for a in range(n)]
        arrive_d2d += [copy(a, 4 + j, (*chip, 1 - c), me) for a in range(n) for j, chip in enumerate(chips)]
        return mine, first, passed, arrive_ici, arrive_d2d

    def start(ins, outs, sems):
        mine, first, _, _, _ = plan(ins, outs, sems)
        for cp in mine + first:
            cp.start()

    def finish(ins, outs, sems):
        mine, first, passed, arrive_ici, arrive_d2d = plan(ins, outs, sems)
        for j in range(3):
            for cp in arrive_ici[j]:
                cp.wait_recv()
            for cp in passed[j]:
                cp.start()
        for cp in arrive_d2d:
            cp.wait_recv()
        for cp in first + [cp for group in passed for cp in group]:
            cp.wait_send()
        for cp in mine:
            cp.wait()

    return _Comm(arrays, [SDS((N_DEV,) + a.shape, a.dtype) for a in arrays],
                 _dma_sems(7 * n) + [pltpu.SemaphoreType.DMA((n,))], start, finish)


def _gather_direct(arrays):
    n = len(arrays)

    def plan(ins, outs, sems):
        send_sems, recv_sems, local_sems = sems
        x, y, c = _coords()
        me = (x, y, c)
        mine = [pltpu.make_async_copy(ins[a], outs[a].at[_dev_index(me)], local_sems.at[a]) for a in range(n)]
        sends, arrivals = [], []
        for a in range(n):
            for k in range(1, N_DEV):
                peer = (1 - x if k & 4 else x, 1 - y if k & 2 else y, 1 - c if k & 1 else c)
                s = 7 * a + k - 1
                for slot, out in ((me, sends), (peer, arrivals)):
                    out.append(pltpu.make_async_remote_copy(
                        src_ref=ins[a], dst_ref=outs[a].at[_dev_index(slot)], send_sem=send_sems.at[s],
                        recv_sem=recv_sems.at[s], device_id=peer, device_id_type=MESH))
        return mine, sends, arrivals

    def start(ins, outs, sems):
        mine, sends, _ = plan(ins, outs, sems)
        for cp in mine + sends:
            cp.start()

    def finish(ins, outs, sems):
        mine, sends, arrivals = plan(ins, outs, sems)
        for cp in arrivals:
            cp.wait_recv()
        for cp in sends:
            cp.wait_send()
        for cp in mine:
            cp.wait()

    return _Comm(arrays, [SDS((N_DEV,) + a.shape, a.dtype) for a in arrays],
                 _dma_sems(7 * n) + [pltpu.SemaphoreType.DMA((n,))], start, finish)


def _scatter_direct(arrays):
    n = len(arrays)

    def plan(ins, outs, sems):
        send_sems, recv_sems = sems
        x, y, c = _coords()
        cps = []
        for a in range(n):
            for k in range(1, N_DEV):
                peer = (1 - x if k & 4 else x, 1 - y if k & 2 else y, 1 - c if k & 1 else c)
                s = 7 * a + k - 1
                cps.append(pltpu.make_async_remote_copy(
                    src_ref=ins[a].at[_dev_index(peer)], dst_ref=outs[a].at[k - 1], send_sem=send_sems.at[s],
                    recv_sem=recv_sems.at[s], device_id=peer, device_id_type=MESH))
        return cps

    def start(ins, outs, sems):
        for cp in plan(ins, outs, sems):
            cp.start()

    def finish(ins, outs, sems):
        for cp in plan(ins, outs, sems):
            cp.wait()

    return _Comm(arrays, [SDS((N_DEV - 1,) + a.shape[1:], a.dtype) for a in arrays], _dma_sems(7 * n),
                 start, finish)


def _run(comm, name):
    def body(*refs):
        k_in, k_out = len(comm.inputs), len(comm.out_shapes)
        ins, outs, sems = refs[:k_in], refs[k_in:k_in + k_out], refs[k_in + k_out:]
        comm.start(ins, outs, sems)
        comm.finish(ins, outs, sems)

    return pl.pallas_call(body, name=name, out_shape=comm.out_shapes, in_specs=[ANY] * len(comm.inputs),
                          out_specs=(ANY,) * len(comm.out_shapes), scratch_shapes=comm.sem_shapes)(*comm.inputs)


TILE_ELEMS = 192 * 1024


def _row_tile(R, C):
    if R * C <= TILE_ELEMS:
        return R
    return max(t for t in range(SUBLANES, R, SUBLANES) if R % t == 0 and t * C <= TILE_ELEMS)


def _adamw_math(w, g, m, v):
    m = ADAM_B1 * m + (1.0 - ADAM_B1) * g
    v = ADAM_B2 * v + (1.0 - ADAM_B2) * (g * g)
    m_hat = m / (1.0 - ADAM_B1 ** ADAM_STEP)
    v_hat = v / (1.0 - ADAM_B2 ** ADAM_STEP)
    delta = -ADAM_LR * (m_hat / (jnp.sqrt(v_hat) + ADAM_EPS) + ADAM_WD * w)
    return delta, m, v


def _adamw_sharded(own, recv, d_idx, w, m, v, name):
    R, C = w.shape
    t = _row_tile(R, C)

    def body(k_ref, p_ref, r_ref, w_ref, m_ref, v_ref, g_ref, d_ref, nm_ref, nv_ref):
        g = p_ref[...].astype(f32)
        for j in range(N_DEV - 1):
            g = g + r_ref[j].astype(f32)
        d, nm, nv = _adamw_math(w_ref[...], g, m_ref[...], v_ref[...])
        g_ref[...], d_ref[...], nm_ref[...], nv_ref[...] = g, d, nm, nv

    tile = lambda: pl.BlockSpec((t, C), lambda i, k: (i, 0))
    return pl.pallas_call(
        body, name="adamw_" + name, out_shape=(SDS((R, C), f32),) * 4,
        grid_spec=pltpu.PrefetchScalarGridSpec(
            num_scalar_prefetch=1, grid=(R // t,),
            in_specs=[pl.BlockSpec((None, t, C), lambda i, k: (k[0], i, 0)),
                      pl.BlockSpec((N_DEV - 1, t, C), lambda i, k: (0, i, 0)), tile(), tile(), tile()],
            out_specs=(tile(), tile(), tile(), tile())),
        compiler_params=_params(("parallel",)))(d_idx, own, recv, w, m, v)


def _adamw_replicated(parts, ws, ms, vs):
    n = len(ws)

    def body(*refs):
        p, w, m, v = (refs[i * n:(i + 1) * n] for i in range(4))
        outs = refs[4 * n:]
        for a in range(n):
            g = p[a][0].astype(f32)
            for j in range(1, N_DEV):
                g = g + p[a][j].astype(f32)
            d, nm, nv = _adamw_math(w[a][...], g, m[a][...], v[a][...])
            for i, val in enumerate((g, d, nm, nv)):
                outs[i * n + a][...] = val

    shapes = tuple(SDS(w.shape, f32) for w in ws)
    res = pl.pallas_call(body, name="adamw_replicated", out_shape=shapes * 4,
                         compiler_params=_params())(*parts, *ws, *ms, *vs)
    return [res[i * n:(i + 1) * n] for i in range(4)]


def _cols_to_full(g):
    n, r, c = g.shape
    return g.transpose(1, 0, 2).reshape(r, n * c)


def _full_to_cols(a):
    r, c = a.shape
    return a.reshape(r, N_DEV, c // N_DEV).transpose(1, 0, 2)


def _rows_blocked(a):
    r, c = a.shape
    return a.reshape(N_DEV, r // N_DEV, c)


def _w_in_to_internal(w):
    K = w.shape[0]
    q = w[:, 1280:2816].reshape(K, NG, KVH, 1, HD).transpose(0, 2, 1, 3, 4).reshape(K, KVH, NG, HD)
    k = w[:, 2816:3328].reshape(K, KVH, 1, HD)
    v = w[:, 3328:3840].reshape(K, KVH, 1, HD)
    att = jnp.concatenate([q, k, v], axis=2).reshape(K, KVH * ATT_COLS)
    return jnp.concatenate([w[:, :1280], att, w[:, 3840:]], axis=1)


def _w_in_from_internal(w):
    K = w.shape[0]
    att = w[:, C_ATT:C_GATE].reshape(K, KVH, 5, HD)
    q = att[:, :, 0:3].transpose(0, 2, 1, 3).reshape(K, NG * KVH * HD)
    k = att[:, :, 3].reshape(K, KVH * HD)
    v = att[:, :, 4].reshape(K, KVH * HD)
    return jnp.concatenate([w[:, :C_ATT], q, k, v, w[:, C_GATE:]], axis=1)


_IN_NAMES = ('x', 'rel_bias', 'norm_mix_pre', 'norm_mix_post', 'w_in', 'conv_rnn_w', 'conv_rnn_b', 'w_rg_a', 'b_rg_a',
             'w_rg_x', 'b_rg_x', 'lru_lambda', 'w_branch_rnn', 'w_branch_att', 'w_out', 'norm_ffn_pre',
             'norm_ffn_post', 'w_ffn_gate', 'w_ffn_up', 'conv_ffn_w', 'conv_ffn_b', 'w_ffn_down')
_WEIGHTS = _IN_NAMES[1:]
_SHARDED = {"w_in": "col", "conv_rnn_w": "col", "w_branch_rnn": "row", "w_branch_att": "col", "w_out": "row",
            "w_ffn_gate": "col", "w_ffn_up": "col", "conv_ffn_w": "col", "w_ffn_down": "row"}
_REPLICATED = tuple(n for n in _WEIGHTS if n not in _SHARDED)


def _flat2(a):
    return a.reshape(-1, a.shape[-1])


def _train_step(inp):
    x_idx, y_idx, c_idx = _coords()
    W = {n: inp[n] for n in _WEIGHTS}
    x = inp["x"].reshape(-1, D)
    target = inp["loss_target"].reshape(-1, D)
    shard = {n: inp[n][0] for n in _SHARDED}

    hn, (g_in, g_cr, g_cf) = _norm_in(x, W["norm_mix_pre"], comm=_gather_two_level(
        [shard["w_in"].astype(bf16), shard["conv_rnn_w"], shard["conv_ffn_w"]]))
    w_in = _w_in_to_internal(_cols_to_full(g_in))
    cw_rnn, cw_ffn = _cols_to_full(g_cr), _cols_to_full(g_cf)
    behind_rnn = ("w_branch_rnn", "w_branch_att", "w_out", "w_ffn_down")
    behind_att = ("w_ffn_gate", "w_ffn_up")

    wa, wx = W["w_rg_a"][0], W["w_rg_x"][0]
    buckets = jnp.asarray(_bucket_maps())

    proj = _mm(hn, w_in, "nn", f32, "mm_proj", 512, IN_W // 2, 1024, cols_outer=True)
    rnn_saved, got = _rnn_fwd(proj, cw_rnn, W["conv_rnn_b"], wa, W["b_rg_a"], wx, W["b_rg_x"], W["lru_lambda"],
                              comm=_gather_direct([shard[n].astype(bf16) for n in behind_rnn]))
    h_rnn = rnn_saved[0]
    gathered = dict(zip(behind_rnn, got))
    bias = _bias_tables(W["rel_bias"], buckets)
    (o_att, lse), got = _att_fwd(proj, bias, comm=_gather_direct([shard[n].astype(bf16) for n in behind_att]))
    gathered.update(zip(behind_att, got))
    w_brnn = gathered["w_branch_rnn"].reshape(RNN_W, D)
    w_batt = _cols_to_full(gathered["w_branch_att"])
    w_out = gathered["w_out"].reshape(D, D)
    w_gate, w_up = _cols_to_full(gathered["w_ffn_gate"]), _cols_to_full(gathered["w_ffn_up"])
    w_down = gathered["w_ffn_down"].reshape(FFN_W, D)
    pr = _mm(h_rnn, w_brnn, "nn", bf16, "mm_pr", 1024, 1024, 1280)
    pa = _mm(o_att, w_batt, "nn", bf16, "mm_pa", 1024, 1024, 512)
    merged = _merge_fwd(proj, pr, pa)
    mix = _mm(merged, w_out, "nn", f32, "mm_mix", 1024, 1024, 1024)
    h1, hn2 = _mid_fwd(x, mix, W["norm_mix_post"], W["norm_ffn_pre"])
    gpre = _mm(hn2, w_gate, "nn", bf16, "mm_gate", 1024, 1024, 1024, cols_outer=True)
    up = _mm(hn2, w_up, "nn", bf16, "mm_up", 1024, 1024, 1024, cols_outer=True)
    act = _ffn_act_fwd(gpre, up, cw_ffn, W["conv_ffn_b"])
    ff = _mm(act, w_down, "nn", f32, "mm_down", 1024, 1024, 1024)
    loss_part, dy, dff, dg_fpost = _final(h1, ff, W["norm_ffn_post"], target)

    grads = {}
    dact = _mm(dff, w_down, "nt", bf16, "mm_dact", 1024, 1024, 1024, cols_outer=True)
    grads["w_ffn_down"] = _rows_blocked(_mm(act, dff, "tn", bf16, "mm_dw_down", 1024, 1024, 2048))
    dgpre, dup, dcw_ffn, dcb_ffn = _ffn_act_bwd(gpre, up, cw_ffn, W["conv_ffn_b"], dact)
    grads["conv_ffn_w"] = _full_to_cols(dcw_ffn.astype(bf16))
    grads["w_ffn_gate"] = _full_to_cols(_mm(hn2, dgpre, "tn", bf16, "mm_dw_gate", 1024, 1024, 2048))
    grads["w_ffn_up"] = _full_to_cols(_mm(hn2, dup, "tn", bf16, "mm_dw_up", 1024, 1024, 2048))
    dhn2 = _mm_nt_sum([(dgpre, w_gate, 1024), (dup, w_up, 1024)], "mm_dhn2", 1024, 1024)
    dh1, dmix, dg_fpre, dg_post = _mid_bwd(dy, dhn2, h1, W["norm_ffn_pre"], mix, W["norm_mix_post"])
    dmerged = _mm(dmix, w_out, "nt", bf16, "mm_dmerged", 1024, 1024, 1024)
    grads["w_out"] = _rows_blocked(_mm(merged, dmix, "tn", bf16, "mm_dw_out", 1024, 1024, 2048))
    dprpa, dgates = _merge_bwd(proj, pr, pa, dmerged)
    dpr, dpa = dprpa[:, :D], dprpa[:, D:]
    dh_rnn = _mm(dpr, w_brnn, "nt", bf16, "mm_dh_rnn", 1024, 1280, 1024)
    grads["w_branch_rnn"] = _rows_blocked(_mm(h_rnn, dpr, "tn", bf16, "mm_dw_brnn", 1280, 1024, 1024))
    do_att = _mm(dpa, w_batt, "nt", f32, "mm_do_att", 1024, 512, 1024)
    grads["w_branch_att"] = _full_to_cols(_mm(o_att, dpa, "tn", bf16, "mm_dw_batt", 512, 1024, 2048))

    received = {}
    behind_att_bwd = ("w_ffn_down", "w_ffn_gate", "conv_ffn_w", "w_out")
    behind_rnn_bwd = ("w_ffn_up", "w_branch_rnn", "w_branch_att")
    (dqkv, dbias), got = _att_bwd(proj, bias, o_att, lse, do_att,
                                  comm=_scatter_direct([grads[n] for n in behind_att_bwd]))
    received.update(zip(behind_att_bwd, got))
    drel = _bias_grad(dbias, buckets)
    (dxr, dcw_rnn, dcb_rnn, dwa, dba, dwx, dbx, dlam), got = _rnn_bwd(
        proj, rnn_saved, dh_rnn, cw_rnn, wa, wx, W["lru_lambda"],
        comm=_scatter_direct([grads[n] for n in behind_rnn_bwd]))
    received.update(zip(behind_rnn_bwd, got))
    gsmall = {"rel_bias": drel, "norm_mix_post": dg_post, "conv_rnn_b": dcb_rnn, "w_rg_a": dwa.astype(bf16),
              "b_rg_a": dba, "w_rg_x": dwx.astype(bf16), "b_rg_x": dbx, "lru_lambda": dlam,
              "norm_ffn_pre": dg_fpre, "norm_ffn_post": dg_fpost, "conv_ffn_b": dcb_ffn}
    dw_in_a, parts = _mm(hn, dqkv, "tn", bf16, "mm_dw_in_a", 1024, 1280, 1024,
                         comm=_gather_direct([_flat2(gsmall[n]) for n in gsmall]))
    parts = dict(zip(gsmall, parts))
    dw_in = jnp.concatenate([_mm(hn, dxr, "tn", bf16, "mm_dw_in_r", 1024, 1280, 1024), dw_in_a,
                             _mm(hn, dgates, "tn", bf16, "mm_dw_in_g", 1024, 1024, 2048)], axis=1)
    grads["w_in"] = _full_to_cols(_w_in_from_internal(dw_in))
    grads["conv_rnn_w"] = _full_to_cols(dcw_rnn.astype(bf16))
    behind_dhn = ("w_in", "conv_rnn_w")
    dhn, got = _mm_nt_sum([(dxr, w_in[:, :C_ATT], 1280), (dqkv, w_in[:, C_ATT:C_GATE], 1280),
                           (dgates, w_in[:, C_GATE:], 1024)], "mm_dhn", 1024, 1024,
                          comm=_scatter_direct([grads[n] for n in behind_dhn]))
    received.update(zip(behind_dhn, got))
    dx, dg_pre = _in_bwd(dh1, dhn, x, W["norm_mix_pre"])
    parts["norm_mix_pre"], = _run(_gather_two_level([dg_pre]), "ag_norm_mix_pre")
    parts = [parts[n] for n in _REPLICATED]

    out = {}
    d_arr = jnp.reshape(4 * x_idx + 2 * y_idx + c_idx, (1,)).astype(jnp.int32)
    for n in _SHARDED:
        res = _adamw_sharded(grads[n], received[n], d_arr, shard[n], inp["m_" + n][0], inp["v_" + n][0], n)
        out[n] = [r[None] for r in res]
    small = _adamw_replicated(parts, *[[_flat2(inp[p + n]) for n in _REPLICATED] for p in ("", "m_", "v_")])
    for a, n in enumerate(_REPLICATED):
        out[n] = [small[i][a].reshape(inp[n].shape) for i in range(4)]

    loss = lax.psum(loss_part[0, 0], ("x", "y", "c"))
    outs = [loss, dx.reshape(inp["x"].shape)]
    for i in range(4):
        outs.extend(out[n][i] for n in _WEIGHTS)
    return tuple(outs)


def kernel(x, rel_bias, norm_mix_pre, norm_mix_post, w_in, conv_rnn_w, conv_rnn_b, w_rg_a, b_rg_a, w_rg_x, b_rg_x, lru_lambda, w_branch_rnn, w_branch_att, w_out, norm_ffn_pre, norm_ffn_post, w_ffn_gate, w_ffn_up, conv_ffn_w, conv_ffn_b, w_ffn_down, loss_target, m_rel_bias, m_norm_mix_pre, m_norm_mix_post, m_w_in, m_conv_rnn_w, m_conv_rnn_b, m_w_rg_a, m_b_rg_a, m_w_rg_x, m_b_rg_x, m_lru_lambda, m_w_branch_rnn, m_w_branch_att, m_w_out, m_norm_ffn_pre, m_norm_ffn_post, m_w_ffn_gate, m_w_ffn_up, m_conv_ffn_w, m_conv_ffn_b, m_w_ffn_down, v_rel_bias, v_norm_mix_pre, v_norm_mix_post, v_w_in, v_conv_rnn_w, v_conv_rnn_b, v_w_rg_a, v_b_rg_a, v_w_rg_x, v_b_rg_x, v_lru_lambda, v_w_branch_rnn, v_w_branch_att, v_w_out, v_norm_ffn_pre, v_norm_ffn_post, v_w_ffn_gate, v_w_ffn_up, v_conv_ffn_w, v_conv_ffn_b, v_w_ffn_down):
    vals = locals()
    names = list(_IN_NAMES) + ["loss_target"] + ["m_" + n for n in _WEIGHTS] + ["v_" + n for n in _WEIGHTS]
    return _train_step({n: vals[n] for n in names})
```

```python
import functools
import math

import numpy as np
import jax
import jax.numpy as jnp
from jax import lax
from jax.experimental import pallas as pl
from jax.experimental.pallas import tpu as pltpu

f32, bf16 = jnp.float32, jnp.bfloat16
SDS = jax.ShapeDtypeStruct
MESH = pl.DeviceIdType.MESH
ANY = pl.BlockSpec(memory_space=pl.ANY)

D = 1024
SEQ = 2048
RNN_W = 1280
RNN_BLOCKS = 10
LANES = 128
SUBLANES = 8
RNN_CONV = 4
LRU_C = 8.0
HD = 128
KVH = 4
DILATIONS = (1, 4, 16)
NG = 3
ATT_BLK = 128
NBLK_SEQ = SEQ // ATT_BLK
ATT_UNROLL = 8
REL_BUCKETS = 32
REL_MAX_DIST = 2048
FFN_W = 3072
FFN_CONV = 3
EPS = 1e-6
IN_W = 5888
ATT_COLS = 5 * HD
C_ATT = RNN_W
C_GATE = RNN_W + KVH * ATT_COLS
NEG = -1e30

ADAM_LR, ADAM_B1, ADAM_B2, ADAM_EPS, ADAM_WD, ADAM_STEP = 0.001, 0.9, 0.999, 1e-08, 0.01, 10

VMEM_LIMIT_BYTES = 56 * 1024 * 1024
N_DEV = 8


def _params(sem=None):
    return pltpu.CompilerParams(dimension_semantics=sem, vmem_limit_bytes=VMEM_LIMIT_BYTES)


def _sigmoid(x):
    return 1.0 / (1.0 + jnp.exp(-x))


class _Comm:
    def __init__(self, inputs, out_shapes, sem_shapes, start, finish):
        self.inputs, self.out_shapes, self.sem_shapes = tuple(inputs), tuple(out_shapes), list(sem_shapes)
        self.start, self.finish = start, finish


def _call(body, args, *, name, grid, in_specs, out_specs, out_shape, scratch_shapes=(), semantics, comm=None):
    if comm is None:
        return pl.pallas_call(body, name=name, grid=grid, in_specs=list(in_specs), out_specs=tuple(out_specs),
                              out_shape=tuple(out_shape), scratch_shapes=list(scratch_shapes),
                              compiler_params=_params(semantics))(*args), ()
    n_in, n_out, n_scr = len(in_specs), len(out_shape), len(scratch_shapes)
    c_in, c_out = len(comm.inputs), len(comm.out_shapes)

    def fused(*refs):
        ins, refs = refs[:n_in], refs[n_in:]
        cin, refs = refs[:c_in], refs[c_in:]
        outs, refs = refs[:n_out], refs[n_out:]
        cout, refs = refs[:c_out], refs[c_out:]
        scr, csem = refs[:n_scr], refs[n_scr:]
        first = functools.reduce(jnp.logical_and, [pl.program_id(d) == 0 for d in range(len(grid))])
        last = functools.reduce(jnp.logical_and, [pl.program_id(d) == grid[d] - 1 for d in range(len(grid))])

        @pl.when(first)
        def _():
            comm.start(cin, cout, csem)

        body(*ins, *outs, *scr)

        @pl.when(last)
        def _():
            comm.finish(cin, cout, csem)

    res = pl.pallas_call(
        fused, name=name, grid=grid, in_specs=list(in_specs) + [ANY] * c_in,
        out_specs=tuple(out_specs) + (ANY,) * c_out, out_shape=tuple(out_shape) + comm.out_shapes,
        scratch_shapes=list(scratch_shapes) + comm.sem_shapes,
        compiler_params=_params(("arbitrary",) * len(grid)))(*args, *comm.inputs)
    return res[:n_out], res[n_out:]


_DIMS = {"nn": (((1,), (0,)), ((), ())), "nt": (((1,), (1,)), ((), ())), "tn": (((0,), (0,)), ((), ()))}


def _mm(a, b, mode, out_dtype, name, tm, tn, tk, cols_outer=False, comm=None):
    if mode == "nn":
        (M, K), (K2, N) = a.shape, b.shape
    elif mode == "nt":
        (M, K), (N, K2) = a.shape, b.shape
    else:
        (K, M), (K2, N) = a.shape, b.shape
    assert K == K2 and M % tm == 0 and N % tn == 0 and K % tk == 0, (name, a.shape, b.shape)
    nk = K // tk

    def body(a_ref, b_ref, o_ref, *scratch):
        part = lax.dot_general(a_ref[...].astype(bf16), b_ref[...].astype(bf16), _DIMS[mode],
                               preferred_element_type=f32)
        if nk == 1:
            o_ref[...] = part.astype(o_ref.dtype)
        else:
            acc_ref, = scratch
            k = pl.program_id(2)

            @pl.when(k == 0)
            def _():
                acc_ref[...] = part

            @pl.when(k > 0)
            def _():
                acc_ref[...] += part

            @pl.when(k == nk - 1)
            def _():
                o_ref[...] = acc_ref[...].astype(o_ref.dtype)

    def ij(f):
        return (lambda j, i, k: f(i, j, k)) if cols_outer else f

    if mode == "tn":
        a_spec = pl.BlockSpec((tk, tm), ij(lambda i, j, k: (k, i)))
    else:
        a_spec = pl.BlockSpec((tm, tk), ij(lambda i, j, k: (i, k)))
    if mode == "nt":
        b_spec = pl.BlockSpec((tn, tk), ij(lambda i, j, k: (j, k)))
    else:
        b_spec = pl.BlockSpec((tk, tn), ij(lambda i, j, k: (k, j)))
    o_spec = pl.BlockSpec((tm, tn), ij(lambda i, j, k: (i, j)))
    grid = (N // tn, M // tm, nk) if cols_outer else (M // tm, N // tn, nk)
    (out,), extra = _call(
        body, (a, b), name=name, out_shape=(SDS((M, N), out_dtype),), grid=grid, in_specs=[a_spec, b_spec],
        out_specs=(o_spec,), scratch_shapes=[pltpu.VMEM((tm, tn), f32)] if nk > 1 else [],
        semantics=("parallel", "parallel", "arbitrary"), comm=comm)
    return out if comm is None else (out, extra)


ROWS_TM = 512


def _mm_rows(pairs, mode, name, epilogue, rows_in, vecs_in, rows_out, vecs_out, comm=None):
    tm = ROWS_TM
    M = pairs[0][0].shape[0]
    N = pairs[0][1].shape[1 if mode == "nn" else 0]
    nks = [a.shape[1] // tk for a, _, tk in pairs]
    starts = [sum(nks[:p]) for p in range(len(pairs))]
    nk = sum(nks)
    n_rows_in, n_vecs_in, n_rows_out = len(rows_in), len(vecs_in), len(rows_out)

    def body(*refs):
        pair_refs, refs = refs[:2 * len(pairs)], refs[2 * len(pairs):]
        rin, refs = refs[:n_rows_in], refs[n_rows_in:]
        vin, refs = refs[:n_vecs_in], refs[n_vecs_in:]
        rout, refs = refs[:n_rows_out], refs[n_rows_out:]
        vout, acc_ref = refs[:-1], refs[-1]
        i, k = pl.program_id(0), pl.program_id(1)
        for p in range(len(pairs)):
            def product(p=p):
                return lax.dot_general(pair_refs[2 * p][...], pair_refs[2 * p + 1][...], _DIMS[mode],
                                       preferred_element_type=f32)

            if p == 0:
                @pl.when(k == 0)
                def _():
                    acc_ref[...] = product()

            @pl.when((k >= max(starts[p], 1)) & (k < starts[p] + nks[p]))
            def _():
                acc_ref[...] += product()

        @pl.when(k == nk - 1)
        def _():
            res = epilogue(acc_ref[...], *[r[...] for r in rin], *[v[...] for v in vin])
            for ref, val in zip(rout, res[:n_rows_out]):
                ref[...] = val.astype(ref.dtype)
            for ref, val in zip(vout, res[n_rows_out:]):
                @pl.when(i == 0)
                def _(ref=ref, val=val):
                    ref[...] = val

                @pl.when(i > 0)
                def _(ref=ref, val=val):
                    ref[...] += val

    in_specs, args = [], []
    for (a, b, tk), k0, n in zip(pairs, starts, nks):
        assert a.shape[1] % tk == 0 and a.dtype == b.dtype == bf16, name
        chunk = lambda k, k0=k0, n=n: jnp.clip(k - k0, 0, n - 1)
        in_specs.append(pl.BlockSpec((tm, tk), lambda i, k, c=chunk: (i, c(k))))
        if mode == "nn":
            in_specs.append(pl.BlockSpec((tk, N), lambda i, k, c=chunk: (c(k), 0)))
        else:
            in_specs.append(pl.BlockSpec((N, tk), lambda i, k, c=chunk: (0, c(k))))
        args += [a, b]
    row = lambda: pl.BlockSpec((tm, N), lambda i, k: (i, 0))
    vec = lambda w: pl.BlockSpec((1, w), lambda i, k: (0, 0))
    in_specs += [row() for _ in rows_in] + [vec(v.shape[1]) for v in vecs_in]
    outs, extra = _call(
        body, (*args, *rows_in, *vecs_in), name=name,
        out_shape=tuple(SDS((M, N), dt) for dt in rows_out) + tuple(SDS((1, w), f32) for w in vecs_out),
        grid=(M // tm, nk), in_specs=in_specs,
        out_specs=tuple(row() for _ in rows_out) + tuple(vec(w) for w in vecs_out),
        scratch_shapes=[pltpu.VMEM((tm, N), f32)], semantics=("arbitrary", "arbitrary"), comm=comm)
    res = (outs[:n_rows_out], outs[n_rows_out:])
    return res if comm is None else (res, extra)


ROW_TILE = 512


def _rms_fwd(x, g):
    r = lax.rsqrt(jnp.mean(x * x, axis=-1, keepdims=True) + EPS)
    return x * r * g


def _rms_bwd(x, g, dy):
    r = lax.rsqrt(jnp.mean(x * x, axis=-1, keepdims=True) + EPS)
    xh = x * r
    dxh = dy * g
    dx = r * (dxh - xh * jnp.mean(dxh * xh, axis=-1, keepdims=True))
    return dx, jnp.sum(dy * xh, axis=0, keepdims=True)


def _norm_in(x, g, comm=None):
    def body(x_ref, g_ref, o_ref):
        o_ref[...] = _rms_fwd(x_ref[...], g_ref[...]).astype(bf16)

    T = x.shape[0]
    row = lambda: pl.BlockSpec((ROW_TILE, D), lambda i: (i, 0))
    (hn,), extra = _call(body, (x, g), name="norm_in", out_shape=(SDS((T, D), bf16),), grid=(T // ROW_TILE,),
                         in_specs=[row(), pl.BlockSpec((1, D), lambda i: (0, 0))], out_specs=(row(),),
                         semantics=("parallel",), comm=comm)
    return hn, extra


def _mid_fwd_rows(mix, x, g_post, g_fpre):
    h1 = x + _rms_fwd(mix, g_post)
    return mix, h1, _rms_fwd(h1, g_fpre)


def _final_rows(ff, h1, target, g_fpost):
    e = h1 + _rms_fwd(ff, g_fpost) - target
    part = jnp.sum(jnp.sum(e * e, axis=1, keepdims=True), axis=0, keepdims=True) * (0.5 / D)
    dy = e * (1.0 / D)
    dff, dg = _rms_bwd(ff, g_fpost, dy)
    return dy, dff, part, dg


def _mid_bwd_rows(dhn2, dy, h1, mix, g_fpre, g_post):
    d1, dgf = _rms_bwd(h1, g_fpre, dhn2)
    dh1 = dy + d1
    dmix, dgp = _rms_bwd(mix, g_post, dh1)
    return dh1, dmix, dgf, dgp


def _in_bwd_rows(dhn, dh1, x, g_pre):
    d, dg = _rms_bwd(x, g_pre, dhn)
    return dh1 + d, dg


def _shift_dn(x, d, row, fill=0.0):
    if d == 0:
        return x
    y = pltpu.roll(x, d, 0)
    head = jnp.where(row[:SUBLANES] >= d, y[:SUBLANES], fill)
    return jnp.concatenate([head, y[SUBLANES:]], axis=0)


def _shift_up(x, d, row, fill=0.0):
    if d == 0:
        return x
    n = x.shape[0]
    y = pltpu.roll(x, n - d, 0)
    tail = jnp.where(row[:SUBLANES] < SUBLANES - d, y[n - SUBLANES:], fill)
    return jnp.concatenate([y[:n - SUBLANES], tail], axis=0)


def _conv_fwd(x, w_ref, b, row):
    K = w_ref.shape[0]
    y = b
    for k in range(K):
        y = y + w_ref[k:k + 1, :] * _shift_dn(x, K - 1 - k, row)
    return y


def _conv_bwd(x, w_ref, dy, row):
    K = w_ref.shape[0]
    dx = jnp.zeros_like(dy)
    dws = []
    for k in range(K):
        dx = dx + w_ref[k:k + 1, :] * _shift_up(dy, K - 1 - k, row)
        dws.append(jnp.sum(dy * _shift_dn(x, K - 1 - k, row), axis=0, keepdims=True))
    return dx, dws, jnp.sum(dy, axis=0, keepdims=True)


def _scan_fwd(a, u, row):
    n = a.shape[0]
    d = 1
    while d < n:
        last = 2 * d >= n
        if d < SUBLANES:
            u = u + a * _shift_dn(u, d, row)
            if not last:
                a = a * _shift_dn(a, d, row, fill=1.0)
        else:
            u = jnp.concatenate([u[:d], u[d:] + a[d:] * u[:n - d]], axis=0)
            if not last:
                a = jnp.concatenate([a[:d], a[d:] * a[:n - d]], axis=0)
        d *= 2
    return u


def _scan_bwd(b, u, row):
    n = b.shape[0]
    d = 1
    while d < n:
        last = 2 * d >= n
        if d < SUBLANES:
            u = u + b * _shift_up(u, d, row)
            if not last:
                b = b * _shift_up(b, d, row, fill=1.0)
        else:
            u = jnp.concatenate([u[:n - d] + b[:n - d] * u[d:], u[n - d:]], axis=0)
            if not last:
                b = jnp.concatenate([b[:n - d] * b[d:], b[n - d:]], axis=0)
        d *= 2
    return u


def _neg_expm1(z):
    series = -z * (1.0 + z * (0.5 + z * (1.0 / 6.0 + z * (1.0 / 24.0 + z * (1.0 / 120.0)))))
    return jnp.where(z > -0.1, series, 1.0 - jnp.exp(z))


def _softplus_neg(lam):
    z = -lam
    return jnp.maximum(z, 0.0) + jnp.log(1.0 + jnp.exp(-jnp.abs(z)))


def _rnn_specs(B):
    blk = lambda: pl.BlockSpec((SEQ, LANES), lambda b, n: (b, n))
    return dict(
        act=blk,
        convw=pl.BlockSpec((RNN_CONV, LANES), lambda b, n: (0, n)),
        vec=lambda: pl.BlockSpec((1, LANES), lambda b, n: (0, n)),
        gate=lambda: pl.BlockSpec((None, LANES, LANES), lambda b, n: (n, 0, 0)),
    )


def _rnn_fwd(proj, cw, cb, wa, ba, wx, bx, lam, comm=None):
    T = proj.shape[0]
    B = T // SEQ

    def body(x_ref, cw_ref, cb_ref, wa_ref, ba_ref, wx_ref, bx_ref, lam_ref, h_ref, xc_ref, r_ref, i_ref, a_ref, s_ref):
        row = lax.broadcasted_iota(jnp.int32, (SEQ, LANES), 0)
        xc = _conv_fwd(x_ref[...], cw_ref, cb_ref[...], row)
        xcb = xc.astype(bf16)
        r = _sigmoid(jnp.dot(xcb, wa_ref[...].astype(bf16), preferred_element_type=f32) + ba_ref[...])
        i = _sigmoid(jnp.dot(xcb, wx_ref[...].astype(bf16), preferred_element_type=f32) + bx_ref[...])
        log_a = (-LRU_C * _softplus_neg(lam_ref[...])) * r
        a = jnp.exp(log_a)
        s = jnp.sqrt(_neg_expm1(2.0 * log_a))
        xc_ref[...], r_ref[...], i_ref[...], a_ref[...], s_ref[...] = xc, r, i, a, s
        h_ref[...] = _scan_fwd(a, s * (i * xc), row)

    sp_ = _rnn_specs(B)
    return _call(
        body, (proj, cw, cb, wa, ba, wx, bx, lam), name="rnn_fwd", out_shape=(SDS((T, RNN_W), f32),) * 6,
        grid=(B, RNN_BLOCKS),
        in_specs=[sp_["act"](), sp_["convw"], sp_["vec"](), sp_["gate"](), sp_["vec"](), sp_["gate"](),
                  sp_["vec"](), sp_["vec"]()],
        out_specs=tuple(sp_["act"]() for _ in range(6)), semantics=("parallel", "parallel"), comm=comm)


def _rnn_bwd(proj, saved, dh, cw, wa, wx, lam, comm=None):
    T = proj.shape[0]
    B = T // SEQ

    def body(x_ref, h_ref, xc_ref, r_ref, i_ref, a_ref, s_ref, dh_ref, cw_ref, wa_ref, wx_ref, lam_ref,
             dx_ref, dcw_ref, dcb_ref, dwa_ref, dba_ref, dwx_ref, dbx_ref, dlam_ref):
        row = lax.broadcasted_iota(jnp.int32, (SEQ, LANES), 0)
        xr = x_ref[...]
        wa, wx, lam = wa_ref[...], wx_ref[...], lam_ref[...]
        xc, r, i, a, s = xc_ref[...], r_ref[...], i_ref[...], a_ref[...], s_ref[...]
        xcb = xc.astype(bf16)
        sp = _softplus_neg(lam)
        hprev = _shift_dn(h_ref[...], 1, row)
        g = _scan_bwd(_shift_up(a, 1, row), dh_ref[...].astype(f32), row)
        da = g * hprev
        ds = g * (i * xc)
        di = g * (s * xc)
        dxc = g * (s * i)
        dla = da * a - ds * (a * a) / s
        dr = dla * (-LRU_C * sp)
        dsp = jnp.sum(dla * (-LRU_C * r), axis=0, keepdims=True)
        dlam = -dsp * _sigmoid(-lam)
        dga = dr * r * (1.0 - r)
        dgx = di * i * (1.0 - i)
        dgab, dgxb = dga.astype(bf16), dgx.astype(bf16)
        dwa = lax.dot_general(xcb, dgab, _DIMS["tn"], preferred_element_type=f32)
        dwx = lax.dot_general(xcb, dgxb, _DIMS["tn"], preferred_element_type=f32)
        dxc = dxc + lax.dot_general(dgab, wa.astype(bf16), _DIMS["nt"], preferred_element_type=f32)
        dxc = dxc + lax.dot_general(dgxb, wx.astype(bf16), _DIMS["nt"], preferred_element_type=f32)
        dx, dws, db = _conv_bwd(xr, cw_ref, dxc, row)
        dx_ref[...] = dx.astype(bf16)
        first = pl.program_id(1) == 0

        def acc(ref, val):
            @pl.when(first)
            def _():
                ref[...] = val

            @pl.when(jnp.logical_not(first))
            def _():
                ref[...] += val

        for k in range(RNN_CONV):
            acc(dcw_ref.at[k:k + 1, :], dws[k])
        acc(dcb_ref, db)
        acc(dwa_ref, dwa)
        acc(dba_ref, jnp.sum(dga, axis=0, keepdims=True))
        acc(dwx_ref, dwx)
        acc(dbx_ref, jnp.sum(dgx, axis=0, keepdims=True))
        acc(dlam_ref, dlam)

    blk = lambda: pl.BlockSpec((SEQ, LANES), lambda n, b: (b, n))
    convw = lambda: pl.BlockSpec((RNN_CONV, LANES), lambda n, b: (0, n))
    vec = lambda: pl.BlockSpec((1, LANES), lambda n, b: (0, n))
    gate = lambda: pl.BlockSpec((None, LANES, LANES), lambda n, b: (n, 0, 0))
    vshape = SDS((1, RNN_W), f32)
    gshape = SDS((RNN_BLOCKS, LANES, LANES), f32)
    return _call(
        body, (proj, *saved, dh, cw, wa, wx, lam), name="rnn_bwd",
        out_shape=(SDS((T, RNN_W), bf16), SDS((RNN_CONV, RNN_W), f32), vshape, gshape, vshape, gshape, vshape, vshape),
        grid=(RNN_BLOCKS, B),
        in_specs=[blk() for _ in range(8)] + [convw(), gate(), gate(), vec()],
        out_specs=(blk(), convw(), vec(), gate(), vec(), gate(), vec(), vec()),
        semantics=("parallel", "arbitrary"), comm=comm)


def _t5_bucket(dist):
    max_exact = REL_BUCKETS // 2
    d = np.maximum(dist, 1).astype(np.float32)
    large = max_exact + np.log(d / max_exact) / math.log(REL_MAX_DIST / max_exact) * (REL_BUCKETS - max_exact)
    large = np.minimum(large.astype(np.int32), REL_BUCKETS - 1)
    return np.where(dist < max_exact, dist, large).astype(np.int32)


def _bucket_maps():
    qi = np.arange(ATT_BLK)[:, None]
    kj = np.arange(2 * ATT_BLK)[None, :]
    delta = ATT_BLK + qi - kj
    valid = (delta >= 0) & (delta <= ATT_BLK)
    maps = [np.where(valid, _t5_bucket(np.maximum(delta, 0) * r), -1) for r in DILATIONS]
    return np.stack(maps).astype(np.int32)


def _bias_tables(rel_bias, buckets):
    def body(rb_ref, bk_ref, o_ref):
        for g in range(NG):
            bk = bk_ref[g]
            for h in range(KVH):
                acc = jnp.full(bk.shape, NEG, f32)
                for b in range(REL_BUCKETS):
                    acc = jnp.where(bk == b, rb_ref[b, g * KVH + h], acc)
                o_ref[h, g] = acc

    return pl.pallas_call(
        body, name="bias_tables", out_shape=SDS((KVH, NG, ATT_BLK, 2 * ATT_BLK), f32),
        in_specs=[pl.BlockSpec(memory_space=pltpu.SMEM), pl.BlockSpec(memory_space=pltpu.VMEM)],
        out_specs=pl.BlockSpec(memory_space=pltpu.VMEM), compiler_params=_params())(rel_bias, buckets)


def _bias_grad(dbias, buckets):
    def body(db_ref, bk_ref, o_ref):
        rr = lax.broadcasted_iota(jnp.int32, (REL_BUCKETS, NG * KVH), 0)
        cc = lax.broadcasted_iota(jnp.int32, (REL_BUCKETS, NG * KVH), 1)
        out = jnp.zeros((REL_BUCKETS, NG * KVH), f32)
        for g in range(NG):
            bk = bk_ref[g]
            for h in range(KVH):
                d = db_ref[h, g]
                for b in range(REL_BUCKETS):
                    m = jnp.where(bk == b, d, 0.0)
                    s = jnp.sum(jnp.sum(m, axis=1, keepdims=True), axis=0, keepdims=True)
                    out = jnp.where((rr == b) & (cc == g * KVH + h), s, out)
        o_ref[...] = out

    return pl.pallas_call(body, name="bias_grad", out_shape=SDS((REL_BUCKETS, NG * KVH), f32),
                          compiler_params=_params())(dbias, buckets)


def _to_sub(dst_ref, src_ref, r, dtype, offset=0):
    M = SEQ // r
    for c in range(r):
        if r == 1:
            v = src_ref[...]
        else:
            v = src_ref[pl.ds(c, M, stride=r), :]
        dst_ref[pl.ds(offset + c * M, M), :] = v.astype(dtype)


def _from_sub(dst_ref, src_ref, r, accumulate=False, offset=0):
    M = SEQ // r
    for c in range(r):
        v = src_ref[pl.ds(offset + c * M, M), :]
        idx = slice(None) if r == 1 else pl.ds(c, M, stride=r)
        if accumulate:
            dst_ref[idx, :] = dst_ref[idx, :] + v
        else:
            dst_ref[idx, :] = v


_COL = lambda k: slice(k * HD, (k + 1) * HD)
SCALE = HD ** -0.5


def _qkv_spec(k, bh):
    def index(*ids):
        b, h = bh(*ids)
        return (b, C_ATT // HD + 5 * h + k)

    return pl.BlockSpec((SEQ, HD), index)


def _key_window(bias_ref, g, nb):
    if nb == 1:
        bias_own = bias_ref[g, :, ATT_BLK:2 * ATT_BLK]
        return lambda j: (pl.ds(pl.multiple_of((j + 1) * ATT_BLK, ATT_BLK), ATT_BLK), bias_own)
    bias_g = bias_ref[g]
    col = lax.broadcasted_iota(jnp.int32, bias_g.shape, 1)
    bias_first = jnp.where(col >= ATT_BLK, bias_g, NEG)
    return lambda j: (pl.ds(pl.multiple_of(j * ATT_BLK, ATT_BLK), 2 * ATT_BLK),
                      jnp.where(j % nb != 0, bias_g, bias_first))


def _att_fwd(proj, bias, comm=None):
    T = proj.shape[0]
    B = T // SEQ

    def body(q0_ref, q1_ref, q2_ref, k_ref, v_ref, bias_ref, o_ref, lse_ref, qp, kp, vp, kt, op, lp, og, lg):
        q_refs = (q0_ref, q1_ref, q2_ref)
        kp[0:ATT_BLK, :] = jnp.zeros((ATT_BLK, HD), bf16)
        vp[0:ATT_BLK, :] = jnp.zeros((ATT_BLK, HD), bf16)
        for g, r in enumerate(DILATIONS):
            nb = NBLK_SEQ // r
            _to_sub(qp, q_refs[g], r, bf16)
            _to_sub(kp, k_ref, r, bf16, offset=ATT_BLK)
            _to_sub(vp, v_ref, r, bf16, offset=ATT_BLK)
            kt[...] = kp[...].T
            keys = _key_window(bias_ref, g, nb)

            def step(j, carry):
                cur = pl.ds(pl.multiple_of(j * ATT_BLK, ATT_BLK), ATT_BLK)
                win, bias_j = keys(j)
                s = jnp.dot(qp[cur, :], kt[:, win], preferred_element_type=f32) * SCALE + bias_j
                m = jnp.max(s, axis=-1, keepdims=True)
                p = jnp.exp(s - m)
                den = jnp.sum(p, axis=-1, keepdims=True)
                o = jnp.dot(p.astype(bf16), vp[win, :], preferred_element_type=f32)
                op[cur, :] = o / den
                lp[cur, :] = jnp.broadcast_to(m + jnp.log(den), (ATT_BLK, HD))
                return carry

            lax.fori_loop(0, NBLK_SEQ, step, 0, unroll=ATT_UNROLL)
            _from_sub(og.at[g], op, r)
            _from_sub(lg.at[g], lp, r)
        l0, l1, l2 = lg[0], lg[1], lg[2]
        mx = jnp.maximum(jnp.maximum(l0, l1), l2)
        e0, e1, e2 = jnp.exp(l0 - mx), jnp.exp(l1 - mx), jnp.exp(l2 - mx)
        den = e0 + e1 + e2
        o_ref[...] = (e0 * og[0] + e1 * og[1] + e2 * og[2]) / den
        lse_ref[...] = mx + jnp.log(den)

    return _call(
        body, (proj, proj, proj, proj, proj, bias), name="att_fwd",
        out_shape=(SDS((T, KVH * HD), f32), SDS((KVH, T, HD), f32)), grid=(B, KVH),
        in_specs=[_qkv_spec(k, lambda b, h: (b, h)) for k in range(5)]
                 + [pl.BlockSpec((None, NG, ATT_BLK, 2 * ATT_BLK), lambda b, h: (h, 0, 0, 0))],
        out_specs=(pl.BlockSpec((SEQ, HD), lambda b, h: (b, h)),
                   pl.BlockSpec((None, SEQ, HD), lambda b, h: (h, b, 0))),
        scratch_shapes=[pltpu.VMEM((SEQ, HD), bf16)] + [pltpu.VMEM((SEQ + ATT_BLK, HD), bf16)] * 2
                       + [pltpu.VMEM((HD, SEQ + ATT_BLK), bf16)]
                       + [pltpu.VMEM((SEQ, HD), f32)] * 2 + [pltpu.VMEM((NG, SEQ, HD), f32)] * 2,
        semantics=("parallel", "parallel"), comm=comm)


def _att_bwd(proj, bias, o, lse, do, comm=None):
    T = proj.shape[0]
    B = T // SEQ

    def body(q0_ref, q1_ref, q2_ref, k_ref, v_ref, bias_ref, o_ref, lse_ref, do_ref, dx_ref, db_ref,
             qp, kp, vp, dop, qt, kt, vt, dot, lp, dlp, dqp, dkt, dvt, dln, nat, dkn, dvn):
        q_refs = (q0_ref, q1_ref, q2_ref)
        first = pl.program_id(1) == 0

        @pl.when(first)
        def _():
            db_ref[...] = jnp.zeros_like(db_ref)

        dln[...] = jnp.broadcast_to(jnp.sum(do_ref[...] * o_ref[...], axis=-1, keepdims=True), (SEQ, HD))
        dkn[...] = jnp.zeros_like(dkn)
        dvn[...] = jnp.zeros_like(dvn)
        kp[0:ATT_BLK, :] = jnp.zeros((ATT_BLK, HD), bf16)
        vp[0:ATT_BLK, :] = jnp.zeros((ATT_BLK, HD), bf16)
        for g, r in enumerate(DILATIONS):
            nb = NBLK_SEQ // r
            _to_sub(qp, q_refs[g], r, bf16)
            _to_sub(kp, k_ref, r, bf16, offset=ATT_BLK)
            _to_sub(vp, v_ref, r, bf16, offset=ATT_BLK)
            _to_sub(dop, do_ref, r, bf16)
            _to_sub(lp, lse_ref, r, f32)
            _to_sub(dlp, dln, r, f32)
            qt[...], kt[...], vt[...], dot[...] = qp[...].T, kp[...].T, vp[...].T, dop[...].T
            dkt[...] = jnp.zeros_like(dkt)
            dvt[...] = jnp.zeros_like(dvt)
            keys = _key_window(bias_ref, g, nb)
            db_cols = slice(ATT_BLK, 2 * ATT_BLK) if nb == 1 else slice(None)

            def step(j, carry):
                cur = pl.ds(pl.multiple_of(j * ATT_BLK, ATT_BLK), ATT_BLK)
                win, bias_j = keys(j)
                s = jnp.dot(qp[cur, :], kt[:, win], preferred_element_type=f32) * SCALE + bias_j
                p = jnp.exp(s - lp[cur, 0:1])
                dp = jnp.dot(dop[cur, :], vt[:, win], preferred_element_type=f32)
                ds = p * (dp - dlp[cur, 0:1])
                db_ref[g, :, db_cols] += ds
                dsb, pb = ds.astype(bf16), p.astype(bf16)
                dqp[cur, :] = jnp.dot(dsb, kp[win, :], preferred_element_type=f32) * SCALE
                dkt[:, win] += jnp.dot(qt[:, cur], dsb, preferred_element_type=f32) * SCALE
                dvt[:, win] += jnp.dot(dot[:, cur], pb, preferred_element_type=f32)
                return carry

            lax.fori_loop(0, NBLK_SEQ, step, 0, unroll=ATT_UNROLL)
            _from_sub(nat, dqp, r)
            dx_ref[:, _COL(g)] = nat[...].astype(bf16)
            dqp[...] = dkt[:, ATT_BLK:].T
            _from_sub(dkn, dqp, r, accumulate=True)
            dqp[...] = dvt[:, ATT_BLK:].T
            _from_sub(dvn, dqp, r, accumulate=True)
        dx_ref[:, _COL(3)] = dkn[...].astype(bf16)
        dx_ref[:, _COL(4)] = dvn[...].astype(bf16)

    blk = lambda: pl.BlockSpec((SEQ, HD), lambda h, b: (b, h))
    bias_spec = lambda: pl.BlockSpec((None, NG, ATT_BLK, 2 * ATT_BLK), lambda h, b: (h, 0, 0, 0))
    pad = lambda dtype: pltpu.VMEM((SEQ + ATT_BLK, HD), dtype)
    pad_t = lambda dtype: pltpu.VMEM((HD, SEQ + ATT_BLK), dtype)
    seq_t = pltpu.VMEM((HD, SEQ), bf16)
    return _call(
        body, (proj, proj, proj, proj, proj, bias, o, lse, do), name="att_bwd",
        out_shape=(SDS((T, KVH * ATT_COLS), bf16), SDS((KVH, NG, ATT_BLK, 2 * ATT_BLK), f32)), grid=(KVH, B),
        in_specs=[_qkv_spec(k, lambda h, b: (b, h)) for k in range(5)]
                 + [bias_spec(), blk(), pl.BlockSpec((None, SEQ, HD), lambda h, b: (h, b, 0)), blk()],
        out_specs=(pl.BlockSpec((SEQ, ATT_COLS), lambda h, b: (b, h)), bias_spec()),
        scratch_shapes=[pltpu.VMEM((SEQ, HD), bf16), pad(bf16), pad(bf16), pltpu.VMEM((SEQ, HD), bf16),
                        seq_t, pad_t(bf16), pad_t(bf16), seq_t]
                       + [pltpu.VMEM((SEQ, HD), f32)] * 3 + [pad_t(f32)] * 2 + [pltpu.VMEM((SEQ, HD), f32)] * 4,
        semantics=("parallel", "arbitrary"), comm=comm)


MERGE_ROWS, MERGE_COLS = 1024, 256


def _merge_fwd(gates, pr, pa):
    def body(gr_ref, ga_ref, pr_ref, pa_ref, o_ref):
        o_ref[...] = (_sigmoid(gr_ref[...].astype(f32)) * pr_ref[...].astype(f32)
                      + _sigmoid(ga_ref[...].astype(f32)) * pa_ref[...].astype(f32)).astype(bf16)

    T = gates.shape[0]
    cols = lambda off: pl.BlockSpec((MERGE_ROWS, MERGE_COLS), lambda i, j: (i, off + j))
    return pl.pallas_call(body, name="merge_fwd", out_shape=SDS((T, D), bf16),
                          grid=(T // MERGE_ROWS, D // MERGE_COLS),
                          in_specs=[cols(0), cols(D // MERGE_COLS), cols(0), cols(0)], out_specs=cols(0),
                          compiler_params=_params(("parallel", "parallel")))(gates, gates, pr, pa)


def _merge_bwd(gates, pr, pa, dm):
    nj = D // MERGE_COLS

    def body(g_ref, pr_ref, pa_ref, dm_ref, dp_ref, dg_ref):
        dm_ = dm_ref[...].astype(f32)
        s = _sigmoid(g_ref[...].astype(f32))
        p = jnp.where(pl.program_id(1) < nj, pr_ref[...], pa_ref[...]).astype(f32)
        dp_ref[...] = (dm_ * s).astype(bf16)
        dg_ref[...] = (dm_ * p * s * (1.0 - s)).astype(bf16)

    T = gates.shape[0]
    blk = (MERGE_ROWS, MERGE_COLS)
    wrap = pl.BlockSpec(blk, lambda i, j: (i, j % nj))
    pr_spec = pl.BlockSpec(blk, lambda i, j: (i, jnp.minimum(j, nj - 1)))
    pa_spec = pl.BlockSpec(blk, lambda i, j: (i, jnp.maximum(j - nj, 0)))
    out = pl.BlockSpec(blk, lambda i, j: (i, j))
    return pl.pallas_call(
        body, name="merge_bwd", out_shape=(SDS((T, 2 * D), bf16), SDS((T, 2 * D), bf16)),
        grid=(T // MERGE_ROWS, 2 * nj),
        in_specs=[out, pr_spec, pa_spec, wrap], out_specs=(out, out),
        compiler_params=_params(("parallel", "parallel")))(gates, pr, pa, dm)


FFN_COLS = 256
GELU_C = math.sqrt(2.0 / math.pi)
GELU_A = 0.044715


def _gelu_parts(x):
    t = jnp.tanh(GELU_C * (x + GELU_A * x * x * x))
    return 0.5 * x * (1.0 + t), t


def _ffn_act_fwd(gpre, up, cw, cb):
    def body(g_ref, u_ref, cw_ref, cb_ref, o_ref):
        row = lax.broadcasted_iota(jnp.int32, (SEQ, FFN_COLS), 0)
        gate = _conv_fwd(g_ref[...].astype(f32), cw_ref, cb_ref[...], row)
        o_ref[...] = (_gelu_parts(gate)[0] * u_ref[...].astype(f32)).astype(bf16)

    T = gpre.shape[0]
    blk = lambda: pl.BlockSpec((SEQ, FFN_COLS), lambda b, j: (b, j))
    return pl.pallas_call(
        body, name="ffn_act_fwd", out_shape=SDS((T, FFN_W), bf16), grid=(T // SEQ, FFN_W // FFN_COLS),
        in_specs=[blk(), blk(), pl.BlockSpec((FFN_CONV, FFN_COLS), lambda b, j: (0, j)),
                  pl.BlockSpec((1, FFN_COLS), lambda b, j: (0, j))],
        out_specs=blk(), compiler_params=_params(("parallel", "parallel")))(gpre, up, cw, cb)


def _ffn_act_bwd(gpre, up, cw, cb, dact):
    def body(g_ref, u_ref, cw_ref, cb_ref, da_ref, dg_ref, du_ref, dcw_ref, dcb_ref):
        row = lax.broadcasted_iota(jnp.int32, (SEQ, FFN_COLS), 0)
        gp = g_ref[...].astype(f32)
        gate = _conv_fwd(gp, cw_ref, cb_ref[...], row)
        gel, t = _gelu_parts(gate)
        da = da_ref[...].astype(f32)
        du_ref[...] = (da * gel).astype(bf16)
        dgel = 0.5 * (1.0 + t) + 0.5 * gate * (1.0 - t * t) * (GELU_C * (1.0 + 3.0 * GELU_A * gate * gate))
        dgate = da * u_ref[...].astype(f32) * dgel
        dx, dws, db = _conv_bwd(gp, cw_ref, dgate, row)
        dg_ref[...] = dx.astype(bf16)
        first = pl.program_id(1) == 0

        def acc(ref, val):
            @pl.when(first)
            def _():
                ref[...] = val

            @pl.when(jnp.logical_not(first))
            def _():
                ref[...] += val

        for k in range(FFN_CONV):
            acc(dcw_ref.at[k:k + 1, :], dws[k])
        acc(dcb_ref, db)

    T = gpre.shape[0]
    blk = lambda: pl.BlockSpec((SEQ, FFN_COLS), lambda j, b: (b, j))
    cws = lambda: pl.BlockSpec((FFN_CONV, FFN_COLS), lambda j, b: (0, j))
    cbs = lambda: pl.BlockSpec((1, FFN_COLS), lambda j, b: (0, j))
    return pl.pallas_call(
        body, name="ffn_act_bwd",
        out_shape=(SDS((T, FFN_W), bf16), SDS((T, FFN_W), bf16), SDS((FFN_CONV, FFN_W), f32), SDS((1, FFN_W), f32)),
        grid=(FFN_W // FFN_COLS, T // SEQ),
        in_specs=[blk(), blk(), cws(), cbs(), blk()], out_specs=(blk(), blk(), cws(), cbs()),
        compiler_params=_params(("parallel", "arbitrary")))(gpre, up, cw, cb, dact)


def _coords():
    return lax.axis_index("x"), lax.axis_index("y"), lax.axis_index("c")


def _dev_index(dev):
    return 4 * dev[0] + 2 * dev[1] + dev[2]


def _dma_sems(n):
    return [pltpu.SemaphoreType.DMA((n,)), pltpu.SemaphoreType.DMA((n,))]


def _gather_two_level(arrays):
    n = len(arrays)

    def plan(ins, outs, sems):
        send_sems, recv_sems, local_sems = sems
        x, y, c = _coords()
        me, sibling = (x, y, c), (x, y, 1 - c)
        chips = [(1 - x, y), (x, 1 - y), (1 - x, 1 - y)]

        def copy(a, k, block, to, own=False):
            dst = outs[a].at[_dev_index(block)]
            return pltpu.make_async_remote_copy(
                src_ref=ins[a] if own else dst, dst_ref=dst, send_sem=send_sems.at[7 * a + k],
                recv_sem=recv_sems.at[7 * a + k], device_id=to, device_id_type=MESH)

        mine = [pltpu.make_async_copy(ins[a], outs[a].at[_dev_index(me)], local_sems.at[a]) for a in range(n)]
        first = [copy(a, 0, me, sibling, own=True) for a in range(n)]
        first += [copy(a, 1 + j, me, (*chip, c), own=True) for a in range(n) for j, chip in enumerate(chips)]
        passed = [[copy(a, 4 + j, (*chip, c), sibling) for a in range(n)] for j, chip in enumerate(chips)]
        arrive_ici = [[copy(a, 1 + j, (*chip, c), me) for a in range(n)] for j, chip in enumerate(chips)]
        arrive_d2d = [copy(a, 0, sibling, me) for a in range(n)]
        arrive_d2d += [copy(a, 4 + j, (*chip, 1 - c), me) for a in range(n) for j, chip in enumerate(chips)]
        return mine, first, passed, arrive_ici, arrive_d2d

    def start(ins, outs, sems):
        mine, first, _, _, _ = plan(ins, outs, sems)
        for cp in mine + first:
            cp.start()

    def finish(ins, outs, sems):
        mine, first, passed, arrive_ici, arrive_d2d = plan(ins, outs, sems)
        for j in range(3):
            for cp in arrive_ici[j]:
                cp.wait_recv()
            for cp in passed[j]:
                cp.start()
        for cp in arrive_d2d:
            cp.wait_recv()
        for cp in first + [cp for group in passed for cp in group]:
            cp.wait_send()
        for cp in mine:
            cp.wait()

    return _Comm(arrays, [SDS((N_DEV,) + a.shape, a.dtype) for a in arrays],
                 _dma_sems(7 * n) + [pltpu.SemaphoreType.DMA((n,))], start, finish)


def _gather_direct(arrays):
    n = len(arrays)

    def plan(ins, outs, sems):
        send_sems, recv_sems, local_sems = sems
        x, y, c = _coords()
        me = (x, y, c)
        mine = [pltpu.make_async_copy(ins[a], outs[a].at[_dev_index(me)], local_sems.at[a]) for a in range(n)]
        sends, arrivals = [], []
        for a in range(n):
            for k in range(1, N_DEV):
                peer = (1 - x if k & 4 else x, 1 - y if k & 2 else y, 1 - c if k & 1 else c)
                s = 7 * a + k - 1
                for slot, out in ((me, sends), (peer, arrivals)):
                    out.append(pltpu.make_async_remote_copy(
                        src_ref=ins[a], dst_ref=outs[a].at[_dev_index(slot)], send_sem=send_sems.at[s],
                        recv_sem=recv_sems.at[s], device_id=peer, device_id_type=MESH))
        return mine, sends, arrivals

    def start(ins, outs, sems):
        mine, sends, _ = plan(ins, outs, sems)
        for cp in mine + sends:
            cp.start()

    def finish(ins, outs, sems):
        mine, sends, arrivals = plan(ins, outs, sems)
        for cp in arrivals:
            cp.wait_recv()
        for cp in sends:
            cp.wait_send()
        for cp in mine:
            cp.wait()

    return _Comm(arrays, [SDS((N_DEV,) + a.shape, a.dtype) for a in arrays],
                 _dma_sems(7 * n) + [pltpu.SemaphoreType.DMA((n,))], start, finish)


def _scatter_direct(arrays):
    n = len(arrays)

    def plan(ins, outs, sems):
        send_sems, recv_sems = sems
        x, y, c = _coords()
        cps = []
        for a in range(n):
            for k in range(1, N_DEV):
                peer = (1 - x if k & 4 else x, 1 - y if k & 2 else y, 1 - c if k & 1 else c)
                s = 7 * a + k - 1
                cps.append(pltpu.make_async_remote_copy(
                    src_ref=ins[a].at[_dev_index(peer)], dst_ref=outs[a].at[k - 1], send_sem=send_sems.at[s],
                    recv_sem=recv_sems.at[s], device_id=peer, device_id_type=MESH))
        return cps

    def start(ins, outs, sems):
        for cp in plan(ins, outs, sems):
            cp.start()

    def finish(ins, outs, sems):
        for cp in plan(ins, outs, sems):
            cp.wait()

    return _Comm(arrays, [SDS((N_DEV - 1,) + a.shape[1:], a.dtype) for a in arrays], _dma_sems(7 * n),
                 start, finish)


def _run(comm, name):
    def body(*refs):
        k_in, k_out = len(comm.inputs), len(comm.out_shapes)
        ins, outs, sems = refs[:k_in], refs[k_in:k_in + k_out], refs[k_in + k_out:]
        comm.start(ins, outs, sems)
        comm.finish(ins, outs, sems)

    return pl.pallas_call(body, name=name, out_shape=comm.out_shapes, in_specs=[ANY] * len(comm.inputs),
                          out_specs=(ANY,) * len(comm.out_shapes), scratch_shapes=comm.sem_shapes)(*comm.inputs)


TILE_ELEMS = 192 * 1024


def _row_tile(R, C):
    if R * C <= TILE_ELEMS:
        return R
    return max(t for t in range(SUBLANES, R, SUBLANES) if R % t == 0 and t * C <= TILE_ELEMS)


def _adamw_math(w, g, m, v):
    m = ADAM_B1 * m + (1.0 - ADAM_B1) * g
    v = ADAM_B2 * v + (1.0 - ADAM_B2) * (g * g)
    m_hat = m / (1.0 - ADAM_B1 ** ADAM_STEP)
    v_hat = v / (1.0 - ADAM_B2 ** ADAM_STEP)
    delta = -ADAM_LR * (m_hat / (jnp.sqrt(v_hat) + ADAM_EPS) + ADAM_WD * w)
    return delta, m, v


def _adamw_sharded(own, recv, d_idx, w, m, v, name):
    R, C = w.shape
    t = _row_tile(R, C)

    def body(k_ref, p_ref, r_ref, w_ref, m_ref, v_ref, g_ref, d_ref, nm_ref, nv_ref):
        g = p_ref[...].astype(f32)
        for j in range(N_DEV - 1):
            g = g + r_ref[j].astype(f32)
        d, nm, nv = _adamw_math(w_ref[...], g, m_ref[...], v_ref[...])
        g_ref[...], d_ref[...], nm_ref[...], nv_ref[...] = g, d, nm, nv

    tile = lambda: pl.BlockSpec((t, C), lambda i, k: (i, 0))
    return pl.pallas_call(
        body, name="adamw_" + name, out_shape=(SDS((R, C), f32),) * 4,
        grid_spec=pltpu.PrefetchScalarGridSpec(
            num_scalar_prefetch=1, grid=(R // t,),
            in_specs=[pl.BlockSpec((None, t, C), lambda i, k: (k[0], i, 0)),
                      pl.BlockSpec((N_DEV - 1, t, C), lambda i, k: (0, i, 0)), tile(), tile(), tile()],
            out_specs=(tile(), tile(), tile(), tile())),
        compiler_params=_params(("parallel",)))(d_idx, own, recv, w, m, v)


def _adamw_replicated(parts, ws, ms, vs):
    n = len(ws)

    def body(*refs):
        p, w, m, v = (refs[i * n:(i + 1) * n] for i in range(4))
        outs = refs[4 * n:]
        for a in range(n):
            g = p[a][0].astype(f32)
            for j in range(1, N_DEV):
                g = g + p[a][j].astype(f32)
            d, nm, nv = _adamw_math(w[a][...], g, m[a][...], v[a][...])
            for i, val in enumerate((g, d, nm, nv)):
                outs[i * n + a][...] = val

    shapes = tuple(SDS(w.shape, f32) for w in ws)
    res = pl.pallas_call(body, name="adamw_replicated", out_shape=shapes * 4,
                         compiler_params=_params())(*parts, *ws, *ms, *vs)
    return [res[i * n:(i + 1) * n] for i in range(4)]


def _cols_to_full(g):
    n, r, c = g.shape
    return g.transpose(1, 0, 2).reshape(r, n * c)


def _full_to_cols(a):
    r, c = a.shape
    return a.reshape(r, N_DEV, c // N_DEV).transpose(1, 0, 2)


def _rows_blocked(a):
    r, c = a.shape
    return a.reshape(N_DEV, r // N_DEV, c)


def _w_in_to_internal(w):
    K = w.shape[0]
    q = w[:, 1280:2816].reshape(K, NG, KVH, 1, HD).transpose(0, 2, 1, 3, 4).reshape(K, KVH, NG, HD)
    k = w[:, 2816:3328].reshape(K, KVH, 1, HD)
    v = w[:, 3328:3840].reshape(K, KVH, 1, HD)
    att = jnp.concatenate([q, k, v], axis=2).reshape(K, KVH * ATT_COLS)
    return jnp.concatenate([w[:, :1280], att, w[:, 3840:]], axis=1)


def _w_in_from_internal(w):
    K = w.shape[0]
    att = w[:, C_ATT:C_GATE].reshape(K, KVH, 5, HD)
    q = att[:, :, 0:3].transpose(0, 2, 1, 3).reshape(K, NG * KVH * HD)
    k = att[:, :, 3].reshape(K, KVH * HD)
    v = att[:, :, 4].reshape(K, KVH * HD)
    return jnp.concatenate([w[:, :C_ATT], q, k, v, w[:, C_GATE:]], axis=1)


_IN_NAMES = ('x', 'rel_bias', 'norm_mix_pre', 'norm_mix_post', 'w_in', 'conv_rnn_w', 'conv_rnn_b', 'w_rg_a', 'b_rg_a',
             'w_rg_x', 'b_rg_x', 'lru_lambda', 'w_branch_rnn', 'w_branch_att', 'w_out', 'norm_ffn_pre',
             'norm_ffn_post', 'w_ffn_gate', 'w_ffn_up', 'conv_ffn_w', 'conv_ffn_b', 'w_ffn_down')
_WEIGHTS = _IN_NAMES[1:]
_SHARDED = {"w_in": "col", "conv_rnn_w": "col", "w_branch_rnn": "row", "w_branch_att": "col", "w_out": "row",
            "w_ffn_gate": "col", "w_ffn_up": "col", "conv_ffn_w": "col", "w_ffn_down": "row"}
_REPLICATED = tuple(n for n in _WEIGHTS if n not in _SHARDED)


def _flat2(a):
    return a.reshape(-1, a.shape[-1])


def _train_step(inp):
    x_idx, y_idx, c_idx = _coords()
    W = {n: inp[n] for n in _WEIGHTS}
    x = inp["x"].reshape(-1, D)
    target = inp["loss_target"].reshape(-1, D)
    shard = {n: inp[n][0] for n in _SHARDED}

    hn, (g_in, g_cr, g_cf) = _norm_in(x, W["norm_mix_pre"], comm=_gather_two_level(
        [shard["w_in"].astype(bf16), shard["conv_rnn_w"], shard["conv_ffn_w"]]))
    w_in = _w_in_to_internal(_cols_to_full(g_in))
    cw_rnn, cw_ffn = _cols_to_full(g_cr), _cols_to_full(g_cf)
    behind_rnn = ("w_branch_rnn", "w_branch_att", "w_out", "w_ffn_down")
    behind_att = ("w_ffn_gate", "w_ffn_up")

    wa, wx = W["w_rg_a"][0], W["w_rg_x"][0]
    buckets = jnp.asarray(_bucket_maps())

    proj = _mm(hn, w_in[:, :C_GATE], "nn", f32, "mm_proj", 512, C_GATE // 2, 1024, cols_outer=True)
    gates = _mm(hn, w_in[:, C_GATE:], "nn", bf16, "mm_gates", 1024, 1024, 1024, cols_outer=True)
    rnn_saved, got = _rnn_fwd(proj, cw_rnn, W["conv_rnn_b"], wa, W["b_rg_a"], wx, W["b_rg_x"], W["lru_lambda"],
                              comm=_gather_direct([shard[n].astype(bf16) for n in behind_rnn]))
    h_rnn = rnn_saved[0]
    gathered = dict(zip(behind_rnn, got))
    bias = _bias_tables(W["rel_bias"], buckets)
    (o_att, lse), got = _att_fwd(proj, bias, comm=_gather_direct([shard[n].astype(bf16) for n in behind_att]))
    gathered.update(zip(behind_att, got))
    w_brnn = gathered["w_branch_rnn"].reshape(RNN_W, D)
    w_batt = _cols_to_full(gathered["w_branch_att"])
    w_out = gathered["w_out"].reshape(D, D)
    w_gate, w_up = _cols_to_full(gathered["w_ffn_gate"]), _cols_to_full(gathered["w_ffn_up"])
    w_down = gathered["w_ffn_down"].reshape(FFN_W, D)
    pr = _mm(h_rnn, w_brnn, "nn", bf16, "mm_pr", 1024, 1024, 1280)
    pa = _mm(o_att, w_batt, "nn", bf16, "mm_pa", 1024, 1024, 512)
    merged = _merge_fwd(gates, pr, pa)
    (mix, h1, hn2), _ = _mm_rows([(merged, w_out, 1024)], "nn", "mm_mix", _mid_fwd_rows, [x],
                                 [W["norm_mix_post"], W["norm_ffn_pre"]], [f32, f32, bf16], [])
    gpre = _mm(hn2, w_gate, "nn", bf16, "mm_gate", 1024, 1024, 1024, cols_outer=True)
    up = _mm(hn2, w_up, "nn", bf16, "mm_up", 1024, 1024, 1024, cols_outer=True)
    act = _ffn_act_fwd(gpre, up, cw_ffn, W["conv_ffn_b"])
    (dy, dff), (loss_part, dg_fpost) = _mm_rows([(act, w_down, 1024)], "nn", "mm_down", _final_rows, [h1, target],
                                                [W["norm_ffn_post"]], [f32, bf16], [1, D])

    grads = {}
    dact = _mm(dff, w_down, "nt", bf16, "mm_dact", 1024, 1024, 1024, cols_outer=True)
    grads["w_ffn_down"] = _rows_blocked(_mm(act, dff, "tn", bf16, "mm_dw_down", 1024, 1024, 2048))
    dgpre, dup, dcw_ffn, dcb_ffn = _ffn_act_bwd(gpre, up, cw_ffn, W["conv_ffn_b"], dact)
    grads["conv_ffn_w"] = _full_to_cols(dcw_ffn.astype(bf16))
    grads["w_ffn_gate"] = _full_to_cols(_mm(hn2, dgpre, "tn", bf16, "mm_dw_gate", 1024, 1024, 2048))
    grads["w_ffn_up"] = _full_to_cols(_mm(hn2, dup, "tn", bf16, "mm_dw_up", 1024, 1024, 2048))
    (dh1, dmix), (dg_fpre, dg_post) = _mm_rows(
        [(dgpre, w_gate, 1024), (dup, w_up, 1024)], "nt", "mm_dhn2", _mid_bwd_rows, [dy, h1, mix],
        [W["norm_ffn_pre"], W["norm_mix_post"]], [f32, bf16], [D, D])
    dmerged = _mm(dmix, w_out, "nt", bf16, "mm_dmerged", 1024, 1024, 1024)
    grads["w_out"] = _rows_blocked(_mm(merged, dmix, "tn", bf16, "mm_dw_out", 1024, 1024, 2048))
    dprpa, dgates = _merge_bwd(gates, pr, pa, dmerged)
    dpr, dpa = dprpa[:, :D], dprpa[:, D:]
    dh_rnn = _mm(dpr, w_brnn, "nt", bf16, "mm_dh_rnn", 1024, 1280, 1024)
    grads["w_branch_rnn"] = _rows_blocked(_mm(h_rnn, dpr, "tn", bf16, "mm_dw_brnn", 1280, 1024, 1024))
    do_att = _mm(dpa, w_batt, "nt", f32, "mm_do_att", 1024, 512, 1024)
    grads["w_branch_att"] = _full_to_cols(_mm(o_att, dpa, "tn", bf16, "mm_dw_batt", 512, 1024, 2048))

    received = {}
    behind_att_bwd = ("w_ffn_down", "w_ffn_gate", "conv_ffn_w", "w_out")
    behind_rnn_bwd = ("w_ffn_up", "w_branch_rnn", "w_branch_att")
    (dqkv, dbias), got = _att_bwd(proj, bias, o_att, lse, do_att,
                                  comm=_scatter_direct([grads[n] for n in behind_att_bwd]))
    received.update(zip(behind_att_bwd, got))
    drel = _bias_grad(dbias, buckets)
    (dxr, dcw_rnn, dcb_rnn, dwa, dba, dwx, dbx, dlam), got = _rnn_bwd(
        proj, rnn_saved, dh_rnn, cw_rnn, wa, wx, W["lru_lambda"],
        comm=_scatter_direct([grads[n] for n in behind_rnn_bwd]))
    received.update(zip(behind_rnn_bwd, got))
    gsmall = {"rel_bias": drel, "norm_mix_post": dg_post, "conv_rnn_b": dcb_rnn, "w_rg_a": dwa.astype(bf16),
              "b_rg_a": dba, "w_rg_x": dwx.astype(bf16), "b_rg_x": dbx, "lru_lambda": dlam,
              "norm_ffn_pre": dg_fpre, "norm_ffn_post": dg_fpost, "conv_ffn_b": dcb_ffn}
    dw_in_a, parts = _mm(hn, dqkv, "tn", bf16, "mm_dw_in_a", 1024, 1280, 1024,
                         comm=_gather_direct([_flat2(gsmall[n]) for n in gsmall]))
    parts = dict(zip(gsmall, parts))
    dw_in = jnp.concatenate([_mm(hn, dxr, "tn", bf16, "mm_dw_in_r", 1024, 1280, 1024), dw_in_a,
                             _mm(hn, dgates, "tn", bf16, "mm_dw_in_g", 1024, 1024, 2048)], axis=1)
    grads["w_in"] = _full_to_cols(_w_in_from_internal(dw_in))
    grads["conv_rnn_w"] = _full_to_cols(dcw_rnn.astype(bf16))
    behind_dhn = ("w_in", "conv_rnn_w")
    ((dx,), (dg_pre,)), got = _mm_rows(
        [(dxr, w_in[:, :C_ATT], 1280), (dqkv, w_in[:, C_ATT:C_GATE], 1280), (dgates, w_in[:, C_GATE:], 1024)],
        "nt", "mm_dhn", _in_bwd_rows, [dh1, x], [W["norm_mix_pre"]], [f32], [D],
        comm=_scatter_direct([grads[n] for n in behind_dhn]))
    received.update(zip(behind_dhn, got))
    parts["norm_mix_pre"], = _run(_gather_two_level([dg_pre]), "ag_norm_mix_pre")
    parts = [parts[n] for n in _REPLICATED]

    out = {}
    d_arr = jnp.reshape(4 * x_idx + 2 * y_idx + c_idx, (1,)).astype(jnp.int32)
    for n in _SHARDED:
        res = _adamw_sharded(grads[n], received[n], d_arr, shard[n], inp["m_" + n][0], inp["v_" + n][0], n)
        out[n] = [r[None] for r in res]
    small = _adamw_replicated(parts, *[[_flat2(inp[p + n]) for n in _REPLICATED] for p in ("", "m_", "v_")])
    for a, n in enumerate(_REPLICATED):
        out[n] = [small[i][a].reshape(inp[n].shape) for i in range(4)]

    loss = lax.psum(loss_part[0, 0], ("x", "y", "c"))
    outs = [loss, dx.reshape(inp["x"].shape)]
    for i in range(4):
        outs.extend(out[n][i] for n in _WEIGHTS)
    return tuple(outs)


def kernel(x, rel_bias, norm_mix_pre, norm_mix_post, w_in, conv_rnn_w, conv_rnn_b, w_rg_a, b_rg_a, w_rg_x, b_rg_x, lru_lambda, w_branch_rnn, w_branch_att, w_out, norm_ffn_pre, norm_ffn_post, w_ffn_gate, w_ffn_up, conv_ffn_w, conv_ffn_b, w_ffn_down, loss_target, m_rel_bias, m_norm_mix_pre, m_norm_mix_post, m_w_in, m_conv_rnn_w, m_conv_rnn_b, m_w_rg_a, m_b_rg_a, m_w_rg_x, m_b_rg_x, m_lru_lambda, m_w_branch_rnn, m_w_branch_att, m_w_out, m_norm_ffn_pre, m_norm_ffn_post, m_w_ffn_gate, m_w_ffn_up, m_conv_ffn_w, m_conv_ffn_b, m_w_ffn_down, v_rel_bias, v_norm_mix_pre, v_norm_mix_post, v_w_in, v_conv_rnn_w, v_conv_rnn_b, v_w_rg_a, v_b_rg_a, v_w_rg_x, v_b_rg_x, v_lru_lambda, v_w_branch_rnn, v_w_branch_att, v_w_out, v_norm_ffn_pre, v_norm_ffn_post, v_w_ffn_gate, v_w_ffn_up, v_conv_ffn_w, v_conv_ffn_b, v_w_ffn_down):
    vals = locals()
    names = list(_IN_NAMES) + ["loss_target"] + ["m_" + n for n in _WEIGHTS] + ["v_" + n for n in _WEIGHTS]
    return _train_step({n: vals[n] for n in names})
```

```python
import functools
import math

import numpy as np
import jax
import jax.numpy as jnp
from jax import lax
from jax.experimental import pallas as pl
from jax.experimental.pallas import tpu as pltpu

f32, bf16 = jnp.float32, jnp.bfloat16
SDS = jax.ShapeDtypeStruct
MESH = pl.DeviceIdType.MESH
ANY = pl.BlockSpec(memory_space=pl.ANY)

D = 1024
SEQ = 2048
RNN_W = 1280
RNN_BLOCKS = 10
LANES = 128
SUBLANES = 8
RNN_CONV = 4
LRU_C = 8.0
HD = 128
KVH = 4
DILATIONS = (1, 4, 16)
NG = 3
ATT_BLK = 128
NBLK_SEQ = SEQ // ATT_BLK
STAT_LANE = 64
ATT_UNROLL = 8
REL_BUCKETS = 32
REL_MAX_DIST = 2048
FFN_W = 3072
FFN_CONV = 3
EPS = 1e-6
IN_W = 5888
ATT_COLS = 5 * HD
C_ATT = RNN_W
C_GATE = RNN_W + KVH * ATT_COLS
NEG = -1e30

ADAM_LR, ADAM_B1, ADAM_B2, ADAM_EPS, ADAM_WD, ADAM_STEP = 0.001, 0.9, 0.999, 1e-08, 0.01, 10

VMEM_LIMIT_BYTES = 56 * 1024 * 1024
N_DEV = 8


def _params(sem=None):
    return pltpu.CompilerParams(dimension_semantics=sem, vmem_limit_bytes=VMEM_LIMIT_BYTES)


def _sigmoid(x):
    return 1.0 / (1.0 + jnp.exp(-x))


class _Comm:
    def __init__(self, inputs, out_shapes, sem_shapes, start, finish):
        self.inputs, self.out_shapes, self.sem_shapes = tuple(inputs), tuple(out_shapes), list(sem_shapes)
        self.start, self.finish = start, finish


def _call(body, args, *, name, grid, in_specs, out_specs, out_shape, scratch_shapes=(), semantics, comm=None):
    if comm is None:
        return pl.pallas_call(body, name=name, grid=grid, in_specs=list(in_specs), out_specs=tuple(out_specs),
                              out_shape=tuple(out_shape), scratch_shapes=list(scratch_shapes),
                              compiler_params=_params(semantics))(*args), ()
    n_in, n_out, n_scr = len(in_specs), len(out_shape), len(scratch_shapes)
    c_in, c_out = len(comm.inputs), len(comm.out_shapes)

    def fused(*refs):
        ins, refs = refs[:n_in], refs[n_in:]
        cin, refs = refs[:c_in], refs[c_in:]
        outs, refs = refs[:n_out], refs[n_out:]
        cout, refs = refs[:c_out], refs[c_out:]
        scr, csem = refs[:n_scr], refs[n_scr:]
        first = functools.reduce(jnp.logical_and, [pl.program_id(d) == 0 for d in range(len(grid))])
        last = functools.reduce(jnp.logical_and, [pl.program_id(d) == grid[d] - 1 for d in range(len(grid))])

        @pl.when(first)
        def _():
            comm.start(cin, cout, csem)

        body(*ins, *outs, *scr)

        @pl.when(last)
        def _():
            comm.finish(cin, cout, csem)

    res = pl.pallas_call(
        fused, name=name, grid=grid, in_specs=list(in_specs) + [ANY] * c_in,
        out_specs=tuple(out_specs) + (ANY,) * c_out, out_shape=tuple(out_shape) + comm.out_shapes,
        scratch_shapes=list(scratch_shapes) + comm.sem_shapes,
        compiler_params=_params(("arbitrary",) * len(grid)))(*args, *comm.inputs)
    return res[:n_out], res[n_out:]


_DIMS = {"nn": (((1,), (0,)), ((), ())), "nt": (((1,), (1,)), ((), ())), "tn": (((0,), (0,)), ((), ()))}


def _mm(a, b, mode, out_dtype, name, tm, tn, tk, cols_outer=False, comm=None):
    if mode == "nn":
        (M, K), (K2, N) = a.shape, b.shape
    elif mode == "nt":
        (M, K), (N, K2) = a.shape, b.shape
    else:
        (K, M), (K2, N) = a.shape, b.shape
    assert K == K2 and M % tm == 0 and N % tn == 0 and K % tk == 0, (name, a.shape, b.shape)
    nk = K // tk

    def body(a_ref, b_ref, o_ref, *scratch):
        part = lax.dot_general(a_ref[...].astype(bf16), b_ref[...].astype(bf16), _DIMS[mode],
                               preferred_element_type=f32)
        if nk == 1:
            o_ref[...] = part.astype(o_ref.dtype)
        else:
            acc_ref, = scratch
            k = pl.program_id(2)

            @pl.when(k == 0)
            def _():
                acc_ref[...] = part

            @pl.when(k > 0)
            def _():
                acc_ref[...] += part

            @pl.when(k == nk - 1)
            def _():
                o_ref[...] = acc_ref[...].astype(o_ref.dtype)

    def ij(f):
        return (lambda j, i, k: f(i, j, k)) if cols_outer else f

    if mode == "tn":
        a_spec = pl.BlockSpec((tk, tm), ij(lambda i, j, k: (k, i)))
    else:
        a_spec = pl.BlockSpec((tm, tk), ij(lambda i, j, k: (i, k)))
    if mode == "nt":
        b_spec = pl.BlockSpec((tn, tk), ij(lambda i, j, k: (j, k)))
    else:
        b_spec = pl.BlockSpec((tk, tn), ij(lambda i, j, k: (k, j)))
    o_spec = pl.BlockSpec((tm, tn), ij(lambda i, j, k: (i, j)))
    grid = (N // tn, M // tm, nk) if cols_outer else (M // tm, N // tn, nk)
    (out,), extra = _call(
        body, (a, b), name=name, out_shape=(SDS((M, N), out_dtype),), grid=grid, in_specs=[a_spec, b_spec],
        out_specs=(o_spec,), scratch_shapes=[pltpu.VMEM((tm, tn), f32)] if nk > 1 else [],
        semantics=("parallel", "parallel", "arbitrary"), comm=comm)
    return out if comm is None else (out, extra)


def _mm_nt_sum(pairs, name, tm, tn, comm=None):
    M, N = pairs[0][0].shape[0], pairs[0][1].shape[0]
    nks = [a.shape[1] // tk for a, _, tk in pairs]
    starts = [sum(nks[:p]) for p in range(len(pairs))]
    nk = sum(nks)

    def body(*refs):
        o_ref, acc_ref = refs[-2], refs[-1]
        k = pl.program_id(2)
        for p in range(len(pairs)):
            def product(p=p):
                return lax.dot_general(refs[2 * p][...], refs[2 * p + 1][...], _DIMS["nt"], preferred_element_type=f32)

            if p == 0:
                @pl.when(k == 0)
                def _():
                    acc_ref[...] = product()

            @pl.when((k >= max(starts[p], 1)) & (k < starts[p] + nks[p]))
            def _():
                acc_ref[...] += product()

        @pl.when(k == nk - 1)
        def _():
            o_ref[...] = acc_ref[...]

    in_specs, args = [], []
    for (a, b, tk), k0, n in zip(pairs, starts, nks):
        assert a.shape[1] == b.shape[1] and a.shape[1] % tk == 0 and a.dtype == b.dtype == bf16, name
        chunk = lambda k, k0=k0, n=n: jnp.clip(k - k0, 0, n - 1)
        in_specs += [pl.BlockSpec((tm, tk), lambda i, j, k, c=chunk: (i, c(k))),
                     pl.BlockSpec((tn, tk), lambda i, j, k, c=chunk: (j, c(k)))]
        args += [a, b]
    o_spec = pl.BlockSpec((tm, tn), lambda i, j, k: (i, j))
    (out,), extra = _call(
        body, args, name=name, out_shape=(SDS((M, N), f32),), grid=(M // tm, N // tn, nk), in_specs=in_specs,
        out_specs=(o_spec,), scratch_shapes=[pltpu.VMEM((tm, tn), f32)],
        semantics=("parallel", "parallel", "arbitrary"), comm=comm)
    return out if comm is None else (out, extra)


def _mm_rows(pairs, mode, name, tm, epilogue, rows_in, vecs_in, rows_out, vecs_out, comm=None):
    M = pairs[0][0].shape[0]
    N = pairs[0][1].shape[1 if mode == "nn" else 0]
    nks = [a.shape[1] // tk for a, _, tk in pairs]
    starts = [sum(nks[:p]) for p in range(len(pairs))]
    nk = sum(nks)
    n_rows_in, n_vecs_in, n_rows_out = len(rows_in), len(vecs_in), len(rows_out)

    def body(*refs):
        pair_refs, refs = refs[:2 * len(pairs)], refs[2 * len(pairs):]
        rin, refs = refs[:n_rows_in], refs[n_rows_in:]
        vin, refs = refs[:n_vecs_in], refs[n_vecs_in:]
        rout, refs = refs[:n_rows_out], refs[n_rows_out:]
        vout, acc_ref = refs[:-1], refs[-1]
        i, k = pl.program_id(0), pl.program_id(1)
        for p in range(len(pairs)):
            def product(p=p):
                return lax.dot_general(pair_refs[2 * p][...], pair_refs[2 * p + 1][...], _DIMS[mode],
                                       preferred_element_type=f32)

            if p == 0:
                @pl.when(k == 0)
                def _():
                    acc_ref[...] = product()

            @pl.when((k >= max(starts[p], 1)) & (k < starts[p] + nks[p]))
            def _():
                acc_ref[...] += product()

        @pl.when(k == nk - 1)
        def _():
            res = epilogue(acc_ref[...], *[r[...] for r in rin], *[v[...] for v in vin])
            for ref, val in zip(rout, res[:n_rows_out]):
                ref[...] = val.astype(ref.dtype)
            for ref, val in zip(vout, res[n_rows_out:]):
                @pl.when(i == 0)
                def _(ref=ref, val=val):
                    ref[...] = val

                @pl.when(i > 0)
                def _(ref=ref, val=val):
                    ref[...] += val

    in_specs, args = [], []
    for (a, b, tk), k0, n in zip(pairs, starts, nks):
        assert a.shape[1] % tk == 0 and a.dtype == b.dtype == bf16, name
        chunk = lambda k, k0=k0, n=n: jnp.clip(k - k0, 0, n - 1)
        in_specs.append(pl.BlockSpec((tm, tk), lambda i, k, c=chunk: (i, c(k))))
        if mode == "nn":
            in_specs.append(pl.BlockSpec((tk, N), lambda i, k, c=chunk: (c(k), 0)))
        else:
            in_specs.append(pl.BlockSpec((N, tk), lambda i, k, c=chunk: (0, c(k))))
        args += [a, b]
    row = lambda: pl.BlockSpec((tm, N), lambda i, k: (i, 0))
    vec = lambda w: pl.BlockSpec((1, w), lambda i, k: (0, 0))
    in_specs += [row() for _ in rows_in] + [vec(v.shape[1]) for v in vecs_in]
    outs, extra = _call(
        body, (*args, *rows_in, *vecs_in), name=name,
        out_shape=tuple(SDS((M, N), dt) for dt in rows_out) + tuple(SDS((1, w), f32) for w in vecs_out),
        grid=(M // tm, nk), in_specs=in_specs,
        out_specs=tuple(row() for _ in rows_out) + tuple(vec(w) for w in vecs_out),
        scratch_shapes=[pltpu.VMEM((tm, N), f32)], semantics=("arbitrary", "arbitrary"), comm=comm)
    res = (outs[:n_rows_out], outs[n_rows_out:])
    return res if comm is None else (res, extra)


ROW_TILE = 512


def _rms_fwd(x, g):
    r = lax.rsqrt(jnp.mean(x * x, axis=-1, keepdims=True) + EPS)
    return x * r * g


def _rms_bwd(x, g, dy):
    r = lax.rsqrt(jnp.mean(x * x, axis=-1, keepdims=True) + EPS)
    xh = x * r
    dxh = dy * g
    dx = r * (dxh - xh * jnp.mean(dxh * xh, axis=-1, keepdims=True))
    return dx, jnp.sum(dy * xh, axis=0, keepdims=True)


def _acc_out(ref, val):
    @pl.when(pl.program_id(0) == 0)
    def _():
        ref[...] = val

    @pl.when(pl.program_id(0) > 0)
    def _():
        ref[...] += val


def _row_spec(width=D):
    return pl.BlockSpec((ROW_TILE, width), lambda i: (i, 0))


def _vec_spec(width=D):
    return pl.BlockSpec((1, width), lambda i: (0, 0))


def _norm_in(x, g, comm=None):
    def body(x_ref, g_ref, o_ref):
        o_ref[...] = _rms_fwd(x_ref[...], g_ref[...]).astype(bf16)

    T = x.shape[0]
    (hn,), extra = _call(body, (x, g), name="norm_in", out_shape=(SDS((T, D), bf16),), grid=(T // ROW_TILE,),
                         in_specs=[_row_spec(), _vec_spec()], out_specs=(_row_spec(),), semantics=("parallel",),
                         comm=comm)
    return hn, extra


def _mid_bwd(dy, dhn2, h1, g_fpre, mix, g_post):
    def body(dy_ref, dhn2_ref, h1_ref, gf_ref, mix_ref, gp_ref, dh1_ref, dmix_ref, dgf_ref, dgp_ref):
        d1, dgf = _rms_bwd(h1_ref[...], gf_ref[...], dhn2_ref[...])
        dh1 = dy_ref[...] + d1
        dh1_ref[...] = dh1
        dmix, dgp = _rms_bwd(mix_ref[...], gp_ref[...], dh1)
        dmix_ref[...] = dmix.astype(bf16)
        _acc_out(dgf_ref, dgf)
        _acc_out(dgp_ref, dgp)

    T = dy.shape[0]
    return pl.pallas_call(
        body, name="mid_bwd", out_shape=(SDS((T, D), f32), SDS((T, D), bf16), SDS((1, D), f32), SDS((1, D), f32)),
        grid=(T // ROW_TILE,),
        in_specs=[_row_spec(), _row_spec(), _row_spec(), _vec_spec(), _row_spec(), _vec_spec()],
        out_specs=(_row_spec(), _row_spec(), _vec_spec(), _vec_spec()),
        compiler_params=_params(("arbitrary",)))(dy, dhn2, h1, g_fpre, mix, g_post)


def _in_bwd(dh1, dhn, x, g_pre):
    def body(dh1_ref, dhn_ref, x_ref, g_ref, dx_ref, dg_ref):
        d, dg = _rms_bwd(x_ref[...], g_ref[...], dhn_ref[...])
        dx_ref[...] = dh1_ref[...] + d
        _acc_out(dg_ref, dg)

    T = x.shape[0]
    return pl.pallas_call(
        body, name="in_bwd", out_shape=(SDS((T, D), f32), SDS((1, D), f32)), grid=(T // ROW_TILE,),
        in_specs=[_row_spec(), _row_spec(), _row_spec(), _vec_spec()], out_specs=(_row_spec(), _vec_spec()),
        compiler_params=_params(("arbitrary",)))(dh1, dhn, x, g_pre)


def _mid_fwd_rows(mix, x, g_post, g_fpre):
    h1 = x + _rms_fwd(mix, g_post)
    return mix, h1, _rms_fwd(h1, g_fpre)


def _final_rows(ff, h1, target, g_fpost):
    e = h1 + _rms_fwd(ff, g_fpost) - target
    part = jnp.sum(jnp.sum(e * e, axis=1, keepdims=True), axis=0, keepdims=True) * (0.5 / D)
    dy = e * (1.0 / D)
    dff, dg = _rms_bwd(ff, g_fpost, dy)
    return dy, dff, part, dg


def _shift_dn(x, d, row, fill=0.0):
    if d == 0:
        return x
    y = pltpu.roll(x, d, 0)
    head = jnp.where(row[:SUBLANES] >= d, y[:SUBLANES], fill)
    return jnp.concatenate([head, y[SUBLANES:]], axis=0)


def _shift_up(x, d, row, fill=0.0):
    if d == 0:
        return x
    n = x.shape[0]
    y = pltpu.roll(x, n - d, 0)
    tail = jnp.where(row[:SUBLANES] < SUBLANES - d, y[n - SUBLANES:], fill)
    return jnp.concatenate([y[:n - SUBLANES], tail], axis=0)


def _conv_fwd(x, w_ref, b, row):
    K = w_ref.shape[0]
    y = b
    for k in range(K):
        y = y + w_ref[k:k + 1, :] * _shift_dn(x, K - 1 - k, row)
    return y


def _conv_bwd(x, w_ref, dy, row):
    K = w_ref.shape[0]
    dx = jnp.zeros_like(dy)
    dws = []
    for k in range(K):
        dx = dx + w_ref[k:k + 1, :] * _shift_up(dy, K - 1 - k, row)
        dws.append(jnp.sum(dy * _shift_dn(x, K - 1 - k, row), axis=0, keepdims=True))
    return dx, dws, jnp.sum(dy, axis=0, keepdims=True)


def _scan_fwd(a, u, row):
    n = a.shape[0]
    d = 1
    while d < n:
        last = 2 * d >= n
        if d < SUBLANES:
            u = u + a * _shift_dn(u, d, row)
            if not last:
                a = a * _shift_dn(a, d, row, fill=1.0)
        else:
            u = jnp.concatenate([u[:d], u[d:] + a[d:] * u[:n - d]], axis=0)
            if not last:
                a = jnp.concatenate([a[:d], a[d:] * a[:n - d]], axis=0)
        d *= 2
    return u


def _scan_bwd(b, u, row):
    n = b.shape[0]
    d = 1
    while d < n:
        last = 2 * d >= n
        if d < SUBLANES:
            u = u + b * _shift_up(u, d, row)
            if not last:
                b = b * _shift_up(b, d, row, fill=1.0)
        else:
            u = jnp.concatenate([u[:n - d] + b[:n - d] * u[d:], u[n - d:]], axis=0)
            if not last:
                b = jnp.concatenate([b[:n - d] * b[d:], b[n - d:]], axis=0)
        d *= 2
    return u


def _neg_expm1(z):
    series = -z * (1.0 + z * (0.5 + z * (1.0 / 6.0 + z * (1.0 / 24.0 + z * (1.0 / 120.0)))))
    return jnp.where(z > -0.1, series, 1.0 - jnp.exp(z))


def _softplus_neg(lam):
    z = -lam
    return jnp.maximum(z, 0.0) + jnp.log(1.0 + jnp.exp(-jnp.abs(z)))


def _rnn_specs(B):
    blk = lambda: pl.BlockSpec((SEQ, LANES), lambda b, n: (b, n))
    return dict(
        act=blk,
        convw=pl.BlockSpec((RNN_CONV, LANES), lambda b, n: (0, n)),
        vec=lambda: pl.BlockSpec((1, LANES), lambda b, n: (0, n)),
        gate=lambda: pl.BlockSpec((None, LANES, LANES), lambda b, n: (n, 0, 0)),
    )


def _rnn_fwd(proj, cw, cb, wa, ba, wx, bx, lam, comm=None):
    T = proj.shape[0]
    B = T // SEQ

    def body(x_ref, cw_ref, cb_ref, wa_ref, ba_ref, wx_ref, bx_ref, lam_ref, h_ref, xc_ref, r_ref, i_ref, a_ref, s_ref):
        row = lax.broadcasted_iota(jnp.int32, (SEQ, LANES), 0)
        xc = _conv_fwd(x_ref[...], cw_ref, cb_ref[...], row)
        xcb = xc.astype(bf16)
        r = _sigmoid(jnp.dot(xcb, wa_ref[...].astype(bf16), preferred_element_type=f32) + ba_ref[...])
        i = _sigmoid(jnp.dot(xcb, wx_ref[...].astype(bf16), preferred_element_type=f32) + bx_ref[...])
        log_a = (-LRU_C * _softplus_neg(lam_ref[...])) * r
        a = jnp.exp(log_a)
        s = jnp.sqrt(_neg_expm1(2.0 * log_a))
        xc_ref[...], r_ref[...], i_ref[...], a_ref[...], s_ref[...] = xc, r, i, a, s
        h_ref[...] = _scan_fwd(a, s * (i * xc), row)

    sp_ = _rnn_specs(B)
    return _call(
        body, (proj, cw, cb, wa, ba, wx, bx, lam), name="rnn_fwd", out_shape=(SDS((T, RNN_W), f32),) * 6,
        grid=(B, RNN_BLOCKS),
        in_specs=[sp_["act"](), sp_["convw"], sp_["vec"](), sp_["gate"](), sp_["vec"](), sp_["gate"](),
                  sp_["vec"](), sp_["vec"]()],
        out_specs=tuple(sp_["act"]() for _ in range(6)), semantics=("parallel", "parallel"), comm=comm)


def _rnn_bwd(proj, saved, dh, cw, wa, wx, lam, comm=None):
    T = proj.shape[0]
    B = T // SEQ

    def body(x_ref, h_ref, xc_ref, r_ref, i_ref, a_ref, s_ref, dh_ref, cw_ref, wa_ref, wx_ref, lam_ref,
             dx_ref, dcw_ref, dcb_ref, dwa_ref, dba_ref, dwx_ref, dbx_ref, dlam_ref):
        row = lax.broadcasted_iota(jnp.int32, (SEQ, LANES), 0)
        xr = x_ref[...]
        wa, wx, lam = wa_ref[...], wx_ref[...], lam_ref[...]
        xc, r, i, a, s = xc_ref[...], r_ref[...], i_ref[...], a_ref[...], s_ref[...]
        xcb = xc.astype(bf16)
        sp = _softplus_neg(lam)
        hprev = _shift_dn(h_ref[...], 1, row)
        g = _scan_bwd(_shift_up(a, 1, row), dh_ref[...].astype(f32), row)
        da = g * hprev
        ds = g * (i * xc)
        di = g * (s * xc)
        dxc = g * (s * i)
        dla = da * a - ds * (a * a) / s
        dr = dla * (-LRU_C * sp)
        dsp = jnp.sum(dla * (-LRU_C * r), axis=0, keepdims=True)
        dlam = -dsp * _sigmoid(-lam)
        dga = dr * r * (1.0 - r)
        dgx = di * i * (1.0 - i)
        dgab, dgxb = dga.astype(bf16), dgx.astype(bf16)
        dwa = lax.dot_general(xcb, dgab, _DIMS["tn"], preferred_element_type=f32)
        dwx = lax.dot_general(xcb, dgxb, _DIMS["tn"], preferred_element_type=f32)
        dxc = dxc + lax.dot_general(dgab, wa.astype(bf16), _DIMS["nt"], preferred_element_type=f32)
        dxc = dxc + lax.dot_general(dgxb, wx.astype(bf16), _DIMS["nt"], preferred_element_type=f32)
        dx, dws, db = _conv_bwd(xr, cw_ref, dxc, row)
        dx_ref[...] = dx.astype(bf16)
        first = pl.program_id(1) == 0

        def acc(ref, val):
            @pl.when(first)
            def _():
                ref[...] = val

            @pl.when(jnp.logical_not(first))
            def _():
                ref[...] += val

        for k in range(RNN_CONV):
            acc(dcw_ref.at[k:k + 1, :], dws[k])
        acc(dcb_ref, db)
        acc(dwa_ref, dwa)
        acc(dba_ref, jnp.sum(dga, axis=0, keepdims=True))
        acc(dwx_ref, dwx)
        acc(dbx_ref, jnp.sum(dgx, axis=0, keepdims=True))
        acc(dlam_ref, dlam)

    blk = lambda: pl.BlockSpec((SEQ, LANES), lambda n, b: (b, n))
    convw = lambda: pl.BlockSpec((RNN_CONV, LANES), lambda n, b: (0, n))
    vec = lambda: pl.BlockSpec((1, LANES), lambda n, b: (0, n))
    gate = lambda: pl.BlockSpec((None, LANES, LANES), lambda n, b: (n, 0, 0))
    vshape = SDS((1, RNN_W), f32)
    gshape = SDS((RNN_BLOCKS, LANES, LANES), f32)
    return _call(
        body, (proj, *saved, dh, cw, wa, wx, lam), name="rnn_bwd",
        out_shape=(SDS((T, RNN_W), bf16), SDS((RNN_CONV, RNN_W), f32), vshape, gshape, vshape, gshape, vshape, vshape),
        grid=(RNN_BLOCKS, B),
        in_specs=[blk() for _ in range(8)] + [convw(), gate(), gate(), vec()],
        out_specs=(blk(), convw(), vec(), gate(), vec(), gate(), vec(), vec()),
        semantics=("parallel", "arbitrary"), comm=comm)


def _t5_bucket(dist):
    max_exact = REL_BUCKETS // 2
    d = np.maximum(dist, 1).astype(np.float32)
    large = max_exact + np.log(d / max_exact) / math.log(REL_MAX_DIST / max_exact) * (REL_BUCKETS - max_exact)
    large = np.minimum(large.astype(np.int32), REL_BUCKETS - 1)
    return np.where(dist < max_exact, dist, large).astype(np.int32)


def _bucket_maps():
    qi = np.arange(ATT_BLK)[:, None]
    kj = np.arange(2 * ATT_BLK)[None, :]
    delta = ATT_BLK + qi - kj
    valid = (delta >= 0) & (delta <= ATT_BLK)
    maps = [np.where(valid, _t5_bucket(np.maximum(delta, 0) * r), -1) for r in DILATIONS]
    return np.stack(maps).astype(np.int32)


def _bias_tables(rel_bias, buckets):
    def body(rb_ref, bk_ref, o_ref):
        for g in range(NG):
            bk = bk_ref[g]
            for h in range(KVH):
                acc = jnp.full(bk.shape, NEG, f32)
                for b in range(REL_BUCKETS):
                    acc = jnp.where(bk == b, rb_ref[b, g * KVH + h], acc)
                o_ref[h, g] = acc

    return pl.pallas_call(
        body, name="bias_tables", out_shape=SDS((KVH, NG, ATT_BLK, 2 * ATT_BLK), f32),
        in_specs=[pl.BlockSpec(memory_space=pltpu.SMEM), pl.BlockSpec(memory_space=pltpu.VMEM)],
        out_specs=pl.BlockSpec(memory_space=pltpu.VMEM), compiler_params=_params())(rel_bias, buckets)


def _bias_grad(dbias, buckets):
    def body(db_ref, bk_ref, o_ref):
        rr = lax.broadcasted_iota(jnp.int32, (REL_BUCKETS, NG * KVH), 0)
        cc = lax.broadcasted_iota(jnp.int32, (REL_BUCKETS, NG * KVH), 1)
        out = jnp.zeros((REL_BUCKETS, NG * KVH), f32)
        for g in range(NG):
            bk = bk_ref[g]
            for h in range(KVH):
                d = db_ref[h, g]
                for b in range(REL_BUCKETS):
                    m = jnp.where(bk == b, d, 0.0)
                    s = jnp.sum(jnp.sum(m, axis=1, keepdims=True), axis=0, keepdims=True)
                    out = jnp.where((rr == b) & (cc == g * KVH + h), s, out)
        o_ref[...] = out

    return pl.pallas_call(body, name="bias_grad", out_shape=SDS((REL_BUCKETS, NG * KVH), f32),
                          compiler_params=_params())(dbias, buckets)


def _to_sub(dst_ref, src_ref, r, dtype, offset=0):
    M = SEQ // r
    for c in range(r):
        if r == 1:
            v = src_ref[...]
        else:
            v = src_ref[pl.ds(c, M, stride=r), :]
        dst_ref[pl.ds(offset + c * M, M), :] = v.astype(dtype)


def _from_sub(dst_ref, src_ref, r, accumulate=False, offset=0):
    M = SEQ // r
    for c in range(r):
        v = src_ref[pl.ds(offset + c * M, M), :]
        idx = slice(None) if r == 1 else pl.ds(c, M, stride=r)
        if accumulate:
            dst_ref[idx, :] = dst_ref[idx, :] + v
        else:
            dst_ref[idx, :] = v


_COL = lambda k: slice(k * HD, (k + 1) * HD)
SCALE = HD ** -0.5


def _qkv_spec(k, bh):
    def index(*ids):
        b, h = bh(*ids)
        return (b, C_ATT // HD + 5 * h + k)

    return pl.BlockSpec((SEQ, HD), index)


def _key_window(bias_ref, g, nb):
    if nb == 1:
        bias_own = bias_ref[g, :, ATT_BLK:2 * ATT_BLK]
        return lambda j: (pl.ds(pl.multiple_of((j + 1) * ATT_BLK, ATT_BLK), ATT_BLK), bias_own)
    bias_g = bias_ref[g]
    col = lax.broadcasted_iota(jnp.int32, bias_g.shape, 1)
    bias_first = jnp.where(col >= ATT_BLK, bias_g, NEG)
    return lambda j: (pl.ds(pl.multiple_of(j * ATT_BLK, ATT_BLK), 2 * ATT_BLK),
                      jnp.where(j % nb != 0, bias_g, bias_first))


def _att_fwd(proj, bias, comm=None):
    T = proj.shape[0]
    B = T // SEQ

    def body(q0_ref, q1_ref, q2_ref, k_ref, v_ref, bias_ref, o_ref, lse_ref, qp, kp, vp, kt, op, lp, og, lg):
        q_refs = (q0_ref, q1_ref, q2_ref)
        kp[0:ATT_BLK, :] = jnp.zeros((ATT_BLK, HD), bf16)
        vp[0:ATT_BLK, :] = jnp.zeros((ATT_BLK, HD), bf16)
        for g, r in enumerate(DILATIONS):
            nb = NBLK_SEQ // r
            _to_sub(qp, q_refs[g], r, bf16)
            _to_sub(kp, k_ref, r, bf16, offset=ATT_BLK)
            _to_sub(vp, v_ref, r, bf16, offset=ATT_BLK)
            kt[...] = kp[...].T
            keys = _key_window(bias_ref, g, nb)

            def step(j, carry):
                cur = pl.ds(pl.multiple_of(j * ATT_BLK, ATT_BLK), ATT_BLK)
                win, bias_j = keys(j)
                s = jnp.dot(qp[cur, :], kt[:, win], preferred_element_type=f32) * SCALE + bias_j
                m = jnp.max(s, axis=-1, keepdims=True)
                p = jnp.exp(s - m)
                den = jnp.sum(p, axis=-1, keepdims=True)
                o = jnp.dot(p.astype(bf16), vp[win, :], preferred_element_type=f32)
                op[cur, :] = o / den
                lp[cur, :] = jnp.broadcast_to(m + jnp.log(den), (ATT_BLK, HD))
                return carry

            lax.fori_loop(0, NBLK_SEQ, step, 0, unroll=ATT_UNROLL)
            _from_sub(og.at[g], op, r)
            _from_sub(lg.at[g], lp, r)
        l0, l1, l2 = lg[0], lg[1], lg[2]
        mx = jnp.maximum(jnp.maximum(l0, l1), l2)
        e0, e1, e2 = jnp.exp(l0 - mx), jnp.exp(l1 - mx), jnp.exp(l2 - mx)
        den = e0 + e1 + e2
        o_ref[...] = (e0 * og[0] + e1 * og[1] + e2 * og[2]) / den
        lse_ref[...] = mx + jnp.log(den)

    return _call(
        body, (proj, proj, proj, proj, proj, bias), name="att_fwd",
        out_shape=(SDS((T, KVH * HD), f32), SDS((KVH, T, HD), f32)), grid=(B, KVH),
        in_specs=[_qkv_spec(k, lambda b, h: (b, h)) for k in range(5)]
                 + [pl.BlockSpec((None, NG, ATT_BLK, 2 * ATT_BLK), lambda b, h: (h, 0, 0, 0))],
        out_specs=(pl.BlockSpec((SEQ, HD), lambda b, h: (b, h)),
                   pl.BlockSpec((None, SEQ, HD), lambda b, h: (h, b, 0))),
        scratch_shapes=[pltpu.VMEM((SEQ, HD), bf16)] + [pltpu.VMEM((SEQ + ATT_BLK, HD), bf16)] * 2
                       + [pltpu.VMEM((HD, SEQ + ATT_BLK), bf16)]
                       + [pltpu.VMEM((SEQ, HD), f32)] * 2 + [pltpu.VMEM((NG, SEQ, HD), f32)] * 2,
        semantics=("parallel", "parallel"), comm=comm)


def _att_bwd(proj, bias, o, lse, do, comm=None):
    T = proj.shape[0]
    B = T // SEQ

    def body(q0_ref, q1_ref, q2_ref, k_ref, v_ref, bias_ref, o_ref, lse_ref, do_ref, dx_ref, db_ref,
             qp, kp, vp, dop, qt, kt, vt, dot, lp, dqp, dkt, dvt, dln, nat, dkn, dvn):
        q_refs = (q0_ref, q1_ref, q2_ref)
        first = pl.program_id(1) == 0

        @pl.when(first)
        def _():
            db_ref[...] = jnp.zeros_like(db_ref)

        lane = lax.broadcasted_iota(jnp.int32, (SEQ, HD), 1)
        dln[...] = jnp.where(lane < STAT_LANE, lse_ref[...],
                             jnp.sum(do_ref[...] * o_ref[...], axis=-1, keepdims=True))
        dkn[...] = jnp.zeros_like(dkn)
        dvn[...] = jnp.zeros_like(dvn)
        kp[0:ATT_BLK, :] = jnp.zeros((ATT_BLK, HD), bf16)
        vp[0:ATT_BLK, :] = jnp.zeros((ATT_BLK, HD), bf16)
        for g, r in enumerate(DILATIONS):
            nb = NBLK_SEQ // r
            _to_sub(qp, q_refs[g], r, bf16)
            _to_sub(kp, k_ref, r, bf16, offset=ATT_BLK)
            _to_sub(vp, v_ref, r, bf16, offset=ATT_BLK)
            _to_sub(dop, do_ref, r, bf16)
            _to_sub(lp, dln, r, f32)
            qt[...], kt[...], vt[...], dot[...] = qp[...].T, kp[...].T, vp[...].T, dop[...].T
            dkt[...] = jnp.zeros_like(dkt)
            dvt[...] = jnp.zeros_like(dvt)
            keys = _key_window(bias_ref, g, nb)
            db_cols = slice(ATT_BLK, 2 * ATT_BLK) if nb == 1 else slice(None)

            def step(j, carry):
                cur = pl.ds(pl.multiple_of(j * ATT_BLK, ATT_BLK), ATT_BLK)
                win, bias_j = keys(j)
                s = jnp.dot(qp[cur, :], kt[:, win], preferred_element_type=f32) * SCALE + bias_j
                p = jnp.exp(s - lp[cur, 0:1])
                dp = jnp.dot(dop[cur, :], vt[:, win], preferred_element_type=f32)
                ds = p * (dp - lp[cur, STAT_LANE:STAT_LANE + 1])
                db_ref[g, :, db_cols] += ds
                dsb, pb = ds.astype(bf16), p.astype(bf16)
                dqp[cur, :] = jnp.dot(dsb, kp[win, :], preferred_element_type=f32) * SCALE
                dkt[:, win] += jnp.dot(qt[:, cur], dsb, preferred_element_type=f32) * SCALE
                dvt[:, win] += jnp.dot(dot[:, cur], pb, preferred_element_type=f32)
                return carry

            lax.fori_loop(0, NBLK_SEQ, step, 0, unroll=ATT_UNROLL)
            _from_sub(nat, dqp, r)
            dx_ref[:, _COL(g)] = nat[...].astype(bf16)
            dqp[...] = dkt[:, ATT_BLK:].T
            _from_sub(dkn, dqp, r, accumulate=True)
            dqp[...] = dvt[:, ATT_BLK:].T
            _from_sub(dvn, dqp, r, accumulate=True)
        dx_ref[:, _COL(3)] = dkn[...].astype(bf16)
        dx_ref[:, _COL(4)] = dvn[...].astype(bf16)

    blk = lambda: pl.BlockSpec((SEQ, HD), lambda h, b: (b, h))
    bias_spec = lambda: pl.BlockSpec((None, NG, ATT_BLK, 2 * ATT_BLK), lambda h, b: (h, 0, 0, 0))
    pad = lambda dtype: pltpu.VMEM((SEQ + ATT_BLK, HD), dtype)
    pad_t = lambda dtype: pltpu.VMEM((HD, SEQ + ATT_BLK), dtype)
    seq_t = pltpu.VMEM((HD, SEQ), bf16)
    return _call(
        body, (proj, proj, proj, proj, proj, bias, o, lse, do), name="att_bwd",
        out_shape=(SDS((T, KVH * ATT_COLS), bf16), SDS((KVH, NG, ATT_BLK, 2 * ATT_BLK), f32)), grid=(KVH, B),
        in_specs=[_qkv_spec(k, lambda h, b: (b, h)) for k in range(5)]
                 + [bias_spec(), blk(), pl.BlockSpec((None, SEQ, HD), lambda h, b: (h, b, 0)), blk()],
        out_specs=(pl.BlockSpec((SEQ, ATT_COLS), lambda h, b: (b, h)), bias_spec()),
        scratch_shapes=[pltpu.VMEM((SEQ, HD), bf16), pad(bf16), pad(bf16), pltpu.VMEM((SEQ, HD), bf16),
                        seq_t, pad_t(bf16), pad_t(bf16), seq_t]
                       + [pltpu.VMEM((SEQ, HD), f32)] * 2 + [pad_t(f32)] * 2 + [pltpu.VMEM((SEQ, HD), f32)] * 4,
        semantics=("parallel", "arbitrary"), comm=comm)


MERGE_ROWS, MERGE_COLS = 1024, 256


def _merge_fwd(gates, pr, pa):
    def body(gr_ref, ga_ref, pr_ref, pa_ref, o_ref):
        o_ref[...] = (_sigmoid(gr_ref[...].astype(f32)) * pr_ref[...].astype(f32)
                      + _sigmoid(ga_ref[...].astype(f32)) * pa_ref[...].astype(f32)).astype(bf16)

    T = gates.shape[0]
    cols = lambda off: pl.BlockSpec((MERGE_ROWS, MERGE_COLS), lambda i, j: (i, off + j))
    return pl.pallas_call(body, name="merge_fwd", out_shape=SDS((T, D), bf16),
                          grid=(T // MERGE_ROWS, D // MERGE_COLS),
                          in_specs=[cols(0), cols(D // MERGE_COLS), cols(0), cols(0)], out_specs=cols(0),
                          compiler_params=_params(("parallel", "parallel")))(gates, gates, pr, pa)


def _merge_bwd(gates, pr, pa, dm):
    nj = D // MERGE_COLS

    def body(g_ref, pr_ref, pa_ref, dm_ref, dp_ref, dg_ref):
        dm_ = dm_ref[...].astype(f32)
        s = _sigmoid(g_ref[...].astype(f32))
        p = jnp.where(pl.program_id(1) < nj, pr_ref[...], pa_ref[...]).astype(f32)
        dp_ref[...] = (dm_ * s).astype(bf16)
        dg_ref[...] = (dm_ * p * s * (1.0 - s)).astype(bf16)

    T = gates.shape[0]
    blk = (MERGE_ROWS, MERGE_COLS)
    wrap = pl.BlockSpec(blk, lambda i, j: (i, j % nj))
    pr_spec = pl.BlockSpec(blk, lambda i, j: (i, jnp.minimum(j, nj - 1)))
    pa_spec = pl.BlockSpec(blk, lambda i, j: (i, jnp.maximum(j - nj, 0)))
    out = pl.BlockSpec(blk, lambda i, j: (i, j))
    return pl.pallas_call(
        body, name="merge_bwd", out_shape=(SDS((T, 2 * D), bf16), SDS((T, 2 * D), bf16)),
        grid=(T // MERGE_ROWS, 2 * nj),
        in_specs=[out, pr_spec, pa_spec, wrap], out_specs=(out, out),
        compiler_params=_params(("parallel", "parallel")))(gates, pr, pa, dm)


FFN_COLS = 256
GELU_C = math.sqrt(2.0 / math.pi)
GELU_A = 0.044715


def _gelu_parts(x):
    t = jnp.tanh(GELU_C * (x + GELU_A * x * x * x))
    return 0.5 * x * (1.0 + t), t


def _ffn_act_fwd(gpre, up, cw, cb):
    def body(g_ref, u_ref, cw_ref, cb_ref, o_ref):
        row = lax.broadcasted_iota(jnp.int32, (SEQ, FFN_COLS), 0)
        gate = _conv_fwd(g_ref[...].astype(f32), cw_ref, cb_ref[...], row)
        o_ref[...] = (_gelu_parts(gate)[0] * u_ref[...].astype(f32)).astype(bf16)

    T = gpre.shape[0]
    blk = lambda: pl.BlockSpec((SEQ, FFN_COLS), lambda b, j: (b, j))
    return pl.pallas_call(
        body, name="ffn_act_fwd", out_shape=SDS((T, FFN_W), bf16), grid=(T // SEQ, FFN_W // FFN_COLS),
        in_specs=[blk(), blk(), pl.BlockSpec((FFN_CONV, FFN_COLS), lambda b, j: (0, j)),
                  pl.BlockSpec((1, FFN_COLS), lambda b, j: (0, j))],
        out_specs=blk(), compiler_params=_params(("parallel", "parallel")))(gpre, up, cw, cb)


def _ffn_act_bwd(gpre, up, cw, cb, dact):
    def body(g_ref, u_ref, cw_ref, cb_ref, da_ref, dg_ref, du_ref, dcw_ref, dcb_ref):
        row = lax.broadcasted_iota(jnp.int32, (SEQ, FFN_COLS), 0)
        gp = g_ref[...].astype(f32)
        gate = _conv_fwd(gp, cw_ref, cb_ref[...], row)
        gel, t = _gelu_parts(gate)
        da = da_ref[...].astype(f32)
        du_ref[...] = (da * gel).astype(bf16)
        dgel = 0.5 * (1.0 + t) + 0.5 * gate * (1.0 - t * t) * (GELU_C * (1.0 + 3.0 * GELU_A * gate * gate))
        dgate = da * u_ref[...].astype(f32) * dgel
        dx, dws, db = _conv_bwd(gp, cw_ref, dgate, row)
        dg_ref[...] = dx.astype(bf16)
        first = pl.program_id(1) == 0

        def acc(ref, val):
            @pl.when(first)
            def _():
                ref[...] = val

            @pl.when(jnp.logical_not(first))
            def _():
                ref[...] += val

        for k in range(FFN_CONV):
            acc(dcw_ref.at[k:k + 1, :], dws[k])
        acc(dcb_ref, db)

    T = gpre.shape[0]
    blk = lambda: pl.BlockSpec((SEQ, FFN_COLS), lambda j, b: (b, j))
    cws = lambda: pl.BlockSpec((FFN_CONV, FFN_COLS), lambda j, b: (0, j))
    cbs = lambda: pl.BlockSpec((1, FFN_COLS), lambda j, b: (0, j))
    return pl.pallas_call(
        body, name="ffn_act_bwd",
        out_shape=(SDS((T, FFN_W), bf16), SDS((T, FFN_W), bf16), SDS((FFN_CONV, FFN_W), f32), SDS((1, FFN_W), f32)),
        grid=(FFN_W // FFN_COLS, T // SEQ),
        in_specs=[blk(), blk(), cws(), cbs(), blk()], out_specs=(blk(), blk(), cws(), cbs()),
        compiler_params=_params(("parallel", "arbitrary")))(gpre, up, cw, cb, dact)


def _coords():
    return lax.axis_index("x"), lax.axis_index("y"), lax.axis_index("c")


def _dev_index(dev):
    return 4 * dev[0] + 2 * dev[1] + dev[2]


def _dma_sems(n):
    return [pltpu.SemaphoreType.DMA((n,)), pltpu.SemaphoreType.DMA((n,))]


def _gather_two_level(arrays):
    n = len(arrays)

    def plan(ins, outs, sems):
        send_sems, recv_sems, local_sems = sems
        x, y, c = _coords()
        me, sibling = (x, y, c), (x, y, 1 - c)
        chips = [(1 - x, y), (x, 1 - y), (1 - x, 1 - y)]

        def copy(a, k, block, to, own=False):
            dst = outs[a].at[_dev_index(block)]
            return pltpu.make_async_remote_copy(
                src_ref=ins[a] if own else dst, dst_ref=dst, send_sem=send_sems.at[7 * a + k],
                recv_sem=recv_sems.at[7 * a + k], device_id=to, device_id_type=MESH)

        mine = [pltpu.make_async_copy(ins[a], outs[a].at[_dev_index(me)], local_sems.at[a]) for a in range(n)]
        first = [copy(a, 0, me, sibling, own=True) for a in range(n)]
        first += [copy(a, 1 + j, me, (*chip, c), own=True) for a in range(n) for j, chip in enumerate(chips)]
        passed = [[copy(a, 4 + j, (*chip, c), sibling) for a in range(n)] for j, chip in enumerate(chips)]
        arrive_ici = [[copy(a, 1 + j, (*chip, c), me) for a in range(n)] for j, chip in enumerate(chips)]
        arrive_d2d = [copy(a, 0, sibling, me) for a in range(n)]
        arrive_d2d += [copy(a, 4 + j, (*chip, 1 - c), me) for a in range(n) for j, chip in enumerate(chips)]
        return mine, first, passed, arrive_ici, arrive_d2d

    def start(ins, outs, sems):
        mine, first, _, _, _ = plan(ins, outs, sems)
        for cp in mine + first:
            cp.start()

    def finish(ins, outs, sems):
        mine, first, passed, arrive_ici, arrive_d2d = plan(ins, outs, sems)
        for j in range(3):
            for cp in arrive_ici[j]:
                cp.wait_recv()
            for cp in passed[j]:
                cp.start()
        for cp in arrive_d2d:
            cp.wait_recv()
        for cp in first + [cp for group in passed for cp in group]:
            cp.wait_send()
        for cp in mine:
            cp.wait()

    return _Comm(arrays, [SDS((N_DEV,) + a.shape, a.dtype) for a in arrays],
                 _dma_sems(7 * n) + [pltpu.SemaphoreType.DMA((n,))], start, finish)


def _gather_direct(arrays):
    n = len(arrays)

    def plan(ins, outs, sems):
        send_sems, recv_sems, local_sems = sems
        x, y, c = _coords()
        me = (x, y, c)
        mine = [pltpu.make_async_copy(ins[a], outs[a].at[_dev_index(me)], local_sems.at[a]) for a in range(n)]
        sends, arrivals = [], []
        for a in range(n):
            for k in range(1, N_DEV):
                peer = (1 - x if k & 4 else x, 1 - y if k & 2 else y, 1 - c if k & 1 else c)
                s = 7 * a + k - 1
                for slot, out in ((me, sends), (peer, arrivals)):
                    out.append(pltpu.make_async_remote_copy(
                        src_ref=ins[a], dst_ref=outs[a].at[_dev_index(slot)], send_sem=send_sems.at[s],
                        recv_sem=recv_sems.at[s], device_id=peer, device_id_type=MESH))
        return mine, sends, arrivals

    def start(ins, outs, sems):
        mine, sends, _ = plan(ins, outs, sems)
        for cp in mine + sends:
            cp.start()

    def finish(ins, outs, sems):
        mine, sends, arrivals = plan(ins, outs, sems)
        for cp in arrivals:
            cp.wait_recv()
        for cp in sends:
            cp.wait_send()
        for cp in mine:
            cp.wait()

    return _Comm(arrays, [SDS((N_DEV,) + a.shape, a.dtype) for a in arrays],
                 _dma_sems(7 * n) + [pltpu.SemaphoreType.DMA((n,))], start, finish)


def _scatter_direct(arrays):
    n = len(arrays)

    def plan(ins, outs, sems):
        send_sems, recv_sems = sems
        x, y, c = _coords()
        cps = []
        for a in range(n):
            for k in range(1, N_DEV):
                peer = (1 - x if k & 4 else x, 1 - y if k & 2 else y, 1 - c if k & 1 else c)
                s = 7 * a + k - 1
                cps.append(pltpu.make_async_remote_copy(
                    src_ref=ins[a].at[_dev_index(peer)], dst_ref=outs[a].at[k - 1], send_sem=send_sems.at[s],
                    recv_sem=recv_sems.at[s], device_id=peer, device_id_type=MESH))
        return cps

    def start(ins, outs, sems):
        for cp in plan(ins, outs, sems):
            cp.start()

    def finish(ins, outs, sems):
        for cp in plan(ins, outs, sems):
            cp.wait()

    return _Comm(arrays, [SDS((N_DEV - 1,) + a.shape[1:], a.dtype) for a in arrays], _dma_sems(7 * n),
                 start, finish)


def _run(comm, name):
    def body(*refs):
        k_in, k_out = len(comm.inputs), len(comm.out_shapes)
        ins, outs, sems = refs[:k_in], refs[k_in:k_in + k_out], refs[k_in + k_out:]
        comm.start(ins, outs, sems)
        comm.finish(ins, outs, sems)

    return pl.pallas_call(body, name=name, out_shape=comm.out_shapes, in_specs=[ANY] * len(comm.inputs),
                          out_specs=(ANY,) * len(comm.out_shapes), scratch_shapes=comm.sem_shapes)(*comm.inputs)


TILE_ELEMS = 192 * 1024


def _row_tile(R, C):
    if R * C <= TILE_ELEMS:
        return R
    return max(t for t in range(SUBLANES, R, SUBLANES) if R % t == 0 and t * C <= TILE_ELEMS)


def _adamw_math(w, g, m, v):
    m = ADAM_B1 * m + (1.0 - ADAM_B1) * g
    v = ADAM_B2 * v + (1.0 - ADAM_B2) * (g * g)
    m_hat = m / (1.0 - ADAM_B1 ** ADAM_STEP)
    v_hat = v / (1.0 - ADAM_B2 ** ADAM_STEP)
    delta = -ADAM_LR * (m_hat / (jnp.sqrt(v_hat) + ADAM_EPS) + ADAM_WD * w)
    return delta, m, v


def _adamw_sharded(own, recv, d_idx, w, m, v, name):
    R, C = w.shape
    t = _row_tile(R, C)

    def body(k_ref, p_ref, r_ref, w_ref, m_ref, v_ref, g_ref, d_ref, nm_ref, nv_ref):
        g = p_ref[...].astype(f32)
        for j in range(N_DEV - 1):
            g = g + r_ref[j].astype(f32)
        d, nm, nv = _adamw_math(w_ref[...], g, m_ref[...], v_ref[...])
        g_ref[...], d_ref[...], nm_ref[...], nv_ref[...] = g, d, nm, nv

    tile = lambda: pl.BlockSpec((t, C), lambda i, k: (i, 0))
    return pl.pallas_call(
        body, name="adamw_" + name, out_shape=(SDS((R, C), f32),) * 4,
        grid_spec=pltpu.PrefetchScalarGridSpec(
            num_scalar_prefetch=1, grid=(R // t,),
            in_specs=[pl.BlockSpec((None, t, C), lambda i, k: (k[0], i, 0)),
                      pl.BlockSpec((N_DEV - 1, t, C), lambda i, k: (0, i, 0)), tile(), tile(), tile()],
            out_specs=(tile(), tile(), tile(), tile())),
        compiler_params=_params(("parallel",)))(d_idx, own, recv, w, m, v)


def _adamw_replicated(parts, ws, ms, vs):
    n = len(ws)

    def body(*refs):
        p, w, m, v = (refs[i * n:(i + 1) * n] for i in range(4))
        outs = refs[4 * n:]
        for a in range(n):
            g = p[a][0].astype(f32)
            for j in range(1, N_DEV):
                g = g + p[a][j].astype(f32)
            d, nm, nv = _adamw_math(w[a][...], g, m[a][...], v[a][...])
            for i, val in enumerate((g, d, nm, nv)):
                outs[i * n + a][...] = val

    shapes = tuple(SDS(w.shape, f32) for w in ws)
    res = pl.pallas_call(body, name="adamw_replicated", out_shape=shapes * 4,
                         compiler_params=_params())(*parts, *ws, *ms, *vs)
    return [res[i * n:(i + 1) * n] for i in range(4)]


def _cols_to_full(g):
    n, r, c = g.shape
    return g.transpose(1, 0, 2).reshape(r, n * c)


def _full_to_cols(a):
    r, c = a.shape
    return a.reshape(r, N_DEV, c // N_DEV).transpose(1, 0, 2)


def _rows_blocked(a):
    r, c = a.shape
    return a.reshape(N_DEV, r // N_DEV, c)


def _w_in_to_internal(w):
    K = w.shape[0]
    q = w[:, 1280:2816].reshape(K, NG, KVH, 1, HD).transpose(0, 2, 1, 3, 4).reshape(K, KVH, NG, HD)
    k = w[:, 2816:3328].reshape(K, KVH, 1, HD)
    v = w[:, 3328:3840].reshape(K, KVH, 1, HD)
    att = jnp.concatenate([q, k, v], axis=2).reshape(K, KVH * ATT_COLS)
    return jnp.concatenate([w[:, :1280], att, w[:, 3840:]], axis=1)


def _w_in_from_internal(w):
    K = w.shape[0]
    att = w[:, C_ATT:C_GATE].reshape(K, KVH, 5, HD)
    q = att[:, :, 0:3].transpose(0, 2, 1, 3).reshape(K, NG * KVH * HD)
    k = att[:, :, 3].reshape(K, KVH * HD)
    v = att[:, :, 4].reshape(K, KVH * HD)
    return jnp.concatenate([w[:, :C_ATT], q, k, v, w[:, C_GATE:]], axis=1)


_IN_NAMES = ('x', 'rel_bias', 'norm_mix_pre', 'norm_mix_post', 'w_in', 'conv_rnn_w', 'conv_rnn_b', 'w_rg_a', 'b_rg_a',
             'w_rg_x', 'b_rg_x', 'lru_lambda', 'w_branch_rnn', 'w_branch_att', 'w_out', 'norm_ffn_pre',
             'norm_ffn_post', 'w_ffn_gate', 'w_ffn_up', 'conv_ffn_w', 'conv_ffn_b', 'w_ffn_down')
_WEIGHTS = _IN_NAMES[1:]
_SHARDED = {"w_in": "col", "conv_rnn_w": "col", "w_branch_rnn": "row", "w_branch_att": "col", "w_out": "row",
            "w_ffn_gate": "col", "w_ffn_up": "col", "conv_ffn_w": "col", "w_ffn_down": "row"}
_REPLICATED = tuple(n for n in _WEIGHTS if n not in _SHARDED)


def _flat2(a):
    return a.reshape(-1, a.shape[-1])


def _train_step(inp):
    x_idx, y_idx, c_idx = _coords()
    W = {n: inp[n] for n in _WEIGHTS}
    x = inp["x"].reshape(-1, D)
    target = inp["loss_target"].reshape(-1, D)
    shard = {n: inp[n][0] for n in _SHARDED}

    hn, (g_in, g_cr, g_cf) = _norm_in(x, W["norm_mix_pre"], comm=_gather_two_level(
        [shard["w_in"].astype(bf16), shard["conv_rnn_w"], shard["conv_ffn_w"]]))
    w_in = _w_in_to_internal(_cols_to_full(g_in))
    cw_rnn, cw_ffn = _cols_to_full(g_cr), _cols_to_full(g_cf)
    behind_rnn = ("w_branch_rnn", "w_branch_att", "w_out", "w_ffn_down")
    behind_att = ("w_ffn_gate", "w_ffn_up")

    wa, wx = W["w_rg_a"][0], W["w_rg_x"][0]
    buckets = jnp.asarray(_bucket_maps())

    proj = _mm(hn, w_in[:, :C_GATE], "nn", f32, "mm_proj", 512, C_GATE // 2, 1024, cols_outer=True)
    gates = _mm(hn, w_in[:, C_GATE:], "nn", bf16, "mm_gates", 1024, 1024, 1024, cols_outer=True)
    rnn_saved, got = _rnn_fwd(proj, cw_rnn, W["conv_rnn_b"], wa, W["b_rg_a"], wx, W["b_rg_x"], W["lru_lambda"],
                              comm=_gather_direct([shard[n].astype(bf16) for n in behind_rnn]))
    h_rnn = rnn_saved[0]
    gathered = dict(zip(behind_rnn, got))
    bias = _bias_tables(W["rel_bias"], buckets)
    (o_att, lse), got = _att_fwd(proj, bias, comm=_gather_direct([shard[n].astype(bf16) for n in behind_att]))
    gathered.update(zip(behind_att, got))
    w_brnn = gathered["w_branch_rnn"].reshape(RNN_W, D)
    w_batt = _cols_to_full(gathered["w_branch_att"])
    w_out = gathered["w_out"].reshape(D, D)
    w_gate, w_up = _cols_to_full(gathered["w_ffn_gate"]), _cols_to_full(gathered["w_ffn_up"])
    w_down = gathered["w_ffn_down"].reshape(FFN_W, D)
    pr = _mm(h_rnn, w_brnn, "nn", bf16, "mm_pr", 1024, 1024, 1280)
    pa = _mm(o_att, w_batt, "nn", bf16, "mm_pa", 1024, 1024, 512)
    merged = _merge_fwd(gates, pr, pa)
    (mix, h1, hn2), _ = _mm_rows([(merged, w_out, 1024)], "nn", "mm_mix", 1024, _mid_fwd_rows, [x],
                                 [W["norm_mix_post"], W["norm_ffn_pre"]], [f32, f32, bf16], [])
    gpre = _mm(hn2, w_gate, "nn", bf16, "mm_gate", 1024, 1024, 1024, cols_outer=True)
    up = _mm(hn2, w_up, "nn", bf16, "mm_up", 1024, 1024, 1024, cols_outer=True)
    act = _ffn_act_fwd(gpre, up, cw_ffn, W["conv_ffn_b"])
    (dy, dff), (loss_part, dg_fpost) = _mm_rows([(act, w_down, 1024)], "nn", "mm_down", 1024, _final_rows,
                                                [h1, target], [W["norm_ffn_post"]], [f32, bf16], [1, D])

    grads = {}
    dact = _mm(dff, w_down, "nt", bf16, "mm_dact", 1024, 1024, 1024, cols_outer=True)
    grads["w_ffn_down"] = _rows_blocked(_mm(act, dff, "tn", bf16, "mm_dw_down", 1024, 1024, 2048))
    dgpre, dup, dcw_ffn, dcb_ffn = _ffn_act_bwd(gpre, up, cw_ffn, W["conv_ffn_b"], dact)
    grads["conv_ffn_w"] = _full_to_cols(dcw_ffn.astype(bf16))
    grads["w_ffn_gate"] = _full_to_cols(_mm(hn2, dgpre, "tn", bf16, "mm_dw_gate", 1024, 1024, 2048))
    grads["w_ffn_up"] = _full_to_cols(_mm(hn2, dup, "tn", bf16, "mm_dw_up", 1024, 1024, 2048))
    dhn2 = _mm_nt_sum([(dgpre, w_gate, 1024), (dup, w_up, 1024)], "mm_dhn2", 1024, 1024)
    dh1, dmix, dg_fpre, dg_post = _mid_bwd(dy, dhn2, h1, W["norm_ffn_pre"], mix, W["norm_mix_post"])
    dmerged = _mm(dmix, w_out, "nt", bf16, "mm_dmerged", 1024, 1024, 1024)
    grads["w_out"] = _rows_blocked(_mm(merged, dmix, "tn", bf16, "mm_dw_out", 1024, 1024, 2048))
    dprpa, dgates = _merge_bwd(gates, pr, pa, dmerged)
    dpr, dpa = dprpa[:, :D], dprpa[:, D:]
    dh_rnn = _mm(dpr, w_brnn, "nt", bf16, "mm_dh_rnn", 1024, 1280, 1024)
    grads["w_branch_rnn"] = _rows_blocked(_mm(h_rnn, dpr, "tn", bf16, "mm_dw_brnn", 1280, 1024, 1024))
    do_att = _mm(dpa, w_batt, "nt", f32, "mm_do_att", 1024, 512, 1024)
    grads["w_branch_att"] = _full_to_cols(_mm(o_att, dpa, "tn", bf16, "mm_dw_batt", 512, 1024, 2048))

    received = {}
    behind_att_bwd = ("w_ffn_down", "w_ffn_gate", "conv_ffn_w", "w_out")
    behind_rnn_bwd = ("w_ffn_up", "w_branch_rnn", "w_branch_att")
    (dqkv, dbias), got = _att_bwd(proj, bias, o_att, lse, do_att,
                                  comm=_scatter_direct([grads[n] for n in behind_att_bwd]))
    received.update(zip(behind_att_bwd, got))
    drel = _bias_grad(dbias, buckets)
    (dxr, dcw_rnn, dcb_rnn, dwa, dba, dwx, dbx, dlam), got = _rnn_bwd(
        proj, rnn_saved, dh_rnn, cw_rnn, wa, wx, W["lru_lambda"],
        comm=_scatter_direct([grads[n] for n in behind_rnn_bwd]))
    received.update(zip(behind_rnn_bwd, got))
    gsmall = {"rel_bias": drel, "norm_mix_post": dg_post, "conv_rnn_b": dcb_rnn, "w_rg_a": dwa.astype(bf16),
              "b_rg_a": dba, "w_rg_x": dwx.astype(bf16), "b_rg_x": dbx, "lru_lambda": dlam,
              "norm_ffn_pre": dg_fpre, "norm_ffn_post": dg_fpost, "conv_ffn_b": dcb_ffn}
    dw_in_a, parts = _mm(hn, dqkv, "tn", bf16, "mm_dw_in_a", 1024, 1280, 1024,
                         comm=_gather_direct([_flat2(gsmall[n]) for n in gsmall]))
    parts = dict(zip(gsmall, parts))
    dw_in = jnp.concatenate([_mm(hn, dxr, "tn", bf16, "mm_dw_in_r", 1024, 1280, 1024), dw_in_a,
                             _mm(hn, dgates, "tn", bf16, "mm_dw_in_g", 1024, 1024, 2048)], axis=1)
    grads["w_in"] = _full_to_cols(_w_in_from_internal(dw_in))
    grads["conv_rnn_w"] = _full_to_cols(dcw_rnn.astype(bf16))
    behind_dhn = ("w_in", "conv_rnn_w")
    dhn, got = _mm_nt_sum([(dxr, w_in[:, :C_ATT], 1280), (dqkv, w_in[:, C_ATT:C_GATE], 1280),
                           (dgates, w_in[:, C_GATE:], 1024)], "mm_dhn", 1024, 1024,
                          comm=_scatter_direct([grads[n] for n in behind_dhn]))
    received.update(zip(behind_dhn, got))
    dx, dg_pre = _in_bwd(dh1, dhn, x, W["norm_mix_pre"])
    parts["norm_mix_pre"], = _run(_gather_two_level([dg_pre]), "ag_norm_mix_pre")
    parts = [parts[n] for n in _REPLICATED]

    out = {}
    d_arr = jnp.reshape(4 * x_idx + 2 * y_idx + c_idx, (1,)).astype(jnp.int32)
    for n in _SHARDED:
        res = _adamw_sharded(grads[n], received[n], d_arr, shard[n], inp["m_" + n][0], inp["v_" + n][0], n)
        out[n] = [r[None] for r in res]
    small = _adamw_replicated(parts, *[[_flat2(inp[p + n]) for n in _REPLICATED] for p in ("", "m_", "v_")])
    for a, n in enumerate(_REPLICATED):
        out[n] = [small[i][a].reshape(inp[n].shape) for i in range(4)]

    loss = lax.psum(loss_part[0, 0], ("x", "y", "c"))
    outs = [loss, dx.reshape(inp["x"].shape)]
    for i in range(4):
        outs.extend(out[n][i] for n in _WEIGHTS)
    return tuple(outs)


def kernel(x, rel_bias, norm_mix_pre, norm_mix_post, w_in, conv_rnn_w, conv_rnn_b, w_rg_a, b_rg_a, w_rg_x, b_rg_x, lru_lambda, w_branch_rnn, w_branch_att, w_out, norm_ffn_pre, norm_ffn_post, w_ffn_gate, w_ffn_up, conv_ffn_w, conv_ffn_b, w_ffn_down, loss_target, m_rel_bias, m_norm_mix_pre, m_norm_mix_post, m_w_in, m_conv_rnn_w, m_conv_rnn_b, m_w_rg_a, m_b_rg_a, m_w_rg_x, m_b_rg_x, m_lru_lambda, m_w_branch_rnn, m_w_branch_att, m_w_out, m_norm_ffn_pre, m_norm_ffn_post, m_w_ffn_gate, m_w_ffn_up, m_conv_ffn_w, m_conv_ffn_b, m_w_ffn_down, v_rel_bias, v_norm_mix_pre, v_norm_mix_post, v_w_in, v_conv_rnn_w, v_conv_rnn_b, v_w_rg_a, v_b_rg_a, v_w_rg_x, v_b_rg_x, v_lru_lambda, v_w_branch_rnn, v_w_branch_att, v_w_out, v_norm_ffn_pre, v_norm_ffn_post, v_w_ffn_gate, v_w_ffn_up, v_conv_ffn_w, v_conv_ffn_b, v_w_ffn_down):
    vals = locals()
    names = list(_IN_NAMES) + ["loss_target"] + ["m_" + n for n in _WEIGHTS] + ["v_" + n for n in _WEIGHTS]
    return _train_step({n: vals[n] for n in names})
```

```python
import functools
import math

import numpy as np
import jax
import jax.numpy as jnp
from jax import lax
from jax.experimental import pallas as pl
from jax.experimental.pallas import tpu as pltpu

f32, bf16 = jnp.float32, jnp.bfloat16
SDS = jax.ShapeDtypeStruct
MESH = pl.DeviceIdType.MESH
ANY = pl.BlockSpec(memory_space=pl.ANY)

D = 1024
SEQ = 2048
RNN_W = 1280
RNN_BLOCKS = 10
LANES = 128
SUBLANES = 8
RNN_CONV = 4
LRU_C = 8.0
HD = 128
KVH = 4
DILATIONS = (1, 4, 16)
NG = 3
ATT_BLK = 128
NBLK_SEQ = SEQ // ATT_BLK
STAT_LANE = 64
ATT_UNROLL = 8
REL_BUCKETS = 32
REL_MAX_DIST = 2048
FFN_W = 3072
FFN_CONV = 3
EPS = 1e-6
IN_W = 5888
ATT_COLS = 5 * HD
C_ATT = RNN_W
C_GATE = RNN_W + KVH * ATT_COLS
NEG = -1e30

ADAM_LR, ADAM_B1, ADAM_B2, ADAM_EPS, ADAM_WD, ADAM_STEP = 0.001, 0.9, 0.999, 1e-08, 0.01, 10

VMEM_LIMIT_BYTES = 56 * 1024 * 1024
N_DEV = 8


def _params(sem=None):
    return pltpu.CompilerParams(dimension_semantics=sem, vmem_limit_bytes=VMEM_LIMIT_BYTES)


def _sigmoid(x):
    return 1.0 / (1.0 + jnp.exp(-x))


class _Comm:
    def __init__(self, inputs, out_shapes, sem_shapes, start, finish):
        self.inputs, self.out_shapes, self.sem_shapes = tuple(inputs), tuple(out_shapes), list(sem_shapes)
        self.start, self.finish = start, finish


def _call(body, args, *, name, grid, in_specs, out_specs, out_shape, scratch_shapes=(), semantics, comm=None):
    if comm is None:
        return pl.pallas_call(body, name=name, grid=grid, in_specs=list(in_specs), out_specs=tuple(out_specs),
                              out_shape=tuple(out_shape), scratch_shapes=list(scratch_shapes),
                              compiler_params=_params(semantics))(*args), ()
    n_in, n_out, n_scr = len(in_specs), len(out_shape), len(scratch_shapes)
    c_in, c_out = len(comm.inputs), len(comm.out_shapes)

    def fused(*refs):
        ins, refs = refs[:n_in], refs[n_in:]
        cin, refs = refs[:c_in], refs[c_in:]
        outs, refs = refs[:n_out], refs[n_out:]
        cout, refs = refs[:c_out], refs[c_out:]
        scr, csem = refs[:n_scr], refs[n_scr:]
        first = functools.reduce(jnp.logical_and, [pl.program_id(d) == 0 for d in range(len(grid))])
        last = functools.reduce(jnp.logical_and, [pl.program_id(d) == grid[d] - 1 for d in range(len(grid))])

        @pl.when(first)
        def _():
            comm.start(cin, cout, csem)

        body(*ins, *outs, *scr)

        @pl.when(last)
        def _():
            comm.finish(cin, cout, csem)

    res = pl.pallas_call(
        fused, name=name, grid=grid, in_specs=list(in_specs) + [ANY] * c_in,
        out_specs=tuple(out_specs) + (ANY,) * c_out, out_shape=tuple(out_shape) + comm.out_shapes,
        scratch_shapes=list(scratch_shapes) + comm.sem_shapes,
        compiler_params=_params(("arbitrary",) * len(grid)))(*args, *comm.inputs)
    return res[:n_out], res[n_out:]


_DIMS = {"nn": (((1,), (0,)), ((), ())), "nt": (((1,), (1,)), ((), ())), "tn": (((0,), (0,)), ((), ()))}


def _mm(a, b, mode, out_dtype, name, tm, tn, tk, cols_outer=False, comm=None):
    if mode == "nn":
        (M, K), (K2, N) = a.shape, b.shape
    elif mode == "nt":
        (M, K), (N, K2) = a.shape, b.shape
    else:
        (K, M), (K2, N) = a.shape, b.shape
    assert K == K2 and M % tm == 0 and N % tn == 0 and K % tk == 0, (name, a.shape, b.shape)
    nk = K // tk

    def body(a_ref, b_ref, o_ref, *scratch):
        part = lax.dot_general(a_ref[...].astype(bf16), b_ref[...].astype(bf16), _DIMS[mode],
                               preferred_element_type=f32)
        if nk == 1:
            o_ref[...] = part.astype(o_ref.dtype)
        else:
            acc_ref, = scratch
            k = pl.program_id(2)

            @pl.when(k == 0)
            def _():
                acc_ref[...] = part

            @pl.when(k > 0)
            def _():
                acc_ref[...] += part

            @pl.when(k == nk - 1)
            def _():
                o_ref[...] = acc_ref[...].astype(o_ref.dtype)

    def ij(f):
        return (lambda j, i, k: f(i, j, k)) if cols_outer else f

    if mode == "tn":
        a_spec = pl.BlockSpec((tk, tm), ij(lambda i, j, k: (k, i)))
    else:
        a_spec = pl.BlockSpec((tm, tk), ij(lambda i, j, k: (i, k)))
    if mode == "nt":
        b_spec = pl.BlockSpec((tn, tk), ij(lambda i, j, k: (j, k)))
    else:
        b_spec = pl.BlockSpec((tk, tn), ij(lambda i, j, k: (k, j)))
    o_spec = pl.BlockSpec((tm, tn), ij(lambda i, j, k: (i, j)))
    grid = (N // tn, M // tm, nk) if cols_outer else (M // tm, N // tn, nk)
    (out,), extra = _call(
        body, (a, b), name=name, out_shape=(SDS((M, N), out_dtype),), grid=grid, in_specs=[a_spec, b_spec],
        out_specs=(o_spec,), scratch_shapes=[pltpu.VMEM((tm, tn), f32)] if nk > 1 else [],
        semantics=("parallel", "parallel", "arbitrary"), comm=comm)
    return out if comm is None else (out, extra)


def _mm_nt_sum(pairs, name, tm, tn, comm=None):
    M, N = pairs[0][0].shape[0], pairs[0][1].shape[0]
    nks = [a.shape[1] // tk for a, _, tk in pairs]
    starts = [sum(nks[:p]) for p in range(len(pairs))]
    nk = sum(nks)

    def body(*refs):
        o_ref, acc_ref = refs[-2], refs[-1]
        k = pl.program_id(2)
        for p in range(len(pairs)):
            def product(p=p):
                return lax.dot_general(refs[2 * p][...], refs[2 * p + 1][...], _DIMS["nt"], preferred_element_type=f32)

            if p == 0:
                @pl.when(k == 0)
                def _():
                    acc_ref[...] = product()

            @pl.when((k >= max(starts[p], 1)) & (k < starts[p] + nks[p]))
            def _():
                acc_ref[...] += product()

        @pl.when(k == nk - 1)
        def _():
            o_ref[...] = acc_ref[...].astype(bf16)

    in_specs, args = [], []
    for (a, b, tk), k0, n in zip(pairs, starts, nks):
        assert a.shape[1] == b.shape[1] and a.shape[1] % tk == 0 and a.dtype == b.dtype == bf16, name
        chunk = lambda k, k0=k0, n=n: jnp.clip(k - k0, 0, n - 1)
        in_specs += [pl.BlockSpec((tm, tk), lambda i, j, k, c=chunk: (i, c(k))),
                     pl.BlockSpec((tn, tk), lambda i, j, k, c=chunk: (j, c(k)))]
        args += [a, b]
    o_spec = pl.BlockSpec((tm, tn), lambda i, j, k: (i, j))
    (out,), extra = _call(
        body, args, name=name, out_shape=(SDS((M, N), bf16),), grid=(M // tm, N // tn, nk), in_specs=in_specs,
        out_specs=(o_spec,), scratch_shapes=[pltpu.VMEM((tm, tn), f32)],
        semantics=("parallel", "parallel", "arbitrary"), comm=comm)
    return out if comm is None else (out, extra)


def _mm_rows(pairs, mode, name, tm, epilogue, rows_in, vecs_in, rows_out, vecs_out, comm=None):
    M = pairs[0][0].shape[0]
    N = pairs[0][1].shape[1 if mode == "nn" else 0]
    nks = [a.shape[1] // tk for a, _, tk in pairs]
    starts = [sum(nks[:p]) for p in range(len(pairs))]
    nk = sum(nks)
    n_rows_in, n_vecs_in, n_rows_out = len(rows_in), len(vecs_in), len(rows_out)

    def body(*refs):
        pair_refs, refs = refs[:2 * len(pairs)], refs[2 * len(pairs):]
        rin, refs = refs[:n_rows_in], refs[n_rows_in:]
        vin, refs = refs[:n_vecs_in], refs[n_vecs_in:]
        rout, refs = refs[:n_rows_out], refs[n_rows_out:]
        vout, acc_ref = refs[:-1], refs[-1]
        i, k = pl.program_id(0), pl.program_id(1)
        for p in range(len(pairs)):
            def product(p=p):
                return lax.dot_general(pair_refs[2 * p][...], pair_refs[2 * p + 1][...], _DIMS[mode],
                                       preferred_element_type=f32)

            if p == 0:
                @pl.when(k == 0)
                def _():
                    acc_ref[...] = product()

            @pl.when((k >= max(starts[p], 1)) & (k < starts[p] + nks[p]))
            def _():
                acc_ref[...] += product()

        @pl.when(k == nk - 1)
        def _():
            res = epilogue(acc_ref[...], *[r[...] for r in rin], *[v[...] for v in vin])
            for ref, val in zip(rout, res[:n_rows_out]):
                ref[...] = val.astype(ref.dtype)
            for ref, val in zip(vout, res[n_rows_out:]):
                @pl.when(i == 0)
                def _(ref=ref, val=val):
                    ref[...] = val

                @pl.when(i > 0)
                def _(ref=ref, val=val):
                    ref[...] += val

    in_specs, args = [], []
    for (a, b, tk), k0, n in zip(pairs, starts, nks):
        assert a.shape[1] % tk == 0 and a.dtype == b.dtype == bf16, name
        chunk = lambda k, k0=k0, n=n: jnp.clip(k - k0, 0, n - 1)
        in_specs.append(pl.BlockSpec((tm, tk), lambda i, k, c=chunk: (i, c(k))))
        if mode == "nn":
            in_specs.append(pl.BlockSpec((tk, N), lambda i, k, c=chunk: (c(k), 0)))
        else:
            in_specs.append(pl.BlockSpec((N, tk), lambda i, k, c=chunk: (0, c(k))))
        args += [a, b]
    row = lambda: pl.BlockSpec((tm, N), lambda i, k: (i, 0))
    vec = lambda w: pl.BlockSpec((1, w), lambda i, k: (0, 0))
    in_specs += [row() for _ in rows_in] + [vec(v.shape[1]) for v in vecs_in]
    outs, extra = _call(
        body, (*args, *rows_in, *vecs_in), name=name,
        out_shape=tuple(SDS((M, N), dt) for dt in rows_out) + tuple(SDS((1, w), f32) for w in vecs_out),
        grid=(M // tm, nk), in_specs=in_specs,
        out_specs=tuple(row() for _ in rows_out) + tuple(vec(w) for w in vecs_out),
        scratch_shapes=[pltpu.VMEM((tm, N), f32)], semantics=("arbitrary", "arbitrary"), comm=comm)
    res = (outs[:n_rows_out], outs[n_rows_out:])
    return res if comm is None else (res, extra)


ROW_TILE = 512


def _rms_fwd(x, g):
    r = lax.rsqrt(jnp.mean(x * x, axis=-1, keepdims=True) + EPS)
    return x * r * g


def _rms_bwd(x, g, dy):
    r = lax.rsqrt(jnp.mean(x * x, axis=-1, keepdims=True) + EPS)
    xh = x * r
    dxh = dy * g
    dx = r * (dxh - xh * jnp.mean(dxh * xh, axis=-1, keepdims=True))
    return dx, jnp.sum(dy * xh, axis=0, keepdims=True)


def _acc_out(ref, val):
    @pl.when(pl.program_id(0) == 0)
    def _():
        ref[...] = val

    @pl.when(pl.program_id(0) > 0)
    def _():
        ref[...] += val


def _row_spec(width=D):
    return pl.BlockSpec((ROW_TILE, width), lambda i: (i, 0))


def _vec_spec(width=D):
    return pl.BlockSpec((1, width), lambda i: (0, 0))


def _norm_in(x, g, comm=None):
    def body(x_ref, g_ref, o_ref):
        o_ref[...] = _rms_fwd(x_ref[...], g_ref[...]).astype(bf16)

    T = x.shape[0]
    (hn,), extra = _call(body, (x, g), name="norm_in", out_shape=(SDS((T, D), bf16),), grid=(T // ROW_TILE,),
                         in_specs=[_row_spec(), _vec_spec()], out_specs=(_row_spec(),), semantics=("parallel",),
                         comm=comm)
    return hn, extra


def _mid_bwd(dy, dhn2, h1, g_fpre, mix, g_post):
    def body(dy_ref, dhn2_ref, h1_ref, gf_ref, mix_ref, gp_ref, dh1_ref, dmix_ref, dgf_ref, dgp_ref):
        d1, dgf = _rms_bwd(h1_ref[...], gf_ref[...], dhn2_ref[...].astype(f32))
        dh1 = dy_ref[...] + d1
        dh1_ref[...] = dh1
        dmix, dgp = _rms_bwd(mix_ref[...], gp_ref[...], dh1)
        dmix_ref[...] = dmix.astype(bf16)
        _acc_out(dgf_ref, dgf)
        _acc_out(dgp_ref, dgp)

    T = dy.shape[0]
    return pl.pallas_call(
        body, name="mid_bwd", out_shape=(SDS((T, D), f32), SDS((T, D), bf16), SDS((1, D), f32), SDS((1, D), f32)),
        grid=(T // ROW_TILE,),
        in_specs=[_row_spec(), _row_spec(), _row_spec(), _vec_spec(), _row_spec(), _vec_spec()],
        out_specs=(_row_spec(), _row_spec(), _vec_spec(), _vec_spec()),
        compiler_params=_params(("arbitrary",)))(dy, dhn2, h1, g_fpre, mix, g_post)


def _in_bwd(dh1, dhn, x, g_pre):
    def body(dh1_ref, dhn_ref, x_ref, g_ref, dx_ref, dg_ref):
        d, dg = _rms_bwd(x_ref[...], g_ref[...], dhn_ref[...].astype(f32))
        dx_ref[...] = dh1_ref[...] + d
        _acc_out(dg_ref, dg)

    T = x.shape[0]
    return pl.pallas_call(
        body, name="in_bwd", out_shape=(SDS((T, D), f32), SDS((1, D), f32)), grid=(T // ROW_TILE,),
        in_specs=[_row_spec(), _row_spec(), _row_spec(), _vec_spec()], out_specs=(_row_spec(), _vec_spec()),
        compiler_params=_params(("arbitrary",)))(dh1, dhn, x, g_pre)


def _mid_fwd_rows(mix, x, g_post, g_fpre):
    h1 = x + _rms_fwd(mix, g_post)
    return mix, h1, _rms_fwd(h1, g_fpre)


def _final_rows(ff, h1, target, g_fpost):
    e = h1 + _rms_fwd(ff, g_fpost) - target
    part = jnp.sum(jnp.sum(e * e, axis=1, keepdims=True), axis=0, keepdims=True) * (0.5 / D)
    dy = e * (1.0 / D)
    dff, dg = _rms_bwd(ff, g_fpost, dy)
    return dy, dff, part, dg


def _shift_dn(x, d, row, fill=0.0):
    if d == 0:
        return x
    y = pltpu.roll(x, d, 0)
    head = jnp.where(row[:SUBLANES] >= d, y[:SUBLANES], fill)
    return jnp.concatenate([head, y[SUBLANES:]], axis=0)


def _shift_up(x, d, row, fill=0.0):
    if d == 0:
        return x
    n = x.shape[0]
    y = pltpu.roll(x, n - d, 0)
    tail = jnp.where(row[:SUBLANES] < SUBLANES - d, y[n - SUBLANES:], fill)
    return jnp.concatenate([y[:n - SUBLANES], tail], axis=0)


def _conv_fwd(x, w_ref, b, row):
    K = w_ref.shape[0]
    y = b
    for k in range(K):
        y = y + w_ref[k:k + 1, :] * _shift_dn(x, K - 1 - k, row)
    return y


def _conv_bwd(x, w_ref, dy, row):
    K = w_ref.shape[0]
    dx = jnp.zeros_like(dy)
    dws = []
    for k in range(K):
        dx = dx + w_ref[k:k + 1, :] * _shift_up(dy, K - 1 - k, row)
        dws.append(jnp.sum(dy * _shift_dn(x, K - 1 - k, row), axis=0, keepdims=True))
    return dx, dws, jnp.sum(dy, axis=0, keepdims=True)


def _scan_fwd(a, u, row):
    n = a.shape[0]
    d = 1
    while d < n:
        last = 2 * d >= n
        if d < SUBLANES:
            u = u + a * _shift_dn(u, d, row)
            if not last:
                a = a * _shift_dn(a, d, row, fill=1.0)
        else:
            u = jnp.concatenate([u[:d], u[d:] + a[d:] * u[:n - d]], axis=0)
            if not last:
                a = jnp.concatenate([a[:d], a[d:] * a[:n - d]], axis=0)
        d *= 2
    return u


def _scan_bwd(b, u, row):
    n = b.shape[0]
    d = 1
    while d < n:
        last = 2 * d >= n
        if d < SUBLANES:
            u = u + b * _shift_up(u, d, row)
            if not last:
                b = b * _shift_up(b, d, row, fill=1.0)
        else:
            u = jnp.concatenate([u[:n - d] + b[:n - d] * u[d:], u[n - d:]], axis=0)
            if not last:
                b = jnp.concatenate([b[:n - d] * b[d:], b[n - d:]], axis=0)
        d *= 2
    return u


def _neg_expm1(z):
    series = -z * (1.0 + z * (0.5 + z * (1.0 / 6.0 + z * (1.0 / 24.0 + z * (1.0 / 120.0)))))
    return jnp.where(z > -0.1, series, 1.0 - jnp.exp(z))


def _softplus_neg(lam):
    z = -lam
    return jnp.maximum(z, 0.0) + jnp.log(1.0 + jnp.exp(-jnp.abs(z)))


def _rnn_specs(B):
    blk = lambda: pl.BlockSpec((SEQ, LANES), lambda b, n: (b, n))
    return dict(
        act=blk,
        convw=pl.BlockSpec((RNN_CONV, LANES), lambda b, n: (0, n)),
        vec=lambda: pl.BlockSpec((1, LANES), lambda b, n: (0, n)),
        gate=lambda: pl.BlockSpec((None, LANES, LANES), lambda b, n: (n, 0, 0)),
    )


def _rnn_fwd(proj, cw, cb, wa, ba, wx, bx, lam, comm=None):
    T = proj.shape[0]
    B = T // SEQ

    def body(x_ref, cw_ref, cb_ref, wa_ref, ba_ref, wx_ref, bx_ref, lam_ref, h_ref, xc_ref, r_ref, i_ref, a_ref, s_ref):
        row = lax.broadcasted_iota(jnp.int32, (SEQ, LANES), 0)
        xc = _conv_fwd(x_ref[...], cw_ref, cb_ref[...], row)
        xcb = xc.astype(bf16)
        r = _sigmoid(jnp.dot(xcb, wa_ref[...].astype(bf16), preferred_element_type=f32) + ba_ref[...])
        i = _sigmoid(jnp.dot(xcb, wx_ref[...].astype(bf16), preferred_element_type=f32) + bx_ref[...])
        log_a = (-LRU_C * _softplus_neg(lam_ref[...])) * r
        a = jnp.exp(log_a)
        s = jnp.sqrt(_neg_expm1(2.0 * log_a))
        xc_ref[...], r_ref[...], i_ref[...], a_ref[...], s_ref[...] = xc, r, i, a, s
        h_ref[...] = _scan_fwd(a, s * (i * xc), row)

    sp_ = _rnn_specs(B)
    return _call(
        body, (proj, cw, cb, wa, ba, wx, bx, lam), name="rnn_fwd", out_shape=(SDS((T, RNN_W), f32),) * 6,
        grid=(B, RNN_BLOCKS),
        in_specs=[sp_["act"](), sp_["convw"], sp_["vec"](), sp_["gate"](), sp_["vec"](), sp_["gate"](),
                  sp_["vec"](), sp_["vec"]()],
        out_specs=tuple(sp_["act"]() for _ in range(6)), semantics=("parallel", "parallel"), comm=comm)


def _rnn_bwd(proj, saved, dh, cw, wa, wx, lam, comm=None):
    T = proj.shape[0]
    B = T // SEQ

    def body(x_ref, h_ref, xc_ref, r_ref, i_ref, a_ref, s_ref, dh_ref, cw_ref, wa_ref, wx_ref, lam_ref,
             dx_ref, dcw_ref, dcb_ref, dwa_ref, dba_ref, dwx_ref, dbx_ref, dlam_ref):
        row = lax.broadcasted_iota(jnp.int32, (SEQ, LANES), 0)
        xr = x_ref[...]
        wa, wx, lam = wa_ref[...], wx_ref[...], lam_ref[...]
        xc, r, i, a, s = xc_ref[...], r_ref[...], i_ref[...], a_ref[...], s_ref[...]
        xcb = xc.astype(bf16)
        sp = _softplus_neg(lam)
        hprev = _shift_dn(h_ref[...], 1, row)
        g = _scan_bwd(_shift_up(a, 1, row), dh_ref[...].astype(f32), row)
        da = g * hprev
        ds = g * (i * xc)
        di = g * (s * xc)
        dxc = g * (s * i)
        dla = da * a - ds * (a * a) / s
        dr = dla * (-LRU_C * sp)
        dsp = jnp.sum(dla * (-LRU_C * r), axis=0, keepdims=True)
        dlam = -dsp * _sigmoid(-lam)
        dga = dr * r * (1.0 - r)
        dgx = di * i * (1.0 - i)
        dgab, dgxb = dga.astype(bf16), dgx.astype(bf16)
        dwa = lax.dot_general(xcb, dgab, _DIMS["tn"], preferred_element_type=f32)
        dwx = lax.dot_general(xcb, dgxb, _DIMS["tn"], preferred_element_type=f32)
        dxc = dxc + lax.dot_general(dgab, wa.astype(bf16), _DIMS["nt"], preferred_element_type=f32)
        dxc = dxc + lax.dot_general(dgxb, wx.astype(bf16), _DIMS["nt"], preferred_element_type=f32)
        dx, dws, db = _conv_bwd(xr, cw_ref, dxc, row)
        dx_ref[...] = dx.astype(bf16)
        first = pl.program_id(1) == 0

        def acc(ref, val):
            @pl.when(first)
            def _():
                ref[...] = val

            @pl.when(jnp.logical_not(first))
            def _():
                ref[...] += val

        for k in range(RNN_CONV):
            acc(dcw_ref.at[k:k + 1, :], dws[k])
        acc(dcb_ref, db)
        acc(dwa_ref, dwa)
        acc(dba_ref, jnp.sum(dga, axis=0, keepdims=True))
        acc(dwx_ref, dwx)
        acc(dbx_ref, jnp.sum(dgx, axis=0, keepdims=True))
        acc(dlam_ref, dlam)

    blk = lambda: pl.BlockSpec((SEQ, LANES), lambda n, b: (b, n))
    convw = lambda: pl.BlockSpec((RNN_CONV, LANES), lambda n, b: (0, n))
    vec = lambda: pl.BlockSpec((1, LANES), lambda n, b: (0, n))
    gate = lambda: pl.BlockSpec((None, LANES, LANES), lambda n, b: (n, 0, 0))
    vshape = SDS((1, RNN_W), f32)
    gshape = SDS((RNN_BLOCKS, LANES, LANES), f32)
    return _call(
        body, (proj, *saved, dh, cw, wa, wx, lam), name="rnn_bwd",
        out_shape=(SDS((T, RNN_W), bf16), SDS((RNN_CONV, RNN_W), f32), vshape, gshape, vshape, gshape, vshape, vshape),
        grid=(RNN_BLOCKS, B),
        in_specs=[blk() for _ in range(8)] + [convw(), gate(), gate(), vec()],
        out_specs=(blk(), convw(), vec(), gate(), vec(), gate(), vec(), vec()),
        semantics=("parallel", "arbitrary"), comm=comm)


def _t5_bucket(dist):
    max_exact = REL_BUCKETS // 2
    d = np.maximum(dist, 1).astype(np.float32)
    large = max_exact + np.log(d / max_exact) / math.log(REL_MAX_DIST / max_exact) * (REL_BUCKETS - max_exact)
    large = np.minimum(large.astype(np.int32), REL_BUCKETS - 1)
    return np.where(dist < max_exact, dist, large).astype(np.int32)


def _bucket_maps():
    qi = np.arange(ATT_BLK)[:, None]
    kj = np.arange(2 * ATT_BLK)[None, :]
    delta = ATT_BLK + qi - kj
    valid = (delta >= 0) & (delta <= ATT_BLK)
    maps = [np.where(valid, _t5_bucket(np.maximum(delta, 0) * r), -1) for r in DILATIONS]
    return np.stack(maps).astype(np.int32)


def _bias_tables(rel_bias, buckets):
    def body(rb_ref, bk_ref, o_ref):
        for g in range(NG):
            bk = bk_ref[g]
            for h in range(KVH):
                acc = jnp.full(bk.shape, NEG, f32)
                for b in range(REL_BUCKETS):
                    acc = jnp.where(bk == b, rb_ref[b, g * KVH + h], acc)
                o_ref[h, g] = acc

    return pl.pallas_call(
        body, name="bias_tables", out_shape=SDS((KVH, NG, ATT_BLK, 2 * ATT_BLK), f32),
        in_specs=[pl.BlockSpec(memory_space=pltpu.SMEM), pl.BlockSpec(memory_space=pltpu.VMEM)],
        out_specs=pl.BlockSpec(memory_space=pltpu.VMEM), compiler_params=_params())(rel_bias, buckets)


def _bias_grad(dbias, buckets):
    def body(db_ref, bk_ref, o_ref):
        rr = lax.broadcasted_iota(jnp.int32, (REL_BUCKETS, NG * KVH), 0)
        cc = lax.broadcasted_iota(jnp.int32, (REL_BUCKETS, NG * KVH), 1)
        out = jnp.zeros((REL_BUCKETS, NG * KVH), f32)
        for g in range(NG):
            bk = bk_ref[g]
            for h in range(KVH):
                d = db_ref[h, g]
                for b in range(REL_BUCKETS):
                    m = jnp.where(bk == b, d, 0.0)
                    s = jnp.sum(jnp.sum(m, axis=1, keepdims=True), axis=0, keepdims=True)
                    out = jnp.where((rr == b) & (cc == g * KVH + h), s, out)
        o_ref[...] = out

    return pl.pallas_call(body, name="bias_grad", out_shape=SDS((REL_BUCKETS, NG * KVH), f32),
                          compiler_params=_params())(dbias, buckets)


def _to_sub(dst_ref, src_ref, r, dtype, offset=0):
    M = SEQ // r
    for c in range(r):
        if r == 1:
            v = src_ref[...]
        else:
            v = src_ref[pl.ds(c, M, stride=r), :]
        dst_ref[pl.ds(offset + c * M, M), :] = v.astype(dtype)


def _from_sub(dst_ref, src_ref, r, accumulate=False, offset=0):
    M = SEQ // r
    for c in range(r):
        v = src_ref[pl.ds(offset + c * M, M), :]
        idx = slice(None) if r == 1 else pl.ds(c, M, stride=r)
        if accumulate:
            dst_ref[idx, :] = dst_ref[idx, :] + v
        else:
            dst_ref[idx, :] = v


_COL = lambda k: slice(k * HD, (k + 1) * HD)
SCALE = HD ** -0.5


def _qkv_spec(k, bh):
    def index(*ids):
        b, h = bh(*ids)
        return (b, C_ATT // HD + 5 * h + k)

    return pl.BlockSpec((SEQ, HD), index)


def _key_window(bias_ref, g, nb):
    if nb == 1:
        bias_own = bias_ref[g, :, ATT_BLK:2 * ATT_BLK]
        return lambda j: (pl.ds(pl.multiple_of((j + 1) * ATT_BLK, ATT_BLK), ATT_BLK), bias_own)
    bias_g = bias_ref[g]
    col = lax.broadcasted_iota(jnp.int32, bias_g.shape, 1)
    bias_first = jnp.where(col >= ATT_BLK, bias_g, NEG)
    return lambda j: (pl.ds(pl.multiple_of(j * ATT_BLK, ATT_BLK), 2 * ATT_BLK),
                      jnp.where(j % nb != 0, bias_g, bias_first))


def _att_fwd(proj, bias, comm=None):
    T = proj.shape[0]
    B = T // SEQ

    def body(q0_ref, q1_ref, q2_ref, k_ref, v_ref, bias_ref, o_ref, lse_ref, *rest):
        saved, (qp, kp, vp, kt, op, lp, og, lg) = rest[:3 * (NG - 1)], rest[3 * (NG - 1):]
        q_refs = (q0_ref, q1_ref, q2_ref)
        kp[0:ATT_BLK, :] = jnp.zeros((ATT_BLK, HD), bf16)
        vp[0:ATT_BLK, :] = jnp.zeros((ATT_BLK, HD), bf16)
        for g, r in enumerate(DILATIONS):
            nb = NBLK_SEQ // r
            _to_sub(qp, q_refs[g], r, bf16)
            _to_sub(kp, k_ref, r, bf16, offset=ATT_BLK)
            _to_sub(vp, v_ref, r, bf16, offset=ATT_BLK)
            if r > 1:
                sq, sk, sv = saved[3 * (g - 1):3 * g]
                sq[...], sk[...], sv[...] = qp[...], kp[ATT_BLK:, :], vp[ATT_BLK:, :]
            kt[...] = kp[...].T
            keys = _key_window(bias_ref, g, nb)

            def step(j, carry):
                cur = pl.ds(pl.multiple_of(j * ATT_BLK, ATT_BLK), ATT_BLK)
                win, bias_j = keys(j)
                s = jnp.dot(qp[cur, :], kt[:, win], preferred_element_type=f32) * SCALE + bias_j
                m = jnp.max(s, axis=-1, keepdims=True)
                p = jnp.exp(s - m)
                den = jnp.sum(p, axis=-1, keepdims=True)
                o = jnp.dot(p.astype(bf16), vp[win, :], preferred_element_type=f32)
                op[cur, :] = o / den
                lp[cur, :] = jnp.broadcast_to(m + jnp.log(den), (ATT_BLK, HD))
                return carry

            lax.fori_loop(0, NBLK_SEQ, step, 0, unroll=ATT_UNROLL)
            _from_sub(og.at[g], op, r)
            _from_sub(lg.at[g], lp, r)
        l0, l1, l2 = lg[0], lg[1], lg[2]
        mx = jnp.maximum(jnp.maximum(l0, l1), l2)
        e0, e1, e2 = jnp.exp(l0 - mx), jnp.exp(l1 - mx), jnp.exp(l2 - mx)
        den = e0 + e1 + e2
        o_ref[...] = (e0 * og[0] + e1 * og[1] + e2 * og[2]) / den
        lse_ref[...] = mx + jnp.log(den)

    return _call(
        body, (proj, proj, proj, proj, proj, bias), name="att_fwd",
        out_shape=(SDS((T, KVH * HD), f32), SDS((KVH, T, HD), f32)) + (SDS((T, KVH * HD), bf16),) * (3 * (NG - 1)),
        grid=(B, KVH),
        in_specs=[_qkv_spec(k, lambda b, h: (b, h)) for k in range(5)]
                 + [pl.BlockSpec((None, NG, ATT_BLK, 2 * ATT_BLK), lambda b, h: (h, 0, 0, 0))],
        out_specs=(pl.BlockSpec((SEQ, HD), lambda b, h: (b, h)),
                   pl.BlockSpec((None, SEQ, HD), lambda b, h: (h, b, 0)))
                  + tuple(pl.BlockSpec((SEQ, HD), lambda b, h: (b, h)) for _ in range(3 * (NG - 1))),
        scratch_shapes=[pltpu.VMEM((SEQ, HD), bf16)] + [pltpu.VMEM((SEQ + ATT_BLK, HD), bf16)] * 2
                       + [pltpu.VMEM((HD, SEQ + ATT_BLK), bf16)]
                       + [pltpu.VMEM((SEQ, HD), f32)] * 2 + [pltpu.VMEM((NG, SEQ, HD), f32)] * 2,
        semantics=("parallel", "parallel"), comm=comm)


def _att_bwd(proj, saved, bias, o, lse, do, comm=None):
    T = proj.shape[0]
    B = T // SEQ
    n_saved = 3 * (NG - 1)

    def body(q0_ref, k_ref, v_ref, *rest):
        saved_refs, rest = rest[:n_saved], rest[n_saved:]
        (bias_ref, o_ref, lse_ref, do_ref, dx_ref, db_ref,
         qp, kp, vp, dop, qt, kt, vt, dot, lp, dqp, dkt, dvt, dln, nat, dkn, dvn) = rest
        first = pl.program_id(1) == 0

        @pl.when(first)
        def _():
            db_ref[...] = jnp.zeros_like(db_ref)

        lane = lax.broadcasted_iota(jnp.int32, (SEQ, HD), 1)
        dln[...] = jnp.where(lane < STAT_LANE, lse_ref[...],
                             jnp.sum(do_ref[...] * o_ref[...], axis=-1, keepdims=True))
        dkn[...] = jnp.zeros_like(dkn)
        dvn[...] = jnp.zeros_like(dvn)
        kp[0:ATT_BLK, :] = jnp.zeros((ATT_BLK, HD), bf16)
        vp[0:ATT_BLK, :] = jnp.zeros((ATT_BLK, HD), bf16)
        for g, r in enumerate(DILATIONS):
            nb = NBLK_SEQ // r
            if r == 1:
                qp[...] = q0_ref[...].astype(bf16)
                kp[ATT_BLK:, :] = k_ref[...].astype(bf16)
                vp[ATT_BLK:, :] = v_ref[...].astype(bf16)
            else:
                sq, sk, sv = saved_refs[3 * (g - 1):3 * g]
                qp[...], kp[ATT_BLK:, :], vp[ATT_BLK:, :] = sq[...], sk[...], sv[...]
            _to_sub(dop, do_ref, r, bf16)
            _to_sub(lp, dln, r, f32)
            qt[...], kt[...], vt[...], dot[...] = qp[...].T, kp[...].T, vp[...].T, dop[...].T
            dkt[...] = jnp.zeros_like(dkt)
            dvt[...] = jnp.zeros_like(dvt)
            keys = _key_window(bias_ref, g, nb)
            db_cols = slice(ATT_BLK, 2 * ATT_BLK) if nb == 1 else slice(None)

            def step(j, carry):
                cur = pl.ds(pl.multiple_of(j * ATT_BLK, ATT_BLK), ATT_BLK)
                win, bias_j = keys(j)
                s = jnp.dot(qp[cur, :], kt[:, win], preferred_element_type=f32) * SCALE + bias_j
                p = jnp.exp(s - lp[cur, 0:1])
                dp = jnp.dot(dop[cur, :], vt[:, win], preferred_element_type=f32)
                ds = p * (dp - lp[cur, STAT_LANE:STAT_LANE + 1])
                db_ref[g, :, db_cols] += ds
                dsb, pb = ds.astype(bf16), p.astype(bf16)
                dqp[cur, :] = jnp.dot(dsb, kp[win, :], preferred_element_type=f32) * SCALE
                dkt[:, win] += jnp.dot(qt[:, cur], dsb, preferred_element_type=f32) * SCALE
                dvt[:, win] += jnp.dot(dot[:, cur], pb, preferred_element_type=f32)
                return carry

            lax.fori_loop(0, NBLK_SEQ, step, 0, unroll=ATT_UNROLL)
            _from_sub(nat, dqp, r)
            dx_ref[:, _COL(g)] = nat[...].astype(bf16)
            dqp[...] = dkt[:, ATT_BLK:].T
            _from_sub(dkn, dqp, r, accumulate=True)
            dqp[...] = dvt[:, ATT_BLK:].T
            _from_sub(dvn, dqp, r, accumulate=True)
        dx_ref[:, _COL(3)] = dkn[...].astype(bf16)
        dx_ref[:, _COL(4)] = dvn[...].astype(bf16)

    blk = lambda: pl.BlockSpec((SEQ, HD), lambda h, b: (b, h))
    bias_spec = lambda: pl.BlockSpec((None, NG, ATT_BLK, 2 * ATT_BLK), lambda h, b: (h, 0, 0, 0))
    pad = lambda dtype: pltpu.VMEM((SEQ + ATT_BLK, HD), dtype)
    pad_t = lambda dtype: pltpu.VMEM((HD, SEQ + ATT_BLK), dtype)
    seq_t = pltpu.VMEM((HD, SEQ), bf16)
    return _call(
        body, (proj, proj, proj, *saved, bias, o, lse, do), name="att_bwd",
        out_shape=(SDS((T, KVH * ATT_COLS), bf16), SDS((KVH, NG, ATT_BLK, 2 * ATT_BLK), f32)), grid=(KVH, B),
        in_specs=[_qkv_spec(k, lambda h, b: (b, h)) for k in (0, 3, 4)] + [blk() for _ in range(n_saved)]
                 + [bias_spec(), blk(), pl.BlockSpec((None, SEQ, HD), lambda h, b: (h, b, 0)), blk()],
        out_specs=(pl.BlockSpec((SEQ, ATT_COLS), lambda h, b: (b, h)), bias_spec()),
        scratch_shapes=[pltpu.VMEM((SEQ, HD), bf16), pad(bf16), pad(bf16), pltpu.VMEM((SEQ, HD), bf16),
                        seq_t, pad_t(bf16), pad_t(bf16), seq_t]
                       + [pltpu.VMEM((SEQ, HD), f32)] * 2 + [pad_t(f32)] * 2 + [pltpu.VMEM((SEQ, HD), f32)] * 4,
        semantics=("parallel", "arbitrary"), comm=comm)


MERGE_ROWS, MERGE_COLS = 1024, 512


def _merge_fwd(gates, pr, pa):
    def body(gr_ref, ga_ref, pr_ref, pa_ref, o_ref):
        o_ref[...] = (_sigmoid(gr_ref[...].astype(f32)) * pr_ref[...].astype(f32)
                      + _sigmoid(ga_ref[...].astype(f32)) * pa_ref[...].astype(f32)).astype(bf16)

    T = gates.shape[0]
    cols = lambda off: pl.BlockSpec((MERGE_ROWS, MERGE_COLS), lambda i, j: (i, off + j))
    return pl.pallas_call(body, name="merge_fwd", out_shape=SDS((T, D), bf16),
                          grid=(T // MERGE_ROWS, D // MERGE_COLS),
                          in_specs=[cols(0), cols(D // MERGE_COLS), cols(0), cols(0)], out_specs=cols(0),
                          compiler_params=_params(("parallel", "parallel")))(gates, gates, pr, pa)


def _merge_bwd(gates, pr, pa, dm):
    nj = D // MERGE_COLS

    def body(g_ref, pr_ref, pa_ref, dm_ref, dp_ref, dg_ref):
        dm_ = dm_ref[...].astype(f32)
        s = _sigmoid(g_ref[...].astype(f32))
        p = jnp.where(pl.program_id(1) < nj, pr_ref[...], pa_ref[...]).astype(f32)
        dp_ref[...] = (dm_ * s).astype(bf16)
        dg_ref[...] = (dm_ * p * s * (1.0 - s)).astype(bf16)

    T = gates.shape[0]
    blk = (MERGE_ROWS, MERGE_COLS)
    wrap = pl.BlockSpec(blk, lambda i, j: (i, j % nj))
    pr_spec = pl.BlockSpec(blk, lambda i, j: (i, jnp.minimum(j, nj - 1)))
    pa_spec = pl.BlockSpec(blk, lambda i, j: (i, jnp.maximum(j - nj, 0)))
    out = pl.BlockSpec(blk, lambda i, j: (i, j))
    return pl.pallas_call(
        body, name="merge_bwd", out_shape=(SDS((T, 2 * D), bf16), SDS((T, 2 * D), bf16)),
        grid=(T // MERGE_ROWS, 2 * nj),
        in_specs=[out, pr_spec, pa_spec, wrap], out_specs=(out, out),
        compiler_params=_params(("parallel", "parallel")))(gates, pr, pa, dm)


FFN_COLS = 256
GELU_C = math.sqrt(2.0 / math.pi)
GELU_A = 0.044715


def _gelu_parts(x):
    t = jnp.tanh(GELU_C * (x + GELU_A * x * x * x))
    return 0.5 * x * (1.0 + t), t


def _ffn_act_fwd(gpre, up, cw, cb):
    def body(g_ref, u_ref, cw_ref, cb_ref, o_ref):
        row = lax.broadcasted_iota(jnp.int32, (SEQ, FFN_COLS), 0)
        gate = _conv_fwd(g_ref[...].astype(f32), cw_ref, cb_ref[...], row)
        o_ref[...] = (_gelu_parts(gate)[0] * u_ref[...].astype(f32)).astype(bf16)

    T = gpre.shape[0]
    blk = lambda: pl.BlockSpec((SEQ, FFN_COLS), lambda b, j: (b, j))
    return pl.pallas_call(
        body, name="ffn_act_fwd", out_shape=SDS((T, FFN_W), bf16), grid=(T // SEQ, FFN_W // FFN_COLS),
        in_specs=[blk(), blk(), pl.BlockSpec((FFN_CONV, FFN_COLS), lambda b, j: (0, j)),
                  pl.BlockSpec((1, FFN_COLS), lambda b, j: (0, j))],
        out_specs=blk(), compiler_params=_params(("parallel", "parallel")))(gpre, up, cw, cb)


def _ffn_act_bwd(gpre, up, cw, cb, dact):
    def body(g_ref, u_ref, cw_ref, cb_ref, da_ref, dg_ref, du_ref, dcw_ref, dcb_ref):
        row = lax.broadcasted_iota(jnp.int32, (SEQ, FFN_COLS), 0)
        gp = g_ref[...].astype(f32)
        gate = _conv_fwd(gp, cw_ref, cb_ref[...], row)
        gel, t = _gelu_parts(gate)
        da = da_ref[...].astype(f32)
        du_ref[...] = (da * gel).astype(bf16)
        dgel = 0.5 * (1.0 + t) + 0.5 * gate * (1.0 - t * t) * (GELU_C * (1.0 + 3.0 * GELU_A * gate * gate))
        dgate = da * u_ref[...].astype(f32) * dgel
        dx, dws, db = _conv_bwd(gp, cw_ref, dgate, row)
        dg_ref[...] = dx.astype(bf16)
        first = pl.program_id(1) == 0

        def acc(ref, val):
            @pl.when(first)
            def _():
                ref[...] = val

            @pl.when(jnp.logical_not(first))
            def _():
                ref[...] += val

        for k in range(FFN_CONV):
            acc(dcw_ref.at[k:k + 1, :], dws[k])
        acc(dcb_ref, db)

    T = gpre.shape[0]
    blk = lambda: pl.BlockSpec((SEQ, FFN_COLS), lambda j, b: (b, j))
    cws = lambda: pl.BlockSpec((FFN_CONV, FFN_COLS), lambda j, b: (0, j))
    cbs = lambda: pl.BlockSpec((1, FFN_COLS), lambda j, b: (0, j))
    return pl.pallas_call(
        body, name="ffn_act_bwd",
        out_shape=(SDS((T, FFN_W), bf16), SDS((T, FFN_W), bf16), SDS((FFN_CONV, FFN_W), f32), SDS((1, FFN_W), f32)),
        grid=(FFN_W // FFN_COLS, T // SEQ),
        in_specs=[blk(), blk(), cws(), cbs(), blk()], out_specs=(blk(), blk(), cws(), cbs()),
        compiler_params=_params(("parallel", "arbitrary")))(gpre, up, cw, cb, dact)


def _coords():
    return lax.axis_index("x"), lax.axis_index("y"), lax.axis_index("c")


def _dev_index(dev):
    return 4 * dev[0] + 2 * dev[1] + dev[2]


def _dma_sems(n):
    return [pltpu.SemaphoreType.DMA((n,)), pltpu.SemaphoreType.DMA((n,))]


def _gather_two_level(arrays):
    n = len(arrays)

    def plan(ins, outs, sems):
        send_sems, recv_sems, local_sems = sems
        x, y, c = _coords()
        me, sibling = (x, y, c), (x, y, 1 - c)
        chips = [(1 - x, y), (x, 1 - y), (1 - x, 1 - y)]

        def copy(a, k, block, to, own=False):
            dst = outs[a].at[_dev_index(block)]
            return pltpu.make_async_remote_copy(
                src_ref=ins[a] if own else dst, dst_ref=dst, send_sem=send_sems.at[7 * a + k],
                recv_sem=recv_sems.at[7 * a + k], device_id=to, device_id_type=MESH)

        mine = [pltpu.make_async_copy(ins[a], outs[a].at[_dev_index(me)], local_sems.at[a]) for a in range(n)]
        first = [copy(a, 0, me, sibling, own=True) for a in range(n)]
        first += [copy(a, 1 + j, me, (*chip, c), own=True) for a in range(n) for j, chip in enumerate(chips)]
        passed = [[copy(a, 4 + j, (*chip, c), sibling) for a in range(n)] for j, chip in enumerate(chips)]
        arrive_ici = [[copy(a, 1 + j, (*chip, c), me) for a in range(n)] for j, chip in enumerate(chips)]
        arrive_d2d = [copy(a, 0, sibling, me) for a in range(n)]
        arrive_d2d += [copy(a, 4 + j, (*chip, 1 - c), me) for a in range(n) for j, chip in enumerate(chips)]
        return mine, first, passed, arrive_ici, arrive_d2d

    def start(ins, outs, sems):
        mine, first, _, _, _ = plan(ins, outs, sems)
        for cp in mine + first:
            cp.start()

    def finish(ins, outs, sems):
        mine, first, passed, arrive_ici, arrive_d2d = plan(ins, outs, sems)
        for j in range(3):
            for cp in arrive_ici[j]:
                cp.wait_recv()
            for cp in passed[j]:
                cp.start()
        for cp in arrive_d2d:
            cp.wait_recv()
        for cp in first + [cp for group in passed for cp in group]:
            cp.wait_send()
        for cp in mine:
            cp.wait()

    return _Comm(arrays, [SDS((N_DEV,) + a.shape, a.dtype) for a in arrays],
                 _dma_sems(7 * n) + [pltpu.SemaphoreType.DMA((n,))], start, finish)


def _gather_direct(arrays):
    n = len(arrays)

    def plan(ins, outs, sems):
        send_sems, recv_sems, local_sems = sems
        x, y, c = _coords()
        me = (x, y, c)
        mine = [pltpu.make_async_copy(ins[a], outs[a].at[_dev_index(me)], local_sems.at[a]) for a in range(n)]
        sends, arrivals = [], []
        for a in range(n):
            for k in range(1, N_DEV):
                peer = (1 - x if k & 4 else x, 1 - y if k & 2 else y, 1 - c if k & 1 else c)
                s = 7 * a + k - 1
                for slot, out in ((me, sends), (peer, arrivals)):
                    out.append(pltpu.make_async_remote_copy(
                        src_ref=ins[a], dst_ref=outs[a].at[_dev_index(slot)], send_sem=send_sems.at[s],
                        recv_sem=recv_sems.at[s], device_id=peer, device_id_type=MESH))
        return mine, sends, arrivals

    def start(ins, outs, sems):
        mine, sends, _ = plan(ins, outs, sems)
        for cp in mine + sends:
            cp.start()

    def finish(ins, outs, sems):
        mine, sends, arrivals = plan(ins, outs, sems)
        for cp in arrivals:
            cp.wait_recv()
        for cp in sends:
            cp.wait_send()
        for cp in mine:
            cp.wait()

    return _Comm(arrays, [SDS((N_DEV,) + a.shape, a.dtype) for a in arrays],
                 _dma_sems(7 * n) + [pltpu.SemaphoreType.DMA((n,))], start, finish)


def _scatter_direct(arrays):
    n = len(arrays)

    def plan(ins, outs, sems):
        send_sems, recv_sems = sems
        x, y, c = _coords()
        cps = []
        for a in range(n):
            for k in range(1, N_DEV):
                peer = (1 - x if k & 4 else x, 1 - y if k & 2 else y, 1 - c if k & 1 else c)
                s = 7 * a + k - 1
                cps.append(pltpu.make_async_remote_copy(
                    src_ref=ins[a].at[_dev_index(peer)], dst_ref=outs[a].at[k - 1], send_sem=send_sems.at[s],
                    recv_sem=recv_sems.at[s], device_id=peer, device_id_type=MESH))
        return cps

    def start(ins, outs, sems):
        for cp in plan(ins, outs, sems):
            cp.start()

    def finish(ins, outs, sems):
        for cp in plan(ins, outs, sems):
            cp.wait()

    return _Comm(arrays, [SDS((N_DEV - 1,) + a.shape[1:], a.dtype) for a in arrays], _dma_sems(7 * n),
                 start, finish)


def _run(comm, name):
    def body(*refs):
        k_in, k_out = len(comm.inputs), len(comm.out_shapes)
        ins, outs, sems = refs[:k_in], refs[k_in:k_in + k_out], refs[k_in + k_out:]
        comm.start(ins, outs, sems)
        comm.finish(ins, outs, sems)

    return pl.pallas_call(body, name=name, out_shape=comm.out_shapes, in_specs=[ANY] * len(comm.inputs),
                          out_specs=(ANY,) * len(comm.out_shapes), scratch_shapes=comm.sem_shapes)(*comm.inputs)


TILE_ELEMS = 192 * 1024


def _row_tile(R, C):
    if R * C <= TILE_ELEMS:
        return R
    return max(t for t in range(SUBLANES, R, SUBLANES) if R % t == 0 and t * C <= TILE_ELEMS)


def _adamw_math(w, g, m, v):
    m = ADAM_B1 * m + (1.0 - ADAM_B1) * g
    v = ADAM_B2 * v + (1.0 - ADAM_B2) * (g * g)
    m_hat = m / (1.0 - ADAM_B1 ** ADAM_STEP)
    v_hat = v / (1.0 - ADAM_B2 ** ADAM_STEP)
    delta = -ADAM_LR * (m_hat / (jnp.sqrt(v_hat) + ADAM_EPS) + ADAM_WD * w)
    return delta, m, v


def _adamw_sharded(own, recv, d_idx, w, m, v, name):
    R, C = w.shape
    t = _row_tile(R, C)

    def body(k_ref, p_ref, r_ref, w_ref, m_ref, v_ref, g_ref, d_ref, nm_ref, nv_ref):
        g = p_ref[...].astype(f32)
        for j in range(N_DEV - 1):
            g = g + r_ref[j].astype(f32)
        d, nm, nv = _adamw_math(w_ref[...], g, m_ref[...], v_ref[...])
        g_ref[...], d_ref[...], nm_ref[...], nv_ref[...] = g, d, nm, nv

    tile = lambda: pl.BlockSpec((t, C), lambda i, k: (i, 0))
    return pl.pallas_call(
        body, name="adamw_" + name, out_shape=(SDS((R, C), f32),) * 4,
        grid_spec=pltpu.PrefetchScalarGridSpec(
            num_scalar_prefetch=1, grid=(R // t,),
            in_specs=[pl.BlockSpec((None, t, C), lambda i, k: (k[0], i, 0)),
                      pl.BlockSpec((N_DEV - 1, t, C), lambda i, k: (0, i, 0)), tile(), tile(), tile()],
            out_specs=(tile(), tile(), tile(), tile())),
        compiler_params=_params(("parallel",)))(d_idx, own, recv, w, m, v)


def _adamw_replicated(parts, ws, ms, vs):
    n = len(ws)

    def body(*refs):
        p, w, m, v = (refs[i * n:(i + 1) * n] for i in range(4))
        outs = refs[4 * n:]
        for a in range(n):
            g = p[a][0].astype(f32)
            for j in range(1, N_DEV):
                g = g + p[a][j].astype(f32)
            d, nm, nv = _adamw_math(w[a][...], g, m[a][...], v[a][...])
            for i, val in enumerate((g, d, nm, nv)):
                outs[i * n + a][...] = val

    shapes = tuple(SDS(w.shape, f32) for w in ws)
    res = pl.pallas_call(body, name="adamw_replicated", out_shape=shapes * 4,
                         compiler_params=_params())(*parts, *ws, *ms, *vs)
    return [res[i * n:(i + 1) * n] for i in range(4)]


def _cols_to_full(g):
    n, r, c = g.shape
    return g.transpose(1, 0, 2).reshape(r, n * c)


def _full_to_cols(a):
    r, c = a.shape
    return a.reshape(r, N_DEV, c // N_DEV).transpose(1, 0, 2)


def _rows_blocked(a):
    r, c = a.shape
    return a.reshape(N_DEV, r // N_DEV, c)


def _w_in_to_internal(w):
    K = w.shape[0]
    q = w[:, 1280:2816].reshape(K, NG, KVH, 1, HD).transpose(0, 2, 1, 3, 4).reshape(K, KVH, NG, HD)
    k = w[:, 2816:3328].reshape(K, KVH, 1, HD)
    v = w[:, 3328:3840].reshape(K, KVH, 1, HD)
    att = jnp.concatenate([q, k, v], axis=2).reshape(K, KVH * ATT_COLS)
    return jnp.concatenate([w[:, :1280], att, w[:, 3840:]], axis=1)


def _w_in_from_internal(w):
    K = w.shape[0]
    att = w[:, C_ATT:C_GATE].reshape(K, KVH, 5, HD)
    q = att[:, :, 0:3].transpose(0, 2, 1, 3).reshape(K, NG * KVH * HD)
    k = att[:, :, 3].reshape(K, KVH * HD)
    v = att[:, :, 4].reshape(K, KVH * HD)
    return jnp.concatenate([w[:, :C_ATT], q, k, v, w[:, C_GATE:]], axis=1)


_IN_NAMES = ('x', 'rel_bias', 'norm_mix_pre', 'norm_mix_post', 'w_in', 'conv_rnn_w', 'conv_rnn_b', 'w_rg_a', 'b_rg_a',
             'w_rg_x', 'b_rg_x', 'lru_lambda', 'w_branch_rnn', 'w_branch_att', 'w_out', 'norm_ffn_pre',
             'norm_ffn_post', 'w_ffn_gate', 'w_ffn_up', 'conv_ffn_w', 'conv_ffn_b', 'w_ffn_down')
_WEIGHTS = _IN_NAMES[1:]
_SHARDED = {"w_in": "col", "conv_rnn_w": "col", "w_branch_rnn": "row", "w_branch_att": "col", "w_out": "row",
            "w_ffn_gate": "col", "w_ffn_up": "col", "conv_ffn_w": "col", "w_ffn_down": "row"}
_REPLICATED = tuple(n for n in _WEIGHTS if n not in _SHARDED)


def _flat2(a):
    return a.reshape(-1, a.shape[-1])


def _train_step(inp):
    x_idx, y_idx, c_idx = _coords()
    W = {n: inp[n] for n in _WEIGHTS}
    x = inp["x"].reshape(-1, D)
    target = inp["loss_target"].reshape(-1, D)
    shard = {n: inp[n][0] for n in _SHARDED}

    hn, (g_in, g_cr, g_cf) = _norm_in(x, W["norm_mix_pre"], comm=_gather_two_level(
        [shard["w_in"].astype(bf16), shard["conv_rnn_w"], shard["conv_ffn_w"]]))
    w_in = _w_in_to_internal(_cols_to_full(g_in))
    cw_rnn, cw_ffn = _cols_to_full(g_cr), _cols_to_full(g_cf)
    behind_rnn = ("w_branch_rnn", "w_branch_att", "w_out", "w_ffn_down")
    behind_att = ("w_ffn_gate", "w_ffn_up")

    wa, wx = W["w_rg_a"][0], W["w_rg_x"][0]
    buckets = jnp.asarray(_bucket_maps())

    proj = _mm(hn, w_in[:, :C_GATE], "nn", f32, "mm_proj", 512, C_GATE // 2, 1024, cols_outer=True)
    gates = _mm(hn, w_in[:, C_GATE:], "nn", bf16, "mm_gates", 1024, 1024, 1024, cols_outer=True)
    rnn_saved, got = _rnn_fwd(proj, cw_rnn, W["conv_rnn_b"], wa, W["b_rg_a"], wx, W["b_rg_x"], W["lru_lambda"],
                              comm=_gather_direct([shard[n].astype(bf16) for n in behind_rnn]))
    h_rnn = rnn_saved[0]
    gathered = dict(zip(behind_rnn, got))
    bias = _bias_tables(W["rel_bias"], buckets)
    (o_att, lse, *att_saved), got = _att_fwd(
        proj, bias, comm=_gather_direct([shard[n].astype(bf16) for n in behind_att]))
    gathered.update(zip(behind_att, got))
    w_brnn = gathered["w_branch_rnn"].reshape(RNN_W, D)
    w_batt = _cols_to_full(gathered["w_branch_att"])
    w_out = gathered["w_out"].reshape(D, D)
    w_gate, w_up = _cols_to_full(gathered["w_ffn_gate"]), _cols_to_full(gathered["w_ffn_up"])
    w_down = gathered["w_ffn_down"].reshape(FFN_W, D)
    pr = _mm(h_rnn, w_brnn, "nn", bf16, "mm_pr", 1024, 1024, 1280)
    pa = _mm(o_att, w_batt, "nn", bf16, "mm_pa", 1024, 1024, 512)
    merged = _merge_fwd(gates, pr, pa)
    (mix, h1, hn2), _ = _mm_rows([(merged, w_out, 1024)], "nn", "mm_mix", 1024, _mid_fwd_rows, [x],
                                 [W["norm_mix_post"], W["norm_ffn_pre"]], [f32, f32, bf16], [])
    gpre = _mm(hn2, w_gate, "nn", bf16, "mm_gate", 1024, 1024, 1024, cols_outer=True)
    up = _mm(hn2, w_up, "nn", bf16, "mm_up", 1024, 1024, 1024, cols_outer=True)
    act = _ffn_act_fwd(gpre, up, cw_ffn, W["conv_ffn_b"])
    (dy, dff), (loss_part, dg_fpost) = _mm_rows([(act, w_down, 1024)], "nn", "mm_down", 1024, _final_rows,
                                                [h1, target], [W["norm_ffn_post"]], [f32, bf16], [1, D])

    grads = {}
    dact = _mm(dff, w_down, "nt", bf16, "mm_dact", 1024, 1024, 1024, cols_outer=True)
    grads["w_ffn_down"] = _rows_blocked(_mm(act, dff, "tn", bf16, "mm_dw_down", 1024, 1024, 2048))
    dgpre, dup, dcw_ffn, dcb_ffn = _ffn_act_bwd(gpre, up, cw_ffn, W["conv_ffn_b"], dact)
    grads["conv_ffn_w"] = _full_to_cols(dcw_ffn.astype(bf16))
    grads["w_ffn_gate"] = _full_to_cols(_mm(hn2, dgpre, "tn", bf16, "mm_dw_gate", 1024, 1024, 2048))
    grads["w_ffn_up"] = _full_to_cols(_mm(hn2, dup, "tn", bf16, "mm_dw_up", 1024, 1024, 2048))
    dhn2 = _mm_nt_sum([(dgpre, w_gate, 1024), (dup, w_up, 1024)], "mm_dhn2", 1024, 1024)
    dh1, dmix, dg_fpre, dg_post = _mid_bwd(dy, dhn2, h1, W["norm_ffn_pre"], mix, W["norm_mix_post"])
    dmerged = _mm(dmix, w_out, "nt", bf16, "mm_dmerged", 1024, 1024, 1024)
    grads["w_out"] = _rows_blocked(_mm(merged, dmix, "tn", bf16, "mm_dw_out", 1024, 1024, 2048))
    dprpa, dgates = _merge_bwd(gates, pr, pa, dmerged)
    dpr, dpa = dprpa[:, :D], dprpa[:, D:]
    dh_rnn = _mm(dpr, w_brnn, "nt", bf16, "mm_dh_rnn", 1024, 1280, 1024)
    grads["w_branch_rnn"] = _rows_blocked(_mm(h_rnn, dpr, "tn", bf16, "mm_dw_brnn", 1280, 1024, 1024))
    do_att = _mm(dpa, w_batt, "nt", f32, "mm_do_att", 1024, 512, 1024)
    grads["w_branch_att"] = _full_to_cols(_mm(o_att, dpa, "tn", bf16, "mm_dw_batt", 512, 1024, 2048))

    received = {}
    behind_att_bwd = ("w_ffn_down", "w_ffn_gate", "conv_ffn_w", "w_out")
    behind_rnn_bwd = ("w_ffn_up", "w_branch_rnn", "w_branch_att")
    (dqkv, dbias), got = _att_bwd(proj, att_saved, bias, o_att, lse, do_att,
                                  comm=_scatter_direct([grads[n] for n in behind_att_bwd]))
    received.update(zip(behind_att_bwd, got))
    drel = _bias_grad(dbias, buckets)
    (dxr, dcw_rnn, dcb_rnn, dwa, dba, dwx, dbx, dlam), got = _rnn_bwd(
        proj, rnn_saved, dh_rnn, cw_rnn, wa, wx, W["lru_lambda"],
        comm=_scatter_direct([grads[n] for n in behind_rnn_bwd]))
    received.update(zip(behind_rnn_bwd, got))
    gsmall = {"rel_bias": drel, "norm_mix_post": dg_post, "conv_rnn_b": dcb_rnn, "w_rg_a": dwa.astype(bf16),
              "b_rg_a": dba, "w_rg_x": dwx.astype(bf16), "b_rg_x": dbx, "lru_lambda": dlam,
              "norm_ffn_pre": dg_fpre, "norm_ffn_post": dg_fpost, "conv_ffn_b": dcb_ffn}
    dw_in_a, parts = _mm(hn, dqkv, "tn", bf16, "mm_dw_in_a", 1024, 1280, 1024,
                         comm=_gather_direct([_flat2(gsmall[n]) for n in gsmall]))
    parts = dict(zip(gsmall, parts))
    dw_in = jnp.concatenate([_mm(hn, dxr, "tn", bf16, "mm_dw_in_r", 1024, 1280, 1024), dw_in_a,
                             _mm(hn, dgates, "tn", bf16, "mm_dw_in_g", 1024, 1024, 2048)], axis=1)
    grads["w_in"] = _full_to_cols(_w_in_from_internal(dw_in))
    grads["conv_rnn_w"] = _full_to_cols(dcw_rnn.astype(bf16))
    behind_dhn = ("w_in", "conv_rnn_w")
    dhn, got = _mm_nt_sum([(dxr, w_in[:, :C_ATT], 1280), (dqkv, w_in[:, C_ATT:C_GATE], 1280),
                           (dgates, w_in[:, C_GATE:], 1024)], "mm_dhn", 1024, 1024,
                          comm=_scatter_direct([grads[n] for n in behind_dhn]))
    received.update(zip(behind_dhn, got))
    dx, dg_pre = _in_bwd(dh1, dhn, x, W["norm_mix_pre"])
    parts["norm_mix_pre"], = _run(_gather_two_level([dg_pre]), "ag_norm_mix_pre")
    parts = [parts[n] for n in _REPLICATED]

    out = {}
    d_arr = jnp.reshape(4 * x_idx + 2 * y_idx + c_idx, (1,)).astype(jnp.int32)
    for n in _SHARDED:
        res = _adamw_sharded(grads[n], received[n], d_arr, shard[n], inp["m_" + n][0], inp["v_" + n][0], n)
        out[n] = [r[None] for r in res]
    small = _adamw_replicated(parts, *[[_flat2(inp[p + n]) for n in _REPLICATED] for p in ("", "m_", "v_")])
    for a, n in enumerate(_REPLICATED):
        out[n] = [small[i][a].reshape(inp[n].shape) for i in range(4)]

    loss = lax.psum(loss_part[0, 0], ("x", "y", "c"))
    outs = [loss, dx.reshape(inp["x"].shape)]
    for i in range(4):
        outs.extend(out[n][i] for n in _WEIGHTS)
    return tuple(outs)


def kernel(x, rel_bias, norm_mix_pre, norm_mix_post, w_in, conv_rnn_w, conv_rnn_b, w_rg_a, b_rg_a, w_rg_x, b_rg_x, lru_lambda, w_branch_rnn, w_branch_att, w_out, norm_ffn_pre, norm_ffn_post, w_ffn_gate, w_ffn_up, conv_ffn_w, conv_ffn_b, w_ffn_down, loss_target, m_rel_bias, m_norm_mix_pre, m_norm_mix_post, m_w_in, m_conv_rnn_w, m_conv_rnn_b, m_w_rg_a, m_b_rg_a, m_w_rg_x, m_b_rg_x, m_lru_lambda, m_w_branch_rnn, m_w_branch_att, m_w_out, m_norm_ffn_pre, m_norm_ffn_post, m_w_ffn_gate, m_w_ffn_up, m_conv_ffn_w, m_conv_ffn_b, m_w_ffn_down, v_rel_bias, v_norm_mix_pre, v_norm_mix_post, v_w_in, v_conv_rnn_w, v_conv_rnn_b, v_w_rg_a, v_b_rg_a, v_w_rg_x, v_b_rg_x, v_lru_lambda, v_w_branch_rnn, v_w_branch_att, v_w_out, v_norm_ffn_pre, v_norm_ffn_post, v_w_ffn_gate, v_w_ffn_up, v_conv_ffn_w, v_conv_ffn_b, v_w_ffn_down):
    vals = locals()
    names = list(_IN_NAMES) + ["loss_target"] + ["m_" + n for n in _WEIGHTS] + ["v_" + n for n in _WEIGHTS]
    return _train_step({n: vals[n] for n in names})
```

```python
import functools
import math

import numpy as np
import jax
import jax.numpy as jnp
from jax import lax
from jax.experimental import pallas as pl
from jax.experimental.pallas import tpu as pltpu

f32, bf16 = jnp.float32, jnp.bfloat16
SDS = jax.ShapeDtypeStruct
MESH = pl.DeviceIdType.MESH
ANY = pl.BlockSpec(memory_space=pl.ANY)

D = 1024
SEQ = 2048
RNN_W = 1280
RNN_BLOCKS = 10
LANES = 128
SUBLANES = 8
RNN_CONV = 4
LRU_C = 8.0
HD = 128
KVH = 4
DILATIONS = (1, 4, 16)
NG = 3
ATT_BLK = 128
NBLK_SEQ = SEQ // ATT_BLK
STAT_LANE = 64
ATT_UNROLL = 8
REL_BUCKETS = 32
REL_MAX_DIST = 2048
FFN_W = 3072
FFN_CONV = 3
EPS = 1e-6
IN_W = 5888
ATT_COLS = 5 * HD
C_ATT = RNN_W
C_GATE = RNN_W + KVH * ATT_COLS
NEG = -1e30

ADAM_LR, ADAM_B1, ADAM_B2, ADAM_EPS, ADAM_WD, ADAM_STEP = 0.001, 0.9, 0.999, 1e-08, 0.01, 10

VMEM_LIMIT_BYTES = 56 * 1024 * 1024
N_DEV = 8


def _params(sem=None):
    return pltpu.CompilerParams(dimension_semantics=sem, vmem_limit_bytes=VMEM_LIMIT_BYTES)


def _sigmoid(x):
    return 1.0 / (1.0 + jnp.exp(-x))


class _Comm:
    def __init__(self, inputs, out_shapes, sem_shapes, start, finish):
        self.inputs, self.out_shapes, self.sem_shapes = tuple(inputs), tuple(out_shapes), list(sem_shapes)
        self.start, self.finish = start, finish


def _call(body, args, *, name, grid, in_specs, out_specs, out_shape, scratch_shapes=(), semantics, comm=None):
    if comm is None:
        return pl.pallas_call(body, name=name, grid=grid, in_specs=list(in_specs), out_specs=tuple(out_specs),
                              out_shape=tuple(out_shape), scratch_shapes=list(scratch_shapes),
                              compiler_params=_params(semantics))(*args), ()
    n_in, n_out, n_scr = len(in_specs), len(out_shape), len(scratch_shapes)
    c_in, c_out = len(comm.inputs), len(comm.out_shapes)

    def fused(*refs):
        ins, refs = refs[:n_in], refs[n_in:]
        cin, refs = refs[:c_in], refs[c_in:]
        outs, refs = refs[:n_out], refs[n_out:]
        cout, refs = refs[:c_out], refs[c_out:]
        scr, csem = refs[:n_scr], refs[n_scr:]
        first = functools.reduce(jnp.logical_and, [pl.program_id(d) == 0 for d in range(len(grid))])
        last = functools.reduce(jnp.logical_and, [pl.program_id(d) == grid[d] - 1 for d in range(len(grid))])

        @pl.when(first)
        def _():
            comm.start(cin, cout, csem)

        body(*ins, *outs, *scr)

        @pl.when(last)
        def _():
            comm.finish(cin, cout, csem)

    res = pl.pallas_call(
        fused, name=name, grid=grid, in_specs=list(in_specs) + [ANY] * c_in,
        out_specs=tuple(out_specs) + (ANY,) * c_out, out_shape=tuple(out_shape) + comm.out_shapes,
        scratch_shapes=list(scratch_shapes) + comm.sem_shapes,
        compiler_params=_params(("arbitrary",) * len(grid)))(*args, *comm.inputs)
    return res[:n_out], res[n_out:]


_DIMS = {"nn": (((1,), (0,)), ((), ())), "nt": (((1,), (1,)), ((), ())), "tn": (((0,), (0,)), ((), ()))}


def _mm(a, b, mode, out_dtype, name, tm, tn, tk, cols_outer=False, comm=None):
    if mode == "nn":
        (M, K), (K2, N) = a.shape, b.shape
    elif mode == "nt":
        (M, K), (N, K2) = a.shape, b.shape
    else:
        (K, M), (K2, N) = a.shape, b.shape
    assert K == K2 and M % tm == 0 and N % tn == 0 and K % tk == 0, (name, a.shape, b.shape)
    nk = K // tk

    def body(a_ref, b_ref, o_ref, *scratch):
        part = lax.dot_general(a_ref[...].astype(bf16), b_ref[...].astype(bf16), _DIMS[mode],
                               preferred_element_type=f32)
        if nk == 1:
            o_ref[...] = part.astype(o_ref.dtype)
        else:
            acc_ref, = scratch
            k = pl.program_id(2)

            @pl.when(k == 0)
            def _():
                acc_ref[...] = part

            @pl.when(k > 0)
            def _():
                acc_ref[...] += part

            @pl.when(k == nk - 1)
            def _():
                o_ref[...] = acc_ref[...].astype(o_ref.dtype)

    def ij(f):
        return (lambda j, i, k: f(i, j, k)) if cols_outer else f

    if mode == "tn":
        a_spec = pl.BlockSpec((tk, tm), ij(lambda i, j, k: (k, i)))
    else:
        a_spec = pl.BlockSpec((tm, tk), ij(lambda i, j, k: (i, k)))
    if mode == "nt":
        b_spec = pl.BlockSpec((tn, tk), ij(lambda i, j, k: (j, k)))
    else:
        b_spec = pl.BlockSpec((tk, tn), ij(lambda i, j, k: (k, j)))
    o_spec = pl.BlockSpec((tm, tn), ij(lambda i, j, k: (i, j)))
    grid = (N // tn, M // tm, nk) if cols_outer else (M // tm, N // tn, nk)
    (out,), extra = _call(
        body, (a, b), name=name, out_shape=(SDS((M, N), out_dtype),), grid=grid, in_specs=[a_spec, b_spec],
        out_specs=(o_spec,), scratch_shapes=[pltpu.VMEM((tm, tn), f32)] if nk > 1 else [],
        semantics=("parallel", "parallel", "arbitrary"), comm=comm)
    return out if comm is None else (out, extra)


def _mm_nt_sum(pairs, name, tm, tn, comm=None):
    M, N = pairs[0][0].shape[0], pairs[0][1].shape[0]
    nks = [a.shape[1] // tk for a, _, tk in pairs]
    starts = [sum(nks[:p]) for p in range(len(pairs))]
    nk = sum(nks)

    def body(*refs):
        o_ref, acc_ref = refs[-2], refs[-1]
        k = pl.program_id(2)
        for p in range(len(pairs)):
            def product(p=p):
                return lax.dot_general(refs[2 * p][...], refs[2 * p + 1][...], _DIMS["nt"], preferred_element_type=f32)

            if p == 0:
                @pl.when(k == 0)
                def _():
                    acc_ref[...] = product()

            @pl.when((k >= max(starts[p], 1)) & (k < starts[p] + nks[p]))
            def _():
                acc_ref[...] += product()

        @pl.when(k == nk - 1)
        def _():
            o_ref[...] = acc_ref[...].astype(bf16)

    in_specs, args = [], []
    for (a, b, tk), k0, n in zip(pairs, starts, nks):
        assert a.shape[1] == b.shape[1] and a.shape[1] % tk == 0 and a.dtype == b.dtype == bf16, name
        chunk = lambda k, k0=k0, n=n: jnp.clip(k - k0, 0, n - 1)
        in_specs += [pl.BlockSpec((tm, tk), lambda i, j, k, c=chunk: (i, c(k))),
                     pl.BlockSpec((tn, tk), lambda i, j, k, c=chunk: (j, c(k)))]
        args += [a, b]
    o_spec = pl.BlockSpec((tm, tn), lambda i, j, k: (i, j))
    (out,), extra = _call(
        body, args, name=name, out_shape=(SDS((M, N), bf16),), grid=(M // tm, N // tn, nk), in_specs=in_specs,
        out_specs=(o_spec,), scratch_shapes=[pltpu.VMEM((tm, tn), f32)],
        semantics=("parallel", "parallel", "arbitrary"), comm=comm)
    return out if comm is None else (out, extra)


def _mm_rows(pairs, mode, name, tm, epilogue, rows_in, vecs_in, rows_out, vecs_out, comm=None):
    M = pairs[0][0].shape[0]
    N = pairs[0][1].shape[1 if mode == "nn" else 0]
    nks = [a.shape[1] // tk for a, _, tk in pairs]
    starts = [sum(nks[:p]) for p in range(len(pairs))]
    nk = sum(nks)
    n_rows_in, n_vecs_in, n_rows_out = len(rows_in), len(vecs_in), len(rows_out)

    def body(*refs):
        pair_refs, refs = refs[:2 * len(pairs)], refs[2 * len(pairs):]
        rin, refs = refs[:n_rows_in], refs[n_rows_in:]
        vin, refs = refs[:n_vecs_in], refs[n_vecs_in:]
        rout, refs = refs[:n_rows_out], refs[n_rows_out:]
        vout, acc_ref = refs[:-1], refs[-1]
        i, k = pl.program_id(0), pl.program_id(1)
        for p in range(len(pairs)):
            def product(p=p):
                return lax.dot_general(pair_refs[2 * p][...], pair_refs[2 * p + 1][...], _DIMS[mode],
                                       preferred_element_type=f32)

            if p == 0:
                @pl.when(k == 0)
                def _():
                    acc_ref[...] = product()

            @pl.when((k >= max(starts[p], 1)) & (k < starts[p] + nks[p]))
            def _():
                acc_ref[...] += product()

        @pl.when(k == nk - 1)
        def _():
            res = epilogue(acc_ref[...], *[r[...] for r in rin], *[v[...] for v in vin])
            for ref, val in zip(rout, res[:n_rows_out]):
                ref[...] = val.astype(ref.dtype)
            for ref, val in zip(vout, res[n_rows_out:]):
                @pl.when(i == 0)
                def _(ref=ref, val=val):
                    ref[...] = val

                @pl.when(i > 0)
                def _(ref=ref, val=val):
                    ref[...] += val

    in_specs, args = [], []
    for (a, b, tk), k0, n in zip(pairs, starts, nks):
        assert a.shape[1] % tk == 0 and a.dtype == b.dtype == bf16, name
        chunk = lambda k, k0=k0, n=n: jnp.clip(k - k0, 0, n - 1)
        in_specs.append(pl.BlockSpec((tm, tk), lambda i, k, c=chunk: (i, c(k))))
        if mode == "nn":
            in_specs.append(pl.BlockSpec((tk, N), lambda i, k, c=chunk: (c(k), 0)))
        else:
            in_specs.append(pl.BlockSpec((N, tk), lambda i, k, c=chunk: (0, c(k))))
        args += [a, b]
    row = lambda: pl.BlockSpec((tm, N), lambda i, k: (i, 0))
    vec = lambda w: pl.BlockSpec((1, w), lambda i, k: (0, 0))
    in_specs += [row() for _ in rows_in] + [vec(v.shape[1]) for v in vecs_in]
    outs, extra = _call(
        body, (*args, *rows_in, *vecs_in), name=name,
        out_shape=tuple(SDS((M, N), dt) for dt in rows_out) + tuple(SDS((1, w), f32) for w in vecs_out),
        grid=(M // tm, nk), in_specs=in_specs,
        out_specs=tuple(row() for _ in rows_out) + tuple(vec(w) for w in vecs_out),
        scratch_shapes=[pltpu.VMEM((tm, N), f32)], semantics=("arbitrary", "arbitrary"), comm=comm)
    res = (outs[:n_rows_out], outs[n_rows_out:])
    return res if comm is None else (res, extra)


ROW_TILE = 512


def _rms_fwd(x, g):
    r = lax.rsqrt(jnp.mean(x * x, axis=-1, keepdims=True) + EPS)
    return x * r * g


def _rms_bwd(x, g, dy):
    r = lax.rsqrt(jnp.mean(x * x, axis=-1, keepdims=True) + EPS)
    xh = x * r
    dxh = dy * g
    dx = r * (dxh - xh * jnp.mean(dxh * xh, axis=-1, keepdims=True))
    return dx, jnp.sum(dy * xh, axis=0, keepdims=True)


def _acc_out(ref, val):
    @pl.when(pl.program_id(0) == 0)
    def _():
        ref[...] = val

    @pl.when(pl.program_id(0) > 0)
    def _():
        ref[...] += val


def _row_spec(width=D):
    return pl.BlockSpec((ROW_TILE, width), lambda i: (i, 0))


def _vec_spec(width=D):
    return pl.BlockSpec((1, width), lambda i: (0, 0))


def _norm_in(x, g, comm=None):
    def body(x_ref, g_ref, o_ref):
        o_ref[...] = _rms_fwd(x_ref[...], g_ref[...]).astype(bf16)

    T = x.shape[0]
    (hn,), extra = _call(body, (x, g), name="norm_in", out_shape=(SDS((T, D), bf16),), grid=(T // ROW_TILE,),
                         in_specs=[_row_spec(), _vec_spec()], out_specs=(_row_spec(),), semantics=("parallel",),
                         comm=comm)
    return hn, extra


def _mid_bwd(dy, dhn2, h1, g_fpre, mix, g_post):
    def body(dy_ref, dhn2_ref, h1_ref, gf_ref, mix_ref, gp_ref, dh1_ref, dmix_ref, dgf_ref, dgp_ref):
        d1, dgf = _rms_bwd(h1_ref[...], gf_ref[...], dhn2_ref[...].astype(f32))
        dh1 = dy_ref[...] + d1
        dh1_ref[...] = dh1
        dmix, dgp = _rms_bwd(mix_ref[...], gp_ref[...], dh1)
        dmix_ref[...] = dmix.astype(bf16)
        _acc_out(dgf_ref, dgf)
        _acc_out(dgp_ref, dgp)

    T = dy.shape[0]
    return pl.pallas_call(
        body, name="mid_bwd", out_shape=(SDS((T, D), f32), SDS((T, D), bf16), SDS((1, D), f32), SDS((1, D), f32)),
        grid=(T // ROW_TILE,),
        in_specs=[_row_spec(), _row_spec(), _row_spec(), _vec_spec(), _row_spec(), _vec_spec()],
        out_specs=(_row_spec(), _row_spec(), _vec_spec(), _vec_spec()),
        compiler_params=_params(("arbitrary",)))(dy, dhn2, h1, g_fpre, mix, g_post)


def _in_bwd(dh1, dhn, x, g_pre):
    def body(dh1_ref, dhn_ref, x_ref, g_ref, dx_ref, dg_ref):
        d, dg = _rms_bwd(x_ref[...], g_ref[...], dhn_ref[...].astype(f32))
        dx_ref[...] = dh1_ref[...] + d
        _acc_out(dg_ref, dg)

    T = x.shape[0]
    return pl.pallas_call(
        body, name="in_bwd", out_shape=(SDS((T, D), f32), SDS((1, D), f32)), grid=(T // ROW_TILE,),
        in_specs=[_row_spec(), _row_spec(), _row_spec(), _vec_spec()], out_specs=(_row_spec(), _vec_spec()),
        compiler_params=_params(("arbitrary",)))(dh1, dhn, x, g_pre)


def _mid_fwd_rows(mix, x, g_post, g_fpre):
    h1 = x + _rms_fwd(mix, g_post)
    return mix, h1, _rms_fwd(h1, g_fpre)


def _final_rows(ff, h1, target, g_fpost):
    e = h1 + _rms_fwd(ff, g_fpost) - target
    part = jnp.sum(jnp.sum(e * e, axis=1, keepdims=True), axis=0, keepdims=True) * (0.5 / D)
    dy = e * (1.0 / D)
    dff, dg = _rms_bwd(ff, g_fpost, dy)
    return dy, dff, part, dg


def _shift_dn(x, d, row, fill=0.0):
    if d == 0:
        return x
    y = pltpu.roll(x, d, 0)
    head = jnp.where(row[:SUBLANES] >= d, y[:SUBLANES], fill)
    return jnp.concatenate([head, y[SUBLANES:]], axis=0)


def _shift_up(x, d, row, fill=0.0):
    if d == 0:
        return x
    n = x.shape[0]
    y = pltpu.roll(x, n - d, 0)
    tail = jnp.where(row[:SUBLANES] < SUBLANES - d, y[n - SUBLANES:], fill)
    return jnp.concatenate([y[:n - SUBLANES], tail], axis=0)


def _conv_fwd(x, w_ref, b, row):
    K = w_ref.shape[0]
    y = b
    for k in range(K):
        y = y + w_ref[k:k + 1, :] * _shift_dn(x, K - 1 - k, row)
    return y


def _conv_bwd(x, w_ref, dy, row):
    K = w_ref.shape[0]
    dx = jnp.zeros_like(dy)
    dws = []
    for k in range(K):
        dx = dx + w_ref[k:k + 1, :] * _shift_up(dy, K - 1 - k, row)
        dws.append(jnp.sum(dy * _shift_dn(x, K - 1 - k, row), axis=0, keepdims=True))
    return dx, dws, jnp.sum(dy, axis=0, keepdims=True)


def _scan_fwd(a, u, row):
    n = a.shape[0]
    d = 1
    while d < n:
        last = 2 * d >= n
        if d < SUBLANES:
            u = u + a * _shift_dn(u, d, row)
            if not last:
                a = a * _shift_dn(a, d, row, fill=1.0)
        else:
            u = jnp.concatenate([u[:d], u[d:] + a[d:] * u[:n - d]], axis=0)
            if not last:
                a = jnp.concatenate([a[:d], a[d:] * a[:n - d]], axis=0)
        d *= 2
    return u


def _scan_bwd(b, u, row):
    n = b.shape[0]
    d = 1
    while d < n:
        last = 2 * d >= n
        if d < SUBLANES:
            u = u + b * _shift_up(u, d, row)
            if not last:
                b = b * _shift_up(b, d, row, fill=1.0)
        else:
            u = jnp.concatenate([u[:n - d] + b[:n - d] * u[d:], u[n - d:]], axis=0)
            if not last:
                b = jnp.concatenate([b[:n - d] * b[d:], b[n - d:]], axis=0)
        d *= 2
    return u


def _neg_expm1(z):
    series = -z * (1.0 + z * (0.5 + z * (1.0 / 6.0 + z * (1.0 / 24.0 + z * (1.0 / 120.0)))))
    return jnp.where(z > -0.1, series, 1.0 - jnp.exp(z))


def _softplus_neg(lam):
    z = -lam
    return jnp.maximum(z, 0.0) + jnp.log(1.0 + jnp.exp(-jnp.abs(z)))


def _rnn_specs(B):
    blk = lambda: pl.BlockSpec((SEQ, LANES), lambda b, n: (b, n))
    return dict(
        act=blk,
        convw=pl.BlockSpec((RNN_CONV, LANES), lambda b, n: (0, n)),
        vec=lambda: pl.BlockSpec((1, LANES), lambda b, n: (0, n)),
        gate=lambda: pl.BlockSpec((None, LANES, LANES), lambda b, n: (n, 0, 0)),
    )


def _rnn_fwd(proj, cw, cb, wa, ba, wx, bx, lam, comm=None):
    T = proj.shape[0]
    B = T // SEQ

    def body(x_ref, cw_ref, cb_ref, wa_ref, ba_ref, wx_ref, bx_ref, lam_ref, h_ref, xc_ref, r_ref, i_ref, a_ref, s_ref):
        row = lax.broadcasted_iota(jnp.int32, (SEQ, LANES), 0)
        xc = _conv_fwd(x_ref[...], cw_ref, cb_ref[...], row)
        xcb = xc.astype(bf16)
        r = _sigmoid(jnp.dot(xcb, wa_ref[...].astype(bf16), preferred_element_type=f32) + ba_ref[...])
        i = _sigmoid(jnp.dot(xcb, wx_ref[...].astype(bf16), preferred_element_type=f32) + bx_ref[...])
        log_a = (-LRU_C * _softplus_neg(lam_ref[...])) * r
        a = jnp.exp(log_a)
        s = jnp.sqrt(_neg_expm1(2.0 * log_a))
        xc_ref[...], r_ref[...], i_ref[...], a_ref[...], s_ref[...] = xc, r, i, a, s
        h_ref[...] = _scan_fwd(a, s * (i * xc), row)

    sp_ = _rnn_specs(B)
    return _call(
        body, (proj, cw, cb, wa, ba, wx, bx, lam), name="rnn_fwd", out_shape=(SDS((T, RNN_W), f32),) * 6,
        grid=(B, RNN_BLOCKS),
        in_specs=[sp_["act"](), sp_["convw"], sp_["vec"](), sp_["gate"](), sp_["vec"](), sp_["gate"](),
                  sp_["vec"](), sp_["vec"]()],
        out_specs=tuple(sp_["act"]() for _ in range(6)), semantics=("parallel", "parallel"), comm=comm)


def _rnn_bwd(proj, saved, dh, cw, wa, wx, lam, comm=None):
    T = proj.shape[0]
    B = T // SEQ

    def body(x_ref, h_ref, xc_ref, r_ref, i_ref, a_ref, s_ref, dh_ref, cw_ref, wa_ref, wx_ref, lam_ref,
             dx_ref, dcw_ref, dcb_ref, dwa_ref, dba_ref, dwx_ref, dbx_ref, dlam_ref):
        row = lax.broadcasted_iota(jnp.int32, (SEQ, LANES), 0)
        xr = x_ref[...]
        wa, wx, lam = wa_ref[...], wx_ref[...], lam_ref[...]
        xc, r, i, a, s = xc_ref[...], r_ref[...], i_ref[...], a_ref[...], s_ref[...]
        xcb = xc.astype(bf16)
        sp = _softplus_neg(lam)
        hprev = _shift_dn(h_ref[...], 1, row)
        g = _scan_bwd(_shift_up(a, 1, row), dh_ref[...].astype(f32), row)
        da = g * hprev
        ds = g * (i * xc)
        di = g * (s * xc)
        dxc = g * (s * i)
        dla = da * a - ds * (a * a) / s
        dr = dla * (-LRU_C * sp)
        dsp = jnp.sum(dla * (-LRU_C * r), axis=0, keepdims=True)
        dlam = -dsp * _sigmoid(-lam)
        dga = dr * r * (1.0 - r)
        dgx = di * i * (1.0 - i)
        dgab, dgxb = dga.astype(bf16), dgx.astype(bf16)
        dwa = lax.dot_general(xcb, dgab, _DIMS["tn"], preferred_element_type=f32)
        dwx = lax.dot_general(xcb, dgxb, _DIMS["tn"], preferred_element_type=f32)
        dxc = dxc + lax.dot_general(dgab, wa.astype(bf16), _DIMS["nt"], preferred_element_type=f32)
        dxc = dxc + lax.dot_general(dgxb, wx.astype(bf16), _DIMS["nt"], preferred_element_type=f32)
        dx, dws, db = _conv_bwd(xr, cw_ref, dxc, row)
        dx_ref[...] = dx.astype(bf16)
        first = pl.program_id(1) == 0

        def acc(ref, val):
            @pl.when(first)
            def _():
                ref[...] = val

            @pl.when(jnp.logical_not(first))
            def _():
                ref[...] += val

        for k in range(RNN_CONV):
            acc(dcw_ref.at[k:k + 1, :], dws[k])
        acc(dcb_ref, db)
        acc(dwa_ref, dwa)
        acc(dba_ref, jnp.sum(dga, axis=0, keepdims=True))
        acc(dwx_ref, dwx)
        acc(dbx_ref, jnp.sum(dgx, axis=0, keepdims=True))
        acc(dlam_ref, dlam)

    blk = lambda: pl.BlockSpec((SEQ, LANES), lambda n, b: (b, n))
    convw = lambda: pl.BlockSpec((RNN_CONV, LANES), lambda n, b: (0, n))
    vec = lambda: pl.BlockSpec((1, LANES), lambda n, b: (0, n))
    gate = lambda: pl.BlockSpec((None, LANES, LANES), lambda n, b: (n, 0, 0))
    vshape = SDS((1, RNN_W), f32)
    gshape = SDS((RNN_BLOCKS, LANES, LANES), f32)
    return _call(
        body, (proj, *saved, dh, cw, wa, wx, lam), name="rnn_bwd",
        out_shape=(SDS((T, RNN_W), bf16), SDS((RNN_CONV, RNN_W), f32), vshape, gshape, vshape, gshape, vshape, vshape),
        grid=(RNN_BLOCKS, B),
        in_specs=[blk() for _ in range(8)] + [convw(), gate(), gate(), vec()],
        out_specs=(blk(), convw(), vec(), gate(), vec(), gate(), vec(), vec()),
        semantics=("parallel", "arbitrary"), comm=comm)


def _t5_bucket(dist):
    max_exact = REL_BUCKETS // 2
    d = np.maximum(dist, 1).astype(np.float32)
    large = max_exact + np.log(d / max_exact) / math.log(REL_MAX_DIST / max_exact) * (REL_BUCKETS - max_exact)
    large = np.minimum(large.astype(np.int32), REL_BUCKETS - 1)
    return np.where(dist < max_exact, dist, large).astype(np.int32)


def _bucket_maps():
    qi = np.arange(ATT_BLK)[:, None]
    kj = np.arange(2 * ATT_BLK)[None, :]
    delta = ATT_BLK + qi - kj
    valid = (delta >= 0) & (delta <= ATT_BLK)
    maps = [np.where(valid, _t5_bucket(np.maximum(delta, 0) * r), -1) for r in DILATIONS]
    return np.stack(maps).astype(np.int32)


def _bias_tables(rel_bias, buckets):
    def body(rb_ref, bk_ref, o_ref):
        for g in range(NG):
            bk = bk_ref[g]
            for h in range(KVH):
                acc = jnp.full(bk.shape, NEG, f32)
                for b in range(REL_BUCKETS):
                    acc = jnp.where(bk == b, rb_ref[b, g * KVH + h], acc)
                o_ref[h, g] = acc

    return pl.pallas_call(
        body, name="bias_tables", out_shape=SDS((KVH, NG, ATT_BLK, 2 * ATT_BLK), f32),
        in_specs=[pl.BlockSpec(memory_space=pltpu.SMEM), pl.BlockSpec(memory_space=pltpu.VMEM)],
        out_specs=pl.BlockSpec(memory_space=pltpu.VMEM), compiler_params=_params())(rel_bias, buckets)


def _bias_grad(dbias, buckets):
    def body(db_ref, bk_ref, o_ref):
        rr = lax.broadcasted_iota(jnp.int32, (REL_BUCKETS, NG * KVH), 0)
        cc = lax.broadcasted_iota(jnp.int32, (REL_BUCKETS, NG * KVH), 1)
        out = jnp.zeros((REL_BUCKETS, NG * KVH), f32)
        for g in range(NG):
            bk = bk_ref[g]
            for h in range(KVH):
                d = db_ref[h, g]
                for b in range(REL_BUCKETS):
                    m = jnp.where(bk == b, d, 0.0)
                    s = jnp.sum(jnp.sum(m, axis=1, keepdims=True), axis=0, keepdims=True)
                    out = jnp.where((rr == b) & (cc == g * KVH + h), s, out)
        o_ref[...] = out

    return pl.pallas_call(body, name="bias_grad", out_shape=SDS((REL_BUCKETS, NG * KVH), f32),
                          compiler_params=_params())(dbias, buckets)


def _to_sub(dst_ref, src_ref, r, dtype, offset=0):
    M = SEQ // r
    for c in range(r):
        if r == 1:
            v = src_ref[...]
        else:
            v = src_ref[pl.ds(c, M, stride=r), :]
        dst_ref[pl.ds(offset + c * M, M), :] = v.astype(dtype)


def _from_sub(dst_ref, src_ref, r, accumulate=False, offset=0):
    M = SEQ // r
    for c in range(r):
        v = src_ref[pl.ds(offset + c * M, M), :]
        idx = slice(None) if r == 1 else pl.ds(c, M, stride=r)
        if accumulate:
            dst_ref[idx, :] = dst_ref[idx, :] + v
        else:
            dst_ref[idx, :] = v


_COL = lambda k: slice(k * HD, (k + 1) * HD)
SCALE = HD ** -0.5


def _qkv_spec(k, bh):
    def index(*ids):
        b, h = bh(*ids)
        return (b, C_ATT // HD + 5 * h + k)

    return pl.BlockSpec((SEQ, HD), index)


def _key_window(bias_ref, g, nb):
    if nb == 1:
        bias_own = bias_ref[g, :, ATT_BLK:2 * ATT_BLK]
        return lambda j: (pl.ds(pl.multiple_of((j + 1) * ATT_BLK, ATT_BLK), ATT_BLK), bias_own)
    bias_g = bias_ref[g]
    col = lax.broadcasted_iota(jnp.int32, bias_g.shape, 1)
    bias_first = jnp.where(col >= ATT_BLK, bias_g, NEG)
    return lambda j: (pl.ds(pl.multiple_of(j * ATT_BLK, ATT_BLK), 2 * ATT_BLK),
                      jnp.where(j % nb != 0, bias_g, bias_first))


def _att_fwd(proj, bias, comm=None):
    T = proj.shape[0]
    B = T // SEQ

    def body(q0_ref, q1_ref, q2_ref, k_ref, v_ref, bias_ref, o_ref, lse_ref, *rest):
        saved, (qp, kp, vp, kt, op, lp, og, lg) = rest[:3 * (NG - 1)], rest[3 * (NG - 1):]
        q_refs = (q0_ref, q1_ref, q2_ref)
        kp[0:ATT_BLK, :] = jnp.zeros((ATT_BLK, HD), bf16)
        vp[0:ATT_BLK, :] = jnp.zeros((ATT_BLK, HD), bf16)
        for g, r in enumerate(DILATIONS):
            nb = NBLK_SEQ // r
            _to_sub(qp, q_refs[g], r, bf16)
            _to_sub(kp, k_ref, r, bf16, offset=ATT_BLK)
            _to_sub(vp, v_ref, r, bf16, offset=ATT_BLK)
            if r > 1:
                sq, sk, sv = saved[3 * (g - 1):3 * g]
                sq[...], sk[...], sv[...] = qp[...], kp[ATT_BLK:, :], vp[ATT_BLK:, :]
            kt[...] = kp[...].T
            keys = _key_window(bias_ref, g, nb)

            def step(j, carry):
                cur = pl.ds(pl.multiple_of(j * ATT_BLK, ATT_BLK), ATT_BLK)
                win, bias_j = keys(j)
                s = jnp.dot(qp[cur, :], kt[:, win], preferred_element_type=f32) * SCALE + bias_j
                m = jnp.max(s, axis=-1, keepdims=True)
                p = jnp.exp(s - m)
                den = jnp.sum(p, axis=-1, keepdims=True)
                o = jnp.dot(p.astype(bf16), vp[win, :], preferred_element_type=f32)
                op[cur, :] = o / den
                lp[cur, :] = jnp.broadcast_to(m + jnp.log(den), (ATT_BLK, HD))
                return carry

            lax.fori_loop(0, NBLK_SEQ, step, 0, unroll=ATT_UNROLL)
            _from_sub(og.at[g], op, r)
            _from_sub(lg.at[g], lp, r)
        l0, l1, l2 = lg[0], lg[1], lg[2]
        mx = jnp.maximum(jnp.maximum(l0, l1), l2)
        e0, e1, e2 = jnp.exp(l0 - mx), jnp.exp(l1 - mx), jnp.exp(l2 - mx)
        den = e0 + e1 + e2
        o_ref[...] = (e0 * og[0] + e1 * og[1] + e2 * og[2]) / den
        lse_ref[...] = mx + jnp.log(den)

    return _call(
        body, (proj, proj, proj, proj, proj, bias), name="att_fwd",
        out_shape=(SDS((T, KVH * HD), f32), SDS((KVH, T, HD), f32)) + (SDS((T, KVH * HD), bf16),) * (3 * (NG - 1)),
        grid=(B, KVH),
        in_specs=[_qkv_spec(k, lambda b, h: (b, h)) for k in range(5)]
                 + [pl.BlockSpec((None, NG, ATT_BLK, 2 * ATT_BLK), lambda b, h: (h, 0, 0, 0))],
        out_specs=(pl.BlockSpec((SEQ, HD), lambda b, h: (b, h)),
                   pl.BlockSpec((None, SEQ, HD), lambda b, h: (h, b, 0)))
                  + tuple(pl.BlockSpec((SEQ, HD), lambda b, h: (b, h)) for _ in range(3 * (NG - 1))),
        scratch_shapes=[pltpu.VMEM((SEQ, HD), bf16)] + [pltpu.VMEM((SEQ + ATT_BLK, HD), bf16)] * 2
                       + [pltpu.VMEM((HD, SEQ + ATT_BLK), bf16)]
                       + [pltpu.VMEM((SEQ, HD), f32)] * 2 + [pltpu.VMEM((NG, SEQ, HD), f32)] * 2,
        semantics=("parallel", "parallel"), comm=comm)


def _att_bwd(proj, saved, bias, o, lse, do, comm=None):
    T = proj.shape[0]
    B = T // SEQ
    n_saved = 3 * (NG - 1)

    def body(q0_ref, k_ref, v_ref, *rest):
        saved_refs, rest = rest[:n_saved], rest[n_saved:]
        (bias_ref, o_ref, lse_ref, do_ref, dx_ref, db_ref,
         qp, kp, vp, dop, qt, kt, vt, dot, lp, dqp, dkt, dvt, dln, nat, dkn, dvn) = rest
        first = pl.program_id(1) == 0

        @pl.when(first)
        def _():
            db_ref[...] = jnp.zeros_like(db_ref)

        lane = lax.broadcasted_iota(jnp.int32, (SEQ, HD), 1)
        dln[...] = jnp.where(lane < STAT_LANE, lse_ref[...],
                             jnp.sum(do_ref[...] * o_ref[...], axis=-1, keepdims=True))
        dkn[...] = jnp.zeros_like(dkn)
        dvn[...] = jnp.zeros_like(dvn)
        kp[0:ATT_BLK, :] = jnp.zeros((ATT_BLK, HD), bf16)
        vp[0:ATT_BLK, :] = jnp.zeros((ATT_BLK, HD), bf16)
        for g, r in enumerate(DILATIONS):
            nb = NBLK_SEQ // r
            if r == 1:
                qp[...] = q0_ref[...].astype(bf16)
                kp[ATT_BLK:, :] = k_ref[...].astype(bf16)
                vp[ATT_BLK:, :] = v_ref[...].astype(bf16)
            else:
                sq, sk, sv = saved_refs[3 * (g - 1):3 * g]
                qp[...], kp[ATT_BLK:, :], vp[ATT_BLK:, :] = sq[...], sk[...], sv[...]
            _to_sub(dop, do_ref, r, bf16)
            _to_sub(lp, dln, r, f32)
            qt[...], kt[...], vt[...], dot[...] = qp[...].T, kp[...].T, vp[...].T, dop[...].T
            dkt[...] = jnp.zeros_like(dkt)
            dvt[...] = jnp.zeros_like(dvt)
            keys = _key_window(bias_ref, g, nb)
            db_cols = slice(ATT_BLK, 2 * ATT_BLK) if nb == 1 else slice(None)

            def step(j, carry):
                cur = pl.ds(pl.multiple_of(j * ATT_BLK, ATT_BLK), ATT_BLK)
                win, bias_j = keys(j)
                s = jnp.dot(qp[cur, :], kt[:, win], preferred_element_type=f32) * SCALE + bias_j
                p = jnp.exp(s - lp[cur, 0:1])
                dp = jnp.dot(dop[cur, :], vt[:, win], preferred_element_type=f32)
                ds = p * (dp - lp[cur, STAT_LANE:STAT_LANE + 1])
                db_ref[g, :, db_cols] += ds
                dsb, pb = ds.astype(bf16), p.astype(bf16)
                dqp[cur, :] = jnp.dot(dsb, kp[win, :], preferred_element_type=f32) * SCALE
                dkt[:, win] += jnp.dot(qt[:, cur], dsb, preferred_element_type=f32) * SCALE
                dvt[:, win] += jnp.dot(dot[:, cur], pb, preferred_element_type=f32)
                return carry

            lax.fori_loop(0, NBLK_SEQ, step, 0, unroll=ATT_UNROLL)
            _from_sub(nat, dqp, r)
            dx_ref[:, _COL(g)] = nat[...].astype(bf16)
            dqp[...] = dkt[:, ATT_BLK:].T
            _from_sub(dkn, dqp, r, accumulate=True)
            dqp[...] = dvt[:, ATT_BLK:].T
            _from_sub(dvn, dqp, r, accumulate=True)
        dx_ref[:, _COL(3)] = dkn[...].astype(bf16)
        dx_ref[:, _COL(4)] = dvn[...].astype(bf16)

    blk = lambda: pl.BlockSpec((SEQ, HD), lambda h, b: (b, h))
    bias_spec = lambda: pl.BlockSpec((None, NG, ATT_BLK, 2 * ATT_BLK), lambda h, b: (h, 0, 0, 0))
    pad = lambda dtype: pltpu.VMEM((SEQ + ATT_BLK, HD), dtype)
    pad_t = lambda dtype: pltpu.VMEM((HD, SEQ + ATT_BLK), dtype)
    seq_t = pltpu.VMEM((HD, SEQ), bf16)
    return _call(
        body, (proj, proj, proj, *saved, bias, o, lse, do), name="att_bwd",
        out_shape=(SDS((T, KVH * ATT_COLS), bf16), SDS((KVH, NG, ATT_BLK, 2 * ATT_BLK), f32)), grid=(KVH, B),
        in_specs=[_qkv_spec(k, lambda h, b: (b, h)) for k in (0, 3, 4)] + [blk() for _ in range(n_saved)]
                 + [bias_spec(), blk(), pl.BlockSpec((None, SEQ, HD), lambda h, b: (h, b, 0)), blk()],
        out_specs=(pl.BlockSpec((SEQ, ATT_COLS), lambda h, b: (b, h)), bias_spec()),
        scratch_shapes=[pltpu.VMEM((SEQ, HD), bf16), pad(bf16), pad(bf16), pltpu.VMEM((SEQ, HD), bf16),
                        seq_t, pad_t(bf16), pad_t(bf16), seq_t]
                       + [pltpu.VMEM((SEQ, HD), f32)] * 2 + [pad_t(f32)] * 2 + [pltpu.VMEM((SEQ, HD), f32)] * 4,
        semantics=("parallel", "arbitrary"), comm=comm)


MERGE_ROWS, MERGE_COLS = 1024, 512


def _merge_fwd(gates, pr, pa):
    def body(gr_ref, ga_ref, pr_ref, pa_ref, o_ref):
        o_ref[...] = (_sigmoid(gr_ref[...].astype(f32)) * pr_ref[...].astype(f32)
                      + _sigmoid(ga_ref[...].astype(f32)) * pa_ref[...].astype(f32)).astype(bf16)

    T = gates.shape[0]
    cols = lambda off: pl.BlockSpec((MERGE_ROWS, MERGE_COLS), lambda i, j: (i, off + j))
    return pl.pallas_call(body, name="merge_fwd", out_shape=SDS((T, D), bf16),
                          grid=(T // MERGE_ROWS, D // MERGE_COLS),
                          in_specs=[cols(0), cols(D // MERGE_COLS), cols(0), cols(0)], out_specs=cols(0),
                          compiler_params=_params(("parallel", "parallel")))(gates, gates, pr, pa)


def _merge_bwd(gates, pr, pa, dm):
    nj = D // MERGE_COLS

    def body(g_ref, pr_ref, pa_ref, dm_ref, dp_ref, dg_ref):
        dm_ = dm_ref[...].astype(f32)
        s = _sigmoid(g_ref[...].astype(f32))
        p = jnp.where(pl.program_id(1) < nj, pr_ref[...], pa_ref[...]).astype(f32)
        dp_ref[...] = (dm_ * s).astype(bf16)
        dg_ref[...] = (dm_ * p * s * (1.0 - s)).astype(bf16)

    T = gates.shape[0]
    blk = (MERGE_ROWS, MERGE_COLS)
    wrap = pl.BlockSpec(blk, lambda i, j: (i, j % nj))
    pr_spec = pl.BlockSpec(blk, lambda i, j: (i, jnp.minimum(j, nj - 1)))
    pa_spec = pl.BlockSpec(blk, lambda i, j: (i, jnp.maximum(j - nj, 0)))
    out = pl.BlockSpec(blk, lambda i, j: (i, j))
    return pl.pallas_call(
        body, name="merge_bwd", out_shape=(SDS((T, 2 * D), bf16), SDS((T, 2 * D), bf16)),
        grid=(T // MERGE_ROWS, 2 * nj),
        in_specs=[out, pr_spec, pa_spec, wrap], out_specs=(out, out),
        compiler_params=_params(("parallel", "parallel")))(gates, pr, pa, dm)


FFN_COLS = 256
GELU_C = math.sqrt(2.0 / math.pi)
GELU_A = 0.044715


def _gelu_parts(x):
    q = x * x
    t = jnp.tanh(x * (GELU_C + (GELU_C * GELU_A) * q))
    h = 0.5 + 0.5 * t
    return x * h, h * (1.0 + x * (1.0 - t) * (GELU_C + (3.0 * GELU_C * GELU_A) * q))


def _ffn_act_fwd(gpre, up, cw, cb):
    def body(g_ref, u_ref, cw_ref, cb_ref, o_ref):
        row = lax.broadcasted_iota(jnp.int32, (SEQ, FFN_COLS), 0)
        gate = _conv_fwd(g_ref[...].astype(f32), cw_ref, cb_ref[...], row)
        o_ref[...] = (_gelu_parts(gate)[0] * u_ref[...].astype(f32)).astype(bf16)

    T = gpre.shape[0]
    blk = lambda: pl.BlockSpec((SEQ, FFN_COLS), lambda b, j: (b, j))
    return pl.pallas_call(
        body, name="ffn_act_fwd", out_shape=SDS((T, FFN_W), bf16), grid=(T // SEQ, FFN_W // FFN_COLS),
        in_specs=[blk(), blk(), pl.BlockSpec((FFN_CONV, FFN_COLS), lambda b, j: (0, j)),
                  pl.BlockSpec((1, FFN_COLS), lambda b, j: (0, j))],
        out_specs=blk(), compiler_params=_params(("parallel", "parallel")))(gpre, up, cw, cb)


def _ffn_act_bwd(gpre, up, cw, cb, dact):
    def body(g_ref, u_ref, cw_ref, cb_ref, da_ref, dg_ref, du_ref, dcw_ref, dcb_ref):
        row = lax.broadcasted_iota(jnp.int32, (SEQ, FFN_COLS), 0)
        gp = g_ref[...].astype(f32)
        gate = _conv_fwd(gp, cw_ref, cb_ref[...], row)
        gel, dgel = _gelu_parts(gate)
        da = da_ref[...].astype(f32)
        du_ref[...] = (da * gel).astype(bf16)
        dgate = da * u_ref[...].astype(f32) * dgel
        dx, dws, db = _conv_bwd(gp, cw_ref, dgate, row)
        dg_ref[...] = dx.astype(bf16)
        first = pl.program_id(1) == 0

        def acc(ref, val):
            @pl.when(first)
            def _():
                ref[...] = val

            @pl.when(jnp.logical_not(first))
            def _():
                ref[...] += val

        for k in range(FFN_CONV):
            acc(dcw_ref.at[k:k + 1, :], dws[k])
        acc(dcb_ref, db)

    T = gpre.shape[0]
    blk = lambda: pl.BlockSpec((SEQ, FFN_COLS), lambda j, b: (b, j))
    cws = lambda: pl.BlockSpec((FFN_CONV, FFN_COLS), lambda j, b: (0, j))
    cbs = lambda: pl.BlockSpec((1, FFN_COLS), lambda j, b: (0, j))
    return pl.pallas_call(
        body, name="ffn_act_bwd",
        out_shape=(SDS((T, FFN_W), bf16), SDS((T, FFN_W), bf16), SDS((FFN_CONV, FFN_W), f32), SDS((1, FFN_W), f32)),
        grid=(FFN_W // FFN_COLS, T // SEQ),
        in_specs=[blk(), blk(), cws(), cbs(), blk()], out_specs=(blk(), blk(), cws(), cbs()),
        compiler_params=_params(("parallel", "arbitrary")))(gpre, up, cw, cb, dact)


def _coords():
    return lax.axis_index("x"), lax.axis_index("y"), lax.axis_index("c")


def _dev_index(dev):
    return 4 * dev[0] + 2 * dev[1] + dev[2]


def _dma_sems(n):
    return [pltpu.SemaphoreType.DMA((n,)), pltpu.SemaphoreType.DMA((n,))]


def _gather_two_level(arrays):
    n = len(arrays)

    def plan(ins, outs, sems):
        send_sems, recv_sems, local_sems = sems
        x, y, c = _coords()
        me, sibling = (x, y, c), (x, y, 1 - c)
        chips = [(1 - x, y), (x, 1 - y), (1 - x, 1 - y)]

        def copy(a, k, block, to, own=False):
            dst = outs[a].at[_dev_index(block)]
            return pltpu.make_async_remote_copy(
                src_ref=ins[a] if own else dst, dst_ref=dst, send_sem=send_sems.at[7 * a + k],
                recv_sem=recv_sems.at[7 * a + k], device_id=to, device_id_type=MESH)

        mine = [pltpu.make_async_copy(ins[a], outs[a].at[_dev_index(me)], local_sems.at[a]) for a in range(n)]
        first = [copy(a, 0, me, sibling, own=True) for a in range(n)]
        first += [copy(a, 1 + j, me, (*chip, c), own=True) for a in range(n) for j, chip in enumerate(chips)]
        passed = [[copy(a, 4 + j, (*chip, c), sibling) for a in range(n)] for j, chip in enumerate(chips)]
        arrive_ici = [[copy(a, 1 + j, (*chip, c), me) for a in range(n)] for j, chip in enumerate(chips)]
        arrive_d2d = [copy(a, 0, sibling, me) for a in range(n)]
        arrive_d2d += [copy(a, 4 + j, (*chip, 1 - c), me) for a in range(n) for j, chip in enumerate(chips)]
        return mine, first, passed, arrive_ici, arrive_d2d

    def start(ins, outs, sems):
        mine, first, _, _, _ = plan(ins, outs, sems)
        for cp in mine + first:
            cp.start()

    def finish(ins, outs, sems):
        mine, first, passed, arrive_ici, arrive_d2d = plan(ins, outs, sems)
        for j in range(3):
            for cp in arrive_ici[j]:
                cp.wait_recv()
            for cp in passed[j]:
                cp.start()
        for cp in arrive_d2d:
            cp.wait_recv()
        for cp in first + [cp for group in passed for cp in group]:
            cp.wait_send()
        for cp in mine:
            cp.wait()

    return _Comm(arrays, [SDS((N_DEV,) + a.shape, a.dtype) for a in arrays],
                 _dma_sems(7 * n) + [pltpu.SemaphoreType.DMA((n,))], start, finish)


def _gather_direct(arrays):
    n = len(arrays)

    def plan(ins, outs, sems):
        send_sems, recv_sems, local_sems = sems
        x, y, c = _coords()
        me = (x, y, c)
        mine = [pltpu.make_async_copy(ins[a], outs[a].at[_dev_index(me)], local_sems.at[a]) for a in range(n)]
        sends, arrivals = [], []
        for a in range(n):
            for k in range(1, N_DEV):
                peer = (1 - x if k & 4 else x, 1 - y if k & 2 else y, 1 - c if k & 1 else c)
                s = 7 * a + k - 1
                for slot, out in ((me, sends), (peer, arrivals)):
                    out.append(pltpu.make_async_remote_copy(
                        src_ref=ins[a], dst_ref=outs[a].at[_dev_index(slot)], send_sem=send_sems.at[s],
                        recv_sem=recv_sems.at[s], device_id=peer, device_id_type=MESH))
        return mine, sends, arrivals

    def start(ins, outs, sems):
        mine, sends, _ = plan(ins, outs, sems)
        for cp in mine + sends:
            cp.start()

    def finish(ins, outs, sems):
        mine, sends, arrivals = plan(ins, outs, sems)
        for cp in arrivals:
            cp.wait_recv()
        for cp in sends:
            cp.wait_send()
        for cp in mine:
            cp.wait()

    return _Comm(arrays, [SDS((N_DEV,) + a.shape, a.dtype) for a in arrays],
                 _dma_sems(7 * n) + [pltpu.SemaphoreType.DMA((n,))], start, finish)


def _scatter_direct(arrays):
    n = len(arrays)

    def plan(ins, outs, sems):
        send_sems, recv_sems = sems
        x, y, c = _coords()
        cps = []
        for a in range(n):
            for k in range(1, N_DEV):
                peer = (1 - x if k & 4 else x, 1 - y if k & 2 else y, 1 - c if k & 1 else c)
                s = 7 * a + k - 1
                cps.append(pltpu.make_async_remote_copy(
                    src_ref=ins[a].at[_dev_index(peer)], dst_ref=outs[a].at[k - 1], send_sem=send_sems.at[s],
                    recv_sem=recv_sems.at[s], device_id=peer, device_id_type=MESH))
        return cps

    def start(ins, outs, sems):
        for cp in plan(ins, outs, sems):
            cp.start()

    def finish(ins, outs, sems):
        for cp in plan(ins, outs, sems):
            cp.wait()

    return _Comm(arrays, [SDS((N_DEV - 1,) + a.shape[1:], a.dtype) for a in arrays], _dma_sems(7 * n),
                 start, finish)


def _run(comm, name):
    def body(*refs):
        k_in, k_out = len(comm.inputs), len(comm.out_shapes)
        ins, outs, sems = refs[:k_in], refs[k_in:k_in + k_out], refs[k_in + k_out:]
        comm.start(ins, outs, sems)
        comm.finish(ins, outs, sems)

    return pl.pallas_call(body, name=name, out_shape=comm.out_shapes, in_specs=[ANY] * len(comm.inputs),
                          out_specs=(ANY,) * len(comm.out_shapes), scratch_shapes=comm.sem_shapes)(*comm.inputs)


TILE_ELEMS = 192 * 1024


def _row_tile(R, C):
    if R * C <= TILE_ELEMS:
        return R
    return max(t for t in range(SUBLANES, R, SUBLANES) if R % t == 0 and t * C <= TILE_ELEMS)


def _adamw_math(w, g, m, v):
    m = ADAM_B1 * m + (1.0 - ADAM_B1) * g
    v = ADAM_B2 * v + (1.0 - ADAM_B2) * (g * g)
    m_hat = m / (1.0 - ADAM_B1 ** ADAM_STEP)
    v_hat = v / (1.0 - ADAM_B2 ** ADAM_STEP)
    delta = -ADAM_LR * (m_hat / (jnp.sqrt(v_hat) + ADAM_EPS) + ADAM_WD * w)
    return delta, m, v


def _adamw_sharded(own, recv, d_idx, w, m, v, name):
    R, C = w.shape
    t = _row_tile(R, C)

    def body(k_ref, p_ref, r_ref, w_ref, m_ref, v_ref, g_ref, d_ref, nm_ref, nv_ref):
        g = p_ref[...].astype(f32)
        for j in range(N_DEV - 1):
            g = g + r_ref[j].astype(f32)
        d, nm, nv = _adamw_math(w_ref[...], g, m_ref[...], v_ref[...])
        g_ref[...], d_ref[...], nm_ref[...], nv_ref[...] = g, d, nm, nv

    tile = lambda: pl.BlockSpec((t, C), lambda i, k: (i, 0))
    return pl.pallas_call(
        body, name="adamw_" + name, out_shape=(SDS((R, C), f32),) * 4,
        grid_spec=pltpu.PrefetchScalarGridSpec(
            num_scalar_prefetch=1, grid=(R // t,),
            in_specs=[pl.BlockSpec((None, t, C), lambda i, k: (k[0], i, 0)),
                      pl.BlockSpec((N_DEV - 1, t, C), lambda i, k: (0, i, 0)), tile(), tile(), tile()],
            out_specs=(tile(), tile(), tile(), tile())),
        compiler_params=_params(("parallel",)))(d_idx, own, recv, w, m, v)


def _adamw_replicated(parts, ws, ms, vs):
    n = len(ws)

    def body(*refs):
        p, w, m, v = (refs[i * n:(i + 1) * n] for i in range(4))
        outs = refs[4 * n:]
        for a in range(n):
            g = p[a][0].astype(f32)
            for j in range(1, N_DEV):
                g = g + p[a][j].astype(f32)
            d, nm, nv = _adamw_math(w[a][...], g, m[a][...], v[a][...])
            for i, val in enumerate((g, d, nm, nv)):
                outs[i * n + a][...] = val

    shapes = tuple(SDS(w.shape, f32) for w in ws)
    res = pl.pallas_call(body, name="adamw_replicated", out_shape=shapes * 4,
                         compiler_params=_params())(*parts, *ws, *ms, *vs)
    return [res[i * n:(i + 1) * n] for i in range(4)]


def _cols_to_full(g):
    n, r, c = g.shape
    return g.transpose(1, 0, 2).reshape(r, n * c)


def _full_to_cols(a):
    r, c = a.shape
    return a.reshape(r, N_DEV, c // N_DEV).transpose(1, 0, 2)


def _rows_blocked(a):
    r, c = a.shape
    return a.reshape(N_DEV, r // N_DEV, c)


def _w_in_to_internal(w):
    K = w.shape[0]
    q = w[:, 1280:2816].reshape(K, NG, KVH, 1, HD).transpose(0, 2, 1, 3, 4).reshape(K, KVH, NG, HD)
    k = w[:, 2816:3328].reshape(K, KVH, 1, HD)
    v = w[:, 3328:3840].reshape(K, KVH, 1, HD)
    att = jnp.concatenate([q, k, v], axis=2).reshape(K, KVH * ATT_COLS)
    return jnp.concatenate([w[:, :1280], att, w[:, 3840:]], axis=1)


def _w_in_from_internal(w):
    K = w.shape[0]
    att = w[:, C_ATT:C_GATE].reshape(K, KVH, 5, HD)
    q = att[:, :, 0:3].transpose(0, 2, 1, 3).reshape(K, NG * KVH * HD)
    k = att[:, :, 3].reshape(K, KVH * HD)
    v = att[:, :, 4].reshape(K, KVH * HD)
    return jnp.concatenate([w[:, :C_ATT], q, k, v, w[:, C_GATE:]], axis=1)


_IN_NAMES = ('x', 'rel_bias', 'norm_mix_pre', 'norm_mix_post', 'w_in', 'conv_rnn_w', 'conv_rnn_b', 'w_rg_a', 'b_rg_a',
             'w_rg_x', 'b_rg_x', 'lru_lambda', 'w_branch_rnn', 'w_branch_att', 'w_out', 'norm_ffn_pre',
             'norm_ffn_post', 'w_ffn_gate', 'w_ffn_up', 'conv_ffn_w', 'conv_ffn_b', 'w_ffn_down')
_WEIGHTS = _IN_NAMES[1:]
_SHARDED = {"w_in": "col", "conv_rnn_w": "col", "w_branch_rnn": "row", "w_branch_att": "col", "w_out": "row",
            "w_ffn_gate": "col", "w_ffn_up": "col", "conv_ffn_w": "col", "w_ffn_down": "row"}
_REPLICATED = tuple(n for n in _WEIGHTS if n not in _SHARDED)


def _flat2(a):
    return a.reshape(-1, a.shape[-1])


def _train_step(inp):
    x_idx, y_idx, c_idx = _coords()
    W = {n: inp[n] for n in _WEIGHTS}
    x = inp["x"].reshape(-1, D)
    target = inp["loss_target"].reshape(-1, D)
    shard = {n: inp[n][0] for n in _SHARDED}

    hn, (g_in, g_cr, g_cf) = _norm_in(x, W["norm_mix_pre"], comm=_gather_two_level(
        [shard["w_in"].astype(bf16), shard["conv_rnn_w"], shard["conv_ffn_w"]]))
    w_in = _w_in_to_internal(_cols_to_full(g_in))
    cw_rnn, cw_ffn = _cols_to_full(g_cr), _cols_to_full(g_cf)
    behind_rnn = ("w_branch_rnn", "w_branch_att", "w_out", "w_ffn_down")
    behind_att = ("w_ffn_gate", "w_ffn_up")

    wa, wx = W["w_rg_a"][0], W["w_rg_x"][0]
    buckets = jnp.asarray(_bucket_maps())

    proj = _mm(hn, w_in[:, :C_GATE], "nn", f32, "mm_proj", 512, C_GATE // 2, 1024, cols_outer=True)
    gates = _mm(hn, w_in[:, C_GATE:], "nn", bf16, "mm_gates", 1024, 1024, 1024, cols_outer=True)
    rnn_saved, got = _rnn_fwd(proj, cw_rnn, W["conv_rnn_b"], wa, W["b_rg_a"], wx, W["b_rg_x"], W["lru_lambda"],
                              comm=_gather_direct([shard[n].astype(bf16) for n in behind_rnn]))
    h_rnn = rnn_saved[0]
    gathered = dict(zip(behind_rnn, got))
    bias = _bias_tables(W["rel_bias"], buckets)
    (o_att, lse, *att_saved), got = _att_fwd(
        proj, bias, comm=_gather_direct([shard[n].astype(bf16) for n in behind_att]))
    gathered.update(zip(behind_att, got))
    w_brnn = gathered["w_branch_rnn"].reshape(RNN_W, D)
    w_batt = _cols_to_full(gathered["w_branch_att"])
    w_out = gathered["w_out"].reshape(D, D)
    w_gate, w_up = _cols_to_full(gathered["w_ffn_gate"]), _cols_to_full(gathered["w_ffn_up"])
    w_down = gathered["w_ffn_down"].reshape(FFN_W, D)
    pr = _mm(h_rnn, w_brnn, "nn", bf16, "mm_pr", 1024, 1024, 1280)
    pa = _mm(o_att, w_batt, "nn", bf16, "mm_pa", 1024, 1024, 512)
    merged = _merge_fwd(gates, pr, pa)
    (mix, h1, hn2), _ = _mm_rows([(merged, w_out, 1024)], "nn", "mm_mix", 1024, _mid_fwd_rows, [x],
                                 [W["norm_mix_post"], W["norm_ffn_pre"]], [f32, f32, bf16], [])
    gpre = _mm(hn2, w_gate, "nn", bf16, "mm_gate", 1024, 1024, 1024, cols_outer=True)
    up = _mm(hn2, w_up, "nn", bf16, "mm_up", 1024, 1024, 1024, cols_outer=True)
    act = _ffn_act_fwd(gpre, up, cw_ffn, W["conv_ffn_b"])
    (dy, dff), (loss_part, dg_fpost) = _mm_rows([(act, w_down, 1024)], "nn", "mm_down", 1024, _final_rows,
                                                [h1, target], [W["norm_ffn_post"]], [f32, bf16], [1, D])

    grads = {}
    dact = _mm(dff, w_down, "nt", bf16, "mm_dact", 1024, 1024, 1024, cols_outer=True)
    grads["w_ffn_down"] = _rows_blocked(_mm(act, dff, "tn", bf16, "mm_dw_down", 1024, 1024, 2048))
    dgpre, dup, dcw_ffn, dcb_ffn = _ffn_act_bwd(gpre, up, cw_ffn, W["conv_ffn_b"], dact)
    grads["conv_ffn_w"] = _full_to_cols(dcw_ffn.astype(bf16))
    grads["w_ffn_gate"] = _full_to_cols(_mm(hn2, dgpre, "tn", bf16, "mm_dw_gate", 1024, 1024, 2048))
    grads["w_ffn_up"] = _full_to_cols(_mm(hn2, dup, "tn", bf16, "mm_dw_up", 1024, 1024, 2048))
    dhn2 = _mm_nt_sum([(dgpre, w_gate, 1024), (dup, w_up, 1024)], "mm_dhn2", 1024, 1024)
    dh1, dmix, dg_fpre, dg_post = _mid_bwd(dy, dhn2, h1, W["norm_ffn_pre"], mix, W["norm_mix_post"])
    dmerged = _mm(dmix, w_out, "nt", bf16, "mm_dmerged", 1024, 1024, 1024)
    grads["w_out"] = _rows_blocked(_mm(merged, dmix, "tn", bf16, "mm_dw_out", 1024, 1024, 2048))
    dprpa, dgates = _merge_bwd(gates, pr, pa, dmerged)
    dpr, dpa = dprpa[:, :D], dprpa[:, D:]
    dh_rnn = _mm(dpr, w_brnn, "nt", bf16, "mm_dh_rnn", 1024, 1280, 1024)
    grads["w_branch_rnn"] = _rows_blocked(_mm(h_rnn, dpr, "tn", bf16, "mm_dw_brnn", 1280, 1024, 1024))
    do_att = _mm(dpa, w_batt, "nt", f32, "mm_do_att", 1024, 512, 1024)
    grads["w_branch_att"] = _full_to_cols(_mm(o_att, dpa, "tn", bf16, "mm_dw_batt", 512, 1024, 2048))

    received = {}
    behind_att_bwd = ("w_ffn_down", "w_ffn_gate", "conv_ffn_w", "w_out")
    behind_rnn_bwd = ("w_ffn_up", "w_branch_rnn", "w_branch_att")
    (dqkv, dbias), got = _att_bwd(proj, att_saved, bias, o_att, lse, do_att,
                                  comm=_scatter_direct([grads[n] for n in behind_att_bwd]))
    received.update(zip(behind_att_bwd, got))
    drel = _bias_grad(dbias, buckets)
    (dxr, dcw_rnn, dcb_rnn, dwa, dba, dwx, dbx, dlam), got = _rnn_bwd(
        proj, rnn_saved, dh_rnn, cw_rnn, wa, wx, W["lru_lambda"],
        comm=_scatter_direct([grads[n] for n in behind_rnn_bwd]))
    received.update(zip(behind_rnn_bwd, got))
    gsmall = {"rel_bias": drel, "norm_mix_post": dg_post, "conv_rnn_b": dcb_rnn, "w_rg_a": dwa.astype(bf16),
              "b_rg_a": dba, "w_rg_x": dwx.astype(bf16), "b_rg_x": dbx, "lru_lambda": dlam,
              "norm_ffn_pre": dg_fpre, "norm_ffn_post": dg_fpost, "conv_ffn_b": dcb_ffn}
    dw_in_a, parts = _mm(hn, dqkv, "tn", bf16, "mm_dw_in_a", 1024, 1280, 1024,
                         comm=_gather_direct([_flat2(gsmall[n]) for n in gsmall]))
    parts = dict(zip(gsmall, parts))
    dw_in = jnp.concatenate([_mm(hn, dxr, "tn", bf16, "mm_dw_in_r", 1024, 1280, 1024), dw_in_a,
                             _mm(hn, dgates, "tn", bf16, "mm_dw_in_g", 1024, 1024, 2048)], axis=1)
    grads["w_in"] = _full_to_cols(_w_in_from_internal(dw_in))
    grads["conv_rnn_w"] = _full_to_cols(dcw_rnn.astype(bf16))
    behind_dhn = ("w_in", "conv_rnn_w")
    dhn, got = _mm_nt_sum([(dxr, w_in[:, :C_ATT], 1280), (dqkv, w_in[:, C_ATT:C_GATE], 1280),
                           (dgates, w_in[:, C_GATE:], 1024)], "mm_dhn", 1024, 1024,
                          comm=_scatter_direct([grads[n] for n in behind_dhn]))
    received.update(zip(behind_dhn, got))
    dx, dg_pre = _in_bwd(dh1, dhn, x, W["norm_mix_pre"])
    parts["norm_mix_pre"], = _run(_gather_two_level([dg_pre]), "ag_norm_mix_pre")
    parts = [parts[n] for n in _REPLICATED]

    out = {}
    d_arr = jnp.reshape(4 * x_idx + 2 * y_idx + c_idx, (1,)).astype(jnp.int32)
    for n in _SHARDED:
        res = _adamw_sharded(grads[n], received[n], d_arr, shard[n], inp["m_" + n][0], inp["v_" + n][0], n)
        out[n] = [r[None] for r in res]
    small = _adamw_replicated(parts, *[[_flat2(inp[p + n]) for n in _REPLICATED] for p in ("", "m_", "v_")])
    for a, n in enumerate(_REPLICATED):
        out[n] = [small[i][a].reshape(inp[n].shape) for i in range(4)]

    loss = lax.psum(loss_part[0, 0], ("x", "y", "c"))
    outs = [loss, dx.reshape(inp["x"].shape)]
    for i in range(4):
        outs.extend(out[n][i] for n in _WEIGHTS)
    return tuple(outs)


def kernel(x, rel_bias, norm_mix_pre, norm_mix_post, w_in, conv_rnn_w, conv_rnn_b, w_rg_a, b_rg_a, w_rg_x, b_rg_x, lru_lambda, w_branch_rnn, w_branch_att, w_out, norm_ffn_pre, norm_ffn_post, w_ffn_gate, w_ffn_up, conv_ffn_w, conv_ffn_b, w_ffn_down, loss_target, m_rel_bias, m_norm_mix_pre, m_norm_mix_post, m_w_in, m_conv_rnn_w, m_conv_rnn_b, m_w_rg_a, m_b_rg_a, m_w_rg_x, m_b_rg_x, m_lru_lambda, m_w_branch_rnn, m_w_branch_att, m_w_out, m_norm_ffn_pre, m_norm_ffn_post, m_w_ffn_gate, m_w_ffn_up, m_conv_ffn_w, m_conv_ffn_b, m_w_ffn_down, v_rel_bias, v_norm_mix_pre, v_norm_mix_post, v_w_in, v_conv_rnn_w, v_conv_rnn_b, v_w_rg_a, v_b_rg_a, v_w_rg_x, v_b_rg_x, v_lru_lambda, v_w_branch_rnn, v_w_branch_att, v_w_out, v_norm_ffn_pre, v_norm_ffn_post, v_w_ffn_gate, v_w_ffn_up, v_conv_ffn_w, v_conv_ffn_b, v_w_ffn_down):
    vals = locals()
    names = list(_IN_NAMES) + ["loss_target"] + ["m_" + n for n in _WEIGHTS] + ["v_" + n for n in _WEIGHTS]
    return _train_step({n: vals[n] for n in names})
```

```python
import functools
import math

import numpy as np
import jax
import jax.numpy as jnp
from jax import lax
from jax.experimental import pallas as pl
from jax.experimental.pallas import tpu as pltpu

f32, bf16 = jnp.float32, jnp.bfloat16
SDS = jax.ShapeDtypeStruct
MESH = pl.DeviceIdType.MESH
ANY = pl.BlockSpec(memory_space=pl.ANY)

D = 1024
SEQ = 2048
RNN_W = 1280
RNN_BLOCKS = 10
LANES = 128
SUBLANES = 8
RNN_CONV = 4
LRU_C = 8.0
HD = 128
KVH = 4
DILATIONS = (1, 4, 16)
NG = 3
ATT_BLK = 128
NBLK_SEQ = SEQ // ATT_BLK
STAT_LANE = 64
ATT_UNROLL = 8
REL_BUCKETS = 32
REL_MAX_DIST = 2048
FFN_W = 3072
FFN_CONV = 3
EPS = 1e-6
IN_W = 5888
ATT_COLS = 5 * HD
C_ATT = RNN_W
C_GATE = RNN_W + KVH * ATT_COLS
NEG = -1e30

ADAM_LR, ADAM_B1, ADAM_B2, ADAM_EPS, ADAM_WD, ADAM_STEP = 0.001, 0.9, 0.999, 1e-08, 0.01, 10

VMEM_LIMIT_BYTES = 56 * 1024 * 1024
N_DEV = 8


def _params(sem=None):
    return pltpu.CompilerParams(dimension_semantics=sem, vmem_limit_bytes=VMEM_LIMIT_BYTES)


def _sigmoid(x):
    return 1.0 / (1.0 + jnp.exp(-x))


class _Comm:
    def __init__(self, inputs, out_shapes, sem_shapes, start, finish):
        self.inputs, self.out_shapes, self.sem_shapes = tuple(inputs), tuple(out_shapes), list(sem_shapes)
        self.start, self.finish = start, finish


def _call(body, args, *, name, grid, in_specs, out_specs, out_shape, scratch_shapes=(), semantics, comm=None):
    if comm is None:
        return pl.pallas_call(body, name=name, grid=grid, in_specs=list(in_specs), out_specs=tuple(out_specs),
                              out_shape=tuple(out_shape), scratch_shapes=list(scratch_shapes),
                              compiler_params=_params(semantics))(*args), ()
    n_in, n_out, n_scr = len(in_specs), len(out_shape), len(scratch_shapes)
    c_in, c_out = len(comm.inputs), len(comm.out_shapes)

    def fused(*refs):
        ins, refs = refs[:n_in], refs[n_in:]
        cin, refs = refs[:c_in], refs[c_in:]
        outs, refs = refs[:n_out], refs[n_out:]
        cout, refs = refs[:c_out], refs[c_out:]
        scr, csem = refs[:n_scr], refs[n_scr:]
        first = functools.reduce(jnp.logical_and, [pl.program_id(d) == 0 for d in range(len(grid))])
        last = functools.reduce(jnp.logical_and, [pl.program_id(d) == grid[d] - 1 for d in range(len(grid))])

        @pl.when(first)
        def _():
            comm.start(cin, cout, csem)

        body(*ins, *outs, *scr)

        @pl.when(last)
        def _():
            comm.finish(cin, cout, csem)

    res = pl.pallas_call(
        fused, name=name, grid=grid, in_specs=list(in_specs) + [ANY] * c_in,
        out_specs=tuple(out_specs) + (ANY,) * c_out, out_shape=tuple(out_shape) + comm.out_shapes,
        scratch_shapes=list(scratch_shapes) + comm.sem_shapes,
        compiler_params=_params(("arbitrary",) * len(grid)))(*args, *comm.inputs)
    return res[:n_out], res[n_out:]


_DIMS = {"nn": (((1,), (0,)), ((), ())), "nt": (((1,), (1,)), ((), ())), "tn": (((0,), (0,)), ((), ()))}


def _mm(a, b, mode, out_dtype, name, tm, tn, tk, cols_outer=False, comm=None):
    (a, a_c0, a_w), (b, b_c0, b_w) = [x if isinstance(x, tuple) else (x, 0, x.shape[1]) for x in (a, b)]
    if mode == "nn":
        (M, K), (K2, N) = (a.shape[0], a_w), (b.shape[0], b_w)
    elif mode == "nt":
        (M, K), (N, K2) = (a.shape[0], a_w), (b.shape[0], b_w)
    else:
        (K, M), (K2, N) = (a.shape[0], a_w), (b.shape[0], b_w)
    assert K == K2 and M % tm == 0 and N % tn == 0 and K % tk == 0, (name, a.shape, b.shape)
    a_tile, b_tile = (tm if mode == "tn" else tk), (tk if mode == "nt" else tn)
    assert a_c0 % a_tile == 0 and b_c0 % b_tile == 0, name
    a_off, b_off = a_c0 // a_tile, b_c0 // b_tile
    nk = K // tk

    def body(a_ref, b_ref, o_ref, *scratch):
        part = lax.dot_general(a_ref[...].astype(bf16), b_ref[...].astype(bf16), _DIMS[mode],
                               preferred_element_type=f32)
        if nk == 1:
            o_ref[...] = part.astype(o_ref.dtype)
        else:
            acc_ref, = scratch
            k = pl.program_id(2)

            @pl.when(k == 0)
            def _():
                acc_ref[...] = part

            @pl.when(k > 0)
            def _():
                acc_ref[...] += part

            @pl.when(k == nk - 1)
            def _():
                o_ref[...] = acc_ref[...].astype(o_ref.dtype)

    def ij(f):
        return (lambda j, i, k: f(i, j, k)) if cols_outer else f

    if mode == "tn":
        a_spec = pl.BlockSpec((tk, tm), ij(lambda i, j, k: (k, i + a_off)))
    else:
        a_spec = pl.BlockSpec((tm, tk), ij(lambda i, j, k: (i, k + a_off)))
    if mode == "nt":
        b_spec = pl.BlockSpec((tn, tk), ij(lambda i, j, k: (j, k + b_off)))
    else:
        b_spec = pl.BlockSpec((tk, tn), ij(lambda i, j, k: (k, j + b_off)))
    o_spec = pl.BlockSpec((tm, tn), ij(lambda i, j, k: (i, j)))
    grid = (N // tn, M // tm, nk) if cols_outer else (M // tm, N // tn, nk)
    (out,), extra = _call(
        body, (a, b), name=name, out_shape=(SDS((M, N), out_dtype),), grid=grid, in_specs=[a_spec, b_spec],
        out_specs=(o_spec,), scratch_shapes=[pltpu.VMEM((tm, tn), f32)] if nk > 1 else [],
        semantics=("parallel", "parallel", "arbitrary"), comm=comm)
    return out if comm is None else (out, extra)


def _mm_nt_sum(pairs, name, tm, tn, comm=None):
    M, N = pairs[0][0].shape[0], pairs[0][1].shape[0]
    nks = [a.shape[1] // tk for a, _, tk in pairs]
    starts = [sum(nks[:p]) for p in range(len(pairs))]
    nk = sum(nks)

    def body(*refs):
        o_ref, acc_ref = refs[-2], refs[-1]
        k = pl.program_id(2)
        for p in range(len(pairs)):
            def product(p=p):
                return lax.dot_general(refs[2 * p][...], refs[2 * p + 1][...], _DIMS["nt"], preferred_element_type=f32)

            if p == 0:
                @pl.when(k == 0)
                def _():
                    acc_ref[...] = product()

            @pl.when((k >= max(starts[p], 1)) & (k < starts[p] + nks[p]))
            def _():
                acc_ref[...] += product()

        @pl.when(k == nk - 1)
        def _():
            o_ref[...] = acc_ref[...].astype(bf16)

    in_specs, args = [], []
    for (a, b, tk), k0, n in zip(pairs, starts, nks):
        assert a.shape[1] == b.shape[1] and a.shape[1] % tk == 0 and a.dtype == b.dtype == bf16, name
        chunk = lambda k, k0=k0, n=n: jnp.clip(k - k0, 0, n - 1)
        in_specs += [pl.BlockSpec((tm, tk), lambda i, j, k, c=chunk: (i, c(k))),
                     pl.BlockSpec((tn, tk), lambda i, j, k, c=chunk: (j, c(k)))]
        args += [a, b]
    o_spec = pl.BlockSpec((tm, tn), lambda i, j, k: (i, j))
    (out,), extra = _call(
        body, args, name=name, out_shape=(SDS((M, N), bf16),), grid=(M // tm, N // tn, nk), in_specs=in_specs,
        out_specs=(o_spec,), scratch_shapes=[pltpu.VMEM((tm, tn), f32)],
        semantics=("parallel", "parallel", "arbitrary"), comm=comm)
    return out if comm is None else (out, extra)


def _mm_rows(pairs, mode, name, tm, epilogue, rows_in, vecs_in, rows_out, vecs_out, comm=None):
    M = pairs[0][0].shape[0]
    N = pairs[0][1].shape[1 if mode == "nn" else 0]
    nks = [a.shape[1] // tk for a, _, tk in pairs]
    starts = [sum(nks[:p]) for p in range(len(pairs))]
    nk = sum(nks)
    n_rows_in, n_vecs_in, n_rows_out = len(rows_in), len(vecs_in), len(rows_out)

    def body(*refs):
        pair_refs, refs = refs[:2 * len(pairs)], refs[2 * len(pairs):]
        rin, refs = refs[:n_rows_in], refs[n_rows_in:]
        vin, refs = refs[:n_vecs_in], refs[n_vecs_in:]
        rout, refs = refs[:n_rows_out], refs[n_rows_out:]
        vout, acc_ref = refs[:-1], refs[-1]
        i, k = pl.program_id(0), pl.program_id(1)
        for p in range(len(pairs)):
            def product(p=p):
                return lax.dot_general(pair_refs[2 * p][...], pair_refs[2 * p + 1][...], _DIMS[mode],
                                       preferred_element_type=f32)

            if p == 0:
                @pl.when(k == 0)
                def _():
                    acc_ref[...] = product()

            @pl.when((k >= max(starts[p], 1)) & (k < starts[p] + nks[p]))
            def _():
                acc_ref[...] += product()

        @pl.when(k == nk - 1)
        def _():
            res = epilogue(acc_ref[...], *[r[...] for r in rin], *[v[...] for v in vin])
            for ref, val in zip(rout, res[:n_rows_out]):
                ref[...] = val.astype(ref.dtype)
            for ref, val in zip(vout, res[n_rows_out:]):
                @pl.when(i == 0)
                def _(ref=ref, val=val):
                    ref[...] = val

                @pl.when(i > 0)
                def _(ref=ref, val=val):
                    ref[...] += val

    in_specs, args = [], []
    for (a, b, tk), k0, n in zip(pairs, starts, nks):
        assert a.shape[1] % tk == 0 and a.dtype == b.dtype == bf16, name
        chunk = lambda k, k0=k0, n=n: jnp.clip(k - k0, 0, n - 1)
        in_specs.append(pl.BlockSpec((tm, tk), lambda i, k, c=chunk: (i, c(k))))
        if mode == "nn":
            in_specs.append(pl.BlockSpec((tk, N), lambda i, k, c=chunk: (c(k), 0)))
        else:
            in_specs.append(pl.BlockSpec((N, tk), lambda i, k, c=chunk: (0, c(k))))
        args += [a, b]
    row = lambda: pl.BlockSpec((tm, N), lambda i, k: (i, 0))
    vec = lambda w: pl.BlockSpec((1, w), lambda i, k: (0, 0))
    in_specs += [row() for _ in rows_in] + [vec(v.shape[1]) for v in vecs_in]
    outs, extra = _call(
        body, (*args, *rows_in, *vecs_in), name=name,
        out_shape=tuple(SDS((M, N), dt) for dt in rows_out) + tuple(SDS((1, w), f32) for w in vecs_out),
        grid=(M // tm, nk), in_specs=in_specs,
        out_specs=tuple(row() for _ in rows_out) + tuple(vec(w) for w in vecs_out),
        scratch_shapes=[pltpu.VMEM((tm, N), f32)], semantics=("arbitrary", "arbitrary"), comm=comm)
    res = (outs[:n_rows_out], outs[n_rows_out:])
    return res if comm is None else (res, extra)


ROW_TILE = 512


def _rms_fwd(x, g):
    r = lax.rsqrt(jnp.mean(x * x, axis=-1, keepdims=True) + EPS)
    return x * r * g


def _rms_bwd(x, g, dy):
    r = lax.rsqrt(jnp.mean(x * x, axis=-1, keepdims=True) + EPS)
    xh = x * r
    dxh = dy * g
    dx = r * (dxh - xh * jnp.mean(dxh * xh, axis=-1, keepdims=True))
    return dx, jnp.sum(dy * xh, axis=0, keepdims=True)


def _acc_out(ref, val):
    @pl.when(pl.program_id(0) == 0)
    def _():
        ref[...] = val

    @pl.when(pl.program_id(0) > 0)
    def _():
        ref[...] += val


def _row_spec(width=D):
    return pl.BlockSpec((ROW_TILE, width), lambda i: (i, 0))


def _vec_spec(width=D):
    return pl.BlockSpec((1, width), lambda i: (0, 0))


def _norm_in(x, g, comm=None):
    def body(x_ref, g_ref, o_ref):
        o_ref[...] = _rms_fwd(x_ref[...], g_ref[...]).astype(bf16)

    T = x.shape[0]
    (hn,), extra = _call(body, (x, g), name="norm_in", out_shape=(SDS((T, D), bf16),), grid=(T // ROW_TILE,),
                         in_specs=[_row_spec(), _vec_spec()], out_specs=(_row_spec(),), semantics=("parallel",),
                         comm=comm)
    return hn, extra


def _mid_bwd(dy, dhn2, h1, g_fpre, mix, g_post):
    def body(dy_ref, dhn2_ref, h1_ref, gf_ref, mix_ref, gp_ref, dh1_ref, dmix_ref, dgf_ref, dgp_ref):
        d1, dgf = _rms_bwd(h1_ref[...], gf_ref[...], dhn2_ref[...].astype(f32))
        dh1 = dy_ref[...] + d1
        dh1_ref[...] = dh1
        dmix, dgp = _rms_bwd(mix_ref[...], gp_ref[...], dh1)
        dmix_ref[...] = dmix.astype(bf16)
        _acc_out(dgf_ref, dgf)
        _acc_out(dgp_ref, dgp)

    T = dy.shape[0]
    return pl.pallas_call(
        body, name="mid_bwd", out_shape=(SDS((T, D), f32), SDS((T, D), bf16), SDS((1, D), f32), SDS((1, D), f32)),
        grid=(T // ROW_TILE,),
        in_specs=[_row_spec(), _row_spec(), _row_spec(), _vec_spec(), _row_spec(), _vec_spec()],
        out_specs=(_row_spec(), _row_spec(), _vec_spec(), _vec_spec()),
        compiler_params=_params(("arbitrary",)))(dy, dhn2, h1, g_fpre, mix, g_post)


def _in_bwd(dh1, dhn, x, g_pre):
    def body(dh1_ref, dhn_ref, x_ref, g_ref, dx_ref, dg_ref):
        d, dg = _rms_bwd(x_ref[...], g_ref[...], dhn_ref[...].astype(f32))
        dx_ref[...] = dh1_ref[...] + d
        _acc_out(dg_ref, dg)

    T = x.shape[0]
    return pl.pallas_call(
        body, name="in_bwd", out_shape=(SDS((T, D), f32), SDS((1, D), f32)), grid=(T // ROW_TILE,),
        in_specs=[_row_spec(), _row_spec(), _row_spec(), _vec_spec()], out_specs=(_row_spec(), _vec_spec()),
        compiler_params=_params(("arbitrary",)))(dh1, dhn, x, g_pre)


def _mid_fwd_rows(mix, x, g_post, g_fpre):
    h1 = x + _rms_fwd(mix, g_post)
    return mix, h1, _rms_fwd(h1, g_fpre)


def _final_rows(ff, h1, target, g_fpost):
    e = h1 + _rms_fwd(ff, g_fpost) - target
    part = jnp.sum(jnp.sum(e * e, axis=1, keepdims=True), axis=0, keepdims=True) * (0.5 / D)
    dy = e * (1.0 / D)
    dff, dg = _rms_bwd(ff, g_fpost, dy)
    return dy, dff, part, dg


def _shift_dn(x, d, row, fill=0.0):
    if d == 0:
        return x
    y = pltpu.roll(x, d, 0)
    head = jnp.where(row[:SUBLANES] >= d, y[:SUBLANES], fill)
    return jnp.concatenate([head, y[SUBLANES:]], axis=0)


def _shift_up(x, d, row, fill=0.0):
    if d == 0:
        return x
    n = x.shape[0]
    y = pltpu.roll(x, n - d, 0)
    tail = jnp.where(row[:SUBLANES] < SUBLANES - d, y[n - SUBLANES:], fill)
    return jnp.concatenate([y[:n - SUBLANES], tail], axis=0)


def _conv_fwd(x, w_ref, b, row):
    K = w_ref.shape[0]
    y = b
    for k in range(K):
        y = y + w_ref[k:k + 1, :] * _shift_dn(x, K - 1 - k, row)
    return y


def _conv_bwd(x, w_ref, dy, row):
    K = w_ref.shape[0]
    dx = jnp.zeros_like(dy)
    dws = []
    for k in range(K):
        dx = dx + w_ref[k:k + 1, :] * _shift_up(dy, K - 1 - k, row)
        dws.append(jnp.sum(dy * _shift_dn(x, K - 1 - k, row), axis=0, keepdims=True))
    return dx, dws, jnp.sum(dy, axis=0, keepdims=True)


def _scan_fwd(a, u, row):
    n = a.shape[0]
    d = 1
    while d < n:
        last = 2 * d >= n
        if d < SUBLANES:
            u = u + a * _shift_dn(u, d, row)
            if not last:
                a = a * _shift_dn(a, d, row, fill=1.0)
        else:
            u = jnp.concatenate([u[:d], u[d:] + a[d:] * u[:n - d]], axis=0)
            if not last:
                a = jnp.concatenate([a[:d], a[d:] * a[:n - d]], axis=0)
        d *= 2
    return u


def _scan_bwd(b, u, row):
    n = b.shape[0]
    d = 1
    while d < n:
        last = 2 * d >= n
        if d < SUBLANES:
            u = u + b * _shift_up(u, d, row)
            if not last:
                b = b * _shift_up(b, d, row, fill=1.0)
        else:
            u = jnp.concatenate([u[:n - d] + b[:n - d] * u[d:], u[n - d:]], axis=0)
            if not last:
                b = jnp.concatenate([b[:n - d] * b[d:], b[n - d:]], axis=0)
        d *= 2
    return u


def _neg_expm1(z):
    series = -z * (1.0 + z * (0.5 + z * (1.0 / 6.0 + z * (1.0 / 24.0 + z * (1.0 / 120.0)))))
    return jnp.where(z > -0.1, series, 1.0 - jnp.exp(z))


def _softplus_neg(lam):
    z = -lam
    return jnp.maximum(z, 0.0) + jnp.log(1.0 + jnp.exp(-jnp.abs(z)))


def _rnn_specs(B):
    blk = lambda: pl.BlockSpec((SEQ, LANES), lambda b, n: (b, n))
    return dict(
        act=blk,
        convw=pl.BlockSpec((RNN_CONV, LANES), lambda b, n: (0, n)),
        vec=lambda: pl.BlockSpec((1, LANES), lambda b, n: (0, n)),
        gate=lambda: pl.BlockSpec((None, LANES, LANES), lambda b, n: (n, 0, 0)),
    )


def _rnn_fwd(proj, cw, cb, wa, ba, wx, bx, lam, comm=None):
    T = proj.shape[0]
    B = T // SEQ

    def body(x_ref, cw_ref, cb_ref, wa_ref, ba_ref, wx_ref, bx_ref, lam_ref, h_ref, xc_ref, r_ref, i_ref, a_ref, s_ref):
        row = lax.broadcasted_iota(jnp.int32, (SEQ, LANES), 0)
        xc = _conv_fwd(x_ref[...], cw_ref, cb_ref[...], row)
        xcb = xc.astype(bf16)
        r = _sigmoid(jnp.dot(xcb, wa_ref[...].astype(bf16), preferred_element_type=f32) + ba_ref[...])
        i = _sigmoid(jnp.dot(xcb, wx_ref[...].astype(bf16), preferred_element_type=f32) + bx_ref[...])
        log_a = (-LRU_C * _softplus_neg(lam_ref[...])) * r
        a = jnp.exp(log_a)
        s = jnp.sqrt(_neg_expm1(2.0 * log_a))
        xc_ref[...], r_ref[...], i_ref[...], a_ref[...], s_ref[...] = xc, r, i, a, s
        h_ref[...] = _scan_fwd(a, s * (i * xc), row)

    sp_ = _rnn_specs(B)
    return _call(
        body, (proj, cw, cb, wa, ba, wx, bx, lam), name="rnn_fwd", out_shape=(SDS((T, RNN_W), f32),) * 6,
        grid=(B, RNN_BLOCKS),
        in_specs=[sp_["act"](), sp_["convw"], sp_["vec"](), sp_["gate"](), sp_["vec"](), sp_["gate"](),
                  sp_["vec"](), sp_["vec"]()],
        out_specs=tuple(sp_["act"]() for _ in range(6)), semantics=("parallel", "parallel"), comm=comm)


def _rnn_bwd(proj, saved, dh, cw, wa, wx, lam, comm=None):
    T = proj.shape[0]
    B = T // SEQ

    def body(x_ref, h_ref, xc_ref, r_ref, i_ref, a_ref, s_ref, dh_ref, cw_ref, wa_ref, wx_ref, lam_ref,
             dx_ref, dcw_ref, dcb_ref, dwa_ref, dba_ref, dwx_ref, dbx_ref, dlam_ref):
        row = lax.broadcasted_iota(jnp.int32, (SEQ, LANES), 0)
        xr = x_ref[...]
        wa, wx, lam = wa_ref[...], wx_ref[...], lam_ref[...]
        xc, r, i, a, s = xc_ref[...], r_ref[...], i_ref[...], a_ref[...], s_ref[...]
        xcb = xc.astype(bf16)
        sp = _softplus_neg(lam)
        hprev = _shift_dn(h_ref[...], 1, row)
        g = _scan_bwd(_shift_up(a, 1, row), dh_ref[...].astype(f32), row)
        da = g * hprev
        ds = g * (i * xc)
        di = g * (s * xc)
        dxc = g * (s * i)
        dla = da * a - ds * (a * a) / s
        dr = dla * (-LRU_C * sp)
        dsp = jnp.sum(dla * (-LRU_C * r), axis=0, keepdims=True)
        dlam = -dsp * _sigmoid(-lam)
        dga = dr * r * (1.0 - r)
        dgx = di * i * (1.0 - i)
        dgab, dgxb = dga.astype(bf16), dgx.astype(bf16)
        dwa = lax.dot_general(xcb, dgab, _DIMS["tn"], preferred_element_type=f32)
        dwx = lax.dot_general(xcb, dgxb, _DIMS["tn"], preferred_element_type=f32)
        dxc = dxc + lax.dot_general(dgab, wa.astype(bf16), _DIMS["nt"], preferred_element_type=f32)
        dxc = dxc + lax.dot_general(dgxb, wx.astype(bf16), _DIMS["nt"], preferred_element_type=f32)
        dx, dws, db = _conv_bwd(xr, cw_ref, dxc, row)
        dx_ref[...] = dx.astype(bf16)
        first = pl.program_id(1) == 0

        def acc(ref, val):
            @pl.when(first)
            def _():
                ref[...] = val

            @pl.when(jnp.logical_not(first))
            def _():
                ref[...] += val

        for k in range(RNN_CONV):
            acc(dcw_ref.at[k:k + 1, :], dws[k])
        acc(dcb_ref, db)
        acc(dwa_ref, dwa)
        acc(dba_ref, jnp.sum(dga, axis=0, keepdims=True))
        acc(dwx_ref, dwx)
        acc(dbx_ref, jnp.sum(dgx, axis=0, keepdims=True))
        acc(dlam_ref, dlam)

    blk = lambda: pl.BlockSpec((SEQ, LANES), lambda n, b: (b, n))
    convw = lambda: pl.BlockSpec((RNN_CONV, LANES), lambda n, b: (0, n))
    vec = lambda: pl.BlockSpec((1, LANES), lambda n, b: (0, n))
    gate = lambda: pl.BlockSpec((None, LANES, LANES), lambda n, b: (n, 0, 0))
    vshape = SDS((1, RNN_W), f32)
    gshape = SDS((RNN_BLOCKS, LANES, LANES), f32)
    return _call(
        body, (proj, *saved, dh, cw, wa, wx, lam), name="rnn_bwd",
        out_shape=(SDS((T, RNN_W), bf16), SDS((RNN_CONV, RNN_W), f32), vshape, gshape, vshape, gshape, vshape, vshape),
        grid=(RNN_BLOCKS, B),
        in_specs=[blk() for _ in range(8)] + [convw(), gate(), gate(), vec()],
        out_specs=(blk(), convw(), vec(), gate(), vec(), gate(), vec(), vec()),
        semantics=("parallel", "arbitrary"), comm=comm)


def _t5_bucket(dist):
    max_exact = REL_BUCKETS // 2
    d = np.maximum(dist, 1).astype(np.float32)
    large = max_exact + np.log(d / max_exact) / math.log(REL_MAX_DIST / max_exact) * (REL_BUCKETS - max_exact)
    large = np.minimum(large.astype(np.int32), REL_BUCKETS - 1)
    return np.where(dist < max_exact, dist, large).astype(np.int32)


def _bucket_maps():
    qi = np.arange(ATT_BLK)[:, None]
    kj = np.arange(2 * ATT_BLK)[None, :]
    delta = ATT_BLK + qi - kj
    valid = (delta >= 0) & (delta <= ATT_BLK)
    maps = [np.where(valid, _t5_bucket(np.maximum(delta, 0) * r), -1) for r in DILATIONS]
    return np.stack(maps).astype(np.int32)


def _bias_tables(rel_bias, buckets):
    def body(rb_ref, bk_ref, o_ref):
        for g in range(NG):
            bk = bk_ref[g]
            for h in range(KVH):
                acc = jnp.full(bk.shape, NEG, f32)
                for b in range(REL_BUCKETS):
                    acc = jnp.where(bk == b, rb_ref[b, g * KVH + h], acc)
                o_ref[h, g] = acc

    return pl.pallas_call(
        body, name="bias_tables", out_shape=SDS((KVH, NG, ATT_BLK, 2 * ATT_BLK), f32),
        in_specs=[pl.BlockSpec(memory_space=pltpu.SMEM), pl.BlockSpec(memory_space=pltpu.VMEM)],
        out_specs=pl.BlockSpec(memory_space=pltpu.VMEM), compiler_params=_params())(rel_bias, buckets)


def _bias_grad(dbias, buckets):
    def body(db_ref, bk_ref, o_ref):
        rr = lax.broadcasted_iota(jnp.int32, (REL_BUCKETS, NG * KVH), 0)
        cc = lax.broadcasted_iota(jnp.int32, (REL_BUCKETS, NG * KVH), 1)
        out = jnp.zeros((REL_BUCKETS, NG * KVH), f32)
        for g in range(NG):
            bk = bk_ref[g]
            for h in range(KVH):
                d = db_ref[h, g]
                for b in range(REL_BUCKETS):
                    m = jnp.where(bk == b, d, 0.0)
                    s = jnp.sum(jnp.sum(m, axis=1, keepdims=True), axis=0, keepdims=True)
                    out = jnp.where((rr == b) & (cc == g * KVH + h), s, out)
        o_ref[...] = out

    return pl.pallas_call(body, name="bias_grad", out_shape=SDS((REL_BUCKETS, NG * KVH), f32),
                          compiler_params=_params())(dbias, buckets)


def _to_sub(dst_ref, src_ref, r, dtype, offset=0):
    M = SEQ // r
    for c in range(r):
        if r == 1:
            v = src_ref[...]
        else:
            v = src_ref[pl.ds(c, M, stride=r), :]
        dst_ref[pl.ds(offset + c * M, M), :] = v.astype(dtype)


def _from_sub(dst_ref, src_ref, r, accumulate=False, offset=0):
    M = SEQ // r
    for c in range(r):
        v = src_ref[pl.ds(offset + c * M, M), :]
        idx = slice(None) if r == 1 else pl.ds(c, M, stride=r)
        if accumulate:
            dst_ref[idx, :] = dst_ref[idx, :] + v
        else:
            dst_ref[idx, :] = v


_COL = lambda k: slice(k * HD, (k + 1) * HD)
SCALE = HD ** -0.5


def _qkv_spec(k, bh):
    def index(*ids):
        b, h = bh(*ids)
        return (b, C_ATT // HD + 5 * h + k)

    return pl.BlockSpec((SEQ, HD), index)


def _key_window(bias_ref, g, nb):
    if nb == 1:
        bias_own = bias_ref[g, :, ATT_BLK:2 * ATT_BLK]
        return lambda j: (pl.ds(pl.multiple_of((j + 1) * ATT_BLK, ATT_BLK), ATT_BLK), bias_own)
    bias_g = bias_ref[g]
    col = lax.broadcasted_iota(jnp.int32, bias_g.shape, 1)
    bias_first = jnp.where(col >= ATT_BLK, bias_g, NEG)
    return lambda j: (pl.ds(pl.multiple_of(j * ATT_BLK, ATT_BLK), 2 * ATT_BLK),
                      jnp.where(j % nb != 0, bias_g, bias_first))


def _att_fwd(proj, bias, comm=None):
    T = proj.shape[0]
    B = T // SEQ

    def body(q0_ref, q1_ref, q2_ref, k_ref, v_ref, bias_ref, o_ref, lse_ref, *rest):
        saved, (qp, kp, vp, kt, op, lp, og, lg) = rest[:3 * (NG - 1)], rest[3 * (NG - 1):]
        q_refs = (q0_ref, q1_ref, q2_ref)
        kp[0:ATT_BLK, :] = jnp.zeros((ATT_BLK, HD), bf16)
        vp[0:ATT_BLK, :] = jnp.zeros((ATT_BLK, HD), bf16)
        for g, r in enumerate(DILATIONS):
            nb = NBLK_SEQ // r
            _to_sub(qp, q_refs[g], r, bf16)
            _to_sub(kp, k_ref, r, bf16, offset=ATT_BLK)
            _to_sub(vp, v_ref, r, bf16, offset=ATT_BLK)
            if r > 1:
                sq, sk, sv = saved[3 * (g - 1):3 * g]
                sq[...], sk[...], sv[...] = qp[...], kp[ATT_BLK:, :], vp[ATT_BLK:, :]
            kt[...] = kp[...].T
            keys = _key_window(bias_ref, g, nb)

            def step(j, carry):
                cur = pl.ds(pl.multiple_of(j * ATT_BLK, ATT_BLK), ATT_BLK)
                win, bias_j = keys(j)
                s = jnp.dot(qp[cur, :], kt[:, win], preferred_element_type=f32) * SCALE + bias_j
                m = jnp.max(s, axis=-1, keepdims=True)
                p = jnp.exp(s - m)
                den = jnp.sum(p, axis=-1, keepdims=True)
                o = jnp.dot(p.astype(bf16), vp[win, :], preferred_element_type=f32)
                op[cur, :] = o / den
                lp[cur, :] = jnp.broadcast_to(m + jnp.log(den), (ATT_BLK, HD))
                return carry

            lax.fori_loop(0, NBLK_SEQ, step, 0, unroll=ATT_UNROLL)
            _from_sub(og.at[g], op, r)
            _from_sub(lg.at[g], lp, r)
        l0, l1, l2 = lg[0], lg[1], lg[2]
        mx = jnp.maximum(jnp.maximum(l0, l1), l2)
        e0, e1, e2 = jnp.exp(l0 - mx), jnp.exp(l1 - mx), jnp.exp(l2 - mx)
        den = e0 + e1 + e2
        o_ref[...] = (e0 * og[0] + e1 * og[1] + e2 * og[2]) / den
        lse_ref[...] = mx + jnp.log(den)

    return _call(
        body, (proj, proj, proj, proj, proj, bias), name="att_fwd",
        out_shape=(SDS((T, KVH * HD), f32), SDS((KVH, T, HD), f32)) + (SDS((T, KVH * HD), bf16),) * (3 * (NG - 1)),
        grid=(B, KVH),
        in_specs=[_qkv_spec(k, lambda b, h: (b, h)) for k in range(5)]
                 + [pl.BlockSpec((None, NG, ATT_BLK, 2 * ATT_BLK), lambda b, h: (h, 0, 0, 0))],
        out_specs=(pl.BlockSpec((SEQ, HD), lambda b, h: (b, h)),
                   pl.BlockSpec((None, SEQ, HD), lambda b, h: (h, b, 0)))
                  + tuple(pl.BlockSpec((SEQ, HD), lambda b, h: (b, h)) for _ in range(3 * (NG - 1))),
        scratch_shapes=[pltpu.VMEM((SEQ, HD), bf16)] + [pltpu.VMEM((SEQ + ATT_BLK, HD), bf16)] * 2
                       + [pltpu.VMEM((HD, SEQ + ATT_BLK), bf16)]
                       + [pltpu.VMEM((SEQ, HD), f32)] * 2 + [pltpu.VMEM((NG, SEQ, HD), f32)] * 2,
        semantics=("parallel", "parallel"), comm=comm)


def _att_bwd(proj, saved, bias, o, lse, do, comm=None):
    T = proj.shape[0]
    B = T // SEQ
    n_saved = 3 * (NG - 1)

    def body(q0_ref, k_ref, v_ref, *rest):
        saved_refs, rest = rest[:n_saved], rest[n_saved:]
        (bias_ref, o_ref, lse_ref, do_ref, dx_ref, db_ref,
         qp, kp, vp, dop, qt, kt, vt, dot, lp, dqp, dkt, dvt, dln, nat, dkn, dvn) = rest
        first = pl.program_id(1) == 0

        @pl.when(first)
        def _():
            db_ref[...] = jnp.zeros_like(db_ref)

        lane = lax.broadcasted_iota(jnp.int32, (SEQ, HD), 1)
        dln[...] = jnp.where(lane < STAT_LANE, lse_ref[...],
                             jnp.sum(do_ref[...] * o_ref[...], axis=-1, keepdims=True))
        dkn[...] = jnp.zeros_like(dkn)
        dvn[...] = jnp.zeros_like(dvn)
        kp[0:ATT_BLK, :] = jnp.zeros((ATT_BLK, HD), bf16)
        vp[0:ATT_BLK, :] = jnp.zeros((ATT_BLK, HD), bf16)
        for g, r in enumerate(DILATIONS):
            nb = NBLK_SEQ // r
            if r == 1:
                qp[...] = q0_ref[...].astype(bf16)
                kp[ATT_BLK:, :] = k_ref[...].astype(bf16)
                vp[ATT_BLK:, :] = v_ref[...].astype(bf16)
            else:
                sq, sk, sv = saved_refs[3 * (g - 1):3 * g]
                qp[...], kp[ATT_BLK:, :], vp[ATT_BLK:, :] = sq[...], sk[...], sv[...]
            _to_sub(dop, do_ref, r, bf16)
            _to_sub(lp, dln, r, f32)
            qt[...], kt[...], vt[...], dot[...] = qp[...].T, kp[...].T, vp[...].T, dop[...].T
            dkt[...] = jnp.zeros_like(dkt)
            dvt[...] = jnp.zeros_like(dvt)
            keys = _key_window(bias_ref, g, nb)
            db_cols = slice(ATT_BLK, 2 * ATT_BLK) if nb == 1 else slice(None)

            def step(j, carry):
                cur = pl.ds(pl.multiple_of(j * ATT_BLK, ATT_BLK), ATT_BLK)
                win, bias_j = keys(j)
                s = jnp.dot(qp[cur, :], kt[:, win], preferred_element_type=f32) * SCALE + bias_j
                p = jnp.exp(s - lp[cur, 0:1])
                dp = jnp.dot(dop[cur, :], vt[:, win], preferred_element_type=f32)
                ds = p * (dp - lp[cur, STAT_LANE:STAT_LANE + 1])
                db_ref[g, :, db_cols] += ds
                dsb, pb = ds.astype(bf16), p.astype(bf16)
                dqp[cur, :] = jnp.dot(dsb, kp[win, :], preferred_element_type=f32) * SCALE
                dkt[:, win] += jnp.dot(qt[:, cur], dsb, preferred_element_type=f32) * SCALE
                dvt[:, win] += jnp.dot(dot[:, cur], pb, preferred_element_type=f32)
                return carry

            lax.fori_loop(0, NBLK_SEQ, step, 0, unroll=ATT_UNROLL)
            _from_sub(nat, dqp, r)
            dx_ref[:, _COL(g)] = nat[...].astype(bf16)
            dqp[...] = dkt[:, ATT_BLK:].T
            _from_sub(dkn, dqp, r, accumulate=True)
            dqp[...] = dvt[:, ATT_BLK:].T
            _from_sub(dvn, dqp, r, accumulate=True)
        dx_ref[:, _COL(3)] = dkn[...].astype(bf16)
        dx_ref[:, _COL(4)] = dvn[...].astype(bf16)

    blk = lambda: pl.BlockSpec((SEQ, HD), lambda h, b: (b, h))
    bias_spec = lambda: pl.BlockSpec((None, NG, ATT_BLK, 2 * ATT_BLK), lambda h, b: (h, 0, 0, 0))
    pad = lambda dtype: pltpu.VMEM((SEQ + ATT_BLK, HD), dtype)
    pad_t = lambda dtype: pltpu.VMEM((HD, SEQ + ATT_BLK), dtype)
    seq_t = pltpu.VMEM((HD, SEQ), bf16)
    return _call(
        body, (proj, proj, proj, *saved, bias, o, lse, do), name="att_bwd",
        out_shape=(SDS((T, KVH * ATT_COLS), bf16), SDS((KVH, NG, ATT_BLK, 2 * ATT_BLK), f32)), grid=(KVH, B),
        in_specs=[_qkv_spec(k, lambda h, b: (b, h)) for k in (0, 3, 4)] + [blk() for _ in range(n_saved)]
                 + [bias_spec(), blk(), pl.BlockSpec((None, SEQ, HD), lambda h, b: (h, b, 0)), blk()],
        out_specs=(pl.BlockSpec((SEQ, ATT_COLS), lambda h, b: (b, h)), bias_spec()),
        scratch_shapes=[pltpu.VMEM((SEQ, HD), bf16), pad(bf16), pad(bf16), pltpu.VMEM((SEQ, HD), bf16),
                        seq_t, pad_t(bf16), pad_t(bf16), seq_t]
                       + [pltpu.VMEM((SEQ, HD), f32)] * 2 + [pad_t(f32)] * 2 + [pltpu.VMEM((SEQ, HD), f32)] * 4,
        semantics=("parallel", "arbitrary"), comm=comm)


MERGE_ROWS, MERGE_COLS = 1024, 512


def _merge_fwd(gates, pr, pa):
    def body(gr_ref, ga_ref, pr_ref, pa_ref, o_ref):
        o_ref[...] = (_sigmoid(gr_ref[...].astype(f32)) * pr_ref[...].astype(f32)
                      + _sigmoid(ga_ref[...].astype(f32)) * pa_ref[...].astype(f32)).astype(bf16)

    T = gates.shape[0]
    cols = lambda off: pl.BlockSpec((MERGE_ROWS, MERGE_COLS), lambda i, j: (i, off + j))
    return pl.pallas_call(body, name="merge_fwd", out_shape=SDS((T, D), bf16),
                          grid=(T // MERGE_ROWS, D // MERGE_COLS),
                          in_specs=[cols(0), cols(D // MERGE_COLS), cols(0), cols(0)], out_specs=cols(0),
                          compiler_params=_params(("parallel", "parallel")))(gates, gates, pr, pa)


def _merge_bwd(gates, pr, pa, dm):
    nj = D // MERGE_COLS

    def body(g_ref, pr_ref, pa_ref, dm_ref, dp_ref, dg_ref):
        dm_ = dm_ref[...].astype(f32)
        s = _sigmoid(g_ref[...].astype(f32))
        p = jnp.where(pl.program_id(1) < nj, pr_ref[...], pa_ref[...]).astype(f32)
        dp_ref[...] = (dm_ * s).astype(bf16)
        dg_ref[...] = (dm_ * p * s * (1.0 - s)).astype(bf16)

    T = gates.shape[0]
    blk = (MERGE_ROWS, MERGE_COLS)
    wrap = pl.BlockSpec(blk, lambda i, j: (i, j % nj))
    pr_spec = pl.BlockSpec(blk, lambda i, j: (i, jnp.minimum(j, nj - 1)))
    pa_spec = pl.BlockSpec(blk, lambda i, j: (i, jnp.maximum(j - nj, 0)))
    out = pl.BlockSpec(blk, lambda i, j: (i, j))
    return pl.pallas_call(
        body, name="merge_bwd", out_shape=(SDS((T, 2 * D), bf16), SDS((T, 2 * D), bf16)),
        grid=(T // MERGE_ROWS, 2 * nj),
        in_specs=[out, pr_spec, pa_spec, wrap], out_specs=(out, out),
        compiler_params=_params(("parallel", "parallel")))(gates, pr, pa, dm)


FFN_COLS = 256
GELU_C = math.sqrt(2.0 / math.pi)
GELU_A = 0.044715


def _gelu_parts(x):
    q = x * x
    t = jnp.tanh(x * (GELU_C + (GELU_C * GELU_A) * q))
    h = 0.5 + 0.5 * t
    return x * h, h * (1.0 + x * (1.0 - t) * (GELU_C + (3.0 * GELU_C * GELU_A) * q))


def _ffn_act_fwd(gpre, up, cw, cb):
    def body(g_ref, u_ref, cw_ref, cb_ref, o_ref):
        row = lax.broadcasted_iota(jnp.int32, (SEQ, FFN_COLS), 0)
        gate = _conv_fwd(g_ref[...].astype(f32), cw_ref, cb_ref[...], row)
        o_ref[...] = (_gelu_parts(gate)[0] * u_ref[...].astype(f32)).astype(bf16)

    T = gpre.shape[0]
    blk = lambda: pl.BlockSpec((SEQ, FFN_COLS), lambda b, j: (b, j))
    return pl.pallas_call(
        body, name="ffn_act_fwd", out_shape=SDS((T, FFN_W), bf16), grid=(T // SEQ, FFN_W // FFN_COLS),
        in_specs=[blk(), blk(), pl.BlockSpec((FFN_CONV, FFN_COLS), lambda b, j: (0, j)),
                  pl.BlockSpec((1, FFN_COLS), lambda b, j: (0, j))],
        out_specs=blk(), compiler_params=_params(("parallel", "parallel")))(gpre, up, cw, cb)


def _ffn_act_bwd(gpre, up, cw, cb, dact):
    def body(g_ref, u_ref, cw_ref, cb_ref, da_ref, dg_ref, du_ref, dcw_ref, dcb_ref):
        row = lax.broadcasted_iota(jnp.int32, (SEQ, FFN_COLS), 0)
        gp = g_ref[...].astype(f32)
        gate = _conv_fwd(gp, cw_ref, cb_ref[...], row)
        gel, dgel = _gelu_parts(gate)
        da = da_ref[...].astype(f32)
        du_ref[...] = (da * gel).astype(bf16)
        dgate = da * u_ref[...].astype(f32) * dgel
        dx, dws, db = _conv_bwd(gp, cw_ref, dgate, row)
        dg_ref[...] = dx.astype(bf16)
        first = pl.program_id(1) == 0

        def acc(ref, val):
            @pl.when(first)
            def _():
                ref[...] = val

            @pl.when(jnp.logical_not(first))
            def _():
                ref[...] += val

        for k in range(FFN_CONV):
            acc(dcw_ref.at[k:k + 1, :], dws[k])
        acc(dcb_ref, db)

    T = gpre.shape[0]
    blk = lambda: pl.BlockSpec((SEQ, FFN_COLS), lambda j, b: (b, j))
    cws = lambda: pl.BlockSpec((FFN_CONV, FFN_COLS), lambda j, b: (0, j))
    cbs = lambda: pl.BlockSpec((1, FFN_COLS), lambda j, b: (0, j))
    return pl.pallas_call(
        body, name="ffn_act_bwd",
        out_shape=(SDS((T, FFN_W), bf16), SDS((T, FFN_W), bf16), SDS((FFN_CONV, FFN_W), f32), SDS((1, FFN_W), f32)),
        grid=(FFN_W // FFN_COLS, T // SEQ),
        in_specs=[blk(), blk(), cws(), cbs(), blk()], out_specs=(blk(), blk(), cws(), cbs()),
        compiler_params=_params(("parallel", "arbitrary")))(gpre, up, cw, cb, dact)


def _coords():
    return lax.axis_index("x"), lax.axis_index("y"), lax.axis_index("c")


def _dev_index(dev):
    return 4 * dev[0] + 2 * dev[1] + dev[2]


def _dma_sems(n):
    return [pltpu.SemaphoreType.DMA((n,)), pltpu.SemaphoreType.DMA((n,))]


def _gather_two_level(arrays):
    n = len(arrays)

    def plan(ins, outs, sems):
        send_sems, recv_sems, local_sems = sems
        x, y, c = _coords()
        me, sibling = (x, y, c), (x, y, 1 - c)
        chips = [(1 - x, y), (x, 1 - y), (1 - x, 1 - y)]

        def copy(a, k, block, to, own=False):
            dst = outs[a].at[_dev_index(block)]
            return pltpu.make_async_remote_copy(
                src_ref=ins[a] if own else dst, dst_ref=dst, send_sem=send_sems.at[7 * a + k],
                recv_sem=recv_sems.at[7 * a + k], device_id=to, device_id_type=MESH)

        mine = [pltpu.make_async_copy(ins[a], outs[a].at[_dev_index(me)], local_sems.at[a]) for a in range(n)]
        first = [copy(a, 0, me, sibling, own=True) for a in range(n)]
        first += [copy(a, 1 + j, me, (*chip, c), own=True) for a in range(n) for j, chip in enumerate(chips)]
        passed = [[copy(a, 4 + j, (*chip, c), sibling) for a in range(n)] for j, chip in enumerate(chips)]
        arrive_ici = [[copy(a, 1 + j, (*chip, c), me) for a in range(n)] for j, chip in enumerate(chips)]
        arrive_d2d = [copy(a, 0, sibling, me) for a in range(n)]
        arrive_d2d += [copy(a, 4 + j, (*chip, 1 - c), me) for a in range(n) for j, chip in enumerate(chips)]
        return mine, first, passed, arrive_ici, arrive_d2d

    def start(ins, outs, sems):
        mine, first, _, _, _ = plan(ins, outs, sems)
        for cp in mine + first:
            cp.start()

    def finish(ins, outs, sems):
        mine, first, passed, arrive_ici, arrive_d2d = plan(ins, outs, sems)
        for j in range(3):
            for cp in arrive_ici[j]:
                cp.wait_recv()
            for cp in passed[j]:
                cp.start()
        for cp in arrive_d2d:
            cp.wait_recv()
        for cp in first + [cp for group in passed for cp in group]:
            cp.wait_send()
        for cp in mine:
            cp.wait()

    return _Comm(arrays, [SDS((N_DEV,) + a.shape, a.dtype) for a in arrays],
                 _dma_sems(7 * n) + [pltpu.SemaphoreType.DMA((n,))], start, finish)


def _gather_direct(arrays):
    n = len(arrays)

    def plan(ins, outs, sems):
        send_sems, recv_sems, local_sems = sems
        x, y, c = _coords()
        me = (x, y, c)
        mine = [pltpu.make_async_copy(ins[a], outs[a].at[_dev_index(me)], local_sems.at[a]) for a in range(n)]
        sends, arrivals = [], []
        for a in range(n):
            for k in range(1, N_DEV):
                peer = (1 - x if k & 4 else x, 1 - y if k & 2 else y, 1 - c if k & 1 else c)
                s = 7 * a + k - 1
                for slot, out in ((me, sends), (peer, arrivals)):
                    out.append(pltpu.make_async_remote_copy(
                        src_ref=ins[a], dst_ref=outs[a].at[_dev_index(slot)], send_sem=send_sems.at[s],
                        recv_sem=recv_sems.at[s], device_id=peer, device_id_type=MESH))
        return mine, sends, arrivals

    def start(ins, outs, sems):
        mine, sends, _ = plan(ins, outs, sems)
        for cp in mine + sends:
            cp.start()

    def finish(ins, outs, sems):
        mine, sends, arrivals = plan(ins, outs, sems)
        for cp in arrivals:
            cp.wait_recv()
        for cp in sends:
            cp.wait_send()
        for cp in mine:
            cp.wait()

    return _Comm(arrays, [SDS((N_DEV,) + a.shape, a.dtype) for a in arrays],
                 _dma_sems(7 * n) + [pltpu.SemaphoreType.DMA((n,))], start, finish)


def _scatter_direct(arrays):
    n = len(arrays)

    def plan(ins, outs, sems):
        send_sems, recv_sems = sems
        x, y, c = _coords()
        cps = []
        for a in range(n):
            for k in range(1, N_DEV):
                peer = (1 - x if k & 4 else x, 1 - y if k & 2 else y, 1 - c if k & 1 else c)
                s = 7 * a + k - 1
                cps.append(pltpu.make_async_remote_copy(
                    src_ref=ins[a].at[_dev_index(peer)], dst_ref=outs[a].at[k - 1], send_sem=send_sems.at[s],
                    recv_sem=recv_sems.at[s], device_id=peer, device_id_type=MESH))
        return cps

    def start(ins, outs, sems):
        for cp in plan(ins, outs, sems):
            cp.start()

    def finish(ins, outs, sems):
        for cp in plan(ins, outs, sems):
            cp.wait()

    return _Comm(arrays, [SDS((N_DEV - 1,) + a.shape[1:], a.dtype) for a in arrays], _dma_sems(7 * n),
                 start, finish)


def _run(comm, name):
    def body(*refs):
        k_in, k_out = len(comm.inputs), len(comm.out_shapes)
        ins, outs, sems = refs[:k_in], refs[k_in:k_in + k_out], refs[k_in + k_out:]
        comm.start(ins, outs, sems)
        comm.finish(ins, outs, sems)

    return pl.pallas_call(body, name=name, out_shape=comm.out_shapes, in_specs=[ANY] * len(comm.inputs),
                          out_specs=(ANY,) * len(comm.out_shapes), scratch_shapes=comm.sem_shapes)(*comm.inputs)


TILE_ELEMS = 192 * 1024


def _row_tile(R, C):
    if R * C <= TILE_ELEMS:
        return R
    return max(t for t in range(SUBLANES, R, SUBLANES) if R % t == 0 and t * C <= TILE_ELEMS)


def _adamw_math(w, g, m, v):
    m = ADAM_B1 * m + (1.0 - ADAM_B1) * g
    v = ADAM_B2 * v + (1.0 - ADAM_B2) * (g * g)
    m_hat = m / (1.0 - ADAM_B1 ** ADAM_STEP)
    v_hat = v / (1.0 - ADAM_B2 ** ADAM_STEP)
    delta = -ADAM_LR * (m_hat / (jnp.sqrt(v_hat) + ADAM_EPS) + ADAM_WD * w)
    return delta, m, v


def _adamw_sharded(own, recv, d_idx, w, m, v, name):
    R, C = w.shape
    t = _row_tile(R, C)

    def body(k_ref, p_ref, r_ref, w_ref, m_ref, v_ref, g_ref, d_ref, nm_ref, nv_ref):
        g = p_ref[...].astype(f32)
        for j in range(N_DEV - 1):
            g = g + r_ref[j].astype(f32)
        d, nm, nv = _adamw_math(w_ref[...], g, m_ref[...], v_ref[...])
        g_ref[...], d_ref[...], nm_ref[...], nv_ref[...] = g, d, nm, nv

    tile = lambda: pl.BlockSpec((t, C), lambda i, k: (i, 0))
    return pl.pallas_call(
        body, name="adamw_" + name, out_shape=(SDS((R, C), f32),) * 4,
        grid_spec=pltpu.PrefetchScalarGridSpec(
            num_scalar_prefetch=1, grid=(R // t,),
            in_specs=[pl.BlockSpec((None, t, C), lambda i, k: (k[0], i, 0)),
                      pl.BlockSpec((N_DEV - 1, t, C), lambda i, k: (0, i, 0)), tile(), tile(), tile()],
            out_specs=(tile(), tile(), tile(), tile())),
        compiler_params=_params(("parallel",)))(d_idx, own, recv, w, m, v)


def _adamw_replicated(parts, ws, ms, vs):
    n = len(ws)

    def body(*refs):
        p, w, m, v = (refs[i * n:(i + 1) * n] for i in range(4))
        outs = refs[4 * n:]
        for a in range(n):
            g = p[a][0].astype(f32)
            for j in range(1, N_DEV):
                g = g + p[a][j].astype(f32)
            d, nm, nv = _adamw_math(w[a][...], g, m[a][...], v[a][...])
            for i, val in enumerate((g, d, nm, nv)):
                outs[i * n + a][...] = val

    shapes = tuple(SDS(w.shape, f32) for w in ws)
    res = pl.pallas_call(body, name="adamw_replicated", out_shape=shapes * 4,
                         compiler_params=_params())(*parts, *ws, *ms, *vs)
    return [res[i * n:(i + 1) * n] for i in range(4)]


def _cols_to_full(g):
    n, r, c = g.shape
    return g.transpose(1, 0, 2).reshape(r, n * c)


def _full_to_cols(a):
    r, c = a.shape
    return a.reshape(r, N_DEV, c // N_DEV).transpose(1, 0, 2)


def _rows_blocked(a):
    r, c = a.shape
    return a.reshape(N_DEV, r // N_DEV, c)


def _w_in_to_internal(w):
    K = w.shape[0]
    q = w[:, 1280:2816].reshape(K, NG, KVH, 1, HD).transpose(0, 2, 1, 3, 4).reshape(K, KVH, NG, HD)
    k = w[:, 2816:3328].reshape(K, KVH, 1, HD)
    v = w[:, 3328:3840].reshape(K, KVH, 1, HD)
    att = jnp.concatenate([q, k, v], axis=2).reshape(K, KVH * ATT_COLS)
    return jnp.concatenate([w[:, :1280], att, w[:, 3840:]], axis=1)


def _w_in_from_internal(w):
    K = w.shape[0]
    att = w[:, C_ATT:C_GATE].reshape(K, KVH, 5, HD)
    q = att[:, :, 0:3].transpose(0, 2, 1, 3).reshape(K, NG * KVH * HD)
    k = att[:, :, 3].reshape(K, KVH * HD)
    v = att[:, :, 4].reshape(K, KVH * HD)
    return jnp.concatenate([w[:, :C_ATT], q, k, v, w[:, C_GATE:]], axis=1)


_IN_NAMES = ('x', 'rel_bias', 'norm_mix_pre', 'norm_mix_post', 'w_in', 'conv_rnn_w', 'conv_rnn_b', 'w_rg_a', 'b_rg_a',
             'w_rg_x', 'b_rg_x', 'lru_lambda', 'w_branch_rnn', 'w_branch_att', 'w_out', 'norm_ffn_pre',
             'norm_ffn_post', 'w_ffn_gate', 'w_ffn_up', 'conv_ffn_w', 'conv_ffn_b', 'w_ffn_down')
_WEIGHTS = _IN_NAMES[1:]
_SHARDED = {"w_in": "col", "conv_rnn_w": "col", "w_branch_rnn": "row", "w_branch_att": "col", "w_out": "row",
            "w_ffn_gate": "col", "w_ffn_up": "col", "conv_ffn_w": "col", "w_ffn_down": "row"}
_REPLICATED = tuple(n for n in _WEIGHTS if n not in _SHARDED)


def _flat2(a):
    return a.reshape(-1, a.shape[-1])


def _train_step(inp):
    x_idx, y_idx, c_idx = _coords()
    W = {n: inp[n] for n in _WEIGHTS}
    x = inp["x"].reshape(-1, D)
    target = inp["loss_target"].reshape(-1, D)
    shard = {n: inp[n][0] for n in _SHARDED}

    hn, (g_in, g_cr, g_cf) = _norm_in(x, W["norm_mix_pre"], comm=_gather_two_level(
        [shard["w_in"].astype(bf16), shard["conv_rnn_w"], shard["conv_ffn_w"]]))
    w_in = _w_in_to_internal(_cols_to_full(g_in))
    cw_rnn, cw_ffn = _cols_to_full(g_cr), _cols_to_full(g_cf)
    behind_rnn = ("w_branch_rnn", "w_branch_att", "w_out", "w_ffn_down")
    behind_att = ("w_ffn_gate", "w_ffn_up")

    wa, wx = W["w_rg_a"][0], W["w_rg_x"][0]
    buckets = jnp.asarray(_bucket_maps())

    proj = _mm(hn, (w_in, 0, C_GATE), "nn", f32, "mm_proj", 512, C_GATE // 2, 1024, cols_outer=True)
    gates = _mm(hn, w_in[:, C_GATE:], "nn", bf16, "mm_gates", 1024, 1024, 1024, cols_outer=True)
    rnn_saved, got = _rnn_fwd(proj, cw_rnn, W["conv_rnn_b"], wa, W["b_rg_a"], wx, W["b_rg_x"], W["lru_lambda"],
                              comm=_gather_direct([shard[n].astype(bf16) for n in behind_rnn]))
    h_rnn = rnn_saved[0]
    gathered = dict(zip(behind_rnn, got))
    bias = _bias_tables(W["rel_bias"], buckets)
    (o_att, lse, *att_saved), got = _att_fwd(
        proj, bias, comm=_gather_direct([shard[n].astype(bf16) for n in behind_att]))
    gathered.update(zip(behind_att, got))
    w_brnn = gathered["w_branch_rnn"].reshape(RNN_W, D)
    w_batt = _cols_to_full(gathered["w_branch_att"])
    w_out = gathered["w_out"].reshape(D, D)
    w_gate, w_up = _cols_to_full(gathered["w_ffn_gate"]), _cols_to_full(gathered["w_ffn_up"])
    w_down = gathered["w_ffn_down"].reshape(FFN_W, D)
    pr = _mm(h_rnn, w_brnn, "nn", bf16, "mm_pr", 1024, 1024, 1280)
    pa = _mm(o_att, w_batt, "nn", bf16, "mm_pa", 1024, 1024, 512)
    merged = _merge_fwd(gates, pr, pa)
    (mix, h1, hn2), _ = _mm_rows([(merged, w_out, 1024)], "nn", "mm_mix", 1024, _mid_fwd_rows, [x],
                                 [W["norm_mix_post"], W["norm_ffn_pre"]], [f32, f32, bf16], [])
    gpre = _mm(hn2, w_gate, "nn", bf16, "mm_gate", 1024, 1024, 1024, cols_outer=True)
    up = _mm(hn2, w_up, "nn", bf16, "mm_up", 1024, 1024, 1024, cols_outer=True)
    act = _ffn_act_fwd(gpre, up, cw_ffn, W["conv_ffn_b"])
    (dy, dff), (loss_part, dg_fpost) = _mm_rows([(act, w_down, 1024)], "nn", "mm_down", 1024, _final_rows,
                                                [h1, target], [W["norm_ffn_post"]], [f32, bf16], [1, D])

    grads = {}
    dact = _mm(dff, w_down, "nt", bf16, "mm_dact", 1024, 1024, 1024, cols_outer=True)
    grads["w_ffn_down"] = _rows_blocked(_mm(act, dff, "tn", bf16, "mm_dw_down", 1024, 1024, 2048))
    dgpre, dup, dcw_ffn, dcb_ffn = _ffn_act_bwd(gpre, up, cw_ffn, W["conv_ffn_b"], dact)
    grads["conv_ffn_w"] = _full_to_cols(dcw_ffn.astype(bf16))
    grads["w_ffn_gate"] = _full_to_cols(_mm(hn2, dgpre, "tn", bf16, "mm_dw_gate", 1024, 1024, 2048))
    grads["w_ffn_up"] = _full_to_cols(_mm(hn2, dup, "tn", bf16, "mm_dw_up", 1024, 1024, 2048))
    dhn2 = _mm_nt_sum([(dgpre, w_gate, 1024), (dup, w_up, 1024)], "mm_dhn2", 1024, 1024)
    dh1, dmix, dg_fpre, dg_post = _mid_bwd(dy, dhn2, h1, W["norm_ffn_pre"], mix, W["norm_mix_post"])
    dmerged = _mm(dmix, w_out, "nt", bf16, "mm_dmerged", 1024, 1024, 1024)
    grads["w_out"] = _rows_blocked(_mm(merged, dmix, "tn", bf16, "mm_dw_out", 1024, 1024, 2048))
    dprpa, dgates = _merge_bwd(gates, pr, pa, dmerged)
    dpr, dpa = (dprpa, 0, D), (dprpa, D, D)
    dh_rnn = _mm(dpr, w_brnn, "nt", bf16, "mm_dh_rnn", 1024, 1280, 1024)
    grads["w_branch_rnn"] = _rows_blocked(_mm(h_rnn, dpr, "tn", bf16, "mm_dw_brnn", 1280, 1024, 1024))
    do_att = _mm(dpa, w_batt, "nt", f32, "mm_do_att", 1024, 512, 1024)
    grads["w_branch_att"] = _full_to_cols(_mm(o_att, dpa, "tn", bf16, "mm_dw_batt", 512, 1024, 2048))

    received = {}
    behind_att_bwd = ("w_ffn_down", "w_ffn_gate", "conv_ffn_w", "w_out")
    behind_rnn_bwd = ("w_ffn_up", "w_branch_rnn", "w_branch_att")
    (dqkv, dbias), got = _att_bwd(proj, att_saved, bias, o_att, lse, do_att,
                                  comm=_scatter_direct([grads[n] for n in behind_att_bwd]))
    received.update(zip(behind_att_bwd, got))
    drel = _bias_grad(dbias, buckets)
    (dxr, dcw_rnn, dcb_rnn, dwa, dba, dwx, dbx, dlam), got = _rnn_bwd(
        proj, rnn_saved, dh_rnn, cw_rnn, wa, wx, W["lru_lambda"],
        comm=_scatter_direct([grads[n] for n in behind_rnn_bwd]))
    received.update(zip(behind_rnn_bwd, got))
    gsmall = {"rel_bias": drel, "norm_mix_post": dg_post, "conv_rnn_b": dcb_rnn, "w_rg_a": dwa.astype(bf16),
              "b_rg_a": dba, "w_rg_x": dwx.astype(bf16), "b_rg_x": dbx, "lru_lambda": dlam,
              "norm_ffn_pre": dg_fpre, "norm_ffn_post": dg_fpost, "conv_ffn_b": dcb_ffn}
    dw_in_a, parts = _mm(hn, dqkv, "tn", bf16, "mm_dw_in_a", 1024, 1280, 1024,
                         comm=_gather_direct([_flat2(gsmall[n]) for n in gsmall]))
    parts = dict(zip(gsmall, parts))
    dw_in = jnp.concatenate([_mm(hn, dxr, "tn", bf16, "mm_dw_in_r", 1024, 1280, 1024), dw_in_a,
                             _mm(hn, dgates, "tn", bf16, "mm_dw_in_g", 1024, 1024, 2048)], axis=1)
    grads["w_in"] = _full_to_cols(_w_in_from_internal(dw_in))
    grads["conv_rnn_w"] = _full_to_cols(dcw_rnn.astype(bf16))
    behind_dhn = ("w_in", "conv_rnn_w")
    dhn, got = _mm_nt_sum([(dxr, w_in[:, :C_ATT], 1280), (dqkv, w_in[:, C_ATT:C_GATE], 1280),
                           (dgates, w_in[:, C_GATE:], 1024)], "mm_dhn", 1024, 1024,
                          comm=_scatter_direct([grads[n] for n in behind_dhn]))
    received.update(zip(behind_dhn, got))
    dx, dg_pre = _in_bwd(dh1, dhn, x, W["norm_mix_pre"])
    parts["norm_mix_pre"], = _run(_gather_two_level([dg_pre]), "ag_norm_mix_pre")
    parts = [parts[n] for n in _REPLICATED]

    out = {}
    d_arr = jnp.reshape(4 * x_idx + 2 * y_idx + c_idx, (1,)).astype(jnp.int32)
    for n in _SHARDED:
        res = _adamw_sharded(grads[n], received[n], d_arr, shard[n], inp["m_" + n][0], inp["v_" + n][0], n)
        out[n] = [r[None] for r in res]
    small = _adamw_replicated(parts, *[[_flat2(inp[p + n]) for n in _REPLICATED] for p in ("", "m_", "v_")])
    for a, n in enumerate(_REPLICATED):
        out[n] = [small[i][a].reshape(inp[n].shape) for i in range(4)]

    loss = lax.psum(loss_part[0, 0], ("x", "y", "c"))
    outs = [loss, dx.reshape(inp["x"].shape)]
    for i in range(4):
        outs.extend(out[n][i] for n in _WEIGHTS)
    return tuple(outs)


def kernel(x, rel_bias, norm_mix_pre, norm_mix_post, w_in, conv_rnn_w, conv_rnn_b, w_rg_a, b_rg_a, w_rg_x, b_rg_x, lru_lambda, w_branch_rnn, w_branch_att, w_out, norm_ffn_pre, norm_ffn_post, w_ffn_gate, w_ffn_up, conv_ffn_w, conv_ffn_b, w_ffn_down, loss_target, m_rel_bias, m_norm_mix_pre, m_norm_mix_post, m_w_in, m_conv_rnn_w, m_conv_rnn_b, m_w_rg_a, m_b_rg_a, m_w_rg_x, m_b_rg_x, m_lru_lambda, m_w_branch_rnn, m_w_branch_att, m_w_out, m_norm_ffn_pre, m_norm_ffn_post, m_w_ffn_gate, m_w_ffn_up, m_conv_ffn_w, m_conv_ffn_b, m_w_ffn_down, v_rel_bias, v_norm_mix_pre, v_norm_mix_post, v_w_in, v_conv_rnn_w, v_conv_rnn_b, v_w_rg_a, v_b_rg_a, v_w_rg_x, v_b_rg_x, v_lru_lambda, v_w_branch_rnn, v_w_branch_att, v_w_out, v_norm_ffn_pre, v_norm_ffn_post, v_w_ffn_gate, v_w_ffn_up, v_conv_ffn_w, v_conv_ffn_b, v_w_ffn_down):
    vals = locals()
    names = list(_IN_NAMES) + ["loss_target"] + ["m_" + n for n in _WEIGHTS] + ["v_" + n for n in _WEIGHTS]
    return _train_step({n: vals[n] for n in names})
```

```python
import functools
import math

import numpy as np
import jax
import jax.numpy as jnp
from jax import lax
from jax.experimental import pallas as pl
from jax.experimental.pallas import tpu as pltpu

f32, bf16 = jnp.float32, jnp.bfloat16
SDS = jax.ShapeDtypeStruct
MESH = pl.DeviceIdType.MESH
ANY = pl.BlockSpec(memory_space=pl.ANY)

D = 1024
SEQ = 2048
RNN_W = 1280
RNN_BLOCKS = 10
LANES = 128
SUBLANES = 8
RNN_CONV = 4
LRU_C = 8.0
HD = 128
KVH = 4
DILATIONS = (1, 4, 16)
NG = 3
ATT_BLK = 128
NBLK_SEQ = SEQ // ATT_BLK
STAT_LANE = 64
ATT_UNROLL = 8
REL_BUCKETS = 32
REL_MAX_DIST = 2048
FFN_W = 3072
FFN_CONV = 3
EPS = 1e-6
IN_W = 5888
ATT_COLS = 5 * HD
C_ATT = RNN_W
C_GATE = RNN_W + KVH * ATT_COLS
NEG = -1e30

ADAM_LR, ADAM_B1, ADAM_B2, ADAM_EPS, ADAM_WD, ADAM_STEP = 0.001, 0.9, 0.999, 1e-08, 0.01, 10

VMEM_LIMIT_BYTES = 56 * 1024 * 1024
N_DEV = 8


def _params(sem=None):
    return pltpu.CompilerParams(dimension_semantics=sem, vmem_limit_bytes=VMEM_LIMIT_BYTES)


def _sigmoid(x):
    return 1.0 / (1.0 + jnp.exp(-x))


class _Comm:
    def __init__(self, inputs, out_shapes, sem_shapes, start, finish):
        self.inputs, self.out_shapes, self.sem_shapes = tuple(inputs), tuple(out_shapes), list(sem_shapes)
        self.start, self.finish = start, finish


def _call(body, args, *, name, grid, in_specs, out_specs, out_shape, scratch_shapes=(), semantics, comm=None):
    if comm is None:
        return pl.pallas_call(body, name=name, grid=grid, in_specs=list(in_specs), out_specs=tuple(out_specs),
                              out_shape=tuple(out_shape), scratch_shapes=list(scratch_shapes),
                              compiler_params=_params(semantics))(*args), ()
    n_in, n_out, n_scr = len(in_specs), len(out_shape), len(scratch_shapes)
    c_in, c_out = len(comm.inputs), len(comm.out_shapes)

    def fused(*refs):
        ins, refs = refs[:n_in], refs[n_in:]
        cin, refs = refs[:c_in], refs[c_in:]
        outs, refs = refs[:n_out], refs[n_out:]
        cout, refs = refs[:c_out], refs[c_out:]
        scr, csem = refs[:n_scr], refs[n_scr:]
        first = functools.reduce(jnp.logical_and, [pl.program_id(d) == 0 for d in range(len(grid))])
        last = functools.reduce(jnp.logical_and, [pl.program_id(d) == grid[d] - 1 for d in range(len(grid))])

        @pl.when(first)
        def _():
            comm.start(cin, cout, csem)

        body(*ins, *outs, *scr)

        @pl.when(last)
        def _():
            comm.finish(cin, cout, csem)

    res = pl.pallas_call(
        fused, name=name, grid=grid, in_specs=list(in_specs) + [ANY] * c_in,
        out_specs=tuple(out_specs) + (ANY,) * c_out, out_shape=tuple(out_shape) + comm.out_shapes,
        scratch_shapes=list(scratch_shapes) + comm.sem_shapes,
        compiler_params=_params(("arbitrary",) * len(grid)))(*args, *comm.inputs)
    return res[:n_out], res[n_out:]


_DIMS = {"nn": (((1,), (0,)), ((), ())), "nt": (((1,), (1,)), ((), ())), "tn": (((0,), (0,)), ((), ()))}


def _mm(a, b, mode, out_dtype, name, tm, tn, tk, cols_outer=False, comm=None):
    (a, a_c0, a_w), (b, b_c0, b_w) = [x if isinstance(x, tuple) else (x, 0, x.shape[1]) for x in (a, b)]
    if mode == "nn":
        (M, K), (K2, N) = (a.shape[0], a_w), (b.shape[0], b_w)
    elif mode == "nt":
        (M, K), (N, K2) = (a.shape[0], a_w), (b.shape[0], b_w)
    else:
        (K, M), (K2, N) = (a.shape[0], a_w), (b.shape[0], b_w)
    assert K == K2 and M % tm == 0 and N % tn == 0 and K % tk == 0, (name, a.shape, b.shape)
    a_tile, b_tile = (tm if mode == "tn" else tk), (tk if mode == "nt" else tn)
    assert a_c0 % a_tile == 0 and b_c0 % b_tile == 0, name
    a_off, b_off = a_c0 // a_tile, b_c0 // b_tile
    nk = K // tk

    def body(a_ref, b_ref, o_ref, *scratch):
        part = lax.dot_general(a_ref[...].astype(bf16), b_ref[...].astype(bf16), _DIMS[mode],
                               preferred_element_type=f32)
        if nk == 1:
            o_ref[...] = part.astype(o_ref.dtype)
        else:
            acc_ref, = scratch
            k = pl.program_id(2)

            @pl.when(k == 0)
            def _():
                acc_ref[...] = part

            @pl.when(k > 0)
            def _():
                acc_ref[...] += part

            @pl.when(k == nk - 1)
            def _():
                o_ref[...] = acc_ref[...].astype(o_ref.dtype)

    def ij(f):
        return (lambda j, i, k: f(i, j, k)) if cols_outer else f

    if mode == "tn":
        a_spec = pl.BlockSpec((tk, tm), ij(lambda i, j, k: (k, i + a_off)))
    else:
        a_spec = pl.BlockSpec((tm, tk), ij(lambda i, j, k: (i, k + a_off)))
    if mode == "nt":
        b_spec = pl.BlockSpec((tn, tk), ij(lambda i, j, k: (j, k + b_off)))
    else:
        b_spec = pl.BlockSpec((tk, tn), ij(lambda i, j, k: (k, j + b_off)))
    o_spec = pl.BlockSpec((tm, tn), ij(lambda i, j, k: (i, j)))
    grid = (N // tn, M // tm, nk) if cols_outer else (M // tm, N // tn, nk)
    (out,), extra = _call(
        body, (a, b), name=name, out_shape=(SDS((M, N), out_dtype),), grid=grid, in_specs=[a_spec, b_spec],
        out_specs=(o_spec,), scratch_shapes=[pltpu.VMEM((tm, tn), f32)] if nk > 1 else [],
        semantics=("parallel", "parallel", "arbitrary"), comm=comm)
    return out if comm is None else (out, extra)


def _mm_nt_sum(pairs, name, tm, tn, comm=None):
    M, N = pairs[0][0].shape[0], pairs[0][1].shape[0]
    nks = [a.shape[1] // tk for a, _, tk in pairs]
    starts = [sum(nks[:p]) for p in range(len(pairs))]
    nk = sum(nks)

    def body(*refs):
        o_ref, acc_ref = refs[-2], refs[-1]
        k = pl.program_id(2)
        for p in range(len(pairs)):
            def product(p=p):
                return lax.dot_general(refs[2 * p][...], refs[2 * p + 1][...], _DIMS["nt"], preferred_element_type=f32)

            if p == 0:
                @pl.when(k == 0)
                def _():
                    acc_ref[...] = product()

            @pl.when((k >= max(starts[p], 1)) & (k < starts[p] + nks[p]))
            def _():
                acc_ref[...] += product()

        @pl.when(k == nk - 1)
        def _():
            o_ref[...] = acc_ref[...].astype(bf16)

    in_specs, args = [], []
    for (a, b, tk), k0, n in zip(pairs, starts, nks):
        assert a.shape[1] == b.shape[1] and a.shape[1] % tk == 0 and a.dtype == b.dtype == bf16, name
        chunk = lambda k, k0=k0, n=n: jnp.clip(k - k0, 0, n - 1)
        in_specs += [pl.BlockSpec((tm, tk), lambda i, j, k, c=chunk: (i, c(k))),
                     pl.BlockSpec((tn, tk), lambda i, j, k, c=chunk: (j, c(k)))]
        args += [a, b]
    o_spec = pl.BlockSpec((tm, tn), lambda i, j, k: (i, j))
    (out,), extra = _call(
        body, args, name=name, out_shape=(SDS((M, N), bf16),), grid=(M // tm, N // tn, nk), in_specs=in_specs,
        out_specs=(o_spec,), scratch_shapes=[pltpu.VMEM((tm, tn), f32)],
        semantics=("parallel", "parallel", "arbitrary"), comm=comm)
    return out if comm is None else (out, extra)


def _mm_rows(pairs, mode, name, tm, epilogue, rows_in, vecs_in, rows_out, vecs_out, comm=None):
    M = pairs[0][0].shape[0]
    N = pairs[0][1].shape[1 if mode == "nn" else 0]
    nks = [a.shape[1] // tk for a, _, tk in pairs]
    starts = [sum(nks[:p]) for p in range(len(pairs))]
    nk = sum(nks)
    n_rows_in, n_vecs_in, n_rows_out = len(rows_in), len(vecs_in), len(rows_out)

    def body(*refs):
        pair_refs, refs = refs[:2 * len(pairs)], refs[2 * len(pairs):]
        rin, refs = refs[:n_rows_in], refs[n_rows_in:]
        vin, refs = refs[:n_vecs_in], refs[n_vecs_in:]
        rout, refs = refs[:n_rows_out], refs[n_rows_out:]
        vout, acc_ref = refs[:-1], refs[-1]
        i, k = pl.program_id(0), pl.program_id(1)
        for p in range(len(pairs)):
            def product(p=p):
                return lax.dot_general(pair_refs[2 * p][...], pair_refs[2 * p + 1][...], _DIMS[mode],
                                       preferred_element_type=f32)

            if p == 0:
                @pl.when(k == 0)
                def _():
                    acc_ref[...] = product()

            @pl.when((k >= max(starts[p], 1)) & (k < starts[p] + nks[p]))
            def _():
                acc_ref[...] += product()

        @pl.when(k == nk - 1)
        def _():
            res = epilogue(acc_ref[...], *[r[...] for r in rin], *[v[...] for v in vin])
            for ref, val in zip(rout, res[:n_rows_out]):
                ref[...] = val.astype(ref.dtype)
            for ref, val in zip(vout, res[n_rows_out:]):
                @pl.when(i == 0)
                def _(ref=ref, val=val):
                    ref[...] = val

                @pl.when(i > 0)
                def _(ref=ref, val=val):
                    ref[...] += val

    in_specs, args = [], []
    for (a, b, tk), k0, n in zip(pairs, starts, nks):
        assert a.shape[1] % tk == 0 and a.dtype == b.dtype == bf16, name
        chunk = lambda k, k0=k0, n=n: jnp.clip(k - k0, 0, n - 1)
        in_specs.append(pl.BlockSpec((tm, tk), lambda i, k, c=chunk: (i, c(k))))
        if mode == "nn":
            in_specs.append(pl.BlockSpec((tk, N), lambda i, k, c=chunk: (c(k), 0)))
        else:
            in_specs.append(pl.BlockSpec((N, tk), lambda i, k, c=chunk: (0, c(k))))
        args += [a, b]
    row = lambda: pl.BlockSpec((tm, N), lambda i, k: (i, 0))
    vec = lambda w: pl.BlockSpec((1, w), lambda i, k: (0, 0))
    in_specs += [row() for _ in rows_in] + [vec(v.shape[1]) for v in vecs_in]
    outs, extra = _call(
        body, (*args, *rows_in, *vecs_in), name=name,
        out_shape=tuple(SDS((M, N), dt) for dt in rows_out) + tuple(SDS((1, w), f32) for w in vecs_out),
        grid=(M // tm, nk), in_specs=in_specs,
        out_specs=tuple(row() for _ in rows_out) + tuple(vec(w) for w in vecs_out),
        scratch_shapes=[pltpu.VMEM((tm, N), f32)], semantics=("arbitrary", "arbitrary"), comm=comm)
    res = (outs[:n_rows_out], outs[n_rows_out:])
    return res if comm is None else (res, extra)


ROW_TILE = 512


def _rms_fwd(x, g):
    r = lax.rsqrt(jnp.mean(x * x, axis=-1, keepdims=True) + EPS)
    return x * r * g


def _rms_bwd(x, g, dy):
    r = lax.rsqrt(jnp.mean(x * x, axis=-1, keepdims=True) + EPS)
    xh = x * r
    dxh = dy * g
    dx = r * (dxh - xh * jnp.mean(dxh * xh, axis=-1, keepdims=True))
    return dx, jnp.sum(dy * xh, axis=0, keepdims=True)


def _acc_out(ref, val):
    @pl.when(pl.program_id(0) == 0)
    def _():
        ref[...] = val

    @pl.when(pl.program_id(0) > 0)
    def _():
        ref[...] += val


def _row_spec(width=D):
    return pl.BlockSpec((ROW_TILE, width), lambda i: (i, 0))


def _vec_spec(width=D):
    return pl.BlockSpec((1, width), lambda i: (0, 0))


def _norm_in(x, g, comm=None):
    def body(x_ref, g_ref, o_ref):
        o_ref[...] = _rms_fwd(x_ref[...], g_ref[...]).astype(bf16)

    T = x.shape[0]
    (hn,), extra = _call(body, (x, g), name="norm_in", out_shape=(SDS((T, D), bf16),), grid=(T // ROW_TILE,),
                         in_specs=[_row_spec(), _vec_spec()], out_specs=(_row_spec(),), semantics=("parallel",),
                         comm=comm)
    return hn, extra


def _mid_bwd(dy, dhn2, h1, g_fpre, mix, g_post):
    def body(dy_ref, dhn2_ref, h1_ref, gf_ref, mix_ref, gp_ref, dh1_ref, dmix_ref, dgf_ref, dgp_ref):
        d1, dgf = _rms_bwd(h1_ref[...], gf_ref[...], dhn2_ref[...].astype(f32))
        dh1 = dy_ref[...] + d1
        dh1_ref[...] = dh1
        dmix, dgp = _rms_bwd(mix_ref[...], gp_ref[...], dh1)
        dmix_ref[...] = dmix.astype(bf16)
        _acc_out(dgf_ref, dgf)
        _acc_out(dgp_ref, dgp)

    T = dy.shape[0]
    return pl.pallas_call(
        body, name="mid_bwd", out_shape=(SDS((T, D), f32), SDS((T, D), bf16), SDS((1, D), f32), SDS((1, D), f32)),
        grid=(T // ROW_TILE,),
        in_specs=[_row_spec(), _row_spec(), _row_spec(), _vec_spec(), _row_spec(), _vec_spec()],
        out_specs=(_row_spec(), _row_spec(), _vec_spec(), _vec_spec()),
        compiler_params=_params(("arbitrary",)))(dy, dhn2, h1, g_fpre, mix, g_post)


def _in_bwd(dh1, dhn, x, g_pre):
    def body(dh1_ref, dhn_ref, x_ref, g_ref, dx_ref, dg_ref):
        d, dg = _rms_bwd(x_ref[...], g_ref[...], dhn_ref[...].astype(f32))
        dx_ref[...] = dh1_ref[...] + d
        _acc_out(dg_ref, dg)

    T = x.shape[0]
    return pl.pallas_call(
        body, name="in_bwd", out_shape=(SDS((T, D), f32), SDS((1, D), f32)), grid=(T // ROW_TILE,),
        in_specs=[_row_spec(), _row_spec(), _row_spec(), _vec_spec()], out_specs=(_row_spec(), _vec_spec()),
        compiler_params=_params(("arbitrary",)))(dh1, dhn, x, g_pre)


def _mid_fwd_rows(mix, x, g_post, g_fpre):
    h1 = x + _rms_fwd(mix, g_post)
    return mix, h1, _rms_fwd(h1, g_fpre)


def _final_rows(ff, h1, target, g_fpost):
    e = h1 + _rms_fwd(ff, g_fpost) - target
    part = jnp.sum(jnp.sum(e * e, axis=1, keepdims=True), axis=0, keepdims=True) * (0.5 / D)
    dy = e * (1.0 / D)
    dff, dg = _rms_bwd(ff, g_fpost, dy)
    return dy, dff, part, dg


def _shift_dn(x, d, row, fill=0.0):
    if d == 0:
        return x
    y = pltpu.roll(x, d, 0)
    head = jnp.where(row[:SUBLANES] >= d, y[:SUBLANES], fill)
    return jnp.concatenate([head, y[SUBLANES:]], axis=0)


def _shift_up(x, d, row, fill=0.0):
    if d == 0:
        return x
    n = x.shape[0]
    y = pltpu.roll(x, n - d, 0)
    tail = jnp.where(row[:SUBLANES] < SUBLANES - d, y[n - SUBLANES:], fill)
    return jnp.concatenate([y[:n - SUBLANES], tail], axis=0)


def _conv_fwd(x, w_ref, b, row):
    K = w_ref.shape[0]
    y = b
    for k in range(K):
        y = y + w_ref[k:k + 1, :] * _shift_dn(x, K - 1 - k, row)
    return y


def _conv_bwd(x, w_ref, dy, row):
    K = w_ref.shape[0]
    dx = jnp.zeros_like(dy)
    dws = []
    for k in range(K):
        dx = dx + w_ref[k:k + 1, :] * _shift_up(dy, K - 1 - k, row)
        dws.append(jnp.sum(dy * _shift_dn(x, K - 1 - k, row), axis=0, keepdims=True))
    return dx, dws, jnp.sum(dy, axis=0, keepdims=True)


def _scan_fwd(a, u, row):
    n = a.shape[0]
    d = 1
    while d < n:
        last = 2 * d >= n
        if d < SUBLANES:
            u = u + a * _shift_dn(u, d, row)
            if not last:
                a = a * _shift_dn(a, d, row, fill=1.0)
        else:
            u = jnp.concatenate([u[:d], u[d:] + a[d:] * u[:n - d]], axis=0)
            if not last:
                a = jnp.concatenate([a[:d], a[d:] * a[:n - d]], axis=0)
        d *= 2
    return u


def _scan_bwd(b, u, row):
    n = b.shape[0]
    d = 1
    while d < n:
        last = 2 * d >= n
        if d < SUBLANES:
            u = u + b * _shift_up(u, d, row)
            if not last:
                b = b * _shift_up(b, d, row, fill=1.0)
        else:
            u = jnp.concatenate([u[:n - d] + b[:n - d] * u[d:], u[n - d:]], axis=0)
            if not last:
                b = jnp.concatenate([b[:n - d] * b[d:], b[n - d:]], axis=0)
        d *= 2
    return u


def _neg_expm1(z):
    series = -z * (1.0 + z * (0.5 + z * (1.0 / 6.0 + z * (1.0 / 24.0 + z * (1.0 / 120.0)))))
    return jnp.where(z > -0.1, series, 1.0 - jnp.exp(z))


def _softplus_neg(lam):
    z = -lam
    return jnp.maximum(z, 0.0) + jnp.log(1.0 + jnp.exp(-jnp.abs(z)))


def _rnn_specs(B):
    blk = lambda: pl.BlockSpec((SEQ, LANES), lambda b, n: (b, n))
    return dict(
        act=blk,
        convw=pl.BlockSpec((RNN_CONV, LANES), lambda b, n: (0, n)),
        vec=lambda: pl.BlockSpec((1, LANES), lambda b, n: (0, n)),
        gate=lambda: pl.BlockSpec((None, LANES, LANES), lambda b, n: (n, 0, 0)),
    )


def _rnn_fwd(proj, cw, cb, wa, ba, wx, bx, lam, comm=None):
    T = proj.shape[0]
    B = T // SEQ

    def body(x_ref, cw_ref, cb_ref, wa_ref, ba_ref, wx_ref, bx_ref, lam_ref, h_ref, xc_ref, r_ref, i_ref, a_ref, s_ref):
        row = lax.broadcasted_iota(jnp.int32, (SEQ, LANES), 0)
        xc = _conv_fwd(x_ref[...], cw_ref, cb_ref[...], row)
        xcb = xc.astype(bf16)
        r = _sigmoid(jnp.dot(xcb, wa_ref[...].astype(bf16), preferred_element_type=f32) + ba_ref[...])
        i = _sigmoid(jnp.dot(xcb, wx_ref[...].astype(bf16), preferred_element_type=f32) + bx_ref[...])
        log_a = (-LRU_C * _softplus_neg(lam_ref[...])) * r
        a = jnp.exp(log_a)
        s = jnp.sqrt(_neg_expm1(2.0 * log_a))
        xc_ref[...], r_ref[...], i_ref[...], a_ref[...], s_ref[...] = xc, r, i, a, s
        h_ref[...] = _scan_fwd(a, s * (i * xc), row)

    sp_ = _rnn_specs(B)
    return _call(
        body, (proj, cw, cb, wa, ba, wx, bx, lam), name="rnn_fwd", out_shape=(SDS((T, RNN_W), f32),) * 6,
        grid=(B, RNN_BLOCKS),
        in_specs=[sp_["act"](), sp_["convw"], sp_["vec"](), sp_["gate"](), sp_["vec"](), sp_["gate"](),
                  sp_["vec"](), sp_["vec"]()],
        out_specs=tuple(sp_["act"]() for _ in range(6)), semantics=("parallel", "parallel"), comm=comm)


def _rnn_bwd(proj, saved, dh, cw, wa, wx, lam, comm=None):
    T = proj.shape[0]
    B = T // SEQ

    def body(x_ref, h_ref, xc_ref, r_ref, i_ref, a_ref, s_ref, dh_ref, cw_ref, wa_ref, wx_ref, lam_ref,
             dx_ref, dcw_ref, dcb_ref, dwa_ref, dba_ref, dwx_ref, dbx_ref, dlam_ref):
        row = lax.broadcasted_iota(jnp.int32, (SEQ, LANES), 0)
        xr = x_ref[...]
        wa, wx, lam = wa_ref[...], wx_ref[...], lam_ref[...]
        xc, r, i, a, s = xc_ref[...], r_ref[...], i_ref[...], a_ref[...], s_ref[...]
        xcb = xc.astype(bf16)
        sp = _softplus_neg(lam)
        hprev = _shift_dn(h_ref[...], 1, row)
        g = _scan_bwd(_shift_up(a, 1, row), dh_ref[...].astype(f32), row)
        da = g * hprev
        ds = g * (i * xc)
        di = g * (s * xc)
        dxc = g * (s * i)
        dla = da * a - ds * (a * a) / s
        dr = dla * (-LRU_C * sp)
        dsp = jnp.sum(dla * (-LRU_C * r), axis=0, keepdims=True)
        dlam = -dsp * _sigmoid(-lam)
        dga = dr * r * (1.0 - r)
        dgx = di * i * (1.0 - i)
        dgab, dgxb = dga.astype(bf16), dgx.astype(bf16)
        dwa = lax.dot_general(xcb, dgab, _DIMS["tn"], preferred_element_type=f32)
        dwx = lax.dot_general(xcb, dgxb, _DIMS["tn"], preferred_element_type=f32)
        dxc = dxc + lax.dot_general(dgab, wa.astype(bf16), _DIMS["nt"], preferred_element_type=f32)
        dxc = dxc + lax.dot_general(dgxb, wx.astype(bf16), _DIMS["nt"], preferred_element_type=f32)
        dx, dws, db = _conv_bwd(xr, cw_ref, dxc, row)
        dx_ref[...] = dx.astype(bf16)
        first = pl.program_id(1) == 0

        def acc(ref, val):
            @pl.when(first)
            def _():
                ref[...] = val

            @pl.when(jnp.logical_not(first))
            def _():
                ref[...] += val

        for k in range(RNN_CONV):
            acc(dcw_ref.at[k:k + 1, :], dws[k])
        acc(dcb_ref, db)
        acc(dwa_ref, dwa)
        acc(dba_ref, jnp.sum(dga, axis=0, keepdims=True))
        acc(dwx_ref, dwx)
        acc(dbx_ref, jnp.sum(dgx, axis=0, keepdims=True))
        acc(dlam_ref, dlam)

    blk = lambda: pl.BlockSpec((SEQ, LANES), lambda n, b: (b, n))
    convw = lambda: pl.BlockSpec((RNN_CONV, LANES), lambda n, b: (0, n))
    vec = lambda: pl.BlockSpec((1, LANES), lambda n, b: (0, n))
    gate = lambda: pl.BlockSpec((None, LANES, LANES), lambda n, b: (n, 0, 0))
    vshape = SDS((1, RNN_W), f32)
    gshape = SDS((RNN_BLOCKS, LANES, LANES), f32)
    return _call(
        body, (proj, *saved, dh, cw, wa, wx, lam), name="rnn_bwd",
        out_shape=(SDS((T, RNN_W), bf16), SDS((RNN_CONV, RNN_W), f32), vshape, gshape, vshape, gshape, vshape, vshape),
        grid=(RNN_BLOCKS, B),
        in_specs=[blk() for _ in range(8)] + [convw(), gate(), gate(), vec()],
        out_specs=(blk(), convw(), vec(), gate(), vec(), gate(), vec(), vec()),
        semantics=("parallel", "arbitrary"), comm=comm)


def _t5_bucket(dist):
    max_exact = REL_BUCKETS // 2
    d = np.maximum(dist, 1).astype(np.float32)
    large = max_exact + np.log(d / max_exact) / math.log(REL_MAX_DIST / max_exact) * (REL_BUCKETS - max_exact)
    large = np.minimum(large.astype(np.int32), REL_BUCKETS - 1)
    return np.where(dist < max_exact, dist, large).astype(np.int32)


def _bucket_maps():
    qi = np.arange(ATT_BLK)[:, None]
    kj = np.arange(2 * ATT_BLK)[None, :]
    delta = ATT_BLK + qi - kj
    valid = (delta >= 0) & (delta <= ATT_BLK)
    maps = [np.where(valid, _t5_bucket(np.maximum(delta, 0) * r), -1) for r in DILATIONS]
    return np.stack(maps).astype(np.int32)


def _bias_tables(rel_bias, buckets):
    def body(rb_ref, bk_ref, o_ref):
        for g in range(NG):
            bk = bk_ref[g]
            for h in range(KVH):
                acc = jnp.full(bk.shape, NEG, f32)
                for b in range(REL_BUCKETS):
                    acc = jnp.where(bk == b, rb_ref[b, g * KVH + h], acc)
                o_ref[h, g] = acc

    return pl.pallas_call(
        body, name="bias_tables", out_shape=SDS((KVH, NG, ATT_BLK, 2 * ATT_BLK), f32),
        in_specs=[pl.BlockSpec(memory_space=pltpu.SMEM), pl.BlockSpec(memory_space=pltpu.VMEM)],
        out_specs=pl.BlockSpec(memory_space=pltpu.VMEM), compiler_params=_params())(rel_bias, buckets)


def _bias_grad(dbias, buckets):
    def body(db_ref, bk_ref, o_ref):
        rr = lax.broadcasted_iota(jnp.int32, (REL_BUCKETS, NG * KVH), 0)
        cc = lax.broadcasted_iota(jnp.int32, (REL_BUCKETS, NG * KVH), 1)
        out = jnp.zeros((REL_BUCKETS, NG * KVH), f32)
        for g in range(NG):
            bk = bk_ref[g]
            for h in range(KVH):
                d = db_ref[h, g]
                for b in range(REL_BUCKETS):
                    m = jnp.where(bk == b, d, 0.0)
                    s = jnp.sum(jnp.sum(m, axis=1, keepdims=True), axis=0, keepdims=True)
                    out = jnp.where((rr == b) & (cc == g * KVH + h), s, out)
        o_ref[...] = out

    return pl.pallas_call(body, name="bias_grad", out_shape=SDS((REL_BUCKETS, NG * KVH), f32),
                          compiler_params=_params())(dbias, buckets)


def _to_sub(dst_ref, src_ref, r, dtype, offset=0):
    M = SEQ // r
    for c in range(r):
        if r == 1:
            v = src_ref[...]
        else:
            v = src_ref[pl.ds(c, M, stride=r), :]
        dst_ref[pl.ds(offset + c * M, M), :] = v.astype(dtype)


def _from_sub(dst_ref, src_ref, r, accumulate=False, offset=0):
    M = SEQ // r
    for c in range(r):
        v = src_ref[pl.ds(offset + c * M, M), :]
        idx = slice(None) if r == 1 else pl.ds(c, M, stride=r)
        if accumulate:
            dst_ref[idx, :] = dst_ref[idx, :] + v
        else:
            dst_ref[idx, :] = v


_COL = lambda k: slice(k * HD, (k + 1) * HD)
SCALE = HD ** -0.5


def _qkv_spec(k, bh):
    def index(*ids):
        b, h = bh(*ids)
        return (b, C_ATT // HD + 5 * h + k)

    return pl.BlockSpec((SEQ, HD), index)


def _key_window(bias_ref, g, nb):
    if nb == 1:
        bias_own = bias_ref[g, :, ATT_BLK:2 * ATT_BLK]
        return lambda j: (pl.ds(pl.multiple_of((j + 1) * ATT_BLK, ATT_BLK), ATT_BLK), bias_own)
    bias_g = bias_ref[g]
    col = lax.broadcasted_iota(jnp.int32, bias_g.shape, 1)
    bias_first = jnp.where(col >= ATT_BLK, bias_g, NEG)
    return lambda j: (pl.ds(pl.multiple_of(j * ATT_BLK, ATT_BLK), 2 * ATT_BLK),
                      jnp.where(j % nb != 0, bias_g, bias_first))


def _att_fwd(proj, bias, comm=None):
    T = proj.shape[0]
    B = T // SEQ

    def body(q0_ref, q1_ref, q2_ref, k_ref, v_ref, bias_ref, o_ref, lse_ref, *rest):
        saved, (qp, kp, vp, kt, op, lp, og, lg) = rest[:3 * (NG - 1)], rest[3 * (NG - 1):]
        q_refs = (q0_ref, q1_ref, q2_ref)
        kp[0:ATT_BLK, :] = jnp.zeros((ATT_BLK, HD), bf16)
        vp[0:ATT_BLK, :] = jnp.zeros((ATT_BLK, HD), bf16)
        for g, r in enumerate(DILATIONS):
            nb = NBLK_SEQ // r
            _to_sub(qp, q_refs[g], r, bf16)
            _to_sub(kp, k_ref, r, bf16, offset=ATT_BLK)
            _to_sub(vp, v_ref, r, bf16, offset=ATT_BLK)
            if r > 1:
                sq, sk, sv = saved[3 * (g - 1):3 * g]
                sq[...], sk[...], sv[...] = qp[...], kp[ATT_BLK:, :], vp[ATT_BLK:, :]
            kt[...] = kp[...].T
            keys = _key_window(bias_ref, g, nb)

            def step(j, carry):
                cur = pl.ds(pl.multiple_of(j * ATT_BLK, ATT_BLK), ATT_BLK)
                win, bias_j = keys(j)
                s = jnp.dot(qp[cur, :], kt[:, win], preferred_element_type=f32) * SCALE + bias_j
                m = jnp.max(s, axis=-1, keepdims=True)
                p = jnp.exp(s - m)
                den = jnp.sum(p, axis=-1, keepdims=True)
                o = jnp.dot(p.astype(bf16), vp[win, :], preferred_element_type=f32)
                op[cur, :] = o / den
                lp[cur, :] = jnp.broadcast_to(m + jnp.log(den), (ATT_BLK, HD))
                return carry

            lax.fori_loop(0, NBLK_SEQ, step, 0, unroll=ATT_UNROLL)
            _from_sub(og.at[g], op, r)
            _from_sub(lg.at[g], lp, r)
        l0, l1, l2 = lg[0], lg[1], lg[2]
        mx = jnp.maximum(jnp.maximum(l0, l1), l2)
        e0, e1, e2 = jnp.exp(l0 - mx), jnp.exp(l1 - mx), jnp.exp(l2 - mx)
        den = e0 + e1 + e2
        o_ref[...] = (e0 * og[0] + e1 * og[1] + e2 * og[2]) / den
        lse_ref[...] = mx + jnp.log(den)

    return _call(
        body, (proj, proj, proj, proj, proj, bias), name="att_fwd",
        out_shape=(SDS((T, KVH * HD), f32), SDS((KVH, T, HD), f32)) + (SDS((T, KVH * HD), bf16),) * (3 * (NG - 1)),
        grid=(B, KVH),
        in_specs=[_qkv_spec(k, lambda b, h: (b, h)) for k in range(5)]
                 + [pl.BlockSpec((None, NG, ATT_BLK, 2 * ATT_BLK), lambda b, h: (h, 0, 0, 0))],
        out_specs=(pl.BlockSpec((SEQ, HD), lambda b, h: (b, h)),
                   pl.BlockSpec((None, SEQ, HD), lambda b, h: (h, b, 0)))
                  + tuple(pl.BlockSpec((SEQ, HD), lambda b, h: (b, h)) for _ in range(3 * (NG - 1))),
        scratch_shapes=[pltpu.VMEM((SEQ, HD), bf16)] + [pltpu.VMEM((SEQ + ATT_BLK, HD), bf16)] * 2
                       + [pltpu.VMEM((HD, SEQ + ATT_BLK), bf16)]
                       + [pltpu.VMEM((SEQ, HD), f32)] * 2 + [pltpu.VMEM((NG, SEQ, HD), f32)] * 2,
        semantics=("parallel", "parallel"), comm=comm)


def _att_bwd(proj, saved, bias, o, lse, do, comm=None):
    T = proj.shape[0]
    B = T // SEQ
    n_saved = 3 * (NG - 1)

    def body(q0_ref, k_ref, v_ref, *rest):
        saved_refs, rest = rest[:n_saved], rest[n_saved:]
        (bias_ref, o_ref, lse_ref, do_ref, dx_ref, db_ref,
         qp, kp, vp, dop, qt, kt, vt, dot, lp, dqp, dkt, dvt, dln, nat, dkn, dvn) = rest
        first = pl.program_id(1) == 0

        @pl.when(first)
        def _():
            db_ref[...] = jnp.zeros_like(db_ref)

        lane = lax.broadcasted_iota(jnp.int32, (SEQ, HD), 1)
        dln[...] = jnp.where(lane < STAT_LANE, lse_ref[...],
                             jnp.sum(do_ref[...] * o_ref[...], axis=-1, keepdims=True))
        dkn[...] = jnp.zeros_like(dkn)
        dvn[...] = jnp.zeros_like(dvn)
        kp[0:ATT_BLK, :] = jnp.zeros((ATT_BLK, HD), bf16)
        vp[0:ATT_BLK, :] = jnp.zeros((ATT_BLK, HD), bf16)
        for g, r in enumerate(DILATIONS):
            nb = NBLK_SEQ // r
            if r == 1:
                qp[...] = q0_ref[...].astype(bf16)
                kp[ATT_BLK:, :] = k_ref[...].astype(bf16)
                vp[ATT_BLK:, :] = v_ref[...].astype(bf16)
            else:
                sq, sk, sv = saved_refs[3 * (g - 1):3 * g]
                qp[...], kp[ATT_BLK:, :], vp[ATT_BLK:, :] = sq[...], sk[...], sv[...]
            _to_sub(dop, do_ref, r, bf16)
            _to_sub(lp, dln, r, f32)
            qt[...], kt[...], vt[...], dot[...] = qp[...].T, kp[...].T, vp[...].T, dop[...].T
            dkt[...] = jnp.zeros_like(dkt)
            dvt[...] = jnp.zeros_like(dvt)
            keys = _key_window(bias_ref, g, nb)
            db_cols = slice(ATT_BLK, 2 * ATT_BLK) if nb == 1 else slice(None)

            def step(j, carry):
                cur = pl.ds(pl.multiple_of(j * ATT_BLK, ATT_BLK), ATT_BLK)
                win, bias_j = keys(j)
                s = jnp.dot(qp[cur, :], kt[:, win], preferred_element_type=f32) * SCALE + bias_j
                p = jnp.exp(s - lp[cur, 0:1])
                dp = jnp.dot(dop[cur, :], vt[:, win], preferred_element_type=f32)
                ds = p * (dp - lp[cur, STAT_LANE:STAT_LANE + 1])
                db_ref[g, :, db_cols] += ds
                dsb, pb = ds.astype(bf16), p.astype(bf16)
                dqp[cur, :] = jnp.dot(dsb, kp[win, :], preferred_element_type=f32) * SCALE
                dkt[:, win] += jnp.dot(qt[:, cur], dsb, preferred_element_type=f32) * SCALE
                dvt[:, win] += jnp.dot(dot[:, cur], pb, preferred_element_type=f32)
                return carry

            lax.fori_loop(0, NBLK_SEQ, step, 0, unroll=ATT_UNROLL)
            _from_sub(nat, dqp, r)
            dx_ref[:, _COL(g)] = nat[...].astype(bf16)
            dqp[...] = dkt[:, ATT_BLK:].T
            _from_sub(dkn, dqp, r, accumulate=True)
            dqp[...] = dvt[:, ATT_BLK:].T
            _from_sub(dvn, dqp, r, accumulate=True)
        dx_ref[:, _COL(3)] = dkn[...].astype(bf16)
        dx_ref[:, _COL(4)] = dvn[...].astype(bf16)

    blk = lambda: pl.BlockSpec((SEQ, HD), lambda h, b: (b, h))
    bias_spec = lambda: pl.BlockSpec((None, NG, ATT_BLK, 2 * ATT_BLK), lambda h, b: (h, 0, 0, 0))
    pad = lambda dtype: pltpu.VMEM((SEQ + ATT_BLK, HD), dtype)
    pad_t = lambda dtype: pltpu.VMEM((HD, SEQ + ATT_BLK), dtype)
    seq_t = pltpu.VMEM((HD, SEQ), bf16)
    return _call(
        body, (proj, proj, proj, *saved, bias, o, lse, do), name="att_bwd",
        out_shape=(SDS((T, KVH * ATT_COLS), bf16), SDS((KVH, NG, ATT_BLK, 2 * ATT_BLK), f32)), grid=(KVH, B),
        in_specs=[_qkv_spec(k, lambda h, b: (b, h)) for k in (0, 3, 4)] + [blk() for _ in range(n_saved)]
                 + [bias_spec(), blk(), pl.BlockSpec((None, SEQ, HD), lambda h, b: (h, b, 0)), blk()],
        out_specs=(pl.BlockSpec((SEQ, ATT_COLS), lambda h, b: (b, h)), bias_spec()),
        scratch_shapes=[pltpu.VMEM((SEQ, HD), bf16), pad(bf16), pad(bf16), pltpu.VMEM((SEQ, HD), bf16),
                        seq_t, pad_t(bf16), pad_t(bf16), seq_t]
                       + [pltpu.VMEM((SEQ, HD), f32)] * 2 + [pad_t(f32)] * 2 + [pltpu.VMEM((SEQ, HD), f32)] * 4,
        semantics=("parallel", "arbitrary"), comm=comm)


MERGE_ROWS, MERGE_COLS = 1024, 512


def _merge_fwd(gates, pr, pa):
    def body(gr_ref, ga_ref, pr_ref, pa_ref, o_ref):
        o_ref[...] = (_sigmoid(gr_ref[...].astype(f32)) * pr_ref[...].astype(f32)
                      + _sigmoid(ga_ref[...].astype(f32)) * pa_ref[...].astype(f32)).astype(bf16)

    T = gates.shape[0]
    cols = lambda off: pl.BlockSpec((MERGE_ROWS, MERGE_COLS), lambda i, j: (i, off + j))
    return pl.pallas_call(body, name="merge_fwd", out_shape=SDS((T, D), bf16),
                          grid=(T // MERGE_ROWS, D // MERGE_COLS),
                          in_specs=[cols(0), cols(D // MERGE_COLS), cols(0), cols(0)], out_specs=cols(0),
                          compiler_params=_params(("parallel", "parallel")))(gates, gates, pr, pa)


def _merge_bwd(gates, pr, pa, dm):
    nj = D // MERGE_COLS

    def body(g_ref, pr_ref, pa_ref, dm_ref, dp_ref, dg_ref):
        dm_ = dm_ref[...].astype(f32)
        s = _sigmoid(g_ref[...].astype(f32))
        p = jnp.where(pl.program_id(1) < nj, pr_ref[...], pa_ref[...]).astype(f32)
        dp_ref[...] = (dm_ * s).astype(bf16)
        dg_ref[...] = (dm_ * p * s * (1.0 - s)).astype(bf16)

    T = gates.shape[0]
    blk = (MERGE_ROWS, MERGE_COLS)
    wrap = pl.BlockSpec(blk, lambda i, j: (i, j % nj))
    pr_spec = pl.BlockSpec(blk, lambda i, j: (i, jnp.minimum(j, nj - 1)))
    pa_spec = pl.BlockSpec(blk, lambda i, j: (i, jnp.maximum(j - nj, 0)))
    out = pl.BlockSpec(blk, lambda i, j: (i, j))
    return pl.pallas_call(
        body, name="merge_bwd", out_shape=(SDS((T, 2 * D), bf16), SDS((T, 2 * D), bf16)),
        grid=(T // MERGE_ROWS, 2 * nj),
        in_specs=[out, pr_spec, pa_spec, wrap], out_specs=(out, out),
        compiler_params=_params(("parallel", "parallel")))(gates, pr, pa, dm)


FFN_COLS = 256
GELU_C = math.sqrt(2.0 / math.pi)
GELU_A = 0.044715


def _gelu_parts(x):
    q = x * x
    t = jnp.tanh(x * (GELU_C + (GELU_C * GELU_A) * q))
    h = 0.5 + 0.5 * t
    return x * h, h * (1.0 + x * (1.0 - t) * (GELU_C + (3.0 * GELU_C * GELU_A) * q))


def _ffn_act_fwd(gpre, up, cw, cb):
    def body(g_ref, u_ref, cw_ref, cb_ref, o_ref):
        row = lax.broadcasted_iota(jnp.int32, (SEQ, FFN_COLS), 0)
        gate = _conv_fwd(g_ref[...].astype(f32), cw_ref, cb_ref[...], row)
        o_ref[...] = (_gelu_parts(gate)[0] * u_ref[...].astype(f32)).astype(bf16)

    T = gpre.shape[0]
    blk = lambda: pl.BlockSpec((SEQ, FFN_COLS), lambda b, j: (b, j))
    return pl.pallas_call(
        body, name="ffn_act_fwd", out_shape=SDS((T, FFN_W), bf16), grid=(T // SEQ, FFN_W // FFN_COLS),
        in_specs=[blk(), blk(), pl.BlockSpec((FFN_CONV, FFN_COLS), lambda b, j: (0, j)),
                  pl.BlockSpec((1, FFN_COLS), lambda b, j: (0, j))],
        out_specs=blk(), compiler_params=_params(("parallel", "parallel")))(gpre, up, cw, cb)


def _ffn_act_bwd(gpre, up, cw, cb, dact):
    def body(g_ref, u_ref, cw_ref, cb_ref, da_ref, dg_ref, du_ref, dcw_ref, dcb_ref):
        row = lax.broadcasted_iota(jnp.int32, (SEQ, FFN_COLS), 0)
        gp = g_ref[...].astype(f32)
        gate = _conv_fwd(gp, cw_ref, cb_ref[...], row)
        gel, dgel = _gelu_parts(gate)
        da = da_ref[...].astype(f32)
        du_ref[...] = (da * gel).astype(bf16)
        dgate = da * u_ref[...].astype(f32) * dgel
        dx, dws, db = _conv_bwd(gp, cw_ref, dgate, row)
        dg_ref[...] = dx.astype(bf16)
        first = pl.program_id(1) == 0

        def acc(ref, val):
            @pl.when(first)
            def _():
                ref[...] = val

            @pl.when(jnp.logical_not(first))
            def _():
                ref[...] += val

        for k in range(FFN_CONV):
            acc(dcw_ref.at[k:k + 1, :], dws[k])
        acc(dcb_ref, db)

    T = gpre.shape[0]
    blk = lambda: pl.BlockSpec((SEQ, FFN_COLS), lambda j, b: (b, j))
    cws = lambda: pl.BlockSpec((FFN_CONV, FFN_COLS), lambda j, b: (0, j))
    cbs = lambda: pl.BlockSpec((1, FFN_COLS), lambda j, b: (0, j))
    return pl.pallas_call(
        body, name="ffn_act_bwd",
        out_shape=(SDS((T, FFN_W), bf16), SDS((T, FFN_W), bf16), SDS((FFN_CONV, FFN_W), f32), SDS((1, FFN_W), f32)),
        grid=(FFN_W // FFN_COLS, T // SEQ),
        in_specs=[blk(), blk(), cws(), cbs(), blk()], out_specs=(blk(), blk(), cws(), cbs()),
        compiler_params=_params(("parallel", "arbitrary")))(gpre, up, cw, cb, dact)


def _coords():
    return lax.axis_index("x"), lax.axis_index("y"), lax.axis_index("c")


def _dev_index(dev):
    return 4 * dev[0] + 2 * dev[1] + dev[2]


def _dma_sems(n):
    return [pltpu.SemaphoreType.DMA((n,)), pltpu.SemaphoreType.DMA((n,))]


def _gather_two_level(arrays):
    n = len(arrays)

    def plan(ins, outs, sems):
        send_sems, recv_sems, local_sems = sems
        x, y, c = _coords()
        me, sibling = (x, y, c), (x, y, 1 - c)
        chips = [(1 - x, y), (x, 1 - y), (1 - x, 1 - y)]

        def copy(a, k, block, to, own=False):
            dst = outs[a].at[_dev_index(block)]
            return pltpu.make_async_remote_copy(
                src_ref=ins[a] if own else dst, dst_ref=dst, send_sem=send_sems.at[7 * a + k],
                recv_sem=recv_sems.at[7 * a + k], device_id=to, device_id_type=MESH)

        mine = [pltpu.make_async_copy(ins[a], outs[a].at[_dev_index(me)], local_sems.at[a]) for a in range(n)]
        first = [copy(a, 0, me, sibling, own=True) for a in range(n)]
        first += [copy(a, 1 + j, me, (*chip, c), own=True) for a in range(n) for j, chip in enumerate(chips)]
        passed = [[copy(a, 4 + j, (*chip, c), sibling) for a in range(n)] for j, chip in enumerate(chips)]
        arrive_ici = [[copy(a, 1 + j, (*chip, c), me) for a in range(n)] for j, chip in enumerate(chips)]
        arrive_d2d = [copy(a, 0, sibling, me) for a in range(n)]
        arrive_d2d += [copy(a, 4 + j, (*chip, 1 - c), me) for a in range(n) for j, chip in enumerate(chips)]
        return mine, first, passed, arrive_ici, arrive_d2d

    def start(ins, outs, sems):
        mine, first, _, _, _ = plan(ins, outs, sems)
        for cp in mine + first:
            cp.start()

    def finish(ins, outs, sems):
        mine, first, passed, arrive_ici, arrive_d2d = plan(ins, outs, sems)
        for j in range(3):
            for cp in arrive_ici[j]:
                cp.wait_recv()
            for cp in passed[j]:
                cp.start()
        for cp in arrive_d2d:
            cp.wait_recv()
        for cp in first + [cp for group in passed for cp in group]:
            cp.wait_send()
        for cp in mine:
            cp.wait()

    return _Comm(arrays, [SDS((N_DEV,) + a.shape, a.dtype) for a in arrays],
                 _dma_sems(7 * n) + [pltpu.SemaphoreType.DMA((n,))], start, finish)


def _gather_direct(arrays):
    n = len(arrays)

    def plan(ins, outs, sems):
        send_sems, recv_sems, local_sems = sems
        x, y, c = _coords()
        me = (x, y, c)
        mine = [pltpu.make_async_copy(ins[a], outs[a].at[_dev_index(me)], local_sems.at[a]) for a in range(n)]
        sends, arrivals = [], []
        for a in range(n):
            for k in range(1, N_DEV):
                peer = (1 - x if k & 4 else x, 1 - y if k & 2 else y, 1 - c if k & 1 else c)
                s = 7 * a + k - 1
                for slot, out in ((me, sends), (peer, arrivals)):
                    out.append(pltpu.make_async_remote_copy(
                        src_ref=ins[a], dst_ref=outs[a].at[_dev_index(slot)], send_sem=send_sems.at[s],
                        recv_sem=recv_sems.at[s], device_id=peer, device_id_type=MESH))
        return mine, sends, arrivals

    def start(ins, outs, sems):
        mine, sends, _ = plan(ins, outs, sems)
        for cp in mine + sends:
            cp.start()

    def finish(ins, outs, sems):
        mine, sends, arrivals = plan(ins, outs, sems)
        for cp in arrivals:
            cp.wait_recv()
        for cp in sends:
            cp.wait_send()
        for cp in mine:
            cp.wait()

    return _Comm(arrays, [SDS((N_DEV,) + a.shape, a.dtype) for a in arrays],
                 _dma_sems(7 * n) + [pltpu.SemaphoreType.DMA((n,))], start, finish)


def _scatter_direct(arrays):
    n = len(arrays)

    def plan(ins, outs, sems):
        send_sems, recv_sems = sems
        x, y, c = _coords()
        cps = []
        for a in range(n):
            for k in range(1, N_DEV):
                peer = (1 - x if k & 4 else x, 1 - y if k & 2 else y, 1 - c if k & 1 else c)
                s = 7 * a + k - 1
                cps.append(pltpu.make_async_remote_copy(
                    src_ref=ins[a].at[_dev_index(peer)], dst_ref=outs[a].at[k - 1], send_sem=send_sems.at[s],
                    recv_sem=recv_sems.at[s], device_id=peer, device_id_type=MESH))
        return cps

    def start(ins, outs, sems):
        for cp in plan(ins, outs, sems):
            cp.start()

    def finish(ins, outs, sems):
        for cp in plan(ins, outs, sems):
            cp.wait()

    return _Comm(arrays, [SDS((N_DEV - 1,) + a.shape[1:], a.dtype) for a in arrays], _dma_sems(7 * n),
                 start, finish)


def _run(comm, name):
    def body(*refs):
        k_in, k_out = len(comm.inputs), len(comm.out_shapes)
        ins, outs, sems = refs[:k_in], refs[k_in:k_in + k_out], refs[k_in + k_out:]
        comm.start(ins, outs, sems)
        comm.finish(ins, outs, sems)

    return pl.pallas_call(body, name=name, out_shape=comm.out_shapes, in_specs=[ANY] * len(comm.inputs),
                          out_specs=(ANY,) * len(comm.out_shapes), scratch_shapes=comm.sem_shapes)(*comm.inputs)


TILE_ELEMS = 192 * 1024


def _row_tile(R, C):
    if R * C <= TILE_ELEMS:
        return R
    return max(t for t in range(SUBLANES, R, SUBLANES) if R % t == 0 and t * C <= TILE_ELEMS)


def _adamw_math(w, g, m, v):
    m = ADAM_B1 * m + (1.0 - ADAM_B1) * g
    v = ADAM_B2 * v + (1.0 - ADAM_B2) * (g * g)
    m_hat = m / (1.0 - ADAM_B1 ** ADAM_STEP)
    v_hat = v / (1.0 - ADAM_B2 ** ADAM_STEP)
    delta = -ADAM_LR * (m_hat / (jnp.sqrt(v_hat) + ADAM_EPS) + ADAM_WD * w)
    return delta, m, v


def _adamw_sharded(own, recv, d_idx, w, m, v, name):
    R, C = w.shape
    t = _row_tile(R, C)

    def body(k_ref, p_ref, r_ref, w_ref, m_ref, v_ref, g_ref, d_ref, nm_ref, nv_ref):
        g = p_ref[...].astype(f32)
        for j in range(N_DEV - 1):
            g = g + r_ref[j].astype(f32)
        d, nm, nv = _adamw_math(w_ref[...], g, m_ref[...], v_ref[...])
        g_ref[...], d_ref[...], nm_ref[...], nv_ref[...] = g, d, nm, nv

    tile = lambda: pl.BlockSpec((t, C), lambda i, k: (i, 0))
    return pl.pallas_call(
        body, name="adamw_" + name, out_shape=(SDS((R, C), f32),) * 4,
        grid_spec=pltpu.PrefetchScalarGridSpec(
            num_scalar_prefetch=1, grid=(R // t,),
            in_specs=[pl.BlockSpec((None, t, C), lambda i, k: (k[0], i, 0)),
                      pl.BlockSpec((N_DEV - 1, t, C), lambda i, k: (0, i, 0)), tile(), tile(), tile()],
            out_specs=(tile(), tile(), tile(), tile())),
        compiler_params=_params(("parallel",)))(d_idx, own, recv, w, m, v)


def _adamw_replicated(parts, ws, ms, vs):
    n = len(ws)

    def body(*refs):
        p, w, m, v = (refs[i * n:(i + 1) * n] for i in range(4))
        outs = refs[4 * n:]
        for a in range(n):
            g = p[a][0].astype(f32)
            for j in range(1, N_DEV):
                g = g + p[a][j].astype(f32)
            d, nm, nv = _adamw_math(w[a][...], g, m[a][...], v[a][...])
            for i, val in enumerate((g, d, nm, nv)):
                outs[i * n + a][...] = val

    shapes = tuple(SDS(w.shape, f32) for w in ws)
    res = pl.pallas_call(body, name="adamw_replicated", out_shape=shapes * 4,
                         compiler_params=_params())(*parts, *ws, *ms, *vs)
    return [res[i * n:(i + 1) * n] for i in range(4)]


def _cols_to_full(g):
    n, r, c = g.shape
    return g.transpose(1, 0, 2).reshape(r, n * c)


def _full_to_cols(a):
    r, c = a.shape
    return a.reshape(r, N_DEV, c // N_DEV).transpose(1, 0, 2)


def _rows_blocked(a):
    r, c = a.shape
    return a.reshape(N_DEV, r // N_DEV, c)


def _w_in_to_internal(w):
    K = w.shape[0]
    q = w[:, 1280:2816].reshape(K, NG, KVH, 1, HD).transpose(0, 2, 1, 3, 4).reshape(K, KVH, NG, HD)
    k = w[:, 2816:3328].reshape(K, KVH, 1, HD)
    v = w[:, 3328:3840].reshape(K, KVH, 1, HD)
    att = jnp.concatenate([q, k, v], axis=2).reshape(K, KVH * ATT_COLS)
    return jnp.concatenate([w[:, :1280], att, w[:, 3840:]], axis=1)


def _w_in_from_internal(w):
    K = w.shape[0]
    att = w[:, C_ATT:C_GATE].reshape(K, KVH, 5, HD)
    q = att[:, :, 0:3].transpose(0, 2, 1, 3).reshape(K, NG * KVH * HD)
    k = att[:, :, 3].reshape(K, KVH * HD)
    v = att[:, :, 4].reshape(K, KVH * HD)
    return jnp.concatenate([w[:, :C_ATT], q, k, v, w[:, C_GATE:]], axis=1)


_IN_NAMES = ('x', 'rel_bias', 'norm_mix_pre', 'norm_mix_post', 'w_in', 'conv_rnn_w', 'conv_rnn_b', 'w_rg_a', 'b_rg_a',
             'w_rg_x', 'b_rg_x', 'lru_lambda', 'w_branch_rnn', 'w_branch_att', 'w_out', 'norm_ffn_pre',
             'norm_ffn_post', 'w_ffn_gate', 'w_ffn_up', 'conv_ffn_w', 'conv_ffn_b', 'w_ffn_down')
_WEIGHTS = _IN_NAMES[1:]
_SHARDED = {"w_in": "col", "conv_rnn_w": "col", "w_branch_rnn": "row", "w_branch_att": "col", "w_out": "row",
            "w_ffn_gate": "col", "w_ffn_up": "col", "conv_ffn_w": "col", "w_ffn_down": "row"}
_REPLICATED = tuple(n for n in _WEIGHTS if n not in _SHARDED)


def _flat2(a):
    return a.reshape(-1, a.shape[-1])


def _train_step(inp):
    x_idx, y_idx, c_idx = _coords()
    W = {n: inp[n] for n in _WEIGHTS}
    x = inp["x"].reshape(-1, D)
    target = inp["loss_target"].reshape(-1, D)
    shard = {n: inp[n][0] for n in _SHARDED}

    hn, (g_in, g_cr, g_cf) = _norm_in(x, W["norm_mix_pre"], comm=_gather_two_level(
        [shard["w_in"].astype(bf16), shard["conv_rnn_w"], shard["conv_ffn_w"]]))
    w_in = _w_in_to_internal(_cols_to_full(g_in))
    cw_rnn, cw_ffn = _cols_to_full(g_cr), _cols_to_full(g_cf)
    behind_rnn = ("w_branch_rnn", "w_branch_att", "w_out", "w_ffn_down")
    behind_att = ("w_ffn_gate", "w_ffn_up")

    wa, wx = W["w_rg_a"][0], W["w_rg_x"][0]
    buckets = jnp.asarray(_bucket_maps())

    proj = _mm(hn, (w_in, 0, C_GATE), "nn", f32, "mm_proj", 1024, C_GATE // 2, 1024, cols_outer=True)
    gates = _mm(hn, w_in[:, C_GATE:], "nn", bf16, "mm_gates", 1024, 1024, 1024, cols_outer=True)
    rnn_saved, got = _rnn_fwd(proj, cw_rnn, W["conv_rnn_b"], wa, W["b_rg_a"], wx, W["b_rg_x"], W["lru_lambda"],
                              comm=_gather_direct([shard[n].astype(bf16) for n in behind_rnn]))
    h_rnn = rnn_saved[0]
    gathered = dict(zip(behind_rnn, got))
    bias = _bias_tables(W["rel_bias"], buckets)
    (o_att, lse, *att_saved), got = _att_fwd(
        proj, bias, comm=_gather_direct([shard[n].astype(bf16) for n in behind_att]))
    gathered.update(zip(behind_att, got))
    w_brnn = gathered["w_branch_rnn"].reshape(RNN_W, D)
    w_batt = _cols_to_full(gathered["w_branch_att"])
    w_out = gathered["w_out"].reshape(D, D)
    w_gate, w_up = _cols_to_full(gathered["w_ffn_gate"]), _cols_to_full(gathered["w_ffn_up"])
    w_down = gathered["w_ffn_down"].reshape(FFN_W, D)
    pr = _mm(h_rnn, w_brnn, "nn", bf16, "mm_pr", 1024, 1024, 1280)
    pa = _mm(o_att, w_batt, "nn", bf16, "mm_pa", 1024, 1024, 512)
    merged = _merge_fwd(gates, pr, pa)
    (mix, h1, hn2), _ = _mm_rows([(merged, w_out, 1024)], "nn", "mm_mix", 1024, _mid_fwd_rows, [x],
                                 [W["norm_mix_post"], W["norm_ffn_pre"]], [f32, f32, bf16], [])
    gpre = _mm(hn2, w_gate, "nn", bf16, "mm_gate", 1024, 1024, 1024, cols_outer=True)
    up = _mm(hn2, w_up, "nn", bf16, "mm_up", 1024, 1024, 1024, cols_outer=True)
    act = _ffn_act_fwd(gpre, up, cw_ffn, W["conv_ffn_b"])
    (dy, dff), (loss_part, dg_fpost) = _mm_rows([(act, w_down, 1024)], "nn", "mm_down", 1024, _final_rows,
                                                [h1, target], [W["norm_ffn_post"]], [f32, bf16], [1, D])

    grads = {}
    dact = _mm(dff, w_down, "nt", bf16, "mm_dact", 1024, 1024, 1024, cols_outer=True)
    grads["w_ffn_down"] = _rows_blocked(_mm(act, dff, "tn", bf16, "mm_dw_down", 1024, 1024, 2048))
    dgpre, dup, dcw_ffn, dcb_ffn = _ffn_act_bwd(gpre, up, cw_ffn, W["conv_ffn_b"], dact)
    grads["conv_ffn_w"] = _full_to_cols(dcw_ffn.astype(bf16))
    grads["w_ffn_gate"] = _full_to_cols(_mm(hn2, dgpre, "tn", bf16, "mm_dw_gate", 1024, 1024, 2048))
    grads["w_ffn_up"] = _full_to_cols(_mm(hn2, dup, "tn", bf16, "mm_dw_up", 1024, 1024, 2048))
    dhn2 = _mm_nt_sum([(dgpre, w_gate, 1024), (dup, w_up, 1024)], "mm_dhn2", 1024, 1024)
    dh1, dmix, dg_fpre, dg_post = _mid_bwd(dy, dhn2, h1, W["norm_ffn_pre"], mix, W["norm_mix_post"])
    dmerged = _mm(dmix, w_out, "nt", bf16, "mm_dmerged", 1024, 1024, 1024)
    grads["w_out"] = _rows_blocked(_mm(merged, dmix, "tn", bf16, "mm_dw_out", 1024, 1024, 2048))
    dprpa, dgates = _merge_bwd(gates, pr, pa, dmerged)
    dpr, dpa = (dprpa, 0, D), (dprpa, D, D)
    dh_rnn = _mm(dpr, w_brnn, "nt", bf16, "mm_dh_rnn", 1024, 1280, 1024)
    grads["w_branch_rnn"] = _rows_blocked(_mm(h_rnn, dpr, "tn", bf16, "mm_dw_brnn", 1280, 1024, 1024))
    do_att = _mm(dpa, w_batt, "nt", f32, "mm_do_att", 1024, 512, 1024)
    grads["w_branch_att"] = _full_to_cols(_mm(o_att, dpa, "tn", bf16, "mm_dw_batt", 512, 1024, 2048))

    received = {}
    behind_att_bwd = ("w_ffn_down", "w_ffn_gate", "conv_ffn_w", "w_out")
    behind_rnn_bwd = ("w_ffn_up", "w_branch_rnn", "w_branch_att")
    (dqkv, dbias), got = _att_bwd(proj, att_saved, bias, o_att, lse, do_att,
                                  comm=_scatter_direct([grads[n] for n in behind_att_bwd]))
    received.update(zip(behind_att_bwd, got))
    drel = _bias_grad(dbias, buckets)
    (dxr, dcw_rnn, dcb_rnn, dwa, dba, dwx, dbx, dlam), got = _rnn_bwd(
        proj, rnn_saved, dh_rnn, cw_rnn, wa, wx, W["lru_lambda"],
        comm=_scatter_direct([grads[n] for n in behind_rnn_bwd]))
    received.update(zip(behind_rnn_bwd, got))
    gsmall = {"rel_bias": drel, "norm_mix_post": dg_post, "conv_rnn_b": dcb_rnn, "w_rg_a": dwa.astype(bf16),
              "b_rg_a": dba, "w_rg_x": dwx.astype(bf16), "b_rg_x": dbx, "lru_lambda": dlam,
              "norm_ffn_pre": dg_fpre, "norm_ffn_post": dg_fpost, "conv_ffn_b": dcb_ffn}
    dw_in_a, parts = _mm(hn, dqkv, "tn", bf16, "mm_dw_in_a", 1024, 1280, 1024,
                         comm=_gather_direct([_flat2(gsmall[n]) for n in gsmall]))
    parts = dict(zip(gsmall, parts))
    dw_in = jnp.concatenate([_mm(hn, dxr, "tn", bf16, "mm_dw_in_r", 1024, 1280, 1024), dw_in_a,
                             _mm(hn, dgates, "tn", bf16, "mm_dw_in_g", 1024, 1024, 2048)], axis=1)
    grads["w_in"] = _full_to_cols(_w_in_from_internal(dw_in))
    grads["conv_rnn_w"] = _full_to_cols(dcw_rnn.astype(bf16))
    behind_dhn = ("w_in", "conv_rnn_w")
    dhn, got = _mm_nt_sum([(dxr, w_in[:, :C_ATT], 1280), (dqkv, w_in[:, C_ATT:C_GATE], 1280),
                           (dgates, w_in[:, C_GATE:], 1024)], "mm_dhn", 1024, 1024,
                          comm=_scatter_direct([grads[n] for n in behind_dhn]))
    received.update(zip(behind_dhn, got))
    dx, dg_pre = _in_bwd(dh1, dhn, x, W["norm_mix_pre"])
    parts["norm_mix_pre"], = _run(_gather_two_level([dg_pre]), "ag_norm_mix_pre")
    parts = [parts[n] for n in _REPLICATED]

    out = {}
    d_arr = jnp.reshape(4 * x_idx + 2 * y_idx + c_idx, (1,)).astype(jnp.int32)
    for n in _SHARDED:
        res = _adamw_sharded(grads[n], received[n], d_arr, shard[n], inp["m_" + n][0], inp["v_" + n][0], n)
        out[n] = [r[None] for r in res]
    small = _adamw_replicated(parts, *[[_flat2(inp[p + n]) for n in _REPLICATED] for p in ("", "m_", "v_")])
    for a, n in enumerate(_REPLICATED):
        out[n] = [small[i][a].reshape(inp[n].shape) for i in range(4)]

    loss = lax.psum(loss_part[0, 0], ("x", "y", "c"))
    outs = [loss, dx.reshape(inp["x"].shape)]
    for i in range(4):
        outs.extend(out[n][i] for n in _WEIGHTS)
    return tuple(outs)


def kernel(x, rel_bias, norm_mix_pre, norm_mix_post, w_in, conv_rnn_w, conv_rnn_b, w_rg_a, b_rg_a, w_rg_x, b_rg_x, lru_lambda, w_branch_rnn, w_branch_att, w_out, norm_ffn_pre, norm_ffn_post, w_ffn_gate, w_ffn_up, conv_ffn_w, conv_ffn_b, w_ffn_down, loss_target, m_rel_bias, m_norm_mix_pre, m_norm_mix_post, m_w_in, m_conv_rnn_w, m_conv_rnn_b, m_w_rg_a, m_b_rg_a, m_w_rg_x, m_b_rg_x, m_lru_lambda, m_w_branch_rnn, m_w_branch_att, m_w_out, m_norm_ffn_pre, m_norm_ffn_post, m_w_ffn_gate, m_w_ffn_up, m_conv_ffn_w, m_conv_ffn_b, m_w_ffn_down, v_rel_bias, v_norm_mix_pre, v_norm_mix_post, v_w_in, v_conv_rnn_w, v_conv_rnn_b, v_w_rg_a, v_b_rg_a, v_w_rg_x, v_b_rg_x, v_lru_lambda, v_w_branch_rnn, v_w_branch_att, v_w_out, v_norm_ffn_pre, v_norm_ffn_post, v_w_ffn_gate, v_w_ffn_up, v_conv_ffn_w, v_conv_ffn_b, v_w_ffn_down):
    vals = locals()
    names = list(_IN_NAMES) + ["loss_target"] + ["m_" + n for n in _WEIGHTS] + ["v_" + n for n in _WEIGHTS]
    return _train_step({n: vals[n] for n in names})
```

```python
import functools
import math

import numpy as np
import jax
import jax.numpy as jnp
from jax import lax
from jax.experimental import pallas as pl
from jax.experimental.pallas import tpu as pltpu

f32, bf16 = jnp.float32, jnp.bfloat16
SDS = jax.ShapeDtypeStruct
MESH = pl.DeviceIdType.MESH
ANY = pl.BlockSpec(memory_space=pl.ANY)

D = 1024
SEQ = 2048
RNN_W = 1280
RNN_BLOCKS = 10
LANES = 128
SUBLANES = 8
RNN_CONV = 4
LRU_C = 8.0
HD = 128
KVH = 4
DILATIONS = (1, 4, 16)
NG = 3
ATT_BLK = 128
NBLK_SEQ = SEQ // ATT_BLK
STAT_LANE = 64
REL_BUCKETS = 32
REL_MAX_DIST = 2048
FFN_W = 3072
FFN_CONV = 3
EPS = 1e-6
IN_W = 5888
ATT_COLS = 5 * HD
C_ATT = RNN_W
C_GATE = RNN_W + KVH * ATT_COLS
NEG = -1e30

ADAM_LR, ADAM_B1, ADAM_B2, ADAM_EPS, ADAM_WD, ADAM_STEP = 0.001, 0.9, 0.999, 1e-08, 0.01, 10

VMEM_LIMIT_BYTES = 56 * 1024 * 1024
N_DEV = 8


def _params(sem=None):
    return pltpu.CompilerParams(dimension_semantics=sem, vmem_limit_bytes=VMEM_LIMIT_BYTES)


def _sigmoid(x):
    return 1.0 / (1.0 + jnp.exp(-x))


class _Comm:
    def __init__(self, inputs, out_shapes, sem_shapes, start, finish):
        self.inputs, self.out_shapes, self.sem_shapes = tuple(inputs), tuple(out_shapes), list(sem_shapes)
        self.start, self.finish = start, finish


def _call(body, args, *, name, grid, in_specs, out_specs, out_shape, scratch_shapes=(), semantics, comm=None):
    if comm is None:
        return pl.pallas_call(body, name=name, grid=grid, in_specs=list(in_specs), out_specs=tuple(out_specs),
                              out_shape=tuple(out_shape), scratch_shapes=list(scratch_shapes),
                              compiler_params=_params(semantics))(*args), ()
    n_in, n_out, n_scr = len(in_specs), len(out_shape), len(scratch_shapes)
    c_in, c_out = len(comm.inputs), len(comm.out_shapes)

    def fused(*refs):
        ins, refs = refs[:n_in], refs[n_in:]
        cin, refs = refs[:c_in], refs[c_in:]
        outs, refs = refs[:n_out], refs[n_out:]
        cout, refs = refs[:c_out], refs[c_out:]
        scr, csem = refs[:n_scr], refs[n_scr:]
        first = functools.reduce(jnp.logical_and, [pl.program_id(d) == 0 for d in range(len(grid))])
        last = functools.reduce(jnp.logical_and, [pl.program_id(d) == grid[d] - 1 for d in range(len(grid))])

        @pl.when(first)
        def _():
            comm.start(cin, cout, csem)

        body(*ins, *outs, *scr)

        @pl.when(last)
        def _():
            comm.finish(cin, cout, csem)

    res = pl.pallas_call(
        fused, name=name, grid=grid, in_specs=list(in_specs) + [ANY] * c_in,
        out_specs=tuple(out_specs) + (ANY,) * c_out, out_shape=tuple(out_shape) + comm.out_shapes,
        scratch_shapes=list(scratch_shapes) + comm.sem_shapes,
        compiler_params=_params(("arbitrary",) * len(grid)))(*args, *comm.inputs)
    return res[:n_out], res[n_out:]


_DIMS = {"nn": (((1,), (0,)), ((), ())), "nt": (((1,), (1,)), ((), ())), "tn": (((0,), (0,)), ((), ()))}


def _mm(a, b, mode, out_dtype, name, tm, tn, tk, cols_outer=False, comm=None):
    (a, a_c0, a_w), (b, b_c0, b_w) = [x if isinstance(x, tuple) else (x, 0, x.shape[1]) for x in (a, b)]
    if mode == "nn":
        (M, K), (K2, N) = (a.shape[0], a_w), (b.shape[0], b_w)
    elif mode == "nt":
        (M, K), (N, K2) = (a.shape[0], a_w), (b.shape[0], b_w)
    else:
        (K, M), (K2, N) = (a.shape[0], a_w), (b.shape[0], b_w)
    assert K == K2 and M % tm == 0 and N % tn == 0 and K % tk == 0, (name, a.shape, b.shape)
    a_tile, b_tile = (tm if mode == "tn" else tk), (tk if mode == "nt" else tn)
    assert a_c0 % a_tile == 0 and b_c0 % b_tile == 0, name
    a_off, b_off = a_c0 // a_tile, b_c0 // b_tile
    nk = K // tk

    def body(a_ref, b_ref, o_ref, *scratch):
        part = lax.dot_general(a_ref[...].astype(bf16), b_ref[...].astype(bf16), _DIMS[mode],
                               preferred_element_type=f32)
        if nk == 1:
            o_ref[...] = part.astype(o_ref.dtype)
        else:
            acc_ref, = scratch
            k = pl.program_id(2)

            @pl.when(k == 0)
            def _():
                acc_ref[...] = part

            @pl.when(k > 0)
            def _():
                acc_ref[...] += part

            @pl.when(k == nk - 1)
            def _():
                o_ref[...] = acc_ref[...].astype(o_ref.dtype)

    def ij(f):
        return (lambda j, i, k: f(i, j, k)) if cols_outer else f

    if mode == "tn":
        a_spec = pl.BlockSpec((tk, tm), ij(lambda i, j, k: (k, i + a_off)))
    else:
        a_spec = pl.BlockSpec((tm, tk), ij(lambda i, j, k: (i, k + a_off)))
    if mode == "nt":
        b_spec = pl.BlockSpec((tn, tk), ij(lambda i, j, k: (j, k + b_off)))
    else:
        b_spec = pl.BlockSpec((tk, tn), ij(lambda i, j, k: (k, j + b_off)))
    o_spec = pl.BlockSpec((tm, tn), ij(lambda i, j, k: (i, j)))
    grid = (N // tn, M // tm, nk) if cols_outer else (M // tm, N // tn, nk)
    (out,), extra = _call(
        body, (a, b), name=name, out_shape=(SDS((M, N), out_dtype),), grid=grid, in_specs=[a_spec, b_spec],
        out_specs=(o_spec,), scratch_shapes=[pltpu.VMEM((tm, tn), f32)] if nk > 1 else [],
        semantics=("parallel", "parallel", "arbitrary"), comm=comm)
    return out if comm is None else (out, extra)


def _mm_nt_sum(pairs, name, tm, tn, comm=None):
    M, N = pairs[0][0].shape[0], pairs[0][1].shape[0]
    nks = [a.shape[1] // tk for a, _, tk in pairs]
    starts = [sum(nks[:p]) for p in range(len(pairs))]
    nk = sum(nks)

    def body(*refs):
        o_ref, acc_ref = refs[-2], refs[-1]
        k = pl.program_id(2)
        for p in range(len(pairs)):
            def product(p=p):
                return lax.dot_general(refs[2 * p][...], refs[2 * p + 1][...], _DIMS["nt"], preferred_element_type=f32)

            if p == 0:
                @pl.when(k == 0)
                def _():
                    acc_ref[...] = product()

            @pl.when((k >= max(starts[p], 1)) & (k < starts[p] + nks[p]))
            def _():
                acc_ref[...] += product()

        @pl.when(k == nk - 1)
        def _():
            o_ref[...] = acc_ref[...].astype(bf16)

    in_specs, args = [], []
    for (a, b, tk), k0, n in zip(pairs, starts, nks):
        assert a.shape[1] == b.shape[1] and a.shape[1] % tk == 0 and a.dtype == b.dtype == bf16, name
        chunk = lambda k, k0=k0, n=n: jnp.clip(k - k0, 0, n - 1)
        in_specs += [pl.BlockSpec((tm, tk), lambda i, j, k, c=chunk: (i, c(k))),
                     pl.BlockSpec((tn, tk), lambda i, j, k, c=chunk: (j, c(k)))]
        args += [a, b]
    o_spec = pl.BlockSpec((tm, tn), lambda i, j, k: (i, j))
    (out,), extra = _call(
        body, args, name=name, out_shape=(SDS((M, N), bf16),), grid=(M // tm, N // tn, nk), in_specs=in_specs,
        out_specs=(o_spec,), scratch_shapes=[pltpu.VMEM((tm, tn), f32)],
        semantics=("parallel", "parallel", "arbitrary"), comm=comm)
    return out if comm is None else (out, extra)


def _mm_rows(pairs, mode, name, tm, epilogue, rows_in, vecs_in, rows_out, vecs_out, comm=None):
    M = pairs[0][0].shape[0]
    N = pairs[0][1].shape[1 if mode == "nn" else 0]
    nks = [a.shape[1] // tk for a, _, tk in pairs]
    starts = [sum(nks[:p]) for p in range(len(pairs))]
    nk = sum(nks)
    n_rows_in, n_vecs_in, n_rows_out = len(rows_in), len(vecs_in), len(rows_out)

    def body(*refs):
        pair_refs, refs = refs[:2 * len(pairs)], refs[2 * len(pairs):]
        rin, refs = refs[:n_rows_in], refs[n_rows_in:]
        vin, refs = refs[:n_vecs_in], refs[n_vecs_in:]
        rout, refs = refs[:n_rows_out], refs[n_rows_out:]
        vout, acc_ref = refs[:-1], refs[-1]
        i, k = pl.program_id(0), pl.program_id(1)
        for p in range(len(pairs)):
            def product(p=p):
                return lax.dot_general(pair_refs[2 * p][...], pair_refs[2 * p + 1][...], _DIMS[mode],
                                       preferred_element_type=f32)

            if p == 0:
                @pl.when(k == 0)
                def _():
                    acc_ref[...] = product()

            @pl.when((k >= max(starts[p], 1)) & (k < starts[p] + nks[p]))
            def _():
                acc_ref[...] += product()

        @pl.when(k == nk - 1)
        def _():
            res = epilogue(acc_ref[...], *[r[...] for r in rin], *[v[...] for v in vin])
            for ref, val in zip(rout, res[:n_rows_out]):
                ref[...] = val.astype(ref.dtype)
            for ref, val in zip(vout, res[n_rows_out:]):
                @pl.when(i == 0)
                def _(ref=ref, val=val):
                    ref[...] = val

                @pl.when(i > 0)
                def _(ref=ref, val=val):
                    ref[...] += val

    in_specs, args = [], []
    for (a, b, tk), k0, n in zip(pairs, starts, nks):
        assert a.shape[1] % tk == 0 and a.dtype == b.dtype == bf16, name
        chunk = lambda k, k0=k0, n=n: jnp.clip(k - k0, 0, n - 1)
        in_specs.append(pl.BlockSpec((tm, tk), lambda i, k, c=chunk: (i, c(k))))
        if mode == "nn":
            in_specs.append(pl.BlockSpec((tk, N), lambda i, k, c=chunk: (c(k), 0)))
        else:
            in_specs.append(pl.BlockSpec((N, tk), lambda i, k, c=chunk: (0, c(k))))
        args += [a, b]
    row = lambda: pl.BlockSpec((tm, N), lambda i, k: (i, 0))
    vec = lambda w: pl.BlockSpec((1, w), lambda i, k: (0, 0))
    in_specs += [row() for _ in rows_in] + [vec(v.shape[1]) for v in vecs_in]
    outs, extra = _call(
        body, (*args, *rows_in, *vecs_in), name=name,
        out_shape=tuple(SDS((M, N), dt) for dt in rows_out) + tuple(SDS((1, w), f32) for w in vecs_out),
        grid=(M // tm, nk), in_specs=in_specs,
        out_specs=tuple(row() for _ in rows_out) + tuple(vec(w) for w in vecs_out),
        scratch_shapes=[pltpu.VMEM((tm, N), f32)], semantics=("arbitrary", "arbitrary"), comm=comm)
    res = (outs[:n_rows_out], outs[n_rows_out:])
    return res if comm is None else (res, extra)


ROW_TILE = 512


def _rms_fwd(x, g):
    r = lax.rsqrt(jnp.mean(x * x, axis=-1, keepdims=True) + EPS)
    return x * r * g


def _rms_bwd(x, g, dy):
    r = lax.rsqrt(jnp.mean(x * x, axis=-1, keepdims=True) + EPS)
    xh = x * r
    dxh = dy * g
    dx = r * (dxh - xh * jnp.mean(dxh * xh, axis=-1, keepdims=True))
    return dx, jnp.sum(dy * xh, axis=0, keepdims=True)


def _acc_out(ref, val):
    @pl.when(pl.program_id(0) == 0)
    def _():
        ref[...] = val

    @pl.when(pl.program_id(0) > 0)
    def _():
        ref[...] += val


def _row_spec(width=D):
    return pl.BlockSpec((ROW_TILE, width), lambda i: (i, 0))


def _vec_spec(width=D):
    return pl.BlockSpec((1, width), lambda i: (0, 0))


def _norm_in(x, g, comm=None):
    def body(x_ref, g_ref, o_ref):
        o_ref[...] = _rms_fwd(x_ref[...], g_ref[...]).astype(bf16)

    T = x.shape[0]
    (hn,), extra = _call(body, (x, g), name="norm_in", out_shape=(SDS((T, D), bf16),), grid=(T // ROW_TILE,),
                         in_specs=[_row_spec(), _vec_spec()], out_specs=(_row_spec(),), semantics=("parallel",),
                         comm=comm)
    return hn, extra


def _mid_bwd(dy, dhn2, h1, g_fpre, mix, g_post):
    def body(dy_ref, dhn2_ref, h1_ref, gf_ref, mix_ref, gp_ref, dh1_ref, dmix_ref, dgf_ref, dgp_ref):
        d1, dgf = _rms_bwd(h1_ref[...], gf_ref[...], dhn2_ref[...].astype(f32))
        dh1 = dy_ref[...] + d1
        dh1_ref[...] = dh1
        dmix, dgp = _rms_bwd(mix_ref[...], gp_ref[...], dh1)
        dmix_ref[...] = dmix.astype(bf16)
        _acc_out(dgf_ref, dgf)
        _acc_out(dgp_ref, dgp)

    T = dy.shape[0]
    return pl.pallas_call(
        body, name="mid_bwd", out_shape=(SDS((T, D), f32), SDS((T, D), bf16), SDS((1, D), f32), SDS((1, D), f32)),
        grid=(T // ROW_TILE,),
        in_specs=[_row_spec(), _row_spec(), _row_spec(), _vec_spec(), _row_spec(), _vec_spec()],
        out_specs=(_row_spec(), _row_spec(), _vec_spec(), _vec_spec()),
        compiler_params=_params(("arbitrary",)))(dy, dhn2, h1, g_fpre, mix, g_post)


def _in_bwd(dh1, dhn, x, g_pre):
    def body(dh1_ref, dhn_ref, x_ref, g_ref, dx_ref, dg_ref):
        d, dg = _rms_bwd(x_ref[...], g_ref[...], dhn_ref[...].astype(f32))
        dx_ref[...] = dh1_ref[...] + d
        _acc_out(dg_ref, dg)

    T = x.shape[0]
    return pl.pallas_call(
        body, name="in_bwd", out_shape=(SDS((T, D), f32), SDS((1, D), f32)), grid=(T // ROW_TILE,),
        in_specs=[_row_spec(), _row_spec(), _row_spec(), _vec_spec()], out_specs=(_row_spec(), _vec_spec()),
        compiler_params=_params(("arbitrary",)))(dh1, dhn, x, g_pre)


def _mid_fwd_rows(mix, x, g_post, g_fpre):
    h1 = x + _rms_fwd(mix, g_post)
    return mix, h1, _rms_fwd(h1, g_fpre)


def _final_rows(ff, h1, target, g_fpost):
    e = h1 + _rms_fwd(ff, g_fpost) - target
    part = jnp.sum(jnp.sum(e * e, axis=1, keepdims=True), axis=0, keepdims=True) * (0.5 / D)
    dy = e * (1.0 / D)
    dff, dg = _rms_bwd(ff, g_fpost, dy)
    return dy, dff, part, dg


def _shift_dn(x, d, row, fill=0.0):
    if d == 0:
        return x
    y = pltpu.roll(x, d, 0)
    head = jnp.where(row[:SUBLANES] >= d, y[:SUBLANES], fill)
    return jnp.concatenate([head, y[SUBLANES:]], axis=0)


def _shift_up(x, d, row, fill=0.0):
    if d == 0:
        return x
    n = x.shape[0]
    y = pltpu.roll(x, n - d, 0)
    tail = jnp.where(row[:SUBLANES] < SUBLANES - d, y[n - SUBLANES:], fill)
    return jnp.concatenate([y[:n - SUBLANES], tail], axis=0)


def _conv_fwd(x, w_ref, b, row):
    K = w_ref.shape[0]
    y = b
    for k in range(K):
        y = y + w_ref[k:k + 1, :] * _shift_dn(x, K - 1 - k, row)
    return y


def _conv_bwd(x, w_ref, dy, row):
    K = w_ref.shape[0]
    dx = jnp.zeros_like(dy)
    dws = []
    for k in range(K):
        dx = dx + w_ref[k:k + 1, :] * _shift_up(dy, K - 1 - k, row)
        dws.append(jnp.sum(dy * _shift_dn(x, K - 1 - k, row), axis=0, keepdims=True))
    return dx, dws, jnp.sum(dy, axis=0, keepdims=True)


def _scan_fwd(a, u, row):
    n = a.shape[0]
    d = 1
    while d < n:
        last = 2 * d >= n
        if d < SUBLANES:
            u = u + a * _shift_dn(u, d, row)
            if not last:
                a = a * _shift_dn(a, d, row, fill=1.0)
        else:
            u = jnp.concatenate([u[:d], u[d:] + a[d:] * u[:n - d]], axis=0)
            if not last:
                a = jnp.concatenate([a[:d], a[d:] * a[:n - d]], axis=0)
        d *= 2
    return u


def _scan_bwd(b, u, row):
    n = b.shape[0]
    d = 1
    while d < n:
        last = 2 * d >= n
        if d < SUBLANES:
            u = u + b * _shift_up(u, d, row)
            if not last:
                b = b * _shift_up(b, d, row, fill=1.0)
        else:
            u = jnp.concatenate([u[:n - d] + b[:n - d] * u[d:], u[n - d:]], axis=0)
            if not last:
                b = jnp.concatenate([b[:n - d] * b[d:], b[n - d:]], axis=0)
        d *= 2
    return u


def _neg_expm1(z):
    series = -z * (1.0 + z * (0.5 + z * (1.0 / 6.0 + z * (1.0 / 24.0 + z * (1.0 / 120.0)))))
    return jnp.where(z > -0.1, series, 1.0 - jnp.exp(z))


def _softplus_neg(lam):
    z = -lam
    return jnp.maximum(z, 0.0) + jnp.log(1.0 + jnp.exp(-jnp.abs(z)))


def _rnn_specs(B):
    blk = lambda: pl.BlockSpec((SEQ, LANES), lambda b, n: (b, n))
    return dict(
        act=blk,
        convw=pl.BlockSpec((RNN_CONV, LANES), lambda b, n: (0, n)),
        vec=lambda: pl.BlockSpec((1, LANES), lambda b, n: (0, n)),
        gate=lambda: pl.BlockSpec((None, LANES, LANES), lambda b, n: (n, 0, 0)),
    )


def _rnn_fwd(proj, cw, cb, wa, ba, wx, bx, lam, comm=None):
    T = proj.shape[0]
    B = T // SEQ

    def body(x_ref, cw_ref, cb_ref, wa_ref, ba_ref, wx_ref, bx_ref, lam_ref, h_ref, xc_ref, r_ref, i_ref, a_ref, s_ref):
        row = lax.broadcasted_iota(jnp.int32, (SEQ, LANES), 0)
        xc = _conv_fwd(x_ref[...], cw_ref, cb_ref[...], row)
        xcb = xc.astype(bf16)
        r = _sigmoid(jnp.dot(xcb, wa_ref[...].astype(bf16), preferred_element_type=f32) + ba_ref[...])
        i = _sigmoid(jnp.dot(xcb, wx_ref[...].astype(bf16), preferred_element_type=f32) + bx_ref[...])
        log_a = (-LRU_C * _softplus_neg(lam_ref[...])) * r
        a = jnp.exp(log_a)
        s = jnp.sqrt(_neg_expm1(2.0 * log_a))
        xc_ref[...], r_ref[...], i_ref[...], a_ref[...], s_ref[...] = xc, r, i, a, s
        h_ref[...] = _scan_fwd(a, s * (i * xc), row)

    sp_ = _rnn_specs(B)
    return _call(
        body, (proj, cw, cb, wa, ba, wx, bx, lam), name="rnn_fwd", out_shape=(SDS((T, RNN_W), f32),) * 6,
        grid=(B, RNN_BLOCKS),
        in_specs=[sp_["act"](), sp_["convw"], sp_["vec"](), sp_["gate"](), sp_["vec"](), sp_["gate"](),
                  sp_["vec"](), sp_["vec"]()],
        out_specs=tuple(sp_["act"]() for _ in range(6)), semantics=("parallel", "parallel"), comm=comm)


def _rnn_bwd(proj, saved, dh, cw, wa, wx, lam, comm=None):
    T = proj.shape[0]
    B = T // SEQ

    def body(x_ref, h_ref, xc_ref, r_ref, i_ref, a_ref, s_ref, dh_ref, cw_ref, wa_ref, wx_ref, lam_ref,
             dx_ref, dcw_ref, dcb_ref, dwa_ref, dba_ref, dwx_ref, dbx_ref, dlam_ref):
        row = lax.broadcasted_iota(jnp.int32, (SEQ, LANES), 0)
        xr = x_ref[...]
        wa, wx, lam = wa_ref[...], wx_ref[...], lam_ref[...]
        xc, r, i, a, s = xc_ref[...], r_ref[...], i_ref[...], a_ref[...], s_ref[...]
        xcb = xc.astype(bf16)
        sp = _softplus_neg(lam)
        hprev = _shift_dn(h_ref[...], 1, row)
        g = _scan_bwd(_shift_up(a, 1, row), dh_ref[...].astype(f32), row)
        da = g * hprev
        ds = g * (i * xc)
        di = g * (s * xc)
        dxc = g * (s * i)
        dla = da * a - ds * (a * a) / s
        dr = dla * (-LRU_C * sp)
        dsp = jnp.sum(dla * (-LRU_C * r), axis=0, keepdims=True)
        dlam = -dsp * _sigmoid(-lam)
        dga = dr * r * (1.0 - r)
        dgx = di * i * (1.0 - i)
        dgab, dgxb = dga.astype(bf16), dgx.astype(bf16)
        dwa = lax.dot_general(xcb, dgab, _DIMS["tn"], preferred_element_type=f32)
        dwx = lax.dot_general(xcb, dgxb, _DIMS["tn"], preferred_element_type=f32)
        dxc = dxc + lax.dot_general(dgab, wa.astype(bf16), _DIMS["nt"], preferred_element_type=f32)
        dxc = dxc + lax.dot_general(dgxb, wx.astype(bf16), _DIMS["nt"], preferred_element_type=f32)
        dx, dws, db = _conv_bwd(xr, cw_ref, dxc, row)
        dx_ref[...] = dx.astype(bf16)
        first = pl.program_id(1) == 0

        def acc(ref, val):
            @pl.when(first)
            def _():
                ref[...] = val

            @pl.when(jnp.logical_not(first))
            def _():
                ref[...] += val

        for k in range(RNN_CONV):
            acc(dcw_ref.at[k:k + 1, :], dws[k])
        acc(dcb_ref, db)
        acc(dwa_ref, dwa)
        acc(dba_ref, jnp.sum(dga, axis=0, keepdims=True))
        acc(dwx_ref, dwx)
        acc(dbx_ref, jnp.sum(dgx, axis=0, keepdims=True))
        acc(dlam_ref, dlam)

    blk = lambda: pl.BlockSpec((SEQ, LANES), lambda n, b: (b, n))
    convw = lambda: pl.BlockSpec((RNN_CONV, LANES), lambda n, b: (0, n))
    vec = lambda: pl.BlockSpec((1, LANES), lambda n, b: (0, n))
    gate = lambda: pl.BlockSpec((None, LANES, LANES), lambda n, b: (n, 0, 0))
    vshape = SDS((1, RNN_W), f32)
    gshape = SDS((RNN_BLOCKS, LANES, LANES), f32)
    return _call(
        body, (proj, *saved, dh, cw, wa, wx, lam), name="rnn_bwd",
        out_shape=(SDS((T, RNN_W), bf16), SDS((RNN_CONV, RNN_W), f32), vshape, gshape, vshape, gshape, vshape, vshape),
        grid=(RNN_BLOCKS, B),
        in_specs=[blk() for _ in range(8)] + [convw(), gate(), gate(), vec()],
        out_specs=(blk(), convw(), vec(), gate(), vec(), gate(), vec(), vec()),
        semantics=("parallel", "arbitrary"), comm=comm)


def _t5_bucket(dist):
    max_exact = REL_BUCKETS // 2
    d = np.maximum(dist, 1).astype(np.float32)
    large = max_exact + np.log(d / max_exact) / math.log(REL_MAX_DIST / max_exact) * (REL_BUCKETS - max_exact)
    large = np.minimum(large.astype(np.int32), REL_BUCKETS - 1)
    return np.where(dist < max_exact, dist, large).astype(np.int32)


def _bucket_maps():
    qi = np.arange(ATT_BLK)[:, None]
    kj = np.arange(2 * ATT_BLK)[None, :]
    delta = ATT_BLK + qi - kj
    valid = (delta >= 0) & (delta <= ATT_BLK)
    maps = [np.where(valid, _t5_bucket(np.maximum(delta, 0) * r), -1) for r in DILATIONS]
    return np.stack(maps).astype(np.int32)


def _bias_tables(rel_bias, buckets):
    def body(rb_ref, bk_ref, o_ref):
        for g in range(NG):
            bk = bk_ref[g]
            for h in range(KVH):
                acc = jnp.full(bk.shape, NEG, f32)
                for b in range(REL_BUCKETS):
                    acc = jnp.where(bk == b, rb_ref[b, g * KVH + h], acc)
                o_ref[h, g] = acc

    return pl.pallas_call(
        body, name="bias_tables", out_shape=SDS((KVH, NG, ATT_BLK, 2 * ATT_BLK), f32),
        in_specs=[pl.BlockSpec(memory_space=pltpu.SMEM), pl.BlockSpec(memory_space=pltpu.VMEM)],
        out_specs=pl.BlockSpec(memory_space=pltpu.VMEM), compiler_params=_params())(rel_bias, buckets)


def _bias_grad(dbias, buckets):
    def body(db_ref, bk_ref, o_ref):
        rr = lax.broadcasted_iota(jnp.int32, (REL_BUCKETS, NG * KVH), 0)
        cc = lax.broadcasted_iota(jnp.int32, (REL_BUCKETS, NG * KVH), 1)
        out = jnp.zeros((REL_BUCKETS, NG * KVH), f32)
        for g in range(NG):
            bk = bk_ref[g]
            for h in range(KVH):
                d = db_ref[h, g]
                for b in range(REL_BUCKETS):
                    m = jnp.where(bk == b, d, 0.0)
                    s = jnp.sum(jnp.sum(m, axis=1, keepdims=True), axis=0, keepdims=True)
                    out = jnp.where((rr == b) & (cc == g * KVH + h), s, out)
        o_ref[...] = out

    return pl.pallas_call(body, name="bias_grad", out_shape=SDS((REL_BUCKETS, NG * KVH), f32),
                          compiler_params=_params())(dbias, buckets)


def _to_sub(dst_ref, src_ref, r, dtype, offset=0):
    M = SEQ // r
    for c in range(r):
        if r == 1:
            v = src_ref[...]
        else:
            v = src_ref[pl.ds(c, M, stride=r), :]
        dst_ref[pl.ds(offset + c * M, M), :] = v.astype(dtype)


def _from_sub(dst_ref, src_ref, r, accumulate=False, offset=0):
    M = SEQ // r
    for c in range(r):
        v = src_ref[pl.ds(offset + c * M, M), :]
        idx = slice(None) if r == 1 else pl.ds(c, M, stride=r)
        if accumulate:
            dst_ref[idx, :] = dst_ref[idx, :] + v
        else:
            dst_ref[idx, :] = v


_COL = lambda k: slice(k * HD, (k + 1) * HD)
SCALE = HD ** -0.5


def _qkv_spec(k, bh):
    def index(*ids):
        b, h = bh(*ids)
        return (b, C_ATT // HD + 5 * h + k)

    return pl.BlockSpec((SEQ, HD), index)


def _key_window(bias_ref, g, nb):
    if nb == 1:
        bias_own = bias_ref[g, :, ATT_BLK:2 * ATT_BLK]
        return lambda j: (pl.ds(pl.multiple_of((j + 1) * ATT_BLK, ATT_BLK), ATT_BLK), bias_own)
    bias_g = bias_ref[g]
    col = lax.broadcasted_iota(jnp.int32, bias_g.shape, 1)
    bias_first = jnp.where(col >= ATT_BLK, bias_g, NEG)
    return lambda j: (pl.ds(pl.multiple_of(j * ATT_BLK, ATT_BLK), 2 * ATT_BLK),
                      jnp.where(j % nb != 0, bias_g, bias_first))


def _att_fwd(proj, bias, comm=None):
    T = proj.shape[0]
    B = T // SEQ

    def body(q0_ref, q1_ref, q2_ref, k_ref, v_ref, bias_ref, o_ref, lse_ref, *rest):
        saved, (qp, kp, vp, kt, op, lp, og, lg) = rest[:3 * (NG - 1)], rest[3 * (NG - 1):]
        q_refs = (q0_ref, q1_ref, q2_ref)
        kp[0:ATT_BLK, :] = jnp.zeros((ATT_BLK, HD), bf16)
        vp[0:ATT_BLK, :] = jnp.zeros((ATT_BLK, HD), bf16)
        for g, r in enumerate(DILATIONS):
            nb = NBLK_SEQ // r
            _to_sub(qp, q_refs[g], r, bf16)
            _to_sub(kp, k_ref, r, bf16, offset=ATT_BLK)
            _to_sub(vp, v_ref, r, bf16, offset=ATT_BLK)
            if r > 1:
                sq, sk, sv = saved[3 * (g - 1):3 * g]
                sq[...], sk[...], sv[...] = qp[...], kp[ATT_BLK:, :], vp[ATT_BLK:, :]
            kt[...] = kp[...].T
            keys = _key_window(bias_ref, g, nb)

            def step(j, carry):
                cur = pl.ds(pl.multiple_of(j * ATT_BLK, ATT_BLK), ATT_BLK)
                win, bias_j = keys(j)
                s = jnp.dot(qp[cur, :], kt[:, win], preferred_element_type=f32) * SCALE + bias_j
                m = jnp.max(s, axis=-1, keepdims=True)
                p = jnp.exp(s - m)
                den = jnp.sum(p, axis=-1, keepdims=True)
                o = jnp.dot(p.astype(bf16), vp[win, :], preferred_element_type=f32)
                op[cur, :] = o / den
                lp[cur, :] = jnp.broadcast_to(m + jnp.log(den), (ATT_BLK, HD))
                return carry

            lax.fori_loop(0, NBLK_SEQ, step, 0, unroll=NBLK_SEQ)
            _from_sub(og.at[g], op, r)
            _from_sub(lg.at[g], lp, r)
        l0, l1, l2 = lg[0], lg[1], lg[2]
        mx = jnp.maximum(jnp.maximum(l0, l1), l2)
        e0, e1, e2 = jnp.exp(l0 - mx), jnp.exp(l1 - mx), jnp.exp(l2 - mx)
        den = e0 + e1 + e2
        o_ref[...] = (e0 * og[0] + e1 * og[1] + e2 * og[2]) / den
        lse_ref[...] = mx + jnp.log(den)

    return _call(
        body, (proj, proj, proj, proj, proj, bias), name="att_fwd",
        out_shape=(SDS((T, KVH * HD), f32), SDS((KVH, T, HD), f32)) + (SDS((T, KVH * HD), bf16),) * (3 * (NG - 1)),
        grid=(B, KVH),
        in_specs=[_qkv_spec(k, lambda b, h: (b, h)) for k in range(5)]
                 + [pl.BlockSpec((None, NG, ATT_BLK, 2 * ATT_BLK), lambda b, h: (h, 0, 0, 0))],
        out_specs=(pl.BlockSpec((SEQ, HD), lambda b, h: (b, h)),
                   pl.BlockSpec((None, SEQ, HD), lambda b, h: (h, b, 0)))
                  + tuple(pl.BlockSpec((SEQ, HD), lambda b, h: (b, h)) for _ in range(3 * (NG - 1))),
        scratch_shapes=[pltpu.VMEM((SEQ, HD), bf16)] + [pltpu.VMEM((SEQ + ATT_BLK, HD), bf16)] * 2
                       + [pltpu.VMEM((HD, SEQ + ATT_BLK), bf16)]
                       + [pltpu.VMEM((SEQ, HD), f32)] * 2 + [pltpu.VMEM((NG, SEQ, HD), f32)] * 2,
        semantics=("parallel", "parallel"), comm=comm)


def _att_bwd(proj, saved, bias, o, lse, do, comm=None):
    T = proj.shape[0]
    B = T // SEQ
    n_saved = 3 * (NG - 1)

    def body(q0_ref, k_ref, v_ref, *rest):
        saved_refs, rest = rest[:n_saved], rest[n_saved:]
        (bias_ref, o_ref, lse_ref, do_ref, dx_ref, db_ref,
         qp, kp, vp, dop, qt, kt, vt, dot, lp, dqp, dkt, dvt, dln, nat, dkn, dvn) = rest
        first = pl.program_id(1) == 0

        @pl.when(first)
        def _():
            db_ref[...] = jnp.zeros_like(db_ref)

        lane = lax.broadcasted_iota(jnp.int32, (SEQ, HD), 1)
        dln[...] = jnp.where(lane < STAT_LANE, lse_ref[...],
                             jnp.sum(do_ref[...] * o_ref[...], axis=-1, keepdims=True))
        dkn[...] = jnp.zeros_like(dkn)
        dvn[...] = jnp.zeros_like(dvn)
        kp[0:ATT_BLK, :] = jnp.zeros((ATT_BLK, HD), bf16)
        vp[0:ATT_BLK, :] = jnp.zeros((ATT_BLK, HD), bf16)
        for g, r in enumerate(DILATIONS):
            nb = NBLK_SEQ // r
            if r == 1:
                qp[...] = q0_ref[...].astype(bf16)
                kp[ATT_BLK:, :] = k_ref[...].astype(bf16)
                vp[ATT_BLK:, :] = v_ref[...].astype(bf16)
            else:
                sq, sk, sv = saved_refs[3 * (g - 1):3 * g]
                qp[...], kp[ATT_BLK:, :], vp[ATT_BLK:, :] = sq[...], sk[...], sv[...]
            _to_sub(dop, do_ref, r, bf16)
            _to_sub(lp, dln, r, f32)
            qt[...], kt[...], vt[...], dot[...] = qp[...].T, kp[...].T, vp[...].T, dop[...].T
            dkt[...] = jnp.zeros_like(dkt)
            dvt[...] = jnp.zeros_like(dvt)
            keys = _key_window(bias_ref, g, nb)
            db_cols = slice(ATT_BLK, 2 * ATT_BLK) if nb == 1 else slice(None)

            def step(j, carry):
                cur = pl.ds(pl.multiple_of(j * ATT_BLK, ATT_BLK), ATT_BLK)
                win, bias_j = keys(j)
                s = jnp.dot(qp[cur, :], kt[:, win], preferred_element_type=f32) * SCALE + bias_j
                p = jnp.exp(s - lp[cur, 0:1])
                dp = jnp.dot(dop[cur, :], vt[:, win], preferred_element_type=f32)
                ds = p * (dp - lp[cur, STAT_LANE:STAT_LANE + 1])
                db_ref[g, :, db_cols] += ds
                dsb, pb = ds.astype(bf16), p.astype(bf16)
                dqp[cur, :] = jnp.dot(dsb, kp[win, :], preferred_element_type=f32) * SCALE
                dkt[:, win] += jnp.dot(qt[:, cur], dsb, preferred_element_type=f32) * SCALE
                dvt[:, win] += jnp.dot(dot[:, cur], pb, preferred_element_type=f32)
                return carry

            lax.fori_loop(0, NBLK_SEQ, step, 0, unroll=NBLK_SEQ)
            _from_sub(nat, dqp, r)
            dx_ref[:, _COL(g)] = nat[...].astype(bf16)
            dqp[...] = dkt[:, ATT_BLK:].T
            _from_sub(dkn, dqp, r, accumulate=True)
            dqp[...] = dvt[:, ATT_BLK:].T
            _from_sub(dvn, dqp, r, accumulate=True)
        dx_ref[:, _COL(3)] = dkn[...].astype(bf16)
        dx_ref[:, _COL(4)] = dvn[...].astype(bf16)

    blk = lambda: pl.BlockSpec((SEQ, HD), lambda h, b: (b, h))
    bias_spec = lambda: pl.BlockSpec((None, NG, ATT_BLK, 2 * ATT_BLK), lambda h, b: (h, 0, 0, 0))
    pad = lambda dtype: pltpu.VMEM((SEQ + ATT_BLK, HD), dtype)
    pad_t = lambda dtype: pltpu.VMEM((HD, SEQ + ATT_BLK), dtype)
    seq_t = pltpu.VMEM((HD, SEQ), bf16)
    return _call(
        body, (proj, proj, proj, *saved, bias, o, lse, do), name="att_bwd",
        out_shape=(SDS((T, KVH * ATT_COLS), bf16), SDS((KVH, NG, ATT_BLK, 2 * ATT_BLK), f32)), grid=(KVH, B),
        in_specs=[_qkv_spec(k, lambda h, b: (b, h)) for k in (0, 3, 4)] + [blk() for _ in range(n_saved)]
                 + [bias_spec(), blk(), pl.BlockSpec((None, SEQ, HD), lambda h, b: (h, b, 0)), blk()],
        out_specs=(pl.BlockSpec((SEQ, ATT_COLS), lambda h, b: (b, h)), bias_spec()),
        scratch_shapes=[pltpu.VMEM((SEQ, HD), bf16), pad(bf16), pad(bf16), pltpu.VMEM((SEQ, HD), bf16),
                        seq_t, pad_t(bf16), pad_t(bf16), seq_t]
                       + [pltpu.VMEM((SEQ, HD), f32)] * 2 + [pad_t(f32)] * 2 + [pltpu.VMEM((SEQ, HD), f32)] * 4,
        semantics=("parallel", "arbitrary"), comm=comm)


MERGE_ROWS, MERGE_COLS = 1024, 512


def _merge_fwd(gates, pr, pa):
    def body(gr_ref, ga_ref, pr_ref, pa_ref, o_ref):
        o_ref[...] = (_sigmoid(gr_ref[...].astype(f32)) * pr_ref[...].astype(f32)
                      + _sigmoid(ga_ref[...].astype(f32)) * pa_ref[...].astype(f32)).astype(bf16)

    T = gates.shape[0]
    cols = lambda off: pl.BlockSpec((MERGE_ROWS, MERGE_COLS), lambda i, j: (i, off + j))
    return pl.pallas_call(body, name="merge_fwd", out_shape=SDS((T, D), bf16),
                          grid=(T // MERGE_ROWS, D // MERGE_COLS),
                          in_specs=[cols(0), cols(D // MERGE_COLS), cols(0), cols(0)], out_specs=cols(0),
                          compiler_params=_params(("parallel", "parallel")))(gates, gates, pr, pa)


def _merge_bwd(gates, pr, pa, dm):
    nj = D // MERGE_COLS

    def body(g_ref, pr_ref, pa_ref, dm_ref, dp_ref, dg_ref):
        dm_ = dm_ref[...].astype(f32)
        s = _sigmoid(g_ref[...].astype(f32))
        p = jnp.where(pl.program_id(1) < nj, pr_ref[...], pa_ref[...]).astype(f32)
        dp_ref[...] = (dm_ * s).astype(bf16)
        dg_ref[...] = (dm_ * p * s * (1.0 - s)).astype(bf16)

    T = gates.shape[0]
    blk = (MERGE_ROWS, MERGE_COLS)
    wrap = pl.BlockSpec(blk, lambda i, j: (i, j % nj))
    pr_spec = pl.BlockSpec(blk, lambda i, j: (i, jnp.minimum(j, nj - 1)))
    pa_spec = pl.BlockSpec(blk, lambda i, j: (i, jnp.maximum(j - nj, 0)))
    out = pl.BlockSpec(blk, lambda i, j: (i, j))
    return pl.pallas_call(
        body, name="merge_bwd", out_shape=(SDS((T, 2 * D), bf16), SDS((T, 2 * D), bf16)),
        grid=(T // MERGE_ROWS, 2 * nj),
        in_specs=[out, pr_spec, pa_spec, wrap], out_specs=(out, out),
        compiler_params=_params(("parallel", "parallel")))(gates, pr, pa, dm)


FFN_COLS = 256
GELU_C = math.sqrt(2.0 / math.pi)
GELU_A = 0.044715


def _gelu_parts(x):
    q = x * x
    t = jnp.tanh(x * (GELU_C + (GELU_C * GELU_A) * q))
    h = 0.5 + 0.5 * t
    return x * h, h * (1.0 + x * (1.0 - t) * (GELU_C + (3.0 * GELU_C * GELU_A) * q))


def _ffn_act_fwd(gpre, up, cw, cb):
    def body(g_ref, u_ref, cw_ref, cb_ref, o_ref):
        row = lax.broadcasted_iota(jnp.int32, (SEQ, FFN_COLS), 0)
        gate = _conv_fwd(g_ref[...].astype(f32), cw_ref, cb_ref[...], row)
        o_ref[...] = (_gelu_parts(gate)[0] * u_ref[...].astype(f32)).astype(bf16)

    T = gpre.shape[0]
    blk = lambda: pl.BlockSpec((SEQ, FFN_COLS), lambda b, j: (b, j))
    return pl.pallas_call(
        body, name="ffn_act_fwd", out_shape=SDS((T, FFN_W), bf16), grid=(T // SEQ, FFN_W // FFN_COLS),
        in_specs=[blk(), blk(), pl.BlockSpec((FFN_CONV, FFN_COLS), lambda b, j: (0, j)),
                  pl.BlockSpec((1, FFN_COLS), lambda b, j: (0, j))],
        out_specs=blk(), compiler_params=_params(("parallel", "parallel")))(gpre, up, cw, cb)


def _ffn_act_bwd(gpre, up, cw, cb, dact):
    def body(g_ref, u_ref, cw_ref, cb_ref, da_ref, dg_ref, du_ref, dcw_ref, dcb_ref):
        row = lax.broadcasted_iota(jnp.int32, (SEQ, FFN_COLS), 0)
        gp = g_ref[...].astype(f32)
        gate = _conv_fwd(gp, cw_ref, cb_ref[...], row)
        gel, dgel = _gelu_parts(gate)
        da = da_ref[...].astype(f32)
        du_ref[...] = (da * gel).astype(bf16)
        dgate = da * u_ref[...].astype(f32) * dgel
        dx, dws, db = _conv_bwd(gp, cw_ref, dgate, row)
        dg_ref[...] = dx.astype(bf16)
        first = pl.program_id(1) == 0

        def acc(ref, val):
            @pl.when(first)
            def _():
                ref[...] = val

            @pl.when(jnp.logical_not(first))
            def _():
                ref[...] += val

        for k in range(FFN_CONV):
            acc(dcw_ref.at[k:k + 1, :], dws[k])
        acc(dcb_ref, db)

    T = gpre.shape[0]
    blk = lambda: pl.BlockSpec((SEQ, FFN_COLS), lambda j, b: (b, j))
    cws = lambda: pl.BlockSpec((FFN_CONV, FFN_COLS), lambda j, b: (0, j))
    cbs = lambda: pl.BlockSpec((1, FFN_COLS), lambda j, b: (0, j))
    return pl.pallas_call(
        body, name="ffn_act_bwd",
        out_shape=(SDS((T, FFN_W), bf16), SDS((T, FFN_W), bf16), SDS((FFN_CONV, FFN_W), f32), SDS((1, FFN_W), f32)),
        grid=(FFN_W // FFN_COLS, T // SEQ),
        in_specs=[blk(), blk(), cws(), cbs(), blk()], out_specs=(blk(), blk(), cws(), cbs()),
        compiler_params=_params(("parallel", "arbitrary")))(gpre, up, cw, cb, dact)


def _coords():
    return lax.axis_index("x"), lax.axis_index("y"), lax.axis_index("c")


def _dev_index(dev):
    return 4 * dev[0] + 2 * dev[1] + dev[2]


def _dma_sems(n):
    return [pltpu.SemaphoreType.DMA((n,)), pltpu.SemaphoreType.DMA((n,))]


def _gather_two_level(arrays):
    n = len(arrays)

    def plan(ins, outs, sems):
        send_sems, recv_sems, local_sems = sems
        x, y, c = _coords()
        me, sibling = (x, y, c), (x, y, 1 - c)
        chips = [(1 - x, y), (x, 1 - y), (1 - x, 1 - y)]

        def copy(a, k, block, to, own=False):
            dst = outs[a].at[_dev_index(block)]
            return pltpu.make_async_remote_copy(
                src_ref=ins[a] if own else dst, dst_ref=dst, send_sem=send_sems.at[7 * a + k],
                recv_sem=recv_sems.at[7 * a + k], device_id=to, device_id_type=MESH)

        mine = [pltpu.make_async_copy(ins[a], outs[a].at[_dev_index(me)], local_sems.at[a]) for a in range(n)]
        first = [copy(a, 0, me, sibling, own=True) for a in range(n)]
        first += [copy(a, 1 + j, me, (*chip, c), own=True) for a in range(n) for j, chip in enumerate(chips)]
        passed = [[copy(a, 4 + j, (*chip, c), sibling) for a in range(n)] for j, chip in enumerate(chips)]
        arrive_ici = [[copy(a, 1 + j, (*chip, c), me) for a in range(n)] for j, chip in enumerate(chips)]
        arrive_d2d = [copy(a, 0, sibling, me) for a in range(n)]
        arrive_d2d += [copy(a, 4 + j, (*chip, 1 - c), me) for a in range(n) for j, chip in enumerate(chips)]
        return mine, first, passed, arrive_ici, arrive_d2d

    def start(ins, outs, sems):
        mine, first, _, _, _ = plan(ins, outs, sems)
        for cp in mine + first:
            cp.start()

    def finish(ins, outs, sems):
        mine, first, passed, arrive_ici, arrive_d2d = plan(ins, outs, sems)
        for j in range(3):
            for cp in arrive_ici[j]:
                cp.wait_recv()
            for cp in passed[j]:
                cp.start()
        for cp in arrive_d2d:
            cp.wait_recv()
        for cp in first + [cp for group in passed for cp in group]:
            cp.wait_send()
        for cp in mine:
            cp.wait()

    return _Comm(arrays, [SDS((N_DEV,) + a.shape, a.dtype) for a in arrays],
                 _dma_sems(7 * n) + [pltpu.SemaphoreType.DMA((n,))], start, finish)


def _gather_direct(arrays):
    n = len(arrays)

    def plan(ins, outs, sems):
        send_sems, recv_sems, local_sems = sems
        x, y, c = _coords()
        me = (x, y, c)
        mine = [pltpu.make_async_copy(ins[a], outs[a].at[_dev_index(me)], local_sems.at[a]) for a in range(n)]
        sends, arrivals = [], []
        for a in range(n):
            for k in range(1, N_DEV):
                peer = (1 - x if k & 4 else x, 1 - y if k & 2 else y, 1 - c if k & 1 else c)
                s = 7 * a + k - 1
                for slot, out in ((me, sends), (peer, arrivals)):
                    out.append(pltpu.make_async_remote_copy(
                        src_ref=ins[a], dst_ref=outs[a].at[_dev_index(slot)], send_sem=send_sems.at[s],
                        recv_sem=recv_sems.at[s], device_id=peer, device_id_type=MESH))
        return mine, sends, arrivals

    def start(ins, outs, sems):
        mine, sends, _ = plan(ins, outs, sems)
        for cp in mine + sends:
            cp.start()

    def finish(ins, outs, sems):
        mine, sends, arrivals = plan(ins, outs, sems)
        for cp in arrivals:
            cp.wait_recv()
        for cp in sends:
            cp.wait_send()
        for cp in mine:
            cp.wait()

    return _Comm(arrays, [SDS((N_DEV,) + a.shape, a.dtype) for a in arrays],
                 _dma_sems(7 * n) + [pltpu.SemaphoreType.DMA((n,))], start, finish)


def _scatter_direct(arrays):
    n = len(arrays)

    def plan(ins, outs, sems):
        send_sems, recv_sems = sems
        x, y, c = _coords()
        cps = []
        for a in range(n):
            for k in range(1, N_DEV):
                peer = (1 - x if k & 4 else x, 1 - y if k & 2 else y, 1 - c if k & 1 else c)
                s = 7 * a + k - 1
                cps.append(pltpu.make_async_remote_copy(
                    src_ref=ins[a].at[_dev_index(peer)], dst_ref=outs[a].at[k - 1], send_sem=send_sems.at[s],
                    recv_sem=recv_sems.at[s], device_id=peer, device_id_type=MESH))
        return cps

    def start(ins, outs, sems):
        for cp in plan(ins, outs, sems):
            cp.start()

    def finish(ins, outs, sems):
        for cp in plan(ins, outs, sems):
            cp.wait()

    return _Comm(arrays, [SDS((N_DEV - 1,) + a.shape[1:], a.dtype) for a in arrays], _dma_sems(7 * n),
                 start, finish)


def _run(comm, name):
    def body(*refs):
        k_in, k_out = len(comm.inputs), len(comm.out_shapes)
        ins, outs, sems = refs[:k_in], refs[k_in:k_in + k_out], refs[k_in + k_out:]
        comm.start(ins, outs, sems)
        comm.finish(ins, outs, sems)

    return pl.pallas_call(body, name=name, out_shape=comm.out_shapes, in_specs=[ANY] * len(comm.inputs),
                          out_specs=(ANY,) * len(comm.out_shapes), scratch_shapes=comm.sem_shapes)(*comm.inputs)


TILE_ELEMS = 192 * 1024


def _row_tile(R, C):
    if R * C <= TILE_ELEMS:
        return R
    return max(t for t in range(SUBLANES, R, SUBLANES) if R % t == 0 and t * C <= TILE_ELEMS)


def _adamw_math(w, g, m, v):
    m = ADAM_B1 * m + (1.0 - ADAM_B1) * g
    v = ADAM_B2 * v + (1.0 - ADAM_B2) * (g * g)
    m_hat = m / (1.0 - ADAM_B1 ** ADAM_STEP)
    v_hat = v / (1.0 - ADAM_B2 ** ADAM_STEP)
    delta = -ADAM_LR * (m_hat / (jnp.sqrt(v_hat) + ADAM_EPS) + ADAM_WD * w)
    return delta, m, v


def _adamw_sharded(own, recv, d_idx, w, m, v, name):
    R, C = w.shape
    t = _row_tile(R, C)

    def body(k_ref, p_ref, r_ref, w_ref, m_ref, v_ref, g_ref, d_ref, nm_ref, nv_ref):
        g = p_ref[...].astype(f32)
        for j in range(N_DEV - 1):
            g = g + r_ref[j].astype(f32)
        d, nm, nv = _adamw_math(w_ref[...], g, m_ref[...], v_ref[...])
        g_ref[...], d_ref[...], nm_ref[...], nv_ref[...] = g, d, nm, nv

    tile = lambda: pl.BlockSpec((t, C), lambda i, k: (i, 0))
    return pl.pallas_call(
        body, name="adamw_" + name, out_shape=(SDS((R, C), f32),) * 4,
        grid_spec=pltpu.PrefetchScalarGridSpec(
            num_scalar_prefetch=1, grid=(R // t,),
            in_specs=[pl.BlockSpec((None, t, C), lambda i, k: (k[0], i, 0)),
                      pl.BlockSpec((N_DEV - 1, t, C), lambda i, k: (0, i, 0)), tile(), tile(), tile()],
            out_specs=(tile(), tile(), tile(), tile())),
        compiler_params=_params(("parallel",)))(d_idx, own, recv, w, m, v)


def _adamw_replicated(parts, ws, ms, vs):
    n = len(ws)

    def body(*refs):
        p, w, m, v = (refs[i * n:(i + 1) * n] for i in range(4))
        outs = refs[4 * n:]
        for a in range(n):
            g = p[a][0].astype(f32)
            for j in range(1, N_DEV):
                g = g + p[a][j].astype(f32)
            d, nm, nv = _adamw_math(w[a][...], g, m[a][...], v[a][...])
            for i, val in enumerate((g, d, nm, nv)):
                outs[i * n + a][...] = val

    shapes = tuple(SDS(w.shape, f32) for w in ws)
    res = pl.pallas_call(body, name="adamw_replicated", out_shape=shapes * 4,
                         compiler_params=_params())(*parts, *ws, *ms, *vs)
    return [res[i * n:(i + 1) * n] for i in range(4)]


def _cols_to_full(g):
    n, r, c = g.shape
    return g.transpose(1, 0, 2).reshape(r, n * c)


def _full_to_cols(a):
    r, c = a.shape
    return a.reshape(r, N_DEV, c // N_DEV).transpose(1, 0, 2)


def _rows_blocked(a):
    r, c = a.shape
    return a.reshape(N_DEV, r // N_DEV, c)


def _w_in_to_internal(w):
    K = w.shape[0]
    q = w[:, 1280:2816].reshape(K, NG, KVH, 1, HD).transpose(0, 2, 1, 3, 4).reshape(K, KVH, NG, HD)
    k = w[:, 2816:3328].reshape(K, KVH, 1, HD)
    v = w[:, 3328:3840].reshape(K, KVH, 1, HD)
    att = jnp.concatenate([q, k, v], axis=2).reshape(K, KVH * ATT_COLS)
    return jnp.concatenate([w[:, :1280], att, w[:, 3840:]], axis=1)


def _w_in_from_internal(w):
    K = w.shape[0]
    att = w[:, C_ATT:C_GATE].reshape(K, KVH, 5, HD)
    q = att[:, :, 0:3].transpose(0, 2, 1, 3).reshape(K, NG * KVH * HD)
    k = att[:, :, 3].reshape(K, KVH * HD)
    v = att[:, :, 4].reshape(K, KVH * HD)
    return jnp.concatenate([w[:, :C_ATT], q, k, v, w[:, C_GATE:]], axis=1)


_IN_NAMES = ('x', 'rel_bias', 'norm_mix_pre', 'norm_mix_post', 'w_in', 'conv_rnn_w', 'conv_rnn_b', 'w_rg_a', 'b_rg_a',
             'w_rg_x', 'b_rg_x', 'lru_lambda', 'w_branch_rnn', 'w_branch_att', 'w_out', 'norm_ffn_pre',
             'norm_ffn_post', 'w_ffn_gate', 'w_ffn_up', 'conv_ffn_w', 'conv_ffn_b', 'w_ffn_down')
_WEIGHTS = _IN_NAMES[1:]
_SHARDED = {"w_in": "col", "conv_rnn_w": "col", "w_branch_rnn": "row", "w_branch_att": "col", "w_out": "row",
            "w_ffn_gate": "col", "w_ffn_up": "col", "conv_ffn_w": "col", "w_ffn_down": "row"}
_REPLICATED = tuple(n for n in _WEIGHTS if n not in _SHARDED)


def _flat2(a):
    return a.reshape(-1, a.shape[-1])


def _train_step(inp):
    x_idx, y_idx, c_idx = _coords()
    W = {n: inp[n] for n in _WEIGHTS}
    x = inp["x"].reshape(-1, D)
    target = inp["loss_target"].reshape(-1, D)
    shard = {n: inp[n][0] for n in _SHARDED}

    hn, (g_in, g_cr, g_cf) = _norm_in(x, W["norm_mix_pre"], comm=_gather_two_level(
        [shard["w_in"].astype(bf16), shard["conv_rnn_w"], shard["conv_ffn_w"]]))
    w_in = _w_in_to_internal(_cols_to_full(g_in))
    cw_rnn, cw_ffn = _cols_to_full(g_cr), _cols_to_full(g_cf)
    behind_rnn = ("w_branch_rnn", "w_branch_att", "w_out", "w_ffn_down")
    behind_att = ("w_ffn_gate", "w_ffn_up")

    wa, wx = W["w_rg_a"][0], W["w_rg_x"][0]
    buckets = jnp.asarray(_bucket_maps())

    proj = _mm(hn, (w_in, 0, C_GATE), "nn", f32, "mm_proj", 1024, C_GATE // 2, 1024, cols_outer=True)
    gates = _mm(hn, w_in[:, C_GATE:], "nn", bf16, "mm_gates", 1024, 1024, 1024, cols_outer=True)
    rnn_saved, got = _rnn_fwd(proj, cw_rnn, W["conv_rnn_b"], wa, W["b_rg_a"], wx, W["b_rg_x"], W["lru_lambda"],
                              comm=_gather_direct([shard[n].astype(bf16) for n in behind_rnn]))
    h_rnn = rnn_saved[0]
    gathered = dict(zip(behind_rnn, got))
    bias = _bias_tables(W["rel_bias"], buckets)
    (o_att, lse, *att_saved), got = _att_fwd(
        proj, bias, comm=_gather_direct([shard[n].astype(bf16) for n in behind_att]))
    gathered.update(zip(behind_att, got))
    w_brnn = gathered["w_branch_rnn"].reshape(RNN_W, D)
    w_batt = _cols_to_full(gathered["w_branch_att"])
    w_out = gathered["w_out"].reshape(D, D)
    w_gate, w_up = _cols_to_full(gathered["w_ffn_gate"]), _cols_to_full(gathered["w_ffn_up"])
    w_down = gathered["w_ffn_down"].reshape(FFN_W, D)
    pr = _mm(h_rnn, w_brnn, "nn", bf16, "mm_pr", 1024, 1024, 1280)
    pa = _mm(o_att, w_batt, "nn", bf16, "mm_pa", 1024, 1024, 512)
    merged = _merge_fwd(gates, pr, pa)
    (mix, h1, hn2), _ = _mm_rows([(merged, w_out, 1024)], "nn", "mm_mix", 1024, _mid_fwd_rows, [x],
                                 [W["norm_mix_post"], W["norm_ffn_pre"]], [f32, f32, bf16], [])
    gpre = _mm(hn2, w_gate, "nn", bf16, "mm_gate", 1024, 1024, 1024, cols_outer=True)
    up = _mm(hn2, w_up, "nn", bf16, "mm_up", 1024, 1024, 1024, cols_outer=True)
    act = _ffn_act_fwd(gpre, up, cw_ffn, W["conv_ffn_b"])
    (dy, dff), (loss_part, dg_fpost) = _mm_rows([(act, w_down, 1024)], "nn", "mm_down", 1024, _final_rows,
                                                [h1, target], [W["norm_ffn_post"]], [f32, bf16], [1, D])

    grads = {}
    dact = _mm(dff, w_down, "nt", bf16, "mm_dact", 1024, 1024, 1024, cols_outer=True)
    grads["w_ffn_down"] = _rows_blocked(_mm(act, dff, "tn", bf16, "mm_dw_down", 1024, 1024, 2048))
    dgpre, dup, dcw_ffn, dcb_ffn = _ffn_act_bwd(gpre, up, cw_ffn, W["conv_ffn_b"], dact)
    grads["conv_ffn_w"] = _full_to_cols(dcw_ffn.astype(bf16))
    grads["w_ffn_gate"] = _full_to_cols(_mm(hn2, dgpre, "tn", bf16, "mm_dw_gate", 1024, 1024, 2048))
    grads["w_ffn_up"] = _full_to_cols(_mm(hn2, dup, "tn", bf16, "mm_dw_up", 1024, 1024, 2048))
    dhn2 = _mm_nt_sum([(dgpre, w_gate, 1024), (dup, w_up, 1024)], "mm_dhn2", 1024, 1024)
    dh1, dmix, dg_fpre, dg_post = _mid_bwd(dy, dhn2, h1, W["norm_ffn_pre"], mix, W["norm_mix_post"])
    dmerged = _mm(dmix, w_out, "nt", bf16, "mm_dmerged", 1024, 1024, 1024)
    grads["w_out"] = _rows_blocked(_mm(merged, dmix, "tn", bf16, "mm_dw_out", 1024, 1024, 2048))
    dprpa, dgates = _merge_bwd(gates, pr, pa, dmerged)
    dpr, dpa = (dprpa, 0, D), (dprpa, D, D)
    dh_rnn = _mm(dpr, w_brnn, "nt", bf16, "mm_dh_rnn", 1024, 1280, 1024)
    grads["w_branch_rnn"] = _rows_blocked(_mm(h_rnn, dpr, "tn", bf16, "mm_dw_brnn", 1280, 1024, 1024))
    do_att = _mm(dpa, w_batt, "nt", f32, "mm_do_att", 1024, 512, 1024)
    grads["w_branch_att"] = _full_to_cols(_mm(o_att, dpa, "tn", bf16, "mm_dw_batt", 512, 1024, 2048))

    received = {}
    behind_att_bwd = ("w_ffn_down", "w_ffn_gate", "conv_ffn_w", "w_out")
    behind_rnn_bwd = ("w_ffn_up", "w_branch_rnn", "w_branch_att")
    (dqkv, dbias), got = _att_bwd(proj, att_saved, bias, o_att, lse, do_att,
                                  comm=_scatter_direct([grads[n] for n in behind_att_bwd]))
    received.update(zip(behind_att_bwd, got))
    drel = _bias_grad(dbias, buckets)
    (dxr, dcw_rnn, dcb_rnn, dwa, dba, dwx, dbx, dlam), got = _rnn_bwd(
        proj, rnn_saved, dh_rnn, cw_rnn, wa, wx, W["lru_lambda"],
        comm=_scatter_direct([grads[n] for n in behind_rnn_bwd]))
    received.update(zip(behind_rnn_bwd, got))
    gsmall = {"rel_bias": drel, "norm_mix_post": dg_post, "conv_rnn_b": dcb_rnn, "w_rg_a": dwa.astype(bf16),
              "b_rg_a": dba, "w_rg_x": dwx.astype(bf16), "b_rg_x": dbx, "lru_lambda": dlam,
              "norm_ffn_pre": dg_fpre, "norm_ffn_post": dg_fpost, "conv_ffn_b": dcb_ffn}
    dw_in_a, parts = _mm(hn, dqkv, "tn", bf16, "mm_dw_in_a", 1024, 1280, 1024,
                         comm=_gather_direct([_flat2(gsmall[n]) for n in gsmall]))
    parts = dict(zip(gsmall, parts))
    dw_in = jnp.concatenate([_mm(hn, dxr, "tn", bf16, "mm_dw_in_r", 1024, 1280, 1024), dw_in_a,
                             _mm(hn, dgates, "tn", bf16, "mm_dw_in_g", 1024, 1024, 2048)], axis=1)
    grads["w_in"] = _full_to_cols(_w_in_from_internal(dw_in))
    grads["conv_rnn_w"] = _full_to_cols(dcw_rnn.astype(bf16))
    behind_dhn = ("w_in", "conv_rnn_w")
    dhn, got = _mm_nt_sum([(dxr, w_in[:, :C_ATT], 1280), (dqkv, w_in[:, C_ATT:C_GATE], 1280),
                           (dgates, w_in[:, C_GATE:], 1024)], "mm_dhn", 1024, 1024,
                          comm=_scatter_direct([grads[n] for n in behind_dhn]))
    received.update(zip(behind_dhn, got))
    dx, dg_pre = _in_bwd(dh1, dhn, x, W["norm_mix_pre"])
    parts["norm_mix_pre"], = _run(_gather_two_level([dg_pre]), "ag_norm_mix_pre")
    parts = [parts[n] for n in _REPLICATED]

    out = {}
    d_arr = jnp.reshape(4 * x_idx + 2 * y_idx + c_idx, (1,)).astype(jnp.int32)
    for n in _SHARDED:
        res = _adamw_sharded(grads[n], received[n], d_arr, shard[n], inp["m_" + n][0], inp["v_" + n][0], n)
        out[n] = [r[None] for r in res]
    small = _adamw_replicated(parts, *[[_flat2(inp[p + n]) for n in _REPLICATED] for p in ("", "m_", "v_")])
    for a, n in enumerate(_REPLICATED):
        out[n] = [small[i][a].reshape(inp[n].shape) for i in range(4)]

    loss = lax.psum(loss_part[0, 0], ("x", "y", "c"))
    outs = [loss, dx.reshape(inp["x"].shape)]
    for i in range(4):
        outs.extend(out[n][i] for n in _WEIGHTS)
    return tuple(outs)


def kernel(x, rel_bias, norm_mix_pre, norm_mix_post, w_in, conv_rnn_w, conv_rnn_b, w_rg_a, b_rg_a, w_rg_x, b_rg_x, lru_lambda, w_branch_rnn, w_branch_att, w_out, norm_ffn_pre, norm_ffn_post, w_ffn_gate, w_ffn_up, conv_ffn_w, conv_ffn_b, w_ffn_down, loss_target, m_rel_bias, m_norm_mix_pre, m_norm_mix_post, m_w_in, m_conv_rnn_w, m_conv_rnn_b, m_w_rg_a, m_b_rg_a, m_w_rg_x, m_b_rg_x, m_lru_lambda, m_w_branch_rnn, m_w_branch_att, m_w_out, m_norm_ffn_pre, m_norm_ffn_post, m_w_ffn_gate, m_w_ffn_up, m_conv_ffn_w, m_conv_ffn_b, m_w_ffn_down, v_rel_bias, v_norm_mix_pre, v_norm_mix_post, v_w_in, v_conv_rnn_w, v_conv_rnn_b, v_w_rg_a, v_b_rg_a, v_w_rg_x, v_b_rg_x, v_lru_lambda, v_w_branch_rnn, v_w_branch_att, v_w_out, v_norm_ffn_pre, v_norm_ffn_post, v_w_ffn_gate, v_w_ffn_up, v_conv_ffn_w, v_conv_ffn_b, v_w_ffn_down):
    vals = locals()
    names = list(_IN_NAMES) + ["loss_target"] + ["m_" + n for n in _WEIGHTS] + ["v_" + n for n in _WEIGHTS]
    return _train_step({n: vals[n] for n in names})
```

```python
import functools
import math

import numpy as np
import jax
import jax.numpy as jnp
from jax import lax
from jax.experimental import pallas as pl
from jax.experimental.pallas import tpu as pltpu

f32, bf16 = jnp.float32, jnp.bfloat16
SDS = jax.ShapeDtypeStruct
MESH = pl.DeviceIdType.MESH
ANY = pl.BlockSpec(memory_space=pl.ANY)

D = 1024
SEQ = 2048
RNN_W = 1280
RNN_BLOCKS = 10
LANES = 128
SUBLANES = 8
RNN_CONV = 4
LRU_C = 8.0
HD = 128
KVH = 4
DILATIONS = (1, 4, 16)
NG = 3
ATT_BLK = 128
NBLK_SEQ = SEQ // ATT_BLK
STAT_LANE = 64
REL_BUCKETS = 32
REL_MAX_DIST = 2048
FFN_W = 3072
FFN_CONV = 3
EPS = 1e-6
IN_W = 5888
ATT_COLS = 5 * HD
C_ATT = RNN_W
C_GATE = RNN_W + KVH * ATT_COLS
NEG = -1e30

ADAM_LR, ADAM_B1, ADAM_B2, ADAM_EPS, ADAM_WD, ADAM_STEP = 0.001, 0.9, 0.999, 1e-08, 0.01, 10

VMEM_LIMIT_BYTES = 56 * 1024 * 1024
N_DEV = 8


def _params(sem=None):
    return pltpu.CompilerParams(dimension_semantics=sem, vmem_limit_bytes=VMEM_LIMIT_BYTES)


def _sigmoid(x):
    return 1.0 / (1.0 + jnp.exp(-x))


class _Comm:
    def __init__(self, inputs, out_shapes, sem_shapes, start, finish):
        self.inputs, self.out_shapes, self.sem_shapes = tuple(inputs), tuple(out_shapes), list(sem_shapes)
        self.start, self.finish = start, finish


def _call(body, args, *, name, grid, in_specs, out_specs, out_shape, scratch_shapes=(), semantics, comm=None):
    if comm is None:
        return pl.pallas_call(body, name=name, grid=grid, in_specs=list(in_specs), out_specs=tuple(out_specs),
                              out_shape=tuple(out_shape), scratch_shapes=list(scratch_shapes),
                              compiler_params=_params(semantics))(*args), ()
    n_in, n_out, n_scr = len(in_specs), len(out_shape), len(scratch_shapes)
    c_in, c_out = len(comm.inputs), len(comm.out_shapes)

    def fused(*refs):
        ins, refs = refs[:n_in], refs[n_in:]
        cin, refs = refs[:c_in], refs[c_in:]
        outs, refs = refs[:n_out], refs[n_out:]
        cout, refs = refs[:c_out], refs[c_out:]
        scr, csem = refs[:n_scr], refs[n_scr:]
        first = functools.reduce(jnp.logical_and, [pl.program_id(d) == 0 for d in range(len(grid))])
        last = functools.reduce(jnp.logical_and, [pl.program_id(d) == grid[d] - 1 for d in range(len(grid))])

        @pl.when(first)
        def _():
            comm.start(cin, cout, csem)

        body(*ins, *outs, *scr)

        @pl.when(last)
        def _():
            comm.finish(cin, cout, csem)

    res = pl.pallas_call(
        fused, name=name, grid=grid, in_specs=list(in_specs) + [ANY] * c_in,
        out_specs=tuple(out_specs) + (ANY,) * c_out, out_shape=tuple(out_shape) + comm.out_shapes,
        scratch_shapes=list(scratch_shapes) + comm.sem_shapes,
        compiler_params=_params(("arbitrary",) * len(grid)))(*args, *comm.inputs)
    return res[:n_out], res[n_out:]


_DIMS = {"nn": (((1,), (0,)), ((), ())), "nt": (((1,), (1,)), ((), ())), "tn": (((0,), (0,)), ((), ()))}


def _mm(a, b, mode, out_dtype, name, tm, tn, tk, cols_outer=False, comm=None):
    (a, a_c0, a_w), (b, b_c0, b_w) = [x if isinstance(x, tuple) else (x, 0, x.shape[1]) for x in (a, b)]
    if mode == "nn":
        (M, K), (K2, N) = (a.shape[0], a_w), (b.shape[0], b_w)
    elif mode == "nt":
        (M, K), (N, K2) = (a.shape[0], a_w), (b.shape[0], b_w)
    else:
        (K, M), (K2, N) = (a.shape[0], a_w), (b.shape[0], b_w)
    assert K == K2 and M % tm == 0 and N % tn == 0 and K % tk == 0, (name, a.shape, b.shape)
    a_tile, b_tile = (tm if mode == "tn" else tk), (tk if mode == "nt" else tn)
    assert a_c0 % a_tile == 0 and b_c0 % b_tile == 0, name
    a_off, b_off = a_c0 // a_tile, b_c0 // b_tile
    nk = K // tk

    def body(a_ref, b_ref, o_ref, *scratch):
        part = lax.dot_general(a_ref[...].astype(bf16), b_ref[...].astype(bf16), _DIMS[mode],
                               preferred_element_type=f32)
        if nk == 1:
            o_ref[...] = part.astype(o_ref.dtype)
        else:
            acc_ref, = scratch
            k = pl.program_id(2)

            @pl.when(k == 0)
            def _():
                acc_ref[...] = part

            @pl.when(k > 0)
            def _():
                acc_ref[...] += part

            @pl.when(k == nk - 1)
            def _():
                o_ref[...] = acc_ref[...].astype(o_ref.dtype)

    def ij(f):
        return (lambda j, i, k: f(i, j, k)) if cols_outer else f

    if mode == "tn":
        a_spec = pl.BlockSpec((tk, tm), ij(lambda i, j, k: (k, i + a_off)))
    else:
        a_spec = pl.BlockSpec((tm, tk), ij(lambda i, j, k: (i, k + a_off)))
    if mode == "nt":
        b_spec = pl.BlockSpec((tn, tk), ij(lambda i, j, k: (j, k + b_off)))
    else:
        b_spec = pl.BlockSpec((tk, tn), ij(lambda i, j, k: (k, j + b_off)))
    o_spec = pl.BlockSpec((tm, tn), ij(lambda i, j, k: (i, j)))
    grid = (N // tn, M // tm, nk) if cols_outer else (M // tm, N // tn, nk)
    (out,), extra = _call(
        body, (a, b), name=name, out_shape=(SDS((M, N), out_dtype),), grid=grid, in_specs=[a_spec, b_spec],
        out_specs=(o_spec,), scratch_shapes=[pltpu.VMEM((tm, tn), f32)] if nk > 1 else [],
        semantics=("parallel", "parallel", "arbitrary"), comm=comm)
    return out if comm is None else (out, extra)


def _mm_nt_sum(pairs, name, tm, tn, comm=None):
    M, N = pairs[0][0].shape[0], pairs[0][1].shape[0]
    nks = [a.shape[1] // tk for a, _, tk in pairs]
    starts = [sum(nks[:p]) for p in range(len(pairs))]
    nk = sum(nks)

    def body(*refs):
        o_ref, acc_ref = refs[-2], refs[-1]
        k = pl.program_id(2)
        for p in range(len(pairs)):
            def product(p=p):
                return lax.dot_general(refs[2 * p][...], refs[2 * p + 1][...], _DIMS["nt"], preferred_element_type=f32)

            if p == 0:
                @pl.when(k == 0)
                def _():
                    acc_ref[...] = product()

            @pl.when((k >= max(starts[p], 1)) & (k < starts[p] + nks[p]))
            def _():
                acc_ref[...] += product()

        @pl.when(k == nk - 1)
        def _():
            o_ref[...] = acc_ref[...].astype(bf16)

    in_specs, args = [], []
    for (a, b, tk), k0, n in zip(pairs, starts, nks):
        assert a.shape[1] == b.shape[1] and a.shape[1] % tk == 0 and a.dtype == b.dtype == bf16, name
        chunk = lambda k, k0=k0, n=n: jnp.clip(k - k0, 0, n - 1)
        in_specs += [pl.BlockSpec((tm, tk), lambda i, j, k, c=chunk: (i, c(k))),
                     pl.BlockSpec((tn, tk), lambda i, j, k, c=chunk: (j, c(k)))]
        args += [a, b]
    o_spec = pl.BlockSpec((tm, tn), lambda i, j, k: (i, j))
    (out,), extra = _call(
        body, args, name=name, out_shape=(SDS((M, N), bf16),), grid=(M // tm, N // tn, nk), in_specs=in_specs,
        out_specs=(o_spec,), scratch_shapes=[pltpu.VMEM((tm, tn), f32)],
        semantics=("parallel", "parallel", "arbitrary"), comm=comm)
    return out if comm is None else (out, extra)


def _mm_rows(pairs, mode, name, tm, epilogue, rows_in, vecs_in, rows_out, vecs_out, comm=None):
    M = pairs[0][0].shape[0]
    N = pairs[0][1].shape[1 if mode == "nn" else 0]
    nks = [a.shape[1] // tk for a, _, tk in pairs]
    starts = [sum(nks[:p]) for p in range(len(pairs))]
    nk = sum(nks)
    n_rows_in, n_vecs_in, n_rows_out = len(rows_in), len(vecs_in), len(rows_out)

    def body(*refs):
        pair_refs, refs = refs[:2 * len(pairs)], refs[2 * len(pairs):]
        rin, refs = refs[:n_rows_in], refs[n_rows_in:]
        vin, refs = refs[:n_vecs_in], refs[n_vecs_in:]
        rout, refs = refs[:n_rows_out], refs[n_rows_out:]
        vout, acc_ref = refs[:-1], refs[-1]
        i, k = pl.program_id(0), pl.program_id(1)
        for p in range(len(pairs)):
            def product(p=p):
                return lax.dot_general(pair_refs[2 * p][...], pair_refs[2 * p + 1][...], _DIMS[mode],
                                       preferred_element_type=f32)

            if p == 0:
                @pl.when(k == 0)
                def _():
                    acc_ref[...] = product()

            @pl.when((k >= max(starts[p], 1)) & (k < starts[p] + nks[p]))
            def _():
                acc_ref[...] += product()

        @pl.when(k == nk - 1)
        def _():
            res = epilogue(acc_ref[...], *[r[...] for r in rin], *[v[...] for v in vin])
            for ref, val in zip(rout, res[:n_rows_out]):
                ref[...] = val.astype(ref.dtype)
            for ref, val in zip(vout, res[n_rows_out:]):
                @pl.when(i == 0)
                def _(ref=ref, val=val):
                    ref[...] = val

                @pl.when(i > 0)
                def _(ref=ref, val=val):
                    ref[...] += val

    in_specs, args = [], []
    for (a, b, tk), k0, n in zip(pairs, starts, nks):
        assert a.shape[1] % tk == 0 and a.dtype == b.dtype == bf16, name
        chunk = lambda k, k0=k0, n=n: jnp.clip(k - k0, 0, n - 1)
        in_specs.append(pl.BlockSpec((tm, tk), lambda i, k, c=chunk: (i, c(k))))
        if mode == "nn":
            in_specs.append(pl.BlockSpec((tk, N), lambda i, k, c=chunk: (c(k), 0)))
        else:
            in_specs.append(pl.BlockSpec((N, tk), lambda i, k, c=chunk: (0, c(k))))
        args += [a, b]
    row = lambda: pl.BlockSpec((tm, N), lambda i, k: (i, 0))
    vec = lambda w: pl.BlockSpec((1, w), lambda i, k: (0, 0))
    in_specs += [row() for _ in rows_in] + [vec(v.shape[1]) for v in vecs_in]
    outs, extra = _call(
        body, (*args, *rows_in, *vecs_in), name=name,
        out_shape=tuple(SDS((M, N), dt) for dt in rows_out) + tuple(SDS((1, w), f32) for w in vecs_out),
        grid=(M // tm, nk), in_specs=in_specs,
        out_specs=tuple(row() for _ in rows_out) + tuple(vec(w) for w in vecs_out),
        scratch_shapes=[pltpu.VMEM((tm, N), f32)], semantics=("arbitrary", "arbitrary"), comm=comm)
    res = (outs[:n_rows_out], outs[n_rows_out:])
    return res if comm is None else (res, extra)


ROW_TILE = 512


def _rms_fwd(x, g):
    r = lax.rsqrt(jnp.mean(x * x, axis=-1, keepdims=True) + EPS)
    return x * r * g


def _rms_bwd(x, g, dy):
    r = lax.rsqrt(jnp.mean(x * x, axis=-1, keepdims=True) + EPS)
    xh = x * r
    dxh = dy * g
    dx = r * (dxh - xh * jnp.mean(dxh * xh, axis=-1, keepdims=True))
    return dx, jnp.sum(dy * xh, axis=0, keepdims=True)


def _acc_out(ref, val):
    @pl.when(pl.program_id(0) == 0)
    def _():
        ref[...] = val

    @pl.when(pl.program_id(0) > 0)
    def _():
        ref[...] += val


def _row_spec(width=D):
    return pl.BlockSpec((ROW_TILE, width), lambda i: (i, 0))


def _vec_spec(width=D):
    return pl.BlockSpec((1, width), lambda i: (0, 0))


def _norm_in(x, g, comm=None):
    def body(x_ref, g_ref, o_ref):
        o_ref[...] = _rms_fwd(x_ref[...], g_ref[...]).astype(bf16)

    T = x.shape[0]
    (hn,), extra = _call(body, (x, g), name="norm_in", out_shape=(SDS((T, D), bf16),), grid=(T // ROW_TILE,),
                         in_specs=[_row_spec(), _vec_spec()], out_specs=(_row_spec(),), semantics=("parallel",),
                         comm=comm)
    return hn, extra


def _mid_bwd(dy, dhn2, h1, g_fpre, mix, g_post):
    def body(dy_ref, dhn2_ref, h1_ref, gf_ref, mix_ref, gp_ref, dh1_ref, dmix_ref, dgf_ref, dgp_ref):
        d1, dgf = _rms_bwd(h1_ref[...], gf_ref[...], dhn2_ref[...].astype(f32))
        dh1 = dy_ref[...] + d1
        dh1_ref[...] = dh1
        dmix, dgp = _rms_bwd(mix_ref[...], gp_ref[...], dh1)
        dmix_ref[...] = dmix.astype(bf16)
        _acc_out(dgf_ref, dgf)
        _acc_out(dgp_ref, dgp)

    T = dy.shape[0]
    return pl.pallas_call(
        body, name="mid_bwd", out_shape=(SDS((T, D), f32), SDS((T, D), bf16), SDS((1, D), f32), SDS((1, D), f32)),
        grid=(T // ROW_TILE,),
        in_specs=[_row_spec(), _row_spec(), _row_spec(), _vec_spec(), _row_spec(), _vec_spec()],
        out_specs=(_row_spec(), _row_spec(), _vec_spec(), _vec_spec()),
        compiler_params=_params(("arbitrary",)))(dy, dhn2, h1, g_fpre, mix, g_post)


def _in_bwd(dh1, dhn, x, g_pre):
    def body(dh1_ref, dhn_ref, x_ref, g_ref, dx_ref, dg_ref):
        d, dg = _rms_bwd(x_ref[...], g_ref[...], dhn_ref[...].astype(f32))
        dx_ref[...] = dh1_ref[...] + d
        _acc_out(dg_ref, dg)

    T = x.shape[0]
    return pl.pallas_call(
        body, name="in_bwd", out_shape=(SDS((T, D), f32), SDS((1, D), f32)), grid=(T // ROW_TILE,),
        in_specs=[_row_spec(), _row_spec(), _row_spec(), _vec_spec()], out_specs=(_row_spec(), _vec_spec()),
        compiler_params=_params(("arbitrary",)))(dh1, dhn, x, g_pre)


def _mid_fwd_rows(mix, x, g_post, g_fpre):
    h1 = x + _rms_fwd(mix, g_post)
    return mix, h1, _rms_fwd(h1, g_fpre)


def _final_rows(ff, h1, target, g_fpost):
    e = h1 + _rms_fwd(ff, g_fpost) - target
    part = jnp.sum(jnp.sum(e * e, axis=1, keepdims=True), axis=0, keepdims=True) * (0.5 / D)
    dy = e * (1.0 / D)
    dff, dg = _rms_bwd(ff, g_fpost, dy)
    return dy, dff, part, dg


def _shift_dn(x, d, row, fill=0.0):
    if d == 0:
        return x
    y = pltpu.roll(x, d, 0)
    head = jnp.where(row[:SUBLANES] >= d, y[:SUBLANES], fill)
    return jnp.concatenate([head, y[SUBLANES:]], axis=0)


def _shift_up(x, d, row, fill=0.0):
    if d == 0:
        return x
    n = x.shape[0]
    y = pltpu.roll(x, n - d, 0)
    tail = jnp.where(row[:SUBLANES] < SUBLANES - d, y[n - SUBLANES:], fill)
    return jnp.concatenate([y[:n - SUBLANES], tail], axis=0)


def _conv_fwd(x, w_ref, b, row):
    K = w_ref.shape[0]
    y = b
    for k in range(K):
        y = y + w_ref[k:k + 1, :] * _shift_dn(x, K - 1 - k, row)
    return y


def _conv_bwd(x, w_ref, dy, row):
    K = w_ref.shape[0]
    dx = jnp.zeros_like(dy)
    dws = []
    for k in range(K):
        dx = dx + w_ref[k:k + 1, :] * _shift_up(dy, K - 1 - k, row)
        dws.append(jnp.sum(dy * _shift_dn(x, K - 1 - k, row), axis=0, keepdims=True))
    return dx, dws, jnp.sum(dy, axis=0, keepdims=True)


def _scan_fwd(a, u, row):
    n = a.shape[0]
    d = 1
    while d < n:
        last = 2 * d >= n
        if d < SUBLANES:
            u = u + a * _shift_dn(u, d, row)
            if not last:
                a = a * _shift_dn(a, d, row, fill=1.0)
        else:
            u = jnp.concatenate([u[:d], u[d:] + a[d:] * u[:n - d]], axis=0)
            if not last:
                a = jnp.concatenate([a[:d], a[d:] * a[:n - d]], axis=0)
        d *= 2
    return u


def _scan_bwd(b, u, row):
    n = b.shape[0]
    d = 1
    while d < n:
        last = 2 * d >= n
        if d < SUBLANES:
            u = u + b * _shift_up(u, d, row)
            if not last:
                b = b * _shift_up(b, d, row, fill=1.0)
        else:
            u = jnp.concatenate([u[:n - d] + b[:n - d] * u[d:], u[n - d:]], axis=0)
            if not last:
                b = jnp.concatenate([b[:n - d] * b[d:], b[n - d:]], axis=0)
        d *= 2
    return u


def _neg_expm1(z):
    series = -z * (1.0 + z * (0.5 + z * (1.0 / 6.0 + z * (1.0 / 24.0 + z * (1.0 / 120.0)))))
    return jnp.where(z > -0.1, series, 1.0 - jnp.exp(z))


def _softplus_neg(lam):
    z = -lam
    return jnp.maximum(z, 0.0) + jnp.log(1.0 + jnp.exp(-jnp.abs(z)))


def _rnn_specs(B):
    blk = lambda: pl.BlockSpec((SEQ, LANES), lambda b, n: (b, n))
    return dict(
        act=blk,
        convw=pl.BlockSpec((RNN_CONV, LANES), lambda b, n: (0, n)),
        vec=lambda: pl.BlockSpec((1, LANES), lambda b, n: (0, n)),
        gate=lambda: pl.BlockSpec((None, LANES, LANES), lambda b, n: (n, 0, 0)),
    )


def _rnn_fwd(proj, cw, cb, wa, ba, wx, bx, lam, comm=None):
    T = proj.shape[0]
    B = T // SEQ

    def body(x_ref, cw_ref, cb_ref, wa_ref, ba_ref, wx_ref, bx_ref, lam_ref, h_ref, xc_ref, r_ref, i_ref, a_ref, s_ref):
        row = lax.broadcasted_iota(jnp.int32, (SEQ, LANES), 0)
        xc = _conv_fwd(x_ref[...], cw_ref, cb_ref[...], row)
        xcb = xc.astype(bf16)
        r = _sigmoid(jnp.dot(xcb, wa_ref[...].astype(bf16), preferred_element_type=f32) + ba_ref[...])
        i = _sigmoid(jnp.dot(xcb, wx_ref[...].astype(bf16), preferred_element_type=f32) + bx_ref[...])
        log_a = (-LRU_C * _softplus_neg(lam_ref[...])) * r
        a = jnp.exp(log_a)
        s = jnp.sqrt(_neg_expm1(2.0 * log_a))
        xc_ref[...], r_ref[...], i_ref[...], a_ref[...], s_ref[...] = xc, r, i, a, s
        h_ref[...] = _scan_fwd(a, s * (i * xc), row)

    sp_ = _rnn_specs(B)
    return _call(
        body, (proj, cw, cb, wa, ba, wx, bx, lam), name="rnn_fwd", out_shape=(SDS((T, RNN_W), f32),) * 6,
        grid=(B, RNN_BLOCKS),
        in_specs=[sp_["act"](), sp_["convw"], sp_["vec"](), sp_["gate"](), sp_["vec"](), sp_["gate"](),
                  sp_["vec"](), sp_["vec"]()],
        out_specs=tuple(sp_["act"]() for _ in range(6)), semantics=("parallel", "parallel"), comm=comm)


def _rnn_bwd(proj, saved, dh, cw, wa, wx, lam, comm=None):
    T = proj.shape[0]
    B = T // SEQ

    def body(x_ref, h_ref, xc_ref, r_ref, i_ref, a_ref, s_ref, dh_ref, cw_ref, wa_ref, wx_ref, lam_ref,
             dx_ref, dcw_ref, dcb_ref, dwa_ref, dba_ref, dwx_ref, dbx_ref, dlam_ref):
        row = lax.broadcasted_iota(jnp.int32, (SEQ, LANES), 0)
        xr = x_ref[...]
        wa, wx, lam = wa_ref[...], wx_ref[...], lam_ref[...]
        xc, r, i, a, s = xc_ref[...], r_ref[...], i_ref[...], a_ref[...], s_ref[...]
        xcb = xc.astype(bf16)
        sp = _softplus_neg(lam)
        hprev = _shift_dn(h_ref[...], 1, row)
        g = _scan_bwd(_shift_up(a, 1, row), dh_ref[...].astype(f32), row)
        da = g * hprev
        ds = g * (i * xc)
        di = g * (s * xc)
        dxc = g * (s * i)
        dla = da * a - ds * (a * a) / s
        dr = dla * (-LRU_C * sp)
        dsp = jnp.sum(dla * (-LRU_C * r), axis=0, keepdims=True)
        dlam = -dsp * _sigmoid(-lam)
        dga = dr * r * (1.0 - r)
        dgx = di * i * (1.0 - i)
        dgab, dgxb = dga.astype(bf16), dgx.astype(bf16)
        dwa = lax.dot_general(xcb, dgab, _DIMS["tn"], preferred_element_type=f32)
        dwx = lax.dot_general(xcb, dgxb, _DIMS["tn"], preferred_element_type=f32)
        dxc = dxc + lax.dot_general(dgab, wa.astype(bf16), _DIMS["nt"], preferred_element_type=f32)
        dxc = dxc + lax.dot_general(dgxb, wx.astype(bf16), _DIMS["nt"], preferred_element_type=f32)
        dx, dws, db = _conv_bwd(xr, cw_ref, dxc, row)
        dx_ref[...] = dx.astype(bf16)
        first = pl.program_id(1) == 0

        def acc(ref, val):
            @pl.when(first)
            def _():
                ref[...] = val

            @pl.when(jnp.logical_not(first))
            def _():
                ref[...] += val

        for k in range(RNN_CONV):
            acc(dcw_ref.at[k:k + 1, :], dws[k])
        acc(dcb_ref, db)
        acc(dwa_ref, dwa)
        acc(dba_ref, jnp.sum(dga, axis=0, keepdims=True))
        acc(dwx_ref, dwx)
        acc(dbx_ref, jnp.sum(dgx, axis=0, keepdims=True))
        acc(dlam_ref, dlam)

    blk = lambda: pl.BlockSpec((SEQ, LANES), lambda n, b: (b, n))
    convw = lambda: pl.BlockSpec((RNN_CONV, LANES), lambda n, b: (0, n))
    vec = lambda: pl.BlockSpec((1, LANES), lambda n, b: (0, n))
    gate = lambda: pl.BlockSpec((None, LANES, LANES), lambda n, b: (n, 0, 0))
    vshape = SDS((1, RNN_W), f32)
    gshape = SDS((RNN_BLOCKS, LANES, LANES), f32)
    return _call(
        body, (proj, *saved, dh, cw, wa, wx, lam), name="rnn_bwd",
        out_shape=(SDS((T, RNN_W), bf16), SDS((RNN_CONV, RNN_W), f32), vshape, gshape, vshape, gshape, vshape, vshape),
        grid=(RNN_BLOCKS, B),
        in_specs=[blk() for _ in range(8)] + [convw(), gate(), gate(), vec()],
        out_specs=(blk(), convw(), vec(), gate(), vec(), gate(), vec(), vec()),
        semantics=("parallel", "arbitrary"), comm=comm)


def _t5_bucket(dist):
    max_exact = REL_BUCKETS // 2
    d = np.maximum(dist, 1).astype(np.float32)
    large = max_exact + np.log(d / max_exact) / math.log(REL_MAX_DIST / max_exact) * (REL_BUCKETS - max_exact)
    large = np.minimum(large.astype(np.int32), REL_BUCKETS - 1)
    return np.where(dist < max_exact, dist, large).astype(np.int32)


def _bucket_maps():
    qi = np.arange(ATT_BLK)[:, None]
    kj = np.arange(2 * ATT_BLK)[None, :]
    delta = ATT_BLK + qi - kj
    valid = (delta >= 0) & (delta <= ATT_BLK)
    maps = [np.where(valid, _t5_bucket(np.maximum(delta, 0) * r), -1) for r in DILATIONS]
    return np.stack(maps).astype(np.int32)


def _bias_tables(rel_bias, buckets):
    def body(rb_ref, bk_ref, o_ref):
        for g in range(NG):
            bk = bk_ref[g]
            for h in range(KVH):
                acc = jnp.full(bk.shape, NEG, f32)
                for b in range(REL_BUCKETS):
                    acc = jnp.where(bk == b, rb_ref[b, g * KVH + h], acc)
                o_ref[h, g] = acc

    return pl.pallas_call(
        body, name="bias_tables", out_shape=SDS((KVH, NG, ATT_BLK, 2 * ATT_BLK), f32),
        in_specs=[pl.BlockSpec(memory_space=pltpu.SMEM), pl.BlockSpec(memory_space=pltpu.VMEM)],
        out_specs=pl.BlockSpec(memory_space=pltpu.VMEM), compiler_params=_params())(rel_bias, buckets)


def _bias_grad(dbias, buckets):
    def body(db_ref, bk_ref, o_ref):
        rr = lax.broadcasted_iota(jnp.int32, (REL_BUCKETS, NG * KVH), 0)
        cc = lax.broadcasted_iota(jnp.int32, (REL_BUCKETS, NG * KVH), 1)
        out = jnp.zeros((REL_BUCKETS, NG * KVH), f32)
        for g in range(NG):
            bk = bk_ref[g]
            for h in range(KVH):
                d = db_ref[h, g]
                for b in range(REL_BUCKETS):
                    m = jnp.where(bk == b, d, 0.0)
                    s = jnp.sum(jnp.sum(m, axis=1, keepdims=True), axis=0, keepdims=True)
                    out = jnp.where((rr == b) & (cc == g * KVH + h), s, out)
        o_ref[...] = out

    return pl.pallas_call(body, name="bias_grad", out_shape=SDS((REL_BUCKETS, NG * KVH), f32),
                          compiler_params=_params())(dbias, buckets)


def _to_sub(dst_ref, src_ref, r, dtype, offset=0):
    M = SEQ // r
    for c in range(r):
        if r == 1:
            v = src_ref[...]
        else:
            v = src_ref[pl.ds(c, M, stride=r), :]
        dst_ref[pl.ds(offset + c * M, M), :] = v.astype(dtype)


def _from_sub(dst_ref, src_ref, r, accumulate=False, offset=0):
    M = SEQ // r
    for c in range(r):
        v = src_ref[pl.ds(offset + c * M, M), :]
        idx = slice(None) if r == 1 else pl.ds(c, M, stride=r)
        if accumulate:
            dst_ref[idx, :] = dst_ref[idx, :] + v
        else:
            dst_ref[idx, :] = v


_COL = lambda k: slice(k * HD, (k + 1) * HD)
SCALE = HD ** -0.5


def _qkv_spec(k, bh):
    def index(*ids):
        b, h = bh(*ids)
        return (b, C_ATT // HD + 5 * h + k)

    return pl.BlockSpec((SEQ, HD), index)


def _key_window(bias_ref, g, nb):
    if nb == 1:
        bias_own = bias_ref[g, :, ATT_BLK:2 * ATT_BLK]
        return lambda j: (pl.ds(pl.multiple_of((j + 1) * ATT_BLK, ATT_BLK), ATT_BLK), bias_own)
    bias_g = bias_ref[g]
    col = lax.broadcasted_iota(jnp.int32, bias_g.shape, 1)
    bias_first = jnp.where(col >= ATT_BLK, bias_g, NEG)
    return lambda j: (pl.ds(pl.multiple_of(j * ATT_BLK, ATT_BLK), 2 * ATT_BLK),
                      jnp.where(j % nb != 0, bias_g, bias_first))


def _att_fwd(proj, bias, comm=None):
    T = proj.shape[0]
    B = T // SEQ

    def body(q0_ref, q1_ref, q2_ref, k_ref, v_ref, bias_ref, o_ref, lse_ref, *rest):
        saved, (qp, kp, vp, kt, op, lp, og, lg) = rest[:3 * (NG - 1)], rest[3 * (NG - 1):]
        q_refs = (q0_ref, q1_ref, q2_ref)
        kp[0:ATT_BLK, :] = jnp.zeros((ATT_BLK, HD), bf16)
        vp[0:ATT_BLK, :] = jnp.zeros((ATT_BLK, HD), bf16)
        for g, r in enumerate(DILATIONS):
            nb = NBLK_SEQ // r
            _to_sub(qp, q_refs[g], r, bf16)
            _to_sub(kp, k_ref, r, bf16, offset=ATT_BLK)
            _to_sub(vp, v_ref, r, bf16, offset=ATT_BLK)
            if r > 1:
                sq, sk, sv = saved[3 * (g - 1):3 * g]
                sq[...], sk[...], sv[...] = qp[...], kp[ATT_BLK:, :], vp[ATT_BLK:, :]
            kt[...] = kp[...].T
            keys = _key_window(bias_ref, g, nb)

            def step(j, carry):
                cur = pl.ds(pl.multiple_of(j * ATT_BLK, ATT_BLK), ATT_BLK)
                win, bias_j = keys(j)
                s = jnp.dot(qp[cur, :], kt[:, win], preferred_element_type=f32) * SCALE + bias_j
                m = jnp.max(s, axis=-1, keepdims=True)
                p = jnp.exp(s - m)
                den = jnp.sum(p, axis=-1, keepdims=True)
                o = jnp.dot(p.astype(bf16), vp[win, :], preferred_element_type=f32)
                op[cur, :] = o / den
                lp[cur, :] = jnp.broadcast_to(m + jnp.log(den), (ATT_BLK, HD))
                return carry

            lax.fori_loop(0, NBLK_SEQ, step, 0, unroll=NBLK_SEQ)
            _from_sub(og.at[g], op, r)
            _from_sub(lg.at[g], lp, r)
        l0, l1, l2 = lg[0], lg[1], lg[2]
        mx = jnp.maximum(jnp.maximum(l0, l1), l2)
        e0, e1, e2 = jnp.exp(l0 - mx), jnp.exp(l1 - mx), jnp.exp(l2 - mx)
        den = e0 + e1 + e2
        o_ref[...] = (e0 * og[0] + e1 * og[1] + e2 * og[2]) / den
        lse_ref[...] = mx + jnp.log(den)

    return _call(
        body, (proj, proj, proj, proj, proj, bias), name="att_fwd",
        out_shape=(SDS((T, KVH * HD), f32), SDS((KVH, T, HD), f32)) + (SDS((T, KVH * HD), bf16),) * (3 * (NG - 1)),
        grid=(B, KVH),
        in_specs=[_qkv_spec(k, lambda b, h: (b, h)) for k in range(5)]
                 + [pl.BlockSpec((None, NG, ATT_BLK, 2 * ATT_BLK), lambda b, h: (h, 0, 0, 0))],
        out_specs=(pl.BlockSpec((SEQ, HD), lambda b, h: (b, h)),
                   pl.BlockSpec((None, SEQ, HD), lambda b, h: (h, b, 0)))
                  + tuple(pl.BlockSpec((SEQ, HD), lambda b, h: (b, h)) for _ in range(3 * (NG - 1))),
        scratch_shapes=[pltpu.VMEM((SEQ, HD), bf16)] + [pltpu.VMEM((SEQ + ATT_BLK, HD), bf16)] * 2
                       + [pltpu.VMEM((HD, SEQ + ATT_BLK), bf16)]
                       + [pltpu.VMEM((SEQ, HD), f32)] * 2 + [pltpu.VMEM((NG, SEQ, HD), f32)] * 2,
        semantics=("parallel", "parallel"), comm=comm)


def _att_bwd(proj, saved, bias, o, lse, do, comm=None):
    T = proj.shape[0]
    B = T // SEQ
    n_saved = 3 * (NG - 1)

    def body(q0_ref, k_ref, v_ref, *rest):
        saved_refs, rest = rest[:n_saved], rest[n_saved:]
        (bias_ref, o_ref, lse_ref, do_ref, dx_ref, db_ref,
         qp, kp, vp, dop, qt, kt, vt, dot, lp, dqp, dkt, dvt, dln, nat, dkn, dvn) = rest
        first = pl.program_id(1) == 0

        @pl.when(first)
        def _():
            db_ref[...] = jnp.zeros_like(db_ref)

        lane = lax.broadcasted_iota(jnp.int32, (SEQ, HD), 1)
        dln[...] = jnp.where(lane < STAT_LANE, lse_ref[...],
                             jnp.sum(do_ref[...] * o_ref[...], axis=-1, keepdims=True))
        dkn[...] = jnp.zeros_like(dkn)
        dvn[...] = jnp.zeros_like(dvn)
        kp[0:ATT_BLK, :] = jnp.zeros((ATT_BLK, HD), bf16)
        vp[0:ATT_BLK, :] = jnp.zeros((ATT_BLK, HD), bf16)
        for g, r in enumerate(DILATIONS):
            nb = NBLK_SEQ // r
            if r == 1:
                qp[...] = q0_ref[...].astype(bf16)
                kp[ATT_BLK:, :] = k_ref[...].astype(bf16)
                vp[ATT_BLK:, :] = v_ref[...].astype(bf16)
            else:
                sq, sk, sv = saved_refs[3 * (g - 1):3 * g]
                qp[...], kp[ATT_BLK:, :], vp[ATT_BLK:, :] = sq[...], sk[...], sv[...]
            _to_sub(dop, do_ref, r, bf16)
            _to_sub(lp, dln, r, f32)
            qt[...], kt[...], vt[...], dot[...] = qp[...].T, kp[...].T, vp[...].T, dop[...].T
            dkt[...] = jnp.zeros_like(dkt)
            dvt[...] = jnp.zeros_like(dvt)
            keys = _key_window(bias_ref, g, nb)
            db_cols = slice(ATT_BLK, 2 * ATT_BLK) if nb == 1 else slice(None)

            def step(j, carry):
                cur = pl.ds(pl.multiple_of(j * ATT_BLK, ATT_BLK), ATT_BLK)
                win, bias_j = keys(j)
                s = jnp.dot(qp[cur, :], kt[:, win], preferred_element_type=f32) * SCALE + bias_j
                p = jnp.exp(s - lp[cur, 0:1])
                dp = jnp.dot(dop[cur, :], vt[:, win], preferred_element_type=f32)
                ds = p * (dp - lp[cur, STAT_LANE:STAT_LANE + 1])
                db_ref[g, :, db_cols] += ds
                dsb, pb = ds.astype(bf16), p.astype(bf16)
                dqp[cur, :] = jnp.dot(dsb, kp[win, :], preferred_element_type=f32) * SCALE
                dkt[:, win] += jnp.dot(qt[:, cur], dsb, preferred_element_type=f32) * SCALE
                dvt[:, win] += jnp.dot(dot[:, cur], pb, preferred_element_type=f32)
                return carry

            lax.fori_loop(0, NBLK_SEQ, step, 0, unroll=NBLK_SEQ)
            _from_sub(nat, dqp, r)
            dx_ref[:, _COL(g)] = nat[...].astype(bf16)
            dqp[...] = dkt[:, ATT_BLK:].T
            _from_sub(dkn, dqp, r, accumulate=True)
            dqp[...] = dvt[:, ATT_BLK:].T
            _from_sub(dvn, dqp, r, accumulate=True)
        dx_ref[:, _COL(3)] = dkn[...].astype(bf16)
        dx_ref[:, _COL(4)] = dvn[...].astype(bf16)

    blk = lambda: pl.BlockSpec((SEQ, HD), lambda h, b: (b, h))
    bias_spec = lambda: pl.BlockSpec((None, NG, ATT_BLK, 2 * ATT_BLK), lambda h, b: (h, 0, 0, 0))
    pad = lambda dtype: pltpu.VMEM((SEQ + ATT_BLK, HD), dtype)
    pad_t = lambda dtype: pltpu.VMEM((HD, SEQ + ATT_BLK), dtype)
    seq_t = pltpu.VMEM((HD, SEQ), bf16)
    return _call(
        body, (proj, proj, proj, *saved, bias, o, lse, do), name="att_bwd",
        out_shape=(SDS((T, KVH * ATT_COLS), bf16), SDS((KVH, NG, ATT_BLK, 2 * ATT_BLK), f32)), grid=(KVH, B),
        in_specs=[_qkv_spec(k, lambda h, b: (b, h)) for k in (0, 3, 4)] + [blk() for _ in range(n_saved)]
                 + [bias_spec(), blk(), pl.BlockSpec((None, SEQ, HD), lambda h, b: (h, b, 0)), blk()],
        out_specs=(pl.BlockSpec((SEQ, ATT_COLS), lambda h, b: (b, h)), bias_spec()),
        scratch_shapes=[pltpu.VMEM((SEQ, HD), bf16), pad(bf16), pad(bf16), pltpu.VMEM((SEQ, HD), bf16),
                        seq_t, pad_t(bf16), pad_t(bf16), seq_t]
                       + [pltpu.VMEM((SEQ, HD), f32)] * 2 + [pad_t(f32)] * 2 + [pltpu.VMEM((SEQ, HD), f32)] * 4,
        semantics=("parallel", "arbitrary"), comm=comm)


MERGE_ROWS, MERGE_COLS = 1024, 512


def _merge_fwd(gates, pr, pa):
    def body(gr_ref, ga_ref, pr_ref, pa_ref, o_ref):
        o_ref[...] = (_sigmoid(gr_ref[...].astype(f32)) * pr_ref[...].astype(f32)
                      + _sigmoid(ga_ref[...].astype(f32)) * pa_ref[...].astype(f32)).astype(bf16)

    T = gates.shape[0]
    cols = lambda off: pl.BlockSpec((MERGE_ROWS, MERGE_COLS), lambda i, j: (i, off + j))
    return pl.pallas_call(body, name="merge_fwd", out_shape=SDS((T, D), bf16),
                          grid=(T // MERGE_ROWS, D // MERGE_COLS),
                          in_specs=[cols(0), cols(D // MERGE_COLS), cols(0), cols(0)], out_specs=cols(0),
                          compiler_params=_params(("parallel", "parallel")))(gates, gates, pr, pa)


def _merge_bwd(gates, pr, pa, dm):
    nj = D // MERGE_COLS

    def body(g_ref, pr_ref, pa_ref, dm_ref, dp_ref, dg_ref):
        dm_ = dm_ref[...].astype(f32)
        s = _sigmoid(g_ref[...].astype(f32))
        p = jnp.where(pl.program_id(1) < nj, pr_ref[...], pa_ref[...]).astype(f32)
        dp_ref[...] = (dm_ * s).astype(bf16)
        dg_ref[...] = (dm_ * p * s * (1.0 - s)).astype(bf16)

    T = gates.shape[0]
    blk = (MERGE_ROWS, MERGE_COLS)
    wrap = pl.BlockSpec(blk, lambda i, j: (i, j % nj))
    pr_spec = pl.BlockSpec(blk, lambda i, j: (i, jnp.minimum(j, nj - 1)))
    pa_spec = pl.BlockSpec(blk, lambda i, j: (i, jnp.maximum(j - nj, 0)))
    out = pl.BlockSpec(blk, lambda i, j: (i, j))
    return pl.pallas_call(
        body, name="merge_bwd", out_shape=(SDS((T, 2 * D), bf16), SDS((T, 2 * D), bf16)),
        grid=(T // MERGE_ROWS, 2 * nj),
        in_specs=[out, pr_spec, pa_spec, wrap], out_specs=(out, out),
        compiler_params=_params(("parallel", "parallel")))(gates, pr, pa, dm)


FFN_COLS = 256
GELU_C = math.sqrt(2.0 / math.pi)
GELU_A = 0.044715


def _gelu_parts(x):
    q = x * x
    t = jnp.tanh(x * (GELU_C + (GELU_C * GELU_A) * q))
    h = 0.5 + 0.5 * t
    return x * h, h * (1.0 + x * (1.0 - t) * (GELU_C + (3.0 * GELU_C * GELU_A) * q))


def _ffn_act_fwd(gpre, up, cw, cb):
    def body(g_ref, u_ref, cw_ref, cb_ref, o_ref):
        row = lax.broadcasted_iota(jnp.int32, (SEQ, FFN_COLS), 0)
        gate = _conv_fwd(g_ref[...].astype(f32), cw_ref, cb_ref[...], row)
        o_ref[...] = (_gelu_parts(gate)[0] * u_ref[...].astype(f32)).astype(bf16)

    T = gpre.shape[0]
    blk = lambda: pl.BlockSpec((SEQ, FFN_COLS), lambda b, j: (b, j))
    return pl.pallas_call(
        body, name="ffn_act_fwd", out_shape=SDS((T, FFN_W), bf16), grid=(T // SEQ, FFN_W // FFN_COLS),
        in_specs=[blk(), blk(), pl.BlockSpec((FFN_CONV, FFN_COLS), lambda b, j: (0, j)),
                  pl.BlockSpec((1, FFN_COLS), lambda b, j: (0, j))],
        out_specs=blk(), compiler_params=_params(("parallel", "parallel")))(gpre, up, cw, cb)


def _ffn_act_bwd(gpre, up, cw, cb, dact):
    def body(g_ref, u_ref, cw_ref, cb_ref, da_ref, dg_ref, du_ref, dcw_ref, dcb_ref):
        row = lax.broadcasted_iota(jnp.int32, (SEQ, FFN_COLS), 0)
        gp = g_ref[...].astype(f32)
        gate = _conv_fwd(gp, cw_ref, cb_ref[...], row)
        gel, dgel = _gelu_parts(gate)
        da = da_ref[...].astype(f32)
        du_ref[...] = (da * gel).astype(bf16)
        dgate = da * u_ref[...].astype(f32) * dgel
        dx, dws, db = _conv_bwd(gp, cw_ref, dgate, row)
        dg_ref[...] = dx.astype(bf16)
        first = pl.program_id(1) == 0

        def acc(ref, val):
            @pl.when(first)
            def _():
                ref[...] = val

            @pl.when(jnp.logical_not(first))
            def _():
                ref[...] += val

        for k in range(FFN_CONV):
            acc(dcw_ref.at[k:k + 1, :], dws[k])
        acc(dcb_ref, db)

    T = gpre.shape[0]
    blk = lambda: pl.BlockSpec((SEQ, FFN_COLS), lambda j, b: (b, j))
    cws = lambda: pl.BlockSpec((FFN_CONV, FFN_COLS), lambda j, b: (0, j))
    cbs = lambda: pl.BlockSpec((1, FFN_COLS), lambda j, b: (0, j))
    return pl.pallas_call(
        body, name="ffn_act_bwd",
        out_shape=(SDS((T, FFN_W), bf16), SDS((T, FFN_W), bf16), SDS((FFN_CONV, FFN_W), f32), SDS((1, FFN_W), f32)),
        grid=(FFN_W // FFN_COLS, T // SEQ),
        in_specs=[blk(), blk(), cws(), cbs(), blk()], out_specs=(blk(), blk(), cws(), cbs()),
        compiler_params=_params(("parallel", "arbitrary")))(gpre, up, cw, cb, dact)


def _coords():
    return lax.axis_index("x"), lax.axis_index("y"), lax.axis_index("c")


def _dev_index(dev):
    return 4 * dev[0] + 2 * dev[1] + dev[2]


def _dma_sems(n):
    return [pltpu.SemaphoreType.DMA((n,)), pltpu.SemaphoreType.DMA((n,))]


def _gather_two_level(arrays):
    n = len(arrays)

    def plan(ins, outs, sems):
        send_sems, recv_sems, local_sems = sems
        x, y, c = _coords()
        me, sibling = (x, y, c), (x, y, 1 - c)
        chips = [(1 - x, y), (x, 1 - y), (1 - x, 1 - y)]

        def copy(a, k, block, to, own=False):
            dst = outs[a].at[_dev_index(block)]
            return pltpu.make_async_remote_copy(
                src_ref=ins[a] if own else dst, dst_ref=dst, send_sem=send_sems.at[7 * a + k],
                recv_sem=recv_sems.at[7 * a + k], device_id=to, device_id_type=MESH)

        mine = [pltpu.make_async_copy(ins[a], outs[a].at[_dev_index(me)], local_sems.at[a]) for a in range(n)]
        first = [copy(a, 0, me, sibling, own=True) for a in range(n)]
        first += [copy(a, 1 + j, me, (*chip, c), own=True) for a in range(n) for j, chip in enumerate(chips)]
        passed = [[copy(a, 4 + j, (*chip, c), sibling) for a in range(n)] for j, chip in enumerate(chips)]
        arrive_ici = [[copy(a, 1 + j, (*chip, c), me) for a in range(n)] for j, chip in enumerate(chips)]
        arrive_d2d = [copy(a, 0, sibling, me) for a in range(n)]
        arrive_d2d += [copy(a, 4 + j, (*chip, 1 - c), me) for a in range(n) for j, chip in enumerate(chips)]
        return mine, first, passed, arrive_ici, arrive_d2d

    def start(ins, outs, sems):
        mine, first, _, _, _ = plan(ins, outs, sems)
        for cp in mine + first:
            cp.start()

    def finish(ins, outs, sems):
        mine, first, passed, arrive_ici, arrive_d2d = plan(ins, outs, sems)
        for j in range(3):
            for cp in arrive_ici[j]:
                cp.wait_recv()
            for cp in passed[j]:
                cp.start()
        for cp in arrive_d2d:
            cp.wait_recv()
        for cp in first + [cp for group in passed for cp in group]:
            cp.wait_send()
        for cp in mine:
            cp.wait()

    return _Comm(arrays, [SDS((N_DEV,) + a.shape, a.dtype) for a in arrays],
                 _dma_sems(7 * n) + [pltpu.SemaphoreType.DMA((n,))], start, finish)


def _gather_direct(arrays):
    n = len(arrays)

    def plan(ins, outs, sems):
        send_sems, recv_sems, local_sems = sems
        x, y, c = _coords()
        me = (x, y, c)
        mine = [pltpu.make_async_copy(ins[a], outs[a].at[_dev_index(me)], local_sems.at[a]) for a in range(n)]
        sends, arrivals = [], []
        for a in range(n):
            for k in range(1, N_DEV):
                peer = (1 - x if k & 4 else x, 1 - y if k & 2 else y, 1 - c if k & 1 else c)
                s = 7 * a + k - 1
                for slot, out in ((me, sends), (peer, arrivals)):
                    out.append(pltpu.make_async_remote_copy(
                        src_ref=ins[a], dst_ref=outs[a].at[_dev_index(slot)], send_sem=send_sems.at[s],
                        recv_sem=recv_sems.at[s], device_id=peer, device_id_type=MESH))
        return mine, sends, arrivals

    def start(ins, outs, sems):
        mine, sends, _ = plan(ins, outs, sems)
        for cp in mine + sends:
            cp.start()

    def finish(ins, outs, sems):
        mine, sends, arrivals = plan(ins, outs, sems)
        for cp in arrivals:
            cp.wait_recv()
        for cp in sends:
            cp.wait_send()
        for cp in mine:
            cp.wait()

    return _Comm(arrays, [SDS((N_DEV,) + a.shape, a.dtype) for a in arrays],
                 _dma_sems(7 * n) + [pltpu.SemaphoreType.DMA((n,))], start, finish)


def _scatter_direct(arrays):
    n = len(arrays)

    def plan(ins, outs, sems):
        send_sems, recv_sems = sems
        x, y, c = _coords()
        cps = []
        for a in range(n):
            for k in range(1, N_DEV):
                peer = (1 - x if k & 4 else x, 1 - y if k & 2 else y, 1 - c if k & 1 else c)
                s = 7 * a + k - 1
                cps.append(pltpu.make_async_remote_copy(
                    src_ref=ins[a].at[_dev_index(peer)], dst_ref=outs[a].at[k - 1], send_sem=send_sems.at[s],
                    recv_sem=recv_sems.at[s], device_id=peer, device_id_type=MESH))
        return cps

    def start(ins, outs, sems):
        for cp in plan(ins, outs, sems):
            cp.start()

    def finish(ins, outs, sems):
        for cp in plan(ins, outs, sems):
            cp.wait()

    return _Comm(arrays, [SDS((N_DEV - 1,) + a.shape[1:], a.dtype) for a in arrays], _dma_sems(7 * n),
                 start, finish)


def _run(comm, name):
    def body(*refs):
        k_in, k_out = len(comm.inputs), len(comm.out_shapes)
        ins, outs, sems = refs[:k_in], refs[k_in:k_in + k_out], refs[k_in + k_out:]
        comm.start(ins, outs, sems)
        comm.finish(ins, outs, sems)

    return pl.pallas_call(body, name=name, out_shape=comm.out_shapes, in_specs=[ANY] * len(comm.inputs),
                          out_specs=(ANY,) * len(comm.out_shapes), scratch_shapes=comm.sem_shapes)(*comm.inputs)


TILE_ELEMS = 192 * 1024


def _row_tile(R, C):
    if R * C <= TILE_ELEMS:
        return R
    return max(t for t in range(SUBLANES, R, SUBLANES) if R % t == 0 and t * C <= TILE_ELEMS)


def _adamw_math(w, g, m, v):
    m = ADAM_B1 * m + (1.0 - ADAM_B1) * g
    v = ADAM_B2 * v + (1.0 - ADAM_B2) * (g * g)
    m_hat = m / (1.0 - ADAM_B1 ** ADAM_STEP)
    v_hat = v / (1.0 - ADAM_B2 ** ADAM_STEP)
    delta = -ADAM_LR * (m_hat / (jnp.sqrt(v_hat) + ADAM_EPS) + ADAM_WD * w)
    return delta, m, v


def _adamw_sharded(own, recv, d_idx, w, m, v, name):
    R, C = w.shape
    t = _row_tile(R, C)

    def body(k_ref, p_ref, r_ref, w_ref, m_ref, v_ref, g_ref, d_ref, nm_ref, nv_ref):
        g = p_ref[...].astype(f32)
        for j in range(N_DEV - 1):
            g = g + r_ref[j].astype(f32)
        d, nm, nv = _adamw_math(w_ref[...], g, m_ref[...], v_ref[...])
        g_ref[...], d_ref[...], nm_ref[...], nv_ref[...] = g, d, nm, nv

    tile = lambda: pl.BlockSpec((t, C), lambda i, k: (i, 0))
    return pl.pallas_call(
        body, name="adamw_" + name, out_shape=(SDS((R, C), f32),) * 4,
        grid_spec=pltpu.PrefetchScalarGridSpec(
            num_scalar_prefetch=1, grid=(R // t,),
            in_specs=[pl.BlockSpec((None, t, C), lambda i, k: (k[0], i, 0)),
                      pl.BlockSpec((N_DEV - 1, t, C), lambda i, k: (0, i, 0)), tile(), tile(), tile()],
            out_specs=(tile(), tile(), tile(), tile())),
        compiler_params=_params(("parallel",)))(d_idx, own, recv, w, m, v)


def _adamw_replicated(parts, ws, ms, vs):
    n = len(ws)

    def body(*refs):
        p, w, m, v = (refs[i * n:(i + 1) * n] for i in range(4))
        outs = refs[4 * n:]
        for a in range(n):
            g = p[a][0].astype(f32)
            for j in range(1, N_DEV):
                g = g + p[a][j].astype(f32)
            d, nm, nv = _adamw_math(w[a][...], g, m[a][...], v[a][...])
            for i, val in enumerate((g, d, nm, nv)):
                outs[i * n + a][...] = val

    shapes = tuple(SDS(w.shape, f32) for w in ws)
    res = pl.pallas_call(body, name="adamw_replicated", out_shape=shapes * 4,
                         compiler_params=_params())(*parts, *ws, *ms, *vs)
    return [res[i * n:(i + 1) * n] for i in range(4)]


def _cols_to_full(g):
    n, r, c = g.shape
    return g.transpose(1, 0, 2).reshape(r, n * c)


def _full_to_cols(a):
    r, c = a.shape
    return a.reshape(r, N_DEV, c // N_DEV).transpose(1, 0, 2)


def _rows_blocked(a):
    r, c = a.shape
    return a.reshape(N_DEV, r // N_DEV, c)


def _w_in_to_internal(w):
    K = w.shape[0]
    q = w[:, 1280:2816].reshape(K, NG, KVH, 1, HD).transpose(0, 2, 1, 3, 4).reshape(K, KVH, NG, HD)
    k = w[:, 2816:3328].reshape(K, KVH, 1, HD)
    v = w[:, 3328:3840].reshape(K, KVH, 1, HD)
    att = jnp.concatenate([q, k, v], axis=2).reshape(K, KVH * ATT_COLS)
    return jnp.concatenate([w[:, :1280], att, w[:, 3840:]], axis=1)


def _w_in_from_internal(w):
    K = w.shape[0]
    att = w[:, C_ATT:C_GATE].reshape(K, KVH, 5, HD)
    q = att[:, :, 0:3].transpose(0, 2, 1, 3).reshape(K, NG * KVH * HD)
    k = att[:, :, 3].reshape(K, KVH * HD)
    v = att[:, :, 4].reshape(K, KVH * HD)
    return jnp.concatenate([w[:, :C_ATT], q, k, v, w[:, C_GATE:]], axis=1)


_IN_NAMES = ('x', 'rel_bias', 'norm_mix_pre', 'norm_mix_post', 'w_in', 'conv_rnn_w', 'conv_rnn_b', 'w_rg_a', 'b_rg_a',
             'w_rg_x', 'b_rg_x', 'lru_lambda', 'w_branch_rnn', 'w_branch_att', 'w_out', 'norm_ffn_pre',
             'norm_ffn_post', 'w_ffn_gate', 'w_ffn_up', 'conv_ffn_w', 'conv_ffn_b', 'w_ffn_down')
_WEIGHTS = _IN_NAMES[1:]
_SHARDED = {"w_in": "col", "conv_rnn_w": "col", "w_branch_rnn": "row", "w_branch_att": "col", "w_out": "row",
            "w_ffn_gate": "col", "w_ffn_up": "col", "conv_ffn_w": "col", "w_ffn_down": "row"}
_REPLICATED = tuple(n for n in _WEIGHTS if n not in _SHARDED)


def _flat2(a):
    return a.reshape(-1, a.shape[-1])


def _train_step(inp):
    x_idx, y_idx, c_idx = _coords()
    W = {n: inp[n] for n in _WEIGHTS}
    x = inp["x"].reshape(-1, D)
    target = inp["loss_target"].reshape(-1, D)
    shard = {n: inp[n][0] for n in _SHARDED}

    hn, (g_in, g_cr, g_cf) = _norm_in(x, W["norm_mix_pre"], comm=_gather_two_level(
        [shard["w_in"].astype(bf16), shard["conv_rnn_w"], shard["conv_ffn_w"]]))
    w_in = _w_in_to_internal(_cols_to_full(g_in))
    cw_rnn, cw_ffn = _cols_to_full(g_cr), _cols_to_full(g_cf)
    behind_proj = ("w_ffn_up",)
    behind_rnn = ("w_branch_rnn", "w_branch_att", "w_out", "w_ffn_down")
    behind_att = ("w_ffn_gate",)

    wa, wx = W["w_rg_a"][0], W["w_rg_x"][0]
    buckets = jnp.asarray(_bucket_maps())

    proj, got = _mm(hn, (w_in, 0, C_GATE), "nn", f32, "mm_proj", 1024, C_GATE // 2, 1024, cols_outer=True,
                    comm=_gather_direct([shard[n].astype(bf16) for n in behind_proj]))
    gathered = dict(zip(behind_proj, got))
    gates = _mm(hn, w_in[:, C_GATE:], "nn", bf16, "mm_gates", 1024, 1024, 1024, cols_outer=True)
    rnn_saved, got = _rnn_fwd(proj, cw_rnn, W["conv_rnn_b"], wa, W["b_rg_a"], wx, W["b_rg_x"], W["lru_lambda"],
                              comm=_gather_direct([shard[n].astype(bf16) for n in behind_rnn]))
    h_rnn = rnn_saved[0]
    gathered.update(zip(behind_rnn, got))
    bias = _bias_tables(W["rel_bias"], buckets)
    (o_att, lse, *att_saved), got = _att_fwd(
        proj, bias, comm=_gather_direct([shard[n].astype(bf16) for n in behind_att]))
    gathered.update(zip(behind_att, got))
    w_brnn = gathered["w_branch_rnn"].reshape(RNN_W, D)
    w_batt = _cols_to_full(gathered["w_branch_att"])
    w_out = gathered["w_out"].reshape(D, D)
    w_gate, w_up = _cols_to_full(gathered["w_ffn_gate"]), _cols_to_full(gathered["w_ffn_up"])
    w_down = gathered["w_ffn_down"].reshape(FFN_W, D)
    pr = _mm(h_rnn, w_brnn, "nn", bf16, "mm_pr", 1024, 1024, 1280)
    pa = _mm(o_att, w_batt, "nn", bf16, "mm_pa", 1024, 1024, 512)
    merged = _merge_fwd(gates, pr, pa)
    (mix, h1, hn2), _ = _mm_rows([(merged, w_out, 1024)], "nn", "mm_mix", 1024, _mid_fwd_rows, [x],
                                 [W["norm_mix_post"], W["norm_ffn_pre"]], [f32, f32, bf16], [])
    gpre = _mm(hn2, w_gate, "nn", bf16, "mm_gate", 1024, 1024, 1024, cols_outer=True)
    up = _mm(hn2, w_up, "nn", bf16, "mm_up", 1024, 1024, 1024, cols_outer=True)
    act = _ffn_act_fwd(gpre, up, cw_ffn, W["conv_ffn_b"])
    (dy, dff), (loss_part, dg_fpost) = _mm_rows([(act, w_down, 1024)], "nn", "mm_down", 1024, _final_rows,
                                                [h1, target], [W["norm_ffn_post"]], [f32, bf16], [1, D])

    grads = {}
    dact = _mm(dff, w_down, "nt", bf16, "mm_dact", 1024, 1024, 1024, cols_outer=True)
    grads["w_ffn_down"] = _rows_blocked(_mm(act, dff, "tn", bf16, "mm_dw_down", 1024, 1024, 2048))
    dgpre, dup, dcw_ffn, dcb_ffn = _ffn_act_bwd(gpre, up, cw_ffn, W["conv_ffn_b"], dact)
    grads["conv_ffn_w"] = _full_to_cols(dcw_ffn.astype(bf16))
    grads["w_ffn_gate"] = _full_to_cols(_mm(hn2, dgpre, "tn", bf16, "mm_dw_gate", 1024, 1024, 2048))
    grads["w_ffn_up"] = _full_to_cols(_mm(hn2, dup, "tn", bf16, "mm_dw_up", 1024, 1024, 2048))
    dhn2 = _mm_nt_sum([(dgpre, w_gate, 1024), (dup, w_up, 1024)], "mm_dhn2", 1024, 1024)
    dh1, dmix, dg_fpre, dg_post = _mid_bwd(dy, dhn2, h1, W["norm_ffn_pre"], mix, W["norm_mix_post"])
    dmerged = _mm(dmix, w_out, "nt", bf16, "mm_dmerged", 1024, 1024, 1024)
    grads["w_out"] = _rows_blocked(_mm(merged, dmix, "tn", bf16, "mm_dw_out", 1024, 1024, 2048))
    dprpa, dgates = _merge_bwd(gates, pr, pa, dmerged)
    dpr, dpa = (dprpa, 0, D), (dprpa, D, D)
    dh_rnn = _mm(dpr, w_brnn, "nt", bf16, "mm_dh_rnn", 1024, 1280, 1024)
    grads["w_branch_rnn"] = _rows_blocked(_mm(h_rnn, dpr, "tn", bf16, "mm_dw_brnn", 1280, 1024, 1024))
    do_att = _mm(dpa, w_batt, "nt", f32, "mm_do_att", 1024, 512, 1024)
    grads["w_branch_att"] = _full_to_cols(_mm(o_att, dpa, "tn", bf16, "mm_dw_batt", 512, 1024, 2048))

    received = {}
    behind_att_bwd = ("w_ffn_down", "w_ffn_gate", "conv_ffn_w", "w_out")
    behind_rnn_bwd = ("w_ffn_up", "w_branch_rnn", "w_branch_att")
    (dqkv, dbias), got = _att_bwd(proj, att_saved, bias, o_att, lse, do_att,
                                  comm=_scatter_direct([grads[n] for n in behind_att_bwd]))
    received.update(zip(behind_att_bwd, got))
    drel = _bias_grad(dbias, buckets)
    (dxr, dcw_rnn, dcb_rnn, dwa, dba, dwx, dbx, dlam), got = _rnn_bwd(
        proj, rnn_saved, dh_rnn, cw_rnn, wa, wx, W["lru_lambda"],
        comm=_scatter_direct([grads[n] for n in behind_rnn_bwd]))
    received.update(zip(behind_rnn_bwd, got))
    gsmall = {"rel_bias": drel, "norm_mix_post": dg_post, "conv_rnn_b": dcb_rnn, "w_rg_a": dwa.astype(bf16),
              "b_rg_a": dba, "w_rg_x": dwx.astype(bf16), "b_rg_x": dbx, "lru_lambda": dlam,
              "norm_ffn_pre": dg_fpre, "norm_ffn_post": dg_fpost, "conv_ffn_b": dcb_ffn}
    dw_in_a, parts = _mm(hn, dqkv, "tn", bf16, "mm_dw_in_a", 1024, 1280, 1024,
                         comm=_gather_direct([_flat2(gsmall[n]) for n in gsmall]))
    parts = dict(zip(gsmall, parts))
    dw_in = jnp.concatenate([_mm(hn, dxr, "tn", bf16, "mm_dw_in_r", 1024, 1280, 1024), dw_in_a,
                             _mm(hn, dgates, "tn", bf16, "mm_dw_in_g", 1024, 1024, 2048)], axis=1)
    grads["w_in"] = _full_to_cols(_w_in_from_internal(dw_in))
    grads["conv_rnn_w"] = _full_to_cols(dcw_rnn.astype(bf16))
    behind_dhn = ("w_in", "conv_rnn_w")
    dhn, got = _mm_nt_sum([(dxr, w_in[:, :C_ATT], 1280), (dqkv, w_in[:, C_ATT:C_GATE], 1280),
                           (dgates, w_in[:, C_GATE:], 1024)], "mm_dhn", 1024, 1024,
                          comm=_scatter_direct([grads[n] for n in behind_dhn]))
    received.update(zip(behind_dhn, got))
    dx, dg_pre = _in_bwd(dh1, dhn, x, W["norm_mix_pre"])
    parts["norm_mix_pre"], = _run(_gather_two_level([dg_pre]), "ag_norm_mix_pre")
    parts = [parts[n] for n in _REPLICATED]

    out = {}
    d_arr = jnp.reshape(4 * x_idx + 2 * y_idx + c_idx, (1,)).astype(jnp.int32)
    for n in _SHARDED:
        res = _adamw_sharded(grads[n], received[n], d_arr, shard[n], inp["m_" + n][0], inp["v_" + n][0], n)
        out[n] = [r[None] for r in res]
    small = _adamw_replicated(parts, *[[_flat2(inp[p + n]) for n in _REPLICATED] for p in ("", "m_", "v_")])
    for a, n in enumerate(_REPLICATED):
        out[n] = [small[i][a].reshape(inp[n].shape) for i in range(4)]

    loss = lax.psum(loss_part[0, 0], ("x", "y", "c"))
    outs = [loss, dx.reshape(inp["x"].shape)]
    for i in range(4):
        outs.extend(out[n][i] for n in _WEIGHTS)
    return tuple(outs)


def kernel(x, rel_bias, norm_mix_pre, norm_mix_post, w_in, conv_rnn_w, conv_rnn_b, w_rg_a, b_rg_a, w_rg_x, b_rg_x, lru_lambda, w_branch_rnn, w_branch_att, w_out, norm_ffn_pre, norm_ffn_post, w_ffn_gate, w_ffn_up, conv_ffn_w, conv_ffn_b, w_ffn_down, loss_target, m_rel_bias, m_norm_mix_pre, m_norm_mix_post, m_w_in, m_conv_rnn_w, m_conv_rnn_b, m_w_rg_a, m_b_rg_a, m_w_rg_x, m_b_rg_x, m_lru_lambda, m_w_branch_rnn, m_w_branch_att, m_w_out, m_norm_ffn_pre, m_norm_ffn_post, m_w_ffn_gate, m_w_ffn_up, m_conv_ffn_w, m_conv_ffn_b, m_w_ffn_down, v_rel_bias, v_norm_mix_pre, v_norm_mix_post, v_w_in, v_conv_rnn_w, v_conv_rnn_b, v_w_rg_a, v_b_rg_a, v_w_rg_x, v_b_rg_x, v_lru_lambda, v_w_branch_rnn, v_w_branch_att, v_w_out, v_norm_ffn_pre, v_norm_ffn_post, v_w_ffn_gate, v_w_ffn_up, v_conv_ffn_w, v_conv_ffn_b, v_w_ffn_down):
    vals = locals()
    names = list(_IN_NAMES) + ["loss_target"] + ["m_" + n for n in _WEIGHTS] + ["v_" + n for n in _WEIGHTS]
    return _train_step({n: vals[n] for n in names})
```

```python
import functools
import math

import numpy as np
import jax
import jax.numpy as jnp
from jax import lax
from jax.experimental import pallas as pl
from jax.experimental.pallas import tpu as pltpu

f32, bf16 = jnp.float32, jnp.bfloat16
SDS = jax.ShapeDtypeStruct
MESH = pl.DeviceIdType.MESH
ANY = pl.BlockSpec(memory_space=pl.ANY)

D = 1024
SEQ = 2048
RNN_W = 1280
RNN_BLOCKS = 10
LANES = 128
SUBLANES = 8
RNN_CONV = 4
LRU_C = 8.0
HD = 128
KVH = 4
DILATIONS = (1, 4, 16)
NG = 3
ATT_BLK = 128
NBLK_SEQ = SEQ // ATT_BLK
STAT_LANE = 64
REL_BUCKETS = 32
REL_MAX_DIST = 2048
FFN_W = 3072
FFN_CONV = 3
EPS = 1e-6
IN_W = 5888
ATT_COLS = 5 * HD
C_ATT = RNN_W
C_GATE = RNN_W + KVH * ATT_COLS
NEG = -1e30

ADAM_LR, ADAM_B1, ADAM_B2, ADAM_EPS, ADAM_WD, ADAM_STEP = 0.001, 0.9, 0.999, 1e-08, 0.01, 10

VMEM_LIMIT_BYTES = 56 * 1024 * 1024
N_DEV = 8


def _params(sem=None):
    return pltpu.CompilerParams(dimension_semantics=sem, vmem_limit_bytes=VMEM_LIMIT_BYTES)


def _sigmoid(x):
    return 1.0 / (1.0 + jnp.exp(-x))


class _Comm:
    def __init__(self, inputs, out_shapes, sem_shapes, start, finish):
        self.inputs, self.out_shapes, self.sem_shapes = tuple(inputs), tuple(out_shapes), list(sem_shapes)
        self.start, self.finish = start, finish


def _call(body, args, *, name, grid, in_specs, out_specs, out_shape, scratch_shapes=(), semantics, comm=None):
    if comm is None:
        return pl.pallas_call(body, name=name, grid=grid, in_specs=list(in_specs), out_specs=tuple(out_specs),
                              out_shape=tuple(out_shape), scratch_shapes=list(scratch_shapes),
                              compiler_params=_params(semantics))(*args), ()
    n_in, n_out, n_scr = len(in_specs), len(out_shape), len(scratch_shapes)
    c_in, c_out = len(comm.inputs), len(comm.out_shapes)

    def fused(*refs):
        ins, refs = refs[:n_in], refs[n_in:]
        cin, refs = refs[:c_in], refs[c_in:]
        outs, refs = refs[:n_out], refs[n_out:]
        cout, refs = refs[:c_out], refs[c_out:]
        scr, csem = refs[:n_scr], refs[n_scr:]
        first = functools.reduce(jnp.logical_and, [pl.program_id(d) == 0 for d in range(len(grid))])
        last = functools.reduce(jnp.logical_and, [pl.program_id(d) == grid[d] - 1 for d in range(len(grid))])

        @pl.when(first)
        def _():
            comm.start(cin, cout, csem)

        body(*ins, *outs, *scr)

        @pl.when(last)
        def _():
            comm.finish(cin, cout, csem)

    res = pl.pallas_call(
        fused, name=name, grid=grid, in_specs=list(in_specs) + [ANY] * c_in,
        out_specs=tuple(out_specs) + (ANY,) * c_out, out_shape=tuple(out_shape) + comm.out_shapes,
        scratch_shapes=list(scratch_shapes) + comm.sem_shapes,
        compiler_params=_params(("arbitrary",) * len(grid)))(*args, *comm.inputs)
    return res[:n_out], res[n_out:]


_DIMS = {"nn": (((1,), (0,)), ((), ())), "nt": (((1,), (1,)), ((), ())), "tn": (((0,), (0,)), ((), ()))}


def _mm(a, b, mode, out_dtype, name, tm, tn, tk, cols_outer=False, comm=None):
    (a, a_c0, a_w), (b, b_c0, b_w) = [x if isinstance(x, tuple) else (x, 0, x.shape[1]) for x in (a, b)]
    if mode == "nn":
        (M, K), (K2, N) = (a.shape[0], a_w), (b.shape[0], b_w)
    elif mode == "nt":
        (M, K), (N, K2) = (a.shape[0], a_w), (b.shape[0], b_w)
    else:
        (K, M), (K2, N) = (a.shape[0], a_w), (b.shape[0], b_w)
    assert K == K2 and M % tm == 0 and N % tn == 0 and K % tk == 0, (name, a.shape, b.shape)
    a_tile, b_tile = (tm if mode == "tn" else tk), (tk if mode == "nt" else tn)
    assert a_c0 % a_tile == 0 and b_c0 % b_tile == 0, name
    a_off, b_off = a_c0 // a_tile, b_c0 // b_tile
    nk = K // tk

    def body(a_ref, b_ref, o_ref, *scratch):
        part = lax.dot_general(a_ref[...].astype(bf16), b_ref[...].astype(bf16), _DIMS[mode],
                               preferred_element_type=f32)
        if nk == 1:
            o_ref[...] = part.astype(o_ref.dtype)
        else:
            acc_ref, = scratch
            k = pl.program_id(2)

            @pl.when(k == 0)
            def _():
                acc_ref[...] = part

            @pl.when(k > 0)
            def _():
                acc_ref[...] += part

            @pl.when(k == nk - 1)
            def _():
                o_ref[...] = acc_ref[...].astype(o_ref.dtype)

    def ij(f):
        return (lambda j, i, k: f(i, j, k)) if cols_outer else f

    if mode == "tn":
        a_spec = pl.BlockSpec((tk, tm), ij(lambda i, j, k: (k, i + a_off)))
    else:
        a_spec = pl.BlockSpec((tm, tk), ij(lambda i, j, k: (i, k + a_off)))
    if mode == "nt":
        b_spec = pl.BlockSpec((tn, tk), ij(lambda i, j, k: (j, k + b_off)))
    else:
        b_spec = pl.BlockSpec((tk, tn), ij(lambda i, j, k: (k, j + b_off)))
    o_spec = pl.BlockSpec((tm, tn), ij(lambda i, j, k: (i, j)))
    grid = (N // tn, M // tm, nk) if cols_outer else (M // tm, N // tn, nk)
    (out,), extra = _call(
        body, (a, b), name=name, out_shape=(SDS((M, N), out_dtype),), grid=grid, in_specs=[a_spec, b_spec],
        out_specs=(o_spec,), scratch_shapes=[pltpu.VMEM((tm, tn), f32)] if nk > 1 else [],
        semantics=("parallel", "parallel", "arbitrary"), comm=comm)
    return out if comm is None else (out, extra)


def _mm_nt_sum(pairs, name, tm, tn, comm=None):
    M, N = pairs[0][0].shape[0], pairs[0][1].shape[0]
    nks = [a.shape[1] // tk for a, _, tk in pairs]
    starts = [sum(nks[:p]) for p in range(len(pairs))]
    nk = sum(nks)

    def body(*refs):
        o_ref, acc_ref = refs[-2], refs[-1]
        k = pl.program_id(2)
        for p in range(len(pairs)):
            def product(p=p):
                return lax.dot_general(refs[2 * p][...], refs[2 * p + 1][...], _DIMS["nt"], preferred_element_type=f32)

            if p == 0:
                @pl.when(k == 0)
                def _():
                    acc_ref[...] = product()

            @pl.when((k >= max(starts[p], 1)) & (k < starts[p] + nks[p]))
            def _():
                acc_ref[...] += product()

        @pl.when(k == nk - 1)
        def _():
            o_ref[...] = acc_ref[...].astype(bf16)

    in_specs, args = [], []
    for (a, b, tk), k0, n in zip(pairs, starts, nks):
        assert a.shape[1] == b.shape[1] and a.shape[1] % tk == 0 and a.dtype == b.dtype == bf16, name
        chunk = lambda k, k0=k0, n=n: jnp.clip(k - k0, 0, n - 1)
        in_specs += [pl.BlockSpec((tm, tk), lambda i, j, k, c=chunk: (i, c(k))),
                     pl.BlockSpec((tn, tk), lambda i, j, k, c=chunk: (j, c(k)))]
        args += [a, b]
    o_spec = pl.BlockSpec((tm, tn), lambda i, j, k: (i, j))
    (out,), extra = _call(
        body, args, name=name, out_shape=(SDS((M, N), bf16),), grid=(M // tm, N // tn, nk), in_specs=in_specs,
        out_specs=(o_spec,), scratch_shapes=[pltpu.VMEM((tm, tn), f32)],
        semantics=("parallel", "parallel", "arbitrary"), comm=comm)
    return out if comm is None else (out, extra)


def _mm_rows(pairs, mode, name, tm, epilogue, rows_in, vecs_in, rows_out, vecs_out, comm=None):
    M = pairs[0][0].shape[0]
    N = pairs[0][1].shape[1 if mode == "nn" else 0]
    nks = [a.shape[1] // tk for a, _, tk in pairs]
    starts = [sum(nks[:p]) for p in range(len(pairs))]
    nk = sum(nks)
    n_rows_in, n_vecs_in, n_rows_out = len(rows_in), len(vecs_in), len(rows_out)

    def body(*refs):
        pair_refs, refs = refs[:2 * len(pairs)], refs[2 * len(pairs):]
        rin, refs = refs[:n_rows_in], refs[n_rows_in:]
        vin, refs = refs[:n_vecs_in], refs[n_vecs_in:]
        rout, refs = refs[:n_rows_out], refs[n_rows_out:]
        vout, acc_ref = refs[:-1], refs[-1]
        i, k = pl.program_id(0), pl.program_id(1)
        for p in range(len(pairs)):
            def product(p=p):
                return lax.dot_general(pair_refs[2 * p][...], pair_refs[2 * p + 1][...], _DIMS[mode],
                                       preferred_element_type=f32)

            if p == 0:
                @pl.when(k == 0)
                def _():
                    acc_ref[...] = product()

            @pl.when((k >= max(starts[p], 1)) & (k < starts[p] + nks[p]))
            def _():
                acc_ref[...] += product()

        @pl.when(k == nk - 1)
        def _():
            res = epilogue(acc_ref[...], *[r[...] for r in rin], *[v[...] for v in vin])
            for ref, val in zip(rout, res[:n_rows_out]):
                ref[...] = val.astype(ref.dtype)
            for ref, val in zip(vout, res[n_rows_out:]):
                @pl.when(i == 0)
                def _(ref=ref, val=val):
                    ref[...] = val

                @pl.when(i > 0)
                def _(ref=ref, val=val):
                    ref[...] += val

    in_specs, args = [], []
    for (a, b, tk), k0, n in zip(pairs, starts, nks):
        assert a.shape[1] % tk == 0 and a.dtype == b.dtype == bf16, name
        chunk = lambda k, k0=k0, n=n: jnp.clip(k - k0, 0, n - 1)
        in_specs.append(pl.BlockSpec((tm, tk), lambda i, k, c=chunk: (i, c(k))))
        if mode == "nn":
            in_specs.append(pl.BlockSpec((tk, N), lambda i, k, c=chunk: (c(k), 0)))
        else:
            in_specs.append(pl.BlockSpec((N, tk), lambda i, k, c=chunk: (0, c(k))))
        args += [a, b]
    row = lambda: pl.BlockSpec((tm, N), lambda i, k: (i, 0))
    vec = lambda w: pl.BlockSpec((1, w), lambda i, k: (0, 0))
    in_specs += [row() for _ in rows_in] + [vec(v.shape[1]) for v in vecs_in]
    outs, extra = _call(
        body, (*args, *rows_in, *vecs_in), name=name,
        out_shape=tuple(SDS((M, N), dt) for dt in rows_out) + tuple(SDS((1, w), f32) for w in vecs_out),
        grid=(M // tm, nk), in_specs=in_specs,
        out_specs=tuple(row() for _ in rows_out) + tuple(vec(w) for w in vecs_out),
        scratch_shapes=[pltpu.VMEM((tm, N), f32)], semantics=("arbitrary", "arbitrary"), comm=comm)
    res = (outs[:n_rows_out], outs[n_rows_out:])
    return res if comm is None else (res, extra)


ROW_TILE = 512


def _rms_fwd(x, g):
    r = lax.rsqrt(jnp.mean(x * x, axis=-1, keepdims=True) + EPS)
    return x * r * g


def _rms_bwd(x, g, dy):
    r = lax.rsqrt(jnp.mean(x * x, axis=-1, keepdims=True) + EPS)
    xh = x * r
    dxh = dy * g
    dx = r * (dxh - xh * jnp.mean(dxh * xh, axis=-1, keepdims=True))
    return dx, jnp.sum(dy * xh, axis=0, keepdims=True)


def _acc_out(ref, val):
    @pl.when(pl.program_id(0) == 0)
    def _():
        ref[...] = val

    @pl.when(pl.program_id(0) > 0)
    def _():
        ref[...] += val


def _row_spec(width=D):
    return pl.BlockSpec((ROW_TILE, width), lambda i: (i, 0))


def _vec_spec(width=D):
    return pl.BlockSpec((1, width), lambda i: (0, 0))


def _norm_in(x, g, comm=None):
    def body(x_ref, g_ref, o_ref):
        o_ref[...] = _rms_fwd(x_ref[...], g_ref[...]).astype(bf16)

    T = x.shape[0]
    (hn,), extra = _call(body, (x, g), name="norm_in", out_shape=(SDS((T, D), bf16),), grid=(T // ROW_TILE,),
                         in_specs=[_row_spec(), _vec_spec()], out_specs=(_row_spec(),), semantics=("parallel",),
                         comm=comm)
    return hn, extra


def _mid_bwd(dy, dhn2, h1, g_fpre, mix, g_post):
    def body(dy_ref, dhn2_ref, h1_ref, gf_ref, mix_ref, gp_ref, dh1_ref, dmix_ref, dgf_ref, dgp_ref):
        d1, dgf = _rms_bwd(h1_ref[...], gf_ref[...], dhn2_ref[...].astype(f32))
        dh1 = dy_ref[...] + d1
        dh1_ref[...] = dh1
        dmix, dgp = _rms_bwd(mix_ref[...], gp_ref[...], dh1)
        dmix_ref[...] = dmix.astype(bf16)
        _acc_out(dgf_ref, dgf)
        _acc_out(dgp_ref, dgp)

    T = dy.shape[0]
    return pl.pallas_call(
        body, name="mid_bwd", out_shape=(SDS((T, D), f32), SDS((T, D), bf16), SDS((1, D), f32), SDS((1, D), f32)),
        grid=(T // ROW_TILE,),
        in_specs=[_row_spec(), _row_spec(), _row_spec(), _vec_spec(), _row_spec(), _vec_spec()],
        out_specs=(_row_spec(), _row_spec(), _vec_spec(), _vec_spec()),
        compiler_params=_params(("arbitrary",)))(dy, dhn2, h1, g_fpre, mix, g_post)


def _in_bwd(dh1, dhn, x, g_pre):
    def body(dh1_ref, dhn_ref, x_ref, g_ref, dx_ref, dg_ref):
        d, dg = _rms_bwd(x_ref[...], g_ref[...], dhn_ref[...].astype(f32))
        dx_ref[...] = dh1_ref[...] + d
        _acc_out(dg_ref, dg)

    T = x.shape[0]
    return pl.pallas_call(
        body, name="in_bwd", out_shape=(SDS((T, D), f32), SDS((1, D), f32)), grid=(T // ROW_TILE,),
        in_specs=[_row_spec(), _row_spec(), _row_spec(), _vec_spec()], out_specs=(_row_spec(), _vec_spec()),
        compiler_params=_params(("arbitrary",)))(dh1, dhn, x, g_pre)


def _mid_fwd_rows(mix, x, g_post, g_fpre):
    h1 = x + _rms_fwd(mix, g_post)
    return mix, h1, _rms_fwd(h1, g_fpre)


def _final_rows(ff, h1, target, g_fpost):
    e = h1 + _rms_fwd(ff, g_fpost) - target
    part = jnp.sum(jnp.sum(e * e, axis=1, keepdims=True), axis=0, keepdims=True) * (0.5 / D)
    dy = e * (1.0 / D)
    dff, dg = _rms_bwd(ff, g_fpost, dy)
    return dy, dff, part, dg


def _shift_dn(x, d, row, fill=0.0):
    if d == 0:
        return x
    y = pltpu.roll(x, d, 0)
    head = jnp.where(row[:SUBLANES] >= d, y[:SUBLANES], fill)
    return jnp.concatenate([head, y[SUBLANES:]], axis=0)


def _shift_up(x, d, row, fill=0.0):
    if d == 0:
        return x
    n = x.shape[0]
    y = pltpu.roll(x, n - d, 0)
    tail = jnp.where(row[:SUBLANES] < SUBLANES - d, y[n - SUBLANES:], fill)
    return jnp.concatenate([y[:n - SUBLANES], tail], axis=0)


def _conv_fwd(x, w_ref, b, row):
    K = w_ref.shape[0]
    y = b
    for k in range(K):
        y = y + w_ref[k:k + 1, :] * _shift_dn(x, K - 1 - k, row)
    return y


def _conv_bwd(x, w_ref, dy, row):
    K = w_ref.shape[0]
    dx = jnp.zeros_like(dy)
    dws = []
    for k in range(K):
        dx = dx + w_ref[k:k + 1, :] * _shift_up(dy, K - 1 - k, row)
        dws.append(jnp.sum(dy * _shift_dn(x, K - 1 - k, row), axis=0, keepdims=True))
    return dx, dws, jnp.sum(dy, axis=0, keepdims=True)


def _scan_fwd(a, u, row):
    n = a.shape[0]
    d = 1
    while d < n:
        last = 2 * d >= n
        if d < SUBLANES:
            u = u + a * _shift_dn(u, d, row)
            if not last:
                a = a * _shift_dn(a, d, row, fill=1.0)
        else:
            u = jnp.concatenate([u[:d], u[d:] + a[d:] * u[:n - d]], axis=0)
            if not last:
                a = jnp.concatenate([a[:d], a[d:] * a[:n - d]], axis=0)
        d *= 2
    return u


def _scan_bwd(b, u, row):
    n = b.shape[0]
    d = 1
    while d < n:
        last = 2 * d >= n
        if d < SUBLANES:
            u = u + b * _shift_up(u, d, row)
            if not last:
                b = b * _shift_up(b, d, row, fill=1.0)
        else:
            u = jnp.concatenate([u[:n - d] + b[:n - d] * u[d:], u[n - d:]], axis=0)
            if not last:
                b = jnp.concatenate([b[:n - d] * b[d:], b[n - d:]], axis=0)
        d *= 2
    return u


def _neg_expm1(z):
    series = -z * (1.0 + z * (0.5 + z * (1.0 / 6.0 + z * (1.0 / 24.0 + z * (1.0 / 120.0)))))
    return jnp.where(z > -0.1, series, 1.0 - jnp.exp(z))


def _softplus_neg(lam):
    z = -lam
    return jnp.maximum(z, 0.0) + jnp.log(1.0 + jnp.exp(-jnp.abs(z)))


def _rnn_specs(B):
    blk = lambda: pl.BlockSpec((SEQ, LANES), lambda b, n: (b, n))
    return dict(
        act=blk,
        convw=pl.BlockSpec((RNN_CONV, LANES), lambda b, n: (0, n)),
        vec=lambda: pl.BlockSpec((1, LANES), lambda b, n: (0, n)),
        gate=lambda: pl.BlockSpec((None, LANES, LANES), lambda b, n: (n, 0, 0)),
    )


def _rnn_fwd(proj, cw, cb, wa, ba, wx, bx, lam, comm=None):
    T = proj.shape[0]
    B = T // SEQ

    def body(x_ref, cw_ref, cb_ref, wa_ref, ba_ref, wx_ref, bx_ref, lam_ref, h_ref, xc_ref, r_ref, i_ref, a_ref, s_ref):
        row = lax.broadcasted_iota(jnp.int32, (SEQ, LANES), 0)
        xc = _conv_fwd(x_ref[...], cw_ref, cb_ref[...], row)
        xcb = xc.astype(bf16)
        r = _sigmoid(jnp.dot(xcb, wa_ref[...].astype(bf16), preferred_element_type=f32) + ba_ref[...])
        i = _sigmoid(jnp.dot(xcb, wx_ref[...].astype(bf16), preferred_element_type=f32) + bx_ref[...])
        log_a = (-LRU_C * _softplus_neg(lam_ref[...])) * r
        a = jnp.exp(log_a)
        s = jnp.sqrt(_neg_expm1(2.0 * log_a))
        xc_ref[...], r_ref[...], i_ref[...], a_ref[...], s_ref[...] = xc, r, i, a, s
        h_ref[...] = _scan_fwd(a, s * (i * xc), row)

    sp_ = _rnn_specs(B)
    return _call(
        body, (proj, cw, cb, wa, ba, wx, bx, lam), name="rnn_fwd", out_shape=(SDS((T, RNN_W), f32),) * 6,
        grid=(B, RNN_BLOCKS),
        in_specs=[sp_["act"](), sp_["convw"], sp_["vec"](), sp_["gate"](), sp_["vec"](), sp_["gate"](),
                  sp_["vec"](), sp_["vec"]()],
        out_specs=tuple(sp_["act"]() for _ in range(6)), semantics=("parallel", "parallel"), comm=comm)


def _rnn_bwd(proj, saved, dh, cw, wa, wx, lam, comm=None):
    T = proj.shape[0]
    B = T // SEQ

    def body(x_ref, h_ref, xc_ref, r_ref, i_ref, a_ref, s_ref, dh_ref, cw_ref, wa_ref, wx_ref, lam_ref,
             dx_ref, dcw_ref, dcb_ref, dwa_ref, dba_ref, dwx_ref, dbx_ref, dlam_ref):
        row = lax.broadcasted_iota(jnp.int32, (SEQ, LANES), 0)
        xr = x_ref[...]
        wa, wx, lam = wa_ref[...], wx_ref[...], lam_ref[...]
        xc, r, i, a, s = xc_ref[...], r_ref[...], i_ref[...], a_ref[...], s_ref[...]
        xcb = xc.astype(bf16)
        sp = _softplus_neg(lam)
        hprev = _shift_dn(h_ref[...], 1, row)
        g = _scan_bwd(_shift_up(a, 1, row), dh_ref[...].astype(f32), row)
        da = g * hprev
        ds = g * (i * xc)
        di = g * (s * xc)
        dxc = g * (s * i)
        dla = da * a - ds * (a * a) / s
        dr = dla * (-LRU_C * sp)
        dsp = jnp.sum(dla * (-LRU_C * r), axis=0, keepdims=True)
        dlam = -dsp * _sigmoid(-lam)
        dga = dr * r * (1.0 - r)
        dgx = di * i * (1.0 - i)
        dgab, dgxb = dga.astype(bf16), dgx.astype(bf16)
        dwa = lax.dot_general(xcb, dgab, _DIMS["tn"], preferred_element_type=f32)
        dwx = lax.dot_general(xcb, dgxb, _DIMS["tn"], preferred_element_type=f32)
        dxc = dxc + lax.dot_general(dgab, wa.astype(bf16), _DIMS["nt"], preferred_element_type=f32)
        dxc = dxc + lax.dot_general(dgxb, wx.astype(bf16), _DIMS["nt"], preferred_element_type=f32)
        dx, dws, db = _conv_bwd(xr, cw_ref, dxc, row)
        dx_ref[...] = dx.astype(bf16)
        first = pl.program_id(1) == 0

        def acc(ref, val):
            @pl.when(first)
            def _():
                ref[...] = val

            @pl.when(jnp.logical_not(first))
            def _():
                ref[...] += val

        for k in range(RNN_CONV):
            acc(dcw_ref.at[k:k + 1, :], dws[k])
        acc(dcb_ref, db)
        acc(dwa_ref, dwa)
        acc(dba_ref, jnp.sum(dga, axis=0, keepdims=True))
        acc(dwx_ref, dwx)
        acc(dbx_ref, jnp.sum(dgx, axis=0, keepdims=True))
        acc(dlam_ref, dlam)

    blk = lambda: pl.BlockSpec((SEQ, LANES), lambda n, b: (b, n))
    convw = lambda: pl.BlockSpec((RNN_CONV, LANES), lambda n, b: (0, n))
    vec = lambda: pl.BlockSpec((1, LANES), lambda n, b: (0, n))
    gate = lambda: pl.BlockSpec((None, LANES, LANES), lambda n, b: (n, 0, 0))
    vshape = SDS((1, RNN_W), f32)
    gshape = SDS((RNN_BLOCKS, LANES, LANES), f32)
    return _call(
        body, (proj, *saved, dh, cw, wa, wx, lam), name="rnn_bwd",
        out_shape=(SDS((T, RNN_W), bf16), SDS((RNN_CONV, RNN_W), f32), vshape, gshape, vshape, gshape, vshape, vshape),
        grid=(RNN_BLOCKS, B),
        in_specs=[blk() for _ in range(8)] + [convw(), gate(), gate(), vec()],
        out_specs=(blk(), convw(), vec(), gate(), vec(), gate(), vec(), vec()),
        semantics=("parallel", "arbitrary"), comm=comm)


def _t5_bucket(dist):
    max_exact = REL_BUCKETS // 2
    d = np.maximum(dist, 1).astype(np.float32)
    large = max_exact + np.log(d / max_exact) / math.log(REL_MAX_DIST / max_exact) * (REL_BUCKETS - max_exact)
    large = np.minimum(large.astype(np.int32), REL_BUCKETS - 1)
    return np.where(dist < max_exact, dist, large).astype(np.int32)


def _bucket_maps():
    qi = np.arange(ATT_BLK)[:, None]
    kj = np.arange(2 * ATT_BLK)[None, :]
    delta = ATT_BLK + qi - kj
    valid = (delta >= 0) & (delta <= ATT_BLK)
    maps = [np.where(valid, _t5_bucket(np.maximum(delta, 0) * r), -1) for r in DILATIONS]
    return np.stack(maps).astype(np.int32)


def _bias_tables(rel_bias, buckets):
    def body(rb_ref, bk_ref, o_ref):
        for g in range(NG):
            bk = bk_ref[g]
            for h in range(KVH):
                acc = jnp.full(bk.shape, NEG, f32)
                for b in range(REL_BUCKETS):
                    acc = jnp.where(bk == b, rb_ref[b, g * KVH + h], acc)
                o_ref[h, g] = acc

    return pl.pallas_call(
        body, name="bias_tables", out_shape=SDS((KVH, NG, ATT_BLK, 2 * ATT_BLK), f32),
        in_specs=[pl.BlockSpec(memory_space=pltpu.SMEM), pl.BlockSpec(memory_space=pltpu.VMEM)],
        out_specs=pl.BlockSpec(memory_space=pltpu.VMEM), compiler_params=_params())(rel_bias, buckets)


def _bias_grad(dbias, buckets):
    def body(db_ref, bk_ref, o_ref):
        rr = lax.broadcasted_iota(jnp.int32, (REL_BUCKETS, NG * KVH), 0)
        cc = lax.broadcasted_iota(jnp.int32, (REL_BUCKETS, NG * KVH), 1)
        out = jnp.zeros((REL_BUCKETS, NG * KVH), f32)
        for g in range(NG):
            bk = bk_ref[g]
            for h in range(KVH):
                d = db_ref[h, g]
                for b in range(REL_BUCKETS):
                    m = jnp.where(bk == b, d, 0.0)
                    s = jnp.sum(jnp.sum(m, axis=1, keepdims=True), axis=0, keepdims=True)
                    out = jnp.where((rr == b) & (cc == g * KVH + h), s, out)
        o_ref[...] = out

    return pl.pallas_call(body, name="bias_grad", out_shape=SDS((REL_BUCKETS, NG * KVH), f32),
                          compiler_params=_params())(dbias, buckets)


def _to_sub(dst_ref, src_ref, r, dtype, offset=0):
    M = SEQ // r
    for c in range(r):
        if r == 1:
            v = src_ref[...]
        else:
            v = src_ref[pl.ds(c, M, stride=r), :]
        dst_ref[pl.ds(offset + c * M, M), :] = v.astype(dtype)


def _from_sub(dst_ref, src_ref, r, accumulate=False, offset=0):
    M = SEQ // r
    for c in range(r):
        v = src_ref[pl.ds(offset + c * M, M), :]
        idx = slice(None) if r == 1 else pl.ds(c, M, stride=r)
        if accumulate:
            dst_ref[idx, :] = dst_ref[idx, :] + v
        else:
            dst_ref[idx, :] = v


_COL = lambda k: slice(k * HD, (k + 1) * HD)
SCALE = HD ** -0.5


def _qkv_spec(k, bh):
    def index(*ids):
        b, h = bh(*ids)
        return (b, C_ATT // HD + 5 * h + k)

    return pl.BlockSpec((SEQ, HD), index)


def _key_window(bias_ref, g, nb):
    if nb == 1:
        bias_own = bias_ref[g, :, ATT_BLK:2 * ATT_BLK]
        return lambda j: (pl.ds(pl.multiple_of((j + 1) * ATT_BLK, ATT_BLK), ATT_BLK), bias_own)
    bias_g = bias_ref[g]
    col = lax.broadcasted_iota(jnp.int32, bias_g.shape, 1)
    bias_first = jnp.where(col >= ATT_BLK, bias_g, NEG)
    return lambda j: (pl.ds(pl.multiple_of(j * ATT_BLK, ATT_BLK), 2 * ATT_BLK),
                      jnp.where(j % nb != 0, bias_g, bias_first))


def _att_fwd(proj, bias, comm=None):
    T = proj.shape[0]
    B = T // SEQ

    def body(q0_ref, q1_ref, q2_ref, k_ref, v_ref, bias_ref, o_ref, lse_ref, *rest):
        saved, (qp, kp, vp, kt, op, lp, og, lg) = rest[:3 * (NG - 1)], rest[3 * (NG - 1):]
        q_refs = (q0_ref, q1_ref, q2_ref)
        kp[0:ATT_BLK, :] = jnp.zeros((ATT_BLK, HD), bf16)
        vp[0:ATT_BLK, :] = jnp.zeros((ATT_BLK, HD), bf16)
        for g, r in enumerate(DILATIONS):
            nb = NBLK_SEQ // r
            _to_sub(qp, q_refs[g], r, bf16)
            _to_sub(kp, k_ref, r, bf16, offset=ATT_BLK)
            _to_sub(vp, v_ref, r, bf16, offset=ATT_BLK)
            if r > 1:
                sq, sk, sv = saved[3 * (g - 1):3 * g]
                sq[...], sk[...], sv[...] = qp[...], kp[ATT_BLK:, :], vp[ATT_BLK:, :]
            kt[...] = kp[...].T
            keys = _key_window(bias_ref, g, nb)

            def step(j, carry):
                cur = pl.ds(pl.multiple_of(j * ATT_BLK, ATT_BLK), ATT_BLK)
                win, bias_j = keys(j)
                s = jnp.dot(qp[cur, :], kt[:, win], preferred_element_type=f32) * SCALE + bias_j
                m = jnp.max(s, axis=-1, keepdims=True)
                p = jnp.exp(s - m)
                den = jnp.sum(p, axis=-1, keepdims=True)
                o = jnp.dot(p.astype(bf16), vp[win, :], preferred_element_type=f32)
                op[cur, :] = o / den
                lp[cur, :] = jnp.broadcast_to(m + jnp.log(den), (ATT_BLK, HD))
                return carry

            lax.fori_loop(0, NBLK_SEQ, step, 0, unroll=NBLK_SEQ)
            _from_sub(og.at[g], op, r)
            _from_sub(lg.at[g], lp, r)
        l0, l1, l2 = lg[0], lg[1], lg[2]
        mx = jnp.maximum(jnp.maximum(l0, l1), l2)
        e0, e1, e2 = jnp.exp(l0 - mx), jnp.exp(l1 - mx), jnp.exp(l2 - mx)
        den = e0 + e1 + e2
        o_ref[...] = (e0 * og[0] + e1 * og[1] + e2 * og[2]) / den
        lse_ref[...] = mx + jnp.log(den)

    return _call(
        body, (proj, proj, proj, proj, proj, bias), name="att_fwd",
        out_shape=(SDS((T, KVH * HD), f32), SDS((KVH, T, HD), f32)) + (SDS((T, KVH * HD), bf16),) * (3 * (NG - 1)),
        grid=(B, KVH),
        in_specs=[_qkv_spec(k, lambda b, h: (b, h)) for k in range(5)]
                 + [pl.BlockSpec((None, NG, ATT_BLK, 2 * ATT_BLK), lambda b, h: (h, 0, 0, 0))],
        out_specs=(pl.BlockSpec((SEQ, HD), lambda b, h: (b, h)),
                   pl.BlockSpec((None, SEQ, HD), lambda b, h: (h, b, 0)))
                  + tuple(pl.BlockSpec((SEQ, HD), lambda b, h: (b, h)) for _ in range(3 * (NG - 1))),
        scratch_shapes=[pltpu.VMEM((SEQ, HD), bf16)] + [pltpu.VMEM((SEQ + ATT_BLK, HD), bf16)] * 2
                       + [pltpu.VMEM((HD, SEQ + ATT_BLK), bf16)]
                       + [pltpu.VMEM((SEQ, HD), f32)] * 2 + [pltpu.VMEM((NG, SEQ, HD), f32)] * 2,
        semantics=("parallel", "parallel"), comm=comm)


def _att_bwd(proj, saved, bias, o, lse, do, comm=None):
    T = proj.shape[0]
    B = T // SEQ
    n_saved = 3 * (NG - 1)

    def body(q0_ref, k_ref, v_ref, *rest):
        saved_refs, rest = rest[:n_saved], rest[n_saved:]
        (bias_ref, o_ref, lse_ref, do_ref, dx_ref, db_ref,
         qp, kp, vp, dop, qt, kt, vt, dot, lp, dqp, dkt, dvt, dln, nat, dkn, dvn) = rest
        first = pl.program_id(1) == 0

        @pl.when(first)
        def _():
            db_ref[...] = jnp.zeros_like(db_ref)

        lane = lax.broadcasted_iota(jnp.int32, (SEQ, HD), 1)
        dln[...] = jnp.where(lane < STAT_LANE, lse_ref[...],
                             jnp.sum(do_ref[...] * o_ref[...], axis=-1, keepdims=True))
        dkn[...] = jnp.zeros_like(dkn)
        dvn[...] = jnp.zeros_like(dvn)
        kp[0:ATT_BLK, :] = jnp.zeros((ATT_BLK, HD), bf16)
        vp[0:ATT_BLK, :] = jnp.zeros((ATT_BLK, HD), bf16)
        for g, r in enumerate(DILATIONS):
            nb = NBLK_SEQ // r
            if r == 1:
                qp[...] = q0_ref[...].astype(bf16)
                kp[ATT_BLK:, :] = k_ref[...].astype(bf16)
                vp[ATT_BLK:, :] = v_ref[...].astype(bf16)
            else:
                sq, sk, sv = saved_refs[3 * (g - 1):3 * g]
                qp[...], kp[ATT_BLK:, :], vp[ATT_BLK:, :] = sq[...], sk[...], sv[...]
            _to_sub(dop, do_ref, r, bf16)
            _to_sub(lp, dln, r, f32)
            qt[...], kt[...], vt[...], dot[...] = qp[...].T, kp[...].T, vp[...].T, dop[...].T
            dkt[...] = jnp.zeros_like(dkt)
            dvt[...] = jnp.zeros_like(dvt)
            keys = _key_window(bias_ref, g, nb)
            db_cols = slice(ATT_BLK, 2 * ATT_BLK) if nb == 1 else slice(None)

            def step(j, carry):
                cur = pl.ds(pl.multiple_of(j * ATT_BLK, ATT_BLK), ATT_BLK)
                win, bias_j = keys(j)
                s = jnp.dot(qp[cur, :], kt[:, win], preferred_element_type=f32) * SCALE + bias_j
                p = jnp.exp(s - lp[cur, 0:1])
                dp = jnp.dot(dop[cur, :], vt[:, win], preferred_element_type=f32)
                ds = p * (dp - lp[cur, STAT_LANE:STAT_LANE + 1])
                db_ref[g, :, db_cols] += ds
                dsb, pb = ds.astype(bf16), p.astype(bf16)
                dqp[cur, :] = jnp.dot(dsb, kp[win, :], preferred_element_type=f32) * SCALE
                dkt[:, win] += jnp.dot(qt[:, cur], dsb, preferred_element_type=f32) * SCALE
                dvt[:, win] += jnp.dot(dot[:, cur], pb, preferred_element_type=f32)
                return carry

            lax.fori_loop(0, NBLK_SEQ, step, 0, unroll=NBLK_SEQ)
            _from_sub(nat, dqp, r)
            dx_ref[:, _COL(g)] = nat[...].astype(bf16)
            dqp[...] = dkt[:, ATT_BLK:].T
            _from_sub(dkn, dqp, r, accumulate=True)
            dqp[...] = dvt[:, ATT_BLK:].T
            _from_sub(dvn, dqp, r, accumulate=True)
        dx_ref[:, _COL(3)] = dkn[...].astype(bf16)
        dx_ref[:, _COL(4)] = dvn[...].astype(bf16)

    blk = lambda: pl.BlockSpec((SEQ, HD), lambda h, b: (b, h))
    bias_spec = lambda: pl.BlockSpec((None, NG, ATT_BLK, 2 * ATT_BLK), lambda h, b: (h, 0, 0, 0))
    pad = lambda dtype: pltpu.VMEM((SEQ + ATT_BLK, HD), dtype)
    pad_t = lambda dtype: pltpu.VMEM((HD, SEQ + ATT_BLK), dtype)
    seq_t = pltpu.VMEM((HD, SEQ), bf16)
    return _call(
        body, (proj, proj, proj, *saved, bias, o, lse, do), name="att_bwd",
        out_shape=(SDS((T, KVH * ATT_COLS), bf16), SDS((KVH, NG, ATT_BLK, 2 * ATT_BLK), f32)), grid=(KVH, B),
        in_specs=[_qkv_spec(k, lambda h, b: (b, h)) for k in (0, 3, 4)] + [blk() for _ in range(n_saved)]
                 + [bias_spec(), blk(), pl.BlockSpec((None, SEQ, HD), lambda h, b: (h, b, 0)), blk()],
        out_specs=(pl.BlockSpec((SEQ, ATT_COLS), lambda h, b: (b, h)), bias_spec()),
        scratch_shapes=[pltpu.VMEM((SEQ, HD), bf16), pad(bf16), pad(bf16), pltpu.VMEM((SEQ, HD), bf16),
                        seq_t, pad_t(bf16), pad_t(bf16), seq_t]
                       + [pltpu.VMEM((SEQ, HD), f32)] * 2 + [pad_t(f32)] * 2 + [pltpu.VMEM((SEQ, HD), f32)] * 4,
        semantics=("parallel", "arbitrary"), comm=comm)


MERGE_ROWS, MERGE_COLS = 1024, 512


def _merge_fwd(gates, pr, pa):
    def body(gr_ref, ga_ref, pr_ref, pa_ref, o_ref):
        o_ref[...] = (_sigmoid(gr_ref[...].astype(f32)) * pr_ref[...].astype(f32)
                      + _sigmoid(ga_ref[...].astype(f32)) * pa_ref[...].astype(f32)).astype(bf16)

    T = gates.shape[0]
    cols = lambda off: pl.BlockSpec((MERGE_ROWS, MERGE_COLS), lambda i, j: (i, off + j))
    return pl.pallas_call(body, name="merge_fwd", out_shape=SDS((T, D), bf16),
                          grid=(T // MERGE_ROWS, D // MERGE_COLS),
                          in_specs=[cols(0), cols(D // MERGE_COLS), cols(0), cols(0)], out_specs=cols(0),
                          compiler_params=_params(("parallel", "parallel")))(gates, gates, pr, pa)


def _merge_bwd(gates, pr, pa, dm):
    nj = D // MERGE_COLS

    def body(g_ref, pr_ref, pa_ref, dm_ref, dp_ref, dg_ref):
        dm_ = dm_ref[...].astype(f32)
        s = _sigmoid(g_ref[...].astype(f32))
        p = jnp.where(pl.program_id(1) < nj, pr_ref[...], pa_ref[...]).astype(f32)
        dp_ref[...] = (dm_ * s).astype(bf16)
        dg_ref[...] = (dm_ * p * s * (1.0 - s)).astype(bf16)

    T = gates.shape[0]
    blk = (MERGE_ROWS, MERGE_COLS)
    wrap = pl.BlockSpec(blk, lambda i, j: (i, j % nj))
    pr_spec = pl.BlockSpec(blk, lambda i, j: (i, jnp.minimum(j, nj - 1)))
    pa_spec = pl.BlockSpec(blk, lambda i, j: (i, jnp.maximum(j - nj, 0)))
    out = pl.BlockSpec(blk, lambda i, j: (i, j))
    return pl.pallas_call(
        body, name="merge_bwd", out_shape=(SDS((T, 2 * D), bf16), SDS((T, 2 * D), bf16)),
        grid=(T // MERGE_ROWS, 2 * nj),
        in_specs=[out, pr_spec, pa_spec, wrap], out_specs=(out, out),
        compiler_params=_params(("parallel", "parallel")))(gates, pr, pa, dm)


FFN_COLS = 256
GELU_C = math.sqrt(2.0 / math.pi)
GELU_A = 0.044715


def _gelu_parts(x):
    q = x * x
    t = jnp.tanh(x * (GELU_C + (GELU_C * GELU_A) * q))
    h = 0.5 + 0.5 * t
    return x * h, h * (1.0 + x * (1.0 - t) * (GELU_C + (3.0 * GELU_C * GELU_A) * q))


def _ffn_act_fwd(gpre, up, cw, cb):
    def body(g_ref, u_ref, cw_ref, cb_ref, o_ref):
        row = lax.broadcasted_iota(jnp.int32, (SEQ, FFN_COLS), 0)
        gate = _conv_fwd(g_ref[...].astype(f32), cw_ref, cb_ref[...], row)
        o_ref[...] = (_gelu_parts(gate)[0] * u_ref[...].astype(f32)).astype(bf16)

    T = gpre.shape[0]
    blk = lambda: pl.BlockSpec((SEQ, FFN_COLS), lambda b, j: (b, j))
    return pl.pallas_call(
        body, name="ffn_act_fwd", out_shape=SDS((T, FFN_W), bf16), grid=(T // SEQ, FFN_W // FFN_COLS),
        in_specs=[blk(), blk(), pl.BlockSpec((FFN_CONV, FFN_COLS), lambda b, j: (0, j)),
                  pl.BlockSpec((1, FFN_COLS), lambda b, j: (0, j))],
        out_specs=blk(), compiler_params=_params(("parallel", "parallel")))(gpre, up, cw, cb)


def _ffn_act_bwd(gpre, up, cw, cb, dact):
    def body(g_ref, u_ref, cw_ref, cb_ref, da_ref, dg_ref, du_ref, dcw_ref, dcb_ref):
        row = lax.broadcasted_iota(jnp.int32, (SEQ, FFN_COLS), 0)
        gp = g_ref[...].astype(f32)
        gate = _conv_fwd(gp, cw_ref, cb_ref[...], row)
        gel, dgel = _gelu_parts(gate)
        da = da_ref[...].astype(f32)
        du_ref[...] = (da * gel).astype(bf16)
        dgate = da * u_ref[...].astype(f32) * dgel
        dx, dws, db = _conv_bwd(gp, cw_ref, dgate, row)
        dg_ref[...] = dx.astype(bf16)
        first = pl.program_id(1) == 0

        def acc(ref, val):
            @pl.when(first)
            def _():
                ref[...] = val

            @pl.when(jnp.logical_not(first))
            def _():
                ref[...] += val

        for k in range(FFN_CONV):
            acc(dcw_ref.at[k:k + 1, :], dws[k])
        acc(dcb_ref, db)

    T = gpre.shape[0]
    blk = lambda: pl.BlockSpec((SEQ, FFN_COLS), lambda j, b: (b, j))
    cws = lambda: pl.BlockSpec((FFN_CONV, FFN_COLS), lambda j, b: (0, j))
    cbs = lambda: pl.BlockSpec((1, FFN_COLS), lambda j, b: (0, j))
    return pl.pallas_call(
        body, name="ffn_act_bwd",
        out_shape=(SDS((T, FFN_W), bf16), SDS((T, FFN_W), bf16), SDS((FFN_CONV, FFN_W), f32), SDS((1, FFN_W), f32)),
        grid=(FFN_W // FFN_COLS, T // SEQ),
        in_specs=[blk(), blk(), cws(), cbs(), blk()], out_specs=(blk(), blk(), cws(), cbs()),
        compiler_params=_params(("parallel", "arbitrary")))(gpre, up, cw, cb, dact)


def _coords():
    return lax.axis_index("x"), lax.axis_index("y"), lax.axis_index("c")


def _dev_index(dev):
    return 4 * dev[0] + 2 * dev[1] + dev[2]


def _dma_sems(n):
    return [pltpu.SemaphoreType.DMA((n,)), pltpu.SemaphoreType.DMA((n,))]


def _gather_two_level(arrays):
    n = len(arrays)

    def plan(ins, outs, sems):
        send_sems, recv_sems, local_sems = sems
        x, y, c = _coords()
        me, sibling = (x, y, c), (x, y, 1 - c)
        chips = [(1 - x, y), (x, 1 - y), (1 - x, 1 - y)]

        def copy(a, k, block, to, own=False):
            dst = outs[a].at[_dev_index(block)]
            return pltpu.make_async_remote_copy(
                src_ref=ins[a] if own else dst, dst_ref=dst, send_sem=send_sems.at[7 * a + k],
                recv_sem=recv_sems.at[7 * a + k], device_id=to, device_id_type=MESH)

        mine = [pltpu.make_async_copy(ins[a], outs[a].at[_dev_index(me)], local_sems.at[a]) for a in range(n)]
        first = [copy(a, 0, me, sibling, own=True) for a in range(n)]
        first += [copy(a, 1 + j, me, (*chip, c), own=True) for a in range(n) for j, chip in enumerate(chips)]
        passed = [[copy(a, 4 + j, (*chip, c), sibling) for a in range(n)] for j, chip in enumerate(chips)]
        arrive_ici = [[copy(a, 1 + j, (*chip, c), me) for a in range(n)] for j, chip in enumerate(chips)]
        arrive_d2d = [copy(a, 0, sibling, me) for a in range(n)]
        arrive_d2d += [copy(a, 4 + j, (*chip, 1 - c), me) for a in range(n) for j, chip in enumerate(chips)]
        return mine, first, passed, arrive_ici, arrive_d2d

    def start(ins, outs, sems):
        mine, first, _, _, _ = plan(ins, outs, sems)
        for cp in mine + first:
            cp.start()

    def finish(ins, outs, sems):
        mine, first, passed, arrive_ici, arrive_d2d = plan(ins, outs, sems)
        for j in range(3):
            for cp in arrive_ici[j]:
                cp.wait_recv()
            for cp in passed[j]:
                cp.start()
        for cp in arrive_d2d:
            cp.wait_recv()
        for cp in first + [cp for group in passed for cp in group]:
            cp.wait_send()
        for cp in mine:
            cp.wait()

    return _Comm(arrays, [SDS((N_DEV,) + a.shape, a.dtype) for a in arrays],
                 _dma_sems(7 * n) + [pltpu.SemaphoreType.DMA((n,))], start, finish)


def _gather_direct(arrays):
    n = len(arrays)

    def plan(ins, outs, sems):
        send_sems, recv_sems, local_sems = sems
        x, y, c = _coords()
        me = (x, y, c)
        mine = [pltpu.make_async_copy(ins[a], outs[a].at[_dev_index(me)], local_sems.at[a]) for a in range(n)]
        sends, arrivals = [], []
        for a in range(n):
            for k in range(1, N_DEV):
                peer = (1 - x if k & 4 else x, 1 - y if k & 2 else y, 1 - c if k & 1 else c)
                s = 7 * a + k - 1
                for slot, out in ((me, sends), (peer, arrivals)):
                    out.append(pltpu.make_async_remote_copy(
                        src_ref=ins[a], dst_ref=outs[a].at[_dev_index(slot)], send_sem=send_sems.at[s],
                        recv_sem=recv_sems.at[s], device_id=peer, device_id_type=MESH))
        return mine, sends, arrivals

    def start(ins, outs, sems):
        mine, sends, _ = plan(ins, outs, sems)
        for cp in mine + sends:
            cp.start()

    def finish(ins, outs, sems):
        mine, sends, arrivals = plan(ins, outs, sems)
        for cp in arrivals:
            cp.wait_recv()
        for cp in sends:
            cp.wait_send()
        for cp in mine:
            cp.wait()

    return _Comm(arrays, [SDS((N_DEV,) + a.shape, a.dtype) for a in arrays],
                 _dma_sems(7 * n) + [pltpu.SemaphoreType.DMA((n,))], start, finish)


def _scatter_direct(arrays):
    n = len(arrays)

    def plan(ins, outs, sems):
        send_sems, recv_sems = sems
        x, y, c = _coords()
        cps = []
        for a in range(n):
            for k in range(1, N_DEV):
                peer = (1 - x if k & 4 else x, 1 - y if k & 2 else y, 1 - c if k & 1 else c)
                s = 7 * a + k - 1
                cps.append(pltpu.make_async_remote_copy(
                    src_ref=ins[a].at[_dev_index(peer)], dst_ref=outs[a].at[k - 1], send_sem=send_sems.at[s],
                    recv_sem=recv_sems.at[s], device_id=peer, device_id_type=MESH))
        return cps

    def start(ins, outs, sems):
        for cp in plan(ins, outs, sems):
            cp.start()

    def finish(ins, outs, sems):
        for cp in plan(ins, outs, sems):
            cp.wait()

    return _Comm(arrays, [SDS((N_DEV - 1,) + a.shape[1:], a.dtype) for a in arrays], _dma_sems(7 * n),
                 start, finish)


def _run(comm, name):
    def body(*refs):
        k_in, k_out = len(comm.inputs), len(comm.out_shapes)
        ins, outs, sems = refs[:k_in], refs[k_in:k_in + k_out], refs[k_in + k_out:]
        comm.start(ins, outs, sems)
        comm.finish(ins, outs, sems)

    return pl.pallas_call(body, name=name, out_shape=comm.out_shapes, in_specs=[ANY] * len(comm.inputs),
                          out_specs=(ANY,) * len(comm.out_shapes), scratch_shapes=comm.sem_shapes)(*comm.inputs)


TILE_ELEMS = 192 * 1024


def _row_tile(R, C):
    if R * C <= TILE_ELEMS:
        return R
    return max(t for t in range(SUBLANES, R, SUBLANES) if R % t == 0 and t * C <= TILE_ELEMS)


def _adamw_math(w, g, m, v):
    m = ADAM_B1 * m + (1.0 - ADAM_B1) * g
    v = ADAM_B2 * v + (1.0 - ADAM_B2) * (g * g)
    m_hat = m / (1.0 - ADAM_B1 ** ADAM_STEP)
    v_hat = v / (1.0 - ADAM_B2 ** ADAM_STEP)
    delta = -ADAM_LR * (m_hat / (jnp.sqrt(v_hat) + ADAM_EPS) + ADAM_WD * w)
    return delta, m, v


def _adamw_sharded(own, recv, d_idx, w, m, v, name):
    R, C = w.shape
    t = _row_tile(R, C)

    def body(k_ref, p_ref, r_ref, w_ref, m_ref, v_ref, g_ref, d_ref, nm_ref, nv_ref):
        g = p_ref[...].astype(f32)
        for j in range(N_DEV - 1):
            g = g + r_ref[j].astype(f32)
        d, nm, nv = _adamw_math(w_ref[...], g, m_ref[...], v_ref[...])
        g_ref[...], d_ref[...], nm_ref[...], nv_ref[...] = g, d, nm, nv

    tile = lambda: pl.BlockSpec((t, C), lambda i, k: (i, 0))
    return pl.pallas_call(
        body, name="adamw_" + name, out_shape=(SDS((R, C), f32),) * 4,
        grid_spec=pltpu.PrefetchScalarGridSpec(
            num_scalar_prefetch=1, grid=(R // t,),
            in_specs=[pl.BlockSpec((None, t, C), lambda i, k: (k[0], i, 0)),
                      pl.BlockSpec((N_DEV - 1, t, C), lambda i, k: (0, i, 0)), tile(), tile(), tile()],
            out_specs=(tile(), tile(), tile(), tile())),
        compiler_params=_params(("parallel",)))(d_idx, own, recv, w, m, v)


def _adamw_replicated(parts, ws, ms, vs):
    n = len(ws)

    def body(*refs):
        p, w, m, v = (refs[i * n:(i + 1) * n] for i in range(4))
        outs = refs[4 * n:]
        for a in range(n):
            g = p[a][0].astype(f32)
            for j in range(1, N_DEV):
                g = g + p[a][j].astype(f32)
            d, nm, nv = _adamw_math(w[a][...], g, m[a][...], v[a][...])
            for i, val in enumerate((g, d, nm, nv)):
                outs[i * n + a][...] = val

    shapes = tuple(SDS(w.shape, f32) for w in ws)
    res = pl.pallas_call(body, name="adamw_replicated", out_shape=shapes * 4,
                         compiler_params=_params())(*parts, *ws, *ms, *vs)
    return [res[i * n:(i + 1) * n] for i in range(4)]


def _cols_to_full(g):
    n, r, c = g.shape
    return g.transpose(1, 0, 2).reshape(r, n * c)


def _full_to_cols(a):
    r, c = a.shape
    return a.reshape(r, N_DEV, c // N_DEV).transpose(1, 0, 2)


def _rows_blocked(a):
    r, c = a.shape
    return a.reshape(N_DEV, r // N_DEV, c)


def _w_in_to_internal(w):
    K = w.shape[0]
    q = w[:, 1280:2816].reshape(K, NG, KVH, 1, HD).transpose(0, 2, 1, 3, 4).reshape(K, KVH, NG, HD)
    k = w[:, 2816:3328].reshape(K, KVH, 1, HD)
    v = w[:, 3328:3840].reshape(K, KVH, 1, HD)
    att = jnp.concatenate([q, k, v], axis=2).reshape(K, KVH * ATT_COLS)
    return jnp.concatenate([w[:, :1280], att, w[:, 3840:]], axis=1)


def _w_in_from_internal(w):
    K = w.shape[0]
    att = w[:, C_ATT:C_GATE].reshape(K, KVH, 5, HD)
    q = att[:, :, 0:3].transpose(0, 2, 1, 3).reshape(K, NG * KVH * HD)
    k = att[:, :, 3].reshape(K, KVH * HD)
    v = att[:, :, 4].reshape(K, KVH * HD)
    return jnp.concatenate([w[:, :C_ATT], q, k, v, w[:, C_GATE:]], axis=1)


_IN_NAMES = ('x', 'rel_bias', 'norm_mix_pre', 'norm_mix_post', 'w_in', 'conv_rnn_w', 'conv_rnn_b', 'w_rg_a', 'b_rg_a',
             'w_rg_x', 'b_rg_x', 'lru_lambda', 'w_branch_rnn', 'w_branch_att', 'w_out', 'norm_ffn_pre',
             'norm_ffn_post', 'w_ffn_gate', 'w_ffn_up', 'conv_ffn_w', 'conv_ffn_b', 'w_ffn_down')
_WEIGHTS = _IN_NAMES[1:]
_SHARDED = {"w_in": "col", "conv_rnn_w": "col", "w_branch_rnn": "row", "w_branch_att": "col", "w_out": "row",
            "w_ffn_gate": "col", "w_ffn_up": "col", "conv_ffn_w": "col", "w_ffn_down": "row"}
_REPLICATED = tuple(n for n in _WEIGHTS if n not in _SHARDED)


def _flat2(a):
    return a.reshape(-1, a.shape[-1])


def _train_step(inp):
    x_idx, y_idx, c_idx = _coords()
    W = {n: inp[n] for n in _WEIGHTS}
    x = inp["x"].reshape(-1, D)
    target = inp["loss_target"].reshape(-1, D)
    shard = {n: inp[n][0] for n in _SHARDED}

    hn, (g_in, g_cr, g_cf) = _norm_in(x, W["norm_mix_pre"], comm=_gather_two_level(
        [shard["w_in"].astype(bf16), shard["conv_rnn_w"], shard["conv_ffn_w"]]))
    w_in = _w_in_to_internal(_cols_to_full(g_in))
    cw_rnn, cw_ffn = _cols_to_full(g_cr), _cols_to_full(g_cf)
    behind_proj = ("w_ffn_up",)
    behind_rnn = ("w_branch_rnn", "w_branch_att", "w_out", "w_ffn_down")
    behind_att = ("w_ffn_gate",)

    wa, wx = W["w_rg_a"][0], W["w_rg_x"][0]
    buckets = jnp.asarray(_bucket_maps())

    proj, got = _mm(hn, (w_in, 0, C_GATE), "nn", f32, "mm_proj", 1024, C_GATE // 2, 1024, cols_outer=True,
                    comm=_gather_direct([shard[n].astype(bf16) for n in behind_proj]))
    gathered = dict(zip(behind_proj, got))
    gates = _mm(hn, w_in[:, C_GATE:], "nn", bf16, "mm_gates", 1024, 1024, 1024, cols_outer=True)
    rnn_saved, got = _rnn_fwd(proj, cw_rnn, W["conv_rnn_b"], wa, W["b_rg_a"], wx, W["b_rg_x"], W["lru_lambda"],
                              comm=_gather_direct([shard[n].astype(bf16) for n in behind_rnn]))
    h_rnn = rnn_saved[0]
    gathered.update(zip(behind_rnn, got))
    bias = _bias_tables(W["rel_bias"], buckets)
    (o_att, lse, *att_saved), got = _att_fwd(
        proj, bias, comm=_gather_direct([shard[n].astype(bf16) for n in behind_att]))
    gathered.update(zip(behind_att, got))
    w_brnn = gathered["w_branch_rnn"].reshape(RNN_W, D)
    w_batt = _cols_to_full(gathered["w_branch_att"])
    w_out = gathered["w_out"].reshape(D, D)
    w_gate, w_up = _cols_to_full(gathered["w_ffn_gate"]), _cols_to_full(gathered["w_ffn_up"])
    w_down = gathered["w_ffn_down"].reshape(FFN_W, D)
    pr = _mm(h_rnn, w_brnn, "nn", bf16, "mm_pr", 1024, 1024, 1280)
    pa = _mm(o_att, w_batt, "nn", bf16, "mm_pa", 1024, 1024, 512)
    merged = _merge_fwd(gates, pr, pa)
    (mix, h1, hn2), _ = _mm_rows([(merged, w_out, 1024)], "nn", "mm_mix", 1024, _mid_fwd_rows, [x],
                                 [W["norm_mix_post"], W["norm_ffn_pre"]], [f32, f32, bf16], [])
    gpre = _mm(hn2, w_gate, "nn", bf16, "mm_gate", 1024, 1024, 1024, cols_outer=True)
    up = _mm(hn2, w_up, "nn", bf16, "mm_up", 1024, 1024, 1024, cols_outer=True)
    act = _ffn_act_fwd(gpre, up, cw_ffn, W["conv_ffn_b"])
    (dy, dff), (loss_part, dg_fpost) = _mm_rows([(act, w_down, 1024)], "nn", "mm_down", 1024, _final_rows,
                                                [h1, target], [W["norm_ffn_post"]], [f32, bf16], [1, D])

    grads = {}
    dact = _mm(dff, w_down, "nt", bf16, "mm_dact", 1024, 1024, 1024, cols_outer=True)
    grads["w_ffn_down"] = _rows_blocked(_mm(act, dff, "tn", bf16, "mm_dw_down", 1024, 1024, 2048))
    dgpre, dup, dcw_ffn, dcb_ffn = _ffn_act_bwd(gpre, up, cw_ffn, W["conv_ffn_b"], dact)
    grads["conv_ffn_w"] = _full_to_cols(dcw_ffn.astype(bf16))
    grads["w_ffn_gate"] = _full_to_cols(_mm(hn2, dgpre, "tn", bf16, "mm_dw_gate", 1024, 1024, 2048))
    grads["w_ffn_up"] = _full_to_cols(_mm(hn2, dup, "tn", bf16, "mm_dw_up", 1024, 1024, 2048))
    dhn2 = _mm_nt_sum([(dgpre, w_gate, 1024), (dup, w_up, 1024)], "mm_dhn2", 1024, 1024)
    dh1, dmix, dg_fpre, dg_post = _mid_bwd(dy, dhn2, h1, W["norm_ffn_pre"], mix, W["norm_mix_post"])
    dmerged = _mm(dmix, w_out, "nt", bf16, "mm_dmerged", 1024, 1024, 1024)
    grads["w_out"] = _rows_blocked(_mm(merged, dmix, "tn", bf16, "mm_dw_out", 1024, 1024, 2048))
    dprpa, dgates = _merge_bwd(gates, pr, pa, dmerged)
    dpr, dpa = (dprpa, 0, D), (dprpa, D, D)
    dh_rnn = _mm(dpr, w_brnn, "nt", bf16, "mm_dh_rnn", 1024, 1280, 1024)
    grads["w_branch_rnn"] = _rows_blocked(_mm(h_rnn, dpr, "tn", bf16, "mm_dw_brnn", 1280, 1024, 1024))
    do_att = _mm(dpa, w_batt, "nt", f32, "mm_do_att", 1024, 512, 1024)
    grads["w_branch_att"] = _full_to_cols(_mm(o_att, dpa, "tn", bf16, "mm_dw_batt", 512, 1024, 2048))

    received = {}
    behind_att_bwd = ("w_ffn_down", "w_ffn_gate", "conv_ffn_w", "w_out")
    behind_rnn_bwd = ("w_ffn_up", "w_branch_rnn", "w_branch_att")
    (dqkv, dbias), got = _att_bwd(proj, att_saved, bias, o_att, lse, do_att,
                                  comm=_scatter_direct([grads[n] for n in behind_att_bwd]))
    received.update(zip(behind_att_bwd, got))
    drel = _bias_grad(dbias, buckets)
    (dxr, dcw_rnn, dcb_rnn, dwa, dba, dwx, dbx, dlam), got = _rnn_bwd(
        proj, rnn_saved, dh_rnn, cw_rnn, wa, wx, W["lru_lambda"],
        comm=_scatter_direct([grads[n] for n in behind_rnn_bwd]))
    received.update(zip(behind_rnn_bwd, got))
    gsmall = {"rel_bias": drel, "norm_mix_post": dg_post, "conv_rnn_b": dcb_rnn, "w_rg_a": dwa.astype(bf16),
              "b_rg_a": dba, "w_rg_x": dwx.astype(bf16), "b_rg_x": dbx, "lru_lambda": dlam,
              "norm_ffn_pre": dg_fpre, "norm_ffn_post": dg_fpost, "conv_ffn_b": dcb_ffn}
    dw_in_a, parts = _mm(hn, dqkv, "tn", bf16, "mm_dw_in_a", 1024, 1280, 1024,
                         comm=_gather_direct([_flat2(gsmall[n]) for n in gsmall]))
    parts = dict(zip(gsmall, parts))
    dw_in = jnp.concatenate([_mm(hn, dxr, "tn", bf16, "mm_dw_in_r", 1024, 1280, 1024), dw_in_a,
                             _mm(hn, dgates, "tn", bf16, "mm_dw_in_g", 1024, 1024, 2048)], axis=1)
    grads["w_in"] = _full_to_cols(_w_in_from_internal(dw_in))
    grads["conv_rnn_w"] = _full_to_cols(dcw_rnn.astype(bf16))
    behind_dhn = ("w_in", "conv_rnn_w")
    dhn, got = _mm_nt_sum([(dxr, w_in[:, :C_ATT], 1280), (dqkv, w_in[:, C_ATT:C_GATE], 1280),
                           (dgates, w_in[:, C_GATE:], 1024)], "mm_dhn", 1024, 1024,
                          comm=_scatter_direct([grads[n] for n in behind_dhn]))
    received.update(zip(behind_dhn, got))
    dx, dg_pre = _in_bwd(dh1, dhn, x, W["norm_mix_pre"])
    tail = jnp.concatenate([dg_pre, jnp.broadcast_to(loss_part, (1, LANES))], axis=1)
    tail, = _run(_gather_two_level([tail]), "ag_tail")
    parts["norm_mix_pre"] = tail[:, :, :D]
    parts = [parts[n] for n in _REPLICATED]

    out = {}
    d_arr = jnp.reshape(4 * x_idx + 2 * y_idx + c_idx, (1,)).astype(jnp.int32)
    for n in _SHARDED:
        res = _adamw_sharded(grads[n], received[n], d_arr, shard[n], inp["m_" + n][0], inp["v_" + n][0], n)
        out[n] = [r[None] for r in res]
    small = _adamw_replicated(parts, *[[_flat2(inp[p + n]) for n in _REPLICATED] for p in ("", "m_", "v_")])
    for a, n in enumerate(_REPLICATED):
        out[n] = [small[i][a].reshape(inp[n].shape) for i in range(4)]

    loss = jnp.sum(tail[:, 0, D])
    outs = [loss, dx.reshape(inp["x"].shape)]
    for i in range(4):
        outs.extend(out[n][i] for n in _WEIGHTS)
    return tuple(outs)


def kernel(x, rel_bias, norm_mix_pre, norm_mix_post, w_in, conv_rnn_w, conv_rnn_b, w_rg_a, b_rg_a, w_rg_x, b_rg_x, lru_lambda, w_branch_rnn, w_branch_att, w_out, norm_ffn_pre, norm_ffn_post, w_ffn_gate, w_ffn_up, conv_ffn_w, conv_ffn_b, w_ffn_down, loss_target, m_rel_bias, m_norm_mix_pre, m_norm_mix_post, m_w_in, m_conv_rnn_w, m_conv_rnn_b, m_w_rg_a, m_b_rg_a, m_w_rg_x, m_b_rg_x, m_lru_lambda, m_w_branch_rnn, m_w_branch_att, m_w_out, m_norm_ffn_pre, m_norm_ffn_post, m_w_ffn_gate, m_w_ffn_up, m_conv_ffn_w, m_conv_ffn_b, m_w_ffn_down, v_rel_bias, v_norm_mix_pre, v_norm_mix_post, v_w_in, v_conv_rnn_w, v_conv_rnn_b, v_w_rg_a, v_b_rg_a, v_w_rg_x, v_b_rg_x, v_lru_lambda, v_w_branch_rnn, v_w_branch_att, v_w_out, v_norm_ffn_pre, v_norm_ffn_post, v_w_ffn_gate, v_w_ffn_up, v_conv_ffn_w, v_conv_ffn_b, v_w_ffn_down):
    vals = locals()
    names = list(_IN_NAMES) + ["loss_target"] + ["m_" + n for n in _WEIGHTS] + ["v_" + n for n in _WEIGHTS]
    return _train_step({n: vals[n] for n in names})
```

```python
import functools
import math

import numpy as np
import jax
import jax.numpy as jnp
from jax import lax
from jax.experimental import pallas as pl
from jax.experimental.pallas import tpu as pltpu

f32, bf16 = jnp.float32, jnp.bfloat16
SDS = jax.ShapeDtypeStruct
MESH = pl.DeviceIdType.MESH
ANY = pl.BlockSpec(memory_space=pl.ANY)

D = 1024
SEQ = 2048
RNN_W = 1280
RNN_BLOCKS = 10
LANES = 128
SUBLANES = 8
RNN_CONV = 4
LRU_C = 8.0
HD = 128
KVH = 4
DILATIONS = (1, 4, 16)
NG = 3
ATT_BLK = 128
NBLK_SEQ = SEQ // ATT_BLK
STAT_LANE = 64
REL_BUCKETS = 32
REL_MAX_DIST = 2048
FFN_W = 3072
FFN_CONV = 3
EPS = 1e-6
IN_W = 5888
ATT_COLS = 5 * HD
C_ATT = RNN_W
C_GATE = RNN_W + KVH * ATT_COLS
NEG = -1e30

ADAM_LR, ADAM_B1, ADAM_B2, ADAM_EPS, ADAM_WD, ADAM_STEP = 0.001, 0.9, 0.999, 1e-08, 0.01, 10

VMEM_LIMIT_BYTES = 56 * 1024 * 1024
N_DEV = 8


def _params(sem=None):
    return pltpu.CompilerParams(dimension_semantics=sem, vmem_limit_bytes=VMEM_LIMIT_BYTES)


def _sigmoid(x):
    return 1.0 / (1.0 + jnp.exp(-x))


class _Comm:
    def __init__(self, inputs, out_shapes, sem_shapes, start, finish):
        self.inputs, self.out_shapes, self.sem_shapes = tuple(inputs), tuple(out_shapes), list(sem_shapes)
        self.start, self.finish = start, finish


def _call(body, args, *, name, grid, in_specs, out_specs, out_shape, scratch_shapes=(), semantics, comm=None):
    if comm is None:
        return pl.pallas_call(body, name=name, grid=grid, in_specs=list(in_specs), out_specs=tuple(out_specs),
                              out_shape=tuple(out_shape), scratch_shapes=list(scratch_shapes),
                              compiler_params=_params(semantics))(*args), ()
    n_in, n_out, n_scr = len(in_specs), len(out_shape), len(scratch_shapes)
    c_in, c_out = len(comm.inputs), len(comm.out_shapes)

    def fused(*refs):
        ins, refs = refs[:n_in], refs[n_in:]
        cin, refs = refs[:c_in], refs[c_in:]
        outs, refs = refs[:n_out], refs[n_out:]
        cout, refs = refs[:c_out], refs[c_out:]
        scr, csem = refs[:n_scr], refs[n_scr:]
        first = functools.reduce(jnp.logical_and, [pl.program_id(d) == 0 for d in range(len(grid))])
        last = functools.reduce(jnp.logical_and, [pl.program_id(d) == grid[d] - 1 for d in range(len(grid))])

        @pl.when(first)
        def _():
            comm.start(cin, cout, csem)

        body(*ins, *outs, *scr)

        @pl.when(last)
        def _():
            comm.finish(cin, cout, csem)

    res = pl.pallas_call(
        fused, name=name, grid=grid, in_specs=list(in_specs) + [ANY] * c_in,
        out_specs=tuple(out_specs) + (ANY,) * c_out, out_shape=tuple(out_shape) + comm.out_shapes,
        scratch_shapes=list(scratch_shapes) + comm.sem_shapes,
        compiler_params=_params(("arbitrary",) * len(grid)))(*args, *comm.inputs)
    return res[:n_out], res[n_out:]


_DIMS = {"nn": (((1,), (0,)), ((), ())), "nt": (((1,), (1,)), ((), ())), "tn": (((0,), (0,)), ((), ()))}


def _mm(a, b, mode, out_dtype, name, tm, tn, tk, cols_outer=False, comm=None):
    (a, a_c0, a_w), (b, b_c0, b_w) = [x if isinstance(x, tuple) else (x, 0, x.shape[1]) for x in (a, b)]
    if mode == "nn":
        (M, K), (K2, N) = (a.shape[0], a_w), (b.shape[0], b_w)
    elif mode == "nt":
        (M, K), (N, K2) = (a.shape[0], a_w), (b.shape[0], b_w)
    else:
        (K, M), (K2, N) = (a.shape[0], a_w), (b.shape[0], b_w)
    assert K == K2 and M % tm == 0 and N % tn == 0 and K % tk == 0, (name, a.shape, b.shape)
    a_tile, b_tile = (tm if mode == "tn" else tk), (tk if mode == "nt" else tn)
    assert a_c0 % a_tile == 0 and b_c0 % b_tile == 0, name
    a_off, b_off = a_c0 // a_tile, b_c0 // b_tile
    nk = K // tk

    def body(a_ref, b_ref, o_ref, *scratch):
        part = lax.dot_general(a_ref[...].astype(bf16), b_ref[...].astype(bf16), _DIMS[mode],
                               preferred_element_type=f32)
        if nk == 1:
            o_ref[...] = part.astype(o_ref.dtype)
        else:
            acc_ref, = scratch
            k = pl.program_id(2)

            @pl.when(k == 0)
            def _():
                acc_ref[...] = part

            @pl.when(k > 0)
            def _():
                acc_ref[...] += part

            @pl.when(k == nk - 1)
            def _():
                o_ref[...] = acc_ref[...].astype(o_ref.dtype)

    def ij(f):
        return (lambda j, i, k: f(i, j, k)) if cols_outer else f

    if mode == "tn":
        a_spec = pl.BlockSpec((tk, tm), ij(lambda i, j, k: (k, i + a_off)))
    else:
        a_spec = pl.BlockSpec((tm, tk), ij(lambda i, j, k: (i, k + a_off)))
    if mode == "nt":
        b_spec = pl.BlockSpec((tn, tk), ij(lambda i, j, k: (j, k + b_off)))
    else:
        b_spec = pl.BlockSpec((tk, tn), ij(lambda i, j, k: (k, j + b_off)))
    o_spec = pl.BlockSpec((tm, tn), ij(lambda i, j, k: (i, j)))
    grid = (N // tn, M // tm, nk) if cols_outer else (M // tm, N // tn, nk)
    (out,), extra = _call(
        body, (a, b), name=name, out_shape=(SDS((M, N), out_dtype),), grid=grid, in_specs=[a_spec, b_spec],
        out_specs=(o_spec,), scratch_shapes=[pltpu.VMEM((tm, tn), f32)] if nk > 1 else [],
        semantics=("parallel", "parallel", "arbitrary"), comm=comm)
    return out if comm is None else (out, extra)


def _mm_nt_sum(pairs, name, tm, tn, comm=None):
    M, N = pairs[0][0].shape[0], pairs[0][1].shape[0]
    nks = [a.shape[1] // tk for a, _, tk in pairs]
    starts = [sum(nks[:p]) for p in range(len(pairs))]
    nk = sum(nks)

    def body(*refs):
        o_ref, acc_ref = refs[-2], refs[-1]
        k = pl.program_id(2)
        for p in range(len(pairs)):
            def product(p=p):
                return lax.dot_general(refs[2 * p][...], refs[2 * p + 1][...], _DIMS["nt"], preferred_element_type=f32)

            if p == 0:
                @pl.when(k == 0)
                def _():
                    acc_ref[...] = product()

            @pl.when((k >= max(starts[p], 1)) & (k < starts[p] + nks[p]))
            def _():
                acc_ref[...] += product()

        @pl.when(k == nk - 1)
        def _():
            o_ref[...] = acc_ref[...].astype(bf16)

    in_specs, args = [], []
    for (a, b, tk), k0, n in zip(pairs, starts, nks):
        assert a.shape[1] == b.shape[1] and a.shape[1] % tk == 0 and a.dtype == b.dtype == bf16, name
        chunk = lambda k, k0=k0, n=n: jnp.clip(k - k0, 0, n - 1)
        in_specs += [pl.BlockSpec((tm, tk), lambda i, j, k, c=chunk: (i, c(k))),
                     pl.BlockSpec((tn, tk), lambda i, j, k, c=chunk: (j, c(k)))]
        args += [a, b]
    o_spec = pl.BlockSpec((tm, tn), lambda i, j, k: (i, j))
    (out,), extra = _call(
        body, args, name=name, out_shape=(SDS((M, N), bf16),), grid=(M // tm, N // tn, nk), in_specs=in_specs,
        out_specs=(o_spec,), scratch_shapes=[pltpu.VMEM((tm, tn), f32)],
        semantics=("parallel", "parallel", "arbitrary"), comm=comm)
    return out if comm is None else (out, extra)


def _mm_rows(pairs, mode, name, tm, epilogue, rows_in, vecs_in, rows_out, vecs_out, comm=None):
    M = pairs[0][0].shape[0]
    N = pairs[0][1].shape[1 if mode == "nn" else 0]
    nks = [a.shape[1] // tk for a, _, tk in pairs]
    starts = [sum(nks[:p]) for p in range(len(pairs))]
    nk = sum(nks)
    n_rows_in, n_vecs_in, n_rows_out = len(rows_in), len(vecs_in), len(rows_out)

    def body(*refs):
        pair_refs, refs = refs[:2 * len(pairs)], refs[2 * len(pairs):]
        rin, refs = refs[:n_rows_in], refs[n_rows_in:]
        vin, refs = refs[:n_vecs_in], refs[n_vecs_in:]
        rout, refs = refs[:n_rows_out], refs[n_rows_out:]
        vout, acc_ref = refs[:-1], refs[-1]
        i, k = pl.program_id(0), pl.program_id(1)
        for p in range(len(pairs)):
            def product(p=p):
                return lax.dot_general(pair_refs[2 * p][...], pair_refs[2 * p + 1][...], _DIMS[mode],
                                       preferred_element_type=f32)

            if p == 0:
                @pl.when(k == 0)
                def _():
                    acc_ref[...] = product()

            @pl.when((k >= max(starts[p], 1)) & (k < starts[p] + nks[p]))
            def _():
                acc_ref[...] += product()

        @pl.when(k == nk - 1)
        def _():
            res = epilogue(acc_ref[...], *[r[...] for r in rin], *[v[...] for v in vin])
            for ref, val in zip(rout, res[:n_rows_out]):
                ref[...] = val.astype(ref.dtype)
            for ref, val in zip(vout, res[n_rows_out:]):
                @pl.when(i == 0)
                def _(ref=ref, val=val):
                    ref[...] = val

                @pl.when(i > 0)
                def _(ref=ref, val=val):
                    ref[...] += val

    in_specs, args = [], []
    for (a, b, tk), k0, n in zip(pairs, starts, nks):
        assert a.shape[1] % tk == 0 and a.dtype == b.dtype == bf16, name
        chunk = lambda k, k0=k0, n=n: jnp.clip(k - k0, 0, n - 1)
        in_specs.append(pl.BlockSpec((tm, tk), lambda i, k, c=chunk: (i, c(k))))
        if mode == "nn":
            in_specs.append(pl.BlockSpec((tk, N), lambda i, k, c=chunk: (c(k), 0)))
        else:
            in_specs.append(pl.BlockSpec((N, tk), lambda i, k, c=chunk: (0, c(k))))
        args += [a, b]
    row = lambda: pl.BlockSpec((tm, N), lambda i, k: (i, 0))
    vec = lambda w: pl.BlockSpec((1, w), lambda i, k: (0, 0))
    in_specs += [row() for _ in rows_in] + [vec(v.shape[1]) for v in vecs_in]
    outs, extra = _call(
        body, (*args, *rows_in, *vecs_in), name=name,
        out_shape=tuple(SDS((M, N), dt) for dt in rows_out) + tuple(SDS((1, w), f32) for w in vecs_out),
        grid=(M // tm, nk), in_specs=in_specs,
        out_specs=tuple(row() for _ in rows_out) + tuple(vec(w) for w in vecs_out),
        scratch_shapes=[pltpu.VMEM((tm, N), f32)], semantics=("arbitrary", "arbitrary"), comm=comm)
    res = (outs[:n_rows_out], outs[n_rows_out:])
    return res if comm is None else (res, extra)


ROW_TILE = 512


def _rms_fwd(x, g):
    r = lax.rsqrt(jnp.mean(x * x, axis=-1, keepdims=True) + EPS)
    return x * r * g


def _rms_bwd(x, g, dy):
    r = lax.rsqrt(jnp.mean(x * x, axis=-1, keepdims=True) + EPS)
    xh = x * r
    dxh = dy * g
    dx = r * (dxh - xh * jnp.mean(dxh * xh, axis=-1, keepdims=True))
    return dx, jnp.sum(dy * xh, axis=0, keepdims=True)


def _acc_out(ref, val):
    @pl.when(pl.program_id(0) == 0)
    def _():
        ref[...] = val

    @pl.when(pl.program_id(0) > 0)
    def _():
        ref[...] += val


def _row_spec(width=D):
    return pl.BlockSpec((ROW_TILE, width), lambda i: (i, 0))


def _vec_spec(width=D):
    return pl.BlockSpec((1, width), lambda i: (0, 0))


def _norm_in(x, g, comm=None):
    def body(x_ref, g_ref, o_ref):
        o_ref[...] = _rms_fwd(x_ref[...], g_ref[...]).astype(bf16)

    T = x.shape[0]
    (hn,), extra = _call(body, (x, g), name="norm_in", out_shape=(SDS((T, D), bf16),), grid=(T // ROW_TILE,),
                         in_specs=[_row_spec(), _vec_spec()], out_specs=(_row_spec(),), semantics=("parallel",),
                         comm=comm)
    return hn, extra


def _mid_bwd(dy, dhn2, h1, g_fpre, mix, g_post):
    def body(dy_ref, dhn2_ref, h1_ref, gf_ref, mix_ref, gp_ref, dh1_ref, dmix_ref, dgf_ref, dgp_ref):
        d1, dgf = _rms_bwd(h1_ref[...], gf_ref[...], dhn2_ref[...].astype(f32))
        dh1 = dy_ref[...] + d1
        dh1_ref[...] = dh1
        dmix, dgp = _rms_bwd(mix_ref[...], gp_ref[...], dh1)
        dmix_ref[...] = dmix.astype(bf16)
        _acc_out(dgf_ref, dgf)
        _acc_out(dgp_ref, dgp)

    T = dy.shape[0]
    return pl.pallas_call(
        body, name="mid_bwd", out_shape=(SDS((T, D), f32), SDS((T, D), bf16), SDS((1, D), f32), SDS((1, D), f32)),
        grid=(T // ROW_TILE,),
        in_specs=[_row_spec(), _row_spec(), _row_spec(), _vec_spec(), _row_spec(), _vec_spec()],
        out_specs=(_row_spec(), _row_spec(), _vec_spec(), _vec_spec()),
        compiler_params=_params(("arbitrary",)))(dy, dhn2, h1, g_fpre, mix, g_post)


def _in_bwd(dh1, dhn, x, g_pre):
    def body(dh1_ref, dhn_ref, x_ref, g_ref, dx_ref, dg_ref):
        d, dg = _rms_bwd(x_ref[...], g_ref[...], dhn_ref[...].astype(f32))
        dx_ref[...] = dh1_ref[...] + d
        _acc_out(dg_ref, dg)

    T = x.shape[0]
    return pl.pallas_call(
        body, name="in_bwd", out_shape=(SDS((T, D), f32), SDS((1, D), f32)), grid=(T // ROW_TILE,),
        in_specs=[_row_spec(), _row_spec(), _row_spec(), _vec_spec()], out_specs=(_row_spec(), _vec_spec()),
        compiler_params=_params(("arbitrary",)))(dh1, dhn, x, g_pre)


def _mid_fwd_rows(mix, x, g_post, g_fpre):
    h1 = x + _rms_fwd(mix, g_post)
    return mix, h1, _rms_fwd(h1, g_fpre)


def _final_rows(ff, h1, target, g_fpost):
    e = h1 + _rms_fwd(ff, g_fpost) - target
    part = jnp.sum(jnp.sum(e * e, axis=1, keepdims=True), axis=0, keepdims=True) * (0.5 / D)
    dy = e * (1.0 / D)
    dff, dg = _rms_bwd(ff, g_fpost, dy)
    return dy, dff, part, dg


def _shift_dn(x, d, row, fill=0.0):
    if d == 0:
        return x
    y = pltpu.roll(x, d, 0)
    head = jnp.where(row[:SUBLANES] >= d, y[:SUBLANES], fill)
    return jnp.concatenate([head, y[SUBLANES:]], axis=0)


def _shift_up(x, d, row, fill=0.0):
    if d == 0:
        return x
    n = x.shape[0]
    y = pltpu.roll(x, n - d, 0)
    tail = jnp.where(row[:SUBLANES] < SUBLANES - d, y[n - SUBLANES:], fill)
    return jnp.concatenate([y[:n - SUBLANES], tail], axis=0)


def _conv_fwd(x, w_ref, b, row):
    K = w_ref.shape[0]
    y = b
    for k in range(K):
        y = y + w_ref[k:k + 1, :] * _shift_dn(x, K - 1 - k, row)
    return y


def _conv_bwd(x, w_ref, dy, row):
    K = w_ref.shape[0]
    dx = jnp.zeros_like(dy)
    dws = []
    for k in range(K):
        dx = dx + w_ref[k:k + 1, :] * _shift_up(dy, K - 1 - k, row)
        dws.append(jnp.sum(dy * _shift_dn(x, K - 1 - k, row), axis=0, keepdims=True))
    return dx, dws, jnp.sum(dy, axis=0, keepdims=True)


def _scan_fwd(a, u, row):
    n = a.shape[0]
    d = 1
    while d < n:
        last = 2 * d >= n
        if d < SUBLANES:
            u = u + a * _shift_dn(u, d, row)
            if not last:
                a = a * _shift_dn(a, d, row, fill=1.0)
        else:
            u = jnp.concatenate([u[:d], u[d:] + a[d:] * u[:n - d]], axis=0)
            if not last:
                a = jnp.concatenate([a[:d], a[d:] * a[:n - d]], axis=0)
        d *= 2
    return u


def _scan_bwd(b, u, row):
    n = b.shape[0]
    d = 1
    while d < n:
        last = 2 * d >= n
        if d < SUBLANES:
            u = u + b * _shift_up(u, d, row)
            if not last:
                b = b * _shift_up(b, d, row, fill=1.0)
        else:
            u = jnp.concatenate([u[:n - d] + b[:n - d] * u[d:], u[n - d:]], axis=0)
            if not last:
                b = jnp.concatenate([b[:n - d] * b[d:], b[n - d:]], axis=0)
        d *= 2
    return u


def _neg_expm1(z):
    series = -z * (1.0 + z * (0.5 + z * (1.0 / 6.0 + z * (1.0 / 24.0 + z * (1.0 / 120.0)))))
    return jnp.where(z > -0.1, series, 1.0 - jnp.exp(z))


def _softplus_neg(lam):
    z = -lam
    return jnp.maximum(z, 0.0) + jnp.log(1.0 + jnp.exp(-jnp.abs(z)))


def _rnn_specs(B):
    blk = lambda: pl.BlockSpec((SEQ, LANES), lambda b, n: (b, n))
    return dict(
        act=blk,
        convw=pl.BlockSpec((RNN_CONV, LANES), lambda b, n: (0, n)),
        vec=lambda: pl.BlockSpec((1, LANES), lambda b, n: (0, n)),
        gate=lambda: pl.BlockSpec((None, LANES, LANES), lambda b, n: (n, 0, 0)),
    )


def _rnn_fwd(proj, cw, cb, wa, ba, wx, bx, lam, comm=None):
    T = proj.shape[0]
    B = T // SEQ

    def body(x_ref, cw_ref, cb_ref, wa_ref, ba_ref, wx_ref, bx_ref, lam_ref, h_ref, xc_ref, r_ref, i_ref, a_ref, s_ref,
             hb_ref):
        row = lax.broadcasted_iota(jnp.int32, (SEQ, LANES), 0)
        xc = _conv_fwd(x_ref[...], cw_ref, cb_ref[...], row)
        xcb = xc.astype(bf16)
        r = _sigmoid(jnp.dot(xcb, wa_ref[...].astype(bf16), preferred_element_type=f32) + ba_ref[...])
        i = _sigmoid(jnp.dot(xcb, wx_ref[...].astype(bf16), preferred_element_type=f32) + bx_ref[...])
        log_a = (-LRU_C * _softplus_neg(lam_ref[...])) * r
        a = jnp.exp(log_a)
        s = jnp.sqrt(_neg_expm1(2.0 * log_a))
        xc_ref[...], r_ref[...], i_ref[...], a_ref[...], s_ref[...] = xc, r, i, a, s
        h = _scan_fwd(a, s * (i * xc), row)
        h_ref[...] = h
        hb_ref[...] = h.astype(bf16)

    sp_ = _rnn_specs(B)
    return _call(
        body, (proj, cw, cb, wa, ba, wx, bx, lam), name="rnn_fwd",
        out_shape=(SDS((T, RNN_W), f32),) * 6 + (SDS((T, RNN_W), bf16),), grid=(B, RNN_BLOCKS),
        in_specs=[sp_["act"](), sp_["convw"], sp_["vec"](), sp_["gate"](), sp_["vec"](), sp_["gate"](),
                  sp_["vec"](), sp_["vec"]()],
        out_specs=tuple(sp_["act"]() for _ in range(7)), semantics=("parallel", "parallel"), comm=comm)


def _rnn_bwd(proj, saved, dh, cw, wa, wx, lam, comm=None):
    T = proj.shape[0]
    B = T // SEQ

    def body(x_ref, h_ref, xc_ref, r_ref, i_ref, a_ref, s_ref, dh_ref, cw_ref, wa_ref, wx_ref, lam_ref,
             dx_ref, dcw_ref, dcb_ref, dwa_ref, dba_ref, dwx_ref, dbx_ref, dlam_ref):
        row = lax.broadcasted_iota(jnp.int32, (SEQ, LANES), 0)
        xr = x_ref[...]
        wa, wx, lam = wa_ref[...], wx_ref[...], lam_ref[...]
        xc, r, i, a, s = xc_ref[...], r_ref[...], i_ref[...], a_ref[...], s_ref[...]
        xcb = xc.astype(bf16)
        sp = _softplus_neg(lam)
        hprev = _shift_dn(h_ref[...], 1, row)
        g = _scan_bwd(_shift_up(a, 1, row), dh_ref[...].astype(f32), row)
        da = g * hprev
        ds = g * (i * xc)
        di = g * (s * xc)
        dxc = g * (s * i)
        dla = da * a - ds * (a * a) / s
        dr = dla * (-LRU_C * sp)
        dsp = jnp.sum(dla * (-LRU_C * r), axis=0, keepdims=True)
        dlam = -dsp * _sigmoid(-lam)
        dga = dr * r * (1.0 - r)
        dgx = di * i * (1.0 - i)
        dgab, dgxb = dga.astype(bf16), dgx.astype(bf16)
        dwa = lax.dot_general(xcb, dgab, _DIMS["tn"], preferred_element_type=f32)
        dwx = lax.dot_general(xcb, dgxb, _DIMS["tn"], preferred_element_type=f32)
        dxc = dxc + lax.dot_general(dgab, wa.astype(bf16), _DIMS["nt"], preferred_element_type=f32)
        dxc = dxc + lax.dot_general(dgxb, wx.astype(bf16), _DIMS["nt"], preferred_element_type=f32)
        dx, dws, db = _conv_bwd(xr, cw_ref, dxc, row)
        dx_ref[...] = dx.astype(bf16)
        first = pl.program_id(1) == 0

        def acc(ref, val):
            @pl.when(first)
            def _():
                ref[...] = val

            @pl.when(jnp.logical_not(first))
            def _():
                ref[...] += val

        for k in range(RNN_CONV):
            acc(dcw_ref.at[k:k + 1, :], dws[k])
        acc(dcb_ref, db)
        acc(dwa_ref, dwa)
        acc(dba_ref, jnp.sum(dga, axis=0, keepdims=True))
        acc(dwx_ref, dwx)
        acc(dbx_ref, jnp.sum(dgx, axis=0, keepdims=True))
        acc(dlam_ref, dlam)

    blk = lambda: pl.BlockSpec((SEQ, LANES), lambda n, b: (b, n))
    convw = lambda: pl.BlockSpec((RNN_CONV, LANES), lambda n, b: (0, n))
    vec = lambda: pl.BlockSpec((1, LANES), lambda n, b: (0, n))
    gate = lambda: pl.BlockSpec((None, LANES, LANES), lambda n, b: (n, 0, 0))
    vshape = SDS((1, RNN_W), f32)
    gshape = SDS((RNN_BLOCKS, LANES, LANES), f32)
    return _call(
        body, (proj, *saved, dh, cw, wa, wx, lam), name="rnn_bwd",
        out_shape=(SDS((T, RNN_W), bf16), SDS((RNN_CONV, RNN_W), f32), vshape, gshape, vshape, gshape, vshape, vshape),
        grid=(RNN_BLOCKS, B),
        in_specs=[blk() for _ in range(8)] + [convw(), gate(), gate(), vec()],
        out_specs=(blk(), convw(), vec(), gate(), vec(), gate(), vec(), vec()),
        semantics=("parallel", "arbitrary"), comm=comm)


def _t5_bucket(dist):
    max_exact = REL_BUCKETS // 2
    d = np.maximum(dist, 1).astype(np.float32)
    large = max_exact + np.log(d / max_exact) / math.log(REL_MAX_DIST / max_exact) * (REL_BUCKETS - max_exact)
    large = np.minimum(large.astype(np.int32), REL_BUCKETS - 1)
    return np.where(dist < max_exact, dist, large).astype(np.int32)


def _bucket_maps():
    qi = np.arange(ATT_BLK)[:, None]
    kj = np.arange(2 * ATT_BLK)[None, :]
    delta = ATT_BLK + qi - kj
    valid = (delta >= 0) & (delta <= ATT_BLK)
    maps = [np.where(valid, _t5_bucket(np.maximum(delta, 0) * r), -1) for r in DILATIONS]
    return np.stack(maps).astype(np.int32)


def _bias_tables(rel_bias, buckets):
    def body(rb_ref, bk_ref, o_ref):
        for g in range(NG):
            bk = bk_ref[g]
            for h in range(KVH):
                acc = jnp.full(bk.shape, NEG, f32)
                for b in range(REL_BUCKETS):
                    acc = jnp.where(bk == b, rb_ref[b, g * KVH + h], acc)
                o_ref[h, g] = acc

    return pl.pallas_call(
        body, name="bias_tables", out_shape=SDS((KVH, NG, ATT_BLK, 2 * ATT_BLK), f32),
        in_specs=[pl.BlockSpec(memory_space=pltpu.SMEM), pl.BlockSpec(memory_space=pltpu.VMEM)],
        out_specs=pl.BlockSpec(memory_space=pltpu.VMEM), compiler_params=_params())(rel_bias, buckets)


def _bias_grad(dbias, buckets):
    def body(db_ref, bk_ref, o_ref):
        rr = lax.broadcasted_iota(jnp.int32, (REL_BUCKETS, NG * KVH), 0)
        cc = lax.broadcasted_iota(jnp.int32, (REL_BUCKETS, NG * KVH), 1)
        out = jnp.zeros((REL_BUCKETS, NG * KVH), f32)
        for g in range(NG):
            bk = bk_ref[g]
            for h in range(KVH):
                d = db_ref[h, g]
                for b in range(REL_BUCKETS):
                    m = jnp.where(bk == b, d, 0.0)
                    s = jnp.sum(jnp.sum(m, axis=1, keepdims=True), axis=0, keepdims=True)
                    out = jnp.where((rr == b) & (cc == g * KVH + h), s, out)
        o_ref[...] = out

    return pl.pallas_call(body, name="bias_grad", out_shape=SDS((REL_BUCKETS, NG * KVH), f32),
                          compiler_params=_params())(dbias, buckets)


def _to_sub(dst_ref, src_ref, r, dtype, offset=0):
    M = SEQ // r
    for c in range(r):
        if r == 1:
            v = src_ref[...]
        else:
            v = src_ref[pl.ds(c, M, stride=r), :]
        dst_ref[pl.ds(offset + c * M, M), :] = v.astype(dtype)


def _from_sub(dst_ref, src_ref, r, accumulate=False, offset=0):
    M = SEQ // r
    for c in range(r):
        v = src_ref[pl.ds(offset + c * M, M), :]
        idx = slice(None) if r == 1 else pl.ds(c, M, stride=r)
        if accumulate:
            dst_ref[idx, :] = dst_ref[idx, :] + v
        else:
            dst_ref[idx, :] = v


_COL = lambda k: slice(k * HD, (k + 1) * HD)
SCALE = HD ** -0.5


def _qkv_spec(k, bh):
    def index(*ids):
        b, h = bh(*ids)
        return (b, C_ATT // HD + 5 * h + k)

    return pl.BlockSpec((SEQ, HD), index)


def _key_window(bias_ref, g, nb):
    if nb == 1:
        bias_own = bias_ref[g, :, ATT_BLK:2 * ATT_BLK]
        return lambda j: (pl.ds(pl.multiple_of((j + 1) * ATT_BLK, ATT_BLK), ATT_BLK), bias_own)
    bias_g = bias_ref[g]
    col = lax.broadcasted_iota(jnp.int32, bias_g.shape, 1)
    bias_first = jnp.where(col >= ATT_BLK, bias_g, NEG)
    return lambda j: (pl.ds(pl.multiple_of(j * ATT_BLK, ATT_BLK), 2 * ATT_BLK),
                      jnp.where(j % nb != 0, bias_g, bias_first))


def _att_fwd(proj, bias, comm=None):
    T = proj.shape[0]
    B = T // SEQ

    def body(q0_ref, q1_ref, q2_ref, k_ref, v_ref, bias_ref, o_ref, lse_ref, *rest):
        saved, (qp, kp, vp, kt, op, lp, og, lg) = rest[:3 * (NG - 1)], rest[3 * (NG - 1):]
        q_refs = (q0_ref, q1_ref, q2_ref)
        kp[0:ATT_BLK, :] = jnp.zeros((ATT_BLK, HD), bf16)
        vp[0:ATT_BLK, :] = jnp.zeros((ATT_BLK, HD), bf16)
        for g, r in enumerate(DILATIONS):
            nb = NBLK_SEQ // r
            _to_sub(qp, q_refs[g], r, bf16)
            _to_sub(kp, k_ref, r, bf16, offset=ATT_BLK)
            _to_sub(vp, v_ref, r, bf16, offset=ATT_BLK)
            if r > 1:
                sq, sk, sv = saved[3 * (g - 1):3 * g]
                sq[...], sk[...], sv[...] = qp[...], kp[ATT_BLK:, :], vp[ATT_BLK:, :]
            kt[...] = kp[...].T
            keys = _key_window(bias_ref, g, nb)

            def step(j, carry):
                cur = pl.ds(pl.multiple_of(j * ATT_BLK, ATT_BLK), ATT_BLK)
                win, bias_j = keys(j)
                s = jnp.dot(qp[cur, :], kt[:, win], preferred_element_type=f32) * SCALE + bias_j
                m = jnp.max(s, axis=-1, keepdims=True)
                p = jnp.exp(s - m)
                den = jnp.sum(p, axis=-1, keepdims=True)
                o = jnp.dot(p.astype(bf16), vp[win, :], preferred_element_type=f32)
                op[cur, :] = o / den
                lp[cur, :] = jnp.broadcast_to(m + jnp.log(den), (ATT_BLK, HD))
                return carry

            lax.fori_loop(0, NBLK_SEQ, step, 0, unroll=NBLK_SEQ)
            _from_sub(og.at[g], op, r)
            _from_sub(lg.at[g], lp, r)
        l0, l1, l2 = lg[0], lg[1], lg[2]
        mx = jnp.maximum(jnp.maximum(l0, l1), l2)
        e0, e1, e2 = jnp.exp(l0 - mx), jnp.exp(l1 - mx), jnp.exp(l2 - mx)
        den = e0 + e1 + e2
        o_ref[...] = (e0 * og[0] + e1 * og[1] + e2 * og[2]) / den
        lse_ref[...] = mx + jnp.log(den)

    return _call(
        body, (proj, proj, proj, proj, proj, bias), name="att_fwd",
        out_shape=(SDS((T, KVH * HD), f32), SDS((KVH, T, HD), f32)) + (SDS((T, KVH * HD), bf16),) * (3 * (NG - 1)),
        grid=(B, KVH),
        in_specs=[_qkv_spec(k, lambda b, h: (b, h)) for k in range(5)]
                 + [pl.BlockSpec((None, NG, ATT_BLK, 2 * ATT_BLK), lambda b, h: (h, 0, 0, 0))],
        out_specs=(pl.BlockSpec((SEQ, HD), lambda b, h: (b, h)),
                   pl.BlockSpec((None, SEQ, HD), lambda b, h: (h, b, 0)))
                  + tuple(pl.BlockSpec((SEQ, HD), lambda b, h: (b, h)) for _ in range(3 * (NG - 1))),
        scratch_shapes=[pltpu.VMEM((SEQ, HD), bf16)] + [pltpu.VMEM((SEQ + ATT_BLK, HD), bf16)] * 2
                       + [pltpu.VMEM((HD, SEQ + ATT_BLK), bf16)]
                       + [pltpu.VMEM((SEQ, HD), f32)] * 2 + [pltpu.VMEM((NG, SEQ, HD), f32)] * 2,
        semantics=("parallel", "parallel"), comm=comm)


def _att_bwd(proj, saved, bias, o, lse, do, comm=None):
    T = proj.shape[0]
    B = T // SEQ
    n_saved = 3 * (NG - 1)

    def body(q0_ref, k_ref, v_ref, *rest):
        saved_refs, rest = rest[:n_saved], rest[n_saved:]
        (bias_ref, o_ref, lse_ref, do_ref, dx_ref, db_ref,
         qp, kp, vp, dop, qt, kt, vt, dot, lp, dqp, dkt, dvt, dln, nat, dkn, dvn) = rest
        first = pl.program_id(1) == 0

        @pl.when(first)
        def _():
            db_ref[...] = jnp.zeros_like(db_ref)

        lane = lax.broadcasted_iota(jnp.int32, (SEQ, HD), 1)
        dln[...] = jnp.where(lane < STAT_LANE, lse_ref[...],
                             jnp.sum(do_ref[...] * o_ref[...], axis=-1, keepdims=True))
        dkn[...] = jnp.zeros_like(dkn)
        dvn[...] = jnp.zeros_like(dvn)
        kp[0:ATT_BLK, :] = jnp.zeros((ATT_BLK, HD), bf16)
        vp[0:ATT_BLK, :] = jnp.zeros((ATT_BLK, HD), bf16)
        for g, r in enumerate(DILATIONS):
            nb = NBLK_SEQ // r
            if r == 1:
                qp[...] = q0_ref[...].astype(bf16)
                kp[ATT_BLK:, :] = k_ref[...].astype(bf16)
                vp[ATT_BLK:, :] = v_ref[...].astype(bf16)
            else:
                sq, sk, sv = saved_refs[3 * (g - 1):3 * g]
                qp[...], kp[ATT_BLK:, :], vp[ATT_BLK:, :] = sq[...], sk[...], sv[...]
            _to_sub(dop, do_ref, r, bf16)
            _to_sub(lp, dln, r, f32)
            qt[...], kt[...], vt[...], dot[...] = qp[...].T, kp[...].T, vp[...].T, dop[...].T
            dkt[...] = jnp.zeros_like(dkt)
            dvt[...] = jnp.zeros_like(dvt)
            keys = _key_window(bias_ref, g, nb)
            db_cols = slice(ATT_BLK, 2 * ATT_BLK) if nb == 1 else slice(None)

            def step(j, carry):
                cur = pl.ds(pl.multiple_of(j * ATT_BLK, ATT_BLK), ATT_BLK)
                win, bias_j = keys(j)
                s = jnp.dot(qp[cur, :], kt[:, win], preferred_element_type=f32) * SCALE + bias_j
                p = jnp.exp(s - lp[cur, 0:1])
                dp = jnp.dot(dop[cur, :], vt[:, win], preferred_element_type=f32)
                ds = p * (dp - lp[cur, STAT_LANE:STAT_LANE + 1])
                db_ref[g, :, db_cols] += ds
                dsb, pb = ds.astype(bf16), p.astype(bf16)
                dqp[cur, :] = jnp.dot(dsb, kp[win, :], preferred_element_type=f32) * SCALE
                dkt[:, win] += jnp.dot(qt[:, cur], dsb, preferred_element_type=f32) * SCALE
                dvt[:, win] += jnp.dot(dot[:, cur], pb, preferred_element_type=f32)
                return carry

            lax.fori_loop(0, NBLK_SEQ, step, 0, unroll=NBLK_SEQ)
            _from_sub(nat, dqp, r)
            dx_ref[:, _COL(g)] = nat[...].astype(bf16)
            dqp[...] = dkt[:, ATT_BLK:].T
            _from_sub(dkn, dqp, r, accumulate=True)
            dqp[...] = dvt[:, ATT_BLK:].T
            _from_sub(dvn, dqp, r, accumulate=True)
        dx_ref[:, _COL(3)] = dkn[...].astype(bf16)
        dx_ref[:, _COL(4)] = dvn[...].astype(bf16)

    blk = lambda: pl.BlockSpec((SEQ, HD), lambda h, b: (b, h))
    bias_spec = lambda: pl.BlockSpec((None, NG, ATT_BLK, 2 * ATT_BLK), lambda h, b: (h, 0, 0, 0))
    pad = lambda dtype: pltpu.VMEM((SEQ + ATT_BLK, HD), dtype)
    pad_t = lambda dtype: pltpu.VMEM((HD, SEQ + ATT_BLK), dtype)
    seq_t = pltpu.VMEM((HD, SEQ), bf16)
    return _call(
        body, (proj, proj, proj, *saved, bias, o, lse, do), name="att_bwd",
        out_shape=(SDS((T, KVH * ATT_COLS), bf16), SDS((KVH, NG, ATT_BLK, 2 * ATT_BLK), f32)), grid=(KVH, B),
        in_specs=[_qkv_spec(k, lambda h, b: (b, h)) for k in (0, 3, 4)] + [blk() for _ in range(n_saved)]
                 + [bias_spec(), blk(), pl.BlockSpec((None, SEQ, HD), lambda h, b: (h, b, 0)), blk()],
        out_specs=(pl.BlockSpec((SEQ, ATT_COLS), lambda h, b: (b, h)), bias_spec()),
        scratch_shapes=[pltpu.VMEM((SEQ, HD), bf16), pad(bf16), pad(bf16), pltpu.VMEM((SEQ, HD), bf16),
                        seq_t, pad_t(bf16), pad_t(bf16), seq_t]
                       + [pltpu.VMEM((SEQ, HD), f32)] * 2 + [pad_t(f32)] * 2 + [pltpu.VMEM((SEQ, HD), f32)] * 4,
        semantics=("parallel", "arbitrary"), comm=comm)


MERGE_ROWS, MERGE_COLS = 1024, 512


def _merge_fwd(gates, pr, pa):
    def body(gr_ref, ga_ref, pr_ref, pa_ref, o_ref):
        o_ref[...] = (_sigmoid(gr_ref[...].astype(f32)) * pr_ref[...].astype(f32)
                      + _sigmoid(ga_ref[...].astype(f32)) * pa_ref[...].astype(f32)).astype(bf16)

    T = gates.shape[0]
    cols = lambda off: pl.BlockSpec((MERGE_ROWS, MERGE_COLS), lambda i, j: (i, off + j))
    return pl.pallas_call(body, name="merge_fwd", out_shape=SDS((T, D), bf16),
                          grid=(T // MERGE_ROWS, D // MERGE_COLS),
                          in_specs=[cols(0), cols(D // MERGE_COLS), cols(0), cols(0)], out_specs=cols(0),
                          compiler_params=_params(("parallel", "parallel")))(gates, gates, pr, pa)


def _merge_bwd(gates, pr, pa, dm):
    nj = D // MERGE_COLS

    def body(g_ref, pr_ref, pa_ref, dm_ref, dp_ref, dg_ref):
        dm_ = dm_ref[...].astype(f32)
        s = _sigmoid(g_ref[...].astype(f32))
        p = jnp.where(pl.program_id(1) < nj, pr_ref[...], pa_ref[...]).astype(f32)
        dp_ref[...] = (dm_ * s).astype(bf16)
        dg_ref[...] = (dm_ * p * s * (1.0 - s)).astype(bf16)

    T = gates.shape[0]
    blk = (MERGE_ROWS, MERGE_COLS)
    wrap = pl.BlockSpec(blk, lambda i, j: (i, j % nj))
    pr_spec = pl.BlockSpec(blk, lambda i, j: (i, jnp.minimum(j, nj - 1)))
    pa_spec = pl.BlockSpec(blk, lambda i, j: (i, jnp.maximum(j - nj, 0)))
    out = pl.BlockSpec(blk, lambda i, j: (i, j))
    return pl.pallas_call(
        body, name="merge_bwd", out_shape=(SDS((T, 2 * D), bf16), SDS((T, 2 * D), bf16)),
        grid=(T // MERGE_ROWS, 2 * nj),
        in_specs=[out, pr_spec, pa_spec, wrap], out_specs=(out, out),
        compiler_params=_params(("parallel", "parallel")))(gates, pr, pa, dm)


FFN_COLS = 256
GELU_C = math.sqrt(2.0 / math.pi)
GELU_A = 0.044715


def _gelu_parts(x):
    q = x * x
    t = jnp.tanh(x * (GELU_C + (GELU_C * GELU_A) * q))
    h = 0.5 + 0.5 * t
    return x * h, h * (1.0 + x * (1.0 - t) * (GELU_C + (3.0 * GELU_C * GELU_A) * q))


def _ffn_act_fwd(gpre, up, cw, cb):
    def body(g_ref, u_ref, cw_ref, cb_ref, o_ref):
        row = lax.broadcasted_iota(jnp.int32, (SEQ, FFN_COLS), 0)
        gate = _conv_fwd(g_ref[...].astype(f32), cw_ref, cb_ref[...], row)
        o_ref[...] = (_gelu_parts(gate)[0] * u_ref[...].astype(f32)).astype(bf16)

    T = gpre.shape[0]
    blk = lambda: pl.BlockSpec((SEQ, FFN_COLS), lambda b, j: (b, j))
    return pl.pallas_call(
        body, name="ffn_act_fwd", out_shape=SDS((T, FFN_W), bf16), grid=(T // SEQ, FFN_W // FFN_COLS),
        in_specs=[blk(), blk(), pl.BlockSpec((FFN_CONV, FFN_COLS), lambda b, j: (0, j)),
                  pl.BlockSpec((1, FFN_COLS), lambda b, j: (0, j))],
        out_specs=blk(), compiler_params=_params(("parallel", "parallel")))(gpre, up, cw, cb)


def _ffn_act_bwd(gpre, up, cw, cb, dact):
    def body(g_ref, u_ref, cw_ref, cb_ref, da_ref, dg_ref, du_ref, dcw_ref, dcb_ref):
        row = lax.broadcasted_iota(jnp.int32, (SEQ, FFN_COLS), 0)
        gp = g_ref[...].astype(f32)
        gate = _conv_fwd(gp, cw_ref, cb_ref[...], row)
        gel, dgel = _gelu_parts(gate)
        da = da_ref[...].astype(f32)
        du_ref[...] = (da * gel).astype(bf16)
        dgate = da * u_ref[...].astype(f32) * dgel
        dx, dws, db = _conv_bwd(gp, cw_ref, dgate, row)
        dg_ref[...] = dx.astype(bf16)
        first = pl.program_id(1) == 0

        def acc(ref, val):
            @pl.when(first)
            def _():
                ref[...] = val

            @pl.when(jnp.logical_not(first))
            def _():
                ref[...] += val

        for k in range(FFN_CONV):
            acc(dcw_ref.at[k:k + 1, :], dws[k])
        acc(dcb_ref, db)

    T = gpre.shape[0]
    blk = lambda: pl.BlockSpec((SEQ, FFN_COLS), lambda j, b: (b, j))
    cws = lambda: pl.BlockSpec((FFN_CONV, FFN_COLS), lambda j, b: (0, j))
    cbs = lambda: pl.BlockSpec((1, FFN_COLS), lambda j, b: (0, j))
    return pl.pallas_call(
        body, name="ffn_act_bwd",
        out_shape=(SDS((T, FFN_W), bf16), SDS((T, FFN_W), bf16), SDS((FFN_CONV, FFN_W), f32), SDS((1, FFN_W), f32)),
        grid=(FFN_W // FFN_COLS, T // SEQ),
        in_specs=[blk(), blk(), cws(), cbs(), blk()], out_specs=(blk(), blk(), cws(), cbs()),
        compiler_params=_params(("parallel", "arbitrary")))(gpre, up, cw, cb, dact)


def _coords():
    return lax.axis_index("x"), lax.axis_index("y"), lax.axis_index("c")


def _dev_index(dev):
    return 4 * dev[0] + 2 * dev[1] + dev[2]


def _dma_sems(n):
    return [pltpu.SemaphoreType.DMA((n,)), pltpu.SemaphoreType.DMA((n,))]


def _gather_two_level(arrays):
    n = len(arrays)

    def plan(ins, outs, sems):
        send_sems, recv_sems, local_sems = sems
        x, y, c = _coords()
        me, sibling = (x, y, c), (x, y, 1 - c)
        chips = [(1 - x, y), (x, 1 - y), (1 - x, 1 - y)]

        def copy(a, k, block, to, own=False):
            dst = outs[a].at[_dev_index(block)]
            return pltpu.make_async_remote_copy(
                src_ref=ins[a] if own else dst, dst_ref=dst, send_sem=send_sems.at[7 * a + k],
                recv_sem=recv_sems.at[7 * a + k], device_id=to, device_id_type=MESH)

        mine = [pltpu.make_async_copy(ins[a], outs[a].at[_dev_index(me)], local_sems.at[a]) for a in range(n)]
        first = [copy(a, 0, me, sibling, own=True) for a in range(n)]
        first += [copy(a, 1 + j, me, (*chip, c), own=True) for a in range(n) for j, chip in enumerate(chips)]
        passed = [[copy(a, 4 + j, (*chip, c), sibling) for a in range(n)] for j, chip in enumerate(chips)]
        arrive_ici = [[copy(a, 1 + j, (*chip, c), me) for a in range(n)] for j, chip in enumerate(chips)]
        arrive_d2d = [copy(a, 0, sibling, me) for a in range(n)]
        arrive_d2d += [copy(a, 4 + j, (*chip, 1 - c), me) for a in range(n) for j, chip in enumerate(chips)]
        return mine, first, passed, arrive_ici, arrive_d2d

    def start(ins, outs, sems):
        mine, first, _, _, _ = plan(ins, outs, sems)
        for cp in mine + first:
            cp.start()

    def finish(ins, outs, sems):
        mine, first, passed, arrive_ici, arrive_d2d = plan(ins, outs, sems)
        for j in range(3):
            for cp in arrive_ici[j]:
                cp.wait_recv()
            for cp in passed[j]:
                cp.start()
        for cp in arrive_d2d:
            cp.wait_recv()
        for cp in first + [cp for group in passed for cp in group]:
            cp.wait_send()
        for cp in mine:
            cp.wait()

    return _Comm(arrays, [SDS((N_DEV,) + a.shape, a.dtype) for a in arrays],
                 _dma_sems(7 * n) + [pltpu.SemaphoreType.DMA((n,))], start, finish)


def _gather_direct(arrays):
    n = len(arrays)

    def plan(ins, outs, sems):
        send_sems, recv_sems, local_sems = sems
        x, y, c = _coords()
        me = (x, y, c)
        mine = [pltpu.make_async_copy(ins[a], outs[a].at[_dev_index(me)], local_sems.at[a]) for a in range(n)]
        sends, arrivals = [], []
        for a in range(n):
            for k in range(1, N_DEV):
                peer = (1 - x if k & 4 else x, 1 - y if k & 2 else y, 1 - c if k & 1 else c)
                s = 7 * a + k - 1
                for slot, out in ((me, sends), (peer, arrivals)):
                    out.append(pltpu.make_async_remote_copy(
                        src_ref=ins[a], dst_ref=outs[a].at[_dev_index(slot)], send_sem=send_sems.at[s],
                        recv_sem=recv_sems.at[s], device_id=peer, device_id_type=MESH))
        return mine, sends, arrivals

    def start(ins, outs, sems):
        mine, sends, _ = plan(ins, outs, sems)
        for cp in mine + sends:
            cp.start()

    def finish(ins, outs, sems):
        mine, sends, arrivals = plan(ins, outs, sems)
        for cp in arrivals:
            cp.wait_recv()
        for cp in sends:
            cp.wait_send()
        for cp in mine:
            cp.wait()

    return _Comm(arrays, [SDS((N_DEV,) + a.shape, a.dtype) for a in arrays],
                 _dma_sems(7 * n) + [pltpu.SemaphoreType.DMA((n,))], start, finish)


def _scatter_direct(arrays):
    n = len(arrays)

    def plan(ins, outs, sems):
        send_sems, recv_sems = sems
        x, y, c = _coords()
        cps = []
        for a in range(n):
            for k in range(1, N_DEV):
                peer = (1 - x if k & 4 else x, 1 - y if k & 2 else y, 1 - c if k & 1 else c)
                s = 7 * a + k - 1
                cps.append(pltpu.make_async_remote_copy(
                    src_ref=ins[a].at[_dev_index(peer)], dst_ref=outs[a].at[k - 1], send_sem=send_sems.at[s],
                    recv_sem=recv_sems.at[s], device_id=peer, device_id_type=MESH))
        return cps

    def start(ins, outs, sems):
        for cp in plan(ins, outs, sems):
            cp.start()

    def finish(ins, outs, sems):
        for cp in plan(ins, outs, sems):
            cp.wait()

    return _Comm(arrays, [SDS((N_DEV - 1,) + a.shape[1:], a.dtype) for a in arrays], _dma_sems(7 * n),
                 start, finish)


def _run(comm, name):
    def body(*refs):
        k_in, k_out = len(comm.inputs), len(comm.out_shapes)
        ins, outs, sems = refs[:k_in], refs[k_in:k_in + k_out], refs[k_in + k_out:]
        comm.start(ins, outs, sems)
        comm.finish(ins, outs, sems)

    return pl.pallas_call(body, name=name, out_shape=comm.out_shapes, in_specs=[ANY] * len(comm.inputs),
                          out_specs=(ANY,) * len(comm.out_shapes), scratch_shapes=comm.sem_shapes)(*comm.inputs)


TILE_ELEMS = 192 * 1024


def _row_tile(R, C):
    if R * C <= TILE_ELEMS:
        return R
    return max(t for t in range(SUBLANES, R, SUBLANES) if R % t == 0 and t * C <= TILE_ELEMS)


def _adamw_math(w, g, m, v):
    m = ADAM_B1 * m + (1.0 - ADAM_B1) * g
    v = ADAM_B2 * v + (1.0 - ADAM_B2) * (g * g)
    m_hat = m / (1.0 - ADAM_B1 ** ADAM_STEP)
    v_hat = v / (1.0 - ADAM_B2 ** ADAM_STEP)
    delta = -ADAM_LR * (m_hat / (jnp.sqrt(v_hat) + ADAM_EPS) + ADAM_WD * w)
    return delta, m, v


def _adamw_sharded(own, recv, d_idx, w, m, v, name):
    R, C = w.shape
    t = _row_tile(R, C)

    def body(k_ref, p_ref, r_ref, w_ref, m_ref, v_ref, g_ref, d_ref, nm_ref, nv_ref):
        g = p_ref[...].astype(f32)
        for j in range(N_DEV - 1):
            g = g + r_ref[j].astype(f32)
        d, nm, nv = _adamw_math(w_ref[...], g, m_ref[...], v_ref[...])
        g_ref[...], d_ref[...], nm_ref[...], nv_ref[...] = g, d, nm, nv

    tile = lambda: pl.BlockSpec((t, C), lambda i, k: (i, 0))
    return pl.pallas_call(
        body, name="adamw_" + name, out_shape=(SDS((R, C), f32),) * 4,
        grid_spec=pltpu.PrefetchScalarGridSpec(
            num_scalar_prefetch=1, grid=(R // t,),
            in_specs=[pl.BlockSpec((None, t, C), lambda i, k: (k[0], i, 0)),
                      pl.BlockSpec((N_DEV - 1, t, C), lambda i, k: (0, i, 0)), tile(), tile(), tile()],
            out_specs=(tile(), tile(), tile(), tile())),
        compiler_params=_params(("parallel",)))(d_idx, own, recv, w, m, v)


def _adamw_replicated(parts, ws, ms, vs):
    n = len(ws)

    def body(*refs):
        p, w, m, v = (refs[i * n:(i + 1) * n] for i in range(4))
        outs = refs[4 * n:]
        for a in range(n):
            g = p[a][0].astype(f32)
            for j in range(1, N_DEV):
                g = g + p[a][j].astype(f32)
            d, nm, nv = _adamw_math(w[a][...], g, m[a][...], v[a][...])
            for i, val in enumerate((g, d, nm, nv)):
                outs[i * n + a][...] = val

    shapes = tuple(SDS(w.shape, f32) for w in ws)
    res = pl.pallas_call(body, name="adamw_replicated", out_shape=shapes * 4,
                         compiler_params=_params())(*parts, *ws, *ms, *vs)
    return [res[i * n:(i + 1) * n] for i in range(4)]


def _cols_to_full(g):
    n, r, c = g.shape
    return g.transpose(1, 0, 2).reshape(r, n * c)


def _full_to_cols(a):
    r, c = a.shape
    return a.reshape(r, N_DEV, c // N_DEV).transpose(1, 0, 2)


def _rows_blocked(a):
    r, c = a.shape
    return a.reshape(N_DEV, r // N_DEV, c)


def _w_in_to_internal(w):
    K = w.shape[0]
    q = w[:, 1280:2816].reshape(K, NG, KVH, 1, HD).transpose(0, 2, 1, 3, 4).reshape(K, KVH, NG, HD)
    k = w[:, 2816:3328].reshape(K, KVH, 1, HD)
    v = w[:, 3328:3840].reshape(K, KVH, 1, HD)
    att = jnp.concatenate([q, k, v], axis=2).reshape(K, KVH * ATT_COLS)
    return jnp.concatenate([w[:, :1280], att, w[:, 3840:]], axis=1)


def _w_in_from_internal(w):
    K = w.shape[0]
    att = w[:, C_ATT:C_GATE].reshape(K, KVH, 5, HD)
    q = att[:, :, 0:3].transpose(0, 2, 1, 3).reshape(K, NG * KVH * HD)
    k = att[:, :, 3].reshape(K, KVH * HD)
    v = att[:, :, 4].reshape(K, KVH * HD)
    return jnp.concatenate([w[:, :C_ATT], q, k, v, w[:, C_GATE:]], axis=1)


_IN_NAMES = ('x', 'rel_bias', 'norm_mix_pre', 'norm_mix_post', 'w_in', 'conv_rnn_w', 'conv_rnn_b', 'w_rg_a', 'b_rg_a',
             'w_rg_x', 'b_rg_x', 'lru_lambda', 'w_branch_rnn', 'w_branch_att', 'w_out', 'norm_ffn_pre',
             'norm_ffn_post', 'w_ffn_gate', 'w_ffn_up', 'conv_ffn_w', 'conv_ffn_b', 'w_ffn_down')
_WEIGHTS = _IN_NAMES[1:]
_SHARDED = {"w_in": "col", "conv_rnn_w": "col", "w_branch_rnn": "row", "w_branch_att": "col", "w_out": "row",
            "w_ffn_gate": "col", "w_ffn_up": "col", "conv_ffn_w": "col", "w_ffn_down": "row"}
_REPLICATED = tuple(n for n in _WEIGHTS if n not in _SHARDED)


def _flat2(a):
    return a.reshape(-1, a.shape[-1])


def _train_step(inp):
    x_idx, y_idx, c_idx = _coords()
    W = {n: inp[n] for n in _WEIGHTS}
    x = inp["x"].reshape(-1, D)
    target = inp["loss_target"].reshape(-1, D)
    shard = {n: inp[n][0] for n in _SHARDED}

    hn, (g_in, g_cr, g_cf) = _norm_in(x, W["norm_mix_pre"], comm=_gather_two_level(
        [shard["w_in"].astype(bf16), shard["conv_rnn_w"], shard["conv_ffn_w"]]))
    w_in = _w_in_to_internal(_cols_to_full(g_in))
    cw_rnn, cw_ffn = _cols_to_full(g_cr), _cols_to_full(g_cf)
    behind_proj = ("w_ffn_up",)
    behind_rnn = ("w_branch_rnn", "w_branch_att", "w_out", "w_ffn_down")
    behind_att = ("w_ffn_gate",)

    wa, wx = W["w_rg_a"][0], W["w_rg_x"][0]
    buckets = jnp.asarray(_bucket_maps())

    proj, got = _mm(hn, (w_in, 0, C_GATE), "nn", f32, "mm_proj", 1024, C_GATE // 2, 1024, cols_outer=True,
                    comm=_gather_direct([shard[n].astype(bf16) for n in behind_proj]))
    gathered = dict(zip(behind_proj, got))
    gates = _mm(hn, w_in[:, C_GATE:], "nn", bf16, "mm_gates", 1024, 1024, 1024, cols_outer=True)
    rnn_saved, got = _rnn_fwd(proj, cw_rnn, W["conv_rnn_b"], wa, W["b_rg_a"], wx, W["b_rg_x"], W["lru_lambda"],
                              comm=_gather_direct([shard[n].astype(bf16) for n in behind_rnn]))
    rnn_saved, h_rnn = rnn_saved[:6], rnn_saved[6]
    gathered.update(zip(behind_rnn, got))
    bias = _bias_tables(W["rel_bias"], buckets)
    (o_att, lse, *att_saved), got = _att_fwd(
        proj, bias, comm=_gather_direct([shard[n].astype(bf16) for n in behind_att]))
    gathered.update(zip(behind_att, got))
    w_brnn = gathered["w_branch_rnn"].reshape(RNN_W, D)
    w_batt = _cols_to_full(gathered["w_branch_att"])
    w_out = gathered["w_out"].reshape(D, D)
    w_gate, w_up = _cols_to_full(gathered["w_ffn_gate"]), _cols_to_full(gathered["w_ffn_up"])
    w_down = gathered["w_ffn_down"].reshape(FFN_W, D)
    pr = _mm(h_rnn, w_brnn, "nn", bf16, "mm_pr", 1024, 1024, 1280)
    pa = _mm(o_att, w_batt, "nn", bf16, "mm_pa", 1024, 1024, 512)
    merged = _merge_fwd(gates, pr, pa)
    (mix, h1, hn2), _ = _mm_rows([(merged, w_out, 1024)], "nn", "mm_mix", 1024, _mid_fwd_rows, [x],
                                 [W["norm_mix_post"], W["norm_ffn_pre"]], [f32, f32, bf16], [])
    gpre = _mm(hn2, w_gate, "nn", bf16, "mm_gate", 1024, 1024, 1024, cols_outer=True)
    up = _mm(hn2, w_up, "nn", bf16, "mm_up", 1024, 1024, 1024, cols_outer=True)
    act = _ffn_act_fwd(gpre, up, cw_ffn, W["conv_ffn_b"])
    (dy, dff), (loss_part, dg_fpost) = _mm_rows([(act, w_down, 1024)], "nn", "mm_down", 1024, _final_rows,
                                                [h1, target], [W["norm_ffn_post"]], [f32, bf16], [1, D])

    grads = {}
    dact = _mm(dff, w_down, "nt", bf16, "mm_dact", 1024, 1024, 1024, cols_outer=True)
    grads["w_ffn_down"] = _rows_blocked(_mm(act, dff, "tn", bf16, "mm_dw_down", 1024, 1024, 2048))
    dgpre, dup, dcw_ffn, dcb_ffn = _ffn_act_bwd(gpre, up, cw_ffn, W["conv_ffn_b"], dact)
    grads["conv_ffn_w"] = _full_to_cols(dcw_ffn.astype(bf16))
    grads["w_ffn_gate"] = _full_to_cols(_mm(hn2, dgpre, "tn", bf16, "mm_dw_gate", 1024, 1024, 2048))
    grads["w_ffn_up"] = _full_to_cols(_mm(hn2, dup, "tn", bf16, "mm_dw_up", 1024, 1024, 2048))
    dhn2 = _mm_nt_sum([(dgpre, w_gate, 1024), (dup, w_up, 1024)], "mm_dhn2", 1024, 1024)
    dh1, dmix, dg_fpre, dg_post = _mid_bwd(dy, dhn2, h1, W["norm_ffn_pre"], mix, W["norm_mix_post"])
    dmerged = _mm(dmix, w_out, "nt", bf16, "mm_dmerged", 1024, 1024, 1024)
    grads["w_out"] = _rows_blocked(_mm(merged, dmix, "tn", bf16, "mm_dw_out", 1024, 1024, 2048))
    dprpa, dgates = _merge_bwd(gates, pr, pa, dmerged)
    dpr, dpa = (dprpa, 0, D), (dprpa, D, D)
    dh_rnn = _mm(dpr, w_brnn, "nt", bf16, "mm_dh_rnn", 1024, 1280, 1024)
    grads["w_branch_rnn"] = _rows_blocked(_mm(h_rnn, dpr, "tn", bf16, "mm_dw_brnn", 1280, 1024, 1024))
    do_att = _mm(dpa, w_batt, "nt", f32, "mm_do_att", 1024, 512, 1024)
    grads["w_branch_att"] = _full_to_cols(_mm(o_att, dpa, "tn", bf16, "mm_dw_batt", 512, 1024, 2048))

    received = {}
    behind_att_bwd = ("w_ffn_down", "w_ffn_gate", "conv_ffn_w", "w_out")
    behind_rnn_bwd = ("w_ffn_up", "w_branch_rnn", "w_branch_att")
    (dqkv, dbias), got = _att_bwd(proj, att_saved, bias, o_att, lse, do_att,
                                  comm=_scatter_direct([grads[n] for n in behind_att_bwd]))
    received.update(zip(behind_att_bwd, got))
    drel = _bias_grad(dbias, buckets)
    (dxr, dcw_rnn, dcb_rnn, dwa, dba, dwx, dbx, dlam), got = _rnn_bwd(
        proj, rnn_saved, dh_rnn, cw_rnn, wa, wx, W["lru_lambda"],
        comm=_scatter_direct([grads[n] for n in behind_rnn_bwd]))
    received.update(zip(behind_rnn_bwd, got))
    gsmall = {"rel_bias": drel, "norm_mix_post": dg_post, "conv_rnn_b": dcb_rnn, "w_rg_a": dwa.astype(bf16),
              "b_rg_a": dba, "w_rg_x": dwx.astype(bf16), "b_rg_x": dbx, "lru_lambda": dlam,
              "norm_ffn_pre": dg_fpre, "norm_ffn_post": dg_fpost, "conv_ffn_b": dcb_ffn}
    dw_in_a, parts = _mm(hn, dqkv, "tn", bf16, "mm_dw_in_a", 1024, 1280, 1024,
                         comm=_gather_direct([_flat2(gsmall[n]) for n in gsmall]))
    parts = dict(zip(gsmall, parts))
    dw_in = jnp.concatenate([_mm(hn, dxr, "tn", bf16, "mm_dw_in_r", 1024, 1280, 1024), dw_in_a,
                             _mm(hn, dgates, "tn", bf16, "mm_dw_in_g", 1024, 1024, 2048)], axis=1)
    grads["w_in"] = _full_to_cols(_w_in_from_internal(dw_in))
    grads["conv_rnn_w"] = _full_to_cols(dcw_rnn.astype(bf16))
    behind_dhn = ("w_in", "conv_rnn_w")
    dhn, got = _mm_nt_sum([(dxr, w_in[:, :C_ATT], 1280), (dqkv, w_in[:, C_ATT:C_GATE], 1280),
                           (dgates, w_in[:, C_GATE:], 1024)], "mm_dhn", 1024, 1024,
                          comm=_scatter_direct([grads[n] for n in behind_dhn]))
    received.update(zip(behind_dhn, got))
    dx, dg_pre = _in_bwd(dh1, dhn, x, W["norm_mix_pre"])
    parts["norm_mix_pre"], = _run(_gather_two_level([dg_pre]), "ag_norm_mix_pre")
    parts = [parts[n] for n in _REPLICATED]

    out = {}
    d_arr = jnp.reshape(4 * x_idx + 2 * y_idx + c_idx, (1,)).astype(jnp.int32)
    for n in _SHARDED:
        res = _adamw_sharded(grads[n], received[n], d_arr, shard[n], inp["m_" + n][0], inp["v_" + n][0], n)
        out[n] = [r[None] for r in res]
    small = _adamw_replicated(parts, *[[_flat2(inp[p + n]) for n in _REPLICATED] for p in ("", "m_", "v_")])
    for a, n in enumerate(_REPLICATED):
        out[n] = [small[i][a].reshape(inp[n].shape) for i in range(4)]

    loss = lax.psum(loss_part[0, 0], ("x", "y", "c"))
    outs = [loss, dx.reshape(inp["x"].shape)]
    for i in range(4):
        outs.extend(out[n][i] for n in _WEIGHTS)
    return tuple(outs)


def kernel(x, rel_bias, norm_mix_pre, norm_mix_post, w_in, conv_rnn_w, conv_rnn_b, w_rg_a, b_rg_a, w_rg_x, b_rg_x, lru_lambda, w_branch_rnn, w_branch_att, w_out, norm_ffn_pre, norm_ffn_post, w_ffn_gate, w_ffn_up, conv_ffn_w, conv_ffn_b, w_ffn_down, loss_target, m_rel_bias, m_norm_mix_pre, m_norm_mix_post, m_w_in, m_conv_rnn_w, m_conv_rnn_b, m_w_rg_a, m_b_rg_a, m_w_rg_x, m_b_rg_x, m_lru_lambda, m_w_branch_rnn, m_w_branch_att, m_w_out, m_norm_ffn_pre, m_norm_ffn_post, m_w_ffn_gate, m_w_ffn_up, m_conv_ffn_w, m_conv_ffn_b, m_w_ffn_down, v_rel_bias, v_norm_mix_pre, v_norm_mix_post, v_w_in, v_conv_rnn_w, v_conv_rnn_b, v_w_rg_a, v_b_rg_a, v_w_rg_x, v_b_rg_x, v_lru_lambda, v_w_branch_rnn, v_w_branch_att, v_w_out, v_norm_ffn_pre, v_norm_ffn_post, v_w_ffn_gate, v_w_ffn_up, v_conv_ffn_w, v_conv_ffn_b, v_w_ffn_down):
    vals = locals()
    names = list(_IN_NAMES) + ["loss_target"] + ["m_" + n for n in _WEIGHTS] + ["v_" + n for n in _WEIGHTS]
    return _train_step({n: vals[n] for n in names})
```
